```python
import jax, jax.numpy as jnp
from jax import lax
import numpy as np

D_MODEL = 1024
BATCH = 8
SEQ = 4096
DEPTH = 1

PLE_DIM = 256
MIX_WIDTH = D_MODEL
HG_WIDTH = D_MODEL // 2
HG_HEADS = 4
HG_DK = HG_WIDTH // HG_HEADS
HG_DV = HG_WIDTH // HG_HEADS
HG_CHUNK = 64
SB_WIDTH = MIX_WIDTH - HG_WIDTH
SB_HEADS = 8
SB_DH = SB_WIDTH // SB_HEADS
SB_BLOCK = 128
D_FF = -(-8 * D_MODEL // (3 * 256)) * 256
IN_COLS = 4 * HG_WIDTH + 3 * SB_WIDTH
EPS = 1e-6

kernel_name = "hybrid_hgrn2_stickbreaking_block"


def rmsnorm(x, w):
    xf = x.astype(jnp.float32)
    xf = xf * lax.rsqrt(jnp.mean(xf * xf, axis=-1, keepdims=True) + EPS)
    return xf.astype(x.dtype) * w


def _to_chunks(a, heads, d):
    B, T, _ = a.shape
    return a.reshape(B, T // HG_CHUNK, HG_CHUNK, heads, d).transpose(1, 0, 3, 2, 4)


def hgrn2_mix(q, f_logit, i_in, lb):
    B, T, _ = q.shape
    q = jax.nn.silu(q.astype(jnp.float32))
    z = f_logit.astype(jnp.float32)
    log_f = jnp.logaddexp(jnp.log(lb), jnp.log1p(-lb) + jax.nn.log_sigmoid(z))
    k = -jnp.expm1(log_f)
    v = i_in.astype(jnp.float32)
    qc = _to_chunks(q, HG_HEADS, HG_DK)
    kc = _to_chunks(k, HG_HEADS, HG_DK)
    gc = _to_chunks(log_f, HG_HEADS, HG_DK)
    vc = _to_chunks(v, HG_HEADS, HG_DV)
    incl = jnp.tril(jnp.ones((HG_CHUNK, HG_CHUNK), dtype=bool))

    def step(S, inp):
        qb, kb, gb, vb = inp
        b = jnp.cumsum(gb, axis=2)
        o_inter = jnp.einsum('bhtk,bhkv->bhtv', qb * jnp.exp(b), S)
        diff = b[:, :, :, None, :] - b[:, :, None, :, :]
        decay = jnp.where(incl[:, :, None], jnp.exp(jnp.minimum(diff, 0.0)), 0.0)
        scores = jnp.einsum('bhtsk,bhsk->bhts', qb[:, :, :, None, :] * decay, kb)
        o_intra = jnp.einsum('bhts,bhsv->bhtv', scores, vb)
        b_last = b[:, :, -1:, :]
        S_new = jnp.exp(b_last[:, :, 0, :])[..., None] * S + jnp.einsum(
            'bhsk,bhsv->bhkv', kb * jnp.exp(b_last - b), vb)
        return S_new, o_inter + o_intra

    S0 = jnp.zeros((B, HG_HEADS, HG_DK, HG_DV), jnp.float32)
    _, oc = lax.scan(step, S0, (qc, kc, gc, vc))
    return oc.transpose(1, 0, 3, 2, 4).reshape(B, T, HG_HEADS * HG_DV)


def stick_breaking_mix(q, k, v):
    B, T, H, d = q.shape
    scale = d ** -0.5
    outs = []
    for blk in range(T // SB_BLOCK):
        t0 = blk * SB_BLOCK
        t1 = t0 + SB_BLOCK
        qb = q[:, t0:t1]
        kp = k[:, :t1]
        vp = v[:, :t1]
        z = jnp.einsum('bthd,bshd->bhts', qb, kp).astype(jnp.float32) * scale
        causal = jnp.arange(t1)[None, :] < jnp.arange(t0, t1)[:, None]
        log_1mb = jnp.where(causal, -jax.nn.softplus(z), 0.0)
        rem = lax.cumsum(log_1mb, axis=3, reverse=True) - log_1mb
        a = jnp.where(causal, jnp.exp(jax.nn.log_sigmoid(z) + rem), 0.0)
        outs.append(jnp.einsum('bhts,bshd->bthd', a.astype(v.dtype), vp))
    return jnp.concatenate(outs, axis=1)


def _fwd_setup_inputs(seed: int = 0) -> dict:
    key = jax.random.key(seed)
    ks = jax.random.split(key, 16)
    f32 = jnp.float32
    nrm = lambda k, shape, s: jax.random.normal(k, shape, f32) * s
    gain = lambda k, shape: 1.0 + 0.05 * jax.random.normal(k, shape, f32)
    return {
        "x": jax.random.normal(ks[0], (BATCH, SEQ, D_MODEL), f32),
        "p": jax.random.normal(ks[1], (DEPTH, BATCH, SEQ, PLE_DIM), f32),
        "attn_pre_norm": gain(ks[2], (DEPTH, D_MODEL)),
        "w_in": nrm(ks[3], (DEPTH, D_MODEL, IN_COLS), D_MODEL ** -0.5),
        "hg_lower_gamma": nrm(ks[4], (DEPTH + 1, HG_WIDTH), 0.5),
        "hg_out_norm": gain(ks[5], (DEPTH, HG_WIDTH)),
        "sb_out_norm": gain(ks[6], (DEPTH, SB_WIDTH)),
        "w_out": nrm(ks[7], (DEPTH, MIX_WIDTH, D_MODEL), MIX_WIDTH ** -0.5),
        "attn_post_norm": gain(ks[8], (DEPTH, D_MODEL)),
        "ffn_pre_norm": gain(ks[9], (DEPTH, D_MODEL)),
        "w_gate_up": nrm(ks[10], (DEPTH, D_MODEL, 2 * D_FF), D_MODEL ** -0.5),
        "w_down": nrm(ks[11], (DEPTH, D_FF, D_MODEL), D_FF ** -0.5),
        "ffn_post_norm": gain(ks[12], (DEPTH, D_MODEL)),
        "ple_proj": nrm(ks[13], (DEPTH, PLE_DIM, D_MODEL), PLE_DIM ** -0.5),
        "ple_gate": nrm(ks[14], (DEPTH, D_MODEL, D_MODEL), D_MODEL ** -0.5),
    }


def _fwd_reference(x, p, attn_pre_norm, w_in, hg_lower_gamma, hg_out_norm, sb_out_norm, w_out,
              attn_post_norm, ffn_pre_norm, w_gate_up, w_down, ffn_post_norm, ple_proj, ple_gate):
    B, T, _ = x.shape
    lb_all = jnp.cumsum(jax.nn.softmax(hg_lower_gamma.astype(jnp.float32), axis=0), axis=0)
    splits = [HG_WIDTH, 2 * HG_WIDTH, 3 * HG_WIDTH, 4 * HG_WIDTH,
              4 * HG_WIDTH + SB_WIDTH, 4 * HG_WIDTH + 2 * SB_WIDTH]
    h = x
    for i in range(DEPTH):
        u = rmsnorm(h, attn_pre_norm[i])
        proj = u @ w_in[i]
        hq, hf, hi, hg, sq, sk, sv = jnp.split(proj, splits, axis=-1)
        o_hg = hgrn2_mix(hq, hf, hi, lb_all[i])
        o_hg = rmsnorm(o_hg.astype(x.dtype), hg_out_norm[i]) * jax.nn.silu(hg)
        o_sb = stick_breaking_mix(sq.reshape(B, T, SB_HEADS, SB_DH),
                                  sk.reshape(B, T, SB_HEADS, SB_DH),
                                  sv.reshape(B, T, SB_HEADS, SB_DH)).reshape(B, T, SB_WIDTH)
        o_sb = rmsnorm(o_sb, sb_out_norm[i])
        mix = jnp.concatenate([o_hg, o_sb], axis=-1) @ w_out[i]
        h = h + rmsnorm(mix, attn_post_norm[i])
        u = rmsnorm(h, ffn_pre_norm[i])
        gate, up = jnp.split(u @ w_gate_up[i], [D_FF], axis=-1)
        y = (jax.nn.silu(gate) * up) @ w_down[i]
        h = h + rmsnorm(y, ffn_post_norm[i])
        h = h + (p[i] @ ple_proj[i]) * jax.nn.sigmoid(h @ ple_gate[i])
    return h


import jax as _jax
import jax.numpy as _jnp

TWIN_FORMAT = 'train_step'
FWD_PARAMS = ['x', 'p', 'attn_pre_norm', 'w_in', 'hg_lower_gamma', 'hg_out_norm', 'sb_out_norm', 'w_out', 'attn_post_norm', 'ffn_pre_norm', 'w_gate_up', 'w_down', 'ffn_post_norm', 'ple_proj', 'ple_gate']
TWIN_WEIGHTS = ['attn_pre_norm', 'w_in', 'hg_lower_gamma', 'hg_out_norm', 'sb_out_norm', 'w_out', 'attn_post_norm', 'ffn_pre_norm', 'w_gate_up', 'w_down', 'ffn_post_norm', 'ple_proj', 'ple_gate']
TWIN_DIFF_INPUT = 'x'
TWIN_INPUTS = ['x', 'p', 'attn_pre_norm', 'w_in', 'hg_lower_gamma', 'hg_out_norm', 'sb_out_norm', 'w_out', 'attn_post_norm', 'ffn_pre_norm', 'w_gate_up', 'w_down', 'ffn_post_norm', 'ple_proj', 'ple_gate', 'loss_target', 'm_attn_pre_norm', 'm_w_in', 'm_hg_lower_gamma', 'm_hg_out_norm', 'm_sb_out_norm', 'm_w_out', 'm_attn_post_norm', 'm_ffn_pre_norm', 'm_w_gate_up', 'm_w_down', 'm_ffn_post_norm', 'm_ple_proj', 'm_ple_gate', 'v_attn_pre_norm', 'v_w_in', 'v_hg_lower_gamma', 'v_hg_out_norm', 'v_sb_out_norm', 'v_w_out', 'v_attn_post_norm', 'v_ffn_pre_norm', 'v_w_gate_up', 'v_w_down', 'v_ffn_post_norm', 'v_ple_proj', 'v_ple_gate']
TWIN_OUTPUTS = ['loss', 'grad_x', 'grad_attn_pre_norm', 'grad_w_in', 'grad_hg_lower_gamma', 'grad_hg_out_norm', 'grad_sb_out_norm', 'grad_w_out', 'grad_attn_post_norm', 'grad_ffn_pre_norm', 'grad_w_gate_up', 'grad_w_down', 'grad_ffn_post_norm', 'grad_ple_proj', 'grad_ple_gate', 'delta_attn_pre_norm', 'delta_w_in', 'delta_hg_lower_gamma', 'delta_hg_out_norm', 'delta_sb_out_norm', 'delta_w_out', 'delta_attn_post_norm', 'delta_ffn_pre_norm', 'delta_w_gate_up', 'delta_w_down', 'delta_ffn_post_norm', 'delta_ple_proj', 'delta_ple_gate', 'new_m_attn_pre_norm', 'new_m_w_in', 'new_m_hg_lower_gamma', 'new_m_hg_out_norm', 'new_m_sb_out_norm', 'new_m_w_out', 'new_m_attn_post_norm', 'new_m_ffn_pre_norm', 'new_m_w_gate_up', 'new_m_w_down', 'new_m_ffn_post_norm', 'new_m_ple_proj', 'new_m_ple_gate', 'new_v_attn_pre_norm', 'new_v_w_in', 'new_v_hg_lower_gamma', 'new_v_hg_out_norm', 'new_v_sb_out_norm', 'new_v_w_out', 'new_v_attn_post_norm', 'new_v_ffn_pre_norm', 'new_v_w_gate_up', 'new_v_w_down', 'new_v_ffn_post_norm', 'new_v_ple_proj', 'new_v_ple_gate']
TWIN_LEAF_KINDS = {'loss': 'loss', 'grad_x': 'grad_x', 'grad_attn_pre_norm': 'grad_w', 'grad_w_in': 'grad_w', 'grad_hg_lower_gamma': 'grad_w', 'grad_hg_out_norm': 'grad_w', 'grad_sb_out_norm': 'grad_w', 'grad_w_out': 'grad_w', 'grad_attn_post_norm': 'grad_w', 'grad_ffn_pre_norm': 'grad_w', 'grad_w_gate_up': 'grad_w', 'grad_w_down': 'grad_w', 'grad_ffn_post_norm': 'grad_w', 'grad_ple_proj': 'grad_w', 'grad_ple_gate': 'grad_w', 'delta_attn_pre_norm': 'delta_w', 'delta_w_in': 'delta_w', 'delta_hg_lower_gamma': 'delta_w', 'delta_hg_out_norm': 'delta_w', 'delta_sb_out_norm': 'delta_w', 'delta_w_out': 'delta_w', 'delta_attn_post_norm': 'delta_w', 'delta_ffn_pre_norm': 'delta_w', 'delta_w_gate_up': 'delta_w', 'delta_w_down': 'delta_w', 'delta_ffn_post_norm': 'delta_w', 'delta_ple_proj': 'delta_w', 'delta_ple_gate': 'delta_w', 'new_m_attn_pre_norm': 'new_m', 'new_m_w_in': 'new_m', 'new_m_hg_lower_gamma': 'new_m', 'new_m_hg_out_norm': 'new_m', 'new_m_sb_out_norm': 'new_m', 'new_m_w_out': 'new_m', 'new_m_attn_post_norm': 'new_m', 'new_m_ffn_pre_norm': 'new_m', 'new_m_w_gate_up': 'new_m', 'new_m_w_down': 'new_m', 'new_m_ffn_post_norm': 'new_m', 'new_m_ple_proj': 'new_m', 'new_m_ple_gate': 'new_m', 'new_v_attn_pre_norm': 'new_v', 'new_v_w_in': 'new_v', 'new_v_hg_lower_gamma': 'new_v', 'new_v_hg_out_norm': 'new_v', 'new_v_sb_out_norm': 'new_v', 'new_v_w_out': 'new_v', 'new_v_attn_post_norm': 'new_v', 'new_v_ffn_pre_norm': 'new_v', 'new_v_w_gate_up': 'new_v', 'new_v_w_down': 'new_v', 'new_v_ffn_post_norm': 'new_v', 'new_v_ple_proj': 'new_v', 'new_v_ple_gate': 'new_v'}


def _forward(args):
    return _fwd_reference(*[args[k] for k in FWD_PARAMS])


def _output_shape():
    out = _jax.eval_shape(lambda: _forward(_fwd_setup_inputs(0)))
    return out.shape, out.dtype

N_MICROBATCH = 1
ADAM_LR = 0.001
ADAM_B1 = 0.9
ADAM_B2 = 0.999
ADAM_EPS = 1e-08
ADAM_WD = 0.01
ADAM_STEP = 10
PER_EXAMPLE_BATCH_AXIS = {'x': 0, 'p': 1, 'loss_target': 0}
SHARED_INPUTS = []
_WEIGHT_DTYPES = {'attn_pre_norm': _jnp.float32, 'w_in': _jnp.float32, 'hg_lower_gamma': _jnp.float32, 'hg_out_norm': _jnp.float32, 'sb_out_norm': _jnp.float32, 'w_out': _jnp.float32, 'attn_post_norm': _jnp.float32, 'ffn_pre_norm': _jnp.float32, 'w_gate_up': _jnp.float32, 'w_down': _jnp.float32, 'ffn_post_norm': _jnp.float32, 'ple_proj': _jnp.float32, 'ple_gate': _jnp.float32}
MOMENT_SCALE = {'attn_pre_norm': 6.586272e-01, 'w_in': 3.356836e-01, 'hg_lower_gamma': 3.738136e-02, 'hg_out_norm': 4.071240e-01, 'sb_out_norm': 6.716993e-01, 'w_out': 5.446829e-01, 'attn_post_norm': 3.239547e+01, 'ffn_pre_norm': 5.873510e-01, 'w_gate_up': 2.504813e-01, 'w_down': 4.794501e-01, 'ffn_post_norm': 3.251106e+01, 'ple_proj': 4.771396e-01, 'ple_gate': 1.353029e-01}


def _to_microbatches(a, axis):
    t = _jnp.moveaxis(a, axis, 0)
    t = t.reshape((N_MICROBATCH, t.shape[0] // N_MICROBATCH) + t.shape[1:])
    return _jnp.moveaxis(t, 1, axis + 1)


def setup_inputs(seed: int = 0) -> dict:
    inp = _fwd_setup_inputs(seed)
    key = _jax.random.fold_in(_jax.random.key(seed), 7919)
    shape, _ = _output_shape()
    out = dict(inp)
    out["loss_target"] = _jax.random.normal(_jax.random.fold_in(key, 0), shape, _jnp.float32)
    for i, name in enumerate(TWIN_WEIGHTS):
        w = inp[name].astype(_jnp.float32)
        if MOMENT_SCALE is None:
            s = _jnp.sqrt(_jnp.mean(_jnp.square(w)) + 1e-30)
        else:
            s = MOMENT_SCALE[name]
        km, kv = _jax.random.split(_jax.random.fold_in(key, i + 1))
        out[name] = w
        out["m_" + name] = s * _jax.random.normal(km, w.shape, _jnp.float32)
        out["v_" + name] = (s * s) * _jax.random.uniform(kv, w.shape, _jnp.float32, 0.5, 1.5)
    if N_MICROBATCH > 1:
        for name, axis in PER_EXAMPLE_BATCH_AXIS.items():
            out[name] = _to_microbatches(out[name], axis)
    return {'x': out['x'], 'p': out['p'], 'attn_pre_norm': out['attn_pre_norm'], 'w_in': out['w_in'], 'hg_lower_gamma': out['hg_lower_gamma'], 'hg_out_norm': out['hg_out_norm'], 'sb_out_norm': out['sb_out_norm'], 'w_out': out['w_out'], 'attn_post_norm': out['attn_post_norm'], 'ffn_pre_norm': out['ffn_pre_norm'], 'w_gate_up': out['w_gate_up'], 'w_down': out['w_down'], 'ffn_post_norm': out['ffn_post_norm'], 'ple_proj': out['ple_proj'], 'ple_gate': out['ple_gate'], 'loss_target': out['loss_target'], 'm_attn_pre_norm': out['m_attn_pre_norm'], 'm_w_in': out['m_w_in'], 'm_hg_lower_gamma': out['m_hg_lower_gamma'], 'm_hg_out_norm': out['m_hg_out_norm'], 'm_sb_out_norm': out['m_sb_out_norm'], 'm_w_out': out['m_w_out'], 'm_attn_post_norm': out['m_attn_post_norm'], 'm_ffn_pre_norm': out['m_ffn_pre_norm'], 'm_w_gate_up': out['m_w_gate_up'], 'm_w_down': out['m_w_down'], 'm_ffn_post_norm': out['m_ffn_post_norm'], 'm_ple_proj': out['m_ple_proj'], 'm_ple_gate': out['m_ple_gate'], 'v_attn_pre_norm': out['v_attn_pre_norm'], 'v_w_in': out['v_w_in'], 'v_hg_lower_gamma': out['v_hg_lower_gamma'], 'v_hg_out_norm': out['v_hg_out_norm'], 'v_sb_out_norm': out['v_sb_out_norm'], 'v_w_out': out['v_w_out'], 'v_attn_post_norm': out['v_attn_post_norm'], 'v_ffn_pre_norm': out['v_ffn_pre_norm'], 'v_w_gate_up': out['v_w_gate_up'], 'v_w_down': out['v_w_down'], 'v_ffn_post_norm': out['v_ffn_post_norm'], 'v_ple_proj': out['v_ple_proj'], 'v_ple_gate': out['v_ple_gate']}


def _loss(weights, diff, rest, loss_target):
    with _jax.named_scope("forward"):
        args = {**rest, TWIN_DIFF_INPUT: diff, **{k: w.astype(_WEIGHT_DTYPES[k]) for k, w in weights.items()}}
        y = _forward(args)
    with _jax.named_scope("loss_head"):
        err = _jnp.square(y.astype(_jnp.float32) - loss_target)
        return 0.5 * _jnp.sum(_jnp.mean(err, axis=-1)) if err.ndim else 0.5 * err


def _adamw(w, g, m, v):
    m = ADAM_B1 * m + (1.0 - ADAM_B1) * g
    v = ADAM_B2 * v + (1.0 - ADAM_B2) * _jnp.square(g)
    m_hat = m / (1.0 - ADAM_B1 ** ADAM_STEP)
    v_hat = v / (1.0 - ADAM_B2 ** ADAM_STEP)
    delta = -ADAM_LR * (m_hat / (_jnp.sqrt(v_hat) + ADAM_EPS) + ADAM_WD * w)
    return delta, m, v


def reference(x, p, attn_pre_norm, w_in, hg_lower_gamma, hg_out_norm, sb_out_norm, w_out, attn_post_norm, ffn_pre_norm, w_gate_up, w_down, ffn_post_norm, ple_proj, ple_gate, loss_target, m_attn_pre_norm, m_w_in, m_hg_lower_gamma, m_hg_out_norm, m_sb_out_norm, m_w_out, m_attn_post_norm, m_ffn_pre_norm, m_w_gate_up, m_w_down, m_ffn_post_norm, m_ple_proj, m_ple_gate, v_attn_pre_norm, v_w_in, v_hg_lower_gamma, v_hg_out_norm, v_sb_out_norm, v_w_out, v_attn_post_norm, v_ffn_pre_norm, v_w_gate_up, v_w_down, v_ffn_post_norm, v_ple_proj, v_ple_gate):
    given = dict(x=x, p=p, attn_pre_norm=attn_pre_norm, w_in=w_in, hg_lower_gamma=hg_lower_gamma, hg_out_norm=hg_out_norm, sb_out_norm=sb_out_norm, w_out=w_out, attn_post_norm=attn_post_norm, ffn_pre_norm=ffn_pre_norm, w_gate_up=w_gate_up, w_down=w_down, ffn_post_norm=ffn_post_norm, ple_proj=ple_proj, ple_gate=ple_gate, loss_target=loss_target, m_attn_pre_norm=m_attn_pre_norm, m_w_in=m_w_in, m_hg_lower_gamma=m_hg_lower_gamma, m_hg_out_norm=m_hg_out_norm, m_sb_out_norm=m_sb_out_norm, m_w_out=m_w_out, m_attn_post_norm=m_attn_post_norm, m_ffn_pre_norm=m_ffn_pre_norm, m_w_gate_up=m_w_gate_up, m_w_down=m_w_down, m_ffn_post_norm=m_ffn_post_norm, m_ple_proj=m_ple_proj, m_ple_gate=m_ple_gate, v_attn_pre_norm=v_attn_pre_norm, v_w_in=v_w_in, v_hg_lower_gamma=v_hg_lower_gamma, v_hg_out_norm=v_hg_out_norm, v_sb_out_norm=v_sb_out_norm, v_w_out=v_w_out, v_attn_post_norm=v_attn_post_norm, v_ffn_pre_norm=v_ffn_pre_norm, v_w_gate_up=v_w_gate_up, v_w_down=v_w_down, v_ffn_post_norm=v_ffn_post_norm, v_ple_proj=v_ple_proj, v_ple_gate=v_ple_gate)
    weights = {n: given[n] for n in TWIN_WEIGHTS}
    shared = {n: given[n] for n in SHARED_INPUTS}
    per_example = {n: given[n] for n in ['x', 'p']}
    grad_fn = _jax.value_and_grad(_loss, argnums=(0, 1))

    def one_microbatch(ex, loss_target):
        ex = dict(ex)
        diff = ex.pop(TWIN_DIFF_INPUT)
        return grad_fn(weights, diff, {**shared, **ex}, loss_target)

    if N_MICROBATCH == 1:
        loss, (grad_w, grad_x) = one_microbatch(per_example, given["loss_target"])
    else:
        def body(carry, xs):
            loss_sum, grad_sum = carry
            l_k, (gw_k, gx_k) = one_microbatch(xs[0], xs[1])
            with _jax.named_scope("update"):
                return (loss_sum + l_k, _jax.tree.map(_jnp.add, grad_sum, gw_k)), gx_k

        init = (_jnp.zeros((), _jnp.float32), _jax.tree.map(_jnp.zeros_like, weights))
        (loss, grad_w), grad_x = _jax.lax.scan(body, init, (per_example, given["loss_target"]))
    with _jax.named_scope("update"):
        delta_w, new_m, new_v = {}, {}, {}
        for n in TWIN_WEIGHTS:
            delta_w[n], new_m[n], new_v[n] = _adamw(weights[n], grad_w[n], given["m_" + n], given["v_" + n])
    return (loss, grad_x, *[grad_w[n] for n in TWIN_WEIGHTS], *[delta_w[n] for n in TWIN_WEIGHTS],
            *[new_m[n] for n in TWIN_WEIGHTS], *[new_v[n] for n in TWIN_WEIGHTS])
```

```python
import functools

import jax
import jax.numpy as jnp
from jax import lax
from jax.experimental import pallas as pl
from jax.experimental.pallas import tpu as pltpu

F32 = jnp.float32
BF16 = jnp.bfloat16

D_MODEL = 1024
HG_WIDTH = 512
HG_HEADS = 4
HG_DK = 128
SB_WIDTH = 512
SB_HEADS = 8
SB_DH = 64
SB_BLOCK = 128
D_FF = 2816
PLE_DIM = 256
IN_COLS = 4 * HG_WIDTH + 3 * SB_WIDTH
EPS = 1e-6
N_DEV = 8

HG_SUB = 16
HG_TILE = 128
FFN_TF = 256
ROW_TILE = 256
VMEM_LIMIT = 56 * 1024 * 1024

ADAM_LR = 0.001
ADAM_B1 = 0.9
ADAM_B2 = 0.999
ADAM_EPS = 1e-08
ADAM_WD = 0.01
ADAM_STEP = 10

PACK_SIZES = (448, 128, 704, 352, 32, 128)
PACK_ROWS = sum(PACK_SIZES)
PACK_OFFS = tuple(sum(PACK_SIZES[:i]) for i in range(len(PACK_SIZES)))

NT_DIMS = (((1,), (1,)), ((), ()))
TN_DIMS = (((0,), (0,)), ((), ()))


def _params(semantics):
    return pltpu.CompilerParams(dimension_semantics=semantics, vmem_limit_bytes=VMEM_LIMIT)


def _dot(a, b):
    return jnp.dot(a, b, preferred_element_type=F32)


def _dot_nt(a, b):
    return lax.dot_general(a, b, NT_DIMS, preferred_element_type=F32)


def _dot_tn(a, b):
    return lax.dot_general(a, b, TN_DIMS, preferred_element_type=F32)


def _sigmoid(z):
    return 1.0 / (1.0 + jnp.exp(-z))


def _rstd(xv):
    return lax.rsqrt(jnp.mean(xv * xv, axis=-1, keepdims=True) + EPS)


def _rms_bwd(xv, r, g, dn):
    xn = xv * r
    gh = dn * g
    dx = r * (gh - xn * jnp.mean(gh * xn, axis=-1, keepdims=True))
    return dx, dn * xn


def _colsum(a):
    return jnp.sum(a, axis=0, keepdims=True)


def _row_spec(tm, width, col=0):
    return pl.BlockSpec((tm, width), lambda i, col=col: (i, col))


def _full_spec(shape):
    return pl.BlockSpec(shape, lambda i: (0,) * len(shape))


def _in_proj(x, g, w):
    T = x.shape[0]
    tm = ROW_TILE

    def body(x_ref, g_ref, w_ref, proj_ref, u_ref):
        xv = x_ref[...]
        u = (xv * _rstd(xv) * g_ref[...]).astype(BF16)
        u_ref[...] = u
        proj_ref[...] = _dot(u, w_ref[...])

    return pl.pallas_call(
        body, grid=(T // tm,), name="in_proj",
        in_specs=[_row_spec(tm, D_MODEL), _full_spec((1, D_MODEL)), _full_spec((D_MODEL, IN_COLS))],
        out_specs=[_row_spec(tm, IN_COLS), _row_spec(tm, D_MODEL)],
        out_shape=[jax.ShapeDtypeStruct((T, IN_COLS), F32), jax.ShapeDtypeStruct((T, D_MODEL), BF16)],
        compiler_params=_params(("parallel",)),
    )(x, g, w)


def _out_proj(o_hg, proj, o_sb, x, g_hg, g_sb, g_post, w):
    T = x.shape[0]
    tm = ROW_TILE

    def body(ohg_ref, hg_ref, osb_ref, x_ref, ghg_ref, gsb_ref, gpost_ref, w_ref, cat_ref, mix_ref, h1_ref):
        ohg = ohg_ref[...]
        hg = hg_ref[...]
        osb = osb_ref[...]
        a = ohg * _rstd(ohg) * ghg_ref[...] * (hg * _sigmoid(hg))
        n2 = osb * _rstd(osb) * gsb_ref[...]
        cat = jnp.concatenate([a, n2], axis=1).astype(BF16)
        cat_ref[...] = cat
        mix = _dot(cat, w_ref[...])
        mix_ref[...] = mix
        h1_ref[...] = x_ref[...] + mix * _rstd(mix) * gpost_ref[...]

    return pl.pallas_call(
        body, grid=(T // tm,), name="out_proj",
        in_specs=[_row_spec(tm, HG_WIDTH), _row_spec(tm, HG_WIDTH, 3), _row_spec(tm, SB_WIDTH), _row_spec(tm, D_MODEL),
                  _full_spec((1, HG_WIDTH)), _full_spec((1, SB_WIDTH)), _full_spec((1, D_MODEL)),
                  _full_spec((D_MODEL, D_MODEL))],
        out_specs=[_row_spec(tm, D_MODEL)] * 3,
        out_shape=[jax.ShapeDtypeStruct((T, D_MODEL), BF16), jax.ShapeDtypeStruct((T, D_MODEL), F32),
                   jax.ShapeDtypeStruct((T, D_MODEL), F32)],
        compiler_params=_params(("parallel",)),
    )(o_hg, proj, o_sb, x, g_hg, g_sb, g_post, w)


def _ffn_fwd(h1, g_pre, w_gu, w_down, g_post):
    T = h1.shape[0]
    tm = 2 * ROW_TILE
    n_f = D_FF // FFN_TF

    def body(h1_ref, gpre_ref, wg_ref, wu_ref, wd_ref, gpost_ref, u2_ref, gate_ref, up_ref, y_ref, h2_ref, acc_ref):
        j = pl.program_id(1)

        @pl.when(j == 0)
        def _():
            hv = h1_ref[...]
            u2_ref[...] = (hv * _rstd(hv) * gpre_ref[...]).astype(BF16)
            acc_ref[...] = jnp.zeros_like(acc_ref)

        u2 = u2_ref[...]
        gate = _dot(u2, wg_ref[...])
        up = _dot(u2, wu_ref[...])
        gate_ref[...] = gate
        up_ref[...] = up
        a = (gate * _sigmoid(gate) * up).astype(BF16)
        acc_ref[...] += _dot(a, wd_ref[...])

        @pl.when(j == n_f - 1)
        def _():
            y = acc_ref[...]
            y_ref[...] = y
            h2_ref[...] = h1_ref[...] + y * _rstd(y) * gpost_ref[...]

    row = lambda width: pl.BlockSpec((tm, width), lambda i, j: (i, 0))
    vec = pl.BlockSpec((1, D_MODEL), lambda i, j: (0, 0))
    return pl.pallas_call(
        body, grid=(T // tm, n_f), name="ffn_fwd",
        in_specs=[row(D_MODEL), vec,
                  pl.BlockSpec((D_MODEL, FFN_TF), lambda i, j: (0, j)),
                  pl.BlockSpec((D_MODEL, FFN_TF), lambda i, j: (0, j + n_f)),
                  pl.BlockSpec((FFN_TF, D_MODEL), lambda i, j: (j, 0)), vec],
        out_specs=[row(D_MODEL),
                   pl.BlockSpec((tm, FFN_TF), lambda i, j: (i, j)),
                   pl.BlockSpec((tm, FFN_TF), lambda i, j: (i, j)),
                   row(D_MODEL), row(D_MODEL)],
        out_shape=[jax.ShapeDtypeStruct((T, D_MODEL), BF16), jax.ShapeDtypeStruct((T, D_FF), F32),
                   jax.ShapeDtypeStruct((T, D_FF), F32), jax.ShapeDtypeStruct((T, D_MODEL), F32),
                   jax.ShapeDtypeStruct((T, D_MODEL), F32)],
        scratch_shapes=[pltpu.VMEM((tm, D_MODEL), F32)],
        compiler_params=_params(("parallel", "arbitrary")),
    )(h1, g_pre, w_gu, w_gu, w_down, g_post)


def _ple_loss(h2, p, target, w_pp, w_pg):
    T = h2.shape[0]
    tm = ROW_TILE

    def body(h2_ref, p_ref, t_ref, wpp_ref, wpg_ref, dh2_ref, de_ref, dg_ref, loss_ref):
        i = pl.program_id(0)
        h2v = h2_ref[...]
        e = _dot(p_ref[...].astype(BF16), wpp_ref[...])
        sg = _sigmoid(_dot(h2v.astype(BF16), wpg_ref[...]))
        diff = h2v + e * sg - t_ref[...]
        part = jnp.sum(jnp.sum(diff * diff, axis=1, keepdims=True), axis=0, keepdims=True) * (0.5 / D_MODEL)

        @pl.when(i == 0)
        def _():
            loss_ref[...] = jnp.zeros_like(loss_ref)

        loss_ref[...] += jnp.broadcast_to(part, loss_ref.shape)
        dh3 = diff * (1.0 / D_MODEL)
        de_ref[...] = (dh3 * sg).astype(BF16)
        dg = (dh3 * e * sg * (1.0 - sg)).astype(BF16)
        dg_ref[...] = dg
        dh2_ref[...] = dh3 + _dot_nt(dg, wpg_ref[...])

    return pl.pallas_call(
        body, grid=(T // tm,), name="ple_loss",
        in_specs=[_row_spec(tm, D_MODEL), _row_spec(tm, PLE_DIM), _row_spec(tm, D_MODEL),
                  _full_spec((PLE_DIM, D_MODEL)), _full_spec((D_MODEL, D_MODEL))],
        out_specs=[_row_spec(tm, D_MODEL)] * 3 + [_full_spec((8, 128))],
        out_shape=[jax.ShapeDtypeStruct((T, D_MODEL), F32), jax.ShapeDtypeStruct((T, D_MODEL), BF16),
                   jax.ShapeDtypeStruct((T, D_MODEL), BF16), jax.ShapeDtypeStruct((8, 128), F32)],
        compiler_params=_params(("arbitrary",)),
    )(h2, p, target, w_pp, w_pg)


def _ffn_bwd(dh2, y, h1, gate, up, g_post, g_pre, w_gu, w_down):
    T = h1.shape[0]
    tm = 2 * ROW_TILE
    n_f = D_FF // FFN_TF

    def body(dh2_ref, y_ref, h1_ref, gate_ref, up_ref, gpost_ref, gpre_ref, wg_ref, wu_ref, wd_ref,
             dy_ref, a_ref, dgate_ref, dup_ref, dh1_ref, dgpost_ref, dgpre_ref, du2_ref):
        i = pl.program_id(0)
        j = pl.program_id(1)

        @pl.when((i == 0) & (j == 0))
        def _():
            dgpost_ref[...] = jnp.zeros_like(dgpost_ref)
            dgpre_ref[...] = jnp.zeros_like(dgpre_ref)

        @pl.when(j == 0)
        def _():
            yv = y_ref[...]
            dy, gterm = _rms_bwd(yv, _rstd(yv), gpost_ref[...], dh2_ref[...])
            dy_ref[...] = dy.astype(BF16)
            dgpost_ref[...] += _colsum(gterm)
            du2_ref[...] = jnp.zeros_like(du2_ref)

        da = _dot_nt(dy_ref[...], wd_ref[...])
        gate = gate_ref[...]
        up = up_ref[...]
        s = _sigmoid(gate)
        sil = gate * s
        a_ref[...] = (sil * up).astype(BF16)
        dgate = (da * up * (s * (1.0 + gate * (1.0 - s)))).astype(BF16)
        dup = (da * sil).astype(BF16)
        dgate_ref[...] = dgate
        dup_ref[...] = dup
        du2_ref[...] += _dot_nt(dgate, wg_ref[...]) + _dot_nt(dup, wu_ref[...])

        @pl.when(j == n_f - 1)
        def _():
            hv = h1_ref[...]
            dx, gterm = _rms_bwd(hv, _rstd(hv), gpre_ref[...], du2_ref[...])
            dh1_ref[...] = dh2_ref[...] + dx
            dgpre_ref[...] += _colsum(gterm)

    row = lambda width: pl.BlockSpec((tm, width), lambda i, j: (i, 0))
    col = pl.BlockSpec((tm, FFN_TF), lambda i, j: (i, j))
    vec = pl.BlockSpec((1, D_MODEL), lambda i, j: (0, 0))
    return pl.pallas_call(
        body, grid=(T // tm, n_f), name="ffn_bwd",
        in_specs=[row(D_MODEL), row(D_MODEL), row(D_MODEL), col, col, vec, vec,
                  pl.BlockSpec((D_MODEL, FFN_TF), lambda i, j: (0, j)),
                  pl.BlockSpec((D_MODEL, FFN_TF), lambda i, j: (0, j + n_f)),
                  pl.BlockSpec((FFN_TF, D_MODEL), lambda i, j: (j, 0))],
        out_specs=[row(D_MODEL), col, col, col, row(D_MODEL), vec, vec],
        out_shape=[jax.ShapeDtypeStruct((T, D_MODEL), BF16), jax.ShapeDtypeStruct((T, D_FF), BF16),
                   jax.ShapeDtypeStruct((T, D_FF), BF16), jax.ShapeDtypeStruct((T, D_FF), BF16),
                   jax.ShapeDtypeStruct((T, D_MODEL), F32), jax.ShapeDtypeStruct((1, D_MODEL), F32),
                   jax.ShapeDtypeStruct((1, D_MODEL), F32)],
        scratch_shapes=[pltpu.VMEM((tm, D_MODEL), F32)],
        compiler_params=_params(("arbitrary", "arbitrary")),
    )(dh2, y, h1, gate, up, g_post, g_pre, w_gu, w_gu, w_down)


def _out_bwd(dh1, mix, o_hg, proj, o_sb, g_hg, g_sb, g_post, w):
    T = dh1.shape[0]
    tm = ROW_TILE

    def body(dh1_ref, mix_ref, ohg_ref, hg_ref, osb_ref, ghg_ref, gsb_ref, gpost_ref, w_ref,
             dmix_ref, dohg_ref, dhg_ref, dosb_ref, dgpost_ref, dghg_ref, dgsb_ref):
        i = pl.program_id(0)

        @pl.when(i == 0)
        def _():
            dgpost_ref[...] = jnp.zeros_like(dgpost_ref)
            dghg_ref[...] = jnp.zeros_like(dghg_ref)
            dgsb_ref[...] = jnp.zeros_like(dgsb_ref)

        mix = mix_ref[...]
        dmix, gterm = _rms_bwd(mix, _rstd(mix), gpost_ref[...], dh1_ref[...])
        dgpost_ref[...] += _colsum(gterm)
        dmix = dmix.astype(BF16)
        dmix_ref[...] = dmix
        dcat = _dot_nt(dmix, w_ref[...])
        da = dcat[:, :HG_WIDTH]
        dn2 = dcat[:, HG_WIDTH:]
        ohg = ohg_ref[...]
        r1 = _rstd(ohg)
        hg = hg_ref[...]
        s = _sigmoid(hg)
        dhg_ref[...] = da * (ohg * r1 * ghg_ref[...]) * (s * (1.0 + hg * (1.0 - s)))
        dohg, gterm = _rms_bwd(ohg, r1, ghg_ref[...], da * (hg * s))
        dohg_ref[...] = dohg
        dghg_ref[...] += _colsum(gterm)
        osb = osb_ref[...]
        dosb, gterm = _rms_bwd(osb, _rstd(osb), gsb_ref[...], dn2)
        dosb_ref[...] = dosb
        dgsb_ref[...] += _colsum(gterm)

    return pl.pallas_call(
        body, grid=(T // tm,), name="out_bwd",
        in_specs=[_row_spec(tm, D_MODEL), _row_spec(tm, D_MODEL), _row_spec(tm, HG_WIDTH), _row_spec(tm, HG_WIDTH, 3),
                  _row_spec(tm, SB_WIDTH), _full_spec((1, HG_WIDTH)), _full_spec((1, SB_WIDTH)),
                  _full_spec((1, D_MODEL)), _full_spec((D_MODEL, D_MODEL))],
        out_specs=[_row_spec(tm, D_MODEL), _row_spec(tm, HG_WIDTH), _row_spec(tm, HG_WIDTH), _row_spec(tm, SB_WIDTH),
                   _full_spec((1, D_MODEL)), _full_spec((1, HG_WIDTH)), _full_spec((1, SB_WIDTH))],
        out_shape=[jax.ShapeDtypeStruct((T, D_MODEL), BF16), jax.ShapeDtypeStruct((T, HG_WIDTH), F32),
                   jax.ShapeDtypeStruct((T, HG_WIDTH), F32), jax.ShapeDtypeStruct((T, SB_WIDTH), F32),
                   jax.ShapeDtypeStruct((1, D_MODEL), F32), jax.ShapeDtypeStruct((1, HG_WIDTH), F32),
                   jax.ShapeDtypeStruct((1, SB_WIDTH), F32)],
        compiler_params=_params(("arbitrary",)),
    )(dh1, mix, o_hg, proj, o_sb, g_hg, g_sb, g_post, w)


def _in_bwd(d_hgrn, d_hg, d_sq, d_sk, d_sv, x, dh1, g_pre, w):
    T = x.shape[0]
    tm = ROW_TILE

    def body(dh_ref, dhg_ref, dsq_ref, dsk_ref, dsv_ref, x_ref, dh1_ref, gpre_ref, w_ref, dproj_ref, dx_ref, dgpre_ref):
        i = pl.program_id(0)

        @pl.when(i == 0)
        def _():
            dgpre_ref[...] = jnp.zeros_like(dgpre_ref)

        dp = jnp.concatenate([dh_ref[...], dhg_ref[...], dsq_ref[...], dsk_ref[...], dsv_ref[...]], axis=1).astype(BF16)
        dproj_ref[...] = dp
        du = _dot_nt(dp, w_ref[...])
        xv = x_ref[...]
        dx, gterm = _rms_bwd(xv, _rstd(xv), gpre_ref[...], du)
        dx_ref[...] = dh1_ref[...] + dx
        dgpre_ref[...] += _colsum(gterm)

    return pl.pallas_call(
        body, grid=(T // tm,), name="in_bwd",
        in_specs=[_row_spec(tm, 3 * HG_WIDTH), _row_spec(tm, HG_WIDTH), _row_spec(tm, SB_WIDTH), _row_spec(tm, SB_WIDTH),
                  _row_spec(tm, SB_WIDTH), _row_spec(tm, D_MODEL), _row_spec(tm, D_MODEL), _full_spec((1, D_MODEL)),
                  _full_spec((D_MODEL, IN_COLS))],
        out_specs=[_row_spec(tm, IN_COLS), _row_spec(tm, D_MODEL), _full_spec((1, D_MODEL))],
        out_shape=[jax.ShapeDtypeStruct((T, IN_COLS), BF16), jax.ShapeDtypeStruct((T, D_MODEL), F32),
                   jax.ShapeDtypeStruct((1, D_MODEL), F32)],
        compiler_params=_params(("arbitrary",)),
    )(d_hgrn, d_hg, d_sq, d_sk, d_sv, x, dh1, g_pre, w)


def _wgrad(a, b, name):
    T, K = a.shape
    N = b.shape[1]
    tk = next(t for t in (512, 256, 128) if K % t == 0)
    tn = next(t for t in (512, 256, 128) if N % t == 0)
    tt = 512
    n_t = T // tt

    def body(a_ref, b_ref, o_ref, acc_ref):
        t = pl.program_id(2)

        @pl.when(t == 0)
        def _():
            acc_ref[...] = jnp.zeros_like(acc_ref)

        acc_ref[...] += _dot_tn(a_ref[...].astype(BF16), b_ref[...].astype(BF16))

        @pl.when(t == n_t - 1)
        def _():
            o_ref[...] = acc_ref[...].astype(BF16)

    return pl.pallas_call(
        body, grid=(K // tk, N // tn, n_t), name=name,
        in_specs=[pl.BlockSpec((tt, tk), lambda k, n, t: (t, k)), pl.BlockSpec((tt, tn), lambda k, n, t: (t, n))],
        out_specs=pl.BlockSpec((tk, tn), lambda k, n, t: (k, n)),
        out_shape=jax.ShapeDtypeStruct((K, N), BF16),
        scratch_shapes=[pltpu.VMEM((tk, tn), F32)],
        compiler_params=_params(("parallel", "parallel", "arbitrary")),
    )(a, b)


def _hg_gates(hq, hf, gam):
    g0 = gam[0:1, :]
    g1 = gam[1:2, :]
    mx = jnp.maximum(g0, g1)
    e0 = jnp.exp(g0 - mx)
    e1 = jnp.exp(g1 - mx)
    lb = e0 / (e0 + e1)
    s = _sigmoid(hf)
    f = lb + (1.0 - lb) * s
    sq = _sigmoid(hq)
    return hq * sq, sq, s, f, (1.0 - lb) * (1.0 - s), jnp.log(f), lb


def _prefix_in_sub(a, rowmod):
    n = a.shape[0]
    sh = 1
    while sh < HG_SUB:
        a = a + jnp.where(rowmod >= sh, pltpu.roll(a, sh, axis=0), 0.0)
        sh *= 2
    return a


def _suffix_in_sub(a, rowmod):
    n = a.shape[0]
    sh = 1
    while sh < HG_SUB:
        a = a + jnp.where(rowmod < HG_SUB - sh, pltpu.roll(a, n - sh, axis=0), 0.0)
        sh *= 2
    return a


def _hg_fwd(proj, gam):
    T = proj.shape[0]
    tb = HG_TILE
    n_sub = tb // HG_SUB

    def body(hq_ref, hf_ref, hi_ref, gam_ref, o_ref, hist_ref, st_ref, qt_ref, kk_ref, b_ref, od_ref):
        i = pl.program_id(0)

        @pl.when(i == 0)
        def _():
            st_ref[...] = jnp.zeros_like(st_ref)

        rowmod_w = lax.broadcasted_iota(jnp.int32, (tb, HG_WIDTH), 0) % HG_SUB
        rowmod = lax.broadcasted_iota(jnp.int32, (tb, 1), 0) % HG_SUB
        q, _, _, _, kk, lf, _ = _hg_gates(hq_ref[...], hf_ref[...], gam_ref[...])
        b = _prefix_in_sub(lf, rowmod_w)
        qt_ref[...] = (q * jnp.exp(b)).astype(BF16)
        kk_ref[...] = kk
        b_ref[...] = b

        for h in range(HG_HEADS):
            cs = slice(h * HG_DK, (h + 1) * HG_DK)
            qh = q[:, cs]
            kh = kk[:, cs]
            bh = b[:, cs]
            vh = hi_ref[:, cs]
            acc = jnp.sum(qh * kh, axis=1, keepdims=True) * vh
            for d in range(1, HG_SUB):
                e = qh * pltpu.roll(kh, d, axis=0) * jnp.exp(bh - pltpu.roll(bh, d, axis=0))
                sc = jnp.where(rowmod >= d, jnp.sum(e, axis=1, keepdims=True), 0.0)
                acc = acc + sc * pltpu.roll(vh, d, axis=0)
            od_ref[:, cs] = acc

        for m in range(n_sub):
            rs = slice(m * HG_SUB, (m + 1) * HG_SUB)
            for h in range(HG_HEADS):
                cs = slice(h * HG_DK, (h + 1) * HG_DK)
                st = st_ref[h]
                hist_ref[m, h] = st
                o_int = _dot_nt(qt_ref[rs, cs], st.astype(BF16))
                bs = b_ref[rs, cs]
                bl = bs[HG_SUB - 1:HG_SUB, :]
                kt = (kk_ref[rs, cs] * jnp.exp(bl - bs)).astype(BF16)
                st_ref[h] = st * jnp.exp(bl) + _dot_tn(hi_ref[rs, cs].astype(BF16), kt)
                o_ref[rs, cs] = o_int + od_ref[rs, cs]

    col = lambda c: pl.BlockSpec((tb, HG_WIDTH), lambda i, c=c: (i, c))
    return pl.pallas_call(
        body, grid=(T // tb,), name="hg_fwd",
        in_specs=[col(0), col(1), col(2), _full_spec((2, HG_WIDTH))],
        out_specs=[pl.BlockSpec((tb, HG_WIDTH), lambda i: (i, 0)),
                   pl.BlockSpec((n_sub, HG_HEADS, HG_DK, HG_DK), lambda i: (i, 0, 0, 0))],
        out_shape=[jax.ShapeDtypeStruct((T, HG_WIDTH), F32),
                   jax.ShapeDtypeStruct((T // HG_SUB, HG_HEADS, HG_DK, HG_DK), F32)],
        scratch_shapes=[pltpu.VMEM((HG_HEADS, HG_DK, HG_DK), F32), pltpu.VMEM((tb, HG_WIDTH), BF16),
                        pltpu.VMEM((tb, HG_WIDTH), F32), pltpu.VMEM((tb, HG_WIDTH), F32),
                        pltpu.VMEM((tb, HG_WIDTH), F32)],
        compiler_params=_params(("arbitrary",)),
    )(proj, proj, proj, gam)


def _hg_bwd(proj, gam, d_o, hist):
    T = proj.shape[0]
    tb = HG_TILE
    n_sub = tb // HG_SUB
    n_t = T // tb

    def body(hq_ref, hf_ref, hi_ref, gam_ref, do_ref, hist_ref, dout_ref, dlb_ref,
             dst_ref, q_ref, kk_ref, b_ref, dq_ref, dkk_ref, db_ref, dv_ref):
        i = pl.program_id(0)

        @pl.when(i == 0)
        def _():
            dst_ref[...] = jnp.zeros_like(dst_ref)
            dlb_ref[...] = jnp.zeros_like(dlb_ref)

        rowmod_w = lax.broadcasted_iota(jnp.int32, (tb, HG_WIDTH), 0) % HG_SUB
        rowmod = lax.broadcasted_iota(jnp.int32, (tb, 1), 0) % HG_SUB
        last_row = lax.broadcasted_iota(jnp.int32, (HG_SUB, 1), 0) == HG_SUB - 1
        hq = hq_ref[...]
        q, sq, s, f, kk, lf, lb = _hg_gates(hq, hf_ref[...], gam_ref[...])
        b = _prefix_in_sub(lf, rowmod_w)
        q_ref[...] = q
        kk_ref[...] = kk
        b_ref[...] = b

        for m in reversed(range(n_sub)):
            rs = slice(m * HG_SUB, (m + 1) * HG_SUB)
            for h in range(HG_HEADS):
                cs = slice(h * HG_DK, (h + 1) * HG_DK)
                st = hist_ref[m, h]
                dst = dst_ref[h]
                dstb = dst.astype(BF16)
                dos = do_ref[rs, cs].astype(BF16)
                vs = hi_ref[rs, cs].astype(BF16)
                bs = b_ref[rs, cs]
                bl = bs[HG_SUB - 1:HG_SUB, :]
                ebl = jnp.exp(bl)
                qt = q_ref[rs, cs] * jnp.exp(bs)
                decay = jnp.exp(bl - bs)
                kt = kk_ref[rs, cs] * decay
                dqt = _dot(dos, st.astype(BF16))
                dkt = _dot(vs, dstb)
                dv_ref[rs, cs] = _dot_nt(kt.astype(BF16), dstb)
                dst_ref[h] = dst * ebl + _dot_tn(dos, qt.astype(BF16))
                dq_ref[rs, cs] = dqt * jnp.exp(bs)
                dkk_ref[rs, cs] = dkt * decay
                dktk = dkt * kt
                dbl = _colsum(dktk) + _colsum(dst * st) * ebl
                db_ref[rs, cs] = dqt * qt - dktk + jnp.where(last_row, dbl, 0.0)

        for h in range(HG_HEADS):
            cs = slice(h * HG_DK, (h + 1) * HG_DK)
            qh = q[:, cs]
            kh = kk[:, cs]
            bh = b[:, cs]
            vh = hi_ref[:, cs]
            doh = do_ref[:, cs]
            dsc = jnp.sum(doh * vh, axis=1, keepdims=True)
            sc = jnp.sum(qh * kh, axis=1, keepdims=True)
            dqh = dsc * kh
            dkh = dsc * qh
            dvh = sc * doh
            dbh = jnp.zeros_like(qh)
            for d in range(1, HG_SUB):
                back = tb - d
                valid = rowmod >= d
                kd = pltpu.roll(kh, d, axis=0)
                ex = jnp.where(valid, jnp.exp(bh - pltpu.roll(bh, d, axis=0)), 0.0)
                pd = kd * ex
                sc = jnp.sum(qh * pd, axis=1, keepdims=True)
                dsc = jnp.where(valid, jnp.sum(doh * pltpu.roll(vh, d, axis=0), axis=1, keepdims=True), 0.0)
                dqh = dqh + dsc * pd
                w = dsc * qh * pd
                dkh = dkh + pltpu.roll(dsc * qh * ex, back, axis=0)
                dvh = dvh + pltpu.roll(sc * doh, back, axis=0)
                dbh = dbh + w - pltpu.roll(w, back, axis=0)
            dq_ref[:, cs] += dqh
            dkk_ref[:, cs] += dkh
            dv_ref[:, cs] += dvh
            db_ref[:, cs] += dbh

        dlf = _suffix_in_sub(db_ref[...], rowmod_w)
        df = dlf / f - dkk_ref[...]
        dout_ref[:, 0:HG_WIDTH] = dq_ref[...] * (sq * (1.0 + hq * (1.0 - sq)))
        dout_ref[:, HG_WIDTH:2 * HG_WIDTH] = df * (1.0 - lb) * (s * (1.0 - s))
        dout_ref[:, 2 * HG_WIDTH:3 * HG_WIDTH] = dv_ref[...]
        dlb_ref[...] += _colsum(df * (1.0 - s))

    col = lambda c: pl.BlockSpec((tb, HG_WIDTH), lambda i, c=c: (n_t - 1 - i, c))
    return pl.pallas_call(
        body, grid=(n_t,), name="hg_bwd",
        in_specs=[col(0), col(1), col(2), _full_spec((2, HG_WIDTH)),
                  pl.BlockSpec((tb, HG_WIDTH), lambda i: (n_t - 1 - i, 0)),
                  pl.BlockSpec((n_sub, HG_HEADS, HG_DK, HG_DK), lambda i: (n_t - 1 - i, 0, 0, 0))],
        out_specs=[pl.BlockSpec((tb, 3 * HG_WIDTH), lambda i: (n_t - 1 - i, 0)), _full_spec((1, HG_WIDTH))],
        out_shape=[jax.ShapeDtypeStruct((T, 3 * HG_WIDTH), F32), jax.ShapeDtypeStruct((1, HG_WIDTH), F32)],
        scratch_shapes=[pltpu.VMEM((HG_HEADS, HG_DK, HG_DK), F32)] + [pltpu.VMEM((tb, HG_WIDTH), F32)] * 7,
        compiler_params=_params(("arbitrary",)),
    )(proj, proj, proj, gam, d_o, hist)


def _sb_masks():
    row = lax.broadcasted_iota(jnp.int32, (SB_BLOCK, SB_BLOCK), 0)
    col = lax.broadcasted_iota(jnp.int32, (SB_BLOCK, SB_BLOCK), 1)
    suffix = (row >= col).astype(BF16)
    prefix = (row <= col).astype(BF16)
    return suffix, prefix, col < row, col


def _sum_right(a, suffix):
    hi = a.astype(BF16)
    lo = (a - hi.astype(F32)).astype(BF16)
    return _dot(hi, suffix) + _dot(lo, suffix)


def _sb_block(qb, kj, suffix, causal, c):
    z = _dot_nt(qb, kj) * (SB_DH ** -0.5)
    sp = jnp.maximum(z, 0.0) + jnp.log(1.0 + jnp.exp(-jnp.abs(z)))
    if causal is not None:
        sp = jnp.where(causal, sp, 0.0)
    big_l = _sum_right(sp, suffix)
    a = jnp.exp(z - big_l - c)
    if causal is not None:
        a = jnp.where(causal, a, 0.0)
    return z, a, big_l


def _sb_fwd(q, k, v):
    H, T, dh = q.shape

    def body(q_ref, k_ref, v_ref, o_ref, right_ref):
        i = pl.program_id(1)
        suffix, _, causal, lane = _sb_masks()
        qb = q_ref[0]

        def step(j, carry, mask):
            acc, c, right = carry
            rows = pl.ds(pl.multiple_of(j * SB_BLOCK, SB_BLOCK), SB_BLOCK)
            _, a, big_l = _sb_block(qb, k_ref[0, rows, :], suffix, mask, c)
            return acc + _dot(a.astype(BF16), v_ref[0, rows, :]), c + big_l[:, 0:1], jnp.where(lane == j, c, right)

        zero = jnp.zeros((SB_BLOCK, 1), F32)
        carry = step(i, (jnp.zeros((SB_BLOCK, dh), F32), zero, jnp.zeros((SB_BLOCK, SB_BLOCK), F32)), causal)
        carry = lax.fori_loop(0, i, lambda t, cr: step(i - 1 - t, cr, None), carry)
        o_ref[0] = carry[0]
        right_ref[0] = carry[2]

    assert T // SB_BLOCK <= SB_BLOCK
    blk = pl.BlockSpec((1, SB_BLOCK, dh), lambda h, i: (h, i, 0))
    wide = pl.BlockSpec((1, SB_BLOCK, SB_BLOCK), lambda h, i: (h, i, 0))
    whole = pl.BlockSpec((1, T, dh), lambda h, i: (h, 0, 0))
    return pl.pallas_call(
        body, grid=(H, T // SB_BLOCK), name="sb_fwd",
        in_specs=[blk, whole, whole], out_specs=[blk, wide],
        out_shape=[jax.ShapeDtypeStruct((H, T, dh), F32), jax.ShapeDtypeStruct((H, T, SB_BLOCK), F32)],
        compiler_params=_params(("parallel", "arbitrary")),
    )(q, k, v)


def _sb_bwd(q, k, v, right, d_o):
    H, T, dh = q.shape
    scale = SB_DH ** -0.5

    def body(q_ref, k_ref, v_ref, right_ref, do_ref, dq_ref, dk_ref, dv_ref):
        i = pl.program_id(1)

        @pl.when(i == 0)
        def _():
            dk_ref[...] = jnp.zeros_like(dk_ref)
            dv_ref[...] = jnp.zeros_like(dv_ref)

        suffix, prefix, causal, lane = _sb_masks()
        qb = q_ref[0]
        dob = do_ref[0].astype(BF16)
        right = right_ref[0]

        def step(j, carry, mask):
            dq, gc = carry
            rows = pl.ds(pl.multiple_of(j * SB_BLOCK, SB_BLOCK), SB_BLOCK)
            kj = k_ref[0, rows, :]
            vj = v_ref[0, rows, :]
            c = jnp.sum(jnp.where(lane == j, right, 0.0), axis=1, keepdims=True)
            z, a, _ = _sb_block(qb, kj, suffix, mask, c)
            g = a * _dot_nt(dob, vj)
            g_left = _sum_right(g, prefix)
            dz = g - _sigmoid(z) * (gc + g_left)
            if mask is not None:
                dz = jnp.where(mask, dz, 0.0)
            dzb = (dz * scale).astype(BF16)
            dk_ref[0, rows, :] += _dot_tn(dzb, qb)
            dv_ref[0, rows, :] += _dot_tn(a.astype(BF16), dob)
            return dq + _dot(dzb, kj), gc + g_left[:, SB_BLOCK - 1:SB_BLOCK]

        carry = (jnp.zeros((SB_BLOCK, dh), F32), jnp.zeros((SB_BLOCK, 1), F32))
        carry = lax.fori_loop(0, i, lambda j, cr: step(j, cr, None), carry)
        dq_ref[0] = step(i, carry, causal)[0]

    blk = pl.BlockSpec((1, SB_BLOCK, dh), lambda h, i: (h, i, 0))
    wide = pl.BlockSpec((1, SB_BLOCK, SB_BLOCK), lambda h, i: (h, i, 0))
    whole = pl.BlockSpec((1, T, dh), lambda h, i: (h, 0, 0))
    return pl.pallas_call(
        body, grid=(H, T // SB_BLOCK), name="sb_bwd",
        in_specs=[blk, whole, whole, wide, blk], out_specs=[blk, whole, whole],
        out_shape=[jax.ShapeDtypeStruct((H, T, dh), F32)] * 3,
        compiler_params=_params(("parallel", "arbitrary")),
    )(q, k, v, right, d_o)


HBM = pl.BlockSpec(memory_space=pltpu.HBM)
MESH = pl.DeviceIdType.MESH


def _place():
    return lax.axis_index("x"), lax.axis_index("y"), lax.axis_index("c")


def _all_gather(block):
    rows, cols = block.shape

    def body(x_ref, out_ref, send_sems, recv_sems, local_sem):
        x, y, c = _place()
        me, sibling = (x, y, c), (x, y, 1 - c)
        chips = [(1 - x, y), (x, 1 - y), (1 - x, 1 - y)]

        def slot(px, py, pc):
            return out_ref.at[4 * px + 2 * py + pc]

        def copy(k, blk, to, src=None):
            return pltpu.make_async_remote_copy(
                src_ref=slot(*blk) if src is None else src, dst_ref=slot(*blk),
                send_sem=send_sems.at[k], recv_sem=recv_sems.at[k], device_id=to, device_id_type=MESH)

        mine = pltpu.make_async_copy(x_ref, slot(*me), local_sem)
        mine.start()
        first = [copy(0, me, sibling, src=x_ref)]
        first += [copy(1 + j, me, (*chip, c), src=x_ref) for j, chip in enumerate(chips)]
        for cp in first:
            cp.start()
        passed = [copy(4 + j, (*chip, c), sibling) for j, chip in enumerate(chips)]
        for j, chip in enumerate(chips):
            copy(1 + j, (*chip, c), me).wait_recv()
            passed[j].start()
        copy(0, sibling, me).wait_recv()
        for j, chip in enumerate(chips):
            copy(4 + j, (*chip, 1 - c), me).wait_recv()
        for cp in first + passed:
            cp.wait_send()
        mine.wait()

    return pl.pallas_call(
        body, name="all_gather",
        out_shape=jax.ShapeDtypeStruct((N_DEV, rows, cols), block.dtype),
        in_specs=[HBM], out_specs=HBM,
        scratch_shapes=[pltpu.SemaphoreType.DMA((7,)), pltpu.SemaphoreType.DMA((7,)), pltpu.SemaphoreType.DMA],
    )(block)


def _exchange(blocks, small):
    _, rows, cols = blocks.shape

    def body(blocks_ref, small_ref, got_ref, smalls_ref, send_sems, recv_sems, local_sems):
        x, y, c = _place()
        my_index = 4 * x + 2 * y + c
        own = pltpu.make_async_copy(blocks_ref.at[my_index], got_ref.at[0], local_sems.at[0])
        own_small = pltpu.make_async_copy(small_ref, smalls_ref.at[my_index], local_sems.at[1])
        own.start()
        own_small.start()
        copies = []
        for r in range(1, N_DEV):
            px = 1 - x if r & 4 else x
            py = 1 - y if r & 2 else y
            pc = 1 - c if r & 1 else c
            peer_index = 4 * px + 2 * py + pc
            copies.append(pltpu.make_async_remote_copy(
                src_ref=blocks_ref.at[peer_index], dst_ref=got_ref.at[r],
                send_sem=send_sems.at[r - 1], recv_sem=recv_sems.at[r - 1], device_id=(px, py, pc), device_id_type=MESH))
            copies.append(pltpu.make_async_remote_copy(
                src_ref=small_ref, dst_ref=smalls_ref.at[my_index],
                send_sem=send_sems.at[6 + r], recv_sem=recv_sems.at[6 + r], device_id=(px, py, pc), device_id_type=MESH))
        for cp in copies:
            cp.start()
        for cp in copies:
            cp.wait()
        own.wait()
        own_small.wait()

    return pl.pallas_call(
        body, name="grad_exchange",
        out_shape=[jax.ShapeDtypeStruct((N_DEV, rows, cols), blocks.dtype),
                   jax.ShapeDtypeStruct((N_DEV,) + small.shape, small.dtype)],
        in_specs=[HBM, HBM], out_specs=[HBM, HBM],
        scratch_shapes=[pltpu.SemaphoreType.DMA((14,)), pltpu.SemaphoreType.DMA((14,)), pltpu.SemaphoreType.DMA((2,))],
    )(blocks, small)


def _sum_blocks(got):
    n, rows, cols = got.shape
    tr = 256

    def body(got_ref, out_ref):
        acc = got_ref[0].astype(F32)
        for r in range(1, n):
            acc = acc + got_ref[r].astype(F32)
        out_ref[...] = acc

    return pl.pallas_call(
        body, grid=(rows // tr,), name="grad_sum",
        in_specs=[pl.BlockSpec((n, tr, cols), lambda i: (0, i, 0))],
        out_specs=pl.BlockSpec((tr, cols), lambda i: (i, 0)),
        out_shape=jax.ShapeDtypeStruct((rows, cols), F32),
        compiler_params=_params(("parallel",)),
    )(got)


def _adamw_math(w, g, m, v):
    m = ADAM_B1 * m + (1.0 - ADAM_B1) * g
    v = ADAM_B2 * v + (1.0 - ADAM_B2) * (g * g)
    m_hat = m / (1.0 - ADAM_B1 ** ADAM_STEP)
    v_hat = v / (1.0 - ADAM_B2 ** ADAM_STEP)
    delta = -ADAM_LR * (m_hat / (jnp.sqrt(v_hat) + ADAM_EPS) + ADAM_WD * w)
    return delta, m, v


def _adamw(w, g, m, v, name):
    rows, cols = w.shape
    tr = rows if rows <= 352 else 256

    def body(w_ref, g_ref, m_ref, v_ref, d_out, m_out, v_out):
        d_out[...], m_out[...], v_out[...] = _adamw_math(w_ref[...], g_ref[...], m_ref[...], v_ref[...])

    spec = pl.BlockSpec((tr, cols), lambda i: (i, 0))
    return pl.pallas_call(
        body, grid=(rows // tr,), name=name,
        in_specs=[spec] * 4, out_specs=[spec] * 3,
        out_shape=[jax.ShapeDtypeStruct((rows, cols), F32)] * 3,
        compiler_params=_params(("parallel",)),
    )(w, g, m, v)


ROW_ATTN_PRE, ROW_ATTN_POST, ROW_FFN_PRE, ROW_FFN_POST, ROW_OUT_NORMS, ROW_GAMMA, ROW_LOSS = range(7)


def _small_update(smalls, w, m, v):
    def body(s_ref, w_ref, m_ref, v_ref, g_out, d_out, m_out, v_out):
        g = s_ref[0]
        for j in range(1, N_DEV):
            g = g + s_ref[j]
        wv = w_ref[...]
        g0 = wv[ROW_GAMMA:ROW_GAMMA + 1, 0:HG_WIDTH]
        g1 = wv[ROW_GAMMA:ROW_GAMMA + 1, HG_WIDTH:]
        mx = jnp.maximum(g0, g1)
        e0 = jnp.exp(g0 - mx)
        e1 = jnp.exp(g1 - mx)
        lb = e0 / (e0 + e1)
        dg0 = g[ROW_GAMMA:ROW_GAMMA + 1, 0:HG_WIDTH] * lb * (1.0 - lb)
        row = lax.broadcasted_iota(jnp.int32, g.shape, 0)
        g = jnp.where(row == ROW_GAMMA, jnp.concatenate([dg0, -dg0], axis=1), g)
        g_out[...] = g
        d_out[...], m_out[...], v_out[...] = _adamw_math(wv, g, m_ref[...], v_ref[...])

    return pl.pallas_call(
        body, name="small_update",
        out_shape=[jax.ShapeDtypeStruct((8, D_MODEL), F32)] * 4,
    )(smalls, w, m, v)


def _to_heads(a):
    T = a.shape[0]
    return a.reshape(T, SB_HEADS, SB_DH).transpose(1, 0, 2)


def _from_heads(a):
    H, T, dh = a.shape
    return a.transpose(1, 0, 2).reshape(T, H * dh)


def _local_step(x, p, target, g_attn_pre, w_in, gamma, g_hg, g_sb, w_out, g_attn_post, g_ffn_pre, w_gu, w_down,
                g_ffn_post, w_pp, w_pg):
    proj, u = _in_proj(x, g_attn_pre, w_in)
    o_hg, hist = _hg_fwd(proj, gamma)
    q = _to_heads(proj[:, 4 * HG_WIDTH:4 * HG_WIDTH + SB_WIDTH]).astype(BF16)
    k = _to_heads(proj[:, 4 * HG_WIDTH + SB_WIDTH:4 * HG_WIDTH + 2 * SB_WIDTH]).astype(BF16)
    v = _to_heads(proj[:, 4 * HG_WIDTH + 2 * SB_WIDTH:]).astype(BF16)
    o_sb_heads, right = _sb_fwd(q, k, v)
    o_sb = _from_heads(o_sb_heads)
    cat, mix, h1 = _out_proj(o_hg, proj, o_sb, x, g_hg, g_sb, g_attn_post, w_out)
    u2, gate, up, y, h2 = _ffn_fwd(h1, g_ffn_pre, w_gu, w_down, g_ffn_post)
    dh2, de, dg, loss = _ple_loss(h2, p, target, w_pp, w_pg)

    d_wpp = _wgrad(p, de, "wgrad_ple_proj")
    d_wpg = _wgrad(h2, dg, "wgrad_ple_gate")
    dy, a, dgate, dup, dh1, dg_ffn_post, dg_ffn_pre = _ffn_bwd(dh2, y, h1, gate, up, g_ffn_post, g_ffn_pre, w_gu, w_down)
    d_wdown = _wgrad(a, dy, "wgrad_down")
    d_wgu = jnp.concatenate([_wgrad(u2, dgate, "wgrad_gate"), _wgrad(u2, dup, "wgrad_up")], axis=1)
    dmix, d_ohg, d_hg, d_osb, dg_attn_post, dg_hg, dg_sb = _out_bwd(dh1, mix, o_hg, proj, o_sb, g_hg, g_sb, g_attn_post,
                                                                    w_out)
    d_wout = _wgrad(cat, dmix, "wgrad_out")
    dq, dk, dv = _sb_bwd(q, k, v, right, _to_heads(d_osb))
    d_hgrn, d_lb = _hg_bwd(proj, gamma, d_ohg, hist)
    dproj, dx, dg_attn_pre = _in_bwd(d_hgrn, d_hg, _from_heads(dq), _from_heads(dk), _from_heads(dv), x, dh1,
                                     g_attn_pre, w_in)
    d_win = _wgrad(u, dproj, "wgrad_in")
    smalls = (dg_attn_pre, dg_attn_post, dg_ffn_pre, dg_ffn_post, dg_hg, dg_sb, d_lb)
    return loss, dx, (d_win, d_wout, d_wgu, d_wdown, d_wpp, d_wpg), smalls


def _cols_to_rows(a):
    rows, cols = a.shape
    n = cols // N_DEV
    return a.reshape(rows, N_DEV, n).transpose(1, 0, 2).reshape(N_DEV, rows * n // D_MODEL, D_MODEL)


def _rows_to_cols(a, rows):
    n = a.shape[1] * D_MODEL // rows
    return a.reshape(N_DEV, rows, n).transpose(1, 0, 2).reshape(rows, N_DEV * n)


def _pack_row(*parts):
    return jnp.concatenate([q.reshape(1, -1) for q in parts], axis=1)


def kernel(x, p, attn_pre_norm, w_in, hg_lower_gamma, hg_out_norm, sb_out_norm, w_out, attn_post_norm, ffn_pre_norm, w_gate_up, w_down, ffn_post_norm, ple_proj, ple_gate, loss_target, m_attn_pre_norm, m_w_in, m_hg_lower_gamma, m_hg_out_norm, m_sb_out_norm, m_w_out, m_attn_post_norm, m_ffn_pre_norm, m_w_gate_up, m_w_down, m_ffn_post_norm, m_ple_proj, m_ple_gate, v_attn_pre_norm, v_w_in, v_hg_lower_gamma, v_hg_out_norm, v_sb_out_norm, v_w_out, v_attn_post_norm, v_ffn_pre_norm, v_w_gate_up, v_w_down, v_ffn_post_norm, v_ple_proj, v_ple_gate):
    big = (w_in[0], w_out[0], w_gate_up[0], w_down[0], ple_proj[0], ple_gate[0])
    big_m = (m_w_in[0], m_w_out[0], m_w_gate_up[0], m_w_down[0], m_ple_proj[0], m_ple_gate[0])
    big_v = (v_w_in[0], v_w_out[0], v_w_gate_up[0], v_w_down[0], v_ple_proj[0], v_ple_gate[0])
    names = ("w_in", "w_out", "w_gate_up", "w_down", "ple_proj", "ple_gate")

    wpack = jnp.concatenate([w.astype(BF16).reshape(-1, D_MODEL) for w in big], axis=0)
    gathered = _all_gather(wpack)
    part = lambda i: gathered[:, PACK_OFFS[i]:PACK_OFFS[i] + PACK_SIZES[i], :]
    full_w_in = _rows_to_cols(part(0), D_MODEL)
    full_w_out = part(1).reshape(D_MODEL, D_MODEL)
    full_w_gu = _rows_to_cols(part(2), D_MODEL)
    full_w_down = part(3).reshape(D_FF, D_MODEL)
    full_w_pp = _rows_to_cols(part(4), PLE_DIM)
    full_w_pg = part(5).reshape(D_MODEL, D_MODEL)

    loss, dx, grads, smalls = _local_step(
        x[0], p[0, 0], loss_target[0], attn_pre_norm, full_w_in, hg_lower_gamma, hg_out_norm, sb_out_norm, full_w_out,
        attn_post_norm, ffn_pre_norm, full_w_gu, full_w_down, ffn_post_norm, full_w_pp, full_w_pg)
    d_win, d_wout, d_wgu, d_wdown, d_wpp, d_wpg = grads
    dg_attn_pre, dg_attn_post, dg_ffn_pre, dg_ffn_post, dg_hg, dg_sb, d_lb = smalls

    gpack = jnp.concatenate([
        _cols_to_rows(d_win), d_wout.reshape(N_DEV, -1, D_MODEL), _cols_to_rows(d_wgu),
        d_wdown.reshape(N_DEV, -1, D_MODEL), _cols_to_rows(d_wpp), d_wpg.reshape(N_DEV, -1, D_MODEL)], axis=1)
    zeros_half = jnp.zeros((1, HG_WIDTH), F32)
    small_pack = jnp.concatenate([
        dg_attn_pre, dg_attn_post, dg_ffn_pre, dg_ffn_post, _pack_row(dg_hg, dg_sb), _pack_row(d_lb, zeros_half),
        jnp.broadcast_to(loss[0:1, 0:1], (1, D_MODEL)), jnp.zeros((1, D_MODEL), F32)], axis=0)
    got, all_smalls = _exchange(gpack, small_pack)
    gsum = _sum_blocks(got)

    out_g, out_d, out_m, out_v = {}, {}, {}, {}
    for i, name in enumerate(names):
        g = gsum[PACK_OFFS[i]:PACK_OFFS[i] + PACK_SIZES[i]].reshape(big[i].shape)
        out_g[name] = g[None]
        d, m, v = _adamw(big[i], g, big_m[i], big_v[i], "adamw_" + name)
        out_d[name], out_m[name], out_v[name] = d[None], m[None], v[None]

    def small_rows(pre, gam, hg, sb, post, fpre, fpost):
        return jnp.concatenate([pre, post, fpre, fpost, _pack_row(hg, sb), _pack_row(gam[0], gam[1]),
                                jnp.zeros((2, D_MODEL), F32)], axis=0)

    packs = _small_update(
        all_smalls,
        small_rows(attn_pre_norm, hg_lower_gamma, hg_out_norm, sb_out_norm, attn_post_norm, ffn_pre_norm, ffn_post_norm),
        small_rows(m_attn_pre_norm, m_hg_lower_gamma, m_hg_out_norm, m_sb_out_norm, m_attn_post_norm, m_ffn_pre_norm,
                   m_ffn_post_norm),
        small_rows(v_attn_pre_norm, v_hg_lower_gamma, v_hg_out_norm, v_sb_out_norm, v_attn_post_norm, v_ffn_pre_norm,
                   v_ffn_post_norm))

    def unpack(pk):
        return {
            "attn_pre_norm": pk[ROW_ATTN_PRE:ROW_ATTN_PRE + 1],
            "hg_lower_gamma": pk[ROW_GAMMA].reshape(2, HG_WIDTH),
            "hg_out_norm": pk[ROW_OUT_NORMS:ROW_OUT_NORMS + 1, :HG_WIDTH],
            "sb_out_norm": pk[ROW_OUT_NORMS:ROW_OUT_NORMS + 1, HG_WIDTH:],
            "attn_post_norm": pk[ROW_ATTN_POST:ROW_ATTN_POST + 1],
            "ffn_pre_norm": pk[ROW_FFN_PRE:ROW_FFN_PRE + 1],
            "ffn_post_norm": pk[ROW_FFN_POST:ROW_FFN_POST + 1],
        }

    sg, sd, sm, sv = (unpack(pk) for pk in packs)
    out_g.update(sg), out_d.update(sd), out_m.update(sm), out_v.update(sv)
    order = ("attn_pre_norm", "w_in", "hg_lower_gamma", "hg_out_norm", "sb_out_norm", "w_out", "attn_post_norm",
             "ffn_pre_norm", "w_gate_up", "w_down", "ffn_post_norm", "ple_proj", "ple_gate")
    total_loss = packs[0][ROW_LOSS, 0]
    return (total_loss, dx[None], *[out_g[n] for n in order], *[out_d[n] for n in order],
            *[out_m[n] for n in order], *[out_v[n] for n in order])
```

```python
import functools

import jax
import jax.numpy as jnp
from jax import lax
from jax.experimental import pallas as pl
from jax.experimental.pallas import tpu as pltpu

F32 = jnp.float32
BF16 = jnp.bfloat16

D_MODEL = 1024
HG_WIDTH = 512
HG_HEADS = 4
HG_DK = 128
SB_WIDTH = 512
SB_HEADS = 8
SB_DH = 64
SB_BLOCK = 128
SB_PP = 2
SB_TQ = 512
D_FF = 2816
PLE_DIM = 256
IN_COLS = 4 * HG_WIDTH + 3 * SB_WIDTH
EPS = 1e-6
N_DEV = 8

HG_SUB = 16
HG_TILE = 128
FFN_TF = 256
ROW_TILE = 256
VMEM_LIMIT = 56 * 1024 * 1024

ADAM_LR = 0.001
ADAM_B1 = 0.9
ADAM_B2 = 0.999
ADAM_EPS = 1e-08
ADAM_WD = 0.01
ADAM_STEP = 10

PACK_SIZES = (448, 128, 704, 352, 32, 128)
PACK_ROWS = sum(PACK_SIZES)
PACK_OFFS = tuple(sum(PACK_SIZES[:i]) for i in range(len(PACK_SIZES)))

NT_DIMS = (((1,), (1,)), ((), ()))
TN_DIMS = (((0,), (0,)), ((), ()))


def _params(semantics):
    return pltpu.CompilerParams(dimension_semantics=semantics, vmem_limit_bytes=VMEM_LIMIT)


def _dot(a, b):
    return jnp.dot(a, b, preferred_element_type=F32)


def _dot_nt(a, b):
    return lax.dot_general(a, b, NT_DIMS, preferred_element_type=F32)


def _dot_tn(a, b):
    return lax.dot_general(a, b, TN_DIMS, preferred_element_type=F32)


def _sigmoid(z):
    return 1.0 / (1.0 + jnp.exp(-z))


def _rstd(xv):
    return lax.rsqrt(jnp.mean(xv * xv, axis=-1, keepdims=True) + EPS)


def _rms_bwd(xv, r, g, dn):
    xn = xv * r
    gh = dn * g
    dx = r * (gh - xn * jnp.mean(gh * xn, axis=-1, keepdims=True))
    return dx, dn * xn


def _colsum(a):
    return jnp.sum(a, axis=0, keepdims=True)


def _row_spec(tm, width, col=0):
    return pl.BlockSpec((tm, width), lambda i, col=col: (i, col))


def _full_spec(shape):
    return pl.BlockSpec(shape, lambda i: (0,) * len(shape))


def _in_proj(x, g, w):
    T = x.shape[0]
    tm = ROW_TILE

    def body(x_ref, g_ref, w_ref, proj_ref, u_ref, qkv_ref):
        xv = x_ref[...]
        u = (xv * _rstd(xv) * g_ref[...]).astype(BF16)
        u_ref[...] = u
        proj = _dot(u, w_ref[...])
        proj_ref[...] = proj
        qkv_ref[...] = proj[:, 4 * HG_WIDTH:].astype(BF16)

    return pl.pallas_call(
        body, grid=(T // tm,), name="in_proj",
        in_specs=[_row_spec(tm, D_MODEL), _full_spec((1, D_MODEL)), _full_spec((D_MODEL, IN_COLS))],
        out_specs=[_row_spec(tm, IN_COLS), _row_spec(tm, D_MODEL), _row_spec(tm, 3 * SB_WIDTH)],
        out_shape=[jax.ShapeDtypeStruct((T, IN_COLS), F32), jax.ShapeDtypeStruct((T, D_MODEL), BF16),
                   jax.ShapeDtypeStruct((T, 3 * SB_WIDTH), BF16)],
        compiler_params=_params(("parallel",)),
    )(x, g, w)


def _out_proj(o_hg, proj, o_sb, x, g_hg, g_sb, g_post, w):
    T = x.shape[0]
    tm = ROW_TILE

    def body(ohg_ref, hg_ref, osb_ref, x_ref, ghg_ref, gsb_ref, gpost_ref, w_ref, cat_ref, mix_ref, h1_ref):
        ohg = ohg_ref[...]
        hg = hg_ref[...]
        osb = osb_ref[...]
        a = ohg * _rstd(ohg) * ghg_ref[...] * (hg * _sigmoid(hg))
        n2 = osb * _rstd(osb) * gsb_ref[...]
        cat = jnp.concatenate([a, n2], axis=1).astype(BF16)
        cat_ref[...] = cat
        mix = _dot(cat, w_ref[...])
        mix_ref[...] = mix
        h1_ref[...] = x_ref[...] + mix * _rstd(mix) * gpost_ref[...]

    return pl.pallas_call(
        body, grid=(T // tm,), name="out_proj",
        in_specs=[_row_spec(tm, HG_WIDTH), _row_spec(tm, HG_WIDTH, 3), _row_spec(tm, SB_WIDTH), _row_spec(tm, D_MODEL),
                  _full_spec((1, HG_WIDTH)), _full_spec((1, SB_WIDTH)), _full_spec((1, D_MODEL)),
                  _full_spec((D_MODEL, D_MODEL))],
        out_specs=[_row_spec(tm, D_MODEL)] * 3,
        out_shape=[jax.ShapeDtypeStruct((T, D_MODEL), BF16), jax.ShapeDtypeStruct((T, D_MODEL), F32),
                   jax.ShapeDtypeStruct((T, D_MODEL), F32)],
        compiler_params=_params(("parallel",)),
    )(o_hg, proj, o_sb, x, g_hg, g_sb, g_post, w)


def _ffn_fwd(h1, g_pre, w_gu, w_down, g_post):
    T = h1.shape[0]
    tm = 2 * ROW_TILE
    n_f = D_FF // FFN_TF

    def body(h1_ref, gpre_ref, wg_ref, wu_ref, wd_ref, gpost_ref, u2_ref, gate_ref, up_ref, y_ref, h2_ref, acc_ref):
        j = pl.program_id(1)

        @pl.when(j == 0)
        def _():
            hv = h1_ref[...]
            u2_ref[...] = (hv * _rstd(hv) * gpre_ref[...]).astype(BF16)
            acc_ref[...] = jnp.zeros_like(acc_ref)

        u2 = u2_ref[...]
        gate = _dot(u2, wg_ref[...])
        up = _dot(u2, wu_ref[...])
        gate_ref[...] = gate
        up_ref[...] = up
        a = (gate * _sigmoid(gate) * up).astype(BF16)
        acc_ref[...] += _dot(a, wd_ref[...])

        @pl.when(j == n_f - 1)
        def _():
            y = acc_ref[...]
            y_ref[...] = y
            h2_ref[...] = h1_ref[...] + y * _rstd(y) * gpost_ref[...]

    row = lambda width: pl.BlockSpec((tm, width), lambda i, j: (i, 0))
    vec = pl.BlockSpec((1, D_MODEL), lambda i, j: (0, 0))
    return pl.pallas_call(
        body, grid=(T // tm, n_f), name="ffn_fwd",
        in_specs=[row(D_MODEL), vec,
                  pl.BlockSpec((D_MODEL, FFN_TF), lambda i, j: (0, j)),
                  pl.BlockSpec((D_MODEL, FFN_TF), lambda i, j: (0, j + n_f)),
                  pl.BlockSpec((FFN_TF, D_MODEL), lambda i, j: (j, 0)), vec],
        out_specs=[row(D_MODEL),
                   pl.BlockSpec((tm, FFN_TF), lambda i, j: (i, j)),
                   pl.BlockSpec((tm, FFN_TF), lambda i, j: (i, j)),
                   row(D_MODEL), row(D_MODEL)],
        out_shape=[jax.ShapeDtypeStruct((T, D_MODEL), BF16), jax.ShapeDtypeStruct((T, D_FF), F32),
                   jax.ShapeDtypeStruct((T, D_FF), F32), jax.ShapeDtypeStruct((T, D_MODEL), F32),
                   jax.ShapeDtypeStruct((T, D_MODEL), F32)],
        scratch_shapes=[pltpu.VMEM((tm, D_MODEL), F32)],
        compiler_params=_params(("parallel", "arbitrary")),
    )(h1, g_pre, w_gu, w_gu, w_down, g_post)


def _ple_loss(h2, p, target, w_pp, w_pg):
    T = h2.shape[0]
    tm = ROW_TILE

    def body(h2_ref, p_ref, t_ref, wpp_ref, wpg_ref, dh2_ref, de_ref, dg_ref, loss_ref):
        i = pl.program_id(0)
        h2v = h2_ref[...]
        e = _dot(p_ref[...].astype(BF16), wpp_ref[...])
        sg = _sigmoid(_dot(h2v.astype(BF16), wpg_ref[...]))
        diff = h2v + e * sg - t_ref[...]
        part = jnp.sum(jnp.sum(diff * diff, axis=1, keepdims=True), axis=0, keepdims=True) * (0.5 / D_MODEL)

        @pl.when(i == 0)
        def _():
            loss_ref[...] = jnp.zeros_like(loss_ref)

        loss_ref[...] += jnp.broadcast_to(part, loss_ref.shape)
        dh3 = diff * (1.0 / D_MODEL)
        de_ref[...] = (dh3 * sg).astype(BF16)
        dg = (dh3 * e * sg * (1.0 - sg)).astype(BF16)
        dg_ref[...] = dg
        dh2_ref[...] = dh3 + _dot_nt(dg, wpg_ref[...])

    return pl.pallas_call(
        body, grid=(T // tm,), name="ple_loss",
        in_specs=[_row_spec(tm, D_MODEL), _row_spec(tm, PLE_DIM), _row_spec(tm, D_MODEL),
                  _full_spec((PLE_DIM, D_MODEL)), _full_spec((D_MODEL, D_MODEL))],
        out_specs=[_row_spec(tm, D_MODEL)] * 3 + [_full_spec((8, 128))],
        out_shape=[jax.ShapeDtypeStruct((T, D_MODEL), F32), jax.ShapeDtypeStruct((T, D_MODEL), BF16),
                   jax.ShapeDtypeStruct((T, D_MODEL), BF16), jax.ShapeDtypeStruct((8, 128), F32)],
        compiler_params=_params(("arbitrary",)),
    )(h2, p, target, w_pp, w_pg)


def _ffn_bwd(dh2, y, h1, gate, up, g_post, g_pre, w_gu, w_down):
    T = h1.shape[0]
    tm = 2 * ROW_TILE
    n_f = D_FF // FFN_TF

    def body(dh2_ref, y_ref, h1_ref, gate_ref, up_ref, gpost_ref, gpre_ref, wg_ref, wu_ref, wd_ref,
             dy_ref, a_ref, dgate_ref, dup_ref, dh1_ref, dgpost_ref, dgpre_ref, du2_ref):
        i = pl.program_id(0)
        j = pl.program_id(1)

        @pl.when((i == 0) & (j == 0))
        def _():
            dgpost_ref[...] = jnp.zeros_like(dgpost_ref)
            dgpre_ref[...] = jnp.zeros_like(dgpre_ref)

        @pl.when(j == 0)
        def _():
            yv = y_ref[...]
            dy, gterm = _rms_bwd(yv, _rstd(yv), gpost_ref[...], dh2_ref[...])
            dy_ref[...] = dy.astype(BF16)
            dgpost_ref[...] += _colsum(gterm)
            du2_ref[...] = jnp.zeros_like(du2_ref)

        da = _dot_nt(dy_ref[...], wd_ref[...])
        gate = gate_ref[...]
        up = up_ref[...]
        s = _sigmoid(gate)
        sil = gate * s
        a_ref[...] = (sil * up).astype(BF16)
        dgate = (da * up * (s * (1.0 + gate * (1.0 - s)))).astype(BF16)
        dup = (da * sil).astype(BF16)
        dgate_ref[...] = dgate
        dup_ref[...] = dup
        du2_ref[...] += _dot_nt(dgate, wg_ref[...]) + _dot_nt(dup, wu_ref[...])

        @pl.when(j == n_f - 1)
        def _():
            hv = h1_ref[...]
            dx, gterm = _rms_bwd(hv, _rstd(hv), gpre_ref[...], du2_ref[...])
            dh1_ref[...] = dh2_ref[...] + dx
            dgpre_ref[...] += _colsum(gterm)

    row = lambda width: pl.BlockSpec((tm, width), lambda i, j: (i, 0))
    col = pl.BlockSpec((tm, FFN_TF), lambda i, j: (i, j))
    vec = pl.BlockSpec((1, D_MODEL), lambda i, j: (0, 0))
    return pl.pallas_call(
        body, grid=(T // tm, n_f), name="ffn_bwd",
        in_specs=[row(D_MODEL), row(D_MODEL), row(D_MODEL), col, col, vec, vec,
                  pl.BlockSpec((D_MODEL, FFN_TF), lambda i, j: (0, j)),
                  pl.BlockSpec((D_MODEL, FFN_TF), lambda i, j: (0, j + n_f)),
                  pl.BlockSpec((FFN_TF, D_MODEL), lambda i, j: (j, 0))],
        out_specs=[row(D_MODEL), col, col, col, row(D_MODEL), vec, vec],
        out_shape=[jax.ShapeDtypeStruct((T, D_MODEL), BF16), jax.ShapeDtypeStruct((T, D_FF), BF16),
                   jax.ShapeDtypeStruct((T, D_FF), BF16), jax.ShapeDtypeStruct((T, D_FF), BF16),
                   jax.ShapeDtypeStruct((T, D_MODEL), F32), jax.ShapeDtypeStruct((1, D_MODEL), F32),
                   jax.ShapeDtypeStruct((1, D_MODEL), F32)],
        scratch_shapes=[pltpu.VMEM((tm, D_MODEL), F32)],
        compiler_params=_params(("arbitrary", "arbitrary")),
    )(dh2, y, h1, gate, up, g_post, g_pre, w_gu, w_gu, w_down)


def _out_bwd(dh1, mix, o_hg, proj, o_sb, g_hg, g_sb, g_post, w):
    T = dh1.shape[0]
    tm = ROW_TILE

    def body(dh1_ref, mix_ref, ohg_ref, hg_ref, osb_ref, ghg_ref, gsb_ref, gpost_ref, w_ref,
             dmix_ref, dohg_ref, dhg_ref, dosb_ref, dgpost_ref, dghg_ref, dgsb_ref):
        i = pl.program_id(0)

        @pl.when(i == 0)
        def _():
            dgpost_ref[...] = jnp.zeros_like(dgpost_ref)
            dghg_ref[...] = jnp.zeros_like(dghg_ref)
            dgsb_ref[...] = jnp.zeros_like(dgsb_ref)

        mix = mix_ref[...]
        dmix, gterm = _rms_bwd(mix, _rstd(mix), gpost_ref[...], dh1_ref[...])
        dgpost_ref[...] += _colsum(gterm)
        dmix = dmix.astype(BF16)
        dmix_ref[...] = dmix
        dcat = _dot_nt(dmix, w_ref[...])
        da = dcat[:, :HG_WIDTH]
        dn2 = dcat[:, HG_WIDTH:]
        ohg = ohg_ref[...]
        r1 = _rstd(ohg)
        hg = hg_ref[...]
        s = _sigmoid(hg)
        dhg_ref[...] = da * (ohg * r1 * ghg_ref[...]) * (s * (1.0 + hg * (1.0 - s)))
        dohg, gterm = _rms_bwd(ohg, r1, ghg_ref[...], da * (hg * s))
        dohg_ref[...] = dohg
        dghg_ref[...] += _colsum(gterm)
        osb = osb_ref[...]
        dosb, gterm = _rms_bwd(osb, _rstd(osb), gsb_ref[...], dn2)
        dosb_ref[...] = dosb
        dgsb_ref[...] += _colsum(gterm)

    return pl.pallas_call(
        body, grid=(T // tm,), name="out_bwd",
        in_specs=[_row_spec(tm, D_MODEL), _row_spec(tm, D_MODEL), _row_spec(tm, HG_WIDTH), _row_spec(tm, HG_WIDTH, 3),
                  _row_spec(tm, SB_WIDTH), _full_spec((1, HG_WIDTH)), _full_spec((1, SB_WIDTH)),
                  _full_spec((1, D_MODEL)), _full_spec((D_MODEL, D_MODEL))],
        out_specs=[_row_spec(tm, D_MODEL), _row_spec(tm, HG_WIDTH), _row_spec(tm, HG_WIDTH), _row_spec(tm, SB_WIDTH),
                   _full_spec((1, D_MODEL)), _full_spec((1, HG_WIDTH)), _full_spec((1, SB_WIDTH))],
        out_shape=[jax.ShapeDtypeStruct((T, D_MODEL), BF16), jax.ShapeDtypeStruct((T, HG_WIDTH), F32),
                   jax.ShapeDtypeStruct((T, HG_WIDTH), F32), jax.ShapeDtypeStruct((T, SB_WIDTH), F32),
                   jax.ShapeDtypeStruct((1, D_MODEL), F32), jax.ShapeDtypeStruct((1, HG_WIDTH), F32),
                   jax.ShapeDtypeStruct((1, SB_WIDTH), F32)],
        compiler_params=_params(("arbitrary",)),
    )(dh1, mix, o_hg, proj, o_sb, g_hg, g_sb, g_post, w)


def _in_bwd(d_hgrn, d_hg, d_sq, d_sk, d_sv, x, dh1, g_pre, w):
    T = x.shape[0]
    tm = ROW_TILE

    def body(dh_ref, dhg_ref, dsq_ref, dsk_ref, dsv_ref, x_ref, dh1_ref, gpre_ref, w_ref, dproj_ref, dx_ref, dgpre_ref):
        i = pl.program_id(0)

        @pl.when(i == 0)
        def _():
            dgpre_ref[...] = jnp.zeros_like(dgpre_ref)

        dp = jnp.concatenate([dh_ref[...], dhg_ref[...], dsq_ref[...], dsk_ref[...], dsv_ref[...]], axis=1).astype(BF16)
        dproj_ref[...] = dp
        du = _dot_nt(dp, w_ref[...])
        xv = x_ref[...]
        dx, gterm = _rms_bwd(xv, _rstd(xv), gpre_ref[...], du)
        dx_ref[...] = dh1_ref[...] + dx
        dgpre_ref[...] += _colsum(gterm)

    return pl.pallas_call(
        body, grid=(T // tm,), name="in_bwd",
        in_specs=[_row_spec(tm, 3 * HG_WIDTH), _row_spec(tm, HG_WIDTH), _row_spec(tm, SB_WIDTH), _row_spec(tm, SB_WIDTH),
                  _row_spec(tm, SB_WIDTH), _row_spec(tm, D_MODEL), _row_spec(tm, D_MODEL), _full_spec((1, D_MODEL)),
                  _full_spec((D_MODEL, IN_COLS))],
        out_specs=[_row_spec(tm, IN_COLS), _row_spec(tm, D_MODEL), _full_spec((1, D_MODEL))],
        out_shape=[jax.ShapeDtypeStruct((T, IN_COLS), BF16), jax.ShapeDtypeStruct((T, D_MODEL), F32),
                   jax.ShapeDtypeStruct((1, D_MODEL), F32)],
        compiler_params=_params(("arbitrary",)),
    )(d_hgrn, d_hg, d_sq, d_sk, d_sv, x, dh1, g_pre, w)


def _wgrad(a, b, name):
    T, K = a.shape
    N = b.shape[1]
    tk = next(t for t in (512, 256, 128) if K % t == 0)
    tn = next(t for t in (512, 256, 128) if N % t == 0)
    tt = 512
    n_t = T // tt

    def body(a_ref, b_ref, o_ref, acc_ref):
        t = pl.program_id(2)

        @pl.when(t == 0)
        def _():
            acc_ref[...] = jnp.zeros_like(acc_ref)

        acc_ref[...] += _dot_tn(a_ref[...].astype(BF16), b_ref[...].astype(BF16))

        @pl.when(t == n_t - 1)
        def _():
            o_ref[...] = acc_ref[...].astype(BF16)

    return pl.pallas_call(
        body, grid=(K // tk, N // tn, n_t), name=name,
        in_specs=[pl.BlockSpec((tt, tk), lambda k, n, t: (t, k)), pl.BlockSpec((tt, tn), lambda k, n, t: (t, n))],
        out_specs=pl.BlockSpec((tk, tn), lambda k, n, t: (k, n)),
        out_shape=jax.ShapeDtypeStruct((K, N), BF16),
        scratch_shapes=[pltpu.VMEM((tk, tn), F32)],
        compiler_params=_params(("parallel", "parallel", "arbitrary")),
    )(a, b)


def _hg_gates(hq, hf, gam):
    g0 = gam[0:1, :]
    g1 = gam[1:2, :]
    mx = jnp.maximum(g0, g1)
    e0 = jnp.exp(g0 - mx)
    e1 = jnp.exp(g1 - mx)
    lb = e0 / (e0 + e1)
    s = _sigmoid(hf)
    f = lb + (1.0 - lb) * s
    sq = _sigmoid(hq)
    return hq * sq, sq, s, f, (1.0 - lb) * (1.0 - s), jnp.log(f), lb


def _prefix_in_sub(a, rowmod):
    n = a.shape[0]
    sh = 1
    while sh < HG_SUB:
        a = a + jnp.where(rowmod >= sh, pltpu.roll(a, sh, axis=0), 0.0)
        sh *= 2
    return a


def _suffix_in_sub(a, rowmod):
    n = a.shape[0]
    sh = 1
    while sh < HG_SUB:
        a = a + jnp.where(rowmod < HG_SUB - sh, pltpu.roll(a, n - sh, axis=0), 0.0)
        sh *= 2
    return a


def _hg_fwd(proj, gam):
    T = proj.shape[0]
    tb = HG_TILE
    n_sub = tb // HG_SUB

    def body(hq_ref, hf_ref, hi_ref, gam_ref, o_ref, hist_ref, st_ref, qt_ref, kk_ref, b_ref, od_ref):
        i = pl.program_id(0)

        @pl.when(i == 0)
        def _():
            st_ref[...] = jnp.zeros_like(st_ref)

        rowmod_w = lax.broadcasted_iota(jnp.int32, (tb, HG_WIDTH), 0) % HG_SUB
        rowmod = lax.broadcasted_iota(jnp.int32, (tb, 1), 0) % HG_SUB
        q, _, _, _, kk, lf, _ = _hg_gates(hq_ref[...], hf_ref[...], gam_ref[...])
        b = _prefix_in_sub(lf, rowmod_w)
        qt_ref[...] = (q * jnp.exp(b)).astype(BF16)
        kk_ref[...] = kk
        b_ref[...] = b

        for h in range(HG_HEADS):
            cs = slice(h * HG_DK, (h + 1) * HG_DK)
            qh = q[:, cs]
            kh = kk[:, cs]
            bh = b[:, cs]
            vh = hi_ref[:, cs]
            acc = jnp.sum(qh * kh, axis=1, keepdims=True) * vh
            for d in range(1, HG_SUB):
                e = qh * pltpu.roll(kh, d, axis=0) * jnp.exp(bh - pltpu.roll(bh, d, axis=0))
                sc = jnp.where(rowmod >= d, jnp.sum(e, axis=1, keepdims=True), 0.0)
                acc = acc + sc * pltpu.roll(vh, d, axis=0)
            od_ref[:, cs] = acc

        for m in range(n_sub):
            rs = slice(m * HG_SUB, (m + 1) * HG_SUB)
            for h in range(HG_HEADS):
                cs = slice(h * HG_DK, (h + 1) * HG_DK)
                st = st_ref[h]
                hist_ref[m, h] = st
                o_int = _dot_nt(qt_ref[rs, cs], st.astype(BF16))
                bs = b_ref[rs, cs]
                bl = bs[HG_SUB - 1:HG_SUB, :]
                kt = (kk_ref[rs, cs] * jnp.exp(bl - bs)).astype(BF16)
                st_ref[h] = st * jnp.exp(bl) + _dot_tn(hi_ref[rs, cs].astype(BF16), kt)
                o_ref[rs, cs] = o_int + od_ref[rs, cs]

    col = lambda c: pl.BlockSpec((tb, HG_WIDTH), lambda i, c=c: (i, c))
    return pl.pallas_call(
        body, grid=(T // tb,), name="hg_fwd",
        in_specs=[col(0), col(1), col(2), _full_spec((2, HG_WIDTH))],
        out_specs=[pl.BlockSpec((tb, HG_WIDTH), lambda i: (i, 0)),
                   pl.BlockSpec((n_sub, HG_HEADS, HG_DK, HG_DK), lambda i: (i, 0, 0, 0))],
        out_shape=[jax.ShapeDtypeStruct((T, HG_WIDTH), F32),
                   jax.ShapeDtypeStruct((T // HG_SUB, HG_HEADS, HG_DK, HG_DK), F32)],
        scratch_shapes=[pltpu.VMEM((HG_HEADS, HG_DK, HG_DK), F32), pltpu.VMEM((tb, HG_WIDTH), BF16),
                        pltpu.VMEM((tb, HG_WIDTH), F32), pltpu.VMEM((tb, HG_WIDTH), F32),
                        pltpu.VMEM((tb, HG_WIDTH), F32)],
        compiler_params=_params(("arbitrary",)),
    )(proj, proj, proj, gam)


def _hg_bwd(proj, gam, d_o, hist):
    T = proj.shape[0]
    tb = HG_TILE
    n_sub = tb // HG_SUB
    n_t = T // tb

    def body(hq_ref, hf_ref, hi_ref, gam_ref, do_ref, hist_ref, dout_ref, dlb_ref,
             dst_ref, q_ref, kk_ref, b_ref, dq_ref, dkk_ref, db_ref, dv_ref):
        i = pl.program_id(0)

        @pl.when(i == 0)
        def _():
            dst_ref[...] = jnp.zeros_like(dst_ref)
            dlb_ref[...] = jnp.zeros_like(dlb_ref)

        rowmod_w = lax.broadcasted_iota(jnp.int32, (tb, HG_WIDTH), 0) % HG_SUB
        rowmod = lax.broadcasted_iota(jnp.int32, (tb, 1), 0) % HG_SUB
        last_row = lax.broadcasted_iota(jnp.int32, (HG_SUB, 1), 0) == HG_SUB - 1
        hq = hq_ref[...]
        q, sq, s, f, kk, lf, lb = _hg_gates(hq, hf_ref[...], gam_ref[...])
        b = _prefix_in_sub(lf, rowmod_w)
        q_ref[...] = q
        kk_ref[...] = kk
        b_ref[...] = b

        for m in reversed(range(n_sub)):
            rs = slice(m * HG_SUB, (m + 1) * HG_SUB)
            for h in range(HG_HEADS):
                cs = slice(h * HG_DK, (h + 1) * HG_DK)
                st = hist_ref[m, h]
                dst = dst_ref[h]
                dstb = dst.astype(BF16)
                dos = do_ref[rs, cs].astype(BF16)
                vs = hi_ref[rs, cs].astype(BF16)
                bs = b_ref[rs, cs]
                bl = bs[HG_SUB - 1:HG_SUB, :]
                ebl = jnp.exp(bl)
                qt = q_ref[rs, cs] * jnp.exp(bs)
                decay = jnp.exp(bl - bs)
                kt = kk_ref[rs, cs] * decay
                dqt = _dot(dos, st.astype(BF16))
                dkt = _dot(vs, dstb)
                dv_ref[rs, cs] = _dot_nt(kt.astype(BF16), dstb)
                dst_ref[h] = dst * ebl + _dot_tn(dos, qt.astype(BF16))
                dq_ref[rs, cs] = dqt * jnp.exp(bs)
                dkk_ref[rs, cs] = dkt * decay
                dktk = dkt * kt
                dbl = _colsum(dktk) + _colsum(dst * st) * ebl
                db_ref[rs, cs] = dqt * qt - dktk + jnp.where(last_row, dbl, 0.0)

        for h in range(HG_HEADS):
            cs = slice(h * HG_DK, (h + 1) * HG_DK)
            qh = q[:, cs]
            kh = kk[:, cs]
            bh = b[:, cs]
            vh = hi_ref[:, cs]
            doh = do_ref[:, cs]
            dsc = jnp.sum(doh * vh, axis=1, keepdims=True)
            sc = jnp.sum(qh * kh, axis=1, keepdims=True)
            dqh = dsc * kh
            dkh = dsc * qh
            dvh = sc * doh
            dbh = jnp.zeros_like(qh)
            for d in range(1, HG_SUB):
                back = tb - d
                valid = rowmod >= d
                kd = pltpu.roll(kh, d, axis=0)
                ex = jnp.where(valid, jnp.exp(bh - pltpu.roll(bh, d, axis=0)), 0.0)
                pd = kd * ex
                sc = jnp.sum(qh * pd, axis=1, keepdims=True)
                dsc = jnp.where(valid, jnp.sum(doh * pltpu.roll(vh, d, axis=0), axis=1, keepdims=True), 0.0)
                dqh = dqh + dsc * pd
                w = dsc * qh * pd
                dkh = dkh + pltpu.roll(dsc * qh * ex, back, axis=0)
                dvh = dvh + pltpu.roll(sc * doh, back, axis=0)
                dbh = dbh + w - pltpu.roll(w, back, axis=0)
            dq_ref[:, cs] += dqh
            dkk_ref[:, cs] += dkh
            dv_ref[:, cs] += dvh
            db_ref[:, cs] += dbh

        dlf = _suffix_in_sub(db_ref[...], rowmod_w)
        df = dlf / f - dkk_ref[...]
        dout_ref[:, 0:HG_WIDTH] = dq_ref[...] * (sq * (1.0 + hq * (1.0 - sq)))
        dout_ref[:, HG_WIDTH:2 * HG_WIDTH] = df * (1.0 - lb) * (s * (1.0 - s))
        dout_ref[:, 2 * HG_WIDTH:3 * HG_WIDTH] = dv_ref[...]
        dlb_ref[...] += _colsum(df * (1.0 - s))

    col = lambda c: pl.BlockSpec((tb, HG_WIDTH), lambda i, c=c: (n_t - 1 - i, c))
    return pl.pallas_call(
        body, grid=(n_t,), name="hg_bwd",
        in_specs=[col(0), col(1), col(2), _full_spec((2, HG_WIDTH)),
                  pl.BlockSpec((tb, HG_WIDTH), lambda i: (n_t - 1 - i, 0)),
                  pl.BlockSpec((n_sub, HG_HEADS, HG_DK, HG_DK), lambda i: (n_t - 1 - i, 0, 0, 0))],
        out_specs=[pl.BlockSpec((tb, 3 * HG_WIDTH), lambda i: (n_t - 1 - i, 0)), _full_spec((1, HG_WIDTH))],
        out_shape=[jax.ShapeDtypeStruct((T, 3 * HG_WIDTH), F32), jax.ShapeDtypeStruct((1, HG_WIDTH), F32)],
        scratch_shapes=[pltpu.VMEM((HG_HEADS, HG_DK, HG_DK), F32)] + [pltpu.VMEM((tb, HG_WIDTH), F32)] * 7,
        compiler_params=_params(("arbitrary",)),
    )(proj, proj, proj, gam, d_o, hist)


def _sb_masks():
    row = lax.broadcasted_iota(jnp.int32, (SB_BLOCK, SB_BLOCK), 0)
    col = lax.broadcasted_iota(jnp.int32, (SB_BLOCK, SB_BLOCK), 1)
    suffix = (row >= col).astype(BF16)
    prefix = (row <= col).astype(BF16)
    query = lax.broadcasted_iota(jnp.int32, (SB_TQ, SB_BLOCK), 0)
    lane = lax.broadcasted_iota(jnp.int32, (SB_TQ, SB_BLOCK), 1)
    causal = [lane + r * SB_BLOCK < query for r in range(SB_TQ // SB_BLOCK)]
    twice = lambda m: jnp.concatenate([m, m], axis=0)
    return twice(suffix), twice(prefix), causal, lane


def _sum_right(a, suffix2):
    hi = a.astype(BF16)
    lo = (a - hi.astype(F32)).astype(BF16)
    return _dot(jnp.concatenate([hi, lo], axis=1), suffix2)


def _sb_block(qb, kj, suffix, causal, c):
    z = _dot_nt(qb, kj)
    sp = jnp.maximum(z, 0.0) + jnp.log(1.0 + jnp.exp(-jnp.abs(z)))
    if causal is not None:
        sp = jnp.where(causal, sp, 0.0)
    big_l = _sum_right(sp, suffix)
    a = jnp.exp(z - big_l - c)
    if causal is not None:
        a = jnp.where(causal, a, 0.0)
    return z, a, big_l


def _half_masked(pair, lane, scale=1.0):
    pair = pair.astype(F32) * scale
    return jnp.where(lane < SB_DH, pair, 0.0).astype(BF16), jnp.where(lane < SB_DH, 0.0, pair).astype(BF16)


def _sb_fwd(qkv):
    T = qkv.shape[0]
    width = SB_PP * SB_BLOCK
    n_chain = 2 * SB_PP
    n_sub = SB_TQ // SB_BLOCK

    def body(q_ref, k_ref, v_ref, o_ref, right_ref, acc_ref, c_ref, qm):
        i = pl.program_id(1)
        suffix, _, causal, lane = _sb_masks()
        for pp in range(SB_PP):
            qm[2 * pp], qm[2 * pp + 1] = _half_masked(q_ref[:, pp * SB_BLOCK:(pp + 1) * SB_BLOCK], lane, SB_DH ** -0.5)
        acc_ref[...] = jnp.zeros_like(acc_ref)
        c_ref[...] = jnp.zeros_like(c_ref)
        right_ref[...] = jnp.zeros_like(right_ref)

        def step(j, mask):
            rows = pl.ds(pl.multiple_of(j * SB_BLOCK, SB_BLOCK), SB_BLOCK)
            for ch in range(n_chain):
                pair = slice((ch // 2) * SB_BLOCK, (ch // 2 + 1) * SB_BLOCK)
                c = c_ref[ch]
                _, a, big_l = _sb_block(qm[ch], k_ref[rows, pair], suffix, mask, c)
                acc_ref[ch] += _dot(a.astype(BF16), v_ref[rows, pair])
                cols = slice(ch * SB_BLOCK, (ch + 1) * SB_BLOCK)
                right_ref[:, cols] = jnp.where(lane == j, c, right_ref[:, cols])
                c_ref[ch] = c + jnp.broadcast_to(big_l[:, 0:1], c.shape)

        first = i * n_sub
        for r in reversed(range(n_sub)):
            step(first + r, causal[r])

        @pl.loop(0, first)
        def _(t):
            step(first - 1 - t, None)

        for pp in range(SB_PP):
            o_ref[:, pp * SB_BLOCK:(pp + 1) * SB_BLOCK] = jnp.where(lane < SB_DH, acc_ref[2 * pp], acc_ref[2 * pp + 1])

    assert T // SB_BLOCK <= SB_BLOCK
    n_g = SB_WIDTH // width
    blk = lambda part: pl.BlockSpec((SB_TQ, width), lambda g, i, part=part: (i, part * n_g + g))
    whole = lambda part: pl.BlockSpec((T, width), lambda g, i, part=part: (0, part * n_g + g))
    return pl.pallas_call(
        body, grid=(n_g, T // SB_TQ), name="sb_fwd",
        in_specs=[blk(0), whole(1), whole(2)],
        out_specs=[blk(0), pl.BlockSpec((SB_TQ, n_chain * SB_BLOCK), lambda g, i: (i, g))],
        out_shape=[jax.ShapeDtypeStruct((T, SB_WIDTH), F32), jax.ShapeDtypeStruct((T, SB_HEADS * SB_BLOCK), F32)],
        scratch_shapes=[pltpu.VMEM((n_chain, SB_TQ, SB_BLOCK), F32), pltpu.VMEM((n_chain, SB_TQ, SB_BLOCK), F32),
                        pltpu.VMEM((n_chain, SB_TQ, SB_BLOCK), BF16)],
        compiler_params=_params(("parallel", "arbitrary")),
    )(qkv, qkv, qkv)


def _sb_bwd(qkv, right, d_o):
    T = qkv.shape[0]
    width = SB_PP * SB_BLOCK
    n_chain = 2 * SB_PP
    n_sub = SB_TQ // SB_BLOCK
    scale = SB_DH ** -0.5

    def body(q_ref, k_ref, v_ref, right_ref, do_ref, dq_ref, dk_ref, dv_ref, acc_ref, gc_ref, qm, dom):
        i = pl.program_id(1)

        @pl.when(i == 0)
        def _():
            dk_ref[...] = jnp.zeros_like(dk_ref)
            dv_ref[...] = jnp.zeros_like(dv_ref)

        suffix, prefix, causal, lane = _sb_masks()
        for pp in range(SB_PP):
            pair = slice(pp * SB_BLOCK, (pp + 1) * SB_BLOCK)
            qm[2 * pp], qm[2 * pp + 1] = _half_masked(q_ref[:, pair], lane, scale)
            dom[2 * pp], dom[2 * pp + 1] = _half_masked(do_ref[:, pair], lane)
        acc_ref[...] = jnp.zeros_like(acc_ref)
        gc_ref[...] = jnp.zeros_like(gc_ref)

        def step(j, mask):
            rows = pl.ds(pl.multiple_of(j * SB_BLOCK, SB_BLOCK), SB_BLOCK)
            for pp in range(SB_PP):
                pair = slice(pp * SB_BLOCK, (pp + 1) * SB_BLOCK)
                kj = k_ref[rows, pair]
                vj = v_ref[rows, pair]
                dk = jnp.zeros((SB_BLOCK, SB_BLOCK), F32)
                dv = jnp.zeros((SB_BLOCK, SB_BLOCK), F32)
                for ch in (2 * pp, 2 * pp + 1):
                    cols = slice(ch * SB_BLOCK, (ch + 1) * SB_BLOCK)
                    c = jnp.sum(jnp.where(lane == j, right_ref[:, cols], 0.0), axis=1, keepdims=True)
                    z, a, _ = _sb_block(qm[ch], kj, suffix, mask, c)
                    g = a * _dot_nt(dom[ch], vj)
                    g_left = _sum_right(g, prefix)
                    gc = gc_ref[ch]
                    dz = g - _sigmoid(z) * (gc + g_left)
                    if mask is not None:
                        dz = jnp.where(mask, dz, 0.0)
                    dzb = dz.astype(BF16)
                    dk = dk + _dot_tn(dzb, qm[ch])
                    dv = dv + _dot_tn(a.astype(BF16), dom[ch])
                    acc_ref[ch] += _dot(dzb, kj)
                    gc_ref[ch] = gc + jnp.broadcast_to(g_left[:, SB_BLOCK - 1:SB_BLOCK], gc.shape)
                dk_ref[rows, pair] += dk
                dv_ref[rows, pair] += dv

        first = i * n_sub

        @pl.loop(0, first)
        def _(j):
            step(j, None)

        for r in range(n_sub):
            step(first + r, causal[r])
        for pp in range(SB_PP):
            dq_ref[:, pp * SB_BLOCK:(pp + 1) * SB_BLOCK] = scale * jnp.where(lane < SB_DH, acc_ref[2 * pp],
                                                                             acc_ref[2 * pp + 1])

    n_g = SB_WIDTH // width
    blk = lambda part: pl.BlockSpec((SB_TQ, width), lambda g, i, part=part: (i, part * n_g + g))
    whole = lambda part: pl.BlockSpec((T, width), lambda g, i, part=part: (0, part * n_g + g))
    return pl.pallas_call(
        body, grid=(n_g, T // SB_TQ), name="sb_bwd",
        in_specs=[blk(0), whole(1), whole(2), pl.BlockSpec((SB_TQ, n_chain * SB_BLOCK), lambda g, i: (i, g)), blk(0)],
        out_specs=[blk(0), whole(0), whole(0)],
        out_shape=[jax.ShapeDtypeStruct((T, SB_WIDTH), F32)] * 3,
        scratch_shapes=[pltpu.VMEM((n_chain, SB_TQ, SB_BLOCK), F32), pltpu.VMEM((n_chain, SB_TQ, SB_BLOCK), F32),
                        pltpu.VMEM((n_chain, SB_TQ, SB_BLOCK), BF16), pltpu.VMEM((n_chain, SB_TQ, SB_BLOCK), BF16)],
        compiler_params=_params(("parallel", "arbitrary")),
    )(qkv, qkv, qkv, right, d_o)


HBM = pl.BlockSpec(memory_space=pltpu.HBM)
MESH = pl.DeviceIdType.MESH


def _place():
    return lax.axis_index("x"), lax.axis_index("y"), lax.axis_index("c")


def _all_gather(block):
    rows, cols = block.shape

    def body(x_ref, out_ref, send_sems, recv_sems, local_sem):
        x, y, c = _place()
        me, sibling = (x, y, c), (x, y, 1 - c)
        chips = [(1 - x, y), (x, 1 - y), (1 - x, 1 - y)]

        def slot(px, py, pc):
            return out_ref.at[4 * px + 2 * py + pc]

        def copy(k, blk, to, src=None):
            return pltpu.make_async_remote_copy(
                src_ref=slot(*blk) if src is None else src, dst_ref=slot(*blk),
                send_sem=send_sems.at[k], recv_sem=recv_sems.at[k], device_id=to, device_id_type=MESH)

        mine = pltpu.make_async_copy(x_ref, slot(*me), local_sem)
        mine.start()
        first = [copy(0, me, sibling, src=x_ref)]
        first += [copy(1 + j, me, (*chip, c), src=x_ref) for j, chip in enumerate(chips)]
        for cp in first:
            cp.start()
        passed = [copy(4 + j, (*chip, c), sibling) for j, chip in enumerate(chips)]
        for j, chip in enumerate(chips):
            copy(1 + j, (*chip, c), me).wait_recv()
            passed[j].start()
        copy(0, sibling, me).wait_recv()
        for j, chip in enumerate(chips):
            copy(4 + j, (*chip, 1 - c), me).wait_recv()
        for cp in first + passed:
            cp.wait_send()
        mine.wait()

    return pl.pallas_call(
        body, name="all_gather",
        out_shape=jax.ShapeDtypeStruct((N_DEV, rows, cols), block.dtype),
        in_specs=[HBM], out_specs=HBM,
        scratch_shapes=[pltpu.SemaphoreType.DMA((7,)), pltpu.SemaphoreType.DMA((7,)), pltpu.SemaphoreType.DMA],
    )(block)


def _exchange(blocks, small):
    _, rows, cols = blocks.shape

    def body(blocks_ref, small_ref, got_ref, smalls_ref, send_sems, recv_sems, local_sems):
        x, y, c = _place()
        my_index = 4 * x + 2 * y + c
        own = pltpu.make_async_copy(blocks_ref.at[my_index], got_ref.at[0], local_sems.at[0])
        own_small = pltpu.make_async_copy(small_ref, smalls_ref.at[my_index], local_sems.at[1])
        own.start()
        own_small.start()
        copies = []
        for r in range(1, N_DEV):
            px = 1 - x if r & 4 else x
            py = 1 - y if r & 2 else y
            pc = 1 - c if r & 1 else c
            peer_index = 4 * px + 2 * py + pc
            copies.append(pltpu.make_async_remote_copy(
                src_ref=blocks_ref.at[peer_index], dst_ref=got_ref.at[r],
                send_sem=send_sems.at[r - 1], recv_sem=recv_sems.at[r - 1], device_id=(px, py, pc), device_id_type=MESH))
            copies.append(pltpu.make_async_remote_copy(
                src_ref=small_ref, dst_ref=smalls_ref.at[my_index],
                send_sem=send_sems.at[6 + r], recv_sem=recv_sems.at[6 + r], device_id=(px, py, pc), device_id_type=MESH))
        for cp in copies:
            cp.start()
        for cp in copies:
            cp.wait()
        own.wait()
        own_small.wait()

    return pl.pallas_call(
        body, name="grad_exchange",
        out_shape=[jax.ShapeDtypeStruct((N_DEV, rows, cols), blocks.dtype),
                   jax.ShapeDtypeStruct((N_DEV,) + small.shape, small.dtype)],
        in_specs=[HBM, HBM], out_specs=[HBM, HBM],
        scratch_shapes=[pltpu.SemaphoreType.DMA((14,)), pltpu.SemaphoreType.DMA((14,)), pltpu.SemaphoreType.DMA((2,))],
    )(blocks, small)


def _sum_blocks(got):
    n, rows, cols = got.shape
    tr = 256

    def body(got_ref, out_ref):
        acc = got_ref[0].astype(F32)
        for r in range(1, n):
            acc = acc + got_ref[r].astype(F32)
        out_ref[...] = acc

    return pl.pallas_call(
        body, grid=(rows // tr,), name="grad_sum",
        in_specs=[pl.BlockSpec((n, tr, cols), lambda i: (0, i, 0))],
        out_specs=pl.BlockSpec((tr, cols), lambda i: (i, 0)),
        out_shape=jax.ShapeDtypeStruct((rows, cols), F32),
        compiler_params=_params(("parallel",)),
    )(got)


def _adamw_math(w, g, m, v):
    m = ADAM_B1 * m + (1.0 - ADAM_B1) * g
    v = ADAM_B2 * v + (1.0 - ADAM_B2) * (g * g)
    m_hat = m / (1.0 - ADAM_B1 ** ADAM_STEP)
    v_hat = v / (1.0 - ADAM_B2 ** ADAM_STEP)
    delta = -ADAM_LR * (m_hat / (jnp.sqrt(v_hat) + ADAM_EPS) + ADAM_WD * w)
    return delta, m, v


def _adamw(w, g, m, v, name):
    rows, cols = w.shape
    tr = rows if rows <= 352 else 256

    def body(w_ref, g_ref, m_ref, v_ref, d_out, m_out, v_out):
        d_out[...], m_out[...], v_out[...] = _adamw_math(w_ref[...], g_ref[...], m_ref[...], v_ref[...])

    spec = pl.BlockSpec((tr, cols), lambda i: (i, 0))
    return pl.pallas_call(
        body, grid=(rows // tr,), name=name,
        in_specs=[spec] * 4, out_specs=[spec] * 3,
        out_shape=[jax.ShapeDtypeStruct((rows, cols), F32)] * 3,
        compiler_params=_params(("parallel",)),
    )(w, g, m, v)


ROW_ATTN_PRE, ROW_ATTN_POST, ROW_FFN_PRE, ROW_FFN_POST, ROW_OUT_NORMS, ROW_GAMMA, ROW_LOSS = range(7)


def _small_update(smalls, w, m, v):
    def body(s_ref, w_ref, m_ref, v_ref, g_out, d_out, m_out, v_out):
        g = s_ref[0]
        for j in range(1, N_DEV):
            g = g + s_ref[j]
        wv = w_ref[...]
        g0 = wv[ROW_GAMMA:ROW_GAMMA + 1, 0:HG_WIDTH]
        g1 = wv[ROW_GAMMA:ROW_GAMMA + 1, HG_WIDTH:]
        mx = jnp.maximum(g0, g1)
        e0 = jnp.exp(g0 - mx)
        e1 = jnp.exp(g1 - mx)
        lb = e0 / (e0 + e1)
        dg0 = g[ROW_GAMMA:ROW_GAMMA + 1, 0:HG_WIDTH] * lb * (1.0 - lb)
        row = lax.broadcasted_iota(jnp.int32, g.shape, 0)
        g = jnp.where(row == ROW_GAMMA, jnp.concatenate([dg0, -dg0], axis=1), g)
        g_out[...] = g
        d_out[...], m_out[...], v_out[...] = _adamw_math(wv, g, m_ref[...], v_ref[...])

    return pl.pallas_call(
        body, name="small_update",
        out_shape=[jax.ShapeDtypeStruct((8, D_MODEL), F32)] * 4,
    )(smalls, w, m, v)


def _local_step(x, p, target, g_attn_pre, w_in, gamma, g_hg, g_sb, w_out, g_attn_post, g_ffn_pre, w_gu, w_down,
                g_ffn_post, w_pp, w_pg):
    proj, u, qkv = _in_proj(x, g_attn_pre, w_in)
    o_hg, hist = _hg_fwd(proj, gamma)
    o_sb, right = _sb_fwd(qkv)
    cat, mix, h1 = _out_proj(o_hg, proj, o_sb, x, g_hg, g_sb, g_attn_post, w_out)
    u2, gate, up, y, h2 = _ffn_fwd(h1, g_ffn_pre, w_gu, w_down, g_ffn_post)
    dh2, de, dg, loss = _ple_loss(h2, p, target, w_pp, w_pg)

    d_wpp = _wgrad(p, de, "wgrad_ple_proj")
    d_wpg = _wgrad(h2, dg, "wgrad_ple_gate")
    dy, a, dgate, dup, dh1, dg_ffn_post, dg_ffn_pre = _ffn_bwd(dh2, y, h1, gate, up, g_ffn_post, g_ffn_pre, w_gu, w_down)
    d_wdown = _wgrad(a, dy, "wgrad_down")
    d_wgu = jnp.concatenate([_wgrad(u2, dgate, "wgrad_gate"), _wgrad(u2, dup, "wgrad_up")], axis=1)
    dmix, d_ohg, d_hg, d_osb, dg_attn_post, dg_hg, dg_sb = _out_bwd(dh1, mix, o_hg, proj, o_sb, g_hg, g_sb, g_attn_post,
                                                                    w_out)
    d_wout = _wgrad(cat, dmix, "wgrad_out")
    dq, dk, dv = _sb_bwd(qkv, right, d_osb)
    d_hgrn, d_lb = _hg_bwd(proj, gamma, d_ohg, hist)
    dproj, dx, dg_attn_pre = _in_bwd(d_hgrn, d_hg, dq, dk, dv, x, dh1, g_attn_pre, w_in)
    d_win = _wgrad(u, dproj, "wgrad_in")
    smalls = (dg_attn_pre, dg_attn_post, dg_ffn_pre, dg_ffn_post, dg_hg, dg_sb, d_lb)
    return loss, dx, (d_win, d_wout, d_wgu, d_wdown, d_wpp, d_wpg), smalls


def _cols_to_rows(a):
    rows, cols = a.shape
    n = cols // N_DEV
    return a.reshape(rows, N_DEV, n).transpose(1, 0, 2).reshape(N_DEV, rows * n // D_MODEL, D_MODEL)


def _rows_to_cols(a, rows):
    n = a.shape[1] * D_MODEL // rows
    return a.reshape(N_DEV, rows, n).transpose(1, 0, 2).reshape(rows, N_DEV * n)


def _pack_row(*parts):
    return jnp.concatenate([q.reshape(1, -1) for q in parts], axis=1)


def kernel(x, p, attn_pre_norm, w_in, hg_lower_gamma, hg_out_norm, sb_out_norm, w_out, attn_post_norm, ffn_pre_norm, w_gate_up, w_down, ffn_post_norm, ple_proj, ple_gate, loss_target, m_attn_pre_norm, m_w_in, m_hg_lower_gamma, m_hg_out_norm, m_sb_out_norm, m_w_out, m_attn_post_norm, m_ffn_pre_norm, m_w_gate_up, m_w_down, m_ffn_post_norm, m_ple_proj, m_ple_gate, v_attn_pre_norm, v_w_in, v_hg_lower_gamma, v_hg_out_norm, v_sb_out_norm, v_w_out, v_attn_post_norm, v_ffn_pre_norm, v_w_gate_up, v_w_down, v_ffn_post_norm, v_ple_proj, v_ple_gate):
    big = (w_in[0], w_out[0], w_gate_up[0], w_down[0], ple_proj[0], ple_gate[0])
    big_m = (m_w_in[0], m_w_out[0], m_w_gate_up[0], m_w_down[0], m_ple_proj[0], m_ple_gate[0])
    big_v = (v_w_in[0], v_w_out[0], v_w_gate_up[0], v_w_down[0], v_ple_proj[0], v_ple_gate[0])
    names = ("w_in", "w_out", "w_gate_up", "w_down", "ple_proj", "ple_gate")

    wpack = jnp.concatenate([w.astype(BF16).reshape(-1, D_MODEL) for w in big], axis=0)
    gathered = _all_gather(wpack)
    part = lambda i: gathered[:, PACK_OFFS[i]:PACK_OFFS[i] + PACK_SIZES[i], :]
    full_w_in = _rows_to_cols(part(0), D_MODEL)
    full_w_out = part(1).reshape(D_MODEL, D_MODEL)
    full_w_gu = _rows_to_cols(part(2), D_MODEL)
    full_w_down = part(3).reshape(D_FF, D_MODEL)
    full_w_pp = _rows_to_cols(part(4), PLE_DIM)
    full_w_pg = part(5).reshape(D_MODEL, D_MODEL)

    loss, dx, grads, smalls = _local_step(
        x[0], p[0, 0], loss_target[0], attn_pre_norm, full_w_in, hg_lower_gamma, hg_out_norm, sb_out_norm, full_w_out,
        attn_post_norm, ffn_pre_norm, full_w_gu, full_w_down, ffn_post_norm, full_w_pp, full_w_pg)
    d_win, d_wout, d_wgu, d_wdown, d_wpp, d_wpg = grads
    dg_attn_pre, dg_attn_post, dg_ffn_pre, dg_ffn_post, dg_hg, dg_sb, d_lb = smalls

    gpack = jnp.concatenate([
        _cols_to_rows(d_win), d_wout.reshape(N_DEV, -1, D_MODEL), _cols_to_rows(d_wgu),
        d_wdown.reshape(N_DEV, -1, D_MODEL), _cols_to_rows(d_wpp), d_wpg.reshape(N_DEV, -1, D_MODEL)], axis=1)
    zeros_half = jnp.zeros((1, HG_WIDTH), F32)
    small_pack = jnp.concatenate([
        dg_attn_pre, dg_attn_post, dg_ffn_pre, dg_ffn_post, _pack_row(dg_hg, dg_sb), _pack_row(d_lb, zeros_half),
        jnp.broadcast_to(loss[0:1, 0:1], (1, D_MODEL)), jnp.zeros((1, D_MODEL), F32)], axis=0)
    got, all_smalls = _exchange(gpack, small_pack)
    gsum = _sum_blocks(got)

    out_g, out_d, out_m, out_v = {}, {}, {}, {}
    for i, name in enumerate(names):
        g = gsum[PACK_OFFS[i]:PACK_OFFS[i] + PACK_SIZES[i]].reshape(big[i].shape)
        out_g[name] = g[None]
        d, m, v = _adamw(big[i], g, big_m[i], big_v[i], "adamw_" + name)
        out_d[name], out_m[name], out_v[name] = d[None], m[None], v[None]

    def small_rows(pre, gam, hg, sb, post, fpre, fpost):
        return jnp.concatenate([pre, post, fpre, fpost, _pack_row(hg, sb), _pack_row(gam[0], gam[1]),
                                jnp.zeros((2, D_MODEL), F32)], axis=0)

    packs = _small_update(
        all_smalls,
        small_rows(attn_pre_norm, hg_lower_gamma, hg_out_norm, sb_out_norm, attn_post_norm, ffn_pre_norm, ffn_post_norm),
        small_rows(m_attn_pre_norm, m_hg_lower_gamma, m_hg_out_norm, m_sb_out_norm, m_attn_post_norm, m_ffn_pre_norm,
                   m_ffn_post_norm),
        small_rows(v_attn_pre_norm, v_hg_lower_gamma, v_hg_out_norm, v_sb_out_norm, v_attn_post_norm, v_ffn_pre_norm,
                   v_ffn_post_norm))

    def unpack(pk):
        return {
            "attn_pre_norm": pk[ROW_ATTN_PRE:ROW_ATTN_PRE + 1],
            "hg_lower_gamma": pk[ROW_GAMMA].reshape(2, HG_WIDTH),
            "hg_out_norm": pk[ROW_OUT_NORMS:ROW_OUT_NORMS + 1, :HG_WIDTH],
            "sb_out_norm": pk[ROW_OUT_NORMS:ROW_OUT_NORMS + 1, HG_WIDTH:],
            "attn_post_norm": pk[ROW_ATTN_POST:ROW_ATTN_POST + 1],
            "ffn_pre_norm": pk[ROW_FFN_PRE:ROW_FFN_PRE + 1],
            "ffn_post_norm": pk[ROW_FFN_POST:ROW_FFN_POST + 1],
        }

    sg, sd, sm, sv = (unpack(pk) for pk in packs)
    out_g.update(sg), out_d.update(sd), out_m.update(sm), out_v.update(sv)
    order = ("attn_pre_norm", "w_in", "hg_lower_gamma", "hg_out_norm", "sb_out_norm", "w_out", "attn_post_norm",
             "ffn_pre_norm", "w_gate_up", "w_down", "ffn_post_norm", "ple_proj", "ple_gate")
    total_loss = packs[0][ROW_LOSS, 0]
    return (total_loss, dx[None], *[out_g[n] for n in order], *[out_d[n] for n in order],
            *[out_m[n] for n in order], *[out_v[n] for n in order])
```

```python
import functools

import jax
import jax.numpy as jnp
from jax import lax
from jax.experimental import pallas as pl
from jax.experimental.pallas import tpu as pltpu

F32 = jnp.float32
BF16 = jnp.bfloat16

D_MODEL = 1024
HG_WIDTH = 512
HG_HEADS = 4
HG_DK = 128
SB_WIDTH = 512
SB_HEADS = 8
SB_DH = 64
SB_BLOCK = 128
SB_PP = 2
SB_TQ = 512
D_FF = 2816
PLE_DIM = 256
IN_COLS = 4 * HG_WIDTH + 3 * SB_WIDTH
EPS = 1e-6
N_DEV = 8

HG_SUB = 16
HG_TILE = 128
FFN_TF = 256
ROW_TILE = 256
VMEM_LIMIT = 56 * 1024 * 1024
WGRAD_ACC_BYTES = 8 * 1024 * 1024

ADAM_LR = 0.001
ADAM_B1 = 0.9
ADAM_B2 = 0.999
ADAM_EPS = 1e-08
ADAM_WD = 0.01
ADAM_STEP = 10

PACK_SIZES = (448, 128, 704, 352, 32, 128)
PACK_ROWS = sum(PACK_SIZES)
PACK_OFFS = tuple(sum(PACK_SIZES[:i]) for i in range(len(PACK_SIZES)))

NT_DIMS = (((1,), (1,)), ((), ()))
TN_DIMS = (((0,), (0,)), ((), ()))


def _params(semantics):
    return pltpu.CompilerParams(dimension_semantics=semantics, vmem_limit_bytes=VMEM_LIMIT)


def _dot(a, b):
    return jnp.dot(a, b, preferred_element_type=F32)


def _dot_nt(a, b):
    return lax.dot_general(a, b, NT_DIMS, preferred_element_type=F32)


def _dot_tn(a, b):
    return lax.dot_general(a, b, TN_DIMS, preferred_element_type=F32)


def _sigmoid(z):
    return 1.0 / (1.0 + jnp.exp(-z))


def _rstd(xv):
    return lax.rsqrt(jnp.mean(xv * xv, axis=-1, keepdims=True) + EPS)


def _rms_bwd(xv, r, g, dn):
    xn = xv * r
    gh = dn * g
    dx = r * (gh - xn * jnp.mean(gh * xn, axis=-1, keepdims=True))
    return dx, dn * xn


def _colsum(a):
    return jnp.sum(a, axis=0, keepdims=True)


def _row_spec(tm, width, col=0):
    return pl.BlockSpec((tm, width), lambda i, col=col: (i, col))


def _full_spec(shape):
    return pl.BlockSpec(shape, lambda i: (0,) * len(shape))


def _in_proj(x, g, w):
    T = x.shape[0]
    tm = ROW_TILE

    def body(x_ref, g_ref, w_ref, proj_ref, u_ref, qkv_ref):
        xv = x_ref[...]
        u = (xv * _rstd(xv) * g_ref[...]).astype(BF16)
        u_ref[...] = u
        proj = _dot(u, w_ref[...])
        proj_ref[...] = proj
        qkv_ref[...] = proj[:, 4 * HG_WIDTH:].astype(BF16)

    return pl.pallas_call(
        body, grid=(T // tm,), name="in_proj",
        in_specs=[_row_spec(tm, D_MODEL), _full_spec((1, D_MODEL)), _full_spec((D_MODEL, IN_COLS))],
        out_specs=[_row_spec(tm, IN_COLS), _row_spec(tm, D_MODEL), _row_spec(tm, 3 * SB_WIDTH)],
        out_shape=[jax.ShapeDtypeStruct((T, IN_COLS), F32), jax.ShapeDtypeStruct((T, D_MODEL), BF16),
                   jax.ShapeDtypeStruct((T, 3 * SB_WIDTH), BF16)],
        compiler_params=_params(("parallel",)),
    )(x, g, w)


def _out_proj(o_hg, proj, o_sb, x, g_hg, g_sb, g_post, w):
    T = x.shape[0]
    tm = ROW_TILE

    def body(ohg_ref, hg_ref, osb_ref, x_ref, ghg_ref, gsb_ref, gpost_ref, w_ref, cat_ref, mix_ref, h1_ref):
        ohg = ohg_ref[...]
        hg = hg_ref[...]
        osb = osb_ref[...]
        a = ohg * _rstd(ohg) * ghg_ref[...] * (hg * _sigmoid(hg))
        n2 = osb * _rstd(osb) * gsb_ref[...]
        cat = jnp.concatenate([a, n2], axis=1).astype(BF16)
        cat_ref[...] = cat
        mix = _dot(cat, w_ref[...])
        mix_ref[...] = mix
        h1_ref[...] = x_ref[...] + mix * _rstd(mix) * gpost_ref[...]

    return pl.pallas_call(
        body, grid=(T // tm,), name="out_proj",
        in_specs=[_row_spec(tm, HG_WIDTH), _row_spec(tm, HG_WIDTH, 3), _row_spec(tm, SB_WIDTH), _row_spec(tm, D_MODEL),
                  _full_spec((1, HG_WIDTH)), _full_spec((1, SB_WIDTH)), _full_spec((1, D_MODEL)),
                  _full_spec((D_MODEL, D_MODEL))],
        out_specs=[_row_spec(tm, D_MODEL)] * 3,
        out_shape=[jax.ShapeDtypeStruct((T, D_MODEL), BF16), jax.ShapeDtypeStruct((T, D_MODEL), F32),
                   jax.ShapeDtypeStruct((T, D_MODEL), F32)],
        compiler_params=_params(("parallel",)),
    )(o_hg, proj, o_sb, x, g_hg, g_sb, g_post, w)


def _ffn_fwd(h1, g_pre, w_gu, w_down, g_post):
    T = h1.shape[0]
    tm = 2 * ROW_TILE
    n_f = D_FF // FFN_TF

    def body(h1_ref, gpre_ref, wg_ref, wu_ref, wd_ref, gpost_ref, u2_ref, gate_ref, up_ref, y_ref, h2_ref, acc_ref):
        j = pl.program_id(1)

        @pl.when(j == 0)
        def _():
            hv = h1_ref[...]
            u2_ref[...] = (hv * _rstd(hv) * gpre_ref[...]).astype(BF16)
            acc_ref[...] = jnp.zeros_like(acc_ref)

        u2 = u2_ref[...]
        gate = _dot(u2, wg_ref[...])
        up = _dot(u2, wu_ref[...])
        gate_ref[...] = gate
        up_ref[...] = up
        a = (gate * _sigmoid(gate) * up).astype(BF16)
        acc_ref[...] += _dot(a, wd_ref[...])

        @pl.when(j == n_f - 1)
        def _():
            y = acc_ref[...]
            y_ref[...] = y
            h2_ref[...] = h1_ref[...] + y * _rstd(y) * gpost_ref[...]

    row = lambda width: pl.BlockSpec((tm, width), lambda i, j: (i, 0))
    vec = pl.BlockSpec((1, D_MODEL), lambda i, j: (0, 0))
    return pl.pallas_call(
        body, grid=(T // tm, n_f), name="ffn_fwd",
        in_specs=[row(D_MODEL), vec,
                  pl.BlockSpec((D_MODEL, FFN_TF), lambda i, j: (0, j)),
                  pl.BlockSpec((D_MODEL, FFN_TF), lambda i, j: (0, j + n_f)),
                  pl.BlockSpec((FFN_TF, D_MODEL), lambda i, j: (j, 0)), vec],
        out_specs=[row(D_MODEL),
                   pl.BlockSpec((tm, FFN_TF), lambda i, j: (i, j)),
                   pl.BlockSpec((tm, FFN_TF), lambda i, j: (i, j)),
                   row(D_MODEL), row(D_MODEL)],
        out_shape=[jax.ShapeDtypeStruct((T, D_MODEL), BF16), jax.ShapeDtypeStruct((T, D_FF), F32),
                   jax.ShapeDtypeStruct((T, D_FF), F32), jax.ShapeDtypeStruct((T, D_MODEL), F32),
                   jax.ShapeDtypeStruct((T, D_MODEL), F32)],
        scratch_shapes=[pltpu.VMEM((tm, D_MODEL), F32)],
        compiler_params=_params(("parallel", "arbitrary")),
    )(h1, g_pre, w_gu, w_gu, w_down, g_post)


def _ple_loss(h2, p, target, w_pp, w_pg):
    T = h2.shape[0]
    tm = ROW_TILE

    def body(h2_ref, p_ref, t_ref, wpp_ref, wpg_ref, dh2_ref, de_ref, dg_ref, loss_ref):
        i = pl.program_id(0)
        h2v = h2_ref[...]
        e = _dot(p_ref[...].astype(BF16), wpp_ref[...])
        sg = _sigmoid(_dot(h2v.astype(BF16), wpg_ref[...]))
        diff = h2v + e * sg - t_ref[...]
        part = jnp.sum(jnp.sum(diff * diff, axis=1, keepdims=True), axis=0, keepdims=True) * (0.5 / D_MODEL)

        @pl.when(i == 0)
        def _():
            loss_ref[...] = jnp.zeros_like(loss_ref)

        loss_ref[...] += jnp.broadcast_to(part, loss_ref.shape)
        dh3 = diff * (1.0 / D_MODEL)
        de_ref[...] = (dh3 * sg).astype(BF16)
        dg = (dh3 * e * sg * (1.0 - sg)).astype(BF16)
        dg_ref[...] = dg
        dh2_ref[...] = dh3 + _dot_nt(dg, wpg_ref[...])

    return pl.pallas_call(
        body, grid=(T // tm,), name="ple_loss",
        in_specs=[_row_spec(tm, D_MODEL), _row_spec(tm, PLE_DIM), _row_spec(tm, D_MODEL),
                  _full_spec((PLE_DIM, D_MODEL)), _full_spec((D_MODEL, D_MODEL))],
        out_specs=[_row_spec(tm, D_MODEL)] * 3 + [_full_spec((8, 128))],
        out_shape=[jax.ShapeDtypeStruct((T, D_MODEL), F32), jax.ShapeDtypeStruct((T, D_MODEL), BF16),
                   jax.ShapeDtypeStruct((T, D_MODEL), BF16), jax.ShapeDtypeStruct((8, 128), F32)],
        compiler_params=_params(("arbitrary",)),
    )(h2, p, target, w_pp, w_pg)


def _ffn_bwd(dh2, y, h1, gate, up, g_post, g_pre, w_gu, w_down):
    T = h1.shape[0]
    tm = 2 * ROW_TILE
    n_f = D_FF // FFN_TF

    def body(dh2_ref, y_ref, h1_ref, gate_ref, up_ref, gpost_ref, gpre_ref, wg_ref, wu_ref, wd_ref,
             dy_ref, a_ref, dgate_ref, dup_ref, dh1_ref, dgpost_ref, dgpre_ref, du2_ref):
        i = pl.program_id(0)
        j = pl.program_id(1)

        @pl.when((i == 0) & (j == 0))
        def _():
            dgpost_ref[...] = jnp.zeros_like(dgpost_ref)
            dgpre_ref[...] = jnp.zeros_like(dgpre_ref)

        @pl.when(j == 0)
        def _():
            yv = y_ref[...]
            dy, gterm = _rms_bwd(yv, _rstd(yv), gpost_ref[...], dh2_ref[...])
            dy_ref[...] = dy.astype(BF16)
            dgpost_ref[...] += _colsum(gterm)
            du2_ref[...] = jnp.zeros_like(du2_ref)

        da = _dot_nt(dy_ref[...], wd_ref[...])
        gate = gate_ref[...]
        up = up_ref[...]
        s = _sigmoid(gate)
        sil = gate * s
        a_ref[...] = (sil * up).astype(BF16)
        dgate = (da * up * (s * (1.0 + gate * (1.0 - s)))).astype(BF16)
        dup = (da * sil).astype(BF16)
        dgate_ref[...] = dgate
        dup_ref[...] = dup
        du2_ref[...] += _dot_nt(dgate, wg_ref[...]) + _dot_nt(dup, wu_ref[...])

        @pl.when(j == n_f - 1)
        def _():
            hv = h1_ref[...]
            dx, gterm = _rms_bwd(hv, _rstd(hv), gpre_ref[...], du2_ref[...])
            dh1_ref[...] = dh2_ref[...] + dx
            dgpre_ref[...] += _colsum(gterm)

    row = lambda width: pl.BlockSpec((tm, width), lambda i, j: (i, 0))
    col = pl.BlockSpec((tm, FFN_TF), lambda i, j: (i, j))
    vec = pl.BlockSpec((1, D_MODEL), lambda i, j: (0, 0))
    return pl.pallas_call(
        body, grid=(T // tm, n_f), name="ffn_bwd",
        in_specs=[row(D_MODEL), row(D_MODEL), row(D_MODEL), col, col, vec, vec,
                  pl.BlockSpec((D_MODEL, FFN_TF), lambda i, j: (0, j)),
                  pl.BlockSpec((D_MODEL, FFN_TF), lambda i, j: (0, j + n_f)),
                  pl.BlockSpec((FFN_TF, D_MODEL), lambda i, j: (j, 0))],
        out_specs=[row(D_MODEL), col, col, col, row(D_MODEL), vec, vec],
        out_shape=[jax.ShapeDtypeStruct((T, D_MODEL), BF16), jax.ShapeDtypeStruct((T, D_FF), BF16),
                   jax.ShapeDtypeStruct((T, D_FF), BF16), jax.ShapeDtypeStruct((T, D_FF), BF16),
                   jax.ShapeDtypeStruct((T, D_MODEL), F32), jax.ShapeDtypeStruct((1, D_MODEL), F32),
                   jax.ShapeDtypeStruct((1, D_MODEL), F32)],
        scratch_shapes=[pltpu.VMEM((tm, D_MODEL), F32)],
        compiler_params=_params(("arbitrary", "arbitrary")),
    )(dh2, y, h1, gate, up, g_post, g_pre, w_gu, w_gu, w_down)


def _out_bwd(dh1, mix, o_hg, proj, o_sb, g_hg, g_sb, g_post, w):
    T = dh1.shape[0]
    tm = ROW_TILE

    def body(dh1_ref, mix_ref, ohg_ref, hg_ref, osb_ref, ghg_ref, gsb_ref, gpost_ref, w_ref,
             dmix_ref, dohg_ref, dhg_ref, dosb_ref, dgpost_ref, dghg_ref, dgsb_ref):
        i = pl.program_id(0)

        @pl.when(i == 0)
        def _():
            dgpost_ref[...] = jnp.zeros_like(dgpost_ref)
            dghg_ref[...] = jnp.zeros_like(dghg_ref)
            dgsb_ref[...] = jnp.zeros_like(dgsb_ref)

        mix = mix_ref[...]
        dmix, gterm = _rms_bwd(mix, _rstd(mix), gpost_ref[...], dh1_ref[...])
        dgpost_ref[...] += _colsum(gterm)
        dmix = dmix.astype(BF16)
        dmix_ref[...] = dmix
        dcat = _dot_nt(dmix, w_ref[...])
        da = dcat[:, :HG_WIDTH]
        dn2 = dcat[:, HG_WIDTH:]
        ohg = ohg_ref[...]
        r1 = _rstd(ohg)
        hg = hg_ref[...]
        s = _sigmoid(hg)
        dhg_ref[...] = da * (ohg * r1 * ghg_ref[...]) * (s * (1.0 + hg * (1.0 - s)))
        dohg, gterm = _rms_bwd(ohg, r1, ghg_ref[...], da * (hg * s))
        dohg_ref[...] = dohg
        dghg_ref[...] += _colsum(gterm)
        osb = osb_ref[...]
        dosb, gterm = _rms_bwd(osb, _rstd(osb), gsb_ref[...], dn2)
        dosb_ref[...] = dosb
        dgsb_ref[...] += _colsum(gterm)

    return pl.pallas_call(
        body, grid=(T // tm,), name="out_bwd",
        in_specs=[_row_spec(tm, D_MODEL), _row_spec(tm, D_MODEL), _row_spec(tm, HG_WIDTH), _row_spec(tm, HG_WIDTH, 3),
                  _row_spec(tm, SB_WIDTH), _full_spec((1, HG_WIDTH)), _full_spec((1, SB_WIDTH)),
                  _full_spec((1, D_MODEL)), _full_spec((D_MODEL, D_MODEL))],
        out_specs=[_row_spec(tm, D_MODEL), _row_spec(tm, HG_WIDTH), _row_spec(tm, HG_WIDTH), _row_spec(tm, SB_WIDTH),
                   _full_spec((1, D_MODEL)), _full_spec((1, HG_WIDTH)), _full_spec((1, SB_WIDTH))],
        out_shape=[jax.ShapeDtypeStruct((T, D_MODEL), BF16), jax.ShapeDtypeStruct((T, HG_WIDTH), F32),
                   jax.ShapeDtypeStruct((T, HG_WIDTH), F32), jax.ShapeDtypeStruct((T, SB_WIDTH), F32),
                   jax.ShapeDtypeStruct((1, D_MODEL), F32), jax.ShapeDtypeStruct((1, HG_WIDTH), F32),
                   jax.ShapeDtypeStruct((1, SB_WIDTH), F32)],
        compiler_params=_params(("arbitrary",)),
    )(dh1, mix, o_hg, proj, o_sb, g_hg, g_sb, g_post, w)


def _in_bwd(d_hgrn, d_hg, d_sq, d_sk, d_sv, x, dh1, g_pre, w):
    T = x.shape[0]
    tm = ROW_TILE

    def body(dh_ref, dhg_ref, dsq_ref, dsk_ref, dsv_ref, x_ref, dh1_ref, gpre_ref, w_ref, dproj_ref, dx_ref, dgpre_ref):
        i = pl.program_id(0)

        @pl.when(i == 0)
        def _():
            dgpre_ref[...] = jnp.zeros_like(dgpre_ref)

        dp = jnp.concatenate([dh_ref[...], dhg_ref[...], dsq_ref[...], dsk_ref[...], dsv_ref[...]], axis=1).astype(BF16)
        dproj_ref[...] = dp
        du = _dot_nt(dp, w_ref[...])
        xv = x_ref[...]
        dx, gterm = _rms_bwd(xv, _rstd(xv), gpre_ref[...], du)
        dx_ref[...] = dh1_ref[...] + dx
        dgpre_ref[...] += _colsum(gterm)

    return pl.pallas_call(
        body, grid=(T // tm,), name="in_bwd",
        in_specs=[_row_spec(tm, 3 * HG_WIDTH), _row_spec(tm, HG_WIDTH), _row_spec(tm, SB_WIDTH), _row_spec(tm, SB_WIDTH),
                  _row_spec(tm, SB_WIDTH), _row_spec(tm, D_MODEL), _row_spec(tm, D_MODEL), _full_spec((1, D_MODEL)),
                  _full_spec((D_MODEL, IN_COLS))],
        out_specs=[_row_spec(tm, IN_COLS), _row_spec(tm, D_MODEL), _full_spec((1, D_MODEL))],
        out_shape=[jax.ShapeDtypeStruct((T, IN_COLS), BF16), jax.ShapeDtypeStruct((T, D_MODEL), F32),
                   jax.ShapeDtypeStruct((1, D_MODEL), F32)],
        compiler_params=_params(("arbitrary",)),
    )(d_hgrn, d_hg, d_sq, d_sk, d_sv, x, dh1, g_pre, w)


def _wgrad(a, b, name):
    T, K = a.shape
    N = b.shape[1]
    tk = K if K * N * 4 <= WGRAD_ACC_BYTES else K // 2
    assert K % tk == 0 and tk % 128 == 0
    tt = 512
    n_t = T // tt

    def body(a_ref, b_ref, o_ref, acc_ref):
        t = pl.program_id(1)

        @pl.when(t == 0)
        def _():
            acc_ref[...] = jnp.zeros_like(acc_ref)

        acc_ref[...] += _dot_tn(a_ref[...].astype(BF16), b_ref[...].astype(BF16))

        @pl.when(t == n_t - 1)
        def _():
            o_ref[...] = acc_ref[...].astype(BF16)

    return pl.pallas_call(
        body, grid=(K // tk, n_t), name=name,
        in_specs=[pl.BlockSpec((tt, tk), lambda k, t: (t, k)), pl.BlockSpec((tt, N), lambda k, t: (t, 0))],
        out_specs=pl.BlockSpec((tk, N), lambda k, t: (k, 0)),
        out_shape=jax.ShapeDtypeStruct((K, N), BF16),
        scratch_shapes=[pltpu.VMEM((tk, N), F32)],
        compiler_params=_params(("parallel", "arbitrary")),
    )(a, b)


def _hg_gates(hq, hf, gam):
    g0 = gam[0:1, :]
    g1 = gam[1:2, :]
    mx = jnp.maximum(g0, g1)
    e0 = jnp.exp(g0 - mx)
    e1 = jnp.exp(g1 - mx)
    lb = e0 / (e0 + e1)
    s = _sigmoid(hf)
    f = lb + (1.0 - lb) * s
    sq = _sigmoid(hq)
    return hq * sq, sq, s, f, (1.0 - lb) * (1.0 - s), jnp.log(f), lb


def _prefix_in_sub(a, rowmod):
    n = a.shape[0]
    sh = 1
    while sh < HG_SUB:
        a = a + jnp.where(rowmod >= sh, pltpu.roll(a, sh, axis=0), 0.0)
        sh *= 2
    return a


def _suffix_in_sub(a, rowmod):
    n = a.shape[0]
    sh = 1
    while sh < HG_SUB:
        a = a + jnp.where(rowmod < HG_SUB - sh, pltpu.roll(a, n - sh, axis=0), 0.0)
        sh *= 2
    return a


def _hg_fwd(proj, gam):
    T = proj.shape[0]
    tb = HG_TILE
    n_sub = tb // HG_SUB

    def body(hq_ref, hf_ref, hi_ref, gam_ref, o_ref, hist_ref, st_ref, qt_ref, kk_ref, b_ref, od_ref):
        i = pl.program_id(0)

        @pl.when(i == 0)
        def _():
            st_ref[...] = jnp.zeros_like(st_ref)

        rowmod_w = lax.broadcasted_iota(jnp.int32, (tb, HG_WIDTH), 0) % HG_SUB
        rowmod = lax.broadcasted_iota(jnp.int32, (tb, 1), 0) % HG_SUB
        q, _, _, _, kk, lf, _ = _hg_gates(hq_ref[...], hf_ref[...], gam_ref[...])
        b = _prefix_in_sub(lf, rowmod_w)
        qt_ref[...] = (q * jnp.exp(b)).astype(BF16)
        kk_ref[...] = kk
        b_ref[...] = b

        for h in range(HG_HEADS):
            cs = slice(h * HG_DK, (h + 1) * HG_DK)
            qh = q[:, cs]
            kh = kk[:, cs]
            bh = b[:, cs]
            vh = hi_ref[:, cs]
            acc = jnp.sum(qh * kh, axis=1, keepdims=True) * vh
            for d in range(1, HG_SUB):
                e = qh * pltpu.roll(kh, d, axis=0) * jnp.exp(bh - pltpu.roll(bh, d, axis=0))
                sc = jnp.where(rowmod >= d, jnp.sum(e, axis=1, keepdims=True), 0.0)
                acc = acc + sc * pltpu.roll(vh, d, axis=0)
            od_ref[:, cs] = acc

        for m in range(n_sub):
            rs = slice(m * HG_SUB, (m + 1) * HG_SUB)
            for h in range(HG_HEADS):
                cs = slice(h * HG_DK, (h + 1) * HG_DK)
                st = st_ref[h]
                hist_ref[m, h] = st
                o_int = _dot_nt(qt_ref[rs, cs], st.astype(BF16))
                bs = b_ref[rs, cs]
                bl = bs[HG_SUB - 1:HG_SUB, :]
                kt = (kk_ref[rs, cs] * jnp.exp(bl - bs)).astype(BF16)
                st_ref[h] = st * jnp.exp(bl) + _dot_tn(hi_ref[rs, cs].astype(BF16), kt)
                o_ref[rs, cs] = o_int + od_ref[rs, cs]

    col = lambda c: pl.BlockSpec((tb, HG_WIDTH), lambda i, c=c: (i, c))
    return pl.pallas_call(
        body, grid=(T // tb,), name="hg_fwd",
        in_specs=[col(0), col(1), col(2), _full_spec((2, HG_WIDTH))],
        out_specs=[pl.BlockSpec((tb, HG_WIDTH), lambda i: (i, 0)),
                   pl.BlockSpec((n_sub, HG_HEADS, HG_DK, HG_DK), lambda i: (i, 0, 0, 0))],
        out_shape=[jax.ShapeDtypeStruct((T, HG_WIDTH), F32),
                   jax.ShapeDtypeStruct((T // HG_SUB, HG_HEADS, HG_DK, HG_DK), F32)],
        scratch_shapes=[pltpu.VMEM((HG_HEADS, HG_DK, HG_DK), F32), pltpu.VMEM((tb, HG_WIDTH), BF16),
                        pltpu.VMEM((tb, HG_WIDTH), F32), pltpu.VMEM((tb, HG_WIDTH), F32),
                        pltpu.VMEM((tb, HG_WIDTH), F32)],
        compiler_params=_params(("arbitrary",)),
    )(proj, proj, proj, gam)


def _hg_bwd(proj, gam, d_o, hist):
    T = proj.shape[0]
    tb = HG_TILE
    n_sub = tb // HG_SUB
    n_t = T // tb

    def body(hq_ref, hf_ref, hi_ref, gam_ref, do_ref, hist_ref, dout_ref, dlb_ref,
             dst_ref, q_ref, kk_ref, b_ref, dq_ref, dkk_ref, db_ref, dv_ref):
        i = pl.program_id(0)

        @pl.when(i == 0)
        def _():
            dst_ref[...] = jnp.zeros_like(dst_ref)
            dlb_ref[...] = jnp.zeros_like(dlb_ref)

        rowmod_w = lax.broadcasted_iota(jnp.int32, (tb, HG_WIDTH), 0) % HG_SUB
        rowmod = lax.broadcasted_iota(jnp.int32, (tb, 1), 0) % HG_SUB
        last_row = lax.broadcasted_iota(jnp.int32, (HG_SUB, 1), 0) == HG_SUB - 1
        hq = hq_ref[...]
        q, sq, s, f, kk, lf, lb = _hg_gates(hq, hf_ref[...], gam_ref[...])
        b = _prefix_in_sub(lf, rowmod_w)
        q_ref[...] = q
        kk_ref[...] = kk
        b_ref[...] = b

        for m in reversed(range(n_sub)):
            rs = slice(m * HG_SUB, (m + 1) * HG_SUB)
            for h in range(HG_HEADS):
                cs = slice(h * HG_DK, (h + 1) * HG_DK)
                st = hist_ref[m, h]
                dst = dst_ref[h]
                dstb = dst.astype(BF16)
                dos = do_ref[rs, cs].astype(BF16)
                vs = hi_ref[rs, cs].astype(BF16)
                bs = b_ref[rs, cs]
                bl = bs[HG_SUB - 1:HG_SUB, :]
                ebl = jnp.exp(bl)
                qt = q_ref[rs, cs] * jnp.exp(bs)
                decay = jnp.exp(bl - bs)
                kt = kk_ref[rs, cs] * decay
                dqt = _dot(dos, st.astype(BF16))
                dkt = _dot(vs, dstb)
                dv_ref[rs, cs] = _dot_nt(kt.astype(BF16), dstb)
                dst_ref[h] = dst * ebl + _dot_tn(dos, qt.astype(BF16))
                dq_ref[rs, cs] = dqt * jnp.exp(bs)
                dkk_ref[rs, cs] = dkt * decay
                dktk = dkt * kt
                dbl = _colsum(dktk) + _colsum(dst * st) * ebl
                db_ref[rs, cs] = dqt * qt - dktk + jnp.where(last_row, dbl, 0.0)

        for h in range(HG_HEADS):
            cs = slice(h * HG_DK, (h + 1) * HG_DK)
            qh = q[:, cs]
            kh = kk[:, cs]
            bh = b[:, cs]
            vh = hi_ref[:, cs]
            doh = do_ref[:, cs]
            dsc = jnp.sum(doh * vh, axis=1, keepdims=True)
            sc = jnp.sum(qh * kh, axis=1, keepdims=True)
            dqh = dsc * kh
            dkh = dsc * qh
            dvh = sc * doh
            dbh = jnp.zeros_like(qh)
            for d in range(1, HG_SUB):
                back = tb - d
                valid = rowmod >= d
                kd = pltpu.roll(kh, d, axis=0)
                ex = jnp.where(valid, jnp.exp(bh - pltpu.roll(bh, d, axis=0)), 0.0)
                pd = kd * ex
                sc = jnp.sum(qh * pd, axis=1, keepdims=True)
                dsc = jnp.where(valid, jnp.sum(doh * pltpu.roll(vh, d, axis=0), axis=1, keepdims=True), 0.0)
                dqh = dqh + dsc * pd
                w = dsc * qh * pd
                dkh = dkh + pltpu.roll(dsc * qh * ex, back, axis=0)
                dvh = dvh + pltpu.roll(sc * doh, back, axis=0)
                dbh = dbh + w - pltpu.roll(w, back, axis=0)
            dq_ref[:, cs] += dqh
            dkk_ref[:, cs] += dkh
            dv_ref[:, cs] += dvh
            db_ref[:, cs] += dbh

        dlf = _suffix_in_sub(db_ref[...], rowmod_w)
        df = dlf / f - dkk_ref[...]
        dout_ref[:, 0:HG_WIDTH] = dq_ref[...] * (sq * (1.0 + hq * (1.0 - sq)))
        dout_ref[:, HG_WIDTH:2 * HG_WIDTH] = df * (1.0 - lb) * (s * (1.0 - s))
        dout_ref[:, 2 * HG_WIDTH:3 * HG_WIDTH] = dv_ref[...]
        dlb_ref[...] += _colsum(df * (1.0 - s))

    col = lambda c: pl.BlockSpec((tb, HG_WIDTH), lambda i, c=c: (n_t - 1 - i, c))
    return pl.pallas_call(
        body, grid=(n_t,), name="hg_bwd",
        in_specs=[col(0), col(1), col(2), _full_spec((2, HG_WIDTH)),
                  pl.BlockSpec((tb, HG_WIDTH), lambda i: (n_t - 1 - i, 0)),
                  pl.BlockSpec((n_sub, HG_HEADS, HG_DK, HG_DK), lambda i: (n_t - 1 - i, 0, 0, 0))],
        out_specs=[pl.BlockSpec((tb, 3 * HG_WIDTH), lambda i: (n_t - 1 - i, 0)), _full_spec((1, HG_WIDTH))],
        out_shape=[jax.ShapeDtypeStruct((T, 3 * HG_WIDTH), F32), jax.ShapeDtypeStruct((1, HG_WIDTH), F32)],
        scratch_shapes=[pltpu.VMEM((HG_HEADS, HG_DK, HG_DK), F32)] + [pltpu.VMEM((tb, HG_WIDTH), F32)] * 7,
        compiler_params=_params(("arbitrary",)),
    )(proj, proj, proj, gam, d_o, hist)


def _sb_masks():
    row = lax.broadcasted_iota(jnp.int32, (SB_BLOCK, SB_BLOCK), 0)
    col = lax.broadcasted_iota(jnp.int32, (SB_BLOCK, SB_BLOCK), 1)
    suffix = (row >= col).astype(BF16)
    prefix = (row <= col).astype(BF16)
    query = lax.broadcasted_iota(jnp.int32, (SB_TQ, SB_BLOCK), 0)
    lane = lax.broadcasted_iota(jnp.int32, (SB_TQ, SB_BLOCK), 1)
    causal = [lane + r * SB_BLOCK < query for r in range(SB_TQ // SB_BLOCK)]
    twice = lambda m: jnp.concatenate([m, m], axis=0)
    return twice(suffix), twice(prefix), causal, lane


def _sum_right(a, suffix2):
    hi = a.astype(BF16)
    lo = (a - hi.astype(F32)).astype(BF16)
    return _dot(jnp.concatenate([hi, lo], axis=1), suffix2)


def _sb_block(qb, kj, suffix, causal, c):
    z = _dot_nt(qb, kj)
    sp = jnp.maximum(z, 0.0) + jnp.log(1.0 + jnp.exp(-jnp.abs(z)))
    if causal is not None:
        sp = jnp.where(causal, sp, 0.0)
    big_l = _sum_right(sp, suffix)
    a = jnp.exp(z - big_l - c)
    if causal is not None:
        a = jnp.where(causal, a, 0.0)
    return z, a, big_l


def _half_masked(pair, lane, scale=1.0):
    pair = pair.astype(F32) * scale
    return jnp.where(lane < SB_DH, pair, 0.0).astype(BF16), jnp.where(lane < SB_DH, 0.0, pair).astype(BF16)


def _sb_fwd(qkv):
    T = qkv.shape[0]
    width = SB_PP * SB_BLOCK
    n_chain = 2 * SB_PP
    n_sub = SB_TQ // SB_BLOCK

    def body(q_ref, k_ref, v_ref, o_ref, right_ref, acc_ref, c_ref, qm):
        i = pl.program_id(1)
        suffix, _, causal, lane = _sb_masks()
        for pp in range(SB_PP):
            qm[2 * pp], qm[2 * pp + 1] = _half_masked(q_ref[:, pp * SB_BLOCK:(pp + 1) * SB_BLOCK], lane, SB_DH ** -0.5)
        acc_ref[...] = jnp.zeros_like(acc_ref)
        c_ref[...] = jnp.zeros_like(c_ref)
        right_ref[...] = jnp.zeros_like(right_ref)

        def step(j, r=None):
            rows = pl.ds(pl.multiple_of(j * SB_BLOCK, SB_BLOCK), SB_BLOCK)
            qs = slice(0 if r is None else r * SB_BLOCK, SB_TQ)
            mask = None if r is None else causal[r][qs]
            for ch in range(n_chain):
                pair = slice((ch // 2) * SB_BLOCK, (ch // 2 + 1) * SB_BLOCK)
                c = c_ref[ch, qs]
                _, a, big_l = _sb_block(qm[ch, qs], k_ref[rows, pair], suffix, mask, c)
                acc_ref[ch, qs] += _dot(a.astype(BF16), v_ref[rows, pair])
                cols = slice(ch * SB_BLOCK, (ch + 1) * SB_BLOCK)
                right_ref[qs, cols] = jnp.where(lane[qs] == j, c, right_ref[qs, cols])
                c_ref[ch, qs] = c + jnp.broadcast_to(big_l[:, 0:1], c.shape)

        first = i * n_sub
        for r in reversed(range(n_sub)):
            step(first + r, r)

        @pl.loop(0, first)
        def _(t):
            step(first - 1 - t)

        for pp in range(SB_PP):
            o_ref[:, pp * SB_BLOCK:(pp + 1) * SB_BLOCK] = jnp.where(lane < SB_DH, acc_ref[2 * pp], acc_ref[2 * pp + 1])

    assert T // SB_BLOCK <= SB_BLOCK
    n_g = SB_WIDTH // width
    blk = lambda part: pl.BlockSpec((SB_TQ, width), lambda g, i, part=part: (i, part * n_g + g))
    whole = lambda part: pl.BlockSpec((T, width), lambda g, i, part=part: (0, part * n_g + g))
    return pl.pallas_call(
        body, grid=(n_g, T // SB_TQ), name="sb_fwd",
        in_specs=[blk(0), whole(1), whole(2)],
        out_specs=[blk(0), pl.BlockSpec((SB_TQ, n_chain * SB_BLOCK), lambda g, i: (i, g))],
        out_shape=[jax.ShapeDtypeStruct((T, SB_WIDTH), F32), jax.ShapeDtypeStruct((T, SB_HEADS * SB_BLOCK), F32)],
        scratch_shapes=[pltpu.VMEM((n_chain, SB_TQ, SB_BLOCK), F32), pltpu.VMEM((n_chain, SB_TQ, SB_BLOCK), F32),
                        pltpu.VMEM((n_chain, SB_TQ, SB_BLOCK), BF16)],
        compiler_params=_params(("parallel", "arbitrary")),
    )(qkv, qkv, qkv)


def _sb_bwd(qkv, right, d_o):
    T = qkv.shape[0]
    width = SB_PP * SB_BLOCK
    n_chain = 2 * SB_PP
    n_sub = SB_TQ // SB_BLOCK
    scale = SB_DH ** -0.5

    def body(q_ref, k_ref, v_ref, right_ref, do_ref, dq_ref, dk_ref, dv_ref, acc_ref, gc_ref, qm, dom):
        i = pl.program_id(1)

        @pl.when(i == 0)
        def _():
            dk_ref[...] = jnp.zeros_like(dk_ref)
            dv_ref[...] = jnp.zeros_like(dv_ref)

        suffix, prefix, causal, lane = _sb_masks()
        for pp in range(SB_PP):
            pair = slice(pp * SB_BLOCK, (pp + 1) * SB_BLOCK)
            qm[2 * pp], qm[2 * pp + 1] = _half_masked(q_ref[:, pair], lane, scale)
            dom[2 * pp], dom[2 * pp + 1] = _half_masked(do_ref[:, pair], lane)
        acc_ref[...] = jnp.zeros_like(acc_ref)
        gc_ref[...] = jnp.zeros_like(gc_ref)

        def step(j, r=None):
            rows = pl.ds(pl.multiple_of(j * SB_BLOCK, SB_BLOCK), SB_BLOCK)
            qs = slice(0 if r is None else r * SB_BLOCK, SB_TQ)
            mask = None if r is None else causal[r][qs]
            for pp in range(SB_PP):
                pair = slice(pp * SB_BLOCK, (pp + 1) * SB_BLOCK)
                kj = k_ref[rows, pair]
                vj = v_ref[rows, pair]
                dk = jnp.zeros((SB_BLOCK, SB_BLOCK), F32)
                dv = jnp.zeros((SB_BLOCK, SB_BLOCK), F32)
                for ch in (2 * pp, 2 * pp + 1):
                    cols = slice(ch * SB_BLOCK, (ch + 1) * SB_BLOCK)
                    c = jnp.sum(jnp.where(lane[qs] == j, right_ref[qs, cols], 0.0), axis=1, keepdims=True)
                    z, a, _ = _sb_block(qm[ch, qs], kj, suffix, mask, c)
                    g = a * _dot_nt(dom[ch, qs], vj)
                    g_left = _sum_right(g, prefix)
                    gc = gc_ref[ch, qs]
                    dz = g - _sigmoid(z) * (gc + g_left)
                    if mask is not None:
                        dz = jnp.where(mask, dz, 0.0)
                    dzb = dz.astype(BF16)
                    dk = dk + _dot_tn(dzb, qm[ch, qs])
                    dv = dv + _dot_tn(a.astype(BF16), dom[ch, qs])
                    acc_ref[ch, qs] += _dot(dzb, kj)
                    gc_ref[ch, qs] = gc + jnp.broadcast_to(g_left[:, SB_BLOCK - 1:SB_BLOCK], gc.shape)
                dk_ref[rows, pair] += dk
                dv_ref[rows, pair] += dv

        first = i * n_sub

        @pl.loop(0, first)
        def _(j):
            step(j)

        for r in range(n_sub):
            step(first + r, r)
        for pp in range(SB_PP):
            dq_ref[:, pp * SB_BLOCK:(pp + 1) * SB_BLOCK] = scale * jnp.where(lane < SB_DH, acc_ref[2 * pp],
                                                                             acc_ref[2 * pp + 1])

    n_g = SB_WIDTH // width
    blk = lambda part: pl.BlockSpec((SB_TQ, width), lambda g, i, part=part: (i, part * n_g + g))
    whole = lambda part: pl.BlockSpec((T, width), lambda g, i, part=part: (0, part * n_g + g))
    return pl.pallas_call(
        body, grid=(n_g, T // SB_TQ), name="sb_bwd",
        in_specs=[blk(0), whole(1), whole(2), pl.BlockSpec((SB_TQ, n_chain * SB_BLOCK), lambda g, i: (i, g)), blk(0)],
        out_specs=[blk(0), whole(0), whole(0)],
        out_shape=[jax.ShapeDtypeStruct((T, SB_WIDTH), F32)] * 3,
        scratch_shapes=[pltpu.VMEM((n_chain, SB_TQ, SB_BLOCK), F32), pltpu.VMEM((n_chain, SB_TQ, SB_BLOCK), F32),
                        pltpu.VMEM((n_chain, SB_TQ, SB_BLOCK), BF16), pltpu.VMEM((n_chain, SB_TQ, SB_BLOCK), BF16)],
        compiler_params=_params(("parallel", "arbitrary")),
    )(qkv, qkv, qkv, right, d_o)


HBM = pl.BlockSpec(memory_space=pltpu.HBM)
MESH = pl.DeviceIdType.MESH


def _place():
    return lax.axis_index("x"), lax.axis_index("y"), lax.axis_index("c")


def _all_gather(block):
    rows, cols = block.shape

    def body(x_ref, out_ref, send_sems, recv_sems, local_sem):
        x, y, c = _place()
        me, sibling = (x, y, c), (x, y, 1 - c)
        chips = [(1 - x, y), (x, 1 - y), (1 - x, 1 - y)]

        def slot(px, py, pc):
            return out_ref.at[4 * px + 2 * py + pc]

        def copy(k, blk, to, src=None):
            return pltpu.make_async_remote_copy(
                src_ref=slot(*blk) if src is None else src, dst_ref=slot(*blk),
                send_sem=send_sems.at[k], recv_sem=recv_sems.at[k], device_id=to, device_id_type=MESH)

        mine = pltpu.make_async_copy(x_ref, slot(*me), local_sem)
        mine.start()
        first = [copy(0, me, sibling, src=x_ref)]
        first += [copy(1 + j, me, (*chip, c), src=x_ref) for j, chip in enumerate(chips)]
        for cp in first:
            cp.start()
        passed = [copy(4 + j, (*chip, c), sibling) for j, chip in enumerate(chips)]
        for j, chip in enumerate(chips):
            copy(1 + j, (*chip, c), me).wait_recv()
            passed[j].start()
        copy(0, sibling, me).wait_recv()
        for j, chip in enumerate(chips):
            copy(4 + j, (*chip, 1 - c), me).wait_recv()
        for cp in first + passed:
            cp.wait_send()
        mine.wait()

    return pl.pallas_call(
        body, name="all_gather",
        out_shape=jax.ShapeDtypeStruct((N_DEV, rows, cols), block.dtype),
        in_specs=[HBM], out_specs=HBM,
        scratch_shapes=[pltpu.SemaphoreType.DMA((7,)), pltpu.SemaphoreType.DMA((7,)), pltpu.SemaphoreType.DMA],
    )(block)


def _exchange(blocks, small):
    _, rows, cols = blocks.shape

    def body(blocks_ref, small_ref, got_ref, smalls_ref, send_sems, recv_sems, local_sems):
        x, y, c = _place()
        my_index = 4 * x + 2 * y + c
        own = pltpu.make_async_copy(blocks_ref.at[my_index], got_ref.at[0], local_sems.at[0])
        own_small = pltpu.make_async_copy(small_ref, smalls_ref.at[my_index], local_sems.at[1])
        own.start()
        own_small.start()
        copies = []
        for r in range(1, N_DEV):
            px = 1 - x if r & 4 else x
            py = 1 - y if r & 2 else y
            pc = 1 - c if r & 1 else c
            peer_index = 4 * px + 2 * py + pc
            copies.append(pltpu.make_async_remote_copy(
                src_ref=blocks_ref.at[peer_index], dst_ref=got_ref.at[r],
                send_sem=send_sems.at[r - 1], recv_sem=recv_sems.at[r - 1], device_id=(px, py, pc), device_id_type=MESH))
            copies.append(pltpu.make_async_remote_copy(
                src_ref=small_ref, dst_ref=smalls_ref.at[my_index],
                send_sem=send_sems.at[6 + r], recv_sem=recv_sems.at[6 + r], device_id=(px, py, pc), device_id_type=MESH))
        for cp in copies:
            cp.start()
        for cp in copies:
            cp.wait()
        own.wait()
        own_small.wait()

    return pl.pallas_call(
        body, name="grad_exchange",
        out_shape=[jax.ShapeDtypeStruct((N_DEV, rows, cols), blocks.dtype),
                   jax.ShapeDtypeStruct((N_DEV,) + small.shape, small.dtype)],
        in_specs=[HBM, HBM], out_specs=[HBM, HBM],
        scratch_shapes=[pltpu.SemaphoreType.DMA((14,)), pltpu.SemaphoreType.DMA((14,)), pltpu.SemaphoreType.DMA((2,))],
    )(blocks, small)


def _sum_blocks(got):
    n, rows, cols = got.shape
    tr = 256

    def body(got_ref, out_ref):
        acc = got_ref[0].astype(F32)
        for r in range(1, n):
            acc = acc + got_ref[r].astype(F32)
        out_ref[...] = acc

    return pl.pallas_call(
        body, grid=(rows // tr,), name="grad_sum",
        in_specs=[pl.BlockSpec((n, tr, cols), lambda i: (0, i, 0))],
        out_specs=pl.BlockSpec((tr, cols), lambda i: (i, 0)),
        out_shape=jax.ShapeDtypeStruct((rows, cols), F32),
        compiler_params=_params(("parallel",)),
    )(got)


def _adamw_math(w, g, m, v):
    m = ADAM_B1 * m + (1.0 - ADAM_B1) * g
    v = ADAM_B2 * v + (1.0 - ADAM_B2) * (g * g)
    m_hat = m / (1.0 - ADAM_B1 ** ADAM_STEP)
    v_hat = v / (1.0 - ADAM_B2 ** ADAM_STEP)
    delta = -ADAM_LR * (m_hat / (jnp.sqrt(v_hat) + ADAM_EPS) + ADAM_WD * w)
    return delta, m, v


def _adamw(w, g, m, v, name):
    rows, cols = w.shape
    tr = rows if rows <= 352 else 256

    def body(w_ref, g_ref, m_ref, v_ref, d_out, m_out, v_out):
        d_out[...], m_out[...], v_out[...] = _adamw_math(w_ref[...], g_ref[...], m_ref[...], v_ref[...])

    spec = pl.BlockSpec((tr, cols), lambda i: (i, 0))
    return pl.pallas_call(
        body, grid=(rows // tr,), name=name,
        in_specs=[spec] * 4, out_specs=[spec] * 3,
        out_shape=[jax.ShapeDtypeStruct((rows, cols), F32)] * 3,
        compiler_params=_params(("parallel",)),
    )(w, g, m, v)


ROW_ATTN_PRE, ROW_ATTN_POST, ROW_FFN_PRE, ROW_FFN_POST, ROW_OUT_NORMS, ROW_GAMMA, ROW_LOSS = range(7)


def _small_update(smalls, w, m, v):
    def body(s_ref, w_ref, m_ref, v_ref, g_out, d_out, m_out, v_out):
        g = s_ref[0]
        for j in range(1, N_DEV):
            g = g + s_ref[j]
        wv = w_ref[...]
        g0 = wv[ROW_GAMMA:ROW_GAMMA + 1, 0:HG_WIDTH]
        g1 = wv[ROW_GAMMA:ROW_GAMMA + 1, HG_WIDTH:]
        mx = jnp.maximum(g0, g1)
        e0 = jnp.exp(g0 - mx)
        e1 = jnp.exp(g1 - mx)
        lb = e0 / (e0 + e1)
        dg0 = g[ROW_GAMMA:ROW_GAMMA + 1, 0:HG_WIDTH] * lb * (1.0 - lb)
        row = lax.broadcasted_iota(jnp.int32, g.shape, 0)
        g = jnp.where(row == ROW_GAMMA, jnp.concatenate([dg0, -dg0], axis=1), g)
        g_out[...] = g
        d_out[...], m_out[...], v_out[...] = _adamw_math(wv, g, m_ref[...], v_ref[...])

    return pl.pallas_call(
        body, name="small_update",
        out_shape=[jax.ShapeDtypeStruct((8, D_MODEL), F32)] * 4,
    )(smalls, w, m, v)


def _local_step(x, p, target, g_attn_pre, w_in, gamma, g_hg, g_sb, w_out, g_attn_post, g_ffn_pre, w_gu, w_down,
                g_ffn_post, w_pp, w_pg):
    proj, u, qkv = _in_proj(x, g_attn_pre, w_in)
    o_hg, hist = _hg_fwd(proj, gamma)
    o_sb, right = _sb_fwd(qkv)
    cat, mix, h1 = _out_proj(o_hg, proj, o_sb, x, g_hg, g_sb, g_attn_post, w_out)
    u2, gate, up, y, h2 = _ffn_fwd(h1, g_ffn_pre, w_gu, w_down, g_ffn_post)
    dh2, de, dg, loss = _ple_loss(h2, p, target, w_pp, w_pg)

    d_wpp = _wgrad(p, de, "wgrad_ple_proj")
    d_wpg = _wgrad(h2, dg, "wgrad_ple_gate")
    dy, a, dgate, dup, dh1, dg_ffn_post, dg_ffn_pre = _ffn_bwd(dh2, y, h1, gate, up, g_ffn_post, g_ffn_pre, w_gu, w_down)
    d_wdown = _wgrad(a, dy, "wgrad_down")
    d_wgu = jnp.concatenate([_wgrad(u2, dgate, "wgrad_gate"), _wgrad(u2, dup, "wgrad_up")], axis=1)
    dmix, d_ohg, d_hg, d_osb, dg_attn_post, dg_hg, dg_sb = _out_bwd(dh1, mix, o_hg, proj, o_sb, g_hg, g_sb, g_attn_post,
                                                                    w_out)
    d_wout = _wgrad(cat, dmix, "wgrad_out")
    dq, dk, dv = _sb_bwd(qkv, right, d_osb)
    d_hgrn, d_lb = _hg_bwd(proj, gamma, d_ohg, hist)
    dproj, dx, dg_attn_pre = _in_bwd(d_hgrn, d_hg, dq, dk, dv, x, dh1, g_attn_pre, w_in)
    d_win = _wgrad(u, dproj, "wgrad_in")
    smalls = (dg_attn_pre, dg_attn_post, dg_ffn_pre, dg_ffn_post, dg_hg, dg_sb, d_lb)
    return loss, dx, (d_win, d_wout, d_wgu, d_wdown, d_wpp, d_wpg), smalls


def _cols_to_rows(a):
    rows, cols = a.shape
    n = cols // N_DEV
    return a.reshape(rows, N_DEV, n).transpose(1, 0, 2).reshape(N_DEV, rows * n // D_MODEL, D_MODEL)


def _rows_to_cols(a, rows):
    n = a.shape[1] * D_MODEL // rows
    return a.reshape(N_DEV, rows, n).transpose(1, 0, 2).reshape(rows, N_DEV * n)


def _pack_row(*parts):
    return jnp.concatenate([q.reshape(1, -1) for q in parts], axis=1)


def kernel(x, p, attn_pre_norm, w_in, hg_lower_gamma, hg_out_norm, sb_out_norm, w_out, attn_post_norm, ffn_pre_norm, w_gate_up, w_down, ffn_post_norm, ple_proj, ple_gate, loss_target, m_attn_pre_norm, m_w_in, m_hg_lower_gamma, m_hg_out_norm, m_sb_out_norm, m_w_out, m_attn_post_norm, m_ffn_pre_norm, m_w_gate_up, m_w_down, m_ffn_post_norm, m_ple_proj, m_ple_gate, v_attn_pre_norm, v_w_in, v_hg_lower_gamma, v_hg_out_norm, v_sb_out_norm, v_w_out, v_attn_post_norm, v_ffn_pre_norm, v_w_gate_up, v_w_down, v_ffn_post_norm, v_ple_proj, v_ple_gate):
    big = (w_in[0], w_out[0], w_gate_up[0], w_down[0], ple_proj[0], ple_gate[0])
    big_m = (m_w_in[0], m_w_out[0], m_w_gate_up[0], m_w_down[0], m_ple_proj[0], m_ple_gate[0])
    big_v = (v_w_in[0], v_w_out[0], v_w_gate_up[0], v_w_down[0], v_ple_proj[0], v_ple_gate[0])
    names = ("w_in", "w_out", "w_gate_up", "w_down", "ple_proj", "ple_gate")

    wpack = jnp.concatenate([w.astype(BF16).reshape(-1, D_MODEL) for w in big], axis=0)
    gathered = _all_gather(wpack)
    part = lambda i: gathered[:, PACK_OFFS[i]:PACK_OFFS[i] + PACK_SIZES[i], :]
    full_w_in = _rows_to_cols(part(0), D_MODEL)
    full_w_out = part(1).reshape(D_MODEL, D_MODEL)
    full_w_gu = _rows_to_cols(part(2), D_MODEL)
    full_w_down = part(3).reshape(D_FF, D_MODEL)
    full_w_pp = _rows_to_cols(part(4), PLE_DIM)
    full_w_pg = part(5).reshape(D_MODEL, D_MODEL)

    loss, dx, grads, smalls = _local_step(
        x[0], p[0, 0], loss_target[0], attn_pre_norm, full_w_in, hg_lower_gamma, hg_out_norm, sb_out_norm, full_w_out,
        attn_post_norm, ffn_pre_norm, full_w_gu, full_w_down, ffn_post_norm, full_w_pp, full_w_pg)
    d_win, d_wout, d_wgu, d_wdown, d_wpp, d_wpg = grads
    dg_attn_pre, dg_attn_post, dg_ffn_pre, dg_ffn_post, dg_hg, dg_sb, d_lb = smalls

    gpack = jnp.concatenate([
        _cols_to_rows(d_win), d_wout.reshape(N_DEV, -1, D_MODEL), _cols_to_rows(d_wgu),
        d_wdown.reshape(N_DEV, -1, D_MODEL), _cols_to_rows(d_wpp), d_wpg.reshape(N_DEV, -1, D_MODEL)], axis=1)
    zeros_half = jnp.zeros((1, HG_WIDTH), F32)
    small_pack = jnp.concatenate([
        dg_attn_pre, dg_attn_post, dg_ffn_pre, dg_ffn_post, _pack_row(dg_hg, dg_sb), _pack_row(d_lb, zeros_half),
        jnp.broadcast_to(loss[0:1, 0:1], (1, D_MODEL)), jnp.zeros((1, D_MODEL), F32)], axis=0)
    got, all_smalls = _exchange(gpack, small_pack)
    gsum = _sum_blocks(got)

    out_g, out_d, out_m, out_v = {}, {}, {}, {}
    for i, name in enumerate(names):
        g = gsum[PACK_OFFS[i]:PACK_OFFS[i] + PACK_SIZES[i]].reshape(big[i].shape)
        out_g[name] = g[None]
        d, m, v = _adamw(big[i], g, big_m[i], big_v[i], "adamw_" + name)
        out_d[name], out_m[name], out_v[name] = d[None], m[None], v[None]

    def small_rows(pre, gam, hg, sb, post, fpre, fpost):
        return jnp.concatenate([pre, post, fpre, fpost, _pack_row(hg, sb), _pack_row(gam[0], gam[1]),
                                jnp.zeros((2, D_MODEL), F32)], axis=0)

    packs = _small_update(
        all_smalls,
        small_rows(attn_pre_norm, hg_lower_gamma, hg_out_norm, sb_out_norm, attn_post_norm, ffn_pre_norm, ffn_post_norm),
        small_rows(m_attn_pre_norm, m_hg_lower_gamma, m_hg_out_norm, m_sb_out_norm, m_attn_post_norm, m_ffn_pre_norm,
                   m_ffn_post_norm),
        small_rows(v_attn_pre_norm, v_hg_lower_gamma, v_hg_out_norm, v_sb_out_norm, v_attn_post_norm, v_ffn_pre_norm,
                   v_ffn_post_norm))

    def unpack(pk):
        return {
            "attn_pre_norm": pk[ROW_ATTN_PRE:ROW_ATTN_PRE + 1],
            "hg_lower_gamma": pk[ROW_GAMMA].reshape(2, HG_WIDTH),
            "hg_out_norm": pk[ROW_OUT_NORMS:ROW_OUT_NORMS + 1, :HG_WIDTH],
            "sb_out_norm": pk[ROW_OUT_NORMS:ROW_OUT_NORMS + 1, HG_WIDTH:],
            "attn_post_norm": pk[ROW_ATTN_POST:ROW_ATTN_POST + 1],
            "ffn_pre_norm": pk[ROW_FFN_PRE:ROW_FFN_PRE + 1],
            "ffn_post_norm": pk[ROW_FFN_POST:ROW_FFN_POST + 1],
        }

    sg, sd, sm, sv = (unpack(pk) for pk in packs)
    out_g.update(sg), out_d.update(sd), out_m.update(sm), out_v.update(sv)
    order = ("attn_pre_norm", "w_in", "hg_lower_gamma", "hg_out_norm", "sb_out_norm", "w_out", "attn_post_norm",
             "ffn_pre_norm", "w_gate_up", "w_down", "ffn_post_norm", "ple_proj", "ple_gate")
    total_loss = packs[0][ROW_LOSS, 0]
    return (total_loss, dx[None], *[out_g[n] for n in order], *[out_d[n] for n in order],
            *[out_m[n] for n in order], *[out_v[n] for n in order])
```

```python
import functools

import jax
import jax.numpy as jnp
from jax import lax
from jax.experimental import pallas as pl
from jax.experimental.pallas import tpu as pltpu

F32 = jnp.float32
BF16 = jnp.bfloat16

D_MODEL = 1024
HG_WIDTH = 512
HG_HEADS = 4
HG_DK = 128
SB_WIDTH = 512
SB_HEADS = 8
SB_DH = 64
SB_BLOCK = 128
SB_PP = 2
SB_TQ = 512
D_FF = 2816
PLE_DIM = 256
IN_COLS = 4 * HG_WIDTH + 3 * SB_WIDTH
EPS = 1e-6
N_DEV = 8

HG_SUB = 16
HG_TILE = 128
FFN_TF = 256
ROW_TILE = 256
VMEM_LIMIT = 56 * 1024 * 1024
WGRAD_ACC_BYTES = 8 * 1024 * 1024

ADAM_LR = 0.001
ADAM_B1 = 0.9
ADAM_B2 = 0.999
ADAM_EPS = 1e-08
ADAM_WD = 0.01
ADAM_STEP = 10

PACK_SIZES = (448, 128, 704, 352, 32, 128)
PACK_ROWS = sum(PACK_SIZES)
PACK_OFFS = tuple(sum(PACK_SIZES[:i]) for i in range(len(PACK_SIZES)))

NT_DIMS = (((1,), (1,)), ((), ()))
TN_DIMS = (((0,), (0,)), ((), ()))


def _params(semantics):
    return pltpu.CompilerParams(dimension_semantics=semantics, vmem_limit_bytes=VMEM_LIMIT)


def _dot(a, b):
    return jnp.dot(a, b, preferred_element_type=F32)


def _dot_nt(a, b):
    return lax.dot_general(a, b, NT_DIMS, preferred_element_type=F32)


def _dot_tn(a, b):
    return lax.dot_general(a, b, TN_DIMS, preferred_element_type=F32)


def _sigmoid(z):
    return 1.0 / (1.0 + jnp.exp(-z))


def _rstd(xv):
    return lax.rsqrt(jnp.mean(xv * xv, axis=-1, keepdims=True) + EPS)


def _rms_bwd(xv, r, g, dn):
    xn = xv * r
    gh = dn * g
    dx = r * (gh - xn * jnp.mean(gh * xn, axis=-1, keepdims=True))
    return dx, dn * xn


def _colsum(a):
    return jnp.sum(a, axis=0, keepdims=True)


def _row_spec(tm, width, col=0):
    return pl.BlockSpec((tm, width), lambda i, col=col: (i, col))


def _full_spec(shape):
    return pl.BlockSpec(shape, lambda i: (0,) * len(shape))


def _in_proj(x, g, w):
    T = x.shape[0]
    tm = ROW_TILE

    def body(x_ref, g_ref, w_ref, proj_ref, u_ref, qkv_ref):
        xv = x_ref[...]
        u = (xv * _rstd(xv) * g_ref[...]).astype(BF16)
        u_ref[...] = u
        proj = _dot(u, w_ref[...])
        proj_ref[...] = proj
        qkv_ref[...] = proj[:, 4 * HG_WIDTH:].astype(BF16)

    return pl.pallas_call(
        body, grid=(T // tm,), name="in_proj",
        in_specs=[_row_spec(tm, D_MODEL), _full_spec((1, D_MODEL)), _full_spec((D_MODEL, IN_COLS))],
        out_specs=[_row_spec(tm, IN_COLS), _row_spec(tm, D_MODEL), _row_spec(tm, 3 * SB_WIDTH)],
        out_shape=[jax.ShapeDtypeStruct((T, IN_COLS), F32), jax.ShapeDtypeStruct((T, D_MODEL), BF16),
                   jax.ShapeDtypeStruct((T, 3 * SB_WIDTH), BF16)],
        compiler_params=_params(("parallel",)),
    )(x, g, w)


def _out_proj(o_hg, proj, o_sb, x, g_hg, g_sb, g_post, w):
    T = x.shape[0]
    tm = ROW_TILE

    def body(ohg_ref, hg_ref, osb_ref, x_ref, ghg_ref, gsb_ref, gpost_ref, w_ref, cat_ref, mix_ref, h1_ref):
        ohg = ohg_ref[...]
        hg = hg_ref[...]
        osb = osb_ref[...]
        a = ohg * _rstd(ohg) * ghg_ref[...] * (hg * _sigmoid(hg))
        n2 = osb * _rstd(osb) * gsb_ref[...]
        cat = jnp.concatenate([a, n2], axis=1).astype(BF16)
        cat_ref[...] = cat
        mix = _dot(cat, w_ref[...])
        mix_ref[...] = mix
        h1_ref[...] = x_ref[...] + mix * _rstd(mix) * gpost_ref[...]

    return pl.pallas_call(
        body, grid=(T // tm,), name="out_proj",
        in_specs=[_row_spec(tm, HG_WIDTH), _row_spec(tm, HG_WIDTH, 3), _row_spec(tm, SB_WIDTH), _row_spec(tm, D_MODEL),
                  _full_spec((1, HG_WIDTH)), _full_spec((1, SB_WIDTH)), _full_spec((1, D_MODEL)),
                  _full_spec((D_MODEL, D_MODEL))],
        out_specs=[_row_spec(tm, D_MODEL)] * 3,
        out_shape=[jax.ShapeDtypeStruct((T, D_MODEL), BF16), jax.ShapeDtypeStruct((T, D_MODEL), F32),
                   jax.ShapeDtypeStruct((T, D_MODEL), F32)],
        compiler_params=_params(("parallel",)),
    )(o_hg, proj, o_sb, x, g_hg, g_sb, g_post, w)


def _ffn_fwd(h1, g_pre, w_gu, w_down, g_post):
    T = h1.shape[0]
    tm = 2 * ROW_TILE
    n_f = D_FF // FFN_TF

    def body(h1_ref, gpre_ref, wg_ref, wu_ref, wd_ref, gpost_ref, u2_ref, gate_ref, up_ref, y_ref, h2_ref, acc_ref):
        j = pl.program_id(1)

        @pl.when(j == 0)
        def _():
            hv = h1_ref[...]
            u2_ref[...] = (hv * _rstd(hv) * gpre_ref[...]).astype(BF16)
            acc_ref[...] = jnp.zeros_like(acc_ref)

        u2 = u2_ref[...]
        gate = _dot(u2, wg_ref[...])
        up = _dot(u2, wu_ref[...])
        gate_ref[...] = gate
        up_ref[...] = up
        a = (gate * _sigmoid(gate) * up).astype(BF16)
        acc_ref[...] += _dot(a, wd_ref[...])

        @pl.when(j == n_f - 1)
        def _():
            y = acc_ref[...]
            y_ref[...] = y
            h2_ref[...] = h1_ref[...] + y * _rstd(y) * gpost_ref[...]

    row = lambda width: pl.BlockSpec((tm, width), lambda i, j: (i, 0))
    vec = pl.BlockSpec((1, D_MODEL), lambda i, j: (0, 0))
    return pl.pallas_call(
        body, grid=(T // tm, n_f), name="ffn_fwd",
        in_specs=[row(D_MODEL), vec,
                  pl.BlockSpec((D_MODEL, FFN_TF), lambda i, j: (0, j)),
                  pl.BlockSpec((D_MODEL, FFN_TF), lambda i, j: (0, j + n_f)),
                  pl.BlockSpec((FFN_TF, D_MODEL), lambda i, j: (j, 0)), vec],
        out_specs=[row(D_MODEL),
                   pl.BlockSpec((tm, FFN_TF), lambda i, j: (i, j)),
                   pl.BlockSpec((tm, FFN_TF), lambda i, j: (i, j)),
                   row(D_MODEL), row(D_MODEL)],
        out_shape=[jax.ShapeDtypeStruct((T, D_MODEL), BF16), jax.ShapeDtypeStruct((T, D_FF), F32),
                   jax.ShapeDtypeStruct((T, D_FF), F32), jax.ShapeDtypeStruct((T, D_MODEL), F32),
                   jax.ShapeDtypeStruct((T, D_MODEL), F32)],
        scratch_shapes=[pltpu.VMEM((tm, D_MODEL), F32)],
        compiler_params=_params(("parallel", "arbitrary")),
    )(h1, g_pre, w_gu, w_gu, w_down, g_post)


def _ple_loss(h2, p, target, w_pp, w_pg):
    T = h2.shape[0]
    tm = ROW_TILE

    def body(h2_ref, p_ref, t_ref, wpp_ref, wpg_ref, dh2_ref, de_ref, dg_ref, loss_ref):
        i = pl.program_id(0)
        h2v = h2_ref[...]
        e = _dot(p_ref[...].astype(BF16), wpp_ref[...])
        sg = _sigmoid(_dot(h2v.astype(BF16), wpg_ref[...]))
        diff = h2v + e * sg - t_ref[...]
        part = jnp.sum(jnp.sum(diff * diff, axis=1, keepdims=True), axis=0, keepdims=True) * (0.5 / D_MODEL)

        @pl.when(i == 0)
        def _():
            loss_ref[...] = jnp.zeros_like(loss_ref)

        loss_ref[...] += jnp.broadcast_to(part, loss_ref.shape)
        dh3 = diff * (1.0 / D_MODEL)
        de_ref[...] = (dh3 * sg).astype(BF16)
        dg = (dh3 * e * sg * (1.0 - sg)).astype(BF16)
        dg_ref[...] = dg
        dh2_ref[...] = dh3 + _dot_nt(dg, wpg_ref[...])

    return pl.pallas_call(
        body, grid=(T // tm,), name="ple_loss",
        in_specs=[_row_spec(tm, D_MODEL), _row_spec(tm, PLE_DIM), _row_spec(tm, D_MODEL),
                  _full_spec((PLE_DIM, D_MODEL)), _full_spec((D_MODEL, D_MODEL))],
        out_specs=[_row_spec(tm, D_MODEL)] * 3 + [_full_spec((8, 128))],
        out_shape=[jax.ShapeDtypeStruct((T, D_MODEL), F32), jax.ShapeDtypeStruct((T, D_MODEL), BF16),
                   jax.ShapeDtypeStruct((T, D_MODEL), BF16), jax.ShapeDtypeStruct((8, 128), F32)],
        compiler_params=_params(("arbitrary",)),
    )(h2, p, target, w_pp, w_pg)


def _ffn_bwd(dh2, y, h1, gate, up, g_post, g_pre, w_gu, w_down):
    T = h1.shape[0]
    tm = 2 * ROW_TILE
    n_f = D_FF // FFN_TF

    def body(dh2_ref, y_ref, h1_ref, gate_ref, up_ref, gpost_ref, gpre_ref, wg_ref, wu_ref, wd_ref,
             dy_ref, a_ref, dgate_ref, dup_ref, dh1_ref, dgpost_ref, dgpre_ref, du2_ref):
        i = pl.program_id(0)
        j = pl.program_id(1)

        @pl.when((i == 0) & (j == 0))
        def _():
            dgpost_ref[...] = jnp.zeros_like(dgpost_ref)
            dgpre_ref[...] = jnp.zeros_like(dgpre_ref)

        @pl.when(j == 0)
        def _():
            yv = y_ref[...]
            dy, gterm = _rms_bwd(yv, _rstd(yv), gpost_ref[...], dh2_ref[...])
            dy_ref[...] = dy.astype(BF16)
            dgpost_ref[...] += _colsum(gterm)
            du2_ref[...] = jnp.zeros_like(du2_ref)

        da = _dot_nt(dy_ref[...], wd_ref[...])
        gate = gate_ref[...]
        up = up_ref[...]
        s = _sigmoid(gate)
        sil = gate * s
        a_ref[...] = (sil * up).astype(BF16)
        dgate = (da * up * (s * (1.0 + gate * (1.0 - s)))).astype(BF16)
        dup = (da * sil).astype(BF16)
        dgate_ref[...] = dgate
        dup_ref[...] = dup
        du2_ref[...] += _dot_nt(dgate, wg_ref[...]) + _dot_nt(dup, wu_ref[...])

        @pl.when(j == n_f - 1)
        def _():
            hv = h1_ref[...]
            dx, gterm = _rms_bwd(hv, _rstd(hv), gpre_ref[...], du2_ref[...])
            dh1_ref[...] = dh2_ref[...] + dx
            dgpre_ref[...] += _colsum(gterm)

    row = lambda width: pl.BlockSpec((tm, width), lambda i, j: (i, 0))
    col = pl.BlockSpec((tm, FFN_TF), lambda i, j: (i, j))
    vec = pl.BlockSpec((1, D_MODEL), lambda i, j: (0, 0))
    return pl.pallas_call(
        body, grid=(T // tm, n_f), name="ffn_bwd",
        in_specs=[row(D_MODEL), row(D_MODEL), row(D_MODEL), col, col, vec, vec,
                  pl.BlockSpec((D_MODEL, FFN_TF), lambda i, j: (0, j)),
                  pl.BlockSpec((D_MODEL, FFN_TF), lambda i, j: (0, j + n_f)),
                  pl.BlockSpec((FFN_TF, D_MODEL), lambda i, j: (j, 0))],
        out_specs=[row(D_MODEL), col, col, col, row(D_MODEL), vec, vec],
        out_shape=[jax.ShapeDtypeStruct((T, D_MODEL), BF16), jax.ShapeDtypeStruct((T, D_FF), BF16),
                   jax.ShapeDtypeStruct((T, D_FF), BF16), jax.ShapeDtypeStruct((T, D_FF), BF16),
                   jax.ShapeDtypeStruct((T, D_MODEL), F32), jax.ShapeDtypeStruct((1, D_MODEL), F32),
                   jax.ShapeDtypeStruct((1, D_MODEL), F32)],
        scratch_shapes=[pltpu.VMEM((tm, D_MODEL), F32)],
        compiler_params=_params(("arbitrary", "arbitrary")),
    )(dh2, y, h1, gate, up, g_post, g_pre, w_gu, w_gu, w_down)


def _out_bwd(dh1, mix, o_hg, proj, o_sb, g_hg, g_sb, g_post, w):
    T = dh1.shape[0]
    tm = ROW_TILE

    def body(dh1_ref, mix_ref, ohg_ref, hg_ref, osb_ref, ghg_ref, gsb_ref, gpost_ref, w_ref,
             dmix_ref, dohg_ref, dhg_ref, dosb_ref, dgpost_ref, dghg_ref, dgsb_ref):
        i = pl.program_id(0)

        @pl.when(i == 0)
        def _():
            dgpost_ref[...] = jnp.zeros_like(dgpost_ref)
            dghg_ref[...] = jnp.zeros_like(dghg_ref)
            dgsb_ref[...] = jnp.zeros_like(dgsb_ref)

        mix = mix_ref[...]
        dmix, gterm = _rms_bwd(mix, _rstd(mix), gpost_ref[...], dh1_ref[...])
        dgpost_ref[...] += _colsum(gterm)
        dmix = dmix.astype(BF16)
        dmix_ref[...] = dmix
        dcat = _dot_nt(dmix, w_ref[...])
        da = dcat[:, :HG_WIDTH]
        dn2 = dcat[:, HG_WIDTH:]
        ohg = ohg_ref[...]
        r1 = _rstd(ohg)
        hg = hg_ref[...]
        s = _sigmoid(hg)
        dhg_ref[...] = da * (ohg * r1 * ghg_ref[...]) * (s * (1.0 + hg * (1.0 - s)))
        dohg, gterm = _rms_bwd(ohg, r1, ghg_ref[...], da * (hg * s))
        dohg_ref[...] = dohg
        dghg_ref[...] += _colsum(gterm)
        osb = osb_ref[...]
        dosb, gterm = _rms_bwd(osb, _rstd(osb), gsb_ref[...], dn2)
        dosb_ref[...] = dosb
        dgsb_ref[...] += _colsum(gterm)

    return pl.pallas_call(
        body, grid=(T // tm,), name="out_bwd",
        in_specs=[_row_spec(tm, D_MODEL), _row_spec(tm, D_MODEL), _row_spec(tm, HG_WIDTH), _row_spec(tm, HG_WIDTH, 3),
                  _row_spec(tm, SB_WIDTH), _full_spec((1, HG_WIDTH)), _full_spec((1, SB_WIDTH)),
                  _full_spec((1, D_MODEL)), _full_spec((D_MODEL, D_MODEL))],
        out_specs=[_row_spec(tm, D_MODEL), _row_spec(tm, HG_WIDTH), _row_spec(tm, HG_WIDTH), _row_spec(tm, SB_WIDTH),
                   _full_spec((1, D_MODEL)), _full_spec((1, HG_WIDTH)), _full_spec((1, SB_WIDTH))],
        out_shape=[jax.ShapeDtypeStruct((T, D_MODEL), BF16), jax.ShapeDtypeStruct((T, HG_WIDTH), F32),
                   jax.ShapeDtypeStruct((T, HG_WIDTH), F32), jax.ShapeDtypeStruct((T, SB_WIDTH), F32),
                   jax.ShapeDtypeStruct((1, D_MODEL), F32), jax.ShapeDtypeStruct((1, HG_WIDTH), F32),
                   jax.ShapeDtypeStruct((1, SB_WIDTH), F32)],
        compiler_params=_params(("arbitrary",)),
    )(dh1, mix, o_hg, proj, o_sb, g_hg, g_sb, g_post, w)


def _in_bwd(d_hgrn, d_hg, d_sq, d_sk, d_sv, x, dh1, g_pre, w):
    T = x.shape[0]
    tm = ROW_TILE

    def body(dh_ref, dhg_ref, dsq_ref, dsk_ref, dsv_ref, x_ref, dh1_ref, gpre_ref, w_ref, dproj_ref, dx_ref, dgpre_ref):
        i = pl.program_id(0)

        @pl.when(i == 0)
        def _():
            dgpre_ref[...] = jnp.zeros_like(dgpre_ref)

        dp = jnp.concatenate([dh_ref[...], dhg_ref[...], dsq_ref[...], dsk_ref[...], dsv_ref[...]], axis=1).astype(BF16)
        dproj_ref[...] = dp
        du = _dot_nt(dp, w_ref[...])
        xv = x_ref[...]
        dx, gterm = _rms_bwd(xv, _rstd(xv), gpre_ref[...], du)
        dx_ref[...] = dh1_ref[...] + dx
        dgpre_ref[...] += _colsum(gterm)

    return pl.pallas_call(
        body, grid=(T // tm,), name="in_bwd",
        in_specs=[_row_spec(tm, 3 * HG_WIDTH), _row_spec(tm, HG_WIDTH), _row_spec(tm, SB_WIDTH), _row_spec(tm, SB_WIDTH),
                  _row_spec(tm, SB_WIDTH), _row_spec(tm, D_MODEL), _row_spec(tm, D_MODEL), _full_spec((1, D_MODEL)),
                  _full_spec((D_MODEL, IN_COLS))],
        out_specs=[_row_spec(tm, IN_COLS), _row_spec(tm, D_MODEL), _full_spec((1, D_MODEL))],
        out_shape=[jax.ShapeDtypeStruct((T, IN_COLS), BF16), jax.ShapeDtypeStruct((T, D_MODEL), F32),
                   jax.ShapeDtypeStruct((1, D_MODEL), F32)],
        compiler_params=_params(("arbitrary",)),
    )(d_hgrn, d_hg, d_sq, d_sk, d_sv, x, dh1, g_pre, w)


def _wgrad(a, b, name):
    T, K = a.shape
    N = b.shape[1]
    tk = K if K * N * 4 <= WGRAD_ACC_BYTES else K // 2
    assert K % tk == 0 and tk % 128 == 0
    tt = 512
    n_t = T // tt

    def body(a_ref, b_ref, o_ref, acc_ref):
        t = pl.program_id(1)

        @pl.when(t == 0)
        def _():
            acc_ref[...] = jnp.zeros_like(acc_ref)

        acc_ref[...] += _dot_tn(a_ref[...].astype(BF16), b_ref[...].astype(BF16))

        @pl.when(t == n_t - 1)
        def _():
            o_ref[...] = acc_ref[...].astype(BF16)

    return pl.pallas_call(
        body, grid=(K // tk, n_t), name=name,
        in_specs=[pl.BlockSpec((tt, tk), lambda k, t: (t, k)), pl.BlockSpec((tt, N), lambda k, t: (t, 0))],
        out_specs=pl.BlockSpec((tk, N), lambda k, t: (k, 0)),
        out_shape=jax.ShapeDtypeStruct((K, N), BF16),
        scratch_shapes=[pltpu.VMEM((tk, N), F32)],
        compiler_params=_params(("parallel", "arbitrary")),
    )(a, b)


def _hg_gates(hq, hf, gam):
    g0 = gam[0:1, :]
    g1 = gam[1:2, :]
    mx = jnp.maximum(g0, g1)
    e0 = jnp.exp(g0 - mx)
    e1 = jnp.exp(g1 - mx)
    lb = e0 / (e0 + e1)
    s = _sigmoid(hf)
    f = lb + (1.0 - lb) * s
    sq = _sigmoid(hq)
    return hq * sq, sq, s, f, (1.0 - lb) * (1.0 - s), jnp.log(f), lb


def _prefix_in_sub(a, rowmod):
    n = a.shape[0]
    sh = 1
    while sh < HG_SUB:
        a = a + jnp.where(rowmod >= sh, pltpu.roll(a, sh, axis=0), 0.0)
        sh *= 2
    return a


def _suffix_in_sub(a, rowmod):
    n = a.shape[0]
    sh = 1
    while sh < HG_SUB:
        a = a + jnp.where(rowmod < HG_SUB - sh, pltpu.roll(a, n - sh, axis=0), 0.0)
        sh *= 2
    return a


def _hg_fwd(proj, gam):
    T = proj.shape[0]
    tb = HG_TILE
    n_sub = tb // HG_SUB

    def body(hq_ref, hf_ref, hi_ref, gam_ref, o_ref, hist_ref, st_ref, qt_ref, kk_ref, b_ref, od_ref):
        i = pl.program_id(0)

        @pl.when(i == 0)
        def _():
            st_ref[...] = jnp.zeros_like(st_ref)

        rowmod_w = lax.broadcasted_iota(jnp.int32, (tb, HG_WIDTH), 0) % HG_SUB
        rowmod = lax.broadcasted_iota(jnp.int32, (tb, 1), 0) % HG_SUB
        q, _, _, _, kk, lf, _ = _hg_gates(hq_ref[...], hf_ref[...], gam_ref[...])
        b = _prefix_in_sub(lf, rowmod_w)
        qt_ref[...] = (q * jnp.exp(b)).astype(BF16)
        kk_ref[...] = kk
        b_ref[...] = b

        for h in range(HG_HEADS):
            cs = slice(h * HG_DK, (h + 1) * HG_DK)
            qh = q[:, cs]
            kh = kk[:, cs]
            bh = b[:, cs]
            vh = hi_ref[:, cs]
            acc = jnp.sum(qh * kh, axis=1, keepdims=True) * vh
            for d in range(1, HG_SUB):
                e = qh * pltpu.roll(kh, d, axis=0) * jnp.exp(bh - pltpu.roll(bh, d, axis=0))
                sc = jnp.where(rowmod >= d, jnp.sum(e, axis=1, keepdims=True), 0.0)
                acc = acc + sc * pltpu.roll(vh, d, axis=0)
            od_ref[:, cs] = acc

        for m in range(n_sub):
            rs = slice(m * HG_SUB, (m + 1) * HG_SUB)
            for h in range(HG_HEADS):
                cs = slice(h * HG_DK, (h + 1) * HG_DK)
                st = st_ref[h]
                hist_ref[m, h] = st
                o_int = _dot_nt(qt_ref[rs, cs], st.astype(BF16))
                bs = b_ref[rs, cs]
                bl = bs[HG_SUB - 1:HG_SUB, :]
                kt = (kk_ref[rs, cs] * jnp.exp(bl - bs)).astype(BF16)
                st_ref[h] = st * jnp.exp(bl) + _dot_tn(hi_ref[rs, cs].astype(BF16), kt)
                o_ref[rs, cs] = o_int + od_ref[rs, cs]

    col = lambda c: pl.BlockSpec((tb, HG_WIDTH), lambda i, c=c: (i, c))
    return pl.pallas_call(
        body, grid=(T // tb,), name="hg_fwd",
        in_specs=[col(0), col(1), col(2), _full_spec((2, HG_WIDTH))],
        out_specs=[pl.BlockSpec((tb, HG_WIDTH), lambda i: (i, 0)),
                   pl.BlockSpec((n_sub, HG_HEADS, HG_DK, HG_DK), lambda i: (i, 0, 0, 0))],
        out_shape=[jax.ShapeDtypeStruct((T, HG_WIDTH), F32),
                   jax.ShapeDtypeStruct((T // HG_SUB, HG_HEADS, HG_DK, HG_DK), F32)],
        scratch_shapes=[pltpu.VMEM((HG_HEADS, HG_DK, HG_DK), F32), pltpu.VMEM((tb, HG_WIDTH), BF16),
                        pltpu.VMEM((tb, HG_WIDTH), F32), pltpu.VMEM((tb, HG_WIDTH), F32),
                        pltpu.VMEM((tb, HG_WIDTH), F32)],
        compiler_params=_params(("arbitrary",)),
    )(proj, proj, proj, gam)


def _hg_bwd(proj, gam, d_o, hist):
    T = proj.shape[0]
    tb = HG_TILE
    n_sub = tb // HG_SUB
    n_t = T // tb

    def body(hq_ref, hf_ref, hi_ref, gam_ref, do_ref, hist_ref, dout_ref, dlb_ref,
             dst_ref, q_ref, kk_ref, b_ref, dq_ref, dkk_ref, db_ref, dv_ref):
        i = pl.program_id(0)

        @pl.when(i == 0)
        def _():
            dst_ref[...] = jnp.zeros_like(dst_ref)
            dlb_ref[...] = jnp.zeros_like(dlb_ref)

        rowmod_w = lax.broadcasted_iota(jnp.int32, (tb, HG_WIDTH), 0) % HG_SUB
        rowmod = lax.broadcasted_iota(jnp.int32, (tb, 1), 0) % HG_SUB
        last_row = lax.broadcasted_iota(jnp.int32, (HG_SUB, 1), 0) == HG_SUB - 1
        hq = hq_ref[...]
        q, sq, s, f, kk, lf, lb = _hg_gates(hq, hf_ref[...], gam_ref[...])
        b = _prefix_in_sub(lf, rowmod_w)
        q_ref[...] = q
        kk_ref[...] = kk
        b_ref[...] = b

        for m in reversed(range(n_sub)):
            rs = slice(m * HG_SUB, (m + 1) * HG_SUB)
            for h in range(HG_HEADS):
                cs = slice(h * HG_DK, (h + 1) * HG_DK)
                st = hist_ref[m, h]
                dst = dst_ref[h]
                dstb = dst.astype(BF16)
                dos = do_ref[rs, cs].astype(BF16)
                vs = hi_ref[rs, cs].astype(BF16)
                bs = b_ref[rs, cs]
                bl = bs[HG_SUB - 1:HG_SUB, :]
                ebl = jnp.exp(bl)
                qt = q_ref[rs, cs] * jnp.exp(bs)
                decay = jnp.exp(bl - bs)
                kt = kk_ref[rs, cs] * decay
                dqt = _dot(dos, st.astype(BF16))
                dkt = _dot(vs, dstb)
                dv_ref[rs, cs] = _dot_nt(kt.astype(BF16), dstb)
                dst_ref[h] = dst * ebl + _dot_tn(dos, qt.astype(BF16))
                dq_ref[rs, cs] = dqt * jnp.exp(bs)
                dkk_ref[rs, cs] = dkt * decay
                dktk = dkt * kt
                dbl = _colsum(dktk) + _colsum(dst * st) * ebl
                db_ref[rs, cs] = dqt * qt - dktk + jnp.where(last_row, dbl, 0.0)

        for h in range(HG_HEADS):
            cs = slice(h * HG_DK, (h + 1) * HG_DK)
            qh = q[:, cs]
            kh = kk[:, cs]
            bh = b[:, cs]
            vh = hi_ref[:, cs]
            doh = do_ref[:, cs]
            dsc = jnp.sum(doh * vh, axis=1, keepdims=True)
            sc = jnp.sum(qh * kh, axis=1, keepdims=True)
            dqh = dsc * kh
            dkh = dsc * qh
            dvh = sc * doh
            dbh = jnp.zeros_like(qh)
            for d in range(1, HG_SUB):
                back = tb - d
                valid = rowmod >= d
                kd = pltpu.roll(kh, d, axis=0)
                ex = jnp.where(valid, jnp.exp(bh - pltpu.roll(bh, d, axis=0)), 0.0)
                pd = kd * ex
                sc = jnp.sum(qh * pd, axis=1, keepdims=True)
                dsc = jnp.where(valid, jnp.sum(doh * pltpu.roll(vh, d, axis=0), axis=1, keepdims=True), 0.0)
                dqh = dqh + dsc * pd
                w = dsc * qh * pd
                dkh = dkh + pltpu.roll(dsc * qh * ex, back, axis=0)
                dvh = dvh + pltpu.roll(sc * doh, back, axis=0)
                dbh = dbh + w - pltpu.roll(w, back, axis=0)
            dq_ref[:, cs] += dqh
            dkk_ref[:, cs] += dkh
            dv_ref[:, cs] += dvh
            db_ref[:, cs] += dbh

        dlf = _suffix_in_sub(db_ref[...], rowmod_w)
        df = dlf / f - dkk_ref[...]
        dout_ref[:, 0:HG_WIDTH] = dq_ref[...] * (sq * (1.0 + hq * (1.0 - sq)))
        dout_ref[:, HG_WIDTH:2 * HG_WIDTH] = df * (1.0 - lb) * (s * (1.0 - s))
        dout_ref[:, 2 * HG_WIDTH:3 * HG_WIDTH] = dv_ref[...]
        dlb_ref[...] += _colsum(df * (1.0 - s))

    col = lambda c: pl.BlockSpec((tb, HG_WIDTH), lambda i, c=c: (n_t - 1 - i, c))
    return pl.pallas_call(
        body, grid=(n_t,), name="hg_bwd",
        in_specs=[col(0), col(1), col(2), _full_spec((2, HG_WIDTH)),
                  pl.BlockSpec((tb, HG_WIDTH), lambda i: (n_t - 1 - i, 0)),
                  pl.BlockSpec((n_sub, HG_HEADS, HG_DK, HG_DK), lambda i: (n_t - 1 - i, 0, 0, 0))],
        out_specs=[pl.BlockSpec((tb, 3 * HG_WIDTH), lambda i: (n_t - 1 - i, 0)), _full_spec((1, HG_WIDTH))],
        out_shape=[jax.ShapeDtypeStruct((T, 3 * HG_WIDTH), F32), jax.ShapeDtypeStruct((1, HG_WIDTH), F32)],
        scratch_shapes=[pltpu.VMEM((HG_HEADS, HG_DK, HG_DK), F32)] + [pltpu.VMEM((tb, HG_WIDTH), F32)] * 7,
        compiler_params=_params(("arbitrary",)),
    )(proj, proj, proj, gam, d_o, hist)


def _sb_masks():
    row = lax.broadcasted_iota(jnp.int32, (SB_BLOCK, SB_BLOCK), 0)
    col = lax.broadcasted_iota(jnp.int32, (SB_BLOCK, SB_BLOCK), 1)
    suffix = (row >= col).astype(BF16)
    prefix = (row <= col).astype(BF16)
    query = lax.broadcasted_iota(jnp.int32, (SB_TQ, SB_BLOCK), 0)
    lane = lax.broadcasted_iota(jnp.int32, (SB_TQ, SB_BLOCK), 1)
    causal = [lane + r * SB_BLOCK < query for r in range(SB_TQ // SB_BLOCK)]
    twice = lambda m: jnp.concatenate([m, m], axis=0)
    return twice(suffix), twice(prefix), causal, lane


def _sum_right(a, suffix2):
    hi = a.astype(BF16)
    lo = (a - hi.astype(F32)).astype(BF16)
    return _dot(jnp.concatenate([hi, lo], axis=1), suffix2)


def _sb_block(qb, kj, suffix, causal, c):
    z = _dot_nt(qb, kj)
    sp = jnp.maximum(z, 0.0) + jnp.log(1.0 + jnp.exp(-jnp.abs(z)))
    if causal is not None:
        sp = jnp.where(causal, sp, 0.0)
    big_l = _sum_right(sp, suffix)
    a = jnp.exp(z - big_l - c)
    if causal is not None:
        a = jnp.where(causal, a, 0.0)
    return z, a, big_l


def _half_masked(pair, lane, scale=1.0):
    pair = pair.astype(F32) * scale
    return jnp.where(lane < SB_DH, pair, 0.0).astype(BF16), jnp.where(lane < SB_DH, 0.0, pair).astype(BF16)


def _sb_fwd(qkv):
    T = qkv.shape[0]
    width = SB_PP * SB_BLOCK
    n_chain = 2 * SB_PP
    n_sub = SB_TQ // SB_BLOCK

    def body(q_ref, k_ref, v_ref, o_ref, right_ref, acc_ref, c_ref, qm):
        i = pl.program_id(1)
        suffix, _, causal, lane = _sb_masks()
        for pp in range(SB_PP):
            qm[2 * pp], qm[2 * pp + 1] = _half_masked(q_ref[:, pp * SB_BLOCK:(pp + 1) * SB_BLOCK], lane, SB_DH ** -0.5)
        acc_ref[...] = jnp.zeros_like(acc_ref)
        c_ref[...] = jnp.zeros_like(c_ref)
        right_ref[...] = jnp.zeros_like(right_ref)

        def step(j, r=None):
            rows = pl.ds(pl.multiple_of(j * SB_BLOCK, SB_BLOCK), SB_BLOCK)
            qs = slice(0 if r is None else r * SB_BLOCK, SB_TQ)
            mask = None if r is None else causal[r][qs]
            for ch in range(n_chain):
                pair = slice((ch // 2) * SB_BLOCK, (ch // 2 + 1) * SB_BLOCK)
                c = c_ref[ch, qs]
                _, a, big_l = _sb_block(qm[ch, qs], k_ref[rows, pair], suffix, mask, c)
                acc_ref[ch, qs] += _dot(a.astype(BF16), v_ref[rows, pair])
                cols = slice(ch * SB_BLOCK, (ch + 1) * SB_BLOCK)
                right_ref[qs, cols] = jnp.where(lane[qs] == j, c, right_ref[qs, cols])
                c_ref[ch, qs] = c + jnp.broadcast_to(big_l[:, 0:1], c.shape)

        first = i * n_sub
        for r in reversed(range(n_sub)):
            step(first + r, r)

        @pl.loop(0, first)
        def _(t):
            step(first - 1 - t)

        for pp in range(SB_PP):
            o_ref[:, pp * SB_BLOCK:(pp + 1) * SB_BLOCK] = jnp.where(lane < SB_DH, acc_ref[2 * pp], acc_ref[2 * pp + 1])

    assert T // SB_BLOCK <= SB_BLOCK
    n_g = SB_WIDTH // width
    blk = lambda part: pl.BlockSpec((SB_TQ, width), lambda g, i, part=part: (i, part * n_g + g))
    whole = lambda part: pl.BlockSpec((T, width), lambda g, i, part=part: (0, part * n_g + g))
    return pl.pallas_call(
        body, grid=(n_g, T // SB_TQ), name="sb_fwd",
        in_specs=[blk(0), whole(1), whole(2)],
        out_specs=[blk(0), pl.BlockSpec((SB_TQ, n_chain * SB_BLOCK), lambda g, i: (i, g))],
        out_shape=[jax.ShapeDtypeStruct((T, SB_WIDTH), F32), jax.ShapeDtypeStruct((T, SB_HEADS * SB_BLOCK), F32)],
        scratch_shapes=[pltpu.VMEM((n_chain, SB_TQ, SB_BLOCK), F32), pltpu.VMEM((n_chain, SB_TQ, SB_BLOCK), F32),
                        pltpu.VMEM((n_chain, SB_TQ, SB_BLOCK), BF16)],
        compiler_params=_params(("parallel", "arbitrary")),
    )(qkv, qkv, qkv)


def _sb_bwd(qkv, right, d_o):
    T = qkv.shape[0]
    width = SB_PP * SB_BLOCK
    n_chain = 2 * SB_PP
    n_sub = SB_TQ // SB_BLOCK
    scale = SB_DH ** -0.5

    def body(q_ref, k_ref, v_ref, right_ref, do_ref, dq_ref, dk_ref, dv_ref, acc_ref, gc_ref, qm, dom):
        i = pl.program_id(1)

        @pl.when(i == 0)
        def _():
            dk_ref[...] = jnp.zeros_like(dk_ref)
            dv_ref[...] = jnp.zeros_like(dv_ref)

        suffix, prefix, causal, lane = _sb_masks()
        for pp in range(SB_PP):
            pair = slice(pp * SB_BLOCK, (pp + 1) * SB_BLOCK)
            qm[2 * pp], qm[2 * pp + 1] = _half_masked(q_ref[:, pair], lane, scale)
            dom[2 * pp], dom[2 * pp + 1] = _half_masked(do_ref[:, pair], lane)
        acc_ref[...] = jnp.zeros_like(acc_ref)
        gc_ref[...] = jnp.zeros_like(gc_ref)

        def step(j, r=None):
            rows = pl.ds(pl.multiple_of(j * SB_BLOCK, SB_BLOCK), SB_BLOCK)
            qs = slice(0 if r is None else r * SB_BLOCK, SB_TQ)
            mask = None if r is None else causal[r][qs]
            for pp in range(SB_PP):
                pair = slice(pp * SB_BLOCK, (pp + 1) * SB_BLOCK)
                kj = k_ref[rows, pair]
                vj = v_ref[rows, pair]
                dk = jnp.zeros((SB_BLOCK, SB_BLOCK), F32)
                dv = jnp.zeros((SB_BLOCK, SB_BLOCK), F32)
                for ch in (2 * pp, 2 * pp + 1):
                    cols = slice(ch * SB_BLOCK, (ch + 1) * SB_BLOCK)
                    c = jnp.sum(jnp.where(lane[qs] == j, right_ref[qs, cols], 0.0), axis=1, keepdims=True)
                    z, a, _ = _sb_block(qm[ch, qs], kj, suffix, mask, c)
                    g = a * _dot_nt(dom[ch, qs], vj)
                    g_left = _sum_right(g, prefix)
                    gc = gc_ref[ch, qs]
                    dz = g - _sigmoid(z) * (gc + g_left)
                    if mask is not None:
                        dz = jnp.where(mask, dz, 0.0)
                    dzb = dz.astype(BF16)
                    dk = dk + _dot_tn(dzb, qm[ch, qs])
                    dv = dv + _dot_tn(a.astype(BF16), dom[ch, qs])
                    acc_ref[ch, qs] += _dot(dzb, kj)
                    gc_ref[ch, qs] = gc + jnp.broadcast_to(g_left[:, SB_BLOCK - 1:SB_BLOCK], gc.shape)
                dk_ref[rows, pair] += dk
                dv_ref[rows, pair] += dv

        first = i * n_sub

        @pl.loop(0, first)
        def _(j):
            step(j)

        for r in range(n_sub):
            step(first + r, r)
        for pp in range(SB_PP):
            dq_ref[:, pp * SB_BLOCK:(pp + 1) * SB_BLOCK] = scale * jnp.where(lane < SB_DH, acc_ref[2 * pp],
                                                                             acc_ref[2 * pp + 1])

    n_g = SB_WIDTH // width
    blk = lambda part: pl.BlockSpec((SB_TQ, width), lambda g, i, part=part: (i, part * n_g + g))
    whole = lambda part: pl.BlockSpec((T, width), lambda g, i, part=part: (0, part * n_g + g))
    return pl.pallas_call(
        body, grid=(n_g, T // SB_TQ), name="sb_bwd",
        in_specs=[blk(0), whole(1), whole(2), pl.BlockSpec((SB_TQ, n_chain * SB_BLOCK), lambda g, i: (i, g)), blk(0)],
        out_specs=[blk(0), whole(0), whole(0)],
        out_shape=[jax.ShapeDtypeStruct((T, SB_WIDTH), F32)] * 3,
        scratch_shapes=[pltpu.VMEM((n_chain, SB_TQ, SB_BLOCK), F32), pltpu.VMEM((n_chain, SB_TQ, SB_BLOCK), F32),
                        pltpu.VMEM((n_chain, SB_TQ, SB_BLOCK), BF16), pltpu.VMEM((n_chain, SB_TQ, SB_BLOCK), BF16)],
        compiler_params=_params(("parallel", "arbitrary")),
    )(qkv, qkv, qkv, right, d_o)


HBM = pl.BlockSpec(memory_space=pltpu.HBM)
MESH = pl.DeviceIdType.MESH


def _place():
    return lax.axis_index("x"), lax.axis_index("y"), lax.axis_index("c")


def _all_gather(block):
    rows, cols = block.shape

    def body(x_ref, out_ref, send_sems, recv_sems, local_sem):
        x, y, c = _place()
        me, sibling = (x, y, c), (x, y, 1 - c)
        chips = [(1 - x, y), (x, 1 - y), (1 - x, 1 - y)]

        def slot(px, py, pc):
            return out_ref.at[4 * px + 2 * py + pc]

        def copy(k, blk, to, src=None):
            return pltpu.make_async_remote_copy(
                src_ref=slot(*blk) if src is None else src, dst_ref=slot(*blk),
                send_sem=send_sems.at[k], recv_sem=recv_sems.at[k], device_id=to, device_id_type=MESH)

        mine = pltpu.make_async_copy(x_ref, slot(*me), local_sem)
        mine.start()
        first = [copy(0, me, sibling, src=x_ref)]
        first += [copy(1 + j, me, (*chip, c), src=x_ref) for j, chip in enumerate(chips)]
        for cp in first:
            cp.start()
        passed = [copy(4 + j, (*chip, c), sibling) for j, chip in enumerate(chips)]
        for j, chip in enumerate(chips):
            copy(1 + j, (*chip, c), me).wait_recv()
            passed[j].start()
        copy(0, sibling, me).wait_recv()
        for j, chip in enumerate(chips):
            copy(4 + j, (*chip, 1 - c), me).wait_recv()
        for cp in first + passed:
            cp.wait_send()
        mine.wait()

    return pl.pallas_call(
        body, name="all_gather",
        out_shape=jax.ShapeDtypeStruct((N_DEV, rows, cols), block.dtype),
        in_specs=[HBM], out_specs=HBM,
        scratch_shapes=[pltpu.SemaphoreType.DMA((7,)), pltpu.SemaphoreType.DMA((7,)), pltpu.SemaphoreType.DMA],
    )(block)


SEM = pl.BlockSpec(memory_space=pltpu.SEMAPHORE)
ANY = pl.BlockSpec(memory_space=pl.ANY)
SPLIT_EFFECT = pltpu.SideEffectType.DATAFLOW_SIDE_EFFECTING
N_PEER = N_DEV - 1


def _my_index():
    x, y, c = _place()
    return 4 * x + 2 * y + c


def _flow_copies(kind, src_ref, land_ref, send_sems, recv_sems):
    x, y, c = _place()
    copies = []
    for r in range(1, N_DEV):
        px = 1 - x if r & 4 else x
        py = 1 - y if r & 2 else y
        pc = 1 - c if r & 1 else c
        if kind == "gather":
            src, dst = src_ref, land_ref.at[4 * x + 2 * y + c]
        else:
            src, dst = src_ref.at[4 * px + 2 * py + pc], land_ref.at[r]
        copies.append(pltpu.make_async_remote_copy(
            src_ref=src, dst_ref=dst, send_sem=send_sems.at[r - 1], recv_sem=recv_sems.at[r - 1],
            device_id=(px, py, pc), device_id_type=MESH))
    return copies


def _landing(kind, src, after, name):
    part = src.shape if kind == "gather" else src.shape[1:]

    def body(src_ref, after_ref, land_ref, sem):
        me = _my_index()
        if kind == "gather":
            own = pltpu.make_async_copy(src_ref, land_ref.at[me], sem)
        else:
            own = pltpu.make_async_copy(src_ref.at[me], land_ref.at[0], sem)
        own.start()
        own.wait()

    return pl.pallas_call(
        body, name=name, out_shape=jax.ShapeDtypeStruct((N_DEV,) + part, src.dtype),
        in_specs=[HBM, ANY], out_specs=HBM, scratch_shapes=[pltpu.SemaphoreType.DMA],
    )(src, src if after is None else after)


def _push_start(kinds, arrays, name):
    n = len(arrays)

    def body(*refs):
        ins, sems, token = refs[:n], refs[n:2 * n], refs[3 * n]
        for f, kind in enumerate(kinds):
            for cp in _flow_copies(kind, ins[2 * f], ins[2 * f + 1], sems[2 * f], sems[2 * f + 1]):
                cp.start()
        token[...] = jnp.zeros_like(token)

    outs = pl.pallas_call(
        body, name=name,
        out_shape=[pltpu.SemaphoreType.DMA((N_PEER,))] * n + [pltpu.HBM(a.shape, a.dtype) for a in arrays]
        + [jax.ShapeDtypeStruct((8, 128), F32)],
        in_specs=[HBM] * n, out_specs=[SEM] * n + [HBM] * n + [pl.BlockSpec(memory_space=pltpu.VMEM)],
        input_output_aliases={i: n + i for i in range(n)},
        compiler_params=pltpu.CompilerParams(has_side_effects=SPLIT_EFFECT),
    )(*[pltpu.with_memory_space_constraint(a, pltpu.HBM) for a in arrays])
    return outs[:n], outs[n:2 * n], outs[2 * n]


def _push_wait(kinds, sems, arrays, after, name):
    n = len(arrays)

    def body(*refs):
        ins, sem_refs = refs[:n], refs[n:2 * n]
        for f, kind in enumerate(kinds):
            for cp in _flow_copies(kind, ins[2 * f], ins[2 * f + 1], sem_refs[2 * f], sem_refs[2 * f + 1]):
                cp.wait_send()
                cp.wait_recv()

    outs = pl.pallas_call(
        body, name=name,
        out_shape=[pltpu.HBM(a.shape, a.dtype) for a in arrays],
        in_specs=[HBM] * n + [SEM] * n + [ANY], out_specs=[HBM] * n,
        input_output_aliases={i: i for i in range(n)},
        compiler_params=pltpu.CompilerParams(has_side_effects=SPLIT_EFFECT),
    )(*arrays, *sems, after)
    return outs[1::2]


def _sum_blocks(got, name):
    n, rows, cols = got.shape
    tr = 64
    assert rows % tr == 0

    def body(got_ref, out_ref):
        acc = got_ref[0].astype(F32)
        for r in range(1, n):
            acc = acc + got_ref[r].astype(F32)
        out_ref[...] = acc

    return pl.pallas_call(
        body, grid=(rows // tr,), name=name,
        in_specs=[pl.BlockSpec((n, tr, cols), lambda i: (0, i, 0))],
        out_specs=pl.BlockSpec((tr, cols), lambda i: (i, 0)),
        out_shape=jax.ShapeDtypeStruct((rows, cols), F32),
        compiler_params=_params(("parallel",)),
    )(got)


def _adamw_math(w, g, m, v):
    m = ADAM_B1 * m + (1.0 - ADAM_B1) * g
    v = ADAM_B2 * v + (1.0 - ADAM_B2) * (g * g)
    m_hat = m / (1.0 - ADAM_B1 ** ADAM_STEP)
    v_hat = v / (1.0 - ADAM_B2 ** ADAM_STEP)
    delta = -ADAM_LR * (m_hat / (jnp.sqrt(v_hat) + ADAM_EPS) + ADAM_WD * w)
    return delta, m, v


def _adamw(w, g, m, v, name):
    rows, cols = w.shape
    tr = rows if rows <= 352 else 256

    def body(w_ref, g_ref, m_ref, v_ref, d_out, m_out, v_out):
        d_out[...], m_out[...], v_out[...] = _adamw_math(w_ref[...], g_ref[...], m_ref[...], v_ref[...])

    spec = pl.BlockSpec((tr, cols), lambda i: (i, 0))
    return pl.pallas_call(
        body, grid=(rows // tr,), name=name,
        in_specs=[spec] * 4, out_specs=[spec] * 3,
        out_shape=[jax.ShapeDtypeStruct((rows, cols), F32)] * 3,
        compiler_params=_params(("parallel",)),
    )(w, g, m, v)


ROW_ATTN_PRE, ROW_ATTN_POST, ROW_FFN_PRE, ROW_FFN_POST, ROW_OUT_NORMS, ROW_GAMMA, ROW_LOSS = range(7)


def _small_update(smalls, w, m, v):
    def body(s_ref, w_ref, m_ref, v_ref, g_out, d_out, m_out, v_out):
        g = s_ref[0]
        for j in range(1, N_DEV):
            g = g + s_ref[j]
        wv = w_ref[...]
        g0 = wv[ROW_GAMMA:ROW_GAMMA + 1, 0:HG_WIDTH]
        g1 = wv[ROW_GAMMA:ROW_GAMMA + 1, HG_WIDTH:]
        mx = jnp.maximum(g0, g1)
        e0 = jnp.exp(g0 - mx)
        e1 = jnp.exp(g1 - mx)
        lb = e0 / (e0 + e1)
        dg0 = g[ROW_GAMMA:ROW_GAMMA + 1, 0:HG_WIDTH] * lb * (1.0 - lb)
        row = lax.broadcasted_iota(jnp.int32, g.shape, 0)
        g = jnp.where(row == ROW_GAMMA, jnp.concatenate([dg0, -dg0], axis=1), g)
        g_out[...] = g
        d_out[...], m_out[...], v_out[...] = _adamw_math(wv, g, m_ref[...], v_ref[...])

    return pl.pallas_call(
        body, name="small_update",
        out_shape=[jax.ShapeDtypeStruct((8, D_MODEL), F32)] * 4,
    )(smalls, w, m, v)


def _local_step(x, p, target, g_attn_pre, w_in, gamma, g_hg, g_sb, g_attn_post, g_ffn_pre, g_ffn_post, later_weights,
                grads_ready):
    proj, u, qkv = _in_proj(x, g_attn_pre, w_in)
    o_hg, hist = _hg_fwd(proj, gamma)
    o_sb, right = _sb_fwd(qkv)
    w_out, w_gu, w_down, w_pp, w_pg = later_weights(o_sb)
    cat, mix, h1 = _out_proj(o_hg, proj, o_sb, x, g_hg, g_sb, g_attn_post, w_out)
    u2, gate, up, y, h2 = _ffn_fwd(h1, g_ffn_pre, w_gu, w_down, g_ffn_post)
    dh2, de, dg, loss = _ple_loss(h2, p, target, w_pp, w_pg)

    d_wpp = _wgrad(p, de, "wgrad_ple_proj")
    d_wpg = _wgrad(h2, dg, "wgrad_ple_gate")
    dy, a, dgate, dup, dh1, dg_ffn_post, dg_ffn_pre = _ffn_bwd(dh2, y, h1, gate, up, g_ffn_post, g_ffn_pre, w_gu, w_down)
    d_wdown = _wgrad(a, dy, "wgrad_down")
    grads_ready("ffn", (_wgrad(u2, dgate, "wgrad_gate"), _wgrad(u2, dup, "wgrad_up"), d_wdown, d_wpp, d_wpg))
    dmix, d_ohg, d_hg, d_osb, dg_attn_post, dg_hg, dg_sb = _out_bwd(dh1, mix, o_hg, proj, o_sb, g_hg, g_sb, g_attn_post,
                                                                    w_out)
    grads_ready("out", (_wgrad(cat, dmix, "wgrad_out"),))
    dq, dk, dv = _sb_bwd(qkv, right, d_osb)
    d_hgrn, d_lb = _hg_bwd(proj, gamma, d_ohg, hist)
    dproj, dx, dg_attn_pre = _in_bwd(d_hgrn, d_hg, dq, dk, dv, x, dh1, g_attn_pre, w_in)
    grads_ready("in", (_wgrad(u, dproj, "wgrad_in"),))
    return loss, dx, (dg_attn_pre, dg_attn_post, dg_ffn_pre, dg_ffn_post, dg_hg, dg_sb, d_lb)


def _cols_to_rows(a, n_blocks=N_DEV):
    rows, cols = a.shape
    n = cols // n_blocks
    return a.reshape(rows, n_blocks, n).transpose(1, 0, 2).reshape(n_blocks, rows * n // D_MODEL, D_MODEL)


def _rows_to_cols(a, rows):
    n = a.shape[1] * D_MODEL // rows
    return a.reshape(N_DEV, rows, n).transpose(1, 0, 2).reshape(rows, N_DEV * n)


def _pack_row(*parts):
    return jnp.concatenate([q.reshape(1, -1) for q in parts], axis=1)


def kernel(x, p, attn_pre_norm, w_in, hg_lower_gamma, hg_out_norm, sb_out_norm, w_out, attn_post_norm, ffn_pre_norm, w_gate_up, w_down, ffn_post_norm, ple_proj, ple_gate, loss_target, m_attn_pre_norm, m_w_in, m_hg_lower_gamma, m_hg_out_norm, m_sb_out_norm, m_w_out, m_attn_post_norm, m_ffn_pre_norm, m_w_gate_up, m_w_down, m_ffn_post_norm, m_ple_proj, m_ple_gate, v_attn_pre_norm, v_w_in, v_hg_lower_gamma, v_hg_out_norm, v_sb_out_norm, v_w_out, v_attn_post_norm, v_ffn_pre_norm, v_w_gate_up, v_w_down, v_ffn_post_norm, v_ple_proj, v_ple_gate):
    big = (w_in[0], w_out[0], w_gate_up[0], w_down[0], ple_proj[0], ple_gate[0])
    big_m = (m_w_in[0], m_w_out[0], m_w_gate_up[0], m_w_down[0], m_ple_proj[0], m_ple_gate[0])
    big_v = (v_w_in[0], v_w_out[0], v_w_gate_up[0], v_w_down[0], v_ple_proj[0], v_ple_gate[0])
    names = ("w_in", "w_out", "w_gate_up", "w_down", "ple_proj", "ple_gate")

    shards = [w.astype(BF16).reshape(-1, D_MODEL) for w in big]
    gathered_in = _all_gather(shards[0])
    later_pack = jnp.concatenate(shards[1:], axis=0)
    w_sems, w_thru, token = _push_start(
        ("gather",), (later_pack, _landing("gather", later_pack, gathered_in, "weights_landing")), "weights_start")

    def later_weights(after):
        (got,) = _push_wait(("gather",), w_sems, w_thru, after, "weights_wait")
        part = lambda i: got[:, PACK_OFFS[i] - PACK_SIZES[0]:PACK_OFFS[i] - PACK_SIZES[0] + PACK_SIZES[i], :]
        return (part(1).reshape(D_MODEL, D_MODEL), _rows_to_cols(part(2), D_MODEL), part(3).reshape(D_FF, D_MODEL),
                _rows_to_cols(part(4), PLE_DIM), part(5).reshape(D_MODEL, D_MODEL))

    flights = {}

    def grads_ready(group, grads):
        if group == "ffn":
            d_gate, d_up, d_down, d_pp, d_pg = grads
            half = N_DEV // 2
            blocks = jnp.concatenate([
                jnp.concatenate([_cols_to_rows(d_gate, half), _cols_to_rows(d_up, half)], axis=0),
                d_down.reshape(N_DEV, -1, D_MODEL), _cols_to_rows(d_pp), d_pg.reshape(N_DEV, -1, D_MODEL)], axis=1)
        elif group == "out":
            blocks = grads[0].reshape(N_DEV, -1, D_MODEL)
        else:
            flights[group] = _cols_to_rows(grads[0])
            return
        sems, thru, _ = _push_start(
            ("scatter",), (blocks, _landing("scatter", blocks, None, "grads_" + group + "_landing")),
            "grads_" + group + "_start")
        flights[group] = (sems, thru)

    loss, dx, smalls = _local_step(
        x[0], p[0, 0], loss_target[0], attn_pre_norm + token[0:1, 0:1], _rows_to_cols(gathered_in, D_MODEL),
        hg_lower_gamma, hg_out_norm, sb_out_norm, attn_post_norm, ffn_pre_norm, ffn_post_norm, later_weights, grads_ready)
    dg_attn_pre, dg_attn_post, dg_ffn_pre, dg_ffn_post, dg_hg, dg_sb, d_lb = smalls

    zeros_half = jnp.zeros((1, HG_WIDTH), F32)
    small_pack = jnp.concatenate([
        dg_attn_pre, dg_attn_post, dg_ffn_pre, dg_ffn_post, _pack_row(dg_hg, dg_sb), _pack_row(d_lb, zeros_half),
        jnp.broadcast_to(loss[0:1, 0:1], (1, D_MODEL)), jnp.zeros((1, D_MODEL), F32)], axis=0)
    in_blocks = flights["in"]
    in_sems, in_thru, in_token = _push_start(
        ("scatter", "gather"),
        (in_blocks, _landing("scatter", in_blocks, None, "grads_in_landing"),
         small_pack, _landing("gather", small_pack, None, "smalls_landing")), "grads_in_start")
    (got_ffn,) = _push_wait(("scatter",), *flights["ffn"], in_token, "grads_ffn_wait")
    (got_out,) = _push_wait(("scatter",), *flights["out"], got_ffn, "grads_out_wait")
    got_in, all_smalls = _push_wait(("scatter", "gather"), in_sems, in_thru, got_out, "grads_in_wait")
    sum_ffn = _sum_blocks(got_ffn, "grad_sum_ffn")
    sums = (_sum_blocks(got_in, "grad_sum_in"), _sum_blocks(got_out, "grad_sum_out"),
            sum_ffn[0:PACK_SIZES[2]], sum_ffn[PACK_SIZES[2]:PACK_SIZES[2] + PACK_SIZES[3]],
            sum_ffn[PACK_SIZES[2] + PACK_SIZES[3]:PACK_SIZES[2] + PACK_SIZES[3] + PACK_SIZES[4]],
            sum_ffn[PACK_SIZES[2] + PACK_SIZES[3] + PACK_SIZES[4]:])

    out_g, out_d, out_m, out_v = {}, {}, {}, {}
    for i, name in enumerate(names):
        g = sums[i].reshape(big[i].shape)
        out_g[name] = g[None]
        d, m, v = _adamw(big[i], g, big_m[i], big_v[i], "adamw_" + name)
        out_d[name], out_m[name], out_v[name] = d[None], m[None], v[None]

    def small_rows(pre, gam, hg, sb, post, fpre, fpost):
        return jnp.concatenate([pre, post, fpre, fpost, _pack_row(hg, sb), _pack_row(gam[0], gam[1]),
                                jnp.zeros((2, D_MODEL), F32)], axis=0)

    packs = _small_update(
        all_smalls,
        small_rows(attn_pre_norm, hg_lower_gamma, hg_out_norm, sb_out_norm, attn_post_norm, ffn_pre_norm, ffn_post_norm),
        small_rows(m_attn_pre_norm, m_hg_lower_gamma, m_hg_out_norm, m_sb_out_norm, m_attn_post_norm, m_ffn_pre_norm,
                   m_ffn_post_norm),
        small_rows(v_attn_pre_norm, v_hg_lower_gamma, v_hg_out_norm, v_sb_out_norm, v_attn_post_norm, v_ffn_pre_norm,
                   v_ffn_post_norm))

    def unpack(pk):
        return {
            "attn_pre_norm": pk[ROW_ATTN_PRE:ROW_ATTN_PRE + 1],
            "hg_lower_gamma": pk[ROW_GAMMA].reshape(2, HG_WIDTH),
            "hg_out_norm": pk[ROW_OUT_NORMS:ROW_OUT_NORMS + 1, :HG_WIDTH],
            "sb_out_norm": pk[ROW_OUT_NORMS:ROW_OUT_NORMS + 1, HG_WIDTH:],
            "attn_post_norm": pk[ROW_ATTN_POST:ROW_ATTN_POST + 1],
            "ffn_pre_norm": pk[ROW_FFN_PRE:ROW_FFN_PRE + 1],
            "ffn_post_norm": pk[ROW_FFN_POST:ROW_FFN_POST + 1],
        }

    sg, sd, sm, sv = (unpack(pk) for pk in packs)
    out_g.update(sg), out_d.update(sd), out_m.update(sm), out_v.update(sv)
    order = ("attn_pre_norm", "w_in", "hg_lower_gamma", "hg_out_norm", "sb_out_norm", "w_out", "attn_post_norm",
             "ffn_pre_norm", "w_gate_up", "w_down", "ffn_post_norm", "ple_proj", "ple_gate")
    total_loss = packs[0][ROW_LOSS, 0]
    return (total_loss, dx[None], *[out_g[n] for n in order], *[out_d[n] for n in order],
            *[out_m[n] for n in order], *[out_v[n] for n in order])
```

```python
import functools

import jax
import jax.numpy as jnp
from jax import lax
from jax.experimental import pallas as pl
from jax.experimental.pallas import tpu as pltpu

F32 = jnp.float32
BF16 = jnp.bfloat16

D_MODEL = 1024
HG_WIDTH = 512
HG_HEADS = 4
HG_DK = 128
SB_WIDTH = 512
SB_HEADS = 8
SB_DH = 64
SB_BLOCK = 128
SB_PP = 2
SB_TQ = 512
D_FF = 2816
PLE_DIM = 256
IN_COLS = 4 * HG_WIDTH + 3 * SB_WIDTH
EPS = 1e-6
N_DEV = 8

HG_SUB = 16
HG_TILE = 128
FFN_TF = 256
ROW_TILE = 256
VMEM_LIMIT = 56 * 1024 * 1024
WGRAD_ACC_BYTES = 8 * 1024 * 1024

ADAM_LR = 0.001
ADAM_B1 = 0.9
ADAM_B2 = 0.999
ADAM_EPS = 1e-08
ADAM_WD = 0.01
ADAM_STEP = 10

PACK_SIZES = (448, 128, 704, 352, 32, 128)
PACK_ROWS = sum(PACK_SIZES)
PACK_OFFS = tuple(sum(PACK_SIZES[:i]) for i in range(len(PACK_SIZES)))

NT_DIMS = (((1,), (1,)), ((), ()))
TN_DIMS = (((0,), (0,)), ((), ()))


def _params(semantics):
    return pltpu.CompilerParams(dimension_semantics=semantics, vmem_limit_bytes=VMEM_LIMIT)


def _dot(a, b):
    return jnp.dot(a, b, preferred_element_type=F32)


def _dot_nt(a, b):
    return lax.dot_general(a, b, NT_DIMS, preferred_element_type=F32)


def _dot_tn(a, b):
    return lax.dot_general(a, b, TN_DIMS, preferred_element_type=F32)


def _sigmoid(z):
    return 1.0 / (1.0 + jnp.exp(-z))


def _rstd(xv):
    return lax.rsqrt(jnp.mean(xv * xv, axis=-1, keepdims=True) + EPS)


def _rms_bwd(xv, r, g, dn):
    xn = xv * r
    gh = dn * g
    dx = r * (gh - xn * jnp.mean(gh * xn, axis=-1, keepdims=True))
    return dx, dn * xn


def _colsum(a):
    return jnp.sum(a, axis=0, keepdims=True)


def _row_spec(tm, width, col=0):
    return pl.BlockSpec((tm, width), lambda i, col=col: (i, col))


def _full_spec(shape):
    return pl.BlockSpec(shape, lambda i: (0,) * len(shape))


def _in_proj(x, g, w):
    T = x.shape[0]
    tm = ROW_TILE

    def body(x_ref, g_ref, w_ref, proj_ref, u_ref, qkv_ref):
        xv = x_ref[...]
        u = (xv * _rstd(xv) * g_ref[...]).astype(BF16)
        u_ref[...] = u
        proj = _dot(u, w_ref[...])
        proj_ref[...] = proj
        qkv_ref[...] = proj[:, 4 * HG_WIDTH:].astype(BF16)

    return pl.pallas_call(
        body, grid=(T // tm,), name="in_proj",
        in_specs=[_row_spec(tm, D_MODEL), _full_spec((1, D_MODEL)), _full_spec((D_MODEL, IN_COLS))],
        out_specs=[_row_spec(tm, IN_COLS), _row_spec(tm, D_MODEL), _row_spec(tm, 3 * SB_WIDTH)],
        out_shape=[jax.ShapeDtypeStruct((T, IN_COLS), F32), jax.ShapeDtypeStruct((T, D_MODEL), BF16),
                   jax.ShapeDtypeStruct((T, 3 * SB_WIDTH), BF16)],
        compiler_params=_params(("parallel",)),
    )(x, g, w)


def _out_proj(o_hg, proj, o_sb, x, g_hg, g_sb, g_post, w):
    T = x.shape[0]
    tm = ROW_TILE

    def body(ohg_ref, hg_ref, osb_ref, x_ref, ghg_ref, gsb_ref, gpost_ref, w_ref, cat_ref, mix_ref, h1_ref):
        ohg = ohg_ref[...]
        hg = hg_ref[...]
        osb = osb_ref[...]
        a = ohg * _rstd(ohg) * ghg_ref[...] * (hg * _sigmoid(hg))
        n2 = osb * _rstd(osb) * gsb_ref[...]
        cat = jnp.concatenate([a, n2], axis=1).astype(BF16)
        cat_ref[...] = cat
        mix = _dot(cat, w_ref[...])
        mix_ref[...] = mix
        h1_ref[...] = x_ref[...] + mix * _rstd(mix) * gpost_ref[...]

    return pl.pallas_call(
        body, grid=(T // tm,), name="out_proj",
        in_specs=[_row_spec(tm, HG_WIDTH), _row_spec(tm, HG_WIDTH, 3), _row_spec(tm, SB_WIDTH), _row_spec(tm, D_MODEL),
                  _full_spec((1, HG_WIDTH)), _full_spec((1, SB_WIDTH)), _full_spec((1, D_MODEL)),
                  _full_spec((D_MODEL, D_MODEL))],
        out_specs=[_row_spec(tm, D_MODEL)] * 3,
        out_shape=[jax.ShapeDtypeStruct((T, D_MODEL), BF16), jax.ShapeDtypeStruct((T, D_MODEL), F32),
                   jax.ShapeDtypeStruct((T, D_MODEL), F32)],
        compiler_params=_params(("parallel",)),
    )(o_hg, proj, o_sb, x, g_hg, g_sb, g_post, w)


def _ffn_fwd(h1, g_pre, w_gu, w_down, g_post):
    T = h1.shape[0]
    tm = 2 * ROW_TILE
    n_f = D_FF // FFN_TF

    def body(h1_ref, gpre_ref, wg_ref, wu_ref, wd_ref, gpost_ref, u2_ref, gate_ref, up_ref, y_ref, h2_ref, acc_ref):
        j = pl.program_id(1)

        @pl.when(j == 0)
        def _():
            hv = h1_ref[...]
            u2_ref[...] = (hv * _rstd(hv) * gpre_ref[...]).astype(BF16)
            acc_ref[...] = jnp.zeros_like(acc_ref)

        u2 = u2_ref[...]
        gate = _dot(u2, wg_ref[...])
        up = _dot(u2, wu_ref[...])
        gate_ref[...] = gate
        up_ref[...] = up
        a = (gate * _sigmoid(gate) * up).astype(BF16)
        acc_ref[...] += _dot(a, wd_ref[...])

        @pl.when(j == n_f - 1)
        def _():
            y = acc_ref[...]
            y_ref[...] = y
            h2_ref[...] = h1_ref[...] + y * _rstd(y) * gpost_ref[...]

    row = lambda width: pl.BlockSpec((tm, width), lambda i, j: (i, 0))
    vec = pl.BlockSpec((1, D_MODEL), lambda i, j: (0, 0))
    return pl.pallas_call(
        body, grid=(T // tm, n_f), name="ffn_fwd",
        in_specs=[row(D_MODEL), vec,
                  pl.BlockSpec((D_MODEL, FFN_TF), lambda i, j: (0, j)),
                  pl.BlockSpec((D_MODEL, FFN_TF), lambda i, j: (0, j + n_f)),
                  pl.BlockSpec((FFN_TF, D_MODEL), lambda i, j: (j, 0)), vec],
        out_specs=[row(D_MODEL),
                   pl.BlockSpec((tm, FFN_TF), lambda i, j: (i, j)),
                   pl.BlockSpec((tm, FFN_TF), lambda i, j: (i, j)),
                   row(D_MODEL), row(D_MODEL)],
        out_shape=[jax.ShapeDtypeStruct((T, D_MODEL), BF16), jax.ShapeDtypeStruct((T, D_FF), F32),
                   jax.ShapeDtypeStruct((T, D_FF), F32), jax.ShapeDtypeStruct((T, D_MODEL), F32),
                   jax.ShapeDtypeStruct((T, D_MODEL), F32)],
        scratch_shapes=[pltpu.VMEM((tm, D_MODEL), F32)],
        compiler_params=_params(("parallel", "arbitrary")),
    )(h1, g_pre, w_gu, w_gu, w_down, g_post)


def _ple_loss(h2, p, target, w_pp, w_pg):
    T = h2.shape[0]
    tm = ROW_TILE

    def body(h2_ref, p_ref, t_ref, wpp_ref, wpg_ref, dh2_ref, de_ref, dg_ref, loss_ref):
        i = pl.program_id(0)
        h2v = h2_ref[...]
        e = _dot(p_ref[...].astype(BF16), wpp_ref[...])
        sg = _sigmoid(_dot(h2v.astype(BF16), wpg_ref[...]))
        diff = h2v + e * sg - t_ref[...]
        part = jnp.sum(jnp.sum(diff * diff, axis=1, keepdims=True), axis=0, keepdims=True) * (0.5 / D_MODEL)

        @pl.when(i == 0)
        def _():
            loss_ref[...] = jnp.zeros_like(loss_ref)

        loss_ref[...] += jnp.broadcast_to(part, loss_ref.shape)
        dh3 = diff * (1.0 / D_MODEL)
        de_ref[...] = (dh3 * sg).astype(BF16)
        dg = (dh3 * e * sg * (1.0 - sg)).astype(BF16)
        dg_ref[...] = dg
        dh2_ref[...] = dh3 + _dot_nt(dg, wpg_ref[...])

    return pl.pallas_call(
        body, grid=(T // tm,), name="ple_loss",
        in_specs=[_row_spec(tm, D_MODEL), _row_spec(tm, PLE_DIM), _row_spec(tm, D_MODEL),
                  _full_spec((PLE_DIM, D_MODEL)), _full_spec((D_MODEL, D_MODEL))],
        out_specs=[_row_spec(tm, D_MODEL)] * 3 + [_full_spec((8, 128))],
        out_shape=[jax.ShapeDtypeStruct((T, D_MODEL), F32), jax.ShapeDtypeStruct((T, D_MODEL), BF16),
                   jax.ShapeDtypeStruct((T, D_MODEL), BF16), jax.ShapeDtypeStruct((8, 128), F32)],
        compiler_params=_params(("arbitrary",)),
    )(h2, p, target, w_pp, w_pg)


def _ffn_bwd(dh2, y, h1, gate, up, g_post, g_pre, w_gu, w_down):
    T = h1.shape[0]
    tm = 2 * ROW_TILE
    n_f = D_FF // FFN_TF

    def body(dh2_ref, y_ref, h1_ref, gate_ref, up_ref, gpost_ref, gpre_ref, wg_ref, wu_ref, wd_ref,
             dy_ref, a_ref, dgate_ref, dup_ref, dh1_ref, dgpost_ref, dgpre_ref, du2_ref):
        i = pl.program_id(0)
        j = pl.program_id(1)

        @pl.when((i == 0) & (j == 0))
        def _():
            dgpost_ref[...] = jnp.zeros_like(dgpost_ref)
            dgpre_ref[...] = jnp.zeros_like(dgpre_ref)

        @pl.when(j == 0)
        def _():
            yv = y_ref[...]
            dy, gterm = _rms_bwd(yv, _rstd(yv), gpost_ref[...], dh2_ref[...])
            dy_ref[...] = dy.astype(BF16)
            dgpost_ref[...] += _colsum(gterm)
            du2_ref[...] = jnp.zeros_like(du2_ref)

        da = _dot_nt(dy_ref[...], wd_ref[...])
        gate = gate_ref[...]
        up = up_ref[...]
        s = _sigmoid(gate)
        sil = gate * s
        a_ref[...] = (sil * up).astype(BF16)
        dgate = (da * up * (s * (1.0 + gate * (1.0 - s)))).astype(BF16)
        dup = (da * sil).astype(BF16)
        dgate_ref[...] = dgate
        dup_ref[...] = dup
        du2_ref[...] += _dot_nt(dgate, wg_ref[...]) + _dot_nt(dup, wu_ref[...])

        @pl.when(j == n_f - 1)
        def _():
            hv = h1_ref[...]
            dx, gterm = _rms_bwd(hv, _rstd(hv), gpre_ref[...], du2_ref[...])
            dh1_ref[...] = dh2_ref[...] + dx
            dgpre_ref[...] += _colsum(gterm)

    row = lambda width: pl.BlockSpec((tm, width), lambda i, j: (i, 0))
    col = pl.BlockSpec((tm, FFN_TF), lambda i, j: (i, j))
    vec = pl.BlockSpec((1, D_MODEL), lambda i, j: (0, 0))
    return pl.pallas_call(
        body, grid=(T // tm, n_f), name="ffn_bwd",
        in_specs=[row(D_MODEL), row(D_MODEL), row(D_MODEL), col, col, vec, vec,
                  pl.BlockSpec((D_MODEL, FFN_TF), lambda i, j: (0, j)),
                  pl.BlockSpec((D_MODEL, FFN_TF), lambda i, j: (0, j + n_f)),
                  pl.BlockSpec((FFN_TF, D_MODEL), lambda i, j: (j, 0))],
        out_specs=[row(D_MODEL), col, col, col, row(D_MODEL), vec, vec],
        out_shape=[jax.ShapeDtypeStruct((T, D_MODEL), BF16), jax.ShapeDtypeStruct((T, D_FF), BF16),
                   jax.ShapeDtypeStruct((T, D_FF), BF16), jax.ShapeDtypeStruct((T, D_FF), BF16),
                   jax.ShapeDtypeStruct((T, D_MODEL), F32), jax.ShapeDtypeStruct((1, D_MODEL), F32),
                   jax.ShapeDtypeStruct((1, D_MODEL), F32)],
        scratch_shapes=[pltpu.VMEM((tm, D_MODEL), F32)],
        compiler_params=_params(("arbitrary", "arbitrary")),
    )(dh2, y, h1, gate, up, g_post, g_pre, w_gu, w_gu, w_down)


def _out_bwd(dh1, mix, o_hg, proj, o_sb, g_hg, g_sb, g_post, w, after):
    T = dh1.shape[0]
    tm = ROW_TILE

    def body(dh1_ref, mix_ref, ohg_ref, hg_ref, osb_ref, ghg_ref, gsb_ref, gpost_ref, w_ref, after_ref,
             dmix_ref, dohg_ref, dhg_ref, dosb_ref, dgpost_ref, dghg_ref, dgsb_ref):
        i = pl.program_id(0)

        @pl.when(i == 0)
        def _():
            dgpost_ref[...] = jnp.zeros_like(dgpost_ref)
            dghg_ref[...] = jnp.zeros_like(dghg_ref)
            dgsb_ref[...] = jnp.zeros_like(dgsb_ref)

        mix = mix_ref[...]
        dmix, gterm = _rms_bwd(mix, _rstd(mix), gpost_ref[...], dh1_ref[...])
        dgpost_ref[...] += _colsum(gterm)
        dmix = dmix.astype(BF16)
        dmix_ref[...] = dmix
        dcat = _dot_nt(dmix, w_ref[...])
        da = dcat[:, :HG_WIDTH]
        dn2 = dcat[:, HG_WIDTH:]
        ohg = ohg_ref[...]
        r1 = _rstd(ohg)
        hg = hg_ref[...]
        s = _sigmoid(hg)
        dhg_ref[...] = da * (ohg * r1 * ghg_ref[...]) * (s * (1.0 + hg * (1.0 - s)))
        dohg, gterm = _rms_bwd(ohg, r1, ghg_ref[...], da * (hg * s))
        dohg_ref[...] = dohg
        dghg_ref[...] += _colsum(gterm)
        osb = osb_ref[...]
        dosb, gterm = _rms_bwd(osb, _rstd(osb), gsb_ref[...], dn2)
        dosb_ref[...] = dosb
        dgsb_ref[...] += _colsum(gterm)

    return pl.pallas_call(
        body, grid=(T // tm,), name="out_bwd",
        in_specs=[_row_spec(tm, D_MODEL), _row_spec(tm, D_MODEL), _row_spec(tm, HG_WIDTH), _row_spec(tm, HG_WIDTH, 3),
                  _row_spec(tm, SB_WIDTH), _full_spec((1, HG_WIDTH)), _full_spec((1, SB_WIDTH)),
                  _full_spec((1, D_MODEL)), _full_spec((D_MODEL, D_MODEL)), pl.BlockSpec(memory_space=pl.ANY)],
        out_specs=[_row_spec(tm, D_MODEL), _row_spec(tm, HG_WIDTH), _row_spec(tm, HG_WIDTH), _row_spec(tm, SB_WIDTH),
                   _full_spec((1, D_MODEL)), _full_spec((1, HG_WIDTH)), _full_spec((1, SB_WIDTH))],
        out_shape=[jax.ShapeDtypeStruct((T, D_MODEL), BF16), jax.ShapeDtypeStruct((T, HG_WIDTH), F32),
                   jax.ShapeDtypeStruct((T, HG_WIDTH), F32), jax.ShapeDtypeStruct((T, SB_WIDTH), F32),
                   jax.ShapeDtypeStruct((1, D_MODEL), F32), jax.ShapeDtypeStruct((1, HG_WIDTH), F32),
                   jax.ShapeDtypeStruct((1, SB_WIDTH), F32)],
        compiler_params=_params(("arbitrary",)),
    )(dh1, mix, o_hg, proj, o_sb, g_hg, g_sb, g_post, w, after)


def _in_bwd(d_hgrn, d_hg, d_sq, d_sk, d_sv, x, dh1, g_pre, w):
    T = x.shape[0]
    tm = ROW_TILE

    def body(dh_ref, dhg_ref, dsq_ref, dsk_ref, dsv_ref, x_ref, dh1_ref, gpre_ref, w_ref, dproj_ref, dx_ref, dgpre_ref):
        i = pl.program_id(0)

        @pl.when(i == 0)
        def _():
            dgpre_ref[...] = jnp.zeros_like(dgpre_ref)

        dp = jnp.concatenate([dh_ref[...], dhg_ref[...], dsq_ref[...], dsk_ref[...], dsv_ref[...]], axis=1).astype(BF16)
        dproj_ref[...] = dp
        du = _dot_nt(dp, w_ref[...])
        xv = x_ref[...]
        dx, gterm = _rms_bwd(xv, _rstd(xv), gpre_ref[...], du)
        dx_ref[...] = dh1_ref[...] + dx
        dgpre_ref[...] += _colsum(gterm)

    return pl.pallas_call(
        body, grid=(T // tm,), name="in_bwd",
        in_specs=[_row_spec(tm, 3 * HG_WIDTH), _row_spec(tm, HG_WIDTH), _row_spec(tm, SB_WIDTH), _row_spec(tm, SB_WIDTH),
                  _row_spec(tm, SB_WIDTH), _row_spec(tm, D_MODEL), _row_spec(tm, D_MODEL), _full_spec((1, D_MODEL)),
                  _full_spec((D_MODEL, IN_COLS))],
        out_specs=[_row_spec(tm, IN_COLS), _row_spec(tm, D_MODEL), _full_spec((1, D_MODEL))],
        out_shape=[jax.ShapeDtypeStruct((T, IN_COLS), BF16), jax.ShapeDtypeStruct((T, D_MODEL), F32),
                   jax.ShapeDtypeStruct((1, D_MODEL), F32)],
        compiler_params=_params(("arbitrary",)),
    )(d_hgrn, d_hg, d_sq, d_sk, d_sv, x, dh1, g_pre, w)


def _wgrad(a, b, name):
    T, K = a.shape
    N = b.shape[1]
    tk = K if K * N * 4 <= WGRAD_ACC_BYTES else K // 2
    assert K % tk == 0 and tk % 128 == 0
    tt = 512
    n_t = T // tt

    def body(a_ref, b_ref, o_ref, acc_ref):
        t = pl.program_id(1)

        @pl.when(t == 0)
        def _():
            acc_ref[...] = jnp.zeros_like(acc_ref)

        acc_ref[...] += _dot_tn(a_ref[...].astype(BF16), b_ref[...].astype(BF16))

        @pl.when(t == n_t - 1)
        def _():
            o_ref[...] = acc_ref[...].astype(BF16)

    return pl.pallas_call(
        body, grid=(K // tk, n_t), name=name,
        in_specs=[pl.BlockSpec((tt, tk), lambda k, t: (t, k)), pl.BlockSpec((tt, N), lambda k, t: (t, 0))],
        out_specs=pl.BlockSpec((tk, N), lambda k, t: (k, 0)),
        out_shape=jax.ShapeDtypeStruct((K, N), BF16),
        scratch_shapes=[pltpu.VMEM((tk, N), F32)],
        compiler_params=_params(("parallel", "arbitrary")),
    )(a, b)


def _hg_gates(hq, hf, gam):
    g0 = gam[0:1, :]
    g1 = gam[1:2, :]
    mx = jnp.maximum(g0, g1)
    e0 = jnp.exp(g0 - mx)
    e1 = jnp.exp(g1 - mx)
    lb = e0 / (e0 + e1)
    s = _sigmoid(hf)
    f = lb + (1.0 - lb) * s
    sq = _sigmoid(hq)
    return hq * sq, sq, s, f, (1.0 - lb) * (1.0 - s), jnp.log(f), lb


def _prefix_in_sub(a, rowmod):
    n = a.shape[0]
    sh = 1
    while sh < HG_SUB:
        a = a + jnp.where(rowmod >= sh, pltpu.roll(a, sh, axis=0), 0.0)
        sh *= 2
    return a


def _suffix_in_sub(a, rowmod):
    n = a.shape[0]
    sh = 1
    while sh < HG_SUB:
        a = a + jnp.where(rowmod < HG_SUB - sh, pltpu.roll(a, n - sh, axis=0), 0.0)
        sh *= 2
    return a


def _hg_fwd(proj, gam):
    T = proj.shape[0]
    tb = HG_TILE
    n_sub = tb // HG_SUB

    def body(hq_ref, hf_ref, hi_ref, gam_ref, o_ref, hist_ref, st_ref, qt_ref, kk_ref, b_ref, od_ref):
        i = pl.program_id(0)

        @pl.when(i == 0)
        def _():
            st_ref[...] = jnp.zeros_like(st_ref)

        rowmod_w = lax.broadcasted_iota(jnp.int32, (tb, HG_WIDTH), 0) % HG_SUB
        rowmod = lax.broadcasted_iota(jnp.int32, (tb, 1), 0) % HG_SUB
        q, _, _, _, kk, lf, _ = _hg_gates(hq_ref[...], hf_ref[...], gam_ref[...])
        b = _prefix_in_sub(lf, rowmod_w)
        qt_ref[...] = (q * jnp.exp(b)).astype(BF16)
        kk_ref[...] = kk
        b_ref[...] = b

        for h in range(HG_HEADS):
            cs = slice(h * HG_DK, (h + 1) * HG_DK)
            qh = q[:, cs]
            kh = kk[:, cs]
            bh = b[:, cs]
            vh = hi_ref[:, cs]
            acc = jnp.sum(qh * kh, axis=1, keepdims=True) * vh
            for d in range(1, HG_SUB):
                e = qh * pltpu.roll(kh, d, axis=0) * jnp.exp(bh - pltpu.roll(bh, d, axis=0))
                sc = jnp.where(rowmod >= d, jnp.sum(e, axis=1, keepdims=True), 0.0)
                acc = acc + sc * pltpu.roll(vh, d, axis=0)
            od_ref[:, cs] = acc

        for m in range(n_sub):
            rs = slice(m * HG_SUB, (m + 1) * HG_SUB)
            for h in range(HG_HEADS):
                cs = slice(h * HG_DK, (h + 1) * HG_DK)
                st = st_ref[h]
                hist_ref[m, h] = st
                o_int = _dot_nt(qt_ref[rs, cs], st.astype(BF16))
                bs = b_ref[rs, cs]
                bl = bs[HG_SUB - 1:HG_SUB, :]
                kt = (kk_ref[rs, cs] * jnp.exp(bl - bs)).astype(BF16)
                st_ref[h] = st * jnp.exp(bl) + _dot_tn(hi_ref[rs, cs].astype(BF16), kt)
                o_ref[rs, cs] = o_int + od_ref[rs, cs]

    col = lambda c: pl.BlockSpec((tb, HG_WIDTH), lambda i, c=c: (i, c))
    return pl.pallas_call(
        body, grid=(T // tb,), name="hg_fwd",
        in_specs=[col(0), col(1), col(2), _full_spec((2, HG_WIDTH))],
        out_specs=[pl.BlockSpec((tb, HG_WIDTH), lambda i: (i, 0)),
                   pl.BlockSpec((n_sub, HG_HEADS, HG_DK, HG_DK), lambda i: (i, 0, 0, 0))],
        out_shape=[jax.ShapeDtypeStruct((T, HG_WIDTH), F32),
                   jax.ShapeDtypeStruct((T // HG_SUB, HG_HEADS, HG_DK, HG_DK), F32)],
        scratch_shapes=[pltpu.VMEM((HG_HEADS, HG_DK, HG_DK), F32), pltpu.VMEM((tb, HG_WIDTH), BF16),
                        pltpu.VMEM((tb, HG_WIDTH), F32), pltpu.VMEM((tb, HG_WIDTH), F32),
                        pltpu.VMEM((tb, HG_WIDTH), F32)],
        compiler_params=_params(("arbitrary",)),
    )(proj, proj, proj, gam)


def _hg_bwd(proj, gam, d_o, hist):
    T = proj.shape[0]
    tb = HG_TILE
    n_sub = tb // HG_SUB
    n_t = T // tb

    def body(hq_ref, hf_ref, hi_ref, gam_ref, do_ref, hist_ref, dout_ref, dlb_ref,
             dst_ref, q_ref, kk_ref, b_ref, dq_ref, dkk_ref, db_ref, dv_ref):
        i = pl.program_id(0)

        @pl.when(i == 0)
        def _():
            dst_ref[...] = jnp.zeros_like(dst_ref)
            dlb_ref[...] = jnp.zeros_like(dlb_ref)

        rowmod_w = lax.broadcasted_iota(jnp.int32, (tb, HG_WIDTH), 0) % HG_SUB
        rowmod = lax.broadcasted_iota(jnp.int32, (tb, 1), 0) % HG_SUB
        last_row = lax.broadcasted_iota(jnp.int32, (HG_SUB, 1), 0) == HG_SUB - 1
        hq = hq_ref[...]
        q, sq, s, f, kk, lf, lb = _hg_gates(hq, hf_ref[...], gam_ref[...])
        b = _prefix_in_sub(lf, rowmod_w)
        q_ref[...] = q
        kk_ref[...] = kk
        b_ref[...] = b

        for m in reversed(range(n_sub)):
            rs = slice(m * HG_SUB, (m + 1) * HG_SUB)
            for h in range(HG_HEADS):
                cs = slice(h * HG_DK, (h + 1) * HG_DK)
                st = hist_ref[m, h]
                dst = dst_ref[h]
                dstb = dst.astype(BF16)
                dos = do_ref[rs, cs].astype(BF16)
                vs = hi_ref[rs, cs].astype(BF16)
                bs = b_ref[rs, cs]
                bl = bs[HG_SUB - 1:HG_SUB, :]
                ebl = jnp.exp(bl)
                qt = q_ref[rs, cs] * jnp.exp(bs)
                decay = jnp.exp(bl - bs)
                kt = kk_ref[rs, cs] * decay
                dqt = _dot(dos, st.astype(BF16))
                dkt = _dot(vs, dstb)
                dv_ref[rs, cs] = _dot_nt(kt.astype(BF16), dstb)
                dst_ref[h] = dst * ebl + _dot_tn(dos, qt.astype(BF16))
                dq_ref[rs, cs] = dqt * jnp.exp(bs)
                dkk_ref[rs, cs] = dkt * decay
                dktk = dkt * kt
                dbl = _colsum(dktk) + _colsum(dst * st) * ebl
                db_ref[rs, cs] = dqt * qt - dktk + jnp.where(last_row, dbl, 0.0)

        for h in range(HG_HEADS):
            cs = slice(h * HG_DK, (h + 1) * HG_DK)
            qh = q[:, cs]
            kh = kk[:, cs]
            bh = b[:, cs]
            vh = hi_ref[:, cs]
            doh = do_ref[:, cs]
            dsc = jnp.sum(doh * vh, axis=1, keepdims=True)
            sc = jnp.sum(qh * kh, axis=1, keepdims=True)
            dqh = dsc * kh
            dkh = dsc * qh
            dvh = sc * doh
            dbh = jnp.zeros_like(qh)
            for d in range(1, HG_SUB):
                back = tb - d
                valid = rowmod >= d
                kd = pltpu.roll(kh, d, axis=0)
                ex = jnp.where(valid, jnp.exp(bh - pltpu.roll(bh, d, axis=0)), 0.0)
                pd = kd * ex
                sc = jnp.sum(qh * pd, axis=1, keepdims=True)
                dsc = jnp.where(valid, jnp.sum(doh * pltpu.roll(vh, d, axis=0), axis=1, keepdims=True), 0.0)
                dqh = dqh + dsc * pd
                w = dsc * qh * pd
                dkh = dkh + pltpu.roll(dsc * qh * ex, back, axis=0)
                dvh = dvh + pltpu.roll(sc * doh, back, axis=0)
                dbh = dbh + w - pltpu.roll(w, back, axis=0)
            dq_ref[:, cs] += dqh
            dkk_ref[:, cs] += dkh
            dv_ref[:, cs] += dvh
            db_ref[:, cs] += dbh

        dlf = _suffix_in_sub(db_ref[...], rowmod_w)
        df = dlf / f - dkk_ref[...]
        dout_ref[:, 0:HG_WIDTH] = dq_ref[...] * (sq * (1.0 + hq * (1.0 - sq)))
        dout_ref[:, HG_WIDTH:2 * HG_WIDTH] = df * (1.0 - lb) * (s * (1.0 - s))
        dout_ref[:, 2 * HG_WIDTH:3 * HG_WIDTH] = dv_ref[...]
        dlb_ref[...] += _colsum(df * (1.0 - s))

    col = lambda c: pl.BlockSpec((tb, HG_WIDTH), lambda i, c=c: (n_t - 1 - i, c))
    return pl.pallas_call(
        body, grid=(n_t,), name="hg_bwd",
        in_specs=[col(0), col(1), col(2), _full_spec((2, HG_WIDTH)),
                  pl.BlockSpec((tb, HG_WIDTH), lambda i: (n_t - 1 - i, 0)),
                  pl.BlockSpec((n_sub, HG_HEADS, HG_DK, HG_DK), lambda i: (n_t - 1 - i, 0, 0, 0))],
        out_specs=[pl.BlockSpec((tb, 3 * HG_WIDTH), lambda i: (n_t - 1 - i, 0)), _full_spec((1, HG_WIDTH))],
        out_shape=[jax.ShapeDtypeStruct((T, 3 * HG_WIDTH), F32), jax.ShapeDtypeStruct((1, HG_WIDTH), F32)],
        scratch_shapes=[pltpu.VMEM((HG_HEADS, HG_DK, HG_DK), F32)] + [pltpu.VMEM((tb, HG_WIDTH), F32)] * 7,
        compiler_params=_params(("arbitrary",)),
    )(proj, proj, proj, gam, d_o, hist)


def _sb_masks():
    row = lax.broadcasted_iota(jnp.int32, (SB_BLOCK, SB_BLOCK), 0)
    col = lax.broadcasted_iota(jnp.int32, (SB_BLOCK, SB_BLOCK), 1)
    suffix = (row >= col).astype(BF16)
    prefix = (row <= col).astype(BF16)
    query = lax.broadcasted_iota(jnp.int32, (SB_TQ, SB_BLOCK), 0)
    lane = lax.broadcasted_iota(jnp.int32, (SB_TQ, SB_BLOCK), 1)
    causal = [lane + r * SB_BLOCK < query for r in range(SB_TQ // SB_BLOCK)]
    twice = lambda m: jnp.concatenate([m, m], axis=0)
    return twice(suffix), twice(prefix), causal, lane


def _sum_right(a, suffix2):
    hi = a.astype(BF16)
    lo = (a - hi.astype(F32)).astype(BF16)
    return _dot(jnp.concatenate([hi, lo], axis=1), suffix2)


def _sb_block(qb, kj, suffix, causal, c):
    z = _dot_nt(qb, kj)
    sp = jnp.maximum(z, 0.0) + jnp.log(1.0 + jnp.exp(-jnp.abs(z)))
    if causal is not None:
        sp = jnp.where(causal, sp, 0.0)
    big_l = _sum_right(sp, suffix)
    a = jnp.exp(z - big_l - c)
    if causal is not None:
        a = jnp.where(causal, a, 0.0)
    return z, a, big_l


def _half_masked(pair, lane, scale=1.0):
    pair = pair.astype(F32) * scale
    return jnp.where(lane < SB_DH, pair, 0.0).astype(BF16), jnp.where(lane < SB_DH, 0.0, pair).astype(BF16)


def _sb_fwd(qkv):
    T = qkv.shape[0]
    width = SB_PP * SB_BLOCK
    n_chain = 2 * SB_PP
    n_sub = SB_TQ // SB_BLOCK

    def body(q_ref, k_ref, v_ref, o_ref, right_ref, acc_ref, c_ref, qm):
        i = pl.program_id(1)
        suffix, _, causal, lane = _sb_masks()
        for pp in range(SB_PP):
            qm[2 * pp], qm[2 * pp + 1] = _half_masked(q_ref[:, pp * SB_BLOCK:(pp + 1) * SB_BLOCK], lane, SB_DH ** -0.5)
        acc_ref[...] = jnp.zeros_like(acc_ref)
        c_ref[...] = jnp.zeros_like(c_ref)
        right_ref[...] = jnp.zeros_like(right_ref)

        def step(j, r=None):
            rows = pl.ds(pl.multiple_of(j * SB_BLOCK, SB_BLOCK), SB_BLOCK)
            qs = slice(0 if r is None else r * SB_BLOCK, SB_TQ)
            mask = None if r is None else causal[r][qs]
            for ch in range(n_chain):
                pair = slice((ch // 2) * SB_BLOCK, (ch // 2 + 1) * SB_BLOCK)
                c = c_ref[ch, qs]
                _, a, big_l = _sb_block(qm[ch, qs], k_ref[rows, pair], suffix, mask, c)
                acc_ref[ch, qs] += _dot(a.astype(BF16), v_ref[rows, pair])
                cols = slice(ch * SB_BLOCK, (ch + 1) * SB_BLOCK)
                right_ref[qs, cols] = jnp.where(lane[qs] == j, c, right_ref[qs, cols])
                c_ref[ch, qs] = c + jnp.broadcast_to(big_l[:, 0:1], c.shape)

        first = i * n_sub
        for r in reversed(range(n_sub)):
            step(first + r, r)

        @pl.loop(0, first)
        def _(t):
            step(first - 1 - t)

        for pp in range(SB_PP):
            o_ref[:, pp * SB_BLOCK:(pp + 1) * SB_BLOCK] = jnp.where(lane < SB_DH, acc_ref[2 * pp], acc_ref[2 * pp + 1])

    assert T // SB_BLOCK <= SB_BLOCK
    n_g = SB_WIDTH // width
    blk = lambda part: pl.BlockSpec((SB_TQ, width), lambda g, i, part=part: (i, part * n_g + g))
    whole = lambda part: pl.BlockSpec((T, width), lambda g, i, part=part: (0, part * n_g + g))
    return pl.pallas_call(
        body, grid=(n_g, T // SB_TQ), name="sb_fwd",
        in_specs=[blk(0), whole(1), whole(2)],
        out_specs=[blk(0), pl.BlockSpec((SB_TQ, n_chain * SB_BLOCK), lambda g, i: (i, g))],
        out_shape=[jax.ShapeDtypeStruct((T, SB_WIDTH), F32), jax.ShapeDtypeStruct((T, SB_HEADS * SB_BLOCK), F32)],
        scratch_shapes=[pltpu.VMEM((n_chain, SB_TQ, SB_BLOCK), F32), pltpu.VMEM((n_chain, SB_TQ, SB_BLOCK), F32),
                        pltpu.VMEM((n_chain, SB_TQ, SB_BLOCK), BF16)],
        compiler_params=_params(("parallel", "arbitrary")),
    )(qkv, qkv, qkv)


def _sb_bwd(qkv, right, d_o, after):
    T = qkv.shape[0]
    width = SB_PP * SB_BLOCK
    n_chain = 2 * SB_PP
    n_sub = SB_TQ // SB_BLOCK
    scale = SB_DH ** -0.5

    def body(q_ref, k_ref, v_ref, right_ref, do_ref, after_ref, dq_ref, dk_ref, dv_ref, acc_ref, gc_ref, qm, dom):
        i = pl.program_id(1)

        @pl.when(i == 0)
        def _():
            dk_ref[...] = jnp.zeros_like(dk_ref)
            dv_ref[...] = jnp.zeros_like(dv_ref)

        suffix, prefix, causal, lane = _sb_masks()
        for pp in range(SB_PP):
            pair = slice(pp * SB_BLOCK, (pp + 1) * SB_BLOCK)
            qm[2 * pp], qm[2 * pp + 1] = _half_masked(q_ref[:, pair], lane, scale)
            dom[2 * pp], dom[2 * pp + 1] = _half_masked(do_ref[:, pair], lane)
        acc_ref[...] = jnp.zeros_like(acc_ref)
        gc_ref[...] = jnp.zeros_like(gc_ref)

        def step(j, r=None):
            rows = pl.ds(pl.multiple_of(j * SB_BLOCK, SB_BLOCK), SB_BLOCK)
            qs = slice(0 if r is None else r * SB_BLOCK, SB_TQ)
            mask = None if r is None else causal[r][qs]
            for pp in range(SB_PP):
                pair = slice(pp * SB_BLOCK, (pp + 1) * SB_BLOCK)
                kj = k_ref[rows, pair]
                vj = v_ref[rows, pair]
                dk = jnp.zeros((SB_BLOCK, SB_BLOCK), F32)
                dv = jnp.zeros((SB_BLOCK, SB_BLOCK), F32)
                for ch in (2 * pp, 2 * pp + 1):
                    cols = slice(ch * SB_BLOCK, (ch + 1) * SB_BLOCK)
                    c = jnp.sum(jnp.where(lane[qs] == j, right_ref[qs, cols], 0.0), axis=1, keepdims=True)
                    z, a, _ = _sb_block(qm[ch, qs], kj, suffix, mask, c)
                    g = a * _dot_nt(dom[ch, qs], vj)
                    g_left = _sum_right(g, prefix)
                    gc = gc_ref[ch, qs]
                    dz = g - _sigmoid(z) * (gc + g_left)
                    if mask is not None:
                        dz = jnp.where(mask, dz, 0.0)
                    dzb = dz.astype(BF16)
                    dk = dk + _dot_tn(dzb, qm[ch, qs])
                    dv = dv + _dot_tn(a.astype(BF16), dom[ch, qs])
                    acc_ref[ch, qs] += _dot(dzb, kj)
                    gc_ref[ch, qs] = gc + jnp.broadcast_to(g_left[:, SB_BLOCK - 1:SB_BLOCK], gc.shape)
                dk_ref[rows, pair] += dk
                dv_ref[rows, pair] += dv

        first = i * n_sub

        @pl.loop(0, first)
        def _(j):
            step(j)

        for r in range(n_sub):
            step(first + r, r)
        for pp in range(SB_PP):
            dq_ref[:, pp * SB_BLOCK:(pp + 1) * SB_BLOCK] = scale * jnp.where(lane < SB_DH, acc_ref[2 * pp],
                                                                             acc_ref[2 * pp + 1])

    n_g = SB_WIDTH // width
    blk = lambda part: pl.BlockSpec((SB_TQ, width), lambda g, i, part=part: (i, part * n_g + g))
    whole = lambda part: pl.BlockSpec((T, width), lambda g, i, part=part: (0, part * n_g + g))
    return pl.pallas_call(
        body, grid=(n_g, T // SB_TQ), name="sb_bwd",
        in_specs=[blk(0), whole(1), whole(2), pl.BlockSpec((SB_TQ, n_chain * SB_BLOCK), lambda g, i: (i, g)), blk(0),
                  pl.BlockSpec(memory_space=pl.ANY)],
        out_specs=[blk(0), whole(0), whole(0)],
        out_shape=[jax.ShapeDtypeStruct((T, SB_WIDTH), F32)] * 3,
        scratch_shapes=[pltpu.VMEM((n_chain, SB_TQ, SB_BLOCK), F32), pltpu.VMEM((n_chain, SB_TQ, SB_BLOCK), F32),
                        pltpu.VMEM((n_chain, SB_TQ, SB_BLOCK), BF16), pltpu.VMEM((n_chain, SB_TQ, SB_BLOCK), BF16)],
        compiler_params=_params(("parallel", "arbitrary")),
    )(qkv, qkv, qkv, right, d_o, after)


HBM = pl.BlockSpec(memory_space=pltpu.HBM)
MESH = pl.DeviceIdType.MESH


def _place():
    return lax.axis_index("x"), lax.axis_index("y"), lax.axis_index("c")


def _all_gather(block):
    rows, cols = block.shape

    def body(x_ref, out_ref, send_sems, recv_sems, local_sem):
        x, y, c = _place()
        me, sibling = (x, y, c), (x, y, 1 - c)
        chips = [(1 - x, y), (x, 1 - y), (1 - x, 1 - y)]

        def slot(px, py, pc):
            return out_ref.at[4 * px + 2 * py + pc]

        def copy(k, blk, to, src=None):
            return pltpu.make_async_remote_copy(
                src_ref=slot(*blk) if src is None else src, dst_ref=slot(*blk),
                send_sem=send_sems.at[k], recv_sem=recv_sems.at[k], device_id=to, device_id_type=MESH)

        mine = pltpu.make_async_copy(x_ref, slot(*me), local_sem)
        mine.start()
        first = [copy(0, me, sibling, src=x_ref)]
        first += [copy(1 + j, me, (*chip, c), src=x_ref) for j, chip in enumerate(chips)]
        for cp in first:
            cp.start()
        passed = [copy(4 + j, (*chip, c), sibling) for j, chip in enumerate(chips)]
        for j, chip in enumerate(chips):
            copy(1 + j, (*chip, c), me).wait_recv()
            passed[j].start()
        copy(0, sibling, me).wait_recv()
        for j, chip in enumerate(chips):
            copy(4 + j, (*chip, 1 - c), me).wait_recv()
        for cp in first + passed:
            cp.wait_send()
        mine.wait()

    return pl.pallas_call(
        body, name="all_gather",
        out_shape=jax.ShapeDtypeStruct((N_DEV, rows, cols), block.dtype),
        in_specs=[HBM], out_specs=HBM,
        scratch_shapes=[pltpu.SemaphoreType.DMA((7,)), pltpu.SemaphoreType.DMA((7,)), pltpu.SemaphoreType.DMA],
    )(block)


SEM = pl.BlockSpec(memory_space=pltpu.SEMAPHORE)
ANY = pl.BlockSpec(memory_space=pl.ANY)
SPLIT_EFFECT = pltpu.SideEffectType.DATAFLOW_SIDE_EFFECTING
N_PEER = N_DEV - 1


def _flow_copies(kind, src_ref, land_ref, send_sems, recv_sems):
    x, y, c = _place()
    copies = []
    for r in range(1, N_DEV):
        px = 1 - x if r & 4 else x
        py = 1 - y if r & 2 else y
        pc = 1 - c if r & 1 else c
        if kind == "gather":
            src, dst = src_ref, land_ref.at[4 * x + 2 * y + c]
        else:
            src, dst = src_ref.at[4 * px + 2 * py + pc], land_ref.at[r - 1]
        copies.append(pltpu.make_async_remote_copy(
            src_ref=src, dst_ref=dst, send_sem=send_sems.at[r - 1], recv_sem=recv_sems.at[r - 1],
            device_id=(px, py, pc), device_id_type=MESH))
    return copies


def _landing(kind, src):
    if kind == "gather":
        return lax.empty((N_DEV,) + src.shape, src.dtype)
    return lax.empty((N_PEER,) + src.shape[1:], src.dtype)


def _push_start(kinds, arrays, name):
    n = len(arrays)

    def body(*refs):
        ins, sems, token = refs[:n], refs[n:2 * n], refs[3 * n]
        for f, kind in enumerate(kinds):
            for cp in _flow_copies(kind, ins[2 * f], ins[2 * f + 1], sems[2 * f], sems[2 * f + 1]):
                cp.start()
        token[...] = jnp.zeros_like(token)

    outs = pl.pallas_call(
        body, name=name,
        out_shape=[pltpu.SemaphoreType.DMA((N_PEER,))] * n + [pltpu.HBM(a.shape, a.dtype) for a in arrays]
        + [jax.ShapeDtypeStruct((8, 128), F32)],
        in_specs=[HBM] * n, out_specs=[SEM] * n + [HBM] * n + [pl.BlockSpec(memory_space=pltpu.VMEM)],
        input_output_aliases={i: n + i for i in range(n)},
        compiler_params=pltpu.CompilerParams(has_side_effects=SPLIT_EFFECT),
    )(*[pltpu.with_memory_space_constraint(a, pltpu.HBM) for a in arrays])
    return outs[:n], outs[n:2 * n], outs[2 * n]


def _push_wait(kinds, sems, arrays, after, name):
    n = len(arrays)

    def body(*refs):
        ins, sem_refs = refs[:n], refs[n:2 * n]
        for f, kind in enumerate(kinds):
            for cp in _flow_copies(kind, ins[2 * f], ins[2 * f + 1], sem_refs[2 * f], sem_refs[2 * f + 1]):
                cp.wait_send()
                cp.wait_recv()

    outs = pl.pallas_call(
        body, name=name,
        out_shape=[pltpu.HBM(a.shape, a.dtype) for a in arrays],
        in_specs=[HBM] * n + [SEM] * n + [ANY], out_specs=[HBM] * n,
        input_output_aliases={i: i for i in range(n)},
        compiler_params=pltpu.CompilerParams(has_side_effects=SPLIT_EFFECT),
    )(*arrays, *sems, after)
    return outs


def _sum_blocks(own, got, name):
    n, rows, cols = got.shape
    tr = 64
    assert rows % tr == 0

    def body(own_ref, got_ref, out_ref):
        acc = own_ref[...].astype(F32)
        for r in range(n):
            acc = acc + got_ref[r].astype(F32)
        out_ref[...] = acc

    return pl.pallas_call(
        body, grid=(rows // tr,), name=name,
        in_specs=[pl.BlockSpec((tr, cols), lambda i: (i, 0)), pl.BlockSpec((n, tr, cols), lambda i: (0, i, 0))],
        out_specs=pl.BlockSpec((tr, cols), lambda i: (i, 0)),
        out_shape=jax.ShapeDtypeStruct((rows, cols), F32),
        compiler_params=_params(("parallel",)),
    )(own, got)


def _adamw_math(w, g, m, v):
    m = ADAM_B1 * m + (1.0 - ADAM_B1) * g
    v = ADAM_B2 * v + (1.0 - ADAM_B2) * (g * g)
    m_hat = m / (1.0 - ADAM_B1 ** ADAM_STEP)
    v_hat = v / (1.0 - ADAM_B2 ** ADAM_STEP)
    delta = -ADAM_LR * (m_hat / (jnp.sqrt(v_hat) + ADAM_EPS) + ADAM_WD * w)
    return delta, m, v


def _adamw(w, g, m, v, name):
    rows, cols = w.shape
    tr = rows if rows <= 352 else 256

    def body(w_ref, g_ref, m_ref, v_ref, d_out, m_out, v_out):
        d_out[...], m_out[...], v_out[...] = _adamw_math(w_ref[...], g_ref[...], m_ref[...], v_ref[...])

    spec = pl.BlockSpec((tr, cols), lambda i: (i, 0))
    return pl.pallas_call(
        body, grid=(rows // tr,), name=name,
        in_specs=[spec] * 4, out_specs=[spec] * 3,
        out_shape=[jax.ShapeDtypeStruct((rows, cols), F32)] * 3,
        compiler_params=_params(("parallel",)),
    )(w, g, m, v)


ROW_ATTN_PRE, ROW_ATTN_POST, ROW_FFN_PRE, ROW_FFN_POST, ROW_OUT_NORMS, ROW_GAMMA, ROW_LOSS = range(7)


def _small_update(smalls, w, m, v):
    def body(s_ref, w_ref, m_ref, v_ref, g_out, d_out, m_out, v_out):
        g = s_ref[0]
        for j in range(1, N_DEV):
            g = g + s_ref[j]
        wv = w_ref[...]
        g0 = wv[ROW_GAMMA:ROW_GAMMA + 1, 0:HG_WIDTH]
        g1 = wv[ROW_GAMMA:ROW_GAMMA + 1, HG_WIDTH:]
        mx = jnp.maximum(g0, g1)
        e0 = jnp.exp(g0 - mx)
        e1 = jnp.exp(g1 - mx)
        lb = e0 / (e0 + e1)
        dg0 = g[ROW_GAMMA:ROW_GAMMA + 1, 0:HG_WIDTH] * lb * (1.0 - lb)
        row = lax.broadcasted_iota(jnp.int32, g.shape, 0)
        g = jnp.where(row == ROW_GAMMA, jnp.concatenate([dg0, -dg0], axis=1), g)
        g_out[...] = g
        d_out[...], m_out[...], v_out[...] = _adamw_math(wv, g, m_ref[...], v_ref[...])

    return pl.pallas_call(
        body, name="small_update",
        out_shape=[jax.ShapeDtypeStruct((8, D_MODEL), F32)] * 4,
    )(smalls, w, m, v)


def _local_step(x, p, target, g_attn_pre, w_in, gamma, g_hg, g_sb, g_attn_post, g_ffn_pre, g_ffn_post, later_weights,
                grads_ready):
    proj, u, qkv = _in_proj(x, g_attn_pre, w_in)
    o_hg, hist = _hg_fwd(proj, gamma)
    o_sb, right = _sb_fwd(qkv)
    w_out, w_gu, w_down, w_pp, w_pg = later_weights(o_sb)
    cat, mix, h1 = _out_proj(o_hg, proj, o_sb, x, g_hg, g_sb, g_attn_post, w_out)
    u2, gate, up, y, h2 = _ffn_fwd(h1, g_ffn_pre, w_gu, w_down, g_ffn_post)
    dh2, de, dg, loss = _ple_loss(h2, p, target, w_pp, w_pg)

    d_wpp = _wgrad(p, de, "wgrad_ple_proj")
    d_wpg = _wgrad(h2, dg, "wgrad_ple_gate")
    dy, a, dgate, dup, dh1, dg_ffn_post, dg_ffn_pre = _ffn_bwd(dh2, y, h1, gate, up, g_ffn_post, g_ffn_pre, w_gu, w_down)
    d_wdown = _wgrad(a, dy, "wgrad_down")
    sent = grads_ready("ffn", (_wgrad(u2, dgate, "wgrad_gate"), _wgrad(u2, dup, "wgrad_up"), d_wdown, d_wpp, d_wpg))
    dmix, d_ohg, d_hg, d_osb, dg_attn_post, dg_hg, dg_sb = _out_bwd(dh1, mix, o_hg, proj, o_sb, g_hg, g_sb, g_attn_post,
                                                                    w_out, sent)
    sent = grads_ready("out", (_wgrad(cat, dmix, "wgrad_out"),))
    dq, dk, dv = _sb_bwd(qkv, right, d_osb, sent)
    d_hgrn, d_lb = _hg_bwd(proj, gamma, d_ohg, hist)
    dproj, dx, dg_attn_pre = _in_bwd(d_hgrn, d_hg, dq, dk, dv, x, dh1, g_attn_pre, w_in)
    grads_ready("in", (_wgrad(u, dproj, "wgrad_in"),))
    return loss, dx, (dg_attn_pre, dg_attn_post, dg_ffn_pre, dg_ffn_post, dg_hg, dg_sb, d_lb)


def _cols_to_rows(a, n_blocks=N_DEV):
    rows, cols = a.shape
    n = cols // n_blocks
    return a.reshape(rows, n_blocks, n).transpose(1, 0, 2).reshape(n_blocks, rows * n // D_MODEL, D_MODEL)


def _rows_to_cols(a, rows):
    n = a.shape[1] * D_MODEL // rows
    return a.reshape(N_DEV, rows, n).transpose(1, 0, 2).reshape(rows, N_DEV * n)


def _pack_row(*parts):
    return jnp.concatenate([q.reshape(1, -1) for q in parts], axis=1)


def kernel(x, p, attn_pre_norm, w_in, hg_lower_gamma, hg_out_norm, sb_out_norm, w_out, attn_post_norm, ffn_pre_norm, w_gate_up, w_down, ffn_post_norm, ple_proj, ple_gate, loss_target, m_attn_pre_norm, m_w_in, m_hg_lower_gamma, m_hg_out_norm, m_sb_out_norm, m_w_out, m_attn_post_norm, m_ffn_pre_norm, m_w_gate_up, m_w_down, m_ffn_post_norm, m_ple_proj, m_ple_gate, v_attn_pre_norm, v_w_in, v_hg_lower_gamma, v_hg_out_norm, v_sb_out_norm, v_w_out, v_attn_post_norm, v_ffn_pre_norm, v_w_gate_up, v_w_down, v_ffn_post_norm, v_ple_proj, v_ple_gate):
    big = (w_in[0], w_out[0], w_gate_up[0], w_down[0], ple_proj[0], ple_gate[0])
    big_m = (m_w_in[0], m_w_out[0], m_w_gate_up[0], m_w_down[0], m_ple_proj[0], m_ple_gate[0])
    big_v = (v_w_in[0], v_w_out[0], v_w_gate_up[0], v_w_down[0], v_ple_proj[0], v_ple_gate[0])
    names = ("w_in", "w_out", "w_gate_up", "w_down", "ple_proj", "ple_gate")

    shards = [w.astype(BF16).reshape(-1, D_MODEL) for w in big]
    gathered_in = _all_gather(shards[0])
    later_pack = jnp.concatenate(shards[1:], axis=0)
    me = 4 * lax.axis_index("x") + 2 * lax.axis_index("y") + lax.axis_index("c")
    w_sems, w_thru, token = _push_start(("gather",), (later_pack, _landing("gather", later_pack)), "weights_start")

    def later_weights(after):
        mine, got = _push_wait(("gather",), w_sems, w_thru, after, "weights_wait")
        got = lax.dynamic_update_index_in_dim(got, mine, me, 0)
        part = lambda i: got[:, PACK_OFFS[i] - PACK_SIZES[0]:PACK_OFFS[i] - PACK_SIZES[0] + PACK_SIZES[i], :]
        return (part(1).reshape(D_MODEL, D_MODEL), _rows_to_cols(part(2), D_MODEL), part(3).reshape(D_FF, D_MODEL),
                _rows_to_cols(part(4), PLE_DIM), part(5).reshape(D_MODEL, D_MODEL))

    flights = {}

    def grads_ready(group, grads):
        if group == "ffn":
            d_gate, d_up, d_down, d_pp, d_pg = grads
            half = N_DEV // 2
            blocks = jnp.concatenate([
                jnp.concatenate([_cols_to_rows(d_gate, half), _cols_to_rows(d_up, half)], axis=0),
                d_down.reshape(N_DEV, -1, D_MODEL), _cols_to_rows(d_pp), d_pg.reshape(N_DEV, -1, D_MODEL)], axis=1)
        elif group == "out":
            blocks = grads[0].reshape(N_DEV, -1, D_MODEL)
        else:
            flights[group] = _cols_to_rows(grads[0])
            return None
        sems, thru, sent = _push_start(("scatter",), (blocks, _landing("scatter", blocks)), "grads_" + group + "_start")
        flights[group] = (sems, thru)
        return sent

    loss, dx, smalls = _local_step(
        x[0], p[0, 0], loss_target[0], attn_pre_norm + token[0:1, 0:1], _rows_to_cols(gathered_in, D_MODEL),
        hg_lower_gamma, hg_out_norm, sb_out_norm, attn_post_norm, ffn_pre_norm, ffn_post_norm, later_weights, grads_ready)
    dg_attn_pre, dg_attn_post, dg_ffn_pre, dg_ffn_post, dg_hg, dg_sb, d_lb = smalls

    zeros_half = jnp.zeros((1, HG_WIDTH), F32)
    small_pack = jnp.concatenate([
        dg_attn_pre, dg_attn_post, dg_ffn_pre, dg_ffn_post, _pack_row(dg_hg, dg_sb), _pack_row(d_lb, zeros_half),
        jnp.broadcast_to(loss[0:1, 0:1], (1, D_MODEL)), jnp.zeros((1, D_MODEL), F32)], axis=0)
    in_blocks = flights["in"]
    in_sems, in_thru, in_token = _push_start(
        ("scatter", "gather"),
        (in_blocks, _landing("scatter", in_blocks), small_pack, _landing("gather", small_pack)), "grads_in_start")
    sent_ffn, got_ffn = _push_wait(("scatter",), *flights["ffn"], in_token, "grads_ffn_wait")
    sent_out, got_out = _push_wait(("scatter",), *flights["out"], got_ffn, "grads_out_wait")
    sent_in, got_in, my_small, all_smalls = _push_wait(("scatter", "gather"), in_sems, in_thru, got_out, "grads_in_wait")
    all_smalls = lax.dynamic_update_index_in_dim(all_smalls, my_small, me, 0)
    own = lambda blocks: lax.dynamic_index_in_dim(blocks, me, 0, keepdims=False)
    sum_ffn = _sum_blocks(own(sent_ffn), got_ffn, "grad_sum_ffn")
    sums = (_sum_blocks(own(sent_in), got_in, "grad_sum_in"), _sum_blocks(own(sent_out), got_out, "grad_sum_out"),
            sum_ffn[0:PACK_SIZES[2]], sum_ffn[PACK_SIZES[2]:PACK_SIZES[2] + PACK_SIZES[3]],
            sum_ffn[PACK_SIZES[2] + PACK_SIZES[3]:PACK_SIZES[2] + PACK_SIZES[3] + PACK_SIZES[4]],
            sum_ffn[PACK_SIZES[2] + PACK_SIZES[3] + PACK_SIZES[4]:])

    out_g, out_d, out_m, out_v = {}, {}, {}, {}
    for i, name in enumerate(names):
        g = sums[i].reshape(big[i].shape)
        out_g[name] = g[None]
        d, m, v = _adamw(big[i], g, big_m[i], big_v[i], "adamw_" + name)
        out_d[name], out_m[name], out_v[name] = d[None], m[None], v[None]

    def small_rows(pre, gam, hg, sb, post, fpre, fpost):
        return jnp.concatenate([pre, post, fpre, fpost, _pack_row(hg, sb), _pack_row(gam[0], gam[1]),
                                jnp.zeros((2, D_MODEL), F32)], axis=0)

    packs = _small_update(
        all_smalls,
        small_rows(attn_pre_norm, hg_lower_gamma, hg_out_norm, sb_out_norm, attn_post_norm, ffn_pre_norm, ffn_post_norm),
        small_rows(m_attn_pre_norm, m_hg_lower_gamma, m_hg_out_norm, m_sb_out_norm, m_attn_post_norm, m_ffn_pre_norm,
                   m_ffn_post_norm),
        small_rows(v_attn_pre_norm, v_hg_lower_gamma, v_hg_out_norm, v_sb_out_norm, v_attn_post_norm, v_ffn_pre_norm,
                   v_ffn_post_norm))

    def unpack(pk):
        return {
            "attn_pre_norm": pk[ROW_ATTN_PRE:ROW_ATTN_PRE + 1],
            "hg_lower_gamma": pk[ROW_GAMMA].reshape(2, HG_WIDTH),
            "hg_out_norm": pk[ROW_OUT_NORMS:ROW_OUT_NORMS + 1, :HG_WIDTH],
            "sb_out_norm": pk[ROW_OUT_NORMS:ROW_OUT_NORMS + 1, HG_WIDTH:],
            "attn_post_norm": pk[ROW_ATTN_POST:ROW_ATTN_POST + 1],
            "ffn_pre_norm": pk[ROW_FFN_PRE:ROW_FFN_PRE + 1],
            "ffn_post_norm": pk[ROW_FFN_POST:ROW_FFN_POST + 1],
        }

    sg, sd, sm, sv = (unpack(pk) for pk in packs)
    out_g.update(sg), out_d.update(sd), out_m.update(sm), out_v.update(sv)
    order = ("attn_pre_norm", "w_in", "hg_lower_gamma", "hg_out_norm", "sb_out_norm", "w_out", "attn_post_norm",
             "ffn_pre_norm", "w_gate_up", "w_down", "ffn_post_norm", "ple_proj", "ple_gate")
    total_loss = packs[0][ROW_LOSS, 0]
    return (total_loss, dx[None], *[out_g[n] for n in order], *[out_d[n] for n in order],
            *[out_m[n] for n in order], *[out_v[n] for n in order])
```

```python
import functools

import jax
import jax.numpy as jnp
from jax import lax
from jax.experimental import pallas as pl
from jax.experimental.pallas import tpu as pltpu

F32 = jnp.float32
BF16 = jnp.bfloat16

D_MODEL = 1024
HG_WIDTH = 512
HG_HEADS = 4
HG_DK = 128
SB_WIDTH = 512
SB_HEADS = 8
SB_DH = 64
SB_BLOCK = 128
SB_PP = 2
SB_TQ = 512
D_FF = 2816
PLE_DIM = 256
IN_COLS = 4 * HG_WIDTH + 3 * SB_WIDTH
EPS = 1e-6
N_DEV = 8

HG_SUB = 16
HG_TILE = 128
FFN_TF = 256
ROW_TILE = 256
VMEM_LIMIT = 56 * 1024 * 1024
WGRAD_ACC_BYTES = 8 * 1024 * 1024

ADAM_LR = 0.001
ADAM_B1 = 0.9
ADAM_B2 = 0.999
ADAM_EPS = 1e-08
ADAM_WD = 0.01
ADAM_STEP = 10

PACK_SIZES = (448, 128, 704, 352, 32, 128)
PACK_ROWS = sum(PACK_SIZES)
PACK_OFFS = tuple(sum(PACK_SIZES[:i]) for i in range(len(PACK_SIZES)))

NT_DIMS = (((1,), (1,)), ((), ()))
TN_DIMS = (((0,), (0,)), ((), ()))


def _params(semantics):
    return pltpu.CompilerParams(dimension_semantics=semantics, vmem_limit_bytes=VMEM_LIMIT)


def _dot(a, b):
    return jnp.dot(a, b, preferred_element_type=F32)


def _dot_nt(a, b):
    return lax.dot_general(a, b, NT_DIMS, preferred_element_type=F32)


def _dot_tn(a, b):
    return lax.dot_general(a, b, TN_DIMS, preferred_element_type=F32)


def _sigmoid(z):
    return 1.0 / (1.0 + jnp.exp(-z))


def _rstd(xv):
    return lax.rsqrt(jnp.mean(xv * xv, axis=-1, keepdims=True) + EPS)


def _rms_bwd(xv, r, g, dn):
    xn = xv * r
    gh = dn * g
    dx = r * (gh - xn * jnp.mean(gh * xn, axis=-1, keepdims=True))
    return dx, dn * xn


def _colsum(a):
    return jnp.sum(a, axis=0, keepdims=True)


def _row_spec(tm, width, col=0):
    return pl.BlockSpec((tm, width), lambda i, col=col: (i, col))


def _full_spec(shape):
    return pl.BlockSpec(shape, lambda i: (0,) * len(shape))


def _in_proj(x, g, w):
    T = x.shape[0]
    tm = ROW_TILE

    def body(x_ref, g_ref, w_ref, proj_ref, u_ref, qkv_ref):
        xv = x_ref[...]
        u = (xv * _rstd(xv) * g_ref[...]).astype(BF16)
        u_ref[...] = u
        proj = _dot(u, w_ref[...])
        proj_ref[...] = proj
        qkv_ref[...] = proj[:, 4 * HG_WIDTH:].astype(BF16)

    return pl.pallas_call(
        body, grid=(T // tm,), name="in_proj",
        in_specs=[_row_spec(tm, D_MODEL), _full_spec((1, D_MODEL)), _full_spec((D_MODEL, IN_COLS))],
        out_specs=[_row_spec(tm, IN_COLS), _row_spec(tm, D_MODEL), _row_spec(tm, 3 * SB_WIDTH)],
        out_shape=[jax.ShapeDtypeStruct((T, IN_COLS), F32), jax.ShapeDtypeStruct((T, D_MODEL), BF16),
                   jax.ShapeDtypeStruct((T, 3 * SB_WIDTH), BF16)],
        compiler_params=_params(("parallel",)),
    )(x, g, w)


def _out_proj(o_hg, proj, o_sb, x, g_hg, g_sb, g_post, w):
    T = x.shape[0]
    tm = ROW_TILE

    def body(ohg_ref, hg_ref, osb_ref, x_ref, ghg_ref, gsb_ref, gpost_ref, w_ref, cat_ref, mix_ref, h1_ref):
        ohg = ohg_ref[...]
        hg = hg_ref[...]
        osb = osb_ref[...]
        a = ohg * _rstd(ohg) * ghg_ref[...] * (hg * _sigmoid(hg))
        n2 = osb * _rstd(osb) * gsb_ref[...]
        cat = jnp.concatenate([a, n2], axis=1).astype(BF16)
        cat_ref[...] = cat
        mix = _dot(cat, w_ref[...])
        mix_ref[...] = mix
        h1_ref[...] = x_ref[...] + mix * _rstd(mix) * gpost_ref[...]

    return pl.pallas_call(
        body, grid=(T // tm,), name="out_proj",
        in_specs=[_row_spec(tm, HG_WIDTH), _row_spec(tm, HG_WIDTH, 3), _row_spec(tm, SB_WIDTH), _row_spec(tm, D_MODEL),
                  _full_spec((1, HG_WIDTH)), _full_spec((1, SB_WIDTH)), _full_spec((1, D_MODEL)),
                  _full_spec((D_MODEL, D_MODEL))],
        out_specs=[_row_spec(tm, D_MODEL)] * 3,
        out_shape=[jax.ShapeDtypeStruct((T, D_MODEL), BF16), jax.ShapeDtypeStruct((T, D_MODEL), F32),
                   jax.ShapeDtypeStruct((T, D_MODEL), F32)],
        compiler_params=_params(("parallel",)),
    )(o_hg, proj, o_sb, x, g_hg, g_sb, g_post, w)


def _ffn_fwd(h1, g_pre, w_gu, w_down, g_post):
    T = h1.shape[0]
    tm = 2 * ROW_TILE
    n_f = D_FF // FFN_TF

    def body(h1_ref, gpre_ref, wg_ref, wu_ref, wd_ref, gpost_ref, u2_ref, gate_ref, up_ref, y_ref, h2_ref, acc_ref):
        j = pl.program_id(1)

        @pl.when(j == 0)
        def _():
            hv = h1_ref[...]
            u2_ref[...] = (hv * _rstd(hv) * gpre_ref[...]).astype(BF16)
            acc_ref[...] = jnp.zeros_like(acc_ref)

        u2 = u2_ref[...]
        gate = _dot(u2, wg_ref[...])
        up = _dot(u2, wu_ref[...])
        gate_ref[...] = gate
        up_ref[...] = up
        a = (gate * _sigmoid(gate) * up).astype(BF16)
        acc_ref[...] += _dot(a, wd_ref[...])

        @pl.when(j == n_f - 1)
        def _():
            y = acc_ref[...]
            y_ref[...] = y
            h2_ref[...] = h1_ref[...] + y * _rstd(y) * gpost_ref[...]

    row = lambda width: pl.BlockSpec((tm, width), lambda i, j: (i, 0))
    vec = pl.BlockSpec((1, D_MODEL), lambda i, j: (0, 0))
    return pl.pallas_call(
        body, grid=(T // tm, n_f), name="ffn_fwd",
        in_specs=[row(D_MODEL), vec,
                  pl.BlockSpec((D_MODEL, FFN_TF), lambda i, j: (0, j)),
                  pl.BlockSpec((D_MODEL, FFN_TF), lambda i, j: (0, j + n_f)),
                  pl.BlockSpec((FFN_TF, D_MODEL), lambda i, j: (j, 0)), vec],
        out_specs=[row(D_MODEL),
                   pl.BlockSpec((tm, FFN_TF), lambda i, j: (i, j)),
                   pl.BlockSpec((tm, FFN_TF), lambda i, j: (i, j)),
                   row(D_MODEL), row(D_MODEL)],
        out_shape=[jax.ShapeDtypeStruct((T, D_MODEL), BF16), jax.ShapeDtypeStruct((T, D_FF), F32),
                   jax.ShapeDtypeStruct((T, D_FF), F32), jax.ShapeDtypeStruct((T, D_MODEL), F32),
                   jax.ShapeDtypeStruct((T, D_MODEL), F32)],
        scratch_shapes=[pltpu.VMEM((tm, D_MODEL), F32)],
        compiler_params=_params(("parallel", "arbitrary")),
    )(h1, g_pre, w_gu, w_gu, w_down, g_post)


def _ple_loss(h2, p, target, w_pp, w_pg):
    T = h2.shape[0]
    tm = ROW_TILE

    def body(h2_ref, p_ref, t_ref, wpp_ref, wpg_ref, dh2_ref, de_ref, dg_ref, loss_ref):
        i = pl.program_id(0)
        h2v = h2_ref[...]
        e = _dot(p_ref[...].astype(BF16), wpp_ref[...])
        sg = _sigmoid(_dot(h2v.astype(BF16), wpg_ref[...]))
        diff = h2v + e * sg - t_ref[...]
        part = jnp.sum(jnp.sum(diff * diff, axis=1, keepdims=True), axis=0, keepdims=True) * (0.5 / D_MODEL)

        @pl.when(i == 0)
        def _():
            loss_ref[...] = jnp.zeros_like(loss_ref)

        loss_ref[...] += jnp.broadcast_to(part, loss_ref.shape)
        dh3 = diff * (1.0 / D_MODEL)
        de_ref[...] = (dh3 * sg).astype(BF16)
        dg = (dh3 * e * sg * (1.0 - sg)).astype(BF16)
        dg_ref[...] = dg
        dh2_ref[...] = dh3 + _dot_nt(dg, wpg_ref[...])

    return pl.pallas_call(
        body, grid=(T // tm,), name="ple_loss",
        in_specs=[_row_spec(tm, D_MODEL), _row_spec(tm, PLE_DIM), _row_spec(tm, D_MODEL),
                  _full_spec((PLE_DIM, D_MODEL)), _full_spec((D_MODEL, D_MODEL))],
        out_specs=[_row_spec(tm, D_MODEL)] * 3 + [_full_spec((8, 128))],
        out_shape=[jax.ShapeDtypeStruct((T, D_MODEL), F32), jax.ShapeDtypeStruct((T, D_MODEL), BF16),
                   jax.ShapeDtypeStruct((T, D_MODEL), BF16), jax.ShapeDtypeStruct((8, 128), F32)],
        compiler_params=_params(("arbitrary",)),
    )(h2, p, target, w_pp, w_pg)


def _ffn_bwd(dh2, y, h1, gate, up, g_post, g_pre, w_gu, w_down):
    T = h1.shape[0]
    tm = 2 * ROW_TILE
    n_f = D_FF // FFN_TF

    def body(dh2_ref, y_ref, h1_ref, gate_ref, up_ref, gpost_ref, gpre_ref, wg_ref, wu_ref, wd_ref,
             dy_ref, a_ref, dgate_ref, dup_ref, dh1_ref, dgpost_ref, dgpre_ref, du2_ref):
        i = pl.program_id(0)
        j = pl.program_id(1)

        @pl.when((i == 0) & (j == 0))
        def _():
            dgpost_ref[...] = jnp.zeros_like(dgpost_ref)
            dgpre_ref[...] = jnp.zeros_like(dgpre_ref)

        @pl.when(j == 0)
        def _():
            yv = y_ref[...]
            dy, gterm = _rms_bwd(yv, _rstd(yv), gpost_ref[...], dh2_ref[...])
            dy_ref[...] = dy.astype(BF16)
            dgpost_ref[...] += _colsum(gterm)
            du2_ref[...] = jnp.zeros_like(du2_ref)

        da = _dot_nt(dy_ref[...], wd_ref[...])
        gate = gate_ref[...]
        up = up_ref[...]
        s = _sigmoid(gate)
        sil = gate * s
        a_ref[...] = (sil * up).astype(BF16)
        dgate = (da * up * (s * (1.0 + gate * (1.0 - s)))).astype(BF16)
        dup = (da * sil).astype(BF16)
        dgate_ref[...] = dgate
        dup_ref[...] = dup
        du2_ref[...] += _dot_nt(dgate, wg_ref[...]) + _dot_nt(dup, wu_ref[...])

        @pl.when(j == n_f - 1)
        def _():
            hv = h1_ref[...]
            dx, gterm = _rms_bwd(hv, _rstd(hv), gpre_ref[...], du2_ref[...])
            dh1_ref[...] = dh2_ref[...] + dx
            dgpre_ref[...] += _colsum(gterm)

    row = lambda width: pl.BlockSpec((tm, width), lambda i, j: (i, 0))
    col = pl.BlockSpec((tm, FFN_TF), lambda i, j: (i, j))
    vec = pl.BlockSpec((1, D_MODEL), lambda i, j: (0, 0))
    return pl.pallas_call(
        body, grid=(T // tm, n_f), name="ffn_bwd",
        in_specs=[row(D_MODEL), row(D_MODEL), row(D_MODEL), col, col, vec, vec,
                  pl.BlockSpec((D_MODEL, FFN_TF), lambda i, j: (0, j)),
                  pl.BlockSpec((D_MODEL, FFN_TF), lambda i, j: (0, j + n_f)),
                  pl.BlockSpec((FFN_TF, D_MODEL), lambda i, j: (j, 0))],
        out_specs=[row(D_MODEL), col, col, col, row(D_MODEL), vec, vec],
        out_shape=[jax.ShapeDtypeStruct((T, D_MODEL), BF16), jax.ShapeDtypeStruct((T, D_FF), BF16),
                   jax.ShapeDtypeStruct((T, D_FF), BF16), jax.ShapeDtypeStruct((T, D_FF), BF16),
                   jax.ShapeDtypeStruct((T, D_MODEL), F32), jax.ShapeDtypeStruct((1, D_MODEL), F32),
                   jax.ShapeDtypeStruct((1, D_MODEL), F32)],
        scratch_shapes=[pltpu.VMEM((tm, D_MODEL), F32)],
        compiler_params=_params(("arbitrary", "arbitrary")),
    )(dh2, y, h1, gate, up, g_post, g_pre, w_gu, w_gu, w_down)


def _out_bwd(dh1, mix, o_hg, proj, o_sb, g_hg, g_sb, g_post, w, after):
    T = dh1.shape[0]
    tm = ROW_TILE

    def body(dh1_ref, mix_ref, ohg_ref, hg_ref, osb_ref, ghg_ref, gsb_ref, gpost_ref, w_ref, after_ref,
             dmix_ref, dohg_ref, dhg_ref, dosb_ref, dgpost_ref, dghg_ref, dgsb_ref):
        i = pl.program_id(0)

        @pl.when(i == 0)
        def _():
            dgpost_ref[...] = jnp.zeros_like(dgpost_ref)
            dghg_ref[...] = jnp.zeros_like(dghg_ref)
            dgsb_ref[...] = jnp.zeros_like(dgsb_ref)

        mix = mix_ref[...]
        dmix, gterm = _rms_bwd(mix, _rstd(mix), gpost_ref[...], dh1_ref[...])
        dgpost_ref[...] += _colsum(gterm)
        dmix = dmix.astype(BF16)
        dmix_ref[...] = dmix
        dcat = _dot_nt(dmix, w_ref[...])
        da = dcat[:, :HG_WIDTH]
        dn2 = dcat[:, HG_WIDTH:]
        ohg = ohg_ref[...]
        r1 = _rstd(ohg)
        hg = hg_ref[...]
        s = _sigmoid(hg)
        dhg_ref[...] = da * (ohg * r1 * ghg_ref[...]) * (s * (1.0 + hg * (1.0 - s)))
        dohg, gterm = _rms_bwd(ohg, r1, ghg_ref[...], da * (hg * s))
        dohg_ref[...] = dohg
        dghg_ref[...] += _colsum(gterm)
        osb = osb_ref[...]
        dosb, gterm = _rms_bwd(osb, _rstd(osb), gsb_ref[...], dn2)
        dosb_ref[...] = dosb
        dgsb_ref[...] += _colsum(gterm)

    return pl.pallas_call(
        body, grid=(T // tm,), name="out_bwd",
        in_specs=[_row_spec(tm, D_MODEL), _row_spec(tm, D_MODEL), _row_spec(tm, HG_WIDTH), _row_spec(tm, HG_WIDTH, 3),
                  _row_spec(tm, SB_WIDTH), _full_spec((1, HG_WIDTH)), _full_spec((1, SB_WIDTH)),
                  _full_spec((1, D_MODEL)), _full_spec((D_MODEL, D_MODEL)), pl.BlockSpec(memory_space=pl.ANY)],
        out_specs=[_row_spec(tm, D_MODEL), _row_spec(tm, HG_WIDTH), _row_spec(tm, HG_WIDTH), _row_spec(tm, SB_WIDTH),
                   _full_spec((1, D_MODEL)), _full_spec((1, HG_WIDTH)), _full_spec((1, SB_WIDTH))],
        out_shape=[jax.ShapeDtypeStruct((T, D_MODEL), BF16), jax.ShapeDtypeStruct((T, HG_WIDTH), F32),
                   jax.ShapeDtypeStruct((T, HG_WIDTH), F32), jax.ShapeDtypeStruct((T, SB_WIDTH), F32),
                   jax.ShapeDtypeStruct((1, D_MODEL), F32), jax.ShapeDtypeStruct((1, HG_WIDTH), F32),
                   jax.ShapeDtypeStruct((1, SB_WIDTH), F32)],
        compiler_params=_params(("arbitrary",)),
    )(dh1, mix, o_hg, proj, o_sb, g_hg, g_sb, g_post, w, after)


def _in_bwd(d_hgrn, d_hg, d_sq, d_sk, d_sv, x, dh1, g_pre, w):
    T = x.shape[0]
    tm = ROW_TILE

    def body(dh_ref, dhg_ref, dsq_ref, dsk_ref, dsv_ref, x_ref, dh1_ref, gpre_ref, w_ref, dproj_ref, dx_ref, dgpre_ref):
        i = pl.program_id(0)

        @pl.when(i == 0)
        def _():
            dgpre_ref[...] = jnp.zeros_like(dgpre_ref)

        dp = jnp.concatenate([dh_ref[...], dhg_ref[...], dsq_ref[...], dsk_ref[...], dsv_ref[...]], axis=1).astype(BF16)
        dproj_ref[...] = dp
        du = _dot_nt(dp, w_ref[...])
        xv = x_ref[...]
        dx, gterm = _rms_bwd(xv, _rstd(xv), gpre_ref[...], du)
        dx_ref[...] = dh1_ref[...] + dx
        dgpre_ref[...] += _colsum(gterm)

    return pl.pallas_call(
        body, grid=(T // tm,), name="in_bwd",
        in_specs=[_row_spec(tm, 3 * HG_WIDTH), _row_spec(tm, HG_WIDTH), _row_spec(tm, SB_WIDTH), _row_spec(tm, SB_WIDTH),
                  _row_spec(tm, SB_WIDTH), _row_spec(tm, D_MODEL), _row_spec(tm, D_MODEL), _full_spec((1, D_MODEL)),
                  _full_spec((D_MODEL, IN_COLS))],
        out_specs=[_row_spec(tm, IN_COLS), _row_spec(tm, D_MODEL), _full_spec((1, D_MODEL))],
        out_shape=[jax.ShapeDtypeStruct((T, IN_COLS), BF16), jax.ShapeDtypeStruct((T, D_MODEL), F32),
                   jax.ShapeDtypeStruct((1, D_MODEL), F32)],
        compiler_params=_params(("arbitrary",)),
    )(d_hgrn, d_hg, d_sq, d_sk, d_sv, x, dh1, g_pre, w)


def _wgrad(a, b, name):
    T, K = a.shape
    N = b.shape[1]
    tk = K if K * N * 4 <= WGRAD_ACC_BYTES else K // 2
    assert K % tk == 0 and tk % 128 == 0
    tt = 512
    n_t = T // tt

    def body(a_ref, b_ref, o_ref, acc_ref):
        t = pl.program_id(1)

        @pl.when(t == 0)
        def _():
            acc_ref[...] = jnp.zeros_like(acc_ref)

        acc_ref[...] += _dot_tn(a_ref[...].astype(BF16), b_ref[...].astype(BF16))

        @pl.when(t == n_t - 1)
        def _():
            o_ref[...] = acc_ref[...].astype(BF16)

    return pl.pallas_call(
        body, grid=(K // tk, n_t), name=name,
        in_specs=[pl.BlockSpec((tt, tk), lambda k, t: (t, k)), pl.BlockSpec((tt, N), lambda k, t: (t, 0))],
        out_specs=pl.BlockSpec((tk, N), lambda k, t: (k, 0)),
        out_shape=jax.ShapeDtypeStruct((K, N), BF16),
        scratch_shapes=[pltpu.VMEM((tk, N), F32)],
        compiler_params=_params(("parallel", "arbitrary")),
    )(a, b)


def _hg_gates(hq, hf, gam):
    g0 = gam[0:1, :]
    g1 = gam[1:2, :]
    mx = jnp.maximum(g0, g1)
    e0 = jnp.exp(g0 - mx)
    e1 = jnp.exp(g1 - mx)
    lb = e0 / (e0 + e1)
    s = _sigmoid(hf)
    f = lb + (1.0 - lb) * s
    sq = _sigmoid(hq)
    return hq * sq, sq, s, f, (1.0 - lb) * (1.0 - s), jnp.log(f), lb


def _prefix_in_sub(a, rowmod):
    n = a.shape[0]
    sh = 1
    while sh < HG_SUB:
        a = a + jnp.where(rowmod >= sh, pltpu.roll(a, sh, axis=0), 0.0)
        sh *= 2
    return a


def _suffix_in_sub(a, rowmod):
    n = a.shape[0]
    sh = 1
    while sh < HG_SUB:
        a = a + jnp.where(rowmod < HG_SUB - sh, pltpu.roll(a, n - sh, axis=0), 0.0)
        sh *= 2
    return a


def _hg_fwd(proj, gam):
    T = proj.shape[0]
    tb = HG_TILE
    n_sub = tb // HG_SUB

    def body(hq_ref, hf_ref, hi_ref, gam_ref, o_ref, hist_ref, st_ref, qt_ref, kk_ref, b_ref, od_ref):
        i = pl.program_id(0)

        @pl.when(i == 0)
        def _():
            st_ref[...] = jnp.zeros_like(st_ref)

        rowmod_w = lax.broadcasted_iota(jnp.int32, (tb, HG_WIDTH), 0) % HG_SUB
        rowmod = lax.broadcasted_iota(jnp.int32, (tb, 1), 0) % HG_SUB
        q, _, _, _, kk, lf, _ = _hg_gates(hq_ref[...], hf_ref[...], gam_ref[...])
        b = _prefix_in_sub(lf, rowmod_w)
        qt_ref[...] = (q * jnp.exp(b)).astype(BF16)
        kk_ref[...] = kk
        b_ref[...] = b

        for h in range(HG_HEADS):
            cs = slice(h * HG_DK, (h + 1) * HG_DK)
            qh = q[:, cs]
            kh = kk[:, cs]
            bh = b[:, cs]
            vh = hi_ref[:, cs]
            acc = jnp.sum(qh * kh, axis=1, keepdims=True) * vh
            for d in range(1, HG_SUB):
                e = qh * pltpu.roll(kh, d, axis=0) * jnp.exp(bh - pltpu.roll(bh, d, axis=0))
                sc = jnp.where(rowmod >= d, jnp.sum(e, axis=1, keepdims=True), 0.0)
                acc = acc + sc * pltpu.roll(vh, d, axis=0)
            od_ref[:, cs] = acc

        for m in range(n_sub):
            rs = slice(m * HG_SUB, (m + 1) * HG_SUB)
            for h in range(HG_HEADS):
                cs = slice(h * HG_DK, (h + 1) * HG_DK)
                st = st_ref[h]
                hist_ref[m, h] = st
                o_int = _dot_nt(qt_ref[rs, cs], st.astype(BF16))
                bs = b_ref[rs, cs]
                bl = bs[HG_SUB - 1:HG_SUB, :]
                kt = (kk_ref[rs, cs] * jnp.exp(bl - bs)).astype(BF16)
                st_ref[h] = st * jnp.exp(bl) + _dot_tn(hi_ref[rs, cs].astype(BF16), kt)
                o_ref[rs, cs] = o_int + od_ref[rs, cs]

    col = lambda c: pl.BlockSpec((tb, HG_WIDTH), lambda i, c=c: (i, c))
    return pl.pallas_call(
        body, grid=(T // tb,), name="hg_fwd",
        in_specs=[col(0), col(1), col(2), _full_spec((2, HG_WIDTH))],
        out_specs=[pl.BlockSpec((tb, HG_WIDTH), lambda i: (i, 0)),
                   pl.BlockSpec((n_sub, HG_HEADS, HG_DK, HG_DK), lambda i: (i, 0, 0, 0))],
        out_shape=[jax.ShapeDtypeStruct((T, HG_WIDTH), F32),
                   jax.ShapeDtypeStruct((T // HG_SUB, HG_HEADS, HG_DK, HG_DK), F32)],
        scratch_shapes=[pltpu.VMEM((HG_HEADS, HG_DK, HG_DK), F32), pltpu.VMEM((tb, HG_WIDTH), BF16),
                        pltpu.VMEM((tb, HG_WIDTH), F32), pltpu.VMEM((tb, HG_WIDTH), F32),
                        pltpu.VMEM((tb, HG_WIDTH), F32)],
        compiler_params=_params(("arbitrary",)),
    )(proj, proj, proj, gam)


def _hg_bwd(proj, gam, d_o, hist):
    T = proj.shape[0]
    tb = HG_TILE
    n_sub = tb // HG_SUB
    n_t = T // tb

    def body(hq_ref, hf_ref, hi_ref, gam_ref, do_ref, hist_ref, dout_ref, dlb_ref,
             dst_ref, q_ref, kk_ref, b_ref, dq_ref, dkk_ref, db_ref, dv_ref):
        i = pl.program_id(0)

        @pl.when(i == 0)
        def _():
            dst_ref[...] = jnp.zeros_like(dst_ref)
            dlb_ref[...] = jnp.zeros_like(dlb_ref)

        rowmod_w = lax.broadcasted_iota(jnp.int32, (tb, HG_WIDTH), 0) % HG_SUB
        rowmod = lax.broadcasted_iota(jnp.int32, (tb, 1), 0) % HG_SUB
        last_row = lax.broadcasted_iota(jnp.int32, (HG_SUB, 1), 0) == HG_SUB - 1
        hq = hq_ref[...]
        q, sq, s, f, kk, lf, lb = _hg_gates(hq, hf_ref[...], gam_ref[...])
        b = _prefix_in_sub(lf, rowmod_w)
        q_ref[...] = q
        kk_ref[...] = kk
        b_ref[...] = b

        for m in reversed(range(n_sub)):
            rs = slice(m * HG_SUB, (m + 1) * HG_SUB)
            for h in range(HG_HEADS):
                cs = slice(h * HG_DK, (h + 1) * HG_DK)
                st = hist_ref[m, h]
                dst = dst_ref[h]
                dstb = dst.astype(BF16)
                dos = do_ref[rs, cs].astype(BF16)
                vs = hi_ref[rs, cs].astype(BF16)
                bs = b_ref[rs, cs]
                bl = bs[HG_SUB - 1:HG_SUB, :]
                ebl = jnp.exp(bl)
                qt = q_ref[rs, cs] * jnp.exp(bs)
                decay = jnp.exp(bl - bs)
                kt = kk_ref[rs, cs] * decay
                dqt = _dot(dos, st.astype(BF16))
                dkt = _dot(vs, dstb)
                dv_ref[rs, cs] = _dot_nt(kt.astype(BF16), dstb)
                dst_ref[h] = dst * ebl + _dot_tn(dos, qt.astype(BF16))
                dq_ref[rs, cs] = dqt * jnp.exp(bs)
                dkk_ref[rs, cs] = dkt * decay
                dktk = dkt * kt
                dbl = _colsum(dktk) + _colsum(dst * st) * ebl
                db_ref[rs, cs] = dqt * qt - dktk + jnp.where(last_row, dbl, 0.0)

        for h in range(HG_HEADS):
            cs = slice(h * HG_DK, (h + 1) * HG_DK)
            qh = q[:, cs]
            kh = kk[:, cs]
            bh = b[:, cs]
            vh = hi_ref[:, cs]
            doh = do_ref[:, cs]
            dsc = jnp.sum(doh * vh, axis=1, keepdims=True)
            sc = jnp.sum(qh * kh, axis=1, keepdims=True)
            dqh = dsc * kh
            dkh = dsc * qh
            dvh = sc * doh
            dbh = jnp.zeros_like(qh)
            for d in range(1, HG_SUB):
                back = tb - d
                valid = rowmod >= d
                kd = pltpu.roll(kh, d, axis=0)
                ex = jnp.where(valid, jnp.exp(bh - pltpu.roll(bh, d, axis=0)), 0.0)
                pd = kd * ex
                sc = jnp.sum(qh * pd, axis=1, keepdims=True)
                dsc = jnp.where(valid, jnp.sum(doh * pltpu.roll(vh, d, axis=0), axis=1, keepdims=True), 0.0)
                dqh = dqh + dsc * pd
                w = dsc * qh * pd
                dkh = dkh + pltpu.roll(dsc * qh * ex, back, axis=0)
                dvh = dvh + pltpu.roll(sc * doh, back, axis=0)
                dbh = dbh + w - pltpu.roll(w, back, axis=0)
            dq_ref[:, cs] += dqh
            dkk_ref[:, cs] += dkh
            dv_ref[:, cs] += dvh
            db_ref[:, cs] += dbh

        dlf = _suffix_in_sub(db_ref[...], rowmod_w)
        df = dlf / f - dkk_ref[...]
        dout_ref[:, 0:HG_WIDTH] = dq_ref[...] * (sq * (1.0 + hq * (1.0 - sq)))
        dout_ref[:, HG_WIDTH:2 * HG_WIDTH] = df * (1.0 - lb) * (s * (1.0 - s))
        dout_ref[:, 2 * HG_WIDTH:3 * HG_WIDTH] = dv_ref[...]
        dlb_ref[...] += _colsum(df * (1.0 - s))

    col = lambda c: pl.BlockSpec((tb, HG_WIDTH), lambda i, c=c: (n_t - 1 - i, c))
    return pl.pallas_call(
        body, grid=(n_t,), name="hg_bwd",
        in_specs=[col(0), col(1), col(2), _full_spec((2, HG_WIDTH)),
                  pl.BlockSpec((tb, HG_WIDTH), lambda i: (n_t - 1 - i, 0)),
                  pl.BlockSpec((n_sub, HG_HEADS, HG_DK, HG_DK), lambda i: (n_t - 1 - i, 0, 0, 0))],
        out_specs=[pl.BlockSpec((tb, 3 * HG_WIDTH), lambda i: (n_t - 1 - i, 0)), _full_spec((1, HG_WIDTH))],
        out_shape=[jax.ShapeDtypeStruct((T, 3 * HG_WIDTH), F32), jax.ShapeDtypeStruct((1, HG_WIDTH), F32)],
        scratch_shapes=[pltpu.VMEM((HG_HEADS, HG_DK, HG_DK), F32)] + [pltpu.VMEM((tb, HG_WIDTH), F32)] * 7,
        compiler_params=_params(("arbitrary",)),
    )(proj, proj, proj, gam, d_o, hist)


def _sb_masks():
    row = lax.broadcasted_iota(jnp.int32, (SB_BLOCK, SB_BLOCK), 0)
    col = lax.broadcasted_iota(jnp.int32, (SB_BLOCK, SB_BLOCK), 1)
    suffix = (row >= col).astype(BF16)
    prefix = (row <= col).astype(BF16)
    query = lax.broadcasted_iota(jnp.int32, (SB_TQ, SB_BLOCK), 0)
    lane = lax.broadcasted_iota(jnp.int32, (SB_TQ, SB_BLOCK), 1)
    causal = [lane + r * SB_BLOCK < query for r in range(SB_TQ // SB_BLOCK)]
    return suffix, prefix, causal, lane


def _sum_right(a, suffix):
    return _dot(a.astype(BF16), suffix)


def _sb_block(qb, kj, suffix, causal, c):
    z = _dot_nt(qb, kj)
    sp = jnp.maximum(z, 0.0) + jnp.log(1.0 + jnp.exp(-jnp.abs(z)))
    if causal is not None:
        sp = jnp.where(causal, sp, 0.0)
    big_l = _sum_right(sp, suffix)
    a = jnp.exp(z - big_l - c)
    if causal is not None:
        a = jnp.where(causal, a, 0.0)
    return z, a, big_l


def _half_masked(pair, lane, scale=1.0):
    pair = pair.astype(F32) * scale
    return jnp.where(lane < SB_DH, pair, 0.0).astype(BF16), jnp.where(lane < SB_DH, 0.0, pair).astype(BF16)


def _sb_fwd(qkv):
    T = qkv.shape[0]
    width = SB_PP * SB_BLOCK
    n_chain = 2 * SB_PP
    n_sub = SB_TQ // SB_BLOCK

    def body(q_ref, k_ref, v_ref, o_ref, right_ref, acc_ref, c_ref, qm):
        i = pl.program_id(1)
        suffix, _, causal, lane = _sb_masks()
        for pp in range(SB_PP):
            qm[2 * pp], qm[2 * pp + 1] = _half_masked(q_ref[:, pp * SB_BLOCK:(pp + 1) * SB_BLOCK], lane, SB_DH ** -0.5)
        acc_ref[...] = jnp.zeros_like(acc_ref)
        c_ref[...] = jnp.zeros_like(c_ref)
        right_ref[...] = jnp.zeros_like(right_ref)

        def step(j, r=None):
            rows = pl.ds(pl.multiple_of(j * SB_BLOCK, SB_BLOCK), SB_BLOCK)
            qs = slice(0 if r is None else r * SB_BLOCK, SB_TQ)
            mask = None if r is None else causal[r][qs]
            for ch in range(n_chain):
                pair = slice((ch // 2) * SB_BLOCK, (ch // 2 + 1) * SB_BLOCK)
                c = c_ref[ch, qs]
                _, a, big_l = _sb_block(qm[ch, qs], k_ref[rows, pair], suffix, mask, c)
                acc_ref[ch, qs] += _dot(a.astype(BF16), v_ref[rows, pair])
                cols = slice(ch * SB_BLOCK, (ch + 1) * SB_BLOCK)
                right_ref[qs, cols] = jnp.where(lane[qs] == j, c, right_ref[qs, cols])
                c_ref[ch, qs] = c + jnp.broadcast_to(big_l[:, 0:1], c.shape)

        first = i * n_sub
        for r in reversed(range(n_sub)):
            step(first + r, r)

        @pl.loop(0, first)
        def _(t):
            step(first - 1 - t)

        for pp in range(SB_PP):
            o_ref[:, pp * SB_BLOCK:(pp + 1) * SB_BLOCK] = jnp.where(lane < SB_DH, acc_ref[2 * pp], acc_ref[2 * pp + 1])

    assert T // SB_BLOCK <= SB_BLOCK
    n_g = SB_WIDTH // width
    blk = lambda part: pl.BlockSpec((SB_TQ, width), lambda g, i, part=part: (i, part * n_g + g))
    whole = lambda part: pl.BlockSpec((T, width), lambda g, i, part=part: (0, part * n_g + g))
    return pl.pallas_call(
        body, grid=(n_g, T // SB_TQ), name="sb_fwd",
        in_specs=[blk(0), whole(1), whole(2)],
        out_specs=[blk(0), pl.BlockSpec((SB_TQ, n_chain * SB_BLOCK), lambda g, i: (i, g))],
        out_shape=[jax.ShapeDtypeStruct((T, SB_WIDTH), F32), jax.ShapeDtypeStruct((T, SB_HEADS * SB_BLOCK), F32)],
        scratch_shapes=[pltpu.VMEM((n_chain, SB_TQ, SB_BLOCK), F32), pltpu.VMEM((n_chain, SB_TQ, SB_BLOCK), F32),
                        pltpu.VMEM((n_chain, SB_TQ, SB_BLOCK), BF16)],
        compiler_params=_params(("parallel", "arbitrary")),
    )(qkv, qkv, qkv)


def _sb_bwd(qkv, right, d_o, after):
    T = qkv.shape[0]
    width = SB_PP * SB_BLOCK
    n_chain = 2 * SB_PP
    n_sub = SB_TQ // SB_BLOCK
    scale = SB_DH ** -0.5

    def body(q_ref, k_ref, v_ref, right_ref, do_ref, after_ref, dq_ref, dk_ref, dv_ref, acc_ref, gc_ref, qm, dom):
        i = pl.program_id(1)

        @pl.when(i == 0)
        def _():
            dk_ref[...] = jnp.zeros_like(dk_ref)
            dv_ref[...] = jnp.zeros_like(dv_ref)

        suffix, prefix, causal, lane = _sb_masks()
        for pp in range(SB_PP):
            pair = slice(pp * SB_BLOCK, (pp + 1) * SB_BLOCK)
            qm[2 * pp], qm[2 * pp + 1] = _half_masked(q_ref[:, pair], lane, scale)
            dom[2 * pp], dom[2 * pp + 1] = _half_masked(do_ref[:, pair], lane)
        acc_ref[...] = jnp.zeros_like(acc_ref)
        gc_ref[...] = jnp.zeros_like(gc_ref)

        def step(j, r=None):
            rows = pl.ds(pl.multiple_of(j * SB_BLOCK, SB_BLOCK), SB_BLOCK)
            qs = slice(0 if r is None else r * SB_BLOCK, SB_TQ)
            mask = None if r is None else causal[r][qs]
            for pp in range(SB_PP):
                pair = slice(pp * SB_BLOCK, (pp + 1) * SB_BLOCK)
                kj = k_ref[rows, pair]
                vj = v_ref[rows, pair]
                dk = jnp.zeros((SB_BLOCK, SB_BLOCK), F32)
                dv = jnp.zeros((SB_BLOCK, SB_BLOCK), F32)
                for ch in (2 * pp, 2 * pp + 1):
                    cols = slice(ch * SB_BLOCK, (ch + 1) * SB_BLOCK)
                    c = jnp.sum(jnp.where(lane[qs] == j, right_ref[qs, cols], 0.0), axis=1, keepdims=True)
                    z, a, _ = _sb_block(qm[ch, qs], kj, suffix, mask, c)
                    g = a * _dot_nt(dom[ch, qs], vj)
                    g_left = _sum_right(g, prefix)
                    gc = gc_ref[ch, qs]
                    dz = g - _sigmoid(z) * (gc + g_left)
                    if mask is not None:
                        dz = jnp.where(mask, dz, 0.0)
                    dzb = dz.astype(BF16)
                    dk = dk + _dot_tn(dzb, qm[ch, qs])
                    dv = dv + _dot_tn(a.astype(BF16), dom[ch, qs])
                    acc_ref[ch, qs] += _dot(dzb, kj)
                    gc_ref[ch, qs] = gc + jnp.broadcast_to(g_left[:, SB_BLOCK - 1:SB_BLOCK], gc.shape)
                dk_ref[rows, pair] += dk
                dv_ref[rows, pair] += dv

        first = i * n_sub

        @pl.loop(0, first)
        def _(j):
            step(j)

        for r in range(n_sub):
            step(first + r, r)
        for pp in range(SB_PP):
            dq_ref[:, pp * SB_BLOCK:(pp + 1) * SB_BLOCK] = scale * jnp.where(lane < SB_DH, acc_ref[2 * pp],
                                                                             acc_ref[2 * pp + 1])

    n_g = SB_WIDTH // width
    blk = lambda part: pl.BlockSpec((SB_TQ, width), lambda g, i, part=part: (i, part * n_g + g))
    whole = lambda part: pl.BlockSpec((T, width), lambda g, i, part=part: (0, part * n_g + g))
    return pl.pallas_call(
        body, grid=(n_g, T // SB_TQ), name="sb_bwd",
        in_specs=[blk(0), whole(1), whole(2), pl.BlockSpec((SB_TQ, n_chain * SB_BLOCK), lambda g, i: (i, g)), blk(0),
                  pl.BlockSpec(memory_space=pl.ANY)],
        out_specs=[blk(0), whole(0), whole(0)],
        out_shape=[jax.ShapeDtypeStruct((T, SB_WIDTH), F32)] * 3,
        scratch_shapes=[pltpu.VMEM((n_chain, SB_TQ, SB_BLOCK), F32), pltpu.VMEM((n_chain, SB_TQ, SB_BLOCK), F32),
                        pltpu.VMEM((n_chain, SB_TQ, SB_BLOCK), BF16), pltpu.VMEM((n_chain, SB_TQ, SB_BLOCK), BF16)],
        compiler_params=_params(("parallel", "arbitrary")),
    )(qkv, qkv, qkv, right, d_o, after)


HBM = pl.BlockSpec(memory_space=pltpu.HBM)
MESH = pl.DeviceIdType.MESH


def _place():
    return lax.axis_index("x"), lax.axis_index("y"), lax.axis_index("c")


def _all_gather(block):
    rows, cols = block.shape

    def body(x_ref, out_ref, send_sems, recv_sems, local_sem):
        x, y, c = _place()
        me, sibling = (x, y, c), (x, y, 1 - c)
        chips = [(1 - x, y), (x, 1 - y), (1 - x, 1 - y)]

        def slot(px, py, pc):
            return out_ref.at[4 * px + 2 * py + pc]

        def copy(k, blk, to, src=None):
            return pltpu.make_async_remote_copy(
                src_ref=slot(*blk) if src is None else src, dst_ref=slot(*blk),
                send_sem=send_sems.at[k], recv_sem=recv_sems.at[k], device_id=to, device_id_type=MESH)

        mine = pltpu.make_async_copy(x_ref, slot(*me), local_sem)
        mine.start()
        first = [copy(0, me, sibling, src=x_ref)]
        first += [copy(1 + j, me, (*chip, c), src=x_ref) for j, chip in enumerate(chips)]
        for cp in first:
            cp.start()
        passed = [copy(4 + j, (*chip, c), sibling) for j, chip in enumerate(chips)]
        for j, chip in enumerate(chips):
            copy(1 + j, (*chip, c), me).wait_recv()
            passed[j].start()
        copy(0, sibling, me).wait_recv()
        for j, chip in enumerate(chips):
            copy(4 + j, (*chip, 1 - c), me).wait_recv()
        for cp in first + passed:
            cp.wait_send()
        mine.wait()

    return pl.pallas_call(
        body, name="all_gather",
        out_shape=jax.ShapeDtypeStruct((N_DEV, rows, cols), block.dtype),
        in_specs=[HBM], out_specs=HBM,
        scratch_shapes=[pltpu.SemaphoreType.DMA((7,)), pltpu.SemaphoreType.DMA((7,)), pltpu.SemaphoreType.DMA],
    )(block)


SEM = pl.BlockSpec(memory_space=pltpu.SEMAPHORE)
ANY = pl.BlockSpec(memory_space=pl.ANY)
SPLIT_EFFECT = pltpu.SideEffectType.DATAFLOW_SIDE_EFFECTING
N_PEER = N_DEV - 1


def _flow_copies(kind, src_ref, land_ref, send_sems, recv_sems):
    x, y, c = _place()
    copies = []
    for r in range(1, N_DEV):
        px = 1 - x if r & 4 else x
        py = 1 - y if r & 2 else y
        pc = 1 - c if r & 1 else c
        if kind == "gather":
            src, dst = src_ref, land_ref.at[4 * x + 2 * y + c]
        else:
            src, dst = src_ref.at[4 * px + 2 * py + pc], land_ref.at[r - 1]
        copies.append(pltpu.make_async_remote_copy(
            src_ref=src, dst_ref=dst, send_sem=send_sems.at[r - 1], recv_sem=recv_sems.at[r - 1],
            device_id=(px, py, pc), device_id_type=MESH))
    return copies


def _landing(kind, src):
    if kind == "gather":
        return lax.empty((N_DEV,) + src.shape, src.dtype)
    return lax.empty((N_PEER,) + src.shape[1:], src.dtype)


def _push_start(kinds, arrays, name):
    n = len(arrays)

    def body(*refs):
        ins, sems, token = refs[:n], refs[n:2 * n], refs[3 * n]
        for f, kind in enumerate(kinds):
            for cp in _flow_copies(kind, ins[2 * f], ins[2 * f + 1], sems[2 * f], sems[2 * f + 1]):
                cp.start()
        token[...] = jnp.zeros_like(token)

    outs = pl.pallas_call(
        body, name=name,
        out_shape=[pltpu.SemaphoreType.DMA((N_PEER,))] * n + [pltpu.HBM(a.shape, a.dtype) for a in arrays]
        + [jax.ShapeDtypeStruct((8, 128), F32)],
        in_specs=[HBM] * n, out_specs=[SEM] * n + [HBM] * n + [pl.BlockSpec(memory_space=pltpu.VMEM)],
        input_output_aliases={i: n + i for i in range(n)},
        compiler_params=pltpu.CompilerParams(has_side_effects=SPLIT_EFFECT),
    )(*[pltpu.with_memory_space_constraint(a, pltpu.HBM) for a in arrays])
    return outs[:n], outs[n:2 * n], outs[2 * n]


def _push_wait(kinds, sems, arrays, after, name):
    n = len(arrays)

    def body(*refs):
        ins, sem_refs = refs[:n], refs[n:2 * n]
        for f, kind in enumerate(kinds):
            for cp in _flow_copies(kind, ins[2 * f], ins[2 * f + 1], sem_refs[2 * f], sem_refs[2 * f + 1]):
                cp.wait_send()
                cp.wait_recv()

    outs = pl.pallas_call(
        body, name=name,
        out_shape=[pltpu.HBM(a.shape, a.dtype) for a in arrays],
        in_specs=[HBM] * n + [SEM] * n + [ANY], out_specs=[HBM] * n,
        input_output_aliases={i: i for i in range(n)},
        compiler_params=pltpu.CompilerParams(has_side_effects=SPLIT_EFFECT),
    )(*arrays, *sems, after)
    return outs


def _sum_blocks(own, got, name):
    n, rows, cols = got.shape
    tr = 64
    assert rows % tr == 0

    def body(own_ref, got_ref, out_ref):
        acc = own_ref[...].astype(F32)
        for r in range(n):
            acc = acc + got_ref[r].astype(F32)
        out_ref[...] = acc

    return pl.pallas_call(
        body, grid=(rows // tr,), name=name,
        in_specs=[pl.BlockSpec((tr, cols), lambda i: (i, 0)), pl.BlockSpec((n, tr, cols), lambda i: (0, i, 0))],
        out_specs=pl.BlockSpec((tr, cols), lambda i: (i, 0)),
        out_shape=jax.ShapeDtypeStruct((rows, cols), F32),
        compiler_params=_params(("parallel",)),
    )(own, got)


def _adamw_math(w, g, m, v):
    m = ADAM_B1 * m + (1.0 - ADAM_B1) * g
    v = ADAM_B2 * v + (1.0 - ADAM_B2) * (g * g)
    m_hat = m / (1.0 - ADAM_B1 ** ADAM_STEP)
    v_hat = v / (1.0 - ADAM_B2 ** ADAM_STEP)
    delta = -ADAM_LR * (m_hat / (jnp.sqrt(v_hat) + ADAM_EPS) + ADAM_WD * w)
    return delta, m, v


def _adamw(w, g, m, v, name):
    rows, cols = w.shape
    tr = rows if rows <= 352 else 256

    def body(w_ref, g_ref, m_ref, v_ref, d_out, m_out, v_out):
        d_out[...], m_out[...], v_out[...] = _adamw_math(w_ref[...], g_ref[...], m_ref[...], v_ref[...])

    spec = pl.BlockSpec((tr, cols), lambda i: (i, 0))
    return pl.pallas_call(
        body, grid=(rows // tr,), name=name,
        in_specs=[spec] * 4, out_specs=[spec] * 3,
        out_shape=[jax.ShapeDtypeStruct((rows, cols), F32)] * 3,
        compiler_params=_params(("parallel",)),
    )(w, g, m, v)


ROW_ATTN_PRE, ROW_ATTN_POST, ROW_FFN_PRE, ROW_FFN_POST, ROW_OUT_NORMS, ROW_GAMMA, ROW_LOSS = range(7)


def _small_update(smalls, w, m, v):
    def body(s_ref, w_ref, m_ref, v_ref, g_out, d_out, m_out, v_out):
        g = s_ref[0]
        for j in range(1, N_DEV):
            g = g + s_ref[j]
        wv = w_ref[...]
        g0 = wv[ROW_GAMMA:ROW_GAMMA + 1, 0:HG_WIDTH]
        g1 = wv[ROW_GAMMA:ROW_GAMMA + 1, HG_WIDTH:]
        mx = jnp.maximum(g0, g1)
        e0 = jnp.exp(g0 - mx)
        e1 = jnp.exp(g1 - mx)
        lb = e0 / (e0 + e1)
        dg0 = g[ROW_GAMMA:ROW_GAMMA + 1, 0:HG_WIDTH] * lb * (1.0 - lb)
        row = lax.broadcasted_iota(jnp.int32, g.shape, 0)
        g = jnp.where(row == ROW_GAMMA, jnp.concatenate([dg0, -dg0], axis=1), g)
        g_out[...] = g
        d_out[...], m_out[...], v_out[...] = _adamw_math(wv, g, m_ref[...], v_ref[...])

    return pl.pallas_call(
        body, name="small_update",
        out_shape=[jax.ShapeDtypeStruct((8, D_MODEL), F32)] * 4,
    )(smalls, w, m, v)


def _local_step(x, p, target, g_attn_pre, w_in, gamma, g_hg, g_sb, g_attn_post, g_ffn_pre, g_ffn_post, later_weights,
                grads_ready):
    proj, u, qkv = _in_proj(x, g_attn_pre, w_in)
    o_hg, hist = _hg_fwd(proj, gamma)
    o_sb, right = _sb_fwd(qkv)
    w_out, w_gu, w_down, w_pp, w_pg = later_weights(o_sb)
    cat, mix, h1 = _out_proj(o_hg, proj, o_sb, x, g_hg, g_sb, g_attn_post, w_out)
    u2, gate, up, y, h2 = _ffn_fwd(h1, g_ffn_pre, w_gu, w_down, g_ffn_post)
    dh2, de, dg, loss = _ple_loss(h2, p, target, w_pp, w_pg)

    d_wpp = _wgrad(p, de, "wgrad_ple_proj")
    d_wpg = _wgrad(h2, dg, "wgrad_ple_gate")
    dy, a, dgate, dup, dh1, dg_ffn_post, dg_ffn_pre = _ffn_bwd(dh2, y, h1, gate, up, g_ffn_post, g_ffn_pre, w_gu, w_down)
    d_wdown = _wgrad(a, dy, "wgrad_down")
    sent = grads_ready("ffn", (_wgrad(u2, dgate, "wgrad_gate"), _wgrad(u2, dup, "wgrad_up"), d_wdown, d_wpp, d_wpg))
    dmix, d_ohg, d_hg, d_osb, dg_attn_post, dg_hg, dg_sb = _out_bwd(dh1, mix, o_hg, proj, o_sb, g_hg, g_sb, g_attn_post,
                                                                    w_out, sent)
    sent = grads_ready("out", (_wgrad(cat, dmix, "wgrad_out"),))
    dq, dk, dv = _sb_bwd(qkv, right, d_osb, sent)
    d_hgrn, d_lb = _hg_bwd(proj, gamma, d_ohg, hist)
    dproj, dx, dg_attn_pre = _in_bwd(d_hgrn, d_hg, dq, dk, dv, x, dh1, g_attn_pre, w_in)
    grads_ready("in", (_wgrad(u, dproj, "wgrad_in"),))
    return loss, dx, (dg_attn_pre, dg_attn_post, dg_ffn_pre, dg_ffn_post, dg_hg, dg_sb, d_lb)


def _cols_to_rows(a, n_blocks=N_DEV):
    rows, cols = a.shape
    n = cols // n_blocks
    return a.reshape(rows, n_blocks, n).transpose(1, 0, 2).reshape(n_blocks, rows * n // D_MODEL, D_MODEL)


def _rows_to_cols(a, rows):
    n = a.shape[1] * D_MODEL // rows
    return a.reshape(N_DEV, rows, n).transpose(1, 0, 2).reshape(rows, N_DEV * n)


def _pack_row(*parts):
    return jnp.concatenate([q.reshape(1, -1) for q in parts], axis=1)


def kernel(x, p, attn_pre_norm, w_in, hg_lower_gamma, hg_out_norm, sb_out_norm, w_out, attn_post_norm, ffn_pre_norm, w_gate_up, w_down, ffn_post_norm, ple_proj, ple_gate, loss_target, m_attn_pre_norm, m_w_in, m_hg_lower_gamma, m_hg_out_norm, m_sb_out_norm, m_w_out, m_attn_post_norm, m_ffn_pre_norm, m_w_gate_up, m_w_down, m_ffn_post_norm, m_ple_proj, m_ple_gate, v_attn_pre_norm, v_w_in, v_hg_lower_gamma, v_hg_out_norm, v_sb_out_norm, v_w_out, v_attn_post_norm, v_ffn_pre_norm, v_w_gate_up, v_w_down, v_ffn_post_norm, v_ple_proj, v_ple_gate):
    big = (w_in[0], w_out[0], w_gate_up[0], w_down[0], ple_proj[0], ple_gate[0])
    big_m = (m_w_in[0], m_w_out[0], m_w_gate_up[0], m_w_down[0], m_ple_proj[0], m_ple_gate[0])
    big_v = (v_w_in[0], v_w_out[0], v_w_gate_up[0], v_w_down[0], v_ple_proj[0], v_ple_gate[0])
    names = ("w_in", "w_out", "w_gate_up", "w_down", "ple_proj", "ple_gate")

    shards = [w.astype(BF16).reshape(-1, D_MODEL) for w in big]
    gathered_in = _all_gather(shards[0])
    later_pack = jnp.concatenate(shards[1:], axis=0)
    me = 4 * lax.axis_index("x") + 2 * lax.axis_index("y") + lax.axis_index("c")
    w_sems, w_thru, token = _push_start(("gather",), (later_pack, _landing("gather", later_pack)), "weights_start")

    def later_weights(after):
        mine, got = _push_wait(("gather",), w_sems, w_thru, after, "weights_wait")
        got = lax.dynamic_update_index_in_dim(got, mine, me, 0)
        part = lambda i: got[:, PACK_OFFS[i] - PACK_SIZES[0]:PACK_OFFS[i] - PACK_SIZES[0] + PACK_SIZES[i], :]
        return (part(1).reshape(D_MODEL, D_MODEL), _rows_to_cols(part(2), D_MODEL), part(3).reshape(D_FF, D_MODEL),
                _rows_to_cols(part(4), PLE_DIM), part(5).reshape(D_MODEL, D_MODEL))

    flights = {}

    def grads_ready(group, grads):
        if group == "ffn":
            d_gate, d_up, d_down, d_pp, d_pg = grads
            half = N_DEV // 2
            blocks = jnp.concatenate([
                jnp.concatenate([_cols_to_rows(d_gate, half), _cols_to_rows(d_up, half)], axis=0),
                d_down.reshape(N_DEV, -1, D_MODEL), _cols_to_rows(d_pp), d_pg.reshape(N_DEV, -1, D_MODEL)], axis=1)
        elif group == "out":
            blocks = grads[0].reshape(N_DEV, -1, D_MODEL)
        else:
            flights[group] = _cols_to_rows(grads[0])
            return None
        sems, thru, sent = _push_start(("scatter",), (blocks, _landing("scatter", blocks)), "grads_" + group + "_start")
        flights[group] = (sems, thru)
        return sent

    loss, dx, smalls = _local_step(
        x[0], p[0, 0], loss_target[0], attn_pre_norm + token[0:1, 0:1], _rows_to_cols(gathered_in, D_MODEL),
        hg_lower_gamma, hg_out_norm, sb_out_norm, attn_post_norm, ffn_pre_norm, ffn_post_norm, later_weights, grads_ready)
    dg_attn_pre, dg_attn_post, dg_ffn_pre, dg_ffn_post, dg_hg, dg_sb, d_lb = smalls

    zeros_half = jnp.zeros((1, HG_WIDTH), F32)
    small_pack = jnp.concatenate([
        dg_attn_pre, dg_attn_post, dg_ffn_pre, dg_ffn_post, _pack_row(dg_hg, dg_sb), _pack_row(d_lb, zeros_half),
        jnp.broadcast_to(loss[0:1, 0:1], (1, D_MODEL)), jnp.zeros((1, D_MODEL), F32)], axis=0)
    in_blocks = flights["in"]
    in_sems, in_thru, in_token = _push_start(
        ("scatter", "gather"),
        (in_blocks, _landing("scatter", in_blocks), small_pack, _landing("gather", small_pack)), "grads_in_start")
    sent_ffn, got_ffn = _push_wait(("scatter",), *flights["ffn"], in_token, "grads_ffn_wait")
    sent_out, got_out = _push_wait(("scatter",), *flights["out"], got_ffn, "grads_out_wait")
    sent_in, got_in, my_small, all_smalls = _push_wait(("scatter", "gather"), in_sems, in_thru, got_out, "grads_in_wait")
    all_smalls = lax.dynamic_update_index_in_dim(all_smalls, my_small, me, 0)
    own = lambda blocks: lax.dynamic_index_in_dim(blocks, me, 0, keepdims=False)
    sum_ffn = _sum_blocks(own(sent_ffn), got_ffn, "grad_sum_ffn")
    sums = (_sum_blocks(own(sent_in), got_in, "grad_sum_in"), _sum_blocks(own(sent_out), got_out, "grad_sum_out"),
            sum_ffn[0:PACK_SIZES[2]], sum_ffn[PACK_SIZES[2]:PACK_SIZES[2] + PACK_SIZES[3]],
            sum_ffn[PACK_SIZES[2] + PACK_SIZES[3]:PACK_SIZES[2] + PACK_SIZES[3] + PACK_SIZES[4]],
            sum_ffn[PACK_SIZES[2] + PACK_SIZES[3] + PACK_SIZES[4]:])

    out_g, out_d, out_m, out_v = {}, {}, {}, {}
    for i, name in enumerate(names):
        g = sums[i].reshape(big[i].shape)
        out_g[name] = g[None]
        d, m, v = _adamw(big[i], g, big_m[i], big_v[i], "adamw_" + name)
        out_d[name], out_m[name], out_v[name] = d[None], m[None], v[None]

    def small_rows(pre, gam, hg, sb, post, fpre, fpost):
        return jnp.concatenate([pre, post, fpre, fpost, _pack_row(hg, sb), _pack_row(gam[0], gam[1]),
                                jnp.zeros((2, D_MODEL), F32)], axis=0)

    packs = _small_update(
        all_smalls,
        small_rows(attn_pre_norm, hg_lower_gamma, hg_out_norm, sb_out_norm, attn_post_norm, ffn_pre_norm, ffn_post_norm),
        small_rows(m_attn_pre_norm, m_hg_lower_gamma, m_hg_out_norm, m_sb_out_norm, m_attn_post_norm, m_ffn_pre_norm,
                   m_ffn_post_norm),
        small_rows(v_attn_pre_norm, v_hg_lower_gamma, v_hg_out_norm, v_sb_out_norm, v_attn_post_norm, v_ffn_pre_norm,
                   v_ffn_post_norm))

    def unpack(pk):
        return {
            "attn_pre_norm": pk[ROW_ATTN_PRE:ROW_ATTN_PRE + 1],
            "hg_lower_gamma": pk[ROW_GAMMA].reshape(2, HG_WIDTH),
            "hg_out_norm": pk[ROW_OUT_NORMS:ROW_OUT_NORMS + 1, :HG_WIDTH],
            "sb_out_norm": pk[ROW_OUT_NORMS:ROW_OUT_NORMS + 1, HG_WIDTH:],
            "attn_post_norm": pk[ROW_ATTN_POST:ROW_ATTN_POST + 1],
            "ffn_pre_norm": pk[ROW_FFN_PRE:ROW_FFN_PRE + 1],
            "ffn_post_norm": pk[ROW_FFN_POST:ROW_FFN_POST + 1],
        }

    sg, sd, sm, sv = (unpack(pk) for pk in packs)
    out_g.update(sg), out_d.update(sd), out_m.update(sm), out_v.update(sv)
    order = ("attn_pre_norm", "w_in", "hg_lower_gamma", "hg_out_norm", "sb_out_norm", "w_out", "attn_post_norm",
             "ffn_pre_norm", "w_gate_up", "w_down", "ffn_post_norm", "ple_proj", "ple_gate")
    total_loss = packs[0][ROW_LOSS, 0]
    return (total_loss, dx[None], *[out_g[n] for n in order], *[out_d[n] for n in order],
            *[out_m[n] for n in order], *[out_v[n] for n in order])
```

```python
import functools

import jax
import jax.numpy as jnp
from jax import lax
from jax.experimental import pallas as pl
from jax.experimental.pallas import tpu as pltpu

F32 = jnp.float32
BF16 = jnp.bfloat16

D_MODEL = 1024
HG_WIDTH = 512
HG_HEADS = 4
HG_DK = 128
SB_WIDTH = 512
SB_HEADS = 8
SB_DH = 64
SB_BLOCK = 128
SB_PP = 2
SB_TQ = 512
D_FF = 2816
PLE_DIM = 256
IN_COLS = 4 * HG_WIDTH + 3 * SB_WIDTH
EPS = 1e-6
N_DEV = 8

HG_SUB = 16
HG_TILE = 128
FFN_TF = 256
ROW_TILE = 256
VMEM_LIMIT = 56 * 1024 * 1024
WGRAD_ACC_BYTES = 8 * 1024 * 1024

ADAM_LR = 0.001
ADAM_B1 = 0.9
ADAM_B2 = 0.999
ADAM_EPS = 1e-08
ADAM_WD = 0.01
ADAM_STEP = 10

NT_DIMS = (((1,), (1,)), ((), ()))
TN_DIMS = (((0,), (0,)), ((), ()))


def _params(semantics):
    return pltpu.CompilerParams(dimension_semantics=semantics, vmem_limit_bytes=VMEM_LIMIT)


def _dot(a, b):
    return jnp.dot(a, b, preferred_element_type=F32)


def _dot_nt(a, b):
    return lax.dot_general(a, b, NT_DIMS, preferred_element_type=F32)


def _dot_tn(a, b):
    return lax.dot_general(a, b, TN_DIMS, preferred_element_type=F32)


def _sigmoid(z):
    return 1.0 / (1.0 + jnp.exp(-z))


def _rstd(xv):
    return lax.rsqrt(jnp.mean(xv * xv, axis=-1, keepdims=True) + EPS)


def _rms_bwd(xv, r, g, dn):
    xn = xv * r
    gh = dn * g
    dx = r * (gh - xn * jnp.mean(gh * xn, axis=-1, keepdims=True))
    return dx, dn * xn


def _colsum(a):
    return jnp.sum(a, axis=0, keepdims=True)


def _row_spec(tm, width, col=0):
    return pl.BlockSpec((tm, width), lambda i, col=col: (i, col))


def _full_spec(shape):
    return pl.BlockSpec(shape, lambda i: (0,) * len(shape))


def _in_proj(x, g, w_t):
    T = x.shape[0]
    tm = ROW_TILE

    def body(x_ref, g_ref, w_ref, proj_ref, u_ref, qkv_ref):
        xv = x_ref[...]
        u = (xv * _rstd(xv) * g_ref[...]).astype(BF16)
        u_ref[...] = u
        proj = _dot_nt(u, w_ref[...])
        proj_ref[...] = proj
        qkv_ref[...] = proj[:, 4 * HG_WIDTH:].astype(BF16)

    return pl.pallas_call(
        body, grid=(T // tm,), name="in_proj",
        in_specs=[_row_spec(tm, D_MODEL), _full_spec((1, D_MODEL)), _full_spec((IN_COLS, D_MODEL))],
        out_specs=[_row_spec(tm, IN_COLS), _row_spec(tm, D_MODEL), _row_spec(tm, 3 * SB_WIDTH)],
        out_shape=[jax.ShapeDtypeStruct((T, IN_COLS), F32), jax.ShapeDtypeStruct((T, D_MODEL), BF16),
                   jax.ShapeDtypeStruct((T, 3 * SB_WIDTH), BF16)],
        compiler_params=_params(("parallel",)),
    )(x, g, w_t)


def _out_proj(o_hg, proj, o_sb, x, g_hg, g_sb, g_post, w):
    T = x.shape[0]
    tm = ROW_TILE

    def body(ohg_ref, hg_ref, osb_ref, x_ref, ghg_ref, gsb_ref, gpost_ref, w_ref, cat_ref, mix_ref, h1_ref):
        ohg = ohg_ref[...]
        hg = hg_ref[...]
        osb = osb_ref[...]
        a = ohg * _rstd(ohg) * ghg_ref[...] * (hg * _sigmoid(hg))
        n2 = osb * _rstd(osb) * gsb_ref[...]
        cat = jnp.concatenate([a, n2], axis=1).astype(BF16)
        cat_ref[...] = cat
        mix = _dot(cat, w_ref[...])
        mix_ref[...] = mix
        h1_ref[...] = x_ref[...] + mix * _rstd(mix) * gpost_ref[...]

    return pl.pallas_call(
        body, grid=(T // tm,), name="out_proj",
        in_specs=[_row_spec(tm, HG_WIDTH), _row_spec(tm, HG_WIDTH, 3), _row_spec(tm, SB_WIDTH), _row_spec(tm, D_MODEL),
                  _full_spec((1, HG_WIDTH)), _full_spec((1, SB_WIDTH)), _full_spec((1, D_MODEL)),
                  _full_spec((D_MODEL, D_MODEL))],
        out_specs=[_row_spec(tm, D_MODEL)] * 3,
        out_shape=[jax.ShapeDtypeStruct((T, D_MODEL), BF16), jax.ShapeDtypeStruct((T, D_MODEL), F32),
                   jax.ShapeDtypeStruct((T, D_MODEL), F32)],
        compiler_params=_params(("parallel",)),
    )(o_hg, proj, o_sb, x, g_hg, g_sb, g_post, w)


def _ffn_fwd(h1, g_pre, w_gu_t, w_down, g_post):
    T = h1.shape[0]
    tm = 2 * ROW_TILE
    n_f = D_FF // FFN_TF

    def body(h1_ref, gpre_ref, wgu_ref, wd_ref, gpost_ref, u2_ref, gu_ref, y_ref, h2_ref, acc_ref):
        j = pl.program_id(1)

        @pl.when(j == 0)
        def _():
            hv = h1_ref[...]
            u2_ref[...] = (hv * _rstd(hv) * gpre_ref[...]).astype(BF16)
            acc_ref[...] = jnp.zeros_like(acc_ref)

        u2 = u2_ref[...]
        gate = _dot_nt(u2, wgu_ref[0])
        up = _dot_nt(u2, wgu_ref[1])
        gu_ref[0] = gate
        gu_ref[1] = up
        a = (gate * _sigmoid(gate) * up).astype(BF16)
        acc_ref[...] += _dot(a, wd_ref[...])

        @pl.when(j == n_f - 1)
        def _():
            y = acc_ref[...]
            y_ref[...] = y
            h2_ref[...] = h1_ref[...] + y * _rstd(y) * gpost_ref[...]

    row = lambda width: pl.BlockSpec((tm, width), lambda i, j: (i, 0))
    vec = pl.BlockSpec((1, D_MODEL), lambda i, j: (0, 0))
    return pl.pallas_call(
        body, grid=(T // tm, n_f), name="ffn_fwd",
        in_specs=[row(D_MODEL), vec,
                  pl.BlockSpec((2, FFN_TF, D_MODEL), lambda i, j: (0, j, 0)),
                  pl.BlockSpec((FFN_TF, D_MODEL), lambda i, j: (j, 0)), vec],
        out_specs=[row(D_MODEL),
                   pl.BlockSpec((2, tm, FFN_TF), lambda i, j: (0, i, j)),
                   row(D_MODEL), row(D_MODEL)],
        out_shape=[jax.ShapeDtypeStruct((T, D_MODEL), BF16), jax.ShapeDtypeStruct((2, T, D_FF), F32),
                   jax.ShapeDtypeStruct((T, D_MODEL), F32), jax.ShapeDtypeStruct((T, D_MODEL), F32)],
        scratch_shapes=[pltpu.VMEM((tm, D_MODEL), F32)],
        compiler_params=_params(("parallel", "arbitrary")),
    )(h1, g_pre, w_gu_t, w_down, g_post)


def _ple_loss(h2, p, target, w_pp_t, w_pg):
    T = h2.shape[0]
    tm = ROW_TILE

    def body(h2_ref, p_ref, t_ref, wpp_ref, wpg_ref, dh2_ref, de_ref, dg_ref, loss_ref):
        i = pl.program_id(0)
        h2v = h2_ref[...]
        e = _dot_nt(p_ref[...].astype(BF16), wpp_ref[...])
        sg = _sigmoid(_dot(h2v.astype(BF16), wpg_ref[...]))
        diff = h2v + e * sg - t_ref[...]
        part = jnp.sum(jnp.sum(diff * diff, axis=1, keepdims=True), axis=0, keepdims=True) * (0.5 / D_MODEL)

        @pl.when(i == 0)
        def _():
            loss_ref[...] = jnp.zeros_like(loss_ref)

        loss_ref[...] += jnp.broadcast_to(part, loss_ref.shape)
        dh3 = diff * (1.0 / D_MODEL)
        de_ref[...] = (dh3 * sg).astype(BF16)
        dg = (dh3 * e * sg * (1.0 - sg)).astype(BF16)
        dg_ref[...] = dg
        dh2_ref[...] = dh3 + _dot_nt(dg, wpg_ref[...])

    return pl.pallas_call(
        body, grid=(T // tm,), name="ple_loss",
        in_specs=[_row_spec(tm, D_MODEL), _row_spec(tm, PLE_DIM), _row_spec(tm, D_MODEL),
                  _full_spec((D_MODEL, PLE_DIM)), _full_spec((D_MODEL, D_MODEL))],
        out_specs=[_row_spec(tm, D_MODEL)] * 3 + [_full_spec((8, 128))],
        out_shape=[jax.ShapeDtypeStruct((T, D_MODEL), F32), jax.ShapeDtypeStruct((T, D_MODEL), BF16),
                   jax.ShapeDtypeStruct((T, D_MODEL), BF16), jax.ShapeDtypeStruct((8, 128), F32)],
        compiler_params=_params(("arbitrary",)),
    )(h2, p, target, w_pp_t, w_pg)


def _ffn_bwd(dh2, y, h1, gu, g_post, g_pre, w_gu_t, w_down):
    T = h1.shape[0]
    tm = 2 * ROW_TILE
    n_f = D_FF // FFN_TF

    def body(dh2_ref, y_ref, h1_ref, gu_ref, gpost_ref, gpre_ref, wgu_ref, wd_ref,
             dy_ref, a_ref, dgu_ref, dh1_ref, dgpost_ref, dgpre_ref, du2_ref):
        i = pl.program_id(0)
        j = pl.program_id(1)

        @pl.when((i == 0) & (j == 0))
        def _():
            dgpost_ref[...] = jnp.zeros_like(dgpost_ref)
            dgpre_ref[...] = jnp.zeros_like(dgpre_ref)

        @pl.when(j == 0)
        def _():
            yv = y_ref[...]
            dy, gterm = _rms_bwd(yv, _rstd(yv), gpost_ref[...], dh2_ref[...])
            dy_ref[...] = dy.astype(BF16)
            dgpost_ref[...] += _colsum(gterm)
            du2_ref[...] = jnp.zeros_like(du2_ref)

        da = _dot_nt(dy_ref[...], wd_ref[...])
        gate = gu_ref[0]
        up = gu_ref[1]
        s = _sigmoid(gate)
        sil = gate * s
        a_ref[...] = (sil * up).astype(BF16)
        dgate = (da * up * (s * (1.0 + gate * (1.0 - s)))).astype(BF16)
        dup = (da * sil).astype(BF16)
        dgu_ref[0] = dgate
        dgu_ref[1] = dup
        du2_ref[...] += _dot(dgate, wgu_ref[0]) + _dot(dup, wgu_ref[1])

        @pl.when(j == n_f - 1)
        def _():
            hv = h1_ref[...]
            dx, gterm = _rms_bwd(hv, _rstd(hv), gpre_ref[...], du2_ref[...])
            dh1_ref[...] = dh2_ref[...] + dx
            dgpre_ref[...] += _colsum(gterm)

    row = lambda width: pl.BlockSpec((tm, width), lambda i, j: (i, 0))
    col = pl.BlockSpec((tm, FFN_TF), lambda i, j: (i, j))
    both = pl.BlockSpec((2, tm, FFN_TF), lambda i, j: (0, i, j))
    vec = pl.BlockSpec((1, D_MODEL), lambda i, j: (0, 0))
    return pl.pallas_call(
        body, grid=(T // tm, n_f), name="ffn_bwd",
        in_specs=[row(D_MODEL), row(D_MODEL), row(D_MODEL), both, vec, vec,
                  pl.BlockSpec((2, FFN_TF, D_MODEL), lambda i, j: (0, j, 0)),
                  pl.BlockSpec((FFN_TF, D_MODEL), lambda i, j: (j, 0))],
        out_specs=[row(D_MODEL), col, both, row(D_MODEL), vec, vec],
        out_shape=[jax.ShapeDtypeStruct((T, D_MODEL), BF16), jax.ShapeDtypeStruct((T, D_FF), BF16),
                   jax.ShapeDtypeStruct((2, T, D_FF), BF16),
                   jax.ShapeDtypeStruct((T, D_MODEL), F32), jax.ShapeDtypeStruct((1, D_MODEL), F32),
                   jax.ShapeDtypeStruct((1, D_MODEL), F32)],
        scratch_shapes=[pltpu.VMEM((tm, D_MODEL), F32)],
        compiler_params=_params(("arbitrary", "arbitrary")),
    )(dh2, y, h1, gu, g_post, g_pre, w_gu_t, w_down)


def _out_bwd(dh1, mix, o_hg, proj, o_sb, g_hg, g_sb, g_post, w, after):
    T = dh1.shape[0]
    tm = ROW_TILE

    def body(dh1_ref, mix_ref, ohg_ref, hg_ref, osb_ref, ghg_ref, gsb_ref, gpost_ref, w_ref, after_ref,
             dmix_ref, dohg_ref, dhg_ref, dosb_ref, dgpost_ref, dghg_ref, dgsb_ref):
        i = pl.program_id(0)

        @pl.when(i == 0)
        def _():
            dgpost_ref[...] = jnp.zeros_like(dgpost_ref)
            dghg_ref[...] = jnp.zeros_like(dghg_ref)
            dgsb_ref[...] = jnp.zeros_like(dgsb_ref)

        mix = mix_ref[...]
        dmix, gterm = _rms_bwd(mix, _rstd(mix), gpost_ref[...], dh1_ref[...])
        dgpost_ref[...] += _colsum(gterm)
        dmix = dmix.astype(BF16)
        dmix_ref[...] = dmix
        dcat = _dot_nt(dmix, w_ref[...])
        da = dcat[:, :HG_WIDTH]
        dn2 = dcat[:, HG_WIDTH:]
        ohg = ohg_ref[...]
        r1 = _rstd(ohg)
        hg = hg_ref[...]
        s = _sigmoid(hg)
        dhg_ref[...] = da * (ohg * r1 * ghg_ref[...]) * (s * (1.0 + hg * (1.0 - s)))
        dohg, gterm = _rms_bwd(ohg, r1, ghg_ref[...], da * (hg * s))
        dohg_ref[...] = dohg
        dghg_ref[...] += _colsum(gterm)
        osb = osb_ref[...]
        dosb, gterm = _rms_bwd(osb, _rstd(osb), gsb_ref[...], dn2)
        dosb_ref[...] = dosb
        dgsb_ref[...] += _colsum(gterm)

    return pl.pallas_call(
        body, grid=(T // tm,), name="out_bwd",
        in_specs=[_row_spec(tm, D_MODEL), _row_spec(tm, D_MODEL), _row_spec(tm, HG_WIDTH), _row_spec(tm, HG_WIDTH, 3),
                  _row_spec(tm, SB_WIDTH), _full_spec((1, HG_WIDTH)), _full_spec((1, SB_WIDTH)),
                  _full_spec((1, D_MODEL)), _full_spec((D_MODEL, D_MODEL)), pl.BlockSpec(memory_space=pl.ANY)],
        out_specs=[_row_spec(tm, D_MODEL), _row_spec(tm, HG_WIDTH), _row_spec(tm, HG_WIDTH), _row_spec(tm, SB_WIDTH),
                   _full_spec((1, D_MODEL)), _full_spec((1, HG_WIDTH)), _full_spec((1, SB_WIDTH))],
        out_shape=[jax.ShapeDtypeStruct((T, D_MODEL), BF16), jax.ShapeDtypeStruct((T, HG_WIDTH), F32),
                   jax.ShapeDtypeStruct((T, HG_WIDTH), F32), jax.ShapeDtypeStruct((T, SB_WIDTH), F32),
                   jax.ShapeDtypeStruct((1, D_MODEL), F32), jax.ShapeDtypeStruct((1, HG_WIDTH), F32),
                   jax.ShapeDtypeStruct((1, SB_WIDTH), F32)],
        compiler_params=_params(("arbitrary",)),
    )(dh1, mix, o_hg, proj, o_sb, g_hg, g_sb, g_post, w, after)


def _in_bwd(d_hgrn, d_hg, d_sq, d_sk, d_sv, x, dh1, g_pre, w_t):
    T = x.shape[0]
    tm = ROW_TILE

    def body(dh_ref, dhg_ref, dsq_ref, dsk_ref, dsv_ref, x_ref, dh1_ref, gpre_ref, w_ref, dproj_ref, dx_ref, dgpre_ref):
        i = pl.program_id(0)

        @pl.when(i == 0)
        def _():
            dgpre_ref[...] = jnp.zeros_like(dgpre_ref)

        dp = jnp.concatenate([dh_ref[...], dhg_ref[...], dsq_ref[...], dsk_ref[...], dsv_ref[...]], axis=1).astype(BF16)
        dproj_ref[...] = dp
        du = _dot(dp, w_ref[...])
        xv = x_ref[...]
        dx, gterm = _rms_bwd(xv, _rstd(xv), gpre_ref[...], du)
        dx_ref[...] = dh1_ref[...] + dx
        dgpre_ref[...] += _colsum(gterm)

    return pl.pallas_call(
        body, grid=(T // tm,), name="in_bwd",
        in_specs=[_row_spec(tm, 3 * HG_WIDTH), _row_spec(tm, HG_WIDTH), _row_spec(tm, SB_WIDTH), _row_spec(tm, SB_WIDTH),
                  _row_spec(tm, SB_WIDTH), _row_spec(tm, D_MODEL), _row_spec(tm, D_MODEL), _full_spec((1, D_MODEL)),
                  _full_spec((IN_COLS, D_MODEL))],
        out_specs=[_row_spec(tm, IN_COLS), _row_spec(tm, D_MODEL), _full_spec((1, D_MODEL))],
        out_shape=[jax.ShapeDtypeStruct((T, IN_COLS), BF16), jax.ShapeDtypeStruct((T, D_MODEL), F32),
                   jax.ShapeDtypeStruct((1, D_MODEL), F32)],
        compiler_params=_params(("arbitrary",)),
    )(d_hgrn, d_hg, d_sq, d_sk, d_sv, x, dh1, g_pre, w_t)


def _wgrad(a, b, name):
    stacked = a.ndim == 3
    S, T, K = a.shape if stacked else (1,) + a.shape
    N = b.shape[1]
    tk = K
    while tk * N * 4 > WGRAD_ACC_BYTES:
        tk //= 2
    assert K % tk == 0 and tk % 128 == 0
    tt = 512
    n_t = T // tt

    def body(a_ref, b_ref, o_ref, acc_ref):
        t = pl.program_id(2)

        @pl.when(t == 0)
        def _():
            acc_ref[...] = jnp.zeros_like(acc_ref)

        acc_ref[...] += _dot_tn(a_ref[...].astype(BF16), b_ref[...].astype(BF16))

        @pl.when(t == n_t - 1)
        def _():
            o_ref[...] = acc_ref[...].astype(BF16)

    if stacked:
        a_spec = pl.BlockSpec((None, tt, tk), lambda s, k, t: (s, t, k))
        o_spec = pl.BlockSpec((None, tk, N), lambda s, k, t: (s, k, 0))
        o_shape = (S, K, N)
    else:
        a_spec = pl.BlockSpec((tt, tk), lambda s, k, t: (t, k))
        o_spec = pl.BlockSpec((tk, N), lambda s, k, t: (k, 0))
        o_shape = (K, N)
    return pl.pallas_call(
        body, grid=(S, K // tk, n_t), name=name,
        in_specs=[a_spec, pl.BlockSpec((tt, N), lambda s, k, t: (t, 0))],
        out_specs=o_spec,
        out_shape=jax.ShapeDtypeStruct(o_shape, BF16),
        scratch_shapes=[pltpu.VMEM((tk, N), F32)],
        compiler_params=_params(("parallel", "parallel", "arbitrary")),
    )(a, b)


def _hg_gates(hq, hf, gam):
    g0 = gam[0:1, :]
    g1 = gam[1:2, :]
    mx = jnp.maximum(g0, g1)
    e0 = jnp.exp(g0 - mx)
    e1 = jnp.exp(g1 - mx)
    lb = e0 / (e0 + e1)
    s = _sigmoid(hf)
    f = lb + (1.0 - lb) * s
    sq = _sigmoid(hq)
    return hq * sq, sq, s, f, (1.0 - lb) * (1.0 - s), jnp.log(f), lb


def _prefix_in_sub(a, rowmod):
    n = a.shape[0]
    sh = 1
    while sh < HG_SUB:
        a = a + jnp.where(rowmod >= sh, pltpu.roll(a, sh, axis=0), 0.0)
        sh *= 2
    return a


def _suffix_in_sub(a, rowmod):
    n = a.shape[0]
    sh = 1
    while sh < HG_SUB:
        a = a + jnp.where(rowmod < HG_SUB - sh, pltpu.roll(a, n - sh, axis=0), 0.0)
        sh *= 2
    return a


def _hg_fwd(proj, gam):
    T = proj.shape[0]
    tb = HG_TILE
    n_sub = tb // HG_SUB

    def body(hq_ref, hf_ref, hi_ref, gam_ref, o_ref, hist_ref, st_ref, qt_ref, kk_ref, b_ref, od_ref):
        i = pl.program_id(0)

        @pl.when(i == 0)
        def _():
            st_ref[...] = jnp.zeros_like(st_ref)

        rowmod_w = lax.broadcasted_iota(jnp.int32, (tb, HG_WIDTH), 0) % HG_SUB
        rowmod = lax.broadcasted_iota(jnp.int32, (tb, 1), 0) % HG_SUB
        q, _, _, _, kk, lf, _ = _hg_gates(hq_ref[...], hf_ref[...], gam_ref[...])
        b = _prefix_in_sub(lf, rowmod_w)
        qt_ref[...] = (q * jnp.exp(b)).astype(BF16)
        kk_ref[...] = kk
        b_ref[...] = b

        for h in range(HG_HEADS):
            cs = slice(h * HG_DK, (h + 1) * HG_DK)
            qh = q[:, cs]
            kh = kk[:, cs]
            bh = b[:, cs]
            vh = hi_ref[:, cs]
            acc = jnp.sum(qh * kh, axis=1, keepdims=True) * vh
            for d in range(1, HG_SUB):
                e = qh * pltpu.roll(kh, d, axis=0) * jnp.exp(bh - pltpu.roll(bh, d, axis=0))
                sc = jnp.where(rowmod >= d, jnp.sum(e, axis=1, keepdims=True), 0.0)
                acc = acc + sc * pltpu.roll(vh, d, axis=0)
            od_ref[:, cs] = acc

        for m in range(n_sub):
            rs = slice(m * HG_SUB, (m + 1) * HG_SUB)
            for h in range(HG_HEADS):
                cs = slice(h * HG_DK, (h + 1) * HG_DK)
                st = st_ref[h]
                hist_ref[m, h] = st
                o_int = _dot_nt(qt_ref[rs, cs], st.astype(BF16))
                bs = b_ref[rs, cs]
                bl = bs[HG_SUB - 1:HG_SUB, :]
                kt = (kk_ref[rs, cs] * jnp.exp(bl - bs)).astype(BF16)
                st_ref[h] = st * jnp.exp(bl) + _dot_tn(hi_ref[rs, cs].astype(BF16), kt)
                o_ref[rs, cs] = o_int + od_ref[rs, cs]

    col = lambda c: pl.BlockSpec((tb, HG_WIDTH), lambda i, c=c: (i, c))
    return pl.pallas_call(
        body, grid=(T // tb,), name="hg_fwd",
        in_specs=[col(0), col(1), col(2), _full_spec((2, HG_WIDTH))],
        out_specs=[pl.BlockSpec((tb, HG_WIDTH), lambda i: (i, 0)),
                   pl.BlockSpec((n_sub, HG_HEADS, HG_DK, HG_DK), lambda i: (i, 0, 0, 0))],
        out_shape=[jax.ShapeDtypeStruct((T, HG_WIDTH), F32),
                   jax.ShapeDtypeStruct((T // HG_SUB, HG_HEADS, HG_DK, HG_DK), F32)],
        scratch_shapes=[pltpu.VMEM((HG_HEADS, HG_DK, HG_DK), F32), pltpu.VMEM((tb, HG_WIDTH), BF16),
                        pltpu.VMEM((tb, HG_WIDTH), F32), pltpu.VMEM((tb, HG_WIDTH), F32),
                        pltpu.VMEM((tb, HG_WIDTH), F32)],
        compiler_params=_params(("arbitrary",)),
    )(proj, proj, proj, gam)


def _hg_bwd(proj, gam, d_o, hist):
    T = proj.shape[0]
    tb = HG_TILE
    n_sub = tb // HG_SUB
    n_t = T // tb

    def body(hq_ref, hf_ref, hi_ref, gam_ref, do_ref, hist_ref, dout_ref, dlb_ref,
             dst_ref, q_ref, kk_ref, b_ref, dq_ref, dkk_ref, db_ref, dv_ref):
        i = pl.program_id(0)

        @pl.when(i == 0)
        def _():
            dst_ref[...] = jnp.zeros_like(dst_ref)
            dlb_ref[...] = jnp.zeros_like(dlb_ref)

        rowmod_w = lax.broadcasted_iota(jnp.int32, (tb, HG_WIDTH), 0) % HG_SUB
        rowmod = lax.broadcasted_iota(jnp.int32, (tb, 1), 0) % HG_SUB
        last_row = lax.broadcasted_iota(jnp.int32, (HG_SUB, 1), 0) == HG_SUB - 1
        hq = hq_ref[...]
        q, sq, s, f, kk, lf, lb = _hg_gates(hq, hf_ref[...], gam_ref[...])
        b = _prefix_in_sub(lf, rowmod_w)
        q_ref[...] = q
        kk_ref[...] = kk
        b_ref[...] = b

        for m in reversed(range(n_sub)):
            rs = slice(m * HG_SUB, (m + 1) * HG_SUB)
            for h in range(HG_HEADS):
                cs = slice(h * HG_DK, (h + 1) * HG_DK)
                st = hist_ref[m, h]
                dst = dst_ref[h]
                dstb = dst.astype(BF16)
                dos = do_ref[rs, cs].astype(BF16)
                vs = hi_ref[rs, cs].astype(BF16)
                bs = b_ref[rs, cs]
                bl = bs[HG_SUB - 1:HG_SUB, :]
                ebl = jnp.exp(bl)
                qt = q_ref[rs, cs] * jnp.exp(bs)
                decay = jnp.exp(bl - bs)
                kt = kk_ref[rs, cs] * decay
                dqt = _dot(dos, st.astype(BF16))
                dkt = _dot(vs, dstb)
                dv_ref[rs, cs] = _dot_nt(kt.astype(BF16), dstb)
                dst_ref[h] = dst * ebl + _dot_tn(dos, qt.astype(BF16))
                dq_ref[rs, cs] = dqt * jnp.exp(bs)
                dkk_ref[rs, cs] = dkt * decay
                dktk = dkt * kt
                dbl = _colsum(dktk) + _colsum(dst * st) * ebl
                db_ref[rs, cs] = dqt * qt - dktk + jnp.where(last_row, dbl, 0.0)

        for h in range(HG_HEADS):
            cs = slice(h * HG_DK, (h + 1) * HG_DK)
            qh = q[:, cs]
            kh = kk[:, cs]
            bh = b[:, cs]
            vh = hi_ref[:, cs]
            doh = do_ref[:, cs]
            dsc = jnp.sum(doh * vh, axis=1, keepdims=True)
            sc = jnp.sum(qh * kh, axis=1, keepdims=True)
            dqh = dsc * kh
            dkh = dsc * qh
            dvh = sc * doh
            dbh = jnp.zeros_like(qh)
            for d in range(1, HG_SUB):
                back = tb - d
                valid = rowmod >= d
                kd = pltpu.roll(kh, d, axis=0)
                ex = jnp.where(valid, jnp.exp(bh - pltpu.roll(bh, d, axis=0)), 0.0)
                pd = kd * ex
                sc = jnp.sum(qh * pd, axis=1, keepdims=True)
                dsc = jnp.where(valid, jnp.sum(doh * pltpu.roll(vh, d, axis=0), axis=1, keepdims=True), 0.0)
                dqh = dqh + dsc * pd
                w = dsc * qh * pd
                dkh = dkh + pltpu.roll(dsc * qh * ex, back, axis=0)
                dvh = dvh + pltpu.roll(sc * doh, back, axis=0)
                dbh = dbh + w - pltpu.roll(w, back, axis=0)
            dq_ref[:, cs] += dqh
            dkk_ref[:, cs] += dkh
            dv_ref[:, cs] += dvh
            db_ref[:, cs] += dbh

        dlf = _suffix_in_sub(db_ref[...], rowmod_w)
        df = dlf / f - dkk_ref[...]
        dout_ref[:, 0:HG_WIDTH] = dq_ref[...] * (sq * (1.0 + hq * (1.0 - sq)))
        dout_ref[:, HG_WIDTH:2 * HG_WIDTH] = df * (1.0 - lb) * (s * (1.0 - s))
        dout_ref[:, 2 * HG_WIDTH:3 * HG_WIDTH] = dv_ref[...]
        dlb_ref[...] += _colsum(df * (1.0 - s))

    col = lambda c: pl.BlockSpec((tb, HG_WIDTH), lambda i, c=c: (n_t - 1 - i, c))
    return pl.pallas_call(
        body, grid=(n_t,), name="hg_bwd",
        in_specs=[col(0), col(1), col(2), _full_spec((2, HG_WIDTH)),
                  pl.BlockSpec((tb, HG_WIDTH), lambda i: (n_t - 1 - i, 0)),
                  pl.BlockSpec((n_sub, HG_HEADS, HG_DK, HG_DK), lambda i: (n_t - 1 - i, 0, 0, 0))],
        out_specs=[pl.BlockSpec((tb, 3 * HG_WIDTH), lambda i: (n_t - 1 - i, 0)), _full_spec((1, HG_WIDTH))],
        out_shape=[jax.ShapeDtypeStruct((T, 3 * HG_WIDTH), F32), jax.ShapeDtypeStruct((1, HG_WIDTH), F32)],
        scratch_shapes=[pltpu.VMEM((HG_HEADS, HG_DK, HG_DK), F32)] + [pltpu.VMEM((tb, HG_WIDTH), F32)] * 7,
        compiler_params=_params(("arbitrary",)),
    )(proj, proj, proj, gam, d_o, hist)


def _sb_masks():
    row = lax.broadcasted_iota(jnp.int32, (SB_BLOCK, SB_BLOCK), 0)
    col = lax.broadcasted_iota(jnp.int32, (SB_BLOCK, SB_BLOCK), 1)
    suffix = (row >= col).astype(BF16)
    prefix = (row <= col).astype(BF16)
    query = lax.broadcasted_iota(jnp.int32, (SB_TQ, SB_BLOCK), 0)
    lane = lax.broadcasted_iota(jnp.int32, (SB_TQ, SB_BLOCK), 1)
    causal = [lane + r * SB_BLOCK < query for r in range(SB_TQ // SB_BLOCK)]
    return suffix, prefix, causal, lane


def _sum_right(a, suffix):
    return _dot(a.astype(BF16), suffix)


def _sb_block(qb, kj, suffix, causal, c):
    z = _dot_nt(qb, kj)
    sp = jnp.maximum(z, 0.0) + jnp.log(1.0 + jnp.exp(-jnp.abs(z)))
    if causal is not None:
        sp = jnp.where(causal, sp, 0.0)
    big_l = _sum_right(sp, suffix)
    a = jnp.exp(z - big_l - c)
    if causal is not None:
        a = jnp.where(causal, a, 0.0)
    return z, a, big_l


def _half_masked(pair, lane, scale=1.0):
    pair = pair.astype(F32) * scale
    return jnp.where(lane < SB_DH, pair, 0.0).astype(BF16), jnp.where(lane < SB_DH, 0.0, pair).astype(BF16)


def _sb_fwd(qkv):
    T = qkv.shape[0]
    width = SB_PP * SB_BLOCK
    n_chain = 2 * SB_PP
    n_sub = SB_TQ // SB_BLOCK

    def body(q_ref, k_ref, v_ref, o_ref, right_ref, acc_ref, c_ref, qm):
        i = pl.program_id(1)
        suffix, _, causal, lane = _sb_masks()
        for pp in range(SB_PP):
            qm[2 * pp], qm[2 * pp + 1] = _half_masked(q_ref[:, pp * SB_BLOCK:(pp + 1) * SB_BLOCK], lane, SB_DH ** -0.5)
        acc_ref[...] = jnp.zeros_like(acc_ref)
        c_ref[...] = jnp.zeros_like(c_ref)
        right_ref[...] = jnp.zeros_like(right_ref)

        def step(j, r=None):
            rows = pl.ds(pl.multiple_of(j * SB_BLOCK, SB_BLOCK), SB_BLOCK)
            qs = slice(0 if r is None else r * SB_BLOCK, SB_TQ)
            mask = None if r is None else causal[r][qs]
            for ch in range(n_chain):
                pair = slice((ch // 2) * SB_BLOCK, (ch // 2 + 1) * SB_BLOCK)
                c = c_ref[ch, qs]
                _, a, big_l = _sb_block(qm[ch, qs], k_ref[rows, pair], suffix, mask, c)
                acc_ref[ch, qs] += _dot(a.astype(BF16), v_ref[rows, pair])
                cols = slice(ch * SB_BLOCK, (ch + 1) * SB_BLOCK)
                right_ref[qs, cols] = jnp.where(lane[qs] == j, c, right_ref[qs, cols])
                c_ref[ch, qs] = c + jnp.broadcast_to(big_l[:, 0:1], c.shape)

        first = i * n_sub
        for r in reversed(range(n_sub)):
            step(first + r, r)

        @pl.loop(0, first)
        def _(t):
            step(first - 1 - t)

        for pp in range(SB_PP):
            o_ref[:, pp * SB_BLOCK:(pp + 1) * SB_BLOCK] = jnp.where(lane < SB_DH, acc_ref[2 * pp], acc_ref[2 * pp + 1])

    assert T // SB_BLOCK <= SB_BLOCK
    n_g = SB_WIDTH // width
    blk = lambda part: pl.BlockSpec((SB_TQ, width), lambda g, i, part=part: (i, part * n_g + g))
    whole = lambda part: pl.BlockSpec((T, width), lambda g, i, part=part: (0, part * n_g + g))
    return pl.pallas_call(
        body, grid=(n_g, T // SB_TQ), name="sb_fwd",
        in_specs=[blk(0), whole(1), whole(2)],
        out_specs=[blk(0), pl.BlockSpec((SB_TQ, n_chain * SB_BLOCK), lambda g, i: (i, g))],
        out_shape=[jax.ShapeDtypeStruct((T, SB_WIDTH), F32), jax.ShapeDtypeStruct((T, SB_HEADS * SB_BLOCK), F32)],
        scratch_shapes=[pltpu.VMEM((n_chain, SB_TQ, SB_BLOCK), F32), pltpu.VMEM((n_chain, SB_TQ, SB_BLOCK), F32),
                        pltpu.VMEM((n_chain, SB_TQ, SB_BLOCK), BF16)],
        compiler_params=_params(("parallel", "arbitrary")),
    )(qkv, qkv, qkv)


def _sb_bwd(qkv, right, d_o, after):
    T = qkv.shape[0]
    width = SB_PP * SB_BLOCK
    n_chain = 2 * SB_PP
    n_sub = SB_TQ // SB_BLOCK
    scale = SB_DH ** -0.5

    def body(q_ref, k_ref, v_ref, right_ref, do_ref, after_ref, dq_ref, dk_ref, dv_ref, acc_ref, gc_ref, qm, dom):
        i = pl.program_id(1)

        @pl.when(i == 0)
        def _():
            dk_ref[...] = jnp.zeros_like(dk_ref)
            dv_ref[...] = jnp.zeros_like(dv_ref)

        suffix, prefix, causal, lane = _sb_masks()
        for pp in range(SB_PP):
            pair = slice(pp * SB_BLOCK, (pp + 1) * SB_BLOCK)
            qm[2 * pp], qm[2 * pp + 1] = _half_masked(q_ref[:, pair], lane, scale)
            dom[2 * pp], dom[2 * pp + 1] = _half_masked(do_ref[:, pair], lane)
        acc_ref[...] = jnp.zeros_like(acc_ref)
        gc_ref[...] = jnp.zeros_like(gc_ref)

        def step(j, r=None):
            rows = pl.ds(pl.multiple_of(j * SB_BLOCK, SB_BLOCK), SB_BLOCK)
            qs = slice(0 if r is None else r * SB_BLOCK, SB_TQ)
            mask = None if r is None else causal[r][qs]
            for pp in range(SB_PP):
                pair = slice(pp * SB_BLOCK, (pp + 1) * SB_BLOCK)
                kj = k_ref[rows, pair]
                vj = v_ref[rows, pair]
                dk = jnp.zeros((SB_BLOCK, SB_BLOCK), F32)
                dv = jnp.zeros((SB_BLOCK, SB_BLOCK), F32)
                for ch in (2 * pp, 2 * pp + 1):
                    cols = slice(ch * SB_BLOCK, (ch + 1) * SB_BLOCK)
                    c = jnp.sum(jnp.where(lane[qs] == j, right_ref[qs, cols], 0.0), axis=1, keepdims=True)
                    z, a, _ = _sb_block(qm[ch, qs], kj, suffix, mask, c)
                    g = a * _dot_nt(dom[ch, qs], vj)
                    g_left = _sum_right(g, prefix)
                    gc = gc_ref[ch, qs]
                    dz = g - _sigmoid(z) * (gc + g_left)
                    if mask is not None:
                        dz = jnp.where(mask, dz, 0.0)
                    dzb = dz.astype(BF16)
                    dk = dk + _dot_tn(dzb, qm[ch, qs])
                    dv = dv + _dot_tn(a.astype(BF16), dom[ch, qs])
                    acc_ref[ch, qs] += _dot(dzb, kj)
                    gc_ref[ch, qs] = gc + jnp.broadcast_to(g_left[:, SB_BLOCK - 1:SB_BLOCK], gc.shape)
                dk_ref[rows, pair] += dk
                dv_ref[rows, pair] += dv

        first = i * n_sub

        @pl.loop(0, first)
        def _(j):
            step(j)

        for r in range(n_sub):
            step(first + r, r)
        for pp in range(SB_PP):
            dq_ref[:, pp * SB_BLOCK:(pp + 1) * SB_BLOCK] = scale * jnp.where(lane < SB_DH, acc_ref[2 * pp],
                                                                             acc_ref[2 * pp + 1])
    n_g = SB_WIDTH // width
    blk = lambda part: pl.BlockSpec((SB_TQ, width), lambda g, i, part=part: (i, part * n_g + g))
    whole = lambda part: pl.BlockSpec((T, width), lambda g, i, part=part: (0, part * n_g + g))
    return pl.pallas_call(
        body, grid=(n_g, T // SB_TQ), name="sb_bwd",
        in_specs=[blk(0), whole(1), whole(2), pl.BlockSpec((SB_TQ, n_chain * SB_BLOCK), lambda g, i: (i, g)), blk(0),
                  pl.BlockSpec(memory_space=pl.ANY)],
        out_specs=[blk(0), whole(0), whole(0)],
        out_shape=[jax.ShapeDtypeStruct((T, SB_WIDTH), F32)] * 3,
        scratch_shapes=[pltpu.VMEM((n_chain, SB_TQ, SB_BLOCK), F32), pltpu.VMEM((n_chain, SB_TQ, SB_BLOCK), F32),
                        pltpu.VMEM((n_chain, SB_TQ, SB_BLOCK), BF16), pltpu.VMEM((n_chain, SB_TQ, SB_BLOCK), BF16)],
        compiler_params=_params(("parallel", "arbitrary")),
    )(qkv, qkv, qkv, right, d_o, after)


HBM = pl.BlockSpec(memory_space=pltpu.HBM)
MESH = pl.DeviceIdType.MESH


def _place():
    return lax.axis_index("x"), lax.axis_index("y"), lax.axis_index("c")


def _all_gather(block):
    rows, cols = block.shape

    def body(x_ref, out_ref, send_sems, recv_sems, local_sem):
        x, y, c = _place()
        me, sibling = (x, y, c), (x, y, 1 - c)
        chips = [(1 - x, y), (x, 1 - y), (1 - x, 1 - y)]

        def slot(px, py, pc):
            return out_ref.at[4 * px + 2 * py + pc]

        def copy(k, blk, to, src=None):
            return pltpu.make_async_remote_copy(
                src_ref=slot(*blk) if src is None else src, dst_ref=slot(*blk),
                send_sem=send_sems.at[k], recv_sem=recv_sems.at[k], device_id=to, device_id_type=MESH)

        mine = pltpu.make_async_copy(x_ref, slot(*me), local_sem)
        mine.start()
        first = [copy(0, me, sibling, src=x_ref)]
        first += [copy(1 + j, me, (*chip, c), src=x_ref) for j, chip in enumerate(chips)]
        for cp in first:
            cp.start()
        passed = [copy(4 + j, (*chip, c), sibling) for j, chip in enumerate(chips)]
        for j, chip in enumerate(chips):
            copy(1 + j, (*chip, c), me).wait_recv()
            passed[j].start()
        copy(0, sibling, me).wait_recv()
        for j, chip in enumerate(chips):
            copy(4 + j, (*chip, 1 - c), me).wait_recv()
        for cp in first + passed:
            cp.wait_send()
        mine.wait()

    return pl.pallas_call(
        body, name="all_gather",
        out_shape=jax.ShapeDtypeStruct((N_DEV, rows, cols), block.dtype),
        in_specs=[HBM], out_specs=HBM,
        scratch_shapes=[pltpu.SemaphoreType.DMA((7,)), pltpu.SemaphoreType.DMA((7,)), pltpu.SemaphoreType.DMA],
    )(block)


SEM = pl.BlockSpec(memory_space=pltpu.SEMAPHORE)
ANY = pl.BlockSpec(memory_space=pl.ANY)
SPLIT_EFFECT = pltpu.SideEffectType.DATAFLOW_SIDE_EFFECTING
N_PEER = N_DEV - 1


def _flow_copies(kind, src_ref, land_ref, send_sems, recv_sems):
    x, y, c = _place()
    copies = []
    for r in range(1, N_DEV):
        px = 1 - x if r & 4 else x
        py = 1 - y if r & 2 else y
        pc = 1 - c if r & 1 else c
        if kind == "gather":
            src, dst = src_ref, land_ref.at[4 * x + 2 * y + c]
        else:
            src, dst = src_ref.at[4 * px + 2 * py + pc], land_ref.at[r - 1]
        copies.append(pltpu.make_async_remote_copy(
            src_ref=src, dst_ref=dst, send_sem=send_sems.at[r - 1], recv_sem=recv_sems.at[r - 1],
            device_id=(px, py, pc), device_id_type=MESH))
    return copies


def _landing(kind, src):
    if kind == "gather":
        return lax.empty((N_DEV,) + src.shape, src.dtype)
    return lax.empty((N_PEER,) + src.shape[1:], src.dtype)


def _push_start(kinds, arrays, after, name):
    n = len(arrays)

    def body(*refs):
        ins, sems, token = refs[:n], refs[n + 1:2 * n + 1], refs[3 * n + 1]
        for f, kind in enumerate(kinds):
            for cp in _flow_copies(kind, ins[2 * f], ins[2 * f + 1], sems[2 * f], sems[2 * f + 1]):
                cp.start()
        token[...] = jnp.zeros_like(token)

    outs = pl.pallas_call(
        body, name=name,
        out_shape=[pltpu.SemaphoreType.DMA((N_PEER,))] * n + [pltpu.HBM(a.shape, a.dtype) for a in arrays]
        + [jax.ShapeDtypeStruct((8, 128), F32)],
        in_specs=[HBM] * n + [ANY], out_specs=[SEM] * n + [HBM] * n + [pl.BlockSpec(memory_space=pltpu.VMEM)],
        input_output_aliases={i: n + i for i in range(n)},
        compiler_params=pltpu.CompilerParams(has_side_effects=SPLIT_EFFECT),
    )(*[pltpu.with_memory_space_constraint(a, pltpu.HBM) for a in arrays], after)
    return outs[:n], outs[n:2 * n], outs[2 * n]


def _push_wait(kinds, sems, arrays, after, name):
    n = len(arrays)

    def body(*refs):
        ins, sem_refs = refs[:n], refs[n:2 * n]
        for f, kind in enumerate(kinds):
            for cp in _flow_copies(kind, ins[2 * f], ins[2 * f + 1], sem_refs[2 * f], sem_refs[2 * f + 1]):
                cp.wait_send()
                cp.wait_recv()

    outs = pl.pallas_call(
        body, name=name,
        out_shape=[pltpu.HBM(a.shape, a.dtype) for a in arrays],
        in_specs=[HBM] * n + [SEM] * n + [ANY], out_specs=[HBM] * n,
        input_output_aliases={i: i for i in range(n)},
        compiler_params=pltpu.CompilerParams(has_side_effects=SPLIT_EFFECT),
    )(*arrays, *sems, after)
    return outs


def _sum_blocks(own, got, name):
    n, rows, cols = got.shape
    tr = next(t for t in (128, 64, 32) if rows % t == 0)

    def body(own_ref, got_ref, out_ref):
        acc = own_ref[...].astype(F32)
        for r in range(n):
            acc = acc + got_ref[r].astype(F32)
        out_ref[...] = acc

    return pl.pallas_call(
        body, grid=(rows // tr,), name=name,
        in_specs=[pl.BlockSpec((tr, cols), lambda i: (i, 0)), pl.BlockSpec((n, tr, cols), lambda i: (0, i, 0))],
        out_specs=pl.BlockSpec((tr, cols), lambda i: (i, 0)),
        out_shape=jax.ShapeDtypeStruct((rows, cols), F32),
        compiler_params=_params(("parallel",)),
    )(own, got)


def _adamw_math(w, g, m, v):
    m = ADAM_B1 * m + (1.0 - ADAM_B1) * g
    v = ADAM_B2 * v + (1.0 - ADAM_B2) * (g * g)
    m_hat = m / (1.0 - ADAM_B1 ** ADAM_STEP)
    v_hat = v / (1.0 - ADAM_B2 ** ADAM_STEP)
    delta = -ADAM_LR * (m_hat / (jnp.sqrt(v_hat) + ADAM_EPS) + ADAM_WD * w)
    return delta, m, v


def _adamw(w, g, m, v, name):
    rows, cols = w.shape
    tr = rows if rows <= 352 else 256

    def body(w_ref, g_ref, m_ref, v_ref, d_out, m_out, v_out):
        d_out[...], m_out[...], v_out[...] = _adamw_math(w_ref[...], g_ref[...], m_ref[...], v_ref[...])

    spec = pl.BlockSpec((tr, cols), lambda i: (i, 0))
    return pl.pallas_call(
        body, grid=(rows // tr,), name=name,
        in_specs=[spec] * 4, out_specs=[spec] * 3,
        out_shape=[jax.ShapeDtypeStruct((rows, cols), F32)] * 3,
        compiler_params=_params(("parallel",)),
    )(w, g, m, v)


ROW_ATTN_PRE, ROW_ATTN_POST, ROW_FFN_PRE, ROW_FFN_POST, ROW_OUT_NORMS, ROW_GAMMA, ROW_LOSS = range(7)


def _small_update(smalls, w, m, v):
    def body(s_ref, w_ref, m_ref, v_ref, g_out, d_out, m_out, v_out):
        g = s_ref[0]
        for j in range(1, N_DEV):
            g = g + s_ref[j]
        wv = w_ref[...]
        g0 = wv[ROW_GAMMA:ROW_GAMMA + 1, 0:HG_WIDTH]
        g1 = wv[ROW_GAMMA:ROW_GAMMA + 1, HG_WIDTH:]
        mx = jnp.maximum(g0, g1)
        e0 = jnp.exp(g0 - mx)
        e1 = jnp.exp(g1 - mx)
        lb = e0 / (e0 + e1)
        dg0 = g[ROW_GAMMA:ROW_GAMMA + 1, 0:HG_WIDTH] * lb * (1.0 - lb)
        row = lax.broadcasted_iota(jnp.int32, g.shape, 0)
        g = jnp.where(row == ROW_GAMMA, jnp.concatenate([dg0, -dg0], axis=1), g)
        g_out[...] = g
        d_out[...], m_out[...], v_out[...] = _adamw_math(wv, g, m_ref[...], v_ref[...])

    return pl.pallas_call(
        body, name="small_update",
        out_shape=[jax.ShapeDtypeStruct((8, D_MODEL), F32)] * 4,
    )(smalls, w, m, v)


def _local_step(x, p, target, g_attn_pre, w_in_t, gamma, g_hg, g_sb, g_attn_post, g_ffn_pre, g_ffn_post, later_weights,
                grads_ready):
    proj, u, qkv = _in_proj(x, g_attn_pre, w_in_t)
    o_hg, hist = _hg_fwd(proj, gamma)
    o_sb, right = _sb_fwd(qkv)
    w_out, w_gu_t, w_down, w_pp_t, w_pg = later_weights(o_sb)
    cat, mix, h1 = _out_proj(o_hg, proj, o_sb, x, g_hg, g_sb, g_attn_post, w_out)
    u2, gu, y, h2 = _ffn_fwd(h1, g_ffn_pre, w_gu_t, w_down, g_ffn_post)
    dh2, de, dg, loss = _ple_loss(h2, p, target, w_pp_t, w_pg)

    d_wpp_t = _wgrad(de, p, "wgrad_ple_proj")
    d_wpg = _wgrad(h2, dg, "wgrad_ple_gate")
    dy, a, dgu, dh1, dg_ffn_post, dg_ffn_pre = _ffn_bwd(dh2, y, h1, gu, g_ffn_post, g_ffn_pre, w_gu_t, w_down)
    d_wdown = _wgrad(a, dy, "wgrad_down")
    sent = grads_ready("ffn", (_wgrad(dgu, u2, "wgrad_gate_up"), d_wdown, d_wpp_t, d_wpg))
    dmix, d_ohg, d_hg, d_osb, dg_attn_post, dg_hg, dg_sb = _out_bwd(dh1, mix, o_hg, proj, o_sb, g_hg, g_sb, g_attn_post,
                                                                    w_out, sent)
    sent = grads_ready("out", (_wgrad(cat, dmix, "wgrad_out"),))
    dq, dk, dv = _sb_bwd(qkv, right, d_osb, sent)
    d_hgrn, d_lb = _hg_bwd(proj, gamma, d_ohg, hist)
    dproj, dx, dg_attn_pre = _in_bwd(d_hgrn, d_hg, dq, dk, dv, x, dh1, g_attn_pre, w_in_t)
    grads_ready("in", (_wgrad(dproj, u, "wgrad_in"),))
    return loss, dx, (dg_attn_pre, dg_attn_post, dg_ffn_pre, dg_ffn_post, dg_hg, dg_sb, d_lb)


def _pack_row(*parts):
    return jnp.concatenate([q.reshape(1, -1) for q in parts], axis=1)


def kernel(x, p, attn_pre_norm, w_in, hg_lower_gamma, hg_out_norm, sb_out_norm, w_out, attn_post_norm, ffn_pre_norm, w_gate_up, w_down, ffn_post_norm, ple_proj, ple_gate, loss_target, m_attn_pre_norm, m_w_in, m_hg_lower_gamma, m_hg_out_norm, m_sb_out_norm, m_w_out, m_attn_post_norm, m_ffn_pre_norm, m_w_gate_up, m_w_down, m_ffn_post_norm, m_ple_proj, m_ple_gate, v_attn_pre_norm, v_w_in, v_hg_lower_gamma, v_hg_out_norm, v_sb_out_norm, v_w_out, v_attn_post_norm, v_ffn_pre_norm, v_w_gate_up, v_w_down, v_ffn_post_norm, v_ple_proj, v_ple_gate):
    big = (w_in[0], w_out[0], w_gate_up[0], w_down[0], ple_proj[0], ple_gate[0])
    big_m = (m_w_in[0], m_w_out[0], m_w_gate_up[0], m_w_down[0], m_ple_proj[0], m_ple_gate[0])
    big_v = (v_w_in[0], v_w_out[0], v_w_gate_up[0], v_w_down[0], v_ple_proj[0], v_ple_gate[0])
    names = ("w_in", "w_out", "w_gate_up", "w_down", "ple_proj", "ple_gate")

    by_columns = (True, False, True, False, True, False)
    shards = [(w.T if t else w).astype(BF16) for w, t in zip(big, by_columns)]
    w_in_t = _all_gather(shards[0]).reshape(IN_COLS, D_MODEL)
    me = 4 * lax.axis_index("x") + 2 * lax.axis_index("y") + lax.axis_index("c")
    later = shards[1:]
    w_sems, w_thru, token = _push_start(
        ("gather",) * len(later), [a for s in later for a in (s, _landing("gather", s))], w_in_t, "weights_start")

    def later_weights(after):
        done = _push_wait(("gather",) * len(later), w_sems, w_thru, after, "weights_wait")
        whole = [lax.dynamic_update_index_in_dim(got, mine, me, 0) for mine, got in zip(done[0::2], done[1::2])]
        return (whole[0].reshape(D_MODEL, D_MODEL), whole[1].reshape(2, D_FF, D_MODEL), whole[2].reshape(D_FF, D_MODEL),
                whole[3].reshape(D_MODEL, PLE_DIM), whole[4].reshape(D_MODEL, D_MODEL))

    flights = {}

    def by_owner(g):
        return g.reshape(N_DEV, -1, g.shape[-1])

    def grads_ready(group, grads):
        blocks = [by_owner(g) for g in grads]
        if group == "in":
            flights[group] = blocks[0]
            return None
        sems, thru, sent = _push_start(
            ("scatter",) * len(blocks), [a for b in blocks for a in (b, _landing("scatter", b))], token,
            "grads_" + group + "_start")
        flights[group] = (sems, thru)
        return sent

    loss, dx, smalls = _local_step(
        x[0], p[0, 0], loss_target[0], attn_pre_norm + token[0:1, 0:1], w_in_t,
        hg_lower_gamma, hg_out_norm, sb_out_norm, attn_post_norm, ffn_pre_norm, ffn_post_norm, later_weights, grads_ready)
    dg_attn_pre, dg_attn_post, dg_ffn_pre, dg_ffn_post, dg_hg, dg_sb, d_lb = smalls

    zeros_half = jnp.zeros((1, HG_WIDTH), F32)
    small_pack = jnp.concatenate([
        dg_attn_pre, dg_attn_post, dg_ffn_pre, dg_ffn_post, _pack_row(dg_hg, dg_sb), _pack_row(d_lb, zeros_half),
        jnp.broadcast_to(loss[0:1, 0:1], (1, D_MODEL)), jnp.zeros((1, D_MODEL), F32)], axis=0)
    in_blocks = flights["in"]
    in_sems, in_thru, in_token = _push_start(
        ("scatter", "gather"),
        (in_blocks, _landing("scatter", in_blocks), small_pack, _landing("gather", small_pack)), token,
        "grads_in_start")
    ffn = _push_wait(("scatter",) * 4, *flights["ffn"], in_token, "grads_ffn_wait")
    sent_out, got_out = _push_wait(("scatter",), *flights["out"], ffn[1], "grads_out_wait")
    sent_in, got_in, my_small, all_smalls = _push_wait(("scatter", "gather"), in_sems, in_thru, got_out, "grads_in_wait")
    all_smalls = lax.dynamic_update_index_in_dim(all_smalls, my_small, me, 0)

    def reduced(sent, got, name):
        return _sum_blocks(lax.dynamic_index_in_dim(sent, me, 0, keepdims=False), got, "grad_sum_" + name)

    sums = (reduced(sent_in, got_in, "w_in"), reduced(sent_out, got_out, "w_out"), reduced(ffn[0], ffn[1], "w_gate_up"),
            reduced(ffn[2], ffn[3], "w_down"), reduced(ffn[4], ffn[5], "ple_proj"), reduced(ffn[6], ffn[7], "ple_gate"))

    out_g, out_d, out_m, out_v = {}, {}, {}, {}
    for i, name in enumerate(names):
        g = sums[i].T if by_columns[i] else sums[i]
        out_g[name] = g[None]
        d, m, v = _adamw(big[i], g, big_m[i], big_v[i], "adamw_" + name)
        out_d[name], out_m[name], out_v[name] = d[None], m[None], v[None]

    def small_rows(pre, gam, hg, sb, post, fpre, fpost):
        return jnp.concatenate([pre, post, fpre, fpost, _pack_row(hg, sb), _pack_row(gam[0], gam[1]),
                                jnp.zeros((2, D_MODEL), F32)], axis=0)

    packs = _small_update(
        all_smalls,
        small_rows(attn_pre_norm, hg_lower_gamma, hg_out_norm, sb_out_norm, attn_post_norm, ffn_pre_norm, ffn_post_norm),
        small_rows(m_attn_pre_norm, m_hg_lower_gamma, m_hg_out_norm, m_sb_out_norm, m_attn_post_norm, m_ffn_pre_norm,
                   m_ffn_post_norm),
        small_rows(v_attn_pre_norm, v_hg_lower_gamma, v_hg_out_norm, v_sb_out_norm, v_attn_post_norm, v_ffn_pre_norm,
                   v_ffn_post_norm))

    def unpack(pk):
        return {
            "attn_pre_norm": pk[ROW_ATTN_PRE:ROW_ATTN_PRE + 1],
            "hg_lower_gamma": pk[ROW_GAMMA].reshape(2, HG_WIDTH),
            "hg_out_norm": pk[ROW_OUT_NORMS:ROW_OUT_NORMS + 1, :HG_WIDTH],
            "sb_out_norm": pk[ROW_OUT_NORMS:ROW_OUT_NORMS + 1, HG_WIDTH:],
            "attn_post_norm": pk[ROW_ATTN_POST:ROW_ATTN_POST + 1],
            "ffn_pre_norm": pk[ROW_FFN_PRE:ROW_FFN_PRE + 1],
            "ffn_post_norm": pk[ROW_FFN_POST:ROW_FFN_POST + 1],
        }

    sg, sd, sm, sv = (unpack(pk) for pk in packs)
    out_g.update(sg), out_d.update(sd), out_m.update(sm), out_v.update(sv)
    order = ("attn_pre_norm", "w_in", "hg_lower_gamma", "hg_out_norm", "sb_out_norm", "w_out", "attn_post_norm",
             "ffn_pre_norm", "w_gate_up", "w_down", "ffn_post_norm", "ple_proj", "ple_gate")
    total_loss = packs[0][ROW_LOSS, 0]
    return (total_loss, dx[None], *[out_g[n] for n in order], *[out_d[n] for n in order],
            *[out_m[n] for n in order], *[out_v[n] for n in order])
```

```python
import functools

import jax
import jax.numpy as jnp
from jax import lax
from jax.experimental import pallas as pl
from jax.experimental.pallas import tpu as pltpu

F32 = jnp.float32
BF16 = jnp.bfloat16

D_MODEL = 1024
HG_WIDTH = 512
HG_HEADS = 4
HG_DK = 128
SB_WIDTH = 512
SB_HEADS = 8
SB_DH = 64
SB_BLOCK = 128
SB_PP = 2
SB_TQ = 512
D_FF = 2816
PLE_DIM = 256
IN_COLS = 4 * HG_WIDTH + 3 * SB_WIDTH
EPS = 1e-6
N_DEV = 8

HG_SUB = 16
HG_TILE = 128
FFN_TF = 256
ROW_TILE = 256
VMEM_LIMIT = 56 * 1024 * 1024
WGRAD_ACC_BYTES = 8 * 1024 * 1024

ADAM_LR = 0.001
ADAM_B1 = 0.9
ADAM_B2 = 0.999
ADAM_EPS = 1e-08
ADAM_WD = 0.01
ADAM_STEP = 10

NT_DIMS = (((1,), (1,)), ((), ()))
TN_DIMS = (((0,), (0,)), ((), ()))


def _params(semantics):
    return pltpu.CompilerParams(dimension_semantics=semantics, vmem_limit_bytes=VMEM_LIMIT)


def _dot(a, b):
    return jnp.dot(a, b, preferred_element_type=F32)


def _dot_nt(a, b):
    return lax.dot_general(a, b, NT_DIMS, preferred_element_type=F32)


def _dot_tn(a, b):
    return lax.dot_general(a, b, TN_DIMS, preferred_element_type=F32)


def _sigmoid(z):
    return 1.0 / (1.0 + jnp.exp(-z))


def _rstd(xv):
    return lax.rsqrt(jnp.mean(xv * xv, axis=-1, keepdims=True) + EPS)


def _rms_bwd(xv, r, g, dn):
    xn = xv * r
    gh = dn * g
    dx = r * (gh - xn * jnp.mean(gh * xn, axis=-1, keepdims=True))
    return dx, dn * xn


def _colsum(a):
    return jnp.sum(a, axis=0, keepdims=True)


def _row_spec(tm, width, col=0):
    return pl.BlockSpec((tm, width), lambda i, col=col: (i, col))


def _full_spec(shape):
    return pl.BlockSpec(shape, lambda i: (0,) * len(shape))


def _in_proj(x, g, w_t):
    T = x.shape[0]
    tm = ROW_TILE

    def body(x_ref, g_ref, w_ref, proj_ref, u_ref, qkv_ref):
        xv = x_ref[...]
        u = (xv * _rstd(xv) * g_ref[...]).astype(BF16)
        u_ref[...] = u
        proj = _dot_nt(u, w_ref[...])
        proj_ref[...] = proj
        qkv_ref[...] = proj[:, 4 * HG_WIDTH:].astype(BF16)

    return pl.pallas_call(
        body, grid=(T // tm,), name="in_proj",
        in_specs=[_row_spec(tm, D_MODEL), _full_spec((1, D_MODEL)), _full_spec((IN_COLS, D_MODEL))],
        out_specs=[_row_spec(tm, IN_COLS), _row_spec(tm, D_MODEL), _row_spec(tm, 3 * SB_WIDTH)],
        out_shape=[jax.ShapeDtypeStruct((T, IN_COLS), F32), jax.ShapeDtypeStruct((T, D_MODEL), BF16),
                   jax.ShapeDtypeStruct((T, 3 * SB_WIDTH), BF16)],
        compiler_params=_params(("parallel",)),
    )(x, g, w_t)


def _out_proj(o_hg, proj, o_sb, x, g_hg, g_sb, g_post, w):
    T = x.shape[0]
    tm = ROW_TILE

    def body(ohg_ref, hg_ref, osb_ref, x_ref, ghg_ref, gsb_ref, gpost_ref, w_ref, cat_ref, mix_ref, h1_ref):
        ohg = ohg_ref[...]
        hg = hg_ref[...]
        osb = osb_ref[...]
        a = ohg * _rstd(ohg) * ghg_ref[...] * (hg * _sigmoid(hg))
        n2 = osb * _rstd(osb) * gsb_ref[...]
        cat = jnp.concatenate([a, n2], axis=1).astype(BF16)
        cat_ref[...] = cat
        mix = _dot(cat, w_ref[...])
        mix_ref[...] = mix
        h1_ref[...] = x_ref[...] + mix * _rstd(mix) * gpost_ref[...]

    return pl.pallas_call(
        body, grid=(T // tm,), name="out_proj",
        in_specs=[_row_spec(tm, HG_WIDTH), _row_spec(tm, HG_WIDTH, 3), _row_spec(tm, SB_WIDTH), _row_spec(tm, D_MODEL),
                  _full_spec((1, HG_WIDTH)), _full_spec((1, SB_WIDTH)), _full_spec((1, D_MODEL)),
                  _full_spec((D_MODEL, D_MODEL))],
        out_specs=[_row_spec(tm, D_MODEL)] * 3,
        out_shape=[jax.ShapeDtypeStruct((T, D_MODEL), BF16), jax.ShapeDtypeStruct((T, D_MODEL), F32),
                   jax.ShapeDtypeStruct((T, D_MODEL), F32)],
        compiler_params=_params(("parallel",)),
    )(o_hg, proj, o_sb, x, g_hg, g_sb, g_post, w)


def _ffn_fwd(h1, g_pre, w_gu_t, w_down, g_post):
    T = h1.shape[0]
    tm = 2 * ROW_TILE
    n_f = D_FF // FFN_TF

    def body(h1_ref, gpre_ref, wgu_ref, wd_ref, gpost_ref, u2_ref, gu_ref, y_ref, h2_ref, acc_ref):
        j = pl.program_id(1)

        @pl.when(j == 0)
        def _():
            hv = h1_ref[...]
            u2_ref[...] = (hv * _rstd(hv) * gpre_ref[...]).astype(BF16)
            acc_ref[...] = jnp.zeros_like(acc_ref)

        u2 = u2_ref[...]
        gate = _dot_nt(u2, wgu_ref[0])
        up = _dot_nt(u2, wgu_ref[1])
        gu_ref[0] = gate
        gu_ref[1] = up
        a = (gate * _sigmoid(gate) * up).astype(BF16)
        acc_ref[...] += _dot(a, wd_ref[...])

        @pl.when(j == n_f - 1)
        def _():
            y = acc_ref[...]
            y_ref[...] = y
            h2_ref[...] = h1_ref[...] + y * _rstd(y) * gpost_ref[...]

    row = lambda width: pl.BlockSpec((tm, width), lambda i, j: (i, 0))
    vec = pl.BlockSpec((1, D_MODEL), lambda i, j: (0, 0))
    return pl.pallas_call(
        body, grid=(T // tm, n_f), name="ffn_fwd",
        in_specs=[row(D_MODEL), vec,
                  pl.BlockSpec((2, FFN_TF, D_MODEL), lambda i, j: (0, j, 0)),
                  pl.BlockSpec((FFN_TF, D_MODEL), lambda i, j: (j, 0)), vec],
        out_specs=[row(D_MODEL),
                   pl.BlockSpec((2, tm, FFN_TF), lambda i, j: (0, i, j)),
                   row(D_MODEL), row(D_MODEL)],
        out_shape=[jax.ShapeDtypeStruct((T, D_MODEL), BF16), jax.ShapeDtypeStruct((2, T, D_FF), F32),
                   jax.ShapeDtypeStruct((T, D_MODEL), F32), jax.ShapeDtypeStruct((T, D_MODEL), F32)],
        scratch_shapes=[pltpu.VMEM((tm, D_MODEL), F32)],
        compiler_params=_params(("parallel", "arbitrary")),
    )(h1, g_pre, w_gu_t, w_down, g_post)


def _ple_loss(h2, p, target, w_pp_t, w_pg):
    T = h2.shape[0]
    tm = ROW_TILE

    def body(h2_ref, p_ref, t_ref, wpp_ref, wpg_ref, dh2_ref, de_ref, dg_ref, loss_ref):
        i = pl.program_id(0)
        h2v = h2_ref[...]
        e = _dot_nt(p_ref[...].astype(BF16), wpp_ref[...])
        sg = _sigmoid(_dot(h2v.astype(BF16), wpg_ref[...]))
        diff = h2v + e * sg - t_ref[...]
        part = jnp.sum(jnp.sum(diff * diff, axis=1, keepdims=True), axis=0, keepdims=True) * (0.5 / D_MODEL)

        @pl.when(i == 0)
        def _():
            loss_ref[...] = jnp.zeros_like(loss_ref)

        loss_ref[...] += jnp.broadcast_to(part, loss_ref.shape)
        dh3 = diff * (1.0 / D_MODEL)
        de_ref[...] = (dh3 * sg).astype(BF16)
        dg = (dh3 * e * sg * (1.0 - sg)).astype(BF16)
        dg_ref[...] = dg
        dh2_ref[...] = dh3 + _dot_nt(dg, wpg_ref[...])

    return pl.pallas_call(
        body, grid=(T // tm,), name="ple_loss",
        in_specs=[_row_spec(tm, D_MODEL), _row_spec(tm, PLE_DIM), _row_spec(tm, D_MODEL),
                  _full_spec((D_MODEL, PLE_DIM)), _full_spec((D_MODEL, D_MODEL))],
        out_specs=[_row_spec(tm, D_MODEL)] * 3 + [_full_spec((8, 128))],
        out_shape=[jax.ShapeDtypeStruct((T, D_MODEL), F32), jax.ShapeDtypeStruct((T, D_MODEL), BF16),
                   jax.ShapeDtypeStruct((T, D_MODEL), BF16), jax.ShapeDtypeStruct((8, 128), F32)],
        compiler_params=_params(("arbitrary",)),
    )(h2, p, target, w_pp_t, w_pg)


def _ffn_bwd(dh2, y, h1, gu, g_post, g_pre, w_gu_t, w_down):
    T = h1.shape[0]
    tm = 2 * ROW_TILE
    n_f = D_FF // FFN_TF

    def body(dh2_ref, y_ref, h1_ref, gu_ref, gpost_ref, gpre_ref, wgu_ref, wd_ref,
             dy_ref, a_ref, dgu_ref, dh1_ref, dgpost_ref, dgpre_ref, du2_ref):
        i = pl.program_id(0)
        j = pl.program_id(1)

        @pl.when((i == 0) & (j == 0))
        def _():
            dgpost_ref[...] = jnp.zeros_like(dgpost_ref)
            dgpre_ref[...] = jnp.zeros_like(dgpre_ref)

        @pl.when(j == 0)
        def _():
            yv = y_ref[...]
            dy, gterm = _rms_bwd(yv, _rstd(yv), gpost_ref[...], dh2_ref[...])
            dy_ref[...] = dy.astype(BF16)
            dgpost_ref[...] += _colsum(gterm)
            du2_ref[...] = jnp.zeros_like(du2_ref)

        da = _dot_nt(dy_ref[...], wd_ref[...])
        gate = gu_ref[0]
        up = gu_ref[1]
        s = _sigmoid(gate)
        sil = gate * s
        a_ref[...] = (sil * up).astype(BF16)
        dgate = (da * up * (s * (1.0 + gate * (1.0 - s)))).astype(BF16)
        dup = (da * sil).astype(BF16)
        dgu_ref[0] = dgate
        dgu_ref[1] = dup
        du2_ref[...] += _dot(dgate, wgu_ref[0]) + _dot(dup, wgu_ref[1])

        @pl.when(j == n_f - 1)
        def _():
            hv = h1_ref[...]
            dx, gterm = _rms_bwd(hv, _rstd(hv), gpre_ref[...], du2_ref[...])
            dh1_ref[...] = dh2_ref[...] + dx
            dgpre_ref[...] += _colsum(gterm)

    row = lambda width: pl.BlockSpec((tm, width), lambda i, j: (i, 0))
    col = pl.BlockSpec((tm, FFN_TF), lambda i, j: (i, j))
    both = pl.BlockSpec((2, tm, FFN_TF), lambda i, j: (0, i, j))
    vec = pl.BlockSpec((1, D_MODEL), lambda i, j: (0, 0))
    return pl.pallas_call(
        body, grid=(T // tm, n_f), name="ffn_bwd",
        in_specs=[row(D_MODEL), row(D_MODEL), row(D_MODEL), both, vec, vec,
                  pl.BlockSpec((2, FFN_TF, D_MODEL), lambda i, j: (0, j, 0)),
                  pl.BlockSpec((FFN_TF, D_MODEL), lambda i, j: (j, 0))],
        out_specs=[row(D_MODEL), col, both, row(D_MODEL), vec, vec],
        out_shape=[jax.ShapeDtypeStruct((T, D_MODEL), BF16), jax.ShapeDtypeStruct((T, D_FF), BF16),
                   jax.ShapeDtypeStruct((2, T, D_FF), BF16),
                   jax.ShapeDtypeStruct((T, D_MODEL), F32), jax.ShapeDtypeStruct((1, D_MODEL), F32),
                   jax.ShapeDtypeStruct((1, D_MODEL), F32)],
        scratch_shapes=[pltpu.VMEM((tm, D_MODEL), F32)],
        compiler_params=_params(("arbitrary", "arbitrary")),
    )(dh2, y, h1, gu, g_post, g_pre, w_gu_t, w_down)


def _out_bwd(dh1, mix, o_hg, proj, o_sb, g_hg, g_sb, g_post, w, after):
    T = dh1.shape[0]
    tm = ROW_TILE

    def body(dh1_ref, mix_ref, ohg_ref, hg_ref, osb_ref, ghg_ref, gsb_ref, gpost_ref, w_ref, after_ref,
             dmix_ref, dohg_ref, dhg_ref, dosb_ref, dgpost_ref, dghg_ref, dgsb_ref):
        i = pl.program_id(0)

        @pl.when(i == 0)
        def _():
            dgpost_ref[...] = jnp.zeros_like(dgpost_ref)
            dghg_ref[...] = jnp.zeros_like(dghg_ref)
            dgsb_ref[...] = jnp.zeros_like(dgsb_ref)

        mix = mix_ref[...]
        dmix, gterm = _rms_bwd(mix, _rstd(mix), gpost_ref[...], dh1_ref[...])
        dgpost_ref[...] += _colsum(gterm)
        dmix = dmix.astype(BF16)
        dmix_ref[...] = dmix
        dcat = _dot_nt(dmix, w_ref[...])
        da = dcat[:, :HG_WIDTH]
        dn2 = dcat[:, HG_WIDTH:]
        ohg = ohg_ref[...]
        r1 = _rstd(ohg)
        hg = hg_ref[...]
        s = _sigmoid(hg)
        dhg_ref[...] = da * (ohg * r1 * ghg_ref[...]) * (s * (1.0 + hg * (1.0 - s)))
        dohg, gterm = _rms_bwd(ohg, r1, ghg_ref[...], da * (hg * s))
        dohg_ref[...] = dohg
        dghg_ref[...] += _colsum(gterm)
        osb = osb_ref[...]
        dosb, gterm = _rms_bwd(osb, _rstd(osb), gsb_ref[...], dn2)
        dosb_ref[...] = dosb
        dgsb_ref[...] += _colsum(gterm)

    return pl.pallas_call(
        body, grid=(T // tm,), name="out_bwd",
        in_specs=[_row_spec(tm, D_MODEL), _row_spec(tm, D_MODEL), _row_spec(tm, HG_WIDTH), _row_spec(tm, HG_WIDTH, 3),
                  _row_spec(tm, SB_WIDTH), _full_spec((1, HG_WIDTH)), _full_spec((1, SB_WIDTH)),
                  _full_spec((1, D_MODEL)), _full_spec((D_MODEL, D_MODEL)), pl.BlockSpec(memory_space=pl.ANY)],
        out_specs=[_row_spec(tm, D_MODEL), _row_spec(tm, HG_WIDTH), _row_spec(tm, HG_WIDTH), _row_spec(tm, SB_WIDTH),
                   _full_spec((1, D_MODEL)), _full_spec((1, HG_WIDTH)), _full_spec((1, SB_WIDTH))],
        out_shape=[jax.ShapeDtypeStruct((T, D_MODEL), BF16), jax.ShapeDtypeStruct((T, HG_WIDTH), F32),
                   jax.ShapeDtypeStruct((T, HG_WIDTH), F32), jax.ShapeDtypeStruct((T, SB_WIDTH), F32),
                   jax.ShapeDtypeStruct((1, D_MODEL), F32), jax.ShapeDtypeStruct((1, HG_WIDTH), F32),
                   jax.ShapeDtypeStruct((1, SB_WIDTH), F32)],
        compiler_params=_params(("arbitrary",)),
    )(dh1, mix, o_hg, proj, o_sb, g_hg, g_sb, g_post, w, after)


def _in_bwd(d_hgrn, d_hg, d_sq, d_sk, d_sv, x, dh1, g_pre, w_t):
    T = x.shape[0]
    tm = ROW_TILE

    def body(dh_ref, dhg_ref, dsq_ref, dsk_ref, dsv_ref, x_ref, dh1_ref, gpre_ref, w_ref, dproj_ref, dx_ref, dgpre_ref):
        i = pl.program_id(0)

        @pl.when(i == 0)
        def _():
            dgpre_ref[...] = jnp.zeros_like(dgpre_ref)

        dp = jnp.concatenate([dh_ref[...], dhg_ref[...], dsq_ref[...], dsk_ref[...], dsv_ref[...]], axis=1).astype(BF16)
        dproj_ref[...] = dp
        du = _dot(dp, w_ref[...])
        xv = x_ref[...]
        dx, gterm = _rms_bwd(xv, _rstd(xv), gpre_ref[...], du)
        dx_ref[...] = dh1_ref[...] + dx
        dgpre_ref[...] += _colsum(gterm)

    return pl.pallas_call(
        body, grid=(T // tm,), name="in_bwd",
        in_specs=[_row_spec(tm, 3 * HG_WIDTH), _row_spec(tm, HG_WIDTH), _row_spec(tm, SB_WIDTH), _row_spec(tm, SB_WIDTH),
                  _row_spec(tm, SB_WIDTH), _row_spec(tm, D_MODEL), _row_spec(tm, D_MODEL), _full_spec((1, D_MODEL)),
                  _full_spec((IN_COLS, D_MODEL))],
        out_specs=[_row_spec(tm, IN_COLS), _row_spec(tm, D_MODEL), _full_spec((1, D_MODEL))],
        out_shape=[jax.ShapeDtypeStruct((T, IN_COLS), BF16), jax.ShapeDtypeStruct((T, D_MODEL), F32),
                   jax.ShapeDtypeStruct((1, D_MODEL), F32)],
        compiler_params=_params(("arbitrary",)),
    )(d_hgrn, d_hg, d_sq, d_sk, d_sv, x, dh1, g_pre, w_t)


def _wgrad(a, b, name):
    stacked = a.ndim == 3
    S, T, K = a.shape if stacked else (1,) + a.shape
    N = b.shape[1]
    tk = K
    while tk * N * 4 > WGRAD_ACC_BYTES:
        tk //= 2
    assert K % tk == 0 and tk % 128 == 0
    tt = 512
    n_t = T // tt

    def body(a_ref, b_ref, o_ref, acc_ref):
        t = pl.program_id(2)

        @pl.when(t == 0)
        def _():
            acc_ref[...] = jnp.zeros_like(acc_ref)

        acc_ref[...] += _dot_tn(a_ref[...].astype(BF16), b_ref[...].astype(BF16))

        @pl.when(t == n_t - 1)
        def _():
            o_ref[...] = acc_ref[...].astype(BF16)

    if stacked:
        a_spec = pl.BlockSpec((None, tt, tk), lambda s, k, t: (s, t, k))
        o_spec = pl.BlockSpec((None, tk, N), lambda s, k, t: (s, k, 0))
        o_shape = (S, K, N)
    else:
        a_spec = pl.BlockSpec((tt, tk), lambda s, k, t: (t, k))
        o_spec = pl.BlockSpec((tk, N), lambda s, k, t: (k, 0))
        o_shape = (K, N)
    return pl.pallas_call(
        body, grid=(S, K // tk, n_t), name=name,
        in_specs=[a_spec, pl.BlockSpec((tt, N), lambda s, k, t: (t, 0))],
        out_specs=o_spec,
        out_shape=jax.ShapeDtypeStruct(o_shape, BF16),
        scratch_shapes=[pltpu.VMEM((tk, N), F32)],
        compiler_params=_params(("parallel", "parallel", "arbitrary")),
    )(a, b)


def _hg_gates(hq, hf, gam):
    g0 = gam[0:1, :]
    g1 = gam[1:2, :]
    mx = jnp.maximum(g0, g1)
    e0 = jnp.exp(g0 - mx)
    e1 = jnp.exp(g1 - mx)
    lb = e0 / (e0 + e1)
    s = _sigmoid(hf)
    f = lb + (1.0 - lb) * s
    sq = _sigmoid(hq)
    return hq * sq, sq, s, f, (1.0 - lb) * (1.0 - s), jnp.log(f), lb


def _prefix_in_sub(a, rowmod):
    n = a.shape[0]
    sh = 1
    while sh < HG_SUB:
        a = a + jnp.where(rowmod >= sh, pltpu.roll(a, sh, axis=0), 0.0)
        sh *= 2
    return a


def _suffix_in_sub(a, rowmod):
    n = a.shape[0]
    sh = 1
    while sh < HG_SUB:
        a = a + jnp.where(rowmod < HG_SUB - sh, pltpu.roll(a, n - sh, axis=0), 0.0)
        sh *= 2
    return a


def _hg_fwd(proj, gam):
    T = proj.shape[0]
    tb = HG_TILE
    n_sub = tb // HG_SUB

    def body(hq_ref, hf_ref, hi_ref, gam_ref, o_ref, hist_ref, st_ref, qt_ref, kk_ref, b_ref, od_ref):
        i = pl.program_id(0)

        @pl.when(i == 0)
        def _():
            st_ref[...] = jnp.zeros_like(st_ref)

        rowmod_w = lax.broadcasted_iota(jnp.int32, (tb, HG_WIDTH), 0) % HG_SUB
        rowmod = lax.broadcasted_iota(jnp.int32, (tb, 1), 0) % HG_SUB
        q, _, _, _, kk, lf, _ = _hg_gates(hq_ref[...], hf_ref[...], gam_ref[...])
        b = _prefix_in_sub(lf, rowmod_w)
        qt_ref[...] = (q * jnp.exp(b)).astype(BF16)
        kk_ref[...] = kk
        b_ref[...] = b

        for h in range(HG_HEADS):
            cs = slice(h * HG_DK, (h + 1) * HG_DK)
            qh = q[:, cs]
            kh = kk[:, cs]
            bh = b[:, cs]
            vh = hi_ref[:, cs]
            acc = jnp.sum(qh * kh, axis=1, keepdims=True) * vh
            for d in range(1, HG_SUB):
                e = qh * pltpu.roll(kh, d, axis=0) * jnp.exp(bh - pltpu.roll(bh, d, axis=0))
                sc = jnp.where(rowmod >= d, jnp.sum(e, axis=1, keepdims=True), 0.0)
                acc = acc + sc * pltpu.roll(vh, d, axis=0)
            od_ref[:, cs] = acc

        for m in range(n_sub):
            rs = slice(m * HG_SUB, (m + 1) * HG_SUB)
            for h in range(HG_HEADS):
                cs = slice(h * HG_DK, (h + 1) * HG_DK)
                st = st_ref[h]
                hist_ref[m, h] = st
                o_int = _dot_nt(qt_ref[rs, cs], st.astype(BF16))
                bs = b_ref[rs, cs]
                bl = bs[HG_SUB - 1:HG_SUB, :]
                kt = (kk_ref[rs, cs] * jnp.exp(bl - bs)).astype(BF16)
                st_ref[h] = st * jnp.exp(bl) + _dot_tn(hi_ref[rs, cs].astype(BF16), kt)
                o_ref[rs, cs] = o_int + od_ref[rs, cs]

    col = lambda c: pl.BlockSpec((tb, HG_WIDTH), lambda i, c=c: (i, c))
    return pl.pallas_call(
        body, grid=(T // tb,), name="hg_fwd",
        in_specs=[col(0), col(1), col(2), _full_spec((2, HG_WIDTH))],
        out_specs=[pl.BlockSpec((tb, HG_WIDTH), lambda i: (i, 0)),
                   pl.BlockSpec((n_sub, HG_HEADS, HG_DK, HG_DK), lambda i: (i, 0, 0, 0))],
        out_shape=[jax.ShapeDtypeStruct((T, HG_WIDTH), F32),
                   jax.ShapeDtypeStruct((T // HG_SUB, HG_HEADS, HG_DK, HG_DK), F32)],
        scratch_shapes=[pltpu.VMEM((HG_HEADS, HG_DK, HG_DK), F32), pltpu.VMEM((tb, HG_WIDTH), BF16),
                        pltpu.VMEM((tb, HG_WIDTH), F32), pltpu.VMEM((tb, HG_WIDTH), F32),
                        pltpu.VMEM((tb, HG_WIDTH), F32)],
        compiler_params=_params(("arbitrary",)),
    )(proj, proj, proj, gam)


def _hg_bwd(proj, gam, d_o, hist):
    T = proj.shape[0]
    tb = HG_TILE
    n_sub = tb // HG_SUB
    n_t = T // tb

    def body(hq_ref, hf_ref, hi_ref, gam_ref, do_ref, hist_ref, dout_ref, dlb_ref,
             dst_ref, q_ref, kk_ref, b_ref, dq_ref, dkk_ref, db_ref, dv_ref):
        i = pl.program_id(0)

        @pl.when(i == 0)
        def _():
            dst_ref[...] = jnp.zeros_like(dst_ref)
            dlb_ref[...] = jnp.zeros_like(dlb_ref)

        rowmod_w = lax.broadcasted_iota(jnp.int32, (tb, HG_WIDTH), 0) % HG_SUB
        rowmod = lax.broadcasted_iota(jnp.int32, (tb, 1), 0) % HG_SUB
        last_row = lax.broadcasted_iota(jnp.int32, (HG_SUB, 1), 0) == HG_SUB - 1
        hq = hq_ref[...]
        q, sq, s, f, kk, lf, lb = _hg_gates(hq, hf_ref[...], gam_ref[...])
        b = _prefix_in_sub(lf, rowmod_w)
        q_ref[...] = q
        kk_ref[...] = kk
        b_ref[...] = b

        for m in reversed(range(n_sub)):
            rs = slice(m * HG_SUB, (m + 1) * HG_SUB)
            for h in range(HG_HEADS):
                cs = slice(h * HG_DK, (h + 1) * HG_DK)
                st = hist_ref[m, h]
                dst = dst_ref[h]
                dstb = dst.astype(BF16)
                dos = do_ref[rs, cs].astype(BF16)
                vs = hi_ref[rs, cs].astype(BF16)
                bs = b_ref[rs, cs]
                bl = bs[HG_SUB - 1:HG_SUB, :]
                ebl = jnp.exp(bl)
                qt = q_ref[rs, cs] * jnp.exp(bs)
                decay = jnp.exp(bl - bs)
                kt = kk_ref[rs, cs] * decay
                dqt = _dot(dos, st.astype(BF16))
                dkt = _dot(vs, dstb)
                dv_ref[rs, cs] = _dot_nt(kt.astype(BF16), dstb)
                dst_ref[h] = dst * ebl + _dot_tn(dos, qt.astype(BF16))
                dq_ref[rs, cs] = dqt * jnp.exp(bs)
                dkk_ref[rs, cs] = dkt * decay
                dktk = dkt * kt
                dbl = _colsum(dktk) + _colsum(dst * st) * ebl
                db_ref[rs, cs] = dqt * qt - dktk + jnp.where(last_row, dbl, 0.0)

        for h in range(HG_HEADS):
            cs = slice(h * HG_DK, (h + 1) * HG_DK)
            qh = q[:, cs]
            kh = kk[:, cs]
            bh = b[:, cs]
            vh = hi_ref[:, cs]
            doh = do_ref[:, cs]
            dsc = jnp.sum(doh * vh, axis=1, keepdims=True)
            sc = jnp.sum(qh * kh, axis=1, keepdims=True)
            dqh = dsc * kh
            dkh = dsc * qh
            dvh = sc * doh
            dbh = jnp.zeros_like(qh)
            for d in range(1, HG_SUB):
                back = tb - d
                valid = rowmod >= d
                kd = pltpu.roll(kh, d, axis=0)
                ex = jnp.where(valid, jnp.exp(bh - pltpu.roll(bh, d, axis=0)), 0.0)
                pd = kd * ex
                sc = jnp.sum(qh * pd, axis=1, keepdims=True)
                dsc = jnp.where(valid, jnp.sum(doh * pltpu.roll(vh, d, axis=0), axis=1, keepdims=True), 0.0)
                dqh = dqh + dsc * pd
                w = dsc * qh * pd
                dkh = dkh + pltpu.roll(dsc * qh * ex, back, axis=0)
                dvh = dvh + pltpu.roll(sc * doh, back, axis=0)
                dbh = dbh + w - pltpu.roll(w, back, axis=0)
            dq_ref[:, cs] += dqh
            dkk_ref[:, cs] += dkh
            dv_ref[:, cs] += dvh
            db_ref[:, cs] += dbh

        dlf = _suffix_in_sub(db_ref[...], rowmod_w)
        df = dlf / f - dkk_ref[...]
        dout_ref[:, 0:HG_WIDTH] = dq_ref[...] * (sq * (1.0 + hq * (1.0 - sq)))
        dout_ref[:, HG_WIDTH:2 * HG_WIDTH] = df * (1.0 - lb) * (s * (1.0 - s))
        dout_ref[:, 2 * HG_WIDTH:3 * HG_WIDTH] = dv_ref[...]
        dlb_ref[...] += _colsum(df * (1.0 - s))

    col = lambda c: pl.BlockSpec((tb, HG_WIDTH), lambda i, c=c: (n_t - 1 - i, c))
    return pl.pallas_call(
        body, grid=(n_t,), name="hg_bwd",
        in_specs=[col(0), col(1), col(2), _full_spec((2, HG_WIDTH)),
                  pl.BlockSpec((tb, HG_WIDTH), lambda i: (n_t - 1 - i, 0)),
                  pl.BlockSpec((n_sub, HG_HEADS, HG_DK, HG_DK), lambda i: (n_t - 1 - i, 0, 0, 0))],
        out_specs=[pl.BlockSpec((tb, 3 * HG_WIDTH), lambda i: (n_t - 1 - i, 0)), _full_spec((1, HG_WIDTH))],
        out_shape=[jax.ShapeDtypeStruct((T, 3 * HG_WIDTH), F32), jax.ShapeDtypeStruct((1, HG_WIDTH), F32)],
        scratch_shapes=[pltpu.VMEM((HG_HEADS, HG_DK, HG_DK), F32)] + [pltpu.VMEM((tb, HG_WIDTH), F32)] * 7,
        compiler_params=_params(("arbitrary",)),
    )(proj, proj, proj, gam, d_o, hist)


def _sb_masks():
    row = lax.broadcasted_iota(jnp.int32, (SB_BLOCK, SB_BLOCK), 0)
    col = lax.broadcasted_iota(jnp.int32, (SB_BLOCK, SB_BLOCK), 1)
    suffix = (row >= col).astype(BF16)
    prefix = (row <= col).astype(BF16)
    query = lax.broadcasted_iota(jnp.int32, (SB_TQ, SB_BLOCK), 0)
    lane = lax.broadcasted_iota(jnp.int32, (SB_TQ, SB_BLOCK), 1)
    causal = [lane + r * SB_BLOCK < query for r in range(SB_TQ // SB_BLOCK)]
    return suffix, prefix, causal, lane


def _sum_right(a, suffix):
    return _dot(a.astype(BF16), suffix)


def _sb_block(qb, kj, suffix, causal, c):
    z = _dot_nt(qb, kj)
    sp = jnp.maximum(z, 0.0) + jnp.log(1.0 + jnp.exp(-jnp.abs(z)))
    sig = jnp.exp(z - sp)
    if causal is not None:
        sp = jnp.where(causal, sp, 0.0)
    big_l = _sum_right(sp, suffix)
    a = jnp.exp(z - big_l - c)
    if causal is not None:
        a = jnp.where(causal, a, 0.0)
    return sig, a, big_l


def _sb_tile_copies(to_hbm, bufs, stores, sems, slot, head, i, j, n_chain):
    copies = []
    for ch in range(n_chain):
        for w in range(2):
            vmem, hbm = bufs[w].at[slot, ch], stores[w].at[head + ch, i, j]
            src, dst = (vmem, hbm) if to_hbm else (hbm, vmem)
            copies.append(pltpu.make_async_copy(src, dst, sems.at[slot, ch, w]))
    return copies


def _half_masked(pair, lane, scale=1.0):
    pair = pair.astype(F32) * scale
    return jnp.where(lane < SB_DH, pair, 0.0).astype(BF16), jnp.where(lane < SB_DH, 0.0, pair).astype(BF16)


def _sb_fwd(qkv):
    T = qkv.shape[0]
    width = SB_PP * SB_BLOCK
    n_chain = 2 * SB_PP
    n_sub = SB_TQ // SB_BLOCK

    def body(q_ref, k_ref, v_ref, o_ref, a_st, s_st, acc_ref, c_ref, qm, a_buf, s_buf, sems):
        g = pl.program_id(0)
        i = pl.program_id(1)
        suffix, _, causal, lane = _sb_masks()
        for pp in range(SB_PP):
            qm[2 * pp], qm[2 * pp + 1] = _half_masked(q_ref[:, pp * SB_BLOCK:(pp + 1) * SB_BLOCK], lane, SB_DH ** -0.5)
        acc_ref[...] = jnp.zeros_like(acc_ref)
        c_ref[...] = jnp.zeros_like(c_ref)
        a_buf[...] = jnp.zeros_like(a_buf)
        s_buf[...] = jnp.zeros_like(s_buf)

        def copies(slot, j):
            return _sb_tile_copies(True, (a_buf, s_buf), (a_st, s_st), sems, slot, g * n_chain, i, j, n_chain)

        def step(j, slot, r=None, reuse=True):
            rows = pl.ds(pl.multiple_of(j * SB_BLOCK, SB_BLOCK), SB_BLOCK)
            qs = slice(0 if r is None else r * SB_BLOCK, SB_TQ)
            mask = None if r is None else causal[r][qs]
            if reuse:
                for cp in copies(slot, j):
                    cp.wait()
            for ch in range(n_chain):
                pair = slice((ch // 2) * SB_BLOCK, (ch // 2 + 1) * SB_BLOCK)
                c = c_ref[ch, qs]
                sig, a, big_l = _sb_block(qm[ch, qs], k_ref[rows, pair], suffix, mask, c)
                ab = a.astype(BF16)
                acc_ref[ch, qs] += _dot(ab, v_ref[rows, pair])
                a_buf[slot, ch, qs] = ab
                s_buf[slot, ch, qs] = sig.astype(BF16)
                c_ref[ch, qs] = c + jnp.broadcast_to(big_l[:, 0:1], c.shape)
            for cp in copies(slot, j):
                cp.start()

        first = i * n_sub
        for s, r in enumerate(reversed(range(n_sub))):
            step(first + r, s % 2, r, reuse=s >= 2)

        @pl.loop(0, first)
        def _(t):
            step(first - 1 - t, (n_sub + t) % 2)

        for slot in range(2):
            for cp in copies(slot, 0):
                cp.wait()
        for pp in range(SB_PP):
            o_ref[:, pp * SB_BLOCK:(pp + 1) * SB_BLOCK] = jnp.where(lane < SB_DH, acc_ref[2 * pp], acc_ref[2 * pp + 1])

    assert n_sub >= 2 and n_sub % 2 == 0
    n_g = SB_WIDTH // width
    blk = lambda part: pl.BlockSpec((SB_TQ, width), lambda g, i, part=part: (i, part * n_g + g))
    whole = lambda part: pl.BlockSpec((T, width), lambda g, i, part=part: (0, part * n_g + g))
    tiles = jax.ShapeDtypeStruct((SB_HEADS, T // SB_TQ, T // SB_BLOCK, SB_TQ, SB_BLOCK), BF16)
    return pl.pallas_call(
        body, grid=(n_g, T // SB_TQ), name="sb_fwd",
        in_specs=[blk(0), whole(1), whole(2)],
        out_specs=[blk(0), pl.BlockSpec(memory_space=pl.ANY), pl.BlockSpec(memory_space=pl.ANY)],
        out_shape=[jax.ShapeDtypeStruct((T, SB_WIDTH), F32), tiles, tiles],
        scratch_shapes=[pltpu.VMEM((n_chain, SB_TQ, SB_BLOCK), F32), pltpu.VMEM((n_chain, SB_TQ, SB_BLOCK), F32),
                        pltpu.VMEM((n_chain, SB_TQ, SB_BLOCK), BF16),
                        pltpu.VMEM((2, n_chain, SB_TQ, SB_BLOCK), BF16), pltpu.VMEM((2, n_chain, SB_TQ, SB_BLOCK), BF16),
                        pltpu.SemaphoreType.DMA((2, n_chain, 2))],
        compiler_params=_params(("parallel", "arbitrary")),
    )(qkv, qkv, qkv)


def _sb_bwd(qkv, a_st, s_st, d_o, after):
    T = qkv.shape[0]
    width = SB_PP * SB_BLOCK
    n_chain = 2 * SB_PP
    n_sub = SB_TQ // SB_BLOCK
    scale = SB_DH ** -0.5

    def body(q_ref, k_ref, v_ref, a_st_ref, s_st_ref, do_ref, after_ref, dq_ref, dk_ref, dv_ref, acc_ref, gc_ref, qm, dom,
             a_buf, s_buf, sems):
        g_idx = pl.program_id(0)
        i = pl.program_id(1)

        @pl.when(i == 0)
        def _():
            dk_ref[...] = jnp.zeros_like(dk_ref)
            dv_ref[...] = jnp.zeros_like(dv_ref)

        _, prefix, causal, lane = _sb_masks()
        for pp in range(SB_PP):
            pair = slice(pp * SB_BLOCK, (pp + 1) * SB_BLOCK)
            qm[2 * pp], qm[2 * pp + 1] = _half_masked(q_ref[:, pair], lane, scale)
            dom[2 * pp], dom[2 * pp + 1] = _half_masked(do_ref[:, pair], lane)
        acc_ref[...] = jnp.zeros_like(acc_ref)
        gc_ref[...] = jnp.zeros_like(gc_ref)

        def copies(slot, j):
            return _sb_tile_copies(False, (a_buf, s_buf), (a_st_ref, s_st_ref), sems, slot, g_idx * n_chain, i, j, n_chain)

        def step(j, slot, r=None, last=False):
            rows = pl.ds(pl.multiple_of(j * SB_BLOCK, SB_BLOCK), SB_BLOCK)
            qs = slice(0 if r is None else r * SB_BLOCK, SB_TQ)
            mask = None if r is None else causal[r][qs]
            for cp in copies(slot, j):
                cp.wait()
            if not last:
                for cp in copies(1 - slot, j + 1):
                    cp.start()
            for pp in range(SB_PP):
                pair = slice(pp * SB_BLOCK, (pp + 1) * SB_BLOCK)
                kj = k_ref[rows, pair]
                vj = v_ref[rows, pair]
                dk = jnp.zeros((SB_BLOCK, SB_BLOCK), F32)
                dv = jnp.zeros((SB_BLOCK, SB_BLOCK), F32)
                for ch in (2 * pp, 2 * pp + 1):
                    ab = a_buf[slot, ch, qs]
                    g = ab.astype(F32) * _dot_nt(dom[ch, qs], vj)
                    g_left = _sum_right(g, prefix)
                    gc = gc_ref[ch, qs]
                    dz = g - s_buf[slot, ch, qs].astype(F32) * (gc + g_left)
                    if mask is not None:
                        dz = jnp.where(mask, dz, 0.0)
                    dzb = dz.astype(BF16)
                    dk = dk + _dot_tn(dzb, qm[ch, qs])
                    dv = dv + _dot_tn(ab, dom[ch, qs])
                    acc_ref[ch, qs] += _dot(dzb, kj)
                    gc_ref[ch, qs] = gc + jnp.broadcast_to(g_left[:, SB_BLOCK - 1:SB_BLOCK], gc.shape)
                dk_ref[rows, pair] += dk
                dv_ref[rows, pair] += dv

        first = i * n_sub
        for cp in copies(0, 0):
            cp.start()

        @pl.loop(0, first)
        def _(j):
            step(j, j % 2)

        for r in range(n_sub):
            step(first + r, r % 2, r, last=r == n_sub - 1)
        for pp in range(SB_PP):
            dq_ref[:, pp * SB_BLOCK:(pp + 1) * SB_BLOCK] = scale * jnp.where(lane < SB_DH, acc_ref[2 * pp],
                                                                             acc_ref[2 * pp + 1])
    n_g = SB_WIDTH // width
    blk = lambda part: pl.BlockSpec((SB_TQ, width), lambda g, i, part=part: (i, part * n_g + g))
    whole = lambda part: pl.BlockSpec((T, width), lambda g, i, part=part: (0, part * n_g + g))
    return pl.pallas_call(
        body, grid=(n_g, T // SB_TQ), name="sb_bwd",
        in_specs=[blk(0), whole(1), whole(2), pl.BlockSpec(memory_space=pl.ANY), pl.BlockSpec(memory_space=pl.ANY), blk(0),
                  pl.BlockSpec(memory_space=pl.ANY)],
        out_specs=[blk(0), whole(0), whole(0)],
        out_shape=[jax.ShapeDtypeStruct((T, SB_WIDTH), F32)] * 3,
        scratch_shapes=[pltpu.VMEM((n_chain, SB_TQ, SB_BLOCK), F32), pltpu.VMEM((n_chain, SB_TQ, SB_BLOCK), F32),
                        pltpu.VMEM((n_chain, SB_TQ, SB_BLOCK), BF16), pltpu.VMEM((n_chain, SB_TQ, SB_BLOCK), BF16),
                        pltpu.VMEM((2, n_chain, SB_TQ, SB_BLOCK), BF16), pltpu.VMEM((2, n_chain, SB_TQ, SB_BLOCK), BF16),
                        pltpu.SemaphoreType.DMA((2, n_chain, 2))],
        compiler_params=_params(("parallel", "arbitrary")),
    )(qkv, qkv, qkv, a_st, s_st, d_o, after)


HBM = pl.BlockSpec(memory_space=pltpu.HBM)
MESH = pl.DeviceIdType.MESH


def _place():
    return lax.axis_index("x"), lax.axis_index("y"), lax.axis_index("c")


def _all_gather(block):
    rows, cols = block.shape

    def body(x_ref, out_ref, send_sems, recv_sems, local_sem):
        x, y, c = _place()
        me, sibling = (x, y, c), (x, y, 1 - c)
        chips = [(1 - x, y), (x, 1 - y), (1 - x, 1 - y)]

        def slot(px, py, pc):
            return out_ref.at[4 * px + 2 * py + pc]

        def copy(k, blk, to, src=None):
            return pltpu.make_async_remote_copy(
                src_ref=slot(*blk) if src is None else src, dst_ref=slot(*blk),
                send_sem=send_sems.at[k], recv_sem=recv_sems.at[k], device_id=to, device_id_type=MESH)

        mine = pltpu.make_async_copy(x_ref, slot(*me), local_sem)
        mine.start()
        first = [copy(0, me, sibling, src=x_ref)]
        first += [copy(1 + j, me, (*chip, c), src=x_ref) for j, chip in enumerate(chips)]
        for cp in first:
            cp.start()
        passed = [copy(4 + j, (*chip, c), sibling) for j, chip in enumerate(chips)]
        for j, chip in enumerate(chips):
            copy(1 + j, (*chip, c), me).wait_recv()
            passed[j].start()
        copy(0, sibling, me).wait_recv()
        for j, chip in enumerate(chips):
            copy(4 + j, (*chip, 1 - c), me).wait_recv()
        for cp in first + passed:
            cp.wait_send()
        mine.wait()

    return pl.pallas_call(
        body, name="all_gather",
        out_shape=jax.ShapeDtypeStruct((N_DEV, rows, cols), block.dtype),
        in_specs=[HBM], out_specs=HBM,
        scratch_shapes=[pltpu.SemaphoreType.DMA((7,)), pltpu.SemaphoreType.DMA((7,)), pltpu.SemaphoreType.DMA],
    )(block)


SEM = pl.BlockSpec(memory_space=pltpu.SEMAPHORE)
ANY = pl.BlockSpec(memory_space=pl.ANY)
SPLIT_EFFECT = pltpu.SideEffectType.DATAFLOW_SIDE_EFFECTING
N_PEER = N_DEV - 1


def _flow_copies(kind, src_ref, land_ref, send_sems, recv_sems):
    x, y, c = _place()
    copies = []
    for r in range(1, N_DEV):
        px = 1 - x if r & 4 else x
        py = 1 - y if r & 2 else y
        pc = 1 - c if r & 1 else c
        if kind == "gather":
            src, dst = src_ref, land_ref.at[4 * x + 2 * y + c]
        else:
            src, dst = src_ref.at[4 * px + 2 * py + pc], land_ref.at[r - 1]
        copies.append(pltpu.make_async_remote_copy(
            src_ref=src, dst_ref=dst, send_sem=send_sems.at[r - 1], recv_sem=recv_sems.at[r - 1],
            device_id=(px, py, pc), device_id_type=MESH))
    return copies


def _landing(kind, src):
    if kind == "gather":
        return lax.empty((N_DEV,) + src.shape, src.dtype)
    return lax.empty((N_PEER,) + src.shape[1:], src.dtype)


def _push_start(kinds, arrays, after, name):
    n = len(arrays)

    def body(*refs):
        ins, sems, token = refs[:n], refs[n + 1:2 * n + 1], refs[3 * n + 1]
        for f, kind in enumerate(kinds):
            for cp in _flow_copies(kind, ins[2 * f], ins[2 * f + 1], sems[2 * f], sems[2 * f + 1]):
                cp.start()
        token[...] = jnp.zeros_like(token)

    outs = pl.pallas_call(
        body, name=name,
        out_shape=[pltpu.SemaphoreType.DMA((N_PEER,))] * n + [pltpu.HBM(a.shape, a.dtype) for a in arrays]
        + [jax.ShapeDtypeStruct((8, 128), F32)],
        in_specs=[HBM] * n + [ANY], out_specs=[SEM] * n + [HBM] * n + [pl.BlockSpec(memory_space=pltpu.VMEM)],
        input_output_aliases={i: n + i for i in range(n)},
        compiler_params=pltpu.CompilerParams(has_side_effects=SPLIT_EFFECT),
    )(*[pltpu.with_memory_space_constraint(a, pltpu.HBM) for a in arrays], after)
    return outs[:n], outs[n:2 * n], outs[2 * n]


def _push_wait(kinds, sems, arrays, after, name):
    n = len(arrays)

    def body(*refs):
        ins, sem_refs = refs[:n], refs[n:2 * n]
        for f, kind in enumerate(kinds):
            for cp in _flow_copies(kind, ins[2 * f], ins[2 * f + 1], sem_refs[2 * f], sem_refs[2 * f + 1]):
                cp.wait_send()
                cp.wait_recv()

    outs = pl.pallas_call(
        body, name=name,
        out_shape=[pltpu.HBM(a.shape, a.dtype) for a in arrays],
        in_specs=[HBM] * n + [SEM] * n + [ANY], out_specs=[HBM] * n,
        input_output_aliases={i: i for i in range(n)},
        compiler_params=pltpu.CompilerParams(has_side_effects=SPLIT_EFFECT),
    )(*arrays, *sems, after)
    return outs


def _sum_blocks(own, got, name):
    n, rows, cols = got.shape
    tr = next(t for t in (128, 64, 32) if rows % t == 0)

    def body(own_ref, got_ref, out_ref):
        acc = own_ref[...].astype(F32)
        for r in range(n):
            acc = acc + got_ref[r].astype(F32)
        out_ref[...] = acc

    return pl.pallas_call(
        body, grid=(rows // tr,), name=name,
        in_specs=[pl.BlockSpec((tr, cols), lambda i: (i, 0)), pl.BlockSpec((n, tr, cols), lambda i: (0, i, 0))],
        out_specs=pl.BlockSpec((tr, cols), lambda i: (i, 0)),
        out_shape=jax.ShapeDtypeStruct((rows, cols), F32),
        compiler_params=_params(("parallel",)),
    )(own, got)


def _adamw_math(w, g, m, v):
    m = ADAM_B1 * m + (1.0 - ADAM_B1) * g
    v = ADAM_B2 * v + (1.0 - ADAM_B2) * (g * g)
    m_hat = m / (1.0 - ADAM_B1 ** ADAM_STEP)
    v_hat = v / (1.0 - ADAM_B2 ** ADAM_STEP)
    delta = -ADAM_LR * (m_hat / (jnp.sqrt(v_hat) + ADAM_EPS) + ADAM_WD * w)
    return delta, m, v


def _adamw(w, g, m, v, name):
    rows, cols = w.shape
    tr = rows if rows <= 352 else 256

    def body(w_ref, g_ref, m_ref, v_ref, d_out, m_out, v_out):
        d_out[...], m_out[...], v_out[...] = _adamw_math(w_ref[...], g_ref[...], m_ref[...], v_ref[...])

    spec = pl.BlockSpec((tr, cols), lambda i: (i, 0))
    return pl.pallas_call(
        body, grid=(rows // tr,), name=name,
        in_specs=[spec] * 4, out_specs=[spec] * 3,
        out_shape=[jax.ShapeDtypeStruct((rows, cols), F32)] * 3,
        compiler_params=_params(("parallel",)),
    )(w, g, m, v)


ROW_ATTN_PRE, ROW_ATTN_POST, ROW_FFN_PRE, ROW_FFN_POST, ROW_OUT_NORMS, ROW_GAMMA, ROW_LOSS = range(7)


def _small_update(smalls, w, m, v):
    def body(s_ref, w_ref, m_ref, v_ref, g_out, d_out, m_out, v_out):
        g = s_ref[0]
        for j in range(1, N_DEV):
            g = g + s_ref[j]
        wv = w_ref[...]
        g0 = wv[ROW_GAMMA:ROW_GAMMA + 1, 0:HG_WIDTH]
        g1 = wv[ROW_GAMMA:ROW_GAMMA + 1, HG_WIDTH:]
        mx = jnp.maximum(g0, g1)
        e0 = jnp.exp(g0 - mx)
        e1 = jnp.exp(g1 - mx)
        lb = e0 / (e0 + e1)
        dg0 = g[ROW_GAMMA:ROW_GAMMA + 1, 0:HG_WIDTH] * lb * (1.0 - lb)
        row = lax.broadcasted_iota(jnp.int32, g.shape, 0)
        g = jnp.where(row == ROW_GAMMA, jnp.concatenate([dg0, -dg0], axis=1), g)
        g_out[...] = g
        d_out[...], m_out[...], v_out[...] = _adamw_math(wv, g, m_ref[...], v_ref[...])

    return pl.pallas_call(
        body, name="small_update",
        out_shape=[jax.ShapeDtypeStruct((8, D_MODEL), F32)] * 4,
    )(smalls, w, m, v)


def _local_step(x, p, target, g_attn_pre, w_in_t, gamma, g_hg, g_sb, g_attn_post, g_ffn_pre, g_ffn_post, later_weights,
                grads_ready):
    proj, u, qkv = _in_proj(x, g_attn_pre, w_in_t)
    o_hg, hist = _hg_fwd(proj, gamma)
    o_sb, a_st, s_st = _sb_fwd(qkv)
    w_out, w_gu_t, w_down, w_pp_t, w_pg = later_weights(o_sb)
    cat, mix, h1 = _out_proj(o_hg, proj, o_sb, x, g_hg, g_sb, g_attn_post, w_out)
    u2, gu, y, h2 = _ffn_fwd(h1, g_ffn_pre, w_gu_t, w_down, g_ffn_post)
    dh2, de, dg, loss = _ple_loss(h2, p, target, w_pp_t, w_pg)

    d_wpp_t = _wgrad(de, p, "wgrad_ple_proj")
    d_wpg = _wgrad(h2, dg, "wgrad_ple_gate")
    dy, a, dgu, dh1, dg_ffn_post, dg_ffn_pre = _ffn_bwd(dh2, y, h1, gu, g_ffn_post, g_ffn_pre, w_gu_t, w_down)
    d_wdown = _wgrad(a, dy, "wgrad_down")
    sent = grads_ready("ffn", (_wgrad(dgu, u2, "wgrad_gate_up"), d_wdown, d_wpp_t, d_wpg))
    dmix, d_ohg, d_hg, d_osb, dg_attn_post, dg_hg, dg_sb = _out_bwd(dh1, mix, o_hg, proj, o_sb, g_hg, g_sb, g_attn_post,
                                                                    w_out, sent)
    sent = grads_ready("out", (_wgrad(cat, dmix, "wgrad_out"),))
    dq, dk, dv = _sb_bwd(qkv, a_st, s_st, d_osb, sent)
    d_hgrn, d_lb = _hg_bwd(proj, gamma, d_ohg, hist)
    dproj, dx, dg_attn_pre = _in_bwd(d_hgrn, d_hg, dq, dk, dv, x, dh1, g_attn_pre, w_in_t)
    grads_ready("in", (_wgrad(dproj, u, "wgrad_in"),))
    return loss, dx, (dg_attn_pre, dg_attn_post, dg_ffn_pre, dg_ffn_post, dg_hg, dg_sb, d_lb)


def _pack_row(*parts):
    return jnp.concatenate([q.reshape(1, -1) for q in parts], axis=1)


def kernel(x, p, attn_pre_norm, w_in, hg_lower_gamma, hg_out_norm, sb_out_norm, w_out, attn_post_norm, ffn_pre_norm, w_gate_up, w_down, ffn_post_norm, ple_proj, ple_gate, loss_target, m_attn_pre_norm, m_w_in, m_hg_lower_gamma, m_hg_out_norm, m_sb_out_norm, m_w_out, m_attn_post_norm, m_ffn_pre_norm, m_w_gate_up, m_w_down, m_ffn_post_norm, m_ple_proj, m_ple_gate, v_attn_pre_norm, v_w_in, v_hg_lower_gamma, v_hg_out_norm, v_sb_out_norm, v_w_out, v_attn_post_norm, v_ffn_pre_norm, v_w_gate_up, v_w_down, v_ffn_post_norm, v_ple_proj, v_ple_gate):
    big = (w_in[0], w_out[0], w_gate_up[0], w_down[0], ple_proj[0], ple_gate[0])
    big_m = (m_w_in[0], m_w_out[0], m_w_gate_up[0], m_w_down[0], m_ple_proj[0], m_ple_gate[0])
    big_v = (v_w_in[0], v_w_out[0], v_w_gate_up[0], v_w_down[0], v_ple_proj[0], v_ple_gate[0])
    names = ("w_in", "w_out", "w_gate_up", "w_down", "ple_proj", "ple_gate")

    by_columns = (True, False, True, False, True, False)
    shards = [(w.T if t else w).astype(BF16) for w, t in zip(big, by_columns)]
    w_in_t = _all_gather(shards[0]).reshape(IN_COLS, D_MODEL)
    me = 4 * lax.axis_index("x") + 2 * lax.axis_index("y") + lax.axis_index("c")
    later = shards[1:]
    w_sems, w_thru, token = _push_start(
        ("gather",) * len(later), [a for s in later for a in (s, _landing("gather", s))], w_in_t, "weights_start")

    def later_weights(after):
        done = _push_wait(("gather",) * len(later), w_sems, w_thru, after, "weights_wait")
        whole = [lax.dynamic_update_index_in_dim(got, mine, me, 0) for mine, got in zip(done[0::2], done[1::2])]
        return (whole[0].reshape(D_MODEL, D_MODEL), whole[1].reshape(2, D_FF, D_MODEL), whole[2].reshape(D_FF, D_MODEL),
                whole[3].reshape(D_MODEL, PLE_DIM), whole[4].reshape(D_MODEL, D_MODEL))

    flights = {}

    def by_owner(g):
        return g.reshape(N_DEV, -1, g.shape[-1])

    def grads_ready(group, grads):
        blocks = [by_owner(g) for g in grads]
        if group == "in":
            flights[group] = blocks[0]
            return None
        sems, thru, sent = _push_start(
            ("scatter",) * len(blocks), [a for b in blocks for a in (b, _landing("scatter", b))], token,
            "grads_" + group + "_start")
        flights[group] = (sems, thru)
        return sent

    loss, dx, smalls = _local_step(
        x[0], p[0, 0], loss_target[0], attn_pre_norm + token[0:1, 0:1], w_in_t,
        hg_lower_gamma, hg_out_norm, sb_out_norm, attn_post_norm, ffn_pre_norm, ffn_post_norm, later_weights, grads_ready)
    dg_attn_pre, dg_attn_post, dg_ffn_pre, dg_ffn_post, dg_hg, dg_sb, d_lb = smalls

    zeros_half = jnp.zeros((1, HG_WIDTH), F32)
    small_pack = jnp.concatenate([
        dg_attn_pre, dg_attn_post, dg_ffn_pre, dg_ffn_post, _pack_row(dg_hg, dg_sb), _pack_row(d_lb, zeros_half),
        jnp.broadcast_to(loss[0:1, 0:1], (1, D_MODEL)), jnp.zeros((1, D_MODEL), F32)], axis=0)
    in_blocks = flights["in"]
    in_sems, in_thru, in_token = _push_start(
        ("scatter", "gather"),
        (in_blocks, _landing("scatter", in_blocks), small_pack, _landing("gather", small_pack)), token,
        "grads_in_start")
    ffn = _push_wait(("scatter",) * 4, *flights["ffn"], in_token, "grads_ffn_wait")
    sent_out, got_out = _push_wait(("scatter",), *flights["out"], ffn[1], "grads_out_wait")
    sent_in, got_in, my_small, all_smalls = _push_wait(("scatter", "gather"), in_sems, in_thru, got_out, "grads_in_wait")
    all_smalls = lax.dynamic_update_index_in_dim(all_smalls, my_small, me, 0)

    def reduced(sent, got, name):
        return _sum_blocks(lax.dynamic_index_in_dim(sent, me, 0, keepdims=False), got, "grad_sum_" + name)

    sums = (reduced(sent_in, got_in, "w_in"), reduced(sent_out, got_out, "w_out"), reduced(ffn[0], ffn[1], "w_gate_up"),
            reduced(ffn[2], ffn[3], "w_down"), reduced(ffn[4], ffn[5], "ple_proj"), reduced(ffn[6], ffn[7], "ple_gate"))

    out_g, out_d, out_m, out_v = {}, {}, {}, {}
    for i, name in enumerate(names):
        g = sums[i].T if by_columns[i] else sums[i]
        out_g[name] = g[None]
        d, m, v = _adamw(big[i], g, big_m[i], big_v[i], "adamw_" + name)
        out_d[name], out_m[name], out_v[name] = d[None], m[None], v[None]

    def small_rows(pre, gam, hg, sb, post, fpre, fpost):
        return jnp.concatenate([pre, post, fpre, fpost, _pack_row(hg, sb), _pack_row(gam[0], gam[1]),
                                jnp.zeros((2, D_MODEL), F32)], axis=0)

    packs = _small_update(
        all_smalls,
        small_rows(attn_pre_norm, hg_lower_gamma, hg_out_norm, sb_out_norm, attn_post_norm, ffn_pre_norm, ffn_post_norm),
        small_rows(m_attn_pre_norm, m_hg_lower_gamma, m_hg_out_norm, m_sb_out_norm, m_attn_post_norm, m_ffn_pre_norm,
                   m_ffn_post_norm),
        small_rows(v_attn_pre_norm, v_hg_lower_gamma, v_hg_out_norm, v_sb_out_norm, v_attn_post_norm, v_ffn_pre_norm,
                   v_ffn_post_norm))

    def unpack(pk):
        return {
            "attn_pre_norm": pk[ROW_ATTN_PRE:ROW_ATTN_PRE + 1],
            "hg_lower_gamma": pk[ROW_GAMMA].reshape(2, HG_WIDTH),
            "hg_out_norm": pk[ROW_OUT_NORMS:ROW_OUT_NORMS + 1, :HG_WIDTH],
            "sb_out_norm": pk[ROW_OUT_NORMS:ROW_OUT_NORMS + 1, HG_WIDTH:],
            "attn_post_norm": pk[ROW_ATTN_POST:ROW_ATTN_POST + 1],
            "ffn_pre_norm": pk[ROW_FFN_PRE:ROW_FFN_PRE + 1],
            "ffn_post_norm": pk[ROW_FFN_POST:ROW_FFN_POST + 1],
        }

    sg, sd, sm, sv = (unpack(pk) for pk in packs)
    out_g.update(sg), out_d.update(sd), out_m.update(sm), out_v.update(sv)
    order = ("attn_pre_norm", "w_in", "hg_lower_gamma", "hg_out_norm", "sb_out_norm", "w_out", "attn_post_norm",
             "ffn_pre_norm", "w_gate_up", "w_down", "ffn_post_norm", "ple_proj", "ple_gate")
    total_loss = packs[0][ROW_LOSS, 0]
    return (total_loss, dx[None], *[out_g[n] for n in order], *[out_d[n] for n in order],
            *[out_m[n] for n in order], *[out_v[n] for n in order])
```

```python
import functools

import jax
import jax.numpy as jnp
from jax import lax
from jax.experimental import pallas as pl
from jax.experimental.pallas import tpu as pltpu

F32 = jnp.float32
BF16 = jnp.bfloat16

D_MODEL = 1024
HG_WIDTH = 512
HG_HEADS = 4
HG_DK = 128
SB_WIDTH = 512
SB_HEADS = 8
SB_DH = 64
SB_BLOCK = 128
SB_PP = 2
SB_TQ = 512
SB_TK = 256
D_FF = 2816
PLE_DIM = 256
IN_COLS = 4 * HG_WIDTH + 3 * SB_WIDTH
EPS = 1e-6
N_DEV = 8

HG_SUB = 16
HG_TILE = 128
FFN_TF = 256
ROW_TILE = 256
VMEM_LIMIT = 56 * 1024 * 1024
WGRAD_ACC_BYTES = 8 * 1024 * 1024

ADAM_LR = 0.001
ADAM_B1 = 0.9
ADAM_B2 = 0.999
ADAM_EPS = 1e-08
ADAM_WD = 0.01
ADAM_STEP = 10

NT_DIMS = (((1,), (1,)), ((), ()))
TN_DIMS = (((0,), (0,)), ((), ()))


def _params(semantics):
    return pltpu.CompilerParams(dimension_semantics=semantics, vmem_limit_bytes=VMEM_LIMIT)


def _dot(a, b):
    return jnp.dot(a, b, preferred_element_type=F32)


def _dot_nt(a, b):
    return lax.dot_general(a, b, NT_DIMS, preferred_element_type=F32)


def _dot_tn(a, b):
    return lax.dot_general(a, b, TN_DIMS, preferred_element_type=F32)


def _sigmoid(z):
    return 1.0 / (1.0 + jnp.exp(-z))


def _rstd(xv):
    return lax.rsqrt(jnp.mean(xv * xv, axis=-1, keepdims=True) + EPS)


def _rms_bwd(xv, r, g, dn):
    xn = xv * r
    gh = dn * g
    dx = r * (gh - xn * jnp.mean(gh * xn, axis=-1, keepdims=True))
    return dx, dn * xn


def _colsum(a):
    return jnp.sum(a, axis=0, keepdims=True)


def _row_spec(tm, width, col=0):
    return pl.BlockSpec((tm, width), lambda i, col=col: (i, col))


def _full_spec(shape):
    return pl.BlockSpec(shape, lambda i: (0,) * len(shape))


def _in_proj(x, g, w_t):
    T = x.shape[0]
    tm = ROW_TILE

    def body(x_ref, g_ref, w_ref, proj_ref, u_ref, qkv_ref):
        xv = x_ref[...]
        u = (xv * _rstd(xv) * g_ref[...]).astype(BF16)
        u_ref[...] = u
        proj = _dot_nt(u, w_ref[...])
        proj_ref[...] = proj
        qkv_ref[...] = proj[:, 4 * HG_WIDTH:].astype(BF16)

    return pl.pallas_call(
        body, grid=(T // tm,), name="in_proj",
        in_specs=[_row_spec(tm, D_MODEL), _full_spec((1, D_MODEL)), _full_spec((IN_COLS, D_MODEL))],
        out_specs=[_row_spec(tm, IN_COLS), _row_spec(tm, D_MODEL), _row_spec(tm, 3 * SB_WIDTH)],
        out_shape=[jax.ShapeDtypeStruct((T, IN_COLS), F32), jax.ShapeDtypeStruct((T, D_MODEL), BF16),
                   jax.ShapeDtypeStruct((T, 3 * SB_WIDTH), BF16)],
        compiler_params=_params(("parallel",)),
    )(x, g, w_t)


def _out_proj(o_hg, proj, o_sb, x, g_hg, g_sb, g_post, w):
    T = x.shape[0]
    tm = ROW_TILE

    def body(ohg_ref, hg_ref, osb_ref, x_ref, ghg_ref, gsb_ref, gpost_ref, w_ref, cat_ref, mix_ref, h1_ref):
        ohg = ohg_ref[...]
        hg = hg_ref[...]
        osb = osb_ref[...]
        a = ohg * _rstd(ohg) * ghg_ref[...] * (hg * _sigmoid(hg))
        n2 = osb * _rstd(osb) * gsb_ref[...]
        cat = jnp.concatenate([a, n2], axis=1).astype(BF16)
        cat_ref[...] = cat
        mix = _dot(cat, w_ref[...])
        mix_ref[...] = mix
        h1_ref[...] = x_ref[...] + mix * _rstd(mix) * gpost_ref[...]

    return pl.pallas_call(
        body, grid=(T // tm,), name="out_proj",
        in_specs=[_row_spec(tm, HG_WIDTH), _row_spec(tm, HG_WIDTH, 3), _row_spec(tm, SB_WIDTH), _row_spec(tm, D_MODEL),
                  _full_spec((1, HG_WIDTH)), _full_spec((1, SB_WIDTH)), _full_spec((1, D_MODEL)),
                  _full_spec((D_MODEL, D_MODEL))],
        out_specs=[_row_spec(tm, D_MODEL)] * 3,
        out_shape=[jax.ShapeDtypeStruct((T, D_MODEL), BF16), jax.ShapeDtypeStruct((T, D_MODEL), F32),
                   jax.ShapeDtypeStruct((T, D_MODEL), F32)],
        compiler_params=_params(("parallel",)),
    )(o_hg, proj, o_sb, x, g_hg, g_sb, g_post, w)


def _ffn_fwd(h1, g_pre, w_gu_t, w_down, g_post):
    T = h1.shape[0]
    tm = 2 * ROW_TILE
    n_f = D_FF // FFN_TF

    def body(h1_ref, gpre_ref, wgu_ref, wd_ref, gpost_ref, u2_ref, gu_ref, y_ref, h2_ref, acc_ref):
        j = pl.program_id(1)

        @pl.when(j == 0)
        def _():
            hv = h1_ref[...]
            u2_ref[...] = (hv * _rstd(hv) * gpre_ref[...]).astype(BF16)
            acc_ref[...] = jnp.zeros_like(acc_ref)

        u2 = u2_ref[...]
        gate = _dot_nt(u2, wgu_ref[0])
        up = _dot_nt(u2, wgu_ref[1])
        gu_ref[0] = gate
        gu_ref[1] = up
        a = (gate * _sigmoid(gate) * up).astype(BF16)
        acc_ref[...] += _dot(a, wd_ref[...])

        @pl.when(j == n_f - 1)
        def _():
            y = acc_ref[...]
            y_ref[...] = y
            h2_ref[...] = h1_ref[...] + y * _rstd(y) * gpost_ref[...]

    row = lambda width: pl.BlockSpec((tm, width), lambda i, j: (i, 0))
    vec = pl.BlockSpec((1, D_MODEL), lambda i, j: (0, 0))
    return pl.pallas_call(
        body, grid=(T // tm, n_f), name="ffn_fwd",
        in_specs=[row(D_MODEL), vec,
                  pl.BlockSpec((2, FFN_TF, D_MODEL), lambda i, j: (0, j, 0)),
                  pl.BlockSpec((FFN_TF, D_MODEL), lambda i, j: (j, 0)), vec],
        out_specs=[row(D_MODEL),
                   pl.BlockSpec((2, tm, FFN_TF), lambda i, j: (0, i, j)),
                   row(D_MODEL), row(D_MODEL)],
        out_shape=[jax.ShapeDtypeStruct((T, D_MODEL), BF16), jax.ShapeDtypeStruct((2, T, D_FF), F32),
                   jax.ShapeDtypeStruct((T, D_MODEL), F32), jax.ShapeDtypeStruct((T, D_MODEL), F32)],
        scratch_shapes=[pltpu.VMEM((tm, D_MODEL), F32)],
        compiler_params=_params(("parallel", "arbitrary")),
    )(h1, g_pre, w_gu_t, w_down, g_post)


def _ple_loss(h2, p, target, w_pp_t, w_pg):
    T = h2.shape[0]
    tm = ROW_TILE

    def body(h2_ref, p_ref, t_ref, wpp_ref, wpg_ref, dh2_ref, de_ref, dg_ref, loss_ref):
        i = pl.program_id(0)
        h2v = h2_ref[...]
        e = _dot_nt(p_ref[...].astype(BF16), wpp_ref[...])
        sg = _sigmoid(_dot(h2v.astype(BF16), wpg_ref[...]))
        diff = h2v + e * sg - t_ref[...]
        part = jnp.sum(jnp.sum(diff * diff, axis=1, keepdims=True), axis=0, keepdims=True) * (0.5 / D_MODEL)

        @pl.when(i == 0)
        def _():
            loss_ref[...] = jnp.zeros_like(loss_ref)

        loss_ref[...] += jnp.broadcast_to(part, loss_ref.shape)
        dh3 = diff * (1.0 / D_MODEL)
        de_ref[...] = (dh3 * sg).astype(BF16)
        dg = (dh3 * e * sg * (1.0 - sg)).astype(BF16)
        dg_ref[...] = dg
        dh2_ref[...] = dh3 + _dot_nt(dg, wpg_ref[...])

    return pl.pallas_call(
        body, grid=(T // tm,), name="ple_loss",
        in_specs=[_row_spec(tm, D_MODEL), _row_spec(tm, PLE_DIM), _row_spec(tm, D_MODEL),
                  _full_spec((D_MODEL, PLE_DIM)), _full_spec((D_MODEL, D_MODEL))],
        out_specs=[_row_spec(tm, D_MODEL)] * 3 + [_full_spec((8, 128))],
        out_shape=[jax.ShapeDtypeStruct((T, D_MODEL), F32), jax.ShapeDtypeStruct((T, D_MODEL), BF16),
                   jax.ShapeDtypeStruct((T, D_MODEL), BF16), jax.ShapeDtypeStruct((8, 128), F32)],
        compiler_params=_params(("arbitrary",)),
    )(h2, p, target, w_pp_t, w_pg)


def _ffn_bwd(dh2, y, h1, gu, g_post, g_pre, w_gu_t, w_down):
    T = h1.shape[0]
    tm = 2 * ROW_TILE
    n_f = D_FF // FFN_TF

    def body(dh2_ref, y_ref, h1_ref, gu_ref, gpost_ref, gpre_ref, wgu_ref, wd_ref,
             dy_ref, a_ref, dgu_ref, dh1_ref, dgpost_ref, dgpre_ref, du2_ref):
        i = pl.program_id(0)
        j = pl.program_id(1)

        @pl.when((i == 0) & (j == 0))
        def _():
            dgpost_ref[...] = jnp.zeros_like(dgpost_ref)
            dgpre_ref[...] = jnp.zeros_like(dgpre_ref)

        @pl.when(j == 0)
        def _():
            yv = y_ref[...]
            dy, gterm = _rms_bwd(yv, _rstd(yv), gpost_ref[...], dh2_ref[...])
            dy_ref[...] = dy.astype(BF16)
            dgpost_ref[...] += _colsum(gterm)
            du2_ref[...] = jnp.zeros_like(du2_ref)

        da = _dot_nt(dy_ref[...], wd_ref[...])
        gate = gu_ref[0]
        up = gu_ref[1]
        s = _sigmoid(gate)
        sil = gate * s
        a_ref[...] = (sil * up).astype(BF16)
        dgate = (da * up * (s * (1.0 + gate * (1.0 - s)))).astype(BF16)
        dup = (da * sil).astype(BF16)
        dgu_ref[0] = dgate
        dgu_ref[1] = dup
        du2_ref[...] += _dot(dgate, wgu_ref[0]) + _dot(dup, wgu_ref[1])

        @pl.when(j == n_f - 1)
        def _():
            hv = h1_ref[...]
            dx, gterm = _rms_bwd(hv, _rstd(hv), gpre_ref[...], du2_ref[...])
            dh1_ref[...] = dh2_ref[...] + dx
            dgpre_ref[...] += _colsum(gterm)

    row = lambda width: pl.BlockSpec((tm, width), lambda i, j: (i, 0))
    col = pl.BlockSpec((tm, FFN_TF), lambda i, j: (i, j))
    both = pl.BlockSpec((2, tm, FFN_TF), lambda i, j: (0, i, j))
    vec = pl.BlockSpec((1, D_MODEL), lambda i, j: (0, 0))
    return pl.pallas_call(
        body, grid=(T // tm, n_f), name="ffn_bwd",
        in_specs=[row(D_MODEL), row(D_MODEL), row(D_MODEL), both, vec, vec,
                  pl.BlockSpec((2, FFN_TF, D_MODEL), lambda i, j: (0, j, 0)),
                  pl.BlockSpec((FFN_TF, D_MODEL), lambda i, j: (j, 0))],
        out_specs=[row(D_MODEL), col, both, row(D_MODEL), vec, vec],
        out_shape=[jax.ShapeDtypeStruct((T, D_MODEL), BF16), jax.ShapeDtypeStruct((T, D_FF), BF16),
                   jax.ShapeDtypeStruct((2, T, D_FF), BF16),
                   jax.ShapeDtypeStruct((T, D_MODEL), F32), jax.ShapeDtypeStruct((1, D_MODEL), F32),
                   jax.ShapeDtypeStruct((1, D_MODEL), F32)],
        scratch_shapes=[pltpu.VMEM((tm, D_MODEL), F32)],
        compiler_params=_params(("arbitrary", "arbitrary")),
    )(dh2, y, h1, gu, g_post, g_pre, w_gu_t, w_down)


def _out_bwd(dh1, mix, o_hg, proj, o_sb, g_hg, g_sb, g_post, w, after):
    T = dh1.shape[0]
    tm = ROW_TILE

    def body(dh1_ref, mix_ref, ohg_ref, hg_ref, osb_ref, ghg_ref, gsb_ref, gpost_ref, w_ref, after_ref,
             dmix_ref, dohg_ref, dhg_ref, dosb_ref, dgpost_ref, dghg_ref, dgsb_ref):
        i = pl.program_id(0)

        @pl.when(i == 0)
        def _():
            dgpost_ref[...] = jnp.zeros_like(dgpost_ref)
            dghg_ref[...] = jnp.zeros_like(dghg_ref)
            dgsb_ref[...] = jnp.zeros_like(dgsb_ref)

        mix = mix_ref[...]
        dmix, gterm = _rms_bwd(mix, _rstd(mix), gpost_ref[...], dh1_ref[...])
        dgpost_ref[...] += _colsum(gterm)
        dmix = dmix.astype(BF16)
        dmix_ref[...] = dmix
        dcat = _dot_nt(dmix, w_ref[...])
        da = dcat[:, :HG_WIDTH]
        dn2 = dcat[:, HG_WIDTH:]
        ohg = ohg_ref[...]
        r1 = _rstd(ohg)
        hg = hg_ref[...]
        s = _sigmoid(hg)
        dhg_ref[...] = da * (ohg * r1 * ghg_ref[...]) * (s * (1.0 + hg * (1.0 - s)))
        dohg, gterm = _rms_bwd(ohg, r1, ghg_ref[...], da * (hg * s))
        dohg_ref[...] = dohg
        dghg_ref[...] += _colsum(gterm)
        osb = osb_ref[...]
        dosb, gterm = _rms_bwd(osb, _rstd(osb), gsb_ref[...], dn2)
        dosb_ref[...] = dosb
        dgsb_ref[...] += _colsum(gterm)

    return pl.pallas_call(
        body, grid=(T // tm,), name="out_bwd",
        in_specs=[_row_spec(tm, D_MODEL), _row_spec(tm, D_MODEL), _row_spec(tm, HG_WIDTH), _row_spec(tm, HG_WIDTH, 3),
                  _row_spec(tm, SB_WIDTH), _full_spec((1, HG_WIDTH)), _full_spec((1, SB_WIDTH)),
                  _full_spec((1, D_MODEL)), _full_spec((D_MODEL, D_MODEL)), pl.BlockSpec(memory_space=pl.ANY)],
        out_specs=[_row_spec(tm, D_MODEL), _row_spec(tm, HG_WIDTH), _row_spec(tm, HG_WIDTH), _row_spec(tm, SB_WIDTH),
                   _full_spec((1, D_MODEL)), _full_spec((1, HG_WIDTH)), _full_spec((1, SB_WIDTH))],
        out_shape=[jax.ShapeDtypeStruct((T, D_MODEL), BF16), jax.ShapeDtypeStruct((T, HG_WIDTH), F32),
                   jax.ShapeDtypeStruct((T, HG_WIDTH), F32), jax.ShapeDtypeStruct((T, SB_WIDTH), F32),
                   jax.ShapeDtypeStruct((1, D_MODEL), F32), jax.ShapeDtypeStruct((1, HG_WIDTH), F32),
                   jax.ShapeDtypeStruct((1, SB_WIDTH), F32)],
        compiler_params=_params(("arbitrary",)),
    )(dh1, mix, o_hg, proj, o_sb, g_hg, g_sb, g_post, w, after)


def _in_bwd(d_hgrn, d_hg, d_sq, d_sk, d_sv, x, dh1, g_pre, w_t):
    T = x.shape[0]
    tm = ROW_TILE

    def body(dh_ref, dhg_ref, dsq_ref, dsk_ref, dsv_ref, x_ref, dh1_ref, gpre_ref, w_ref, dproj_ref, dx_ref, dgpre_ref):
        i = pl.program_id(0)

        @pl.when(i == 0)
        def _():
            dgpre_ref[...] = jnp.zeros_like(dgpre_ref)

        dp = jnp.concatenate([dh_ref[...], dhg_ref[...], dsq_ref[...], dsk_ref[...], dsv_ref[...]], axis=1).astype(BF16)
        dproj_ref[...] = dp
        du = _dot(dp, w_ref[...])
        xv = x_ref[...]
        dx, gterm = _rms_bwd(xv, _rstd(xv), gpre_ref[...], du)
        dx_ref[...] = dh1_ref[...] + dx
        dgpre_ref[...] += _colsum(gterm)

    return pl.pallas_call(
        body, grid=(T // tm,), name="in_bwd",
        in_specs=[_row_spec(tm, 3 * HG_WIDTH), _row_spec(tm, HG_WIDTH), _row_spec(tm, SB_WIDTH), _row_spec(tm, SB_WIDTH),
                  _row_spec(tm, SB_WIDTH), _row_spec(tm, D_MODEL), _row_spec(tm, D_MODEL), _full_spec((1, D_MODEL)),
                  _full_spec((IN_COLS, D_MODEL))],
        out_specs=[_row_spec(tm, IN_COLS), _row_spec(tm, D_MODEL), _full_spec((1, D_MODEL))],
        out_shape=[jax.ShapeDtypeStruct((T, IN_COLS), BF16), jax.ShapeDtypeStruct((T, D_MODEL), F32),
                   jax.ShapeDtypeStruct((1, D_MODEL), F32)],
        compiler_params=_params(("arbitrary",)),
    )(d_hgrn, d_hg, d_sq, d_sk, d_sv, x, dh1, g_pre, w_t)


def _wgrad(a, b, name):
    stacked = a.ndim == 3
    S, T, K = a.shape if stacked else (1,) + a.shape
    N = b.shape[1]
    tk = K
    while tk * N * 4 > WGRAD_ACC_BYTES:
        tk //= 2
    assert K % tk == 0 and tk % 128 == 0
    tt = 512
    n_t = T // tt

    def body(a_ref, b_ref, o_ref, acc_ref):
        t = pl.program_id(2)

        @pl.when(t == 0)
        def _():
            acc_ref[...] = jnp.zeros_like(acc_ref)

        acc_ref[...] += _dot_tn(a_ref[...].astype(BF16), b_ref[...].astype(BF16))

        @pl.when(t == n_t - 1)
        def _():
            o_ref[...] = acc_ref[...].astype(BF16)

    if stacked:
        a_spec = pl.BlockSpec((None, tt, tk), lambda s, k, t: (s, t, k))
        o_spec = pl.BlockSpec((None, tk, N), lambda s, k, t: (s, k, 0))
        o_shape = (S, K, N)
    else:
        a_spec = pl.BlockSpec((tt, tk), lambda s, k, t: (t, k))
        o_spec = pl.BlockSpec((tk, N), lambda s, k, t: (k, 0))
        o_shape = (K, N)
    return pl.pallas_call(
        body, grid=(S, K // tk, n_t), name=name,
        in_specs=[a_spec, pl.BlockSpec((tt, N), lambda s, k, t: (t, 0))],
        out_specs=o_spec,
        out_shape=jax.ShapeDtypeStruct(o_shape, BF16),
        scratch_shapes=[pltpu.VMEM((tk, N), F32)],
        compiler_params=_params(("parallel", "parallel", "arbitrary")),
    )(a, b)


def _hg_gates(hq, hf, gam):
    g0 = gam[0:1, :]
    g1 = gam[1:2, :]
    mx = jnp.maximum(g0, g1)
    e0 = jnp.exp(g0 - mx)
    e1 = jnp.exp(g1 - mx)
    lb = e0 / (e0 + e1)
    s = _sigmoid(hf)
    f = lb + (1.0 - lb) * s
    sq = _sigmoid(hq)
    return hq * sq, sq, s, f, (1.0 - lb) * (1.0 - s), jnp.log(f), lb


def _prefix_in_sub(a, rowmod):
    n = a.shape[0]
    sh = 1
    while sh < HG_SUB:
        a = a + jnp.where(rowmod >= sh, pltpu.roll(a, sh, axis=0), 0.0)
        sh *= 2
    return a


def _suffix_in_sub(a, rowmod):
    n = a.shape[0]
    sh = 1
    while sh < HG_SUB:
        a = a + jnp.where(rowmod < HG_SUB - sh, pltpu.roll(a, n - sh, axis=0), 0.0)
        sh *= 2
    return a


def _hg_fwd(proj, gam):
    T = proj.shape[0]
    tb = HG_TILE
    n_sub = tb // HG_SUB

    def body(hq_ref, hf_ref, hi_ref, gam_ref, o_ref, hist_ref, st_ref, qt_ref, kk_ref, b_ref, od_ref):
        i = pl.program_id(0)

        @pl.when(i == 0)
        def _():
            st_ref[...] = jnp.zeros_like(st_ref)

        rowmod_w = lax.broadcasted_iota(jnp.int32, (tb, HG_WIDTH), 0) % HG_SUB
        rowmod = lax.broadcasted_iota(jnp.int32, (tb, 1), 0) % HG_SUB
        q, _, _, _, kk, lf, _ = _hg_gates(hq_ref[...], hf_ref[...], gam_ref[...])
        b = _prefix_in_sub(lf, rowmod_w)
        qt_ref[...] = (q * jnp.exp(b)).astype(BF16)
        kk_ref[...] = kk
        b_ref[...] = b

        for h in range(HG_HEADS):
            cs = slice(h * HG_DK, (h + 1) * HG_DK)
            qh = q[:, cs]
            kh = kk[:, cs]
            bh = b[:, cs]
            vh = hi_ref[:, cs]
            acc = jnp.sum(qh * kh, axis=1, keepdims=True) * vh
            for d in range(1, HG_SUB):
                e = qh * pltpu.roll(kh, d, axis=0) * jnp.exp(bh - pltpu.roll(bh, d, axis=0))
                sc = jnp.where(rowmod >= d, jnp.sum(e, axis=1, keepdims=True), 0.0)
                acc = acc + sc * pltpu.roll(vh, d, axis=0)
            od_ref[:, cs] = acc

        for m in range(n_sub):
            rs = slice(m * HG_SUB, (m + 1) * HG_SUB)
            for h in range(HG_HEADS):
                cs = slice(h * HG_DK, (h + 1) * HG_DK)
                st = st_ref[h]
                hist_ref[m, h] = st
                o_int = _dot_nt(qt_ref[rs, cs], st.astype(BF16))
                bs = b_ref[rs, cs]
                bl = bs[HG_SUB - 1:HG_SUB, :]
                kt = (kk_ref[rs, cs] * jnp.exp(bl - bs)).astype(BF16)
                st_ref[h] = st * jnp.exp(bl) + _dot_tn(hi_ref[rs, cs].astype(BF16), kt)
                o_ref[rs, cs] = o_int + od_ref[rs, cs]

    col = lambda c: pl.BlockSpec((tb, HG_WIDTH), lambda i, c=c: (i, c))
    return pl.pallas_call(
        body, grid=(T // tb,), name="hg_fwd",
        in_specs=[col(0), col(1), col(2), _full_spec((2, HG_WIDTH))],
        out_specs=[pl.BlockSpec((tb, HG_WIDTH), lambda i: (i, 0)),
                   pl.BlockSpec((n_sub, HG_HEADS, HG_DK, HG_DK), lambda i: (i, 0, 0, 0))],
        out_shape=[jax.ShapeDtypeStruct((T, HG_WIDTH), F32),
                   jax.ShapeDtypeStruct((T // HG_SUB, HG_HEADS, HG_DK, HG_DK), F32)],
        scratch_shapes=[pltpu.VMEM((HG_HEADS, HG_DK, HG_DK), F32), pltpu.VMEM((tb, HG_WIDTH), BF16),
                        pltpu.VMEM((tb, HG_WIDTH), F32), pltpu.VMEM((tb, HG_WIDTH), F32),
                        pltpu.VMEM((tb, HG_WIDTH), F32)],
        compiler_params=_params(("arbitrary",)),
    )(proj, proj, proj, gam)


def _hg_bwd(proj, gam, d_o, hist):
    T = proj.shape[0]
    tb = HG_TILE
    n_sub = tb // HG_SUB
    n_t = T // tb

    def body(hq_ref, hf_ref, hi_ref, gam_ref, do_ref, hist_ref, dout_ref, dlb_ref,
             dst_ref, q_ref, kk_ref, b_ref, dq_ref, dkk_ref, db_ref, dv_ref):
        i = pl.program_id(0)

        @pl.when(i == 0)
        def _():
            dst_ref[...] = jnp.zeros_like(dst_ref)
            dlb_ref[...] = jnp.zeros_like(dlb_ref)

        rowmod_w = lax.broadcasted_iota(jnp.int32, (tb, HG_WIDTH), 0) % HG_SUB
        rowmod = lax.broadcasted_iota(jnp.int32, (tb, 1), 0) % HG_SUB
        last_row = lax.broadcasted_iota(jnp.int32, (HG_SUB, 1), 0) == HG_SUB - 1
        hq = hq_ref[...]
        q, sq, s, f, kk, lf, lb = _hg_gates(hq, hf_ref[...], gam_ref[...])
        b = _prefix_in_sub(lf, rowmod_w)
        q_ref[...] = q
        kk_ref[...] = kk
        b_ref[...] = b

        for m in reversed(range(n_sub)):
            rs = slice(m * HG_SUB, (m + 1) * HG_SUB)
            for h in range(HG_HEADS):
                cs = slice(h * HG_DK, (h + 1) * HG_DK)
                st = hist_ref[m, h]
                dst = dst_ref[h]
                dstb = dst.astype(BF16)
                dos = do_ref[rs, cs].astype(BF16)
                vs = hi_ref[rs, cs].astype(BF16)
                bs = b_ref[rs, cs]
                bl = bs[HG_SUB - 1:HG_SUB, :]
                ebl = jnp.exp(bl)
                qt = q_ref[rs, cs] * jnp.exp(bs)
                decay = jnp.exp(bl - bs)
                kt = kk_ref[rs, cs] * decay
                dqt = _dot(dos, st.astype(BF16))
                dkt = _dot(vs, dstb)
                dv_ref[rs, cs] = _dot_nt(kt.astype(BF16), dstb)
                dst_ref[h] = dst * ebl + _dot_tn(dos, qt.astype(BF16))
                dq_ref[rs, cs] = dqt * jnp.exp(bs)
                dkk_ref[rs, cs] = dkt * decay
                dktk = dkt * kt
                dbl = _colsum(dktk) + _colsum(dst * st) * ebl
                db_ref[rs, cs] = dqt * qt - dktk + jnp.where(last_row, dbl, 0.0)

        for h in range(HG_HEADS):
            cs = slice(h * HG_DK, (h + 1) * HG_DK)
            qh = q[:, cs]
            kh = kk[:, cs]
            bh = b[:, cs]
            vh = hi_ref[:, cs]
            doh = do_ref[:, cs]
            dsc = jnp.sum(doh * vh, axis=1, keepdims=True)
            sc = jnp.sum(qh * kh, axis=1, keepdims=True)
            dqh = dsc * kh
            dkh = dsc * qh
            dvh = sc * doh
            dbh = jnp.zeros_like(qh)
            for d in range(1, HG_SUB):
                back = tb - d
                valid = rowmod >= d
                kd = pltpu.roll(kh, d, axis=0)
                ex = jnp.where(valid, jnp.exp(bh - pltpu.roll(bh, d, axis=0)), 0.0)
                pd = kd * ex
                sc = jnp.sum(qh * pd, axis=1, keepdims=True)
                dsc = jnp.where(valid, jnp.sum(doh * pltpu.roll(vh, d, axis=0), axis=1, keepdims=True), 0.0)
                dqh = dqh + dsc * pd
                w = dsc * qh * pd
                dkh = dkh + pltpu.roll(dsc * qh * ex, back, axis=0)
                dvh = dvh + pltpu.roll(sc * doh, back, axis=0)
                dbh = dbh + w - pltpu.roll(w, back, axis=0)
            dq_ref[:, cs] += dqh
            dkk_ref[:, cs] += dkh
            dv_ref[:, cs] += dvh
            db_ref[:, cs] += dbh

        dlf = _suffix_in_sub(db_ref[...], rowmod_w)
        df = dlf / f - dkk_ref[...]
        dout_ref[:, 0:HG_WIDTH] = dq_ref[...] * (sq * (1.0 + hq * (1.0 - sq)))
        dout_ref[:, HG_WIDTH:2 * HG_WIDTH] = df * (1.0 - lb) * (s * (1.0 - s))
        dout_ref[:, 2 * HG_WIDTH:3 * HG_WIDTH] = dv_ref[...]
        dlb_ref[...] += _colsum(df * (1.0 - s))

    col = lambda c: pl.BlockSpec((tb, HG_WIDTH), lambda i, c=c: (n_t - 1 - i, c))
    return pl.pallas_call(
        body, grid=(n_t,), name="hg_bwd",
        in_specs=[col(0), col(1), col(2), _full_spec((2, HG_WIDTH)),
                  pl.BlockSpec((tb, HG_WIDTH), lambda i: (n_t - 1 - i, 0)),
                  pl.BlockSpec((n_sub, HG_HEADS, HG_DK, HG_DK), lambda i: (n_t - 1 - i, 0, 0, 0))],
        out_specs=[pl.BlockSpec((tb, 3 * HG_WIDTH), lambda i: (n_t - 1 - i, 0)), _full_spec((1, HG_WIDTH))],
        out_shape=[jax.ShapeDtypeStruct((T, 3 * HG_WIDTH), F32), jax.ShapeDtypeStruct((1, HG_WIDTH), F32)],
        scratch_shapes=[pltpu.VMEM((HG_HEADS, HG_DK, HG_DK), F32)] + [pltpu.VMEM((tb, HG_WIDTH), F32)] * 7,
        compiler_params=_params(("arbitrary",)),
    )(proj, proj, proj, gam, d_o, hist)


def _sb_masks():
    row = lax.broadcasted_iota(jnp.int32, (SB_TK, SB_TK), 0)
    col = lax.broadcasted_iota(jnp.int32, (SB_TK, SB_TK), 1)
    suffix = (row >= col).astype(BF16)
    prefix = (row <= col).astype(BF16)
    query = lax.broadcasted_iota(jnp.int32, (SB_TQ, SB_TK), 0)
    key = lax.broadcasted_iota(jnp.int32, (SB_TQ, SB_TK), 1)
    causal = [key + r * SB_TK < query for r in range(SB_TQ // SB_TK)]
    lane = lax.broadcasted_iota(jnp.int32, (SB_TQ, SB_BLOCK), 1)
    return suffix, prefix, causal, lane


def _sum_right(a, suffix):
    return _dot(a.astype(BF16), suffix)


def _sb_block(qb, kj, suffix, causal, c):
    z = _dot_nt(qb, kj)
    sp = jnp.maximum(z, 0.0) + jnp.log(1.0 + jnp.exp(-jnp.abs(z)))
    sig = jnp.exp(z - sp)
    if causal is not None:
        sp = jnp.where(causal, sp, 0.0)
    big_l = _sum_right(sp, suffix)
    a = jnp.exp(z - big_l - c)
    if causal is not None:
        a = jnp.where(causal, a, 0.0)
    return sig, a, big_l


def _sb_tile_copies(to_hbm, bufs, stores, sems, slot, head, i, j, n_chain):
    copies = []
    for ch in range(n_chain):
        for w in range(2):
            vmem, hbm = bufs[w].at[slot, ch], stores[w].at[head + ch, i, j]
            src, dst = (vmem, hbm) if to_hbm else (hbm, vmem)
            copies.append(pltpu.make_async_copy(src, dst, sems.at[slot, ch, w]))
    return copies


def _half_masked(pair, lane, scale=1.0):
    pair = pair.astype(F32) * scale
    return jnp.where(lane < SB_DH, pair, 0.0).astype(BF16), jnp.where(lane < SB_DH, 0.0, pair).astype(BF16)


def _sb_fwd(qkv):
    T = qkv.shape[0]
    width = SB_PP * SB_BLOCK
    n_chain = 2 * SB_PP
    n_sub = SB_TQ // SB_TK

    def body(q_ref, k_ref, v_ref, o_ref, a_st, s_st, acc_ref, c_ref, qm, a_buf, s_buf, sems):
        g = pl.program_id(0)
        i = pl.program_id(1)
        suffix, _, causal, lane = _sb_masks()
        for pp in range(SB_PP):
            qm[2 * pp], qm[2 * pp + 1] = _half_masked(q_ref[:, pp * SB_BLOCK:(pp + 1) * SB_BLOCK], lane, SB_DH ** -0.5)
        acc_ref[...] = jnp.zeros_like(acc_ref)
        c_ref[...] = jnp.zeros_like(c_ref)
        a_buf[...] = jnp.zeros_like(a_buf)
        s_buf[...] = jnp.zeros_like(s_buf)

        def copies(slot, j):
            return _sb_tile_copies(True, (a_buf, s_buf), (a_st, s_st), sems, slot, g * n_chain, i, j, n_chain)

        def step(j, slot, r=None, reuse=True):
            rows = pl.ds(pl.multiple_of(j * SB_TK, SB_TK), SB_TK)
            qs = slice(0 if r is None else r * SB_TK, SB_TQ)
            mask = None if r is None else causal[r][qs]
            if reuse:
                for cp in copies(slot, j):
                    cp.wait()
            for ch in range(n_chain):
                pair = slice((ch // 2) * SB_BLOCK, (ch // 2 + 1) * SB_BLOCK)
                c = c_ref[ch, qs]
                sig, a, big_l = _sb_block(qm[ch, qs], k_ref[rows, pair], suffix, mask, c)
                ab = a.astype(BF16)
                acc_ref[ch, qs] += _dot(ab, v_ref[rows, pair])
                a_buf[slot, ch, qs] = ab
                s_buf[slot, ch, qs] = sig.astype(BF16)
                c_ref[ch, qs] = c + jnp.broadcast_to(big_l[:, 0:1], c.shape)
            for cp in copies(slot, j):
                cp.start()

        first = i * n_sub
        for s, r in enumerate(reversed(range(n_sub))):
            step(first + r, s % 2, r, reuse=s >= 2)

        @pl.loop(0, first)
        def _(t):
            step(first - 1 - t, (n_sub + t) % 2)

        for slot in range(2):
            for cp in copies(slot, 0):
                cp.wait()
        for pp in range(SB_PP):
            o_ref[:, pp * SB_BLOCK:(pp + 1) * SB_BLOCK] = jnp.where(lane < SB_DH, acc_ref[2 * pp], acc_ref[2 * pp + 1])

    assert n_sub >= 2 and n_sub % 2 == 0
    n_g = SB_WIDTH // width
    blk = lambda part: pl.BlockSpec((SB_TQ, width), lambda g, i, part=part: (i, part * n_g + g))
    whole = lambda part: pl.BlockSpec((T, width), lambda g, i, part=part: (0, part * n_g + g))
    tiles = jax.ShapeDtypeStruct((SB_HEADS, T // SB_TQ, T // SB_TK, SB_TQ, SB_TK), BF16)
    return pl.pallas_call(
        body, grid=(n_g, T // SB_TQ), name="sb_fwd",
        in_specs=[blk(0), whole(1), whole(2)],
        out_specs=[blk(0), pl.BlockSpec(memory_space=pl.ANY), pl.BlockSpec(memory_space=pl.ANY)],
        out_shape=[jax.ShapeDtypeStruct((T, SB_WIDTH), F32), tiles, tiles],
        scratch_shapes=[pltpu.VMEM((n_chain, SB_TQ, SB_BLOCK), F32), pltpu.VMEM((n_chain, SB_TQ, SB_TK), F32),
                        pltpu.VMEM((n_chain, SB_TQ, SB_BLOCK), BF16),
                        pltpu.VMEM((2, n_chain, SB_TQ, SB_TK), BF16), pltpu.VMEM((2, n_chain, SB_TQ, SB_TK), BF16),
                        pltpu.SemaphoreType.DMA((2, n_chain, 2))],
        compiler_params=_params(("parallel", "arbitrary")),
    )(qkv, qkv, qkv)


def _sb_bwd(qkv, a_st, s_st, d_o, after):
    T = qkv.shape[0]
    width = SB_PP * SB_BLOCK
    n_chain = 2 * SB_PP
    n_sub = SB_TQ // SB_TK
    scale = SB_DH ** -0.5

    def body(q_ref, k_ref, v_ref, a_st_ref, s_st_ref, do_ref, after_ref, dq_ref, dk_ref, dv_ref, acc_ref, gc_ref, qm, dom,
             a_buf, s_buf, sems):
        g_idx = pl.program_id(0)
        i = pl.program_id(1)

        @pl.when(i == 0)
        def _():
            dk_ref[...] = jnp.zeros_like(dk_ref)
            dv_ref[...] = jnp.zeros_like(dv_ref)

        _, prefix, causal, lane = _sb_masks()
        for pp in range(SB_PP):
            pair = slice(pp * SB_BLOCK, (pp + 1) * SB_BLOCK)
            qm[2 * pp], qm[2 * pp + 1] = _half_masked(q_ref[:, pair], lane, scale)
            dom[2 * pp], dom[2 * pp + 1] = _half_masked(do_ref[:, pair], lane)
        acc_ref[...] = jnp.zeros_like(acc_ref)
        gc_ref[...] = jnp.zeros_like(gc_ref)

        def copies(slot, j):
            return _sb_tile_copies(False, (a_buf, s_buf), (a_st_ref, s_st_ref), sems, slot, g_idx * n_chain, i, j, n_chain)

        def step(j, slot, r=None, last=False):
            rows = pl.ds(pl.multiple_of(j * SB_TK, SB_TK), SB_TK)
            qs = slice(0 if r is None else r * SB_TK, SB_TQ)
            mask = None if r is None else causal[r][qs]
            for cp in copies(slot, j):
                cp.wait()
            if not last:
                for cp in copies(1 - slot, j + 1):
                    cp.start()
            for pp in range(SB_PP):
                pair = slice(pp * SB_BLOCK, (pp + 1) * SB_BLOCK)
                kj = k_ref[rows, pair]
                vj = v_ref[rows, pair]
                dk = jnp.zeros((SB_TK, SB_BLOCK), F32)
                dv = jnp.zeros((SB_TK, SB_BLOCK), F32)
                for ch in (2 * pp, 2 * pp + 1):
                    ab = a_buf[slot, ch, qs]
                    g = ab.astype(F32) * _dot_nt(dom[ch, qs], vj)
                    g_left = _sum_right(g, prefix)
                    gc = gc_ref[ch, qs]
                    dz = g - s_buf[slot, ch, qs].astype(F32) * (gc + g_left)
                    if mask is not None:
                        dz = jnp.where(mask, dz, 0.0)
                    dzb = dz.astype(BF16)
                    dk = dk + _dot_tn(dzb, qm[ch, qs])
                    dv = dv + _dot_tn(ab, dom[ch, qs])
                    acc_ref[ch, qs] += _dot(dzb, kj)
                    gc_ref[ch, qs] = gc + jnp.broadcast_to(g_left[:, SB_TK - 1:SB_TK], gc.shape)
                dk_ref[rows, pair] += dk
                dv_ref[rows, pair] += dv

        first = i * n_sub
        for cp in copies(0, 0):
            cp.start()

        @pl.loop(0, first)
        def _(j):
            step(j, j % 2)

        for r in range(n_sub):
            step(first + r, r % 2, r, last=r == n_sub - 1)
        for pp in range(SB_PP):
            dq_ref[:, pp * SB_BLOCK:(pp + 1) * SB_BLOCK] = scale * jnp.where(lane < SB_DH, acc_ref[2 * pp],
                                                                             acc_ref[2 * pp + 1])
    n_g = SB_WIDTH // width
    blk = lambda part: pl.BlockSpec((SB_TQ, width), lambda g, i, part=part: (i, part * n_g + g))
    whole = lambda part: pl.BlockSpec((T, width), lambda g, i, part=part: (0, part * n_g + g))
    return pl.pallas_call(
        body, grid=(n_g, T // SB_TQ), name="sb_bwd",
        in_specs=[blk(0), whole(1), whole(2), pl.BlockSpec(memory_space=pl.ANY), pl.BlockSpec(memory_space=pl.ANY), blk(0),
                  pl.BlockSpec(memory_space=pl.ANY)],
        out_specs=[blk(0), whole(0), whole(0)],
        out_shape=[jax.ShapeDtypeStruct((T, SB_WIDTH), F32)] * 3,
        scratch_shapes=[pltpu.VMEM((n_chain, SB_TQ, SB_BLOCK), F32), pltpu.VMEM((n_chain, SB_TQ, SB_TK), F32),
                        pltpu.VMEM((n_chain, SB_TQ, SB_BLOCK), BF16), pltpu.VMEM((n_chain, SB_TQ, SB_BLOCK), BF16),
                        pltpu.VMEM((2, n_chain, SB_TQ, SB_TK), BF16), pltpu.VMEM((2, n_chain, SB_TQ, SB_TK), BF16),
                        pltpu.SemaphoreType.DMA((2, n_chain, 2))],
        compiler_params=_params(("parallel", "arbitrary")),
    )(qkv, qkv, qkv, a_st, s_st, d_o, after)


HBM = pl.BlockSpec(memory_space=pltpu.HBM)
MESH = pl.DeviceIdType.MESH


def _place():
    return lax.axis_index("x"), lax.axis_index("y"), lax.axis_index("c")


def _all_gather(block):
    rows, cols = block.shape

    def body(x_ref, out_ref, send_sems, recv_sems, local_sem):
        x, y, c = _place()
        me, sibling = (x, y, c), (x, y, 1 - c)
        chips = [(1 - x, y), (x, 1 - y), (1 - x, 1 - y)]

        def slot(px, py, pc):
            return out_ref.at[4 * px + 2 * py + pc]

        def copy(k, blk, to, src=None):
            return pltpu.make_async_remote_copy(
                src_ref=slot(*blk) if src is None else src, dst_ref=slot(*blk),
                send_sem=send_sems.at[k], recv_sem=recv_sems.at[k], device_id=to, device_id_type=MESH)

        mine = pltpu.make_async_copy(x_ref, slot(*me), local_sem)
        mine.start()
        first = [copy(0, me, sibling, src=x_ref)]
        first += [copy(1 + j, me, (*chip, c), src=x_ref) for j, chip in enumerate(chips)]
        for cp in first:
            cp.start()
        passed = [copy(4 + j, (*chip, c), sibling) for j, chip in enumerate(chips)]
        for j, chip in enumerate(chips):
            copy(1 + j, (*chip, c), me).wait_recv()
            passed[j].start()
        copy(0, sibling, me).wait_recv()
        for j, chip in enumerate(chips):
            copy(4 + j, (*chip, 1 - c), me).wait_recv()
        for cp in first + passed:
            cp.wait_send()
        mine.wait()

    return pl.pallas_call(
        body, name="all_gather",
        out_shape=jax.ShapeDtypeStruct((N_DEV, rows, cols), block.dtype),
        in_specs=[HBM], out_specs=HBM,
        scratch_shapes=[pltpu.SemaphoreType.DMA((7,)), pltpu.SemaphoreType.DMA((7,)), pltpu.SemaphoreType.DMA],
    )(block)


SEM = pl.BlockSpec(memory_space=pltpu.SEMAPHORE)
ANY = pl.BlockSpec(memory_space=pl.ANY)
SPLIT_EFFECT = pltpu.SideEffectType.DATAFLOW_SIDE_EFFECTING
N_PEER = N_DEV - 1


def _flow_copies(kind, src_ref, land_ref, send_sems, recv_sems):
    x, y, c = _place()
    copies = []
    for r in range(1, N_DEV):
        px = 1 - x if r & 4 else x
        py = 1 - y if r & 2 else y
        pc = 1 - c if r & 1 else c
        if kind == "gather":
            src, dst = src_ref, land_ref.at[4 * x + 2 * y + c]
        else:
            src, dst = src_ref.at[4 * px + 2 * py + pc], land_ref.at[r - 1]
        copies.append(pltpu.make_async_remote_copy(
            src_ref=src, dst_ref=dst, send_sem=send_sems.at[r - 1], recv_sem=recv_sems.at[r - 1],
            device_id=(px, py, pc), device_id_type=MESH))
    return copies


def _landing(kind, src):
    if kind == "gather":
        return lax.empty((N_DEV,) + src.shape, src.dtype)
    return lax.empty((N_PEER,) + src.shape[1:], src.dtype)


def _push_start(kinds, arrays, after, name):
    n = len(arrays)

    def body(*refs):
        ins, sems, token = refs[:n], refs[n + 1:2 * n + 1], refs[3 * n + 1]
        for f, kind in enumerate(kinds):
            for cp in _flow_copies(kind, ins[2 * f], ins[2 * f + 1], sems[2 * f], sems[2 * f + 1]):
                cp.start()
        token[...] = jnp.zeros_like(token)

    outs = pl.pallas_call(
        body, name=name,
        out_shape=[pltpu.SemaphoreType.DMA((N_PEER,))] * n + [pltpu.HBM(a.shape, a.dtype) for a in arrays]
        + [jax.ShapeDtypeStruct((8, 128), F32)],
        in_specs=[HBM] * n + [ANY], out_specs=[SEM] * n + [HBM] * n + [pl.BlockSpec(memory_space=pltpu.VMEM)],
        input_output_aliases={i: n + i for i in range(n)},
        compiler_params=pltpu.CompilerParams(has_side_effects=SPLIT_EFFECT),
    )(*[pltpu.with_memory_space_constraint(a, pltpu.HBM) for a in arrays], after)
    return outs[:n], outs[n:2 * n], outs[2 * n]


def _push_wait(kinds, sems, arrays, after, name):
    n = len(arrays)

    def body(*refs):
        ins, sem_refs = refs[:n], refs[n:2 * n]
        for f, kind in enumerate(kinds):
            for cp in _flow_copies(kind, ins[2 * f], ins[2 * f + 1], sem_refs[2 * f], sem_refs[2 * f + 1]):
                cp.wait_send()
                cp.wait_recv()

    outs = pl.pallas_call(
        body, name=name,
        out_shape=[pltpu.HBM(a.shape, a.dtype) for a in arrays],
        in_specs=[HBM] * n + [SEM] * n + [ANY], out_specs=[HBM] * n,
        input_output_aliases={i: i for i in range(n)},
        compiler_params=pltpu.CompilerParams(has_side_effects=SPLIT_EFFECT),
    )(*arrays, *sems, after)
    return outs


def _sum_blocks(own, got, name):
    n, rows, cols = got.shape
    tr = next(t for t in (128, 64, 32) if rows % t == 0)

    def body(own_ref, got_ref, out_ref):
        acc = own_ref[...].astype(F32)
        for r in range(n):
            acc = acc + got_ref[r].astype(F32)
        out_ref[...] = acc

    return pl.pallas_call(
        body, grid=(rows // tr,), name=name,
        in_specs=[pl.BlockSpec((tr, cols), lambda i: (i, 0)), pl.BlockSpec((n, tr, cols), lambda i: (0, i, 0))],
        out_specs=pl.BlockSpec((tr, cols), lambda i: (i, 0)),
        out_shape=jax.ShapeDtypeStruct((rows, cols), F32),
        compiler_params=_params(("parallel",)),
    )(own, got)


def _adamw_math(w, g, m, v):
    m = ADAM_B1 * m + (1.0 - ADAM_B1) * g
    v = ADAM_B2 * v + (1.0 - ADAM_B2) * (g * g)
    m_hat = m / (1.0 - ADAM_B1 ** ADAM_STEP)
    v_hat = v / (1.0 - ADAM_B2 ** ADAM_STEP)
    delta = -ADAM_LR * (m_hat / (jnp.sqrt(v_hat) + ADAM_EPS) + ADAM_WD * w)
    return delta, m, v


def _adamw(w, g, m, v, name):
    rows, cols = w.shape
    tr = rows if rows <= 352 else 256

    def body(w_ref, g_ref, m_ref, v_ref, d_out, m_out, v_out):
        d_out[...], m_out[...], v_out[...] = _adamw_math(w_ref[...], g_ref[...], m_ref[...], v_ref[...])

    spec = pl.BlockSpec((tr, cols), lambda i: (i, 0))
    return pl.pallas_call(
        body, grid=(rows // tr,), name=name,
        in_specs=[spec] * 4, out_specs=[spec] * 3,
        out_shape=[jax.ShapeDtypeStruct((rows, cols), F32)] * 3,
        compiler_params=_params(("parallel",)),
    )(w, g, m, v)


ROW_ATTN_PRE, ROW_ATTN_POST, ROW_FFN_PRE, ROW_FFN_POST, ROW_OUT_NORMS, ROW_GAMMA, ROW_LOSS = range(7)


def _small_update(smalls, w, m, v):
    def body(s_ref, w_ref, m_ref, v_ref, g_out, d_out, m_out, v_out):
        g = s_ref[0]
        for j in range(1, N_DEV):
            g = g + s_ref[j]
        wv = w_ref[...]
        g0 = wv[ROW_GAMMA:ROW_GAMMA + 1, 0:HG_WIDTH]
        g1 = wv[ROW_GAMMA:ROW_GAMMA + 1, HG_WIDTH:]
        mx = jnp.maximum(g0, g1)
        e0 = jnp.exp(g0 - mx)
        e1 = jnp.exp(g1 - mx)
        lb = e0 / (e0 + e1)
        dg0 = g[ROW_GAMMA:ROW_GAMMA + 1, 0:HG_WIDTH] * lb * (1.0 - lb)
        row = lax.broadcasted_iota(jnp.int32, g.shape, 0)
        g = jnp.where(row == ROW_GAMMA, jnp.concatenate([dg0, -dg0], axis=1), g)
        g_out[...] = g
        d_out[...], m_out[...], v_out[...] = _adamw_math(wv, g, m_ref[...], v_ref[...])

    return pl.pallas_call(
        body, name="small_update",
        out_shape=[jax.ShapeDtypeStruct((8, D_MODEL), F32)] * 4,
    )(smalls, w, m, v)


def _local_step(x, p, target, g_attn_pre, w_in_t, gamma, g_hg, g_sb, g_attn_post, g_ffn_pre, g_ffn_post, later_weights,
                grads_ready):
    proj, u, qkv = _in_proj(x, g_attn_pre, w_in_t)
    o_hg, hist = _hg_fwd(proj, gamma)
    o_sb, a_st, s_st = _sb_fwd(qkv)
    w_out, w_gu_t, w_down, w_pp_t, w_pg = later_weights(o_sb)
    cat, mix, h1 = _out_proj(o_hg, proj, o_sb, x, g_hg, g_sb, g_attn_post, w_out)
    u2, gu, y, h2 = _ffn_fwd(h1, g_ffn_pre, w_gu_t, w_down, g_ffn_post)
    dh2, de, dg, loss = _ple_loss(h2, p, target, w_pp_t, w_pg)

    d_wpp_t = _wgrad(de, p, "wgrad_ple_proj")
    d_wpg = _wgrad(h2, dg, "wgrad_ple_gate")
    dy, a, dgu, dh1, dg_ffn_post, dg_ffn_pre = _ffn_bwd(dh2, y, h1, gu, g_ffn_post, g_ffn_pre, w_gu_t, w_down)
    d_wdown = _wgrad(a, dy, "wgrad_down")
    sent = grads_ready("ffn", (_wgrad(dgu, u2, "wgrad_gate_up"), d_wdown, d_wpp_t, d_wpg))
    dmix, d_ohg, d_hg, d_osb, dg_attn_post, dg_hg, dg_sb = _out_bwd(dh1, mix, o_hg, proj, o_sb, g_hg, g_sb, g_attn_post,
                                                                    w_out, sent)
    sent = grads_ready("out", (_wgrad(cat, dmix, "wgrad_out"),))
    dq, dk, dv = _sb_bwd(qkv, a_st, s_st, d_osb, sent)
    d_hgrn, d_lb = _hg_bwd(proj, gamma, d_ohg, hist)
    dproj, dx, dg_attn_pre = _in_bwd(d_hgrn, d_hg, dq, dk, dv, x, dh1, g_attn_pre, w_in_t)
    grads_ready("in", (_wgrad(dproj, u, "wgrad_in"),))
    return loss, dx, (dg_attn_pre, dg_attn_post, dg_ffn_pre, dg_ffn_post, dg_hg, dg_sb, d_lb)


def _pack_row(*parts):
    return jnp.concatenate([q.reshape(1, -1) for q in parts], axis=1)


def kernel(x, p, attn_pre_norm, w_in, hg_lower_gamma, hg_out_norm, sb_out_norm, w_out, attn_post_norm, ffn_pre_norm, w_gate_up, w_down, ffn_post_norm, ple_proj, ple_gate, loss_target, m_attn_pre_norm, m_w_in, m_hg_lower_gamma, m_hg_out_norm, m_sb_out_norm, m_w_out, m_attn_post_norm, m_ffn_pre_norm, m_w_gate_up, m_w_down, m_ffn_post_norm, m_ple_proj, m_ple_gate, v_attn_pre_norm, v_w_in, v_hg_lower_gamma, v_hg_out_norm, v_sb_out_norm, v_w_out, v_attn_post_norm, v_ffn_pre_norm, v_w_gate_up, v_w_down, v_ffn_post_norm, v_ple_proj, v_ple_gate):
    big = (w_in[0], w_out[0], w_gate_up[0], w_down[0], ple_proj[0], ple_gate[0])
    big_m = (m_w_in[0], m_w_out[0], m_w_gate_up[0], m_w_down[0], m_ple_proj[0], m_ple_gate[0])
    big_v = (v_w_in[0], v_w_out[0], v_w_gate_up[0], v_w_down[0], v_ple_proj[0], v_ple_gate[0])
    names = ("w_in", "w_out", "w_gate_up", "w_down", "ple_proj", "ple_gate")

    by_columns = (True, False, True, False, True, False)
    shards = [(w.T if t else w).astype(BF16) for w, t in zip(big, by_columns)]
    w_in_t = _all_gather(shards[0]).reshape(IN_COLS, D_MODEL)
    me = 4 * lax.axis_index("x") + 2 * lax.axis_index("y") + lax.axis_index("c")
    later = shards[1:]
    w_sems, w_thru, token = _push_start(
        ("gather",) * len(later), [a for s in later for a in (s, _landing("gather", s))], w_in_t, "weights_start")

    def later_weights(after):
        done = _push_wait(("gather",) * len(later), w_sems, w_thru, after, "weights_wait")
        whole = [lax.dynamic_update_index_in_dim(got, mine, me, 0) for mine, got in zip(done[0::2], done[1::2])]
        return (whole[0].reshape(D_MODEL, D_MODEL), whole[1].reshape(2, D_FF, D_MODEL), whole[2].reshape(D_FF, D_MODEL),
                whole[3].reshape(D_MODEL, PLE_DIM), whole[4].reshape(D_MODEL, D_MODEL))

    flights = {}

    def by_owner(g):
        return g.reshape(N_DEV, -1, g.shape[-1])

    def grads_ready(group, grads):
        blocks = [by_owner(g) for g in grads]
        if group == "in":
            flights[group] = blocks[0]
            return None
        sems, thru, sent = _push_start(
            ("scatter",) * len(blocks), [a for b in blocks for a in (b, _landing("scatter", b))], token,
            "grads_" + group + "_start")
        flights[group] = (sems, thru)
        return sent

    loss, dx, smalls = _local_step(
        x[0], p[0, 0], loss_target[0], attn_pre_norm + token[0:1, 0:1], w_in_t,
        hg_lower_gamma, hg_out_norm, sb_out_norm, attn_post_norm, ffn_pre_norm, ffn_post_norm, later_weights, grads_ready)
    dg_attn_pre, dg_attn_post, dg_ffn_pre, dg_ffn_post, dg_hg, dg_sb, d_lb = smalls

    zeros_half = jnp.zeros((1, HG_WIDTH), F32)
    small_pack = jnp.concatenate([
        dg_attn_pre, dg_attn_post, dg_ffn_pre, dg_ffn_post, _pack_row(dg_hg, dg_sb), _pack_row(d_lb, zeros_half),
        jnp.broadcast_to(loss[0:1, 0:1], (1, D_MODEL)), jnp.zeros((1, D_MODEL), F32)], axis=0)
    in_blocks = flights["in"]
    in_sems, in_thru, in_token = _push_start(
        ("scatter", "gather"),
        (in_blocks, _landing("scatter", in_blocks), small_pack, _landing("gather", small_pack)), token,
        "grads_in_start")
    ffn = _push_wait(("scatter",) * 4, *flights["ffn"], in_token, "grads_ffn_wait")
    sent_out, got_out = _push_wait(("scatter",), *flights["out"], ffn[1], "grads_out_wait")
    sent_in, got_in, my_small, all_smalls = _push_wait(("scatter", "gather"), in_sems, in_thru, got_out, "grads_in_wait")
    all_smalls = lax.dynamic_update_index_in_dim(all_smalls, my_small, me, 0)

    def reduced(sent, got, name):
        return _sum_blocks(lax.dynamic_index_in_dim(sent, me, 0, keepdims=False), got, "grad_sum_" + name)

    sums = (reduced(sent_in, got_in, "w_in"), reduced(sent_out, got_out, "w_out"), reduced(ffn[0], ffn[1], "w_gate_up"),
            reduced(ffn[2], ffn[3], "w_down"), reduced(ffn[4], ffn[5], "ple_proj"), reduced(ffn[6], ffn[7], "ple_gate"))

    out_g, out_d, out_m, out_v = {}, {}, {}, {}
    for i, name in enumerate(names):
        g = sums[i].T if by_columns[i] else sums[i]
        out_g[name] = g[None]
        d, m, v = _adamw(big[i], g, big_m[i], big_v[i], "adamw_" + name)
        out_d[name], out_m[name], out_v[name] = d[None], m[None], v[None]

    def small_rows(pre, gam, hg, sb, post, fpre, fpost):
        return jnp.concatenate([pre, post, fpre, fpost, _pack_row(hg, sb), _pack_row(gam[0], gam[1]),
                                jnp.zeros((2, D_MODEL), F32)], axis=0)

    packs = _small_update(
        all_smalls,
        small_rows(attn_pre_norm, hg_lower_gamma, hg_out_norm, sb_out_norm, attn_post_norm, ffn_pre_norm, ffn_post_norm),
        small_rows(m_attn_pre_norm, m_hg_lower_gamma, m_hg_out_norm, m_sb_out_norm, m_attn_post_norm, m_ffn_pre_norm,
                   m_ffn_post_norm),
        small_rows(v_attn_pre_norm, v_hg_lower_gamma, v_hg_out_norm, v_sb_out_norm, v_attn_post_norm, v_ffn_pre_norm,
                   v_ffn_post_norm))

    def unpack(pk):
        return {
            "attn_pre_norm": pk[ROW_ATTN_PRE:ROW_ATTN_PRE + 1],
            "hg_lower_gamma": pk[ROW_GAMMA].reshape(2, HG_WIDTH),
            "hg_out_norm": pk[ROW_OUT_NORMS:ROW_OUT_NORMS + 1, :HG_WIDTH],
            "sb_out_norm": pk[ROW_OUT_NORMS:ROW_OUT_NORMS + 1, HG_WIDTH:],
            "attn_post_norm": pk[ROW_ATTN_POST:ROW_ATTN_POST + 1],
            "ffn_pre_norm": pk[ROW_FFN_PRE:ROW_FFN_PRE + 1],
            "ffn_post_norm": pk[ROW_FFN_POST:ROW_FFN_POST + 1],
        }

    sg, sd, sm, sv = (unpack(pk) for pk in packs)
    out_g.update(sg), out_d.update(sd), out_m.update(sm), out_v.update(sv)
    order = ("attn_pre_norm", "w_in", "hg_lower_gamma", "hg_out_norm", "sb_out_norm", "w_out", "attn_post_norm",
             "ffn_pre_norm", "w_gate_up", "w_down", "ffn_post_norm", "ple_proj", "ple_gate")
    total_loss = packs[0][ROW_LOSS, 0]
    return (total_loss, dx[None], *[out_g[n] for n in order], *[out_d[n] for n in order],
            *[out_m[n] for n in order], *[out_v[n] for n in order])
```

```python
import functools

import jax
import jax.numpy as jnp
from jax import lax
from jax.experimental import pallas as pl
from jax.experimental.pallas import tpu as pltpu

F32 = jnp.float32
BF16 = jnp.bfloat16

D_MODEL = 1024
HG_WIDTH = 512
HG_HEADS = 4
HG_DK = 128
SB_WIDTH = 512
SB_HEADS = 8
SB_DH = 64
SB_BLOCK = 128
SB_PP = 2
SB_TQ = 512
SB_TK = 256
D_FF = 2816
PLE_DIM = 256
IN_COLS = 4 * HG_WIDTH + 3 * SB_WIDTH
EPS = 1e-6
N_DEV = 8

HG_SUB = 16
HG_TILE = 128
FFN_TF = 1408
ROW_TILE = 256
VMEM_LIMIT = 56 * 1024 * 1024
WGRAD_ACC_BYTES = 8 * 1024 * 1024

ADAM_LR = 0.001
ADAM_B1 = 0.9
ADAM_B2 = 0.999
ADAM_EPS = 1e-08
ADAM_WD = 0.01
ADAM_STEP = 10

NT_DIMS = (((1,), (1,)), ((), ()))
TN_DIMS = (((0,), (0,)), ((), ()))


def _params(semantics):
    return pltpu.CompilerParams(dimension_semantics=semantics, vmem_limit_bytes=VMEM_LIMIT)


def _dot(a, b):
    return jnp.dot(a, b, preferred_element_type=F32)


def _dot_nt(a, b):
    return lax.dot_general(a, b, NT_DIMS, preferred_element_type=F32)


def _dot_tn(a, b):
    return lax.dot_general(a, b, TN_DIMS, preferred_element_type=F32)


def _sigmoid(z):
    return 1.0 / (1.0 + jnp.exp(-z))


def _rstd(xv):
    return lax.rsqrt(jnp.mean(xv * xv, axis=-1, keepdims=True) + EPS)


def _rms_bwd(xv, r, g, dn):
    xn = xv * r
    gh = dn * g
    dx = r * (gh - xn * jnp.mean(gh * xn, axis=-1, keepdims=True))
    return dx, dn * xn


def _colsum(a):
    return jnp.sum(a, axis=0, keepdims=True)


def _row_spec(tm, width, col=0):
    return pl.BlockSpec((tm, width), lambda i, col=col: (i, col))


def _full_spec(shape):
    return pl.BlockSpec(shape, lambda i: (0,) * len(shape))


def _in_proj(x, g, w_t):
    T = x.shape[0]
    tm = ROW_TILE

    def body(x_ref, g_ref, w_ref, proj_ref, u_ref, qkv_ref):
        xv = x_ref[...]
        u = (xv * _rstd(xv) * g_ref[...]).astype(BF16)
        u_ref[...] = u
        proj = _dot_nt(u, w_ref[...])
        proj_ref[...] = proj
        qkv_ref[...] = proj[:, 4 * HG_WIDTH:].astype(BF16)

    return pl.pallas_call(
        body, grid=(T // tm,), name="in_proj",
        in_specs=[_row_spec(tm, D_MODEL), _full_spec((1, D_MODEL)), _full_spec((IN_COLS, D_MODEL))],
        out_specs=[_row_spec(tm, IN_COLS), _row_spec(tm, D_MODEL), _row_spec(tm, 3 * SB_WIDTH)],
        out_shape=[jax.ShapeDtypeStruct((T, IN_COLS), F32), jax.ShapeDtypeStruct((T, D_MODEL), BF16),
                   jax.ShapeDtypeStruct((T, 3 * SB_WIDTH), BF16)],
        compiler_params=_params(("parallel",)),
    )(x, g, w_t)


def _out_proj(o_hg, proj, o_sb, x, g_hg, g_sb, g_post, w):
    T = x.shape[0]
    tm = ROW_TILE

    def body(ohg_ref, hg_ref, osb_ref, x_ref, ghg_ref, gsb_ref, gpost_ref, w_ref, cat_ref, mix_ref, h1_ref):
        ohg = ohg_ref[...]
        hg = hg_ref[...]
        osb = osb_ref[...]
        a = ohg * _rstd(ohg) * ghg_ref[...] * (hg * _sigmoid(hg))
        n2 = osb * _rstd(osb) * gsb_ref[...]
        cat = jnp.concatenate([a, n2], axis=1).astype(BF16)
        cat_ref[...] = cat
        mix = _dot(cat, w_ref[...])
        mix_ref[...] = mix
        h1_ref[...] = x_ref[...] + mix * _rstd(mix) * gpost_ref[...]

    return pl.pallas_call(
        body, grid=(T // tm,), name="out_proj",
        in_specs=[_row_spec(tm, HG_WIDTH), _row_spec(tm, HG_WIDTH, 3), _row_spec(tm, SB_WIDTH), _row_spec(tm, D_MODEL),
                  _full_spec((1, HG_WIDTH)), _full_spec((1, SB_WIDTH)), _full_spec((1, D_MODEL)),
                  _full_spec((D_MODEL, D_MODEL))],
        out_specs=[_row_spec(tm, D_MODEL)] * 3,
        out_shape=[jax.ShapeDtypeStruct((T, D_MODEL), BF16), jax.ShapeDtypeStruct((T, D_MODEL), F32),
                   jax.ShapeDtypeStruct((T, D_MODEL), F32)],
        compiler_params=_params(("parallel",)),
    )(o_hg, proj, o_sb, x, g_hg, g_sb, g_post, w)


def _ffn_fwd(h1, g_pre, w_gu_t, w_down, g_post):
    T = h1.shape[0]
    tm = 2 * ROW_TILE
    n_f = D_FF // FFN_TF

    def body(h1_ref, gpre_ref, wgu_ref, wd_ref, gpost_ref, u2_ref, gu_ref, y_ref, h2_ref, acc_ref):
        j = pl.program_id(1)

        @pl.when(j == 0)
        def _():
            hv = h1_ref[...]
            u2_ref[...] = (hv * _rstd(hv) * gpre_ref[...]).astype(BF16)
            acc_ref[...] = jnp.zeros_like(acc_ref)

        u2 = u2_ref[...]
        gate = _dot_nt(u2, wgu_ref[0])
        up = _dot_nt(u2, wgu_ref[1])
        gu_ref[0] = gate.astype(BF16)
        gu_ref[1] = up.astype(BF16)
        a = (gate * _sigmoid(gate) * up).astype(BF16)
        acc_ref[...] += _dot(a, wd_ref[...])

        @pl.when(j == n_f - 1)
        def _():
            y = acc_ref[...]
            y_ref[...] = y
            h2_ref[...] = h1_ref[...] + y * _rstd(y) * gpost_ref[...]

    row = lambda width: pl.BlockSpec((tm, width), lambda i, j: (i, 0))
    vec = pl.BlockSpec((1, D_MODEL), lambda i, j: (0, 0))
    return pl.pallas_call(
        body, grid=(T // tm, n_f), name="ffn_fwd",
        in_specs=[row(D_MODEL), vec,
                  pl.BlockSpec((2, FFN_TF, D_MODEL), lambda i, j: (0, j, 0)),
                  pl.BlockSpec((FFN_TF, D_MODEL), lambda i, j: (j, 0)), vec],
        out_specs=[row(D_MODEL),
                   pl.BlockSpec((2, tm, FFN_TF), lambda i, j: (0, i, j)),
                   row(D_MODEL), row(D_MODEL)],
        out_shape=[jax.ShapeDtypeStruct((T, D_MODEL), BF16), jax.ShapeDtypeStruct((2, T, D_FF), BF16),
                   jax.ShapeDtypeStruct((T, D_MODEL), F32), jax.ShapeDtypeStruct((T, D_MODEL), F32)],
        scratch_shapes=[pltpu.VMEM((tm, D_MODEL), F32)],
        compiler_params=_params(("parallel", "arbitrary")),
    )(h1, g_pre, w_gu_t, w_down, g_post)


def _ple_loss(h2, p, target, w_pp_t, w_pg):
    T = h2.shape[0]
    tm = ROW_TILE

    def body(h2_ref, p_ref, t_ref, wpp_ref, wpg_ref, dh2_ref, de_ref, dg_ref, loss_ref):
        i = pl.program_id(0)
        h2v = h2_ref[...]
        e = _dot_nt(p_ref[...].astype(BF16), wpp_ref[...])
        sg = _sigmoid(_dot(h2v.astype(BF16), wpg_ref[...]))
        diff = h2v + e * sg - t_ref[...]
        part = jnp.sum(jnp.sum(diff * diff, axis=1, keepdims=True), axis=0, keepdims=True) * (0.5 / D_MODEL)

        @pl.when(i == 0)
        def _():
            loss_ref[...] = jnp.zeros_like(loss_ref)

        loss_ref[...] += jnp.broadcast_to(part, loss_ref.shape)
        dh3 = diff * (1.0 / D_MODEL)
        de_ref[...] = (dh3 * sg).astype(BF16)
        dg = (dh3 * e * sg * (1.0 - sg)).astype(BF16)
        dg_ref[...] = dg
        dh2_ref[...] = dh3 + _dot_nt(dg, wpg_ref[...])

    return pl.pallas_call(
        body, grid=(T // tm,), name="ple_loss",
        in_specs=[_row_spec(tm, D_MODEL), _row_spec(tm, PLE_DIM), _row_spec(tm, D_MODEL),
                  _full_spec((D_MODEL, PLE_DIM)), _full_spec((D_MODEL, D_MODEL))],
        out_specs=[_row_spec(tm, D_MODEL)] * 3 + [_full_spec((8, 128))],
        out_shape=[jax.ShapeDtypeStruct((T, D_MODEL), F32), jax.ShapeDtypeStruct((T, D_MODEL), BF16),
                   jax.ShapeDtypeStruct((T, D_MODEL), BF16), jax.ShapeDtypeStruct((8, 128), F32)],
        compiler_params=_params(("arbitrary",)),
    )(h2, p, target, w_pp_t, w_pg)


def _ffn_bwd(dh2, y, h1, gu, g_post, g_pre, w_gu_t, w_down):
    T = h1.shape[0]
    tm = ROW_TILE
    n_f = D_FF // FFN_TF

    def body(dh2_ref, y_ref, h1_ref, gu_ref, gpost_ref, gpre_ref, wgu_ref, wd_ref,
             dy_ref, a_ref, dgu_ref, dh1_ref, dgpost_ref, dgpre_ref, du2_ref):
        i = pl.program_id(0)
        j = pl.program_id(1)

        @pl.when((i == 0) & (j == 0))
        def _():
            dgpost_ref[...] = jnp.zeros_like(dgpost_ref)
            dgpre_ref[...] = jnp.zeros_like(dgpre_ref)

        @pl.when(j == 0)
        def _():
            yv = y_ref[...]
            dy, gterm = _rms_bwd(yv, _rstd(yv), gpost_ref[...], dh2_ref[...])
            dy_ref[...] = dy.astype(BF16)
            dgpost_ref[...] += _colsum(gterm)
            du2_ref[...] = jnp.zeros_like(du2_ref)

        da = _dot_nt(dy_ref[...], wd_ref[...])
        gate = gu_ref[0].astype(F32)
        up = gu_ref[1].astype(F32)
        s = _sigmoid(gate)
        sil = gate * s
        a_ref[...] = (sil * up).astype(BF16)
        dgate = (da * up * (s * (1.0 + gate * (1.0 - s)))).astype(BF16)
        dup = (da * sil).astype(BF16)
        dgu_ref[0] = dgate
        dgu_ref[1] = dup
        du2_ref[...] += _dot(dgate, wgu_ref[0]) + _dot(dup, wgu_ref[1])

        @pl.when(j == n_f - 1)
        def _():
            hv = h1_ref[...]
            dx, gterm = _rms_bwd(hv, _rstd(hv), gpre_ref[...], du2_ref[...])
            dh1_ref[...] = dh2_ref[...] + dx
            dgpre_ref[...] += _colsum(gterm)

    row = lambda width: pl.BlockSpec((tm, width), lambda i, j: (i, 0))
    col = pl.BlockSpec((tm, FFN_TF), lambda i, j: (i, j))
    both = pl.BlockSpec((2, tm, FFN_TF), lambda i, j: (0, i, j))
    vec = pl.BlockSpec((1, D_MODEL), lambda i, j: (0, 0))
    return pl.pallas_call(
        body, grid=(T // tm, n_f), name="ffn_bwd",
        in_specs=[row(D_MODEL), row(D_MODEL), row(D_MODEL), both, vec, vec,
                  pl.BlockSpec((2, FFN_TF, D_MODEL), lambda i, j: (0, j, 0)),
                  pl.BlockSpec((FFN_TF, D_MODEL), lambda i, j: (j, 0))],
        out_specs=[row(D_MODEL), col, both, row(D_MODEL), vec, vec],
        out_shape=[jax.ShapeDtypeStruct((T, D_MODEL), BF16), jax.ShapeDtypeStruct((T, D_FF), BF16),
                   jax.ShapeDtypeStruct((2, T, D_FF), BF16),
                   jax.ShapeDtypeStruct((T, D_MODEL), F32), jax.ShapeDtypeStruct((1, D_MODEL), F32),
                   jax.ShapeDtypeStruct((1, D_MODEL), F32)],
        scratch_shapes=[pltpu.VMEM((tm, D_MODEL), F32)],
        compiler_params=_params(("arbitrary", "arbitrary")),
    )(dh2, y, h1, gu, g_post, g_pre, w_gu_t, w_down)


def _out_bwd(dh1, mix, o_hg, proj, o_sb, g_hg, g_sb, g_post, w, after):
    T = dh1.shape[0]
    tm = ROW_TILE

    def body(dh1_ref, mix_ref, ohg_ref, hg_ref, osb_ref, ghg_ref, gsb_ref, gpost_ref, w_ref, after_ref,
             dmix_ref, dohg_ref, dhg_ref, dosb_ref, dgpost_ref, dghg_ref, dgsb_ref):
        i = pl.program_id(0)

        @pl.when(i == 0)
        def _():
            dgpost_ref[...] = jnp.zeros_like(dgpost_ref)
            dghg_ref[...] = jnp.zeros_like(dghg_ref)
            dgsb_ref[...] = jnp.zeros_like(dgsb_ref)

        mix = mix_ref[...]
        dmix, gterm = _rms_bwd(mix, _rstd(mix), gpost_ref[...], dh1_ref[...])
        dgpost_ref[...] += _colsum(gterm)
        dmix = dmix.astype(BF16)
        dmix_ref[...] = dmix
        dcat = _dot_nt(dmix, w_ref[...])
        da = dcat[:, :HG_WIDTH]
        dn2 = dcat[:, HG_WIDTH:]
        ohg = ohg_ref[...]
        r1 = _rstd(ohg)
        hg = hg_ref[...]
        s = _sigmoid(hg)
        dhg_ref[...] = da * (ohg * r1 * ghg_ref[...]) * (s * (1.0 + hg * (1.0 - s)))
        dohg, gterm = _rms_bwd(ohg, r1, ghg_ref[...], da * (hg * s))
        dohg_ref[...] = dohg
        dghg_ref[...] += _colsum(gterm)
        osb = osb_ref[...]
        dosb, gterm = _rms_bwd(osb, _rstd(osb), gsb_ref[...], dn2)
        dosb_ref[...] = dosb
        dgsb_ref[...] += _colsum(gterm)

    return pl.pallas_call(
        body, grid=(T // tm,), name="out_bwd",
        in_specs=[_row_spec(tm, D_MODEL), _row_spec(tm, D_MODEL), _row_spec(tm, HG_WIDTH), _row_spec(tm, HG_WIDTH, 3),
                  _row_spec(tm, SB_WIDTH), _full_spec((1, HG_WIDTH)), _full_spec((1, SB_WIDTH)),
                  _full_spec((1, D_MODEL)), _full_spec((D_MODEL, D_MODEL)), pl.BlockSpec(memory_space=pl.ANY)],
        out_specs=[_row_spec(tm, D_MODEL), _row_spec(tm, HG_WIDTH), _row_spec(tm, HG_WIDTH), _row_spec(tm, SB_WIDTH),
                   _full_spec((1, D_MODEL)), _full_spec((1, HG_WIDTH)), _full_spec((1, SB_WIDTH))],
        out_shape=[jax.ShapeDtypeStruct((T, D_MODEL), BF16), jax.ShapeDtypeStruct((T, HG_WIDTH), F32),
                   jax.ShapeDtypeStruct((T, HG_WIDTH), F32), jax.ShapeDtypeStruct((T, SB_WIDTH), F32),
                   jax.ShapeDtypeStruct((1, D_MODEL), F32), jax.ShapeDtypeStruct((1, HG_WIDTH), F32),
                   jax.ShapeDtypeStruct((1, SB_WIDTH), F32)],
        compiler_params=_params(("arbitrary",)),
    )(dh1, mix, o_hg, proj, o_sb, g_hg, g_sb, g_post, w, after)


def _in_bwd(d_hgrn, d_hg, d_sq, d_sk, d_sv, x, dh1, g_pre, w_t):
    T = x.shape[0]
    tm = ROW_TILE

    def body(dh_ref, dhg_ref, dsq_ref, dsk_ref, dsv_ref, x_ref, dh1_ref, gpre_ref, w_ref, dproj_ref, dx_ref, dgpre_ref):
        i = pl.program_id(0)

        @pl.when(i == 0)
        def _():
            dgpre_ref[...] = jnp.zeros_like(dgpre_ref)

        dp = jnp.concatenate([dh_ref[...], dhg_ref[...], dsq_ref[...], dsk_ref[...], dsv_ref[...]], axis=1).astype(BF16)
        dproj_ref[...] = dp
        du = _dot(dp, w_ref[...])
        xv = x_ref[...]
        dx, gterm = _rms_bwd(xv, _rstd(xv), gpre_ref[...], du)
        dx_ref[...] = dh1_ref[...] + dx
        dgpre_ref[...] += _colsum(gterm)

    return pl.pallas_call(
        body, grid=(T // tm,), name="in_bwd",
        in_specs=[_row_spec(tm, 3 * HG_WIDTH), _row_spec(tm, HG_WIDTH), _row_spec(tm, SB_WIDTH), _row_spec(tm, SB_WIDTH),
                  _row_spec(tm, SB_WIDTH), _row_spec(tm, D_MODEL), _row_spec(tm, D_MODEL), _full_spec((1, D_MODEL)),
                  _full_spec((IN_COLS, D_MODEL))],
        out_specs=[_row_spec(tm, IN_COLS), _row_spec(tm, D_MODEL), _full_spec((1, D_MODEL))],
        out_shape=[jax.ShapeDtypeStruct((T, IN_COLS), BF16), jax.ShapeDtypeStruct((T, D_MODEL), F32),
                   jax.ShapeDtypeStruct((1, D_MODEL), F32)],
        compiler_params=_params(("arbitrary",)),
    )(d_hgrn, d_hg, d_sq, d_sk, d_sv, x, dh1, g_pre, w_t)


def _wgrad(a, b, name):
    stacked = a.ndim == 3
    S, T, K = a.shape if stacked else (1,) + a.shape
    N = b.shape[1]
    tk = K
    while tk * N * 4 > WGRAD_ACC_BYTES:
        tk //= 2
    assert K % tk == 0 and tk % 128 == 0
    tt = 512
    n_t = T // tt

    def body(a_ref, b_ref, o_ref, acc_ref):
        t = pl.program_id(2)

        @pl.when(t == 0)
        def _():
            acc_ref[...] = jnp.zeros_like(acc_ref)

        acc_ref[...] += _dot_tn(a_ref[...].astype(BF16), b_ref[...].astype(BF16))

        @pl.when(t == n_t - 1)
        def _():
            o_ref[...] = acc_ref[...].astype(BF16)

    if stacked:
        a_spec = pl.BlockSpec((None, tt, tk), lambda s, k, t: (s, t, k))
        o_spec = pl.BlockSpec((None, tk, N), lambda s, k, t: (s, k, 0))
        o_shape = (S, K, N)
    else:
        a_spec = pl.BlockSpec((tt, tk), lambda s, k, t: (t, k))
        o_spec = pl.BlockSpec((tk, N), lambda s, k, t: (k, 0))
        o_shape = (K, N)
    return pl.pallas_call(
        body, grid=(S, K // tk, n_t), name=name,
        in_specs=[a_spec, pl.BlockSpec((tt, N), lambda s, k, t: (t, 0))],
        out_specs=o_spec,
        out_shape=jax.ShapeDtypeStruct(o_shape, BF16),
        scratch_shapes=[pltpu.VMEM((tk, N), F32)],
        compiler_params=_params(("parallel", "parallel", "arbitrary")),
    )(a, b)


def _hg_gates(hq, hf, gam):
    g0 = gam[0:1, :]
    g1 = gam[1:2, :]
    mx = jnp.maximum(g0, g1)
    e0 = jnp.exp(g0 - mx)
    e1 = jnp.exp(g1 - mx)
    lb = e0 / (e0 + e1)
    s = _sigmoid(hf)
    f = lb + (1.0 - lb) * s
    sq = _sigmoid(hq)
    return hq * sq, sq, s, f, (1.0 - lb) * (1.0 - s), jnp.log(f), lb


def _prefix_in_sub(a, rowmod):
    n = a.shape[0]
    sh = 1
    while sh < HG_SUB:
        a = a + jnp.where(rowmod >= sh, pltpu.roll(a, sh, axis=0), 0.0)
        sh *= 2
    return a


def _suffix_in_sub(a, rowmod):
    n = a.shape[0]
    sh = 1
    while sh < HG_SUB:
        a = a + jnp.where(rowmod < HG_SUB - sh, pltpu.roll(a, n - sh, axis=0), 0.0)
        sh *= 2
    return a


def _hg_fwd(proj, gam):
    T = proj.shape[0]
    tb = HG_TILE
    n_sub = tb // HG_SUB

    def body(hq_ref, hf_ref, hi_ref, gam_ref, o_ref, hist_ref, st_ref, qt_ref, kk_ref, b_ref, od_ref):
        i = pl.program_id(0)

        @pl.when(i == 0)
        def _():
            st_ref[...] = jnp.zeros_like(st_ref)

        rowmod_w = lax.broadcasted_iota(jnp.int32, (tb, HG_WIDTH), 0) % HG_SUB
        rowmod = lax.broadcasted_iota(jnp.int32, (tb, 1), 0) % HG_SUB
        q, _, _, _, kk, lf, _ = _hg_gates(hq_ref[...], hf_ref[...], gam_ref[...])
        b = _prefix_in_sub(lf, rowmod_w)
        qt_ref[...] = (q * jnp.exp(b)).astype(BF16)
        kk_ref[...] = kk
        b_ref[...] = b

        for h in range(HG_HEADS):
            cs = slice(h * HG_DK, (h + 1) * HG_DK)
            qh = q[:, cs]
            kh = kk[:, cs]
            bh = b[:, cs]
            vh = hi_ref[:, cs]
            acc = jnp.sum(qh * kh, axis=1, keepdims=True) * vh
            for d in range(1, HG_SUB):
                e = qh * pltpu.roll(kh, d, axis=0) * jnp.exp(bh - pltpu.roll(bh, d, axis=0))
                sc = jnp.where(rowmod >= d, jnp.sum(e, axis=1, keepdims=True), 0.0)
                acc = acc + sc * pltpu.roll(vh, d, axis=0)
            od_ref[:, cs] = acc

        for m in range(n_sub):
            rs = slice(m * HG_SUB, (m + 1) * HG_SUB)
            for h in range(HG_HEADS):
                cs = slice(h * HG_DK, (h + 1) * HG_DK)
                st = st_ref[h]
                hist_ref[m, h] = st
                o_int = _dot_nt(qt_ref[rs, cs], st.astype(BF16))
                bs = b_ref[rs, cs]
                bl = bs[HG_SUB - 1:HG_SUB, :]
                kt = (kk_ref[rs, cs] * jnp.exp(bl - bs)).astype(BF16)
                st_ref[h] = st * jnp.exp(bl) + _dot_tn(hi_ref[rs, cs].astype(BF16), kt)
                o_ref[rs, cs] = o_int + od_ref[rs, cs]

    col = lambda c: pl.BlockSpec((tb, HG_WIDTH), lambda i, c=c: (i, c))
    return pl.pallas_call(
        body, grid=(T // tb,), name="hg_fwd",
        in_specs=[col(0), col(1), col(2), _full_spec((2, HG_WIDTH))],
        out_specs=[pl.BlockSpec((tb, HG_WIDTH), lambda i: (i, 0)),
                   pl.BlockSpec((n_sub, HG_HEADS, HG_DK, HG_DK), lambda i: (i, 0, 0, 0))],
        out_shape=[jax.ShapeDtypeStruct((T, HG_WIDTH), F32),
                   jax.ShapeDtypeStruct((T // HG_SUB, HG_HEADS, HG_DK, HG_DK), F32)],
        scratch_shapes=[pltpu.VMEM((HG_HEADS, HG_DK, HG_DK), F32), pltpu.VMEM((tb, HG_WIDTH), BF16),
                        pltpu.VMEM((tb, HG_WIDTH), F32), pltpu.VMEM((tb, HG_WIDTH), F32),
                        pltpu.VMEM((tb, HG_WIDTH), F32)],
        compiler_params=_params(("arbitrary",)),
    )(proj, proj, proj, gam)


def _hg_bwd(proj, gam, d_o, hist):
    T = proj.shape[0]
    tb = HG_TILE
    n_sub = tb // HG_SUB
    n_t = T // tb

    def body(hq_ref, hf_ref, hi_ref, gam_ref, do_ref, hist_ref, dout_ref, dlb_ref,
             dst_ref, q_ref, kk_ref, b_ref, dq_ref, dkk_ref, db_ref, dv_ref):
        i = pl.program_id(0)

        @pl.when(i == 0)
        def _():
            dst_ref[...] = jnp.zeros_like(dst_ref)
            dlb_ref[...] = jnp.zeros_like(dlb_ref)

        rowmod_w = lax.broadcasted_iota(jnp.int32, (tb, HG_WIDTH), 0) % HG_SUB
        rowmod = lax.broadcasted_iota(jnp.int32, (tb, 1), 0) % HG_SUB
        last_row = lax.broadcasted_iota(jnp.int32, (HG_SUB, 1), 0) == HG_SUB - 1
        hq = hq_ref[...]
        q, sq, s, f, kk, lf, lb = _hg_gates(hq, hf_ref[...], gam_ref[...])
        b = _prefix_in_sub(lf, rowmod_w)
        q_ref[...] = q
        kk_ref[...] = kk
        b_ref[...] = b

        for m in reversed(range(n_sub)):
            rs = slice(m * HG_SUB, (m + 1) * HG_SUB)
            for h in range(HG_HEADS):
                cs = slice(h * HG_DK, (h + 1) * HG_DK)
                st = hist_ref[m, h]
                dst = dst_ref[h]
                dstb = dst.astype(BF16)
                dos = do_ref[rs, cs].astype(BF16)
                vs = hi_ref[rs, cs].astype(BF16)
                bs = b_ref[rs, cs]
                bl = bs[HG_SUB - 1:HG_SUB, :]
                ebl = jnp.exp(bl)
                qt = q_ref[rs, cs] * jnp.exp(bs)
                decay = jnp.exp(bl - bs)
                kt = kk_ref[rs, cs] * decay
                dqt = _dot(dos, st.astype(BF16))
                dkt = _dot(vs, dstb)
                dv_ref[rs, cs] = _dot_nt(kt.astype(BF16), dstb)
                dst_ref[h] = dst * ebl + _dot_tn(dos, qt.astype(BF16))
                dq_ref[rs, cs] = dqt * jnp.exp(bs)
                dkk_ref[rs, cs] = dkt * decay
                dktk = dkt * kt
                dbl = _colsum(dktk) + _colsum(dst * st) * ebl
                db_ref[rs, cs] = dqt * qt - dktk + jnp.where(last_row, dbl, 0.0)

        for h in range(HG_HEADS):
            cs = slice(h * HG_DK, (h + 1) * HG_DK)
            qh = q[:, cs]
            kh = kk[:, cs]
            bh = b[:, cs]
            vh = hi_ref[:, cs]
            doh = do_ref[:, cs]
            dsc = jnp.sum(doh * vh, axis=1, keepdims=True)
            sc = jnp.sum(qh * kh, axis=1, keepdims=True)
            dqh = dsc * kh
            dkh = dsc * qh
            dvh = sc * doh
            dbh = jnp.zeros_like(qh)
            for d in range(1, HG_SUB):
                back = tb - d
                valid = rowmod >= d
                kd = pltpu.roll(kh, d, axis=0)
                ex = jnp.where(valid, jnp.exp(bh - pltpu.roll(bh, d, axis=0)), 0.0)
                pd = kd * ex
                sc = jnp.sum(qh * pd, axis=1, keepdims=True)
                dsc = jnp.where(valid, jnp.sum(doh * pltpu.roll(vh, d, axis=0), axis=1, keepdims=True), 0.0)
                dqh = dqh + dsc * pd
                w = dsc * qh * pd
                dkh = dkh + pltpu.roll(dsc * qh * ex, back, axis=0)
                dvh = dvh + pltpu.roll(sc * doh, back, axis=0)
                dbh = dbh + w - pltpu.roll(w, back, axis=0)
            dq_ref[:, cs] += dqh
            dkk_ref[:, cs] += dkh
            dv_ref[:, cs] += dvh
            db_ref[:, cs] += dbh

        dlf = _suffix_in_sub(db_ref[...], rowmod_w)
        df = dlf / f - dkk_ref[...]
        dout_ref[:, 0:HG_WIDTH] = dq_ref[...] * (sq * (1.0 + hq * (1.0 - sq)))
        dout_ref[:, HG_WIDTH:2 * HG_WIDTH] = df * (1.0 - lb) * (s * (1.0 - s))
        dout_ref[:, 2 * HG_WIDTH:3 * HG_WIDTH] = dv_ref[...]
        dlb_ref[...] += _colsum(df * (1.0 - s))

    col = lambda c: pl.BlockSpec((tb, HG_WIDTH), lambda i, c=c: (n_t - 1 - i, c))
    return pl.pallas_call(
        body, grid=(n_t,), name="hg_bwd",
        in_specs=[col(0), col(1), col(2), _full_spec((2, HG_WIDTH)),
                  pl.BlockSpec((tb, HG_WIDTH), lambda i: (n_t - 1 - i, 0)),
                  pl.BlockSpec((n_sub, HG_HEADS, HG_DK, HG_DK), lambda i: (n_t - 1 - i, 0, 0, 0))],
        out_specs=[pl.BlockSpec((tb, 3 * HG_WIDTH), lambda i: (n_t - 1 - i, 0)), _full_spec((1, HG_WIDTH))],
        out_shape=[jax.ShapeDtypeStruct((T, 3 * HG_WIDTH), F32), jax.ShapeDtypeStruct((1, HG_WIDTH), F32)],
        scratch_shapes=[pltpu.VMEM((HG_HEADS, HG_DK, HG_DK), F32)] + [pltpu.VMEM((tb, HG_WIDTH), F32)] * 7,
        compiler_params=_params(("arbitrary",)),
    )(proj, proj, proj, gam, d_o, hist)


def _sb_masks():
    row = lax.broadcasted_iota(jnp.int32, (SB_TK, SB_TK), 0)
    col = lax.broadcasted_iota(jnp.int32, (SB_TK, SB_TK), 1)
    suffix = (row >= col).astype(BF16)
    prefix = (row <= col).astype(BF16)
    query = lax.broadcasted_iota(jnp.int32, (SB_TQ, SB_TK), 0)
    key = lax.broadcasted_iota(jnp.int32, (SB_TQ, SB_TK), 1)
    causal = [key + r * SB_TK < query for r in range(SB_TQ // SB_TK)]
    lane = lax.broadcasted_iota(jnp.int32, (SB_TQ, SB_BLOCK), 1)
    return suffix, prefix, causal, lane


def _sum_right(a, suffix):
    return _dot(a.astype(BF16), suffix)


def _sb_block(qb, kj, suffix, causal, c):
    z = _dot_nt(qb, kj)
    sp = jnp.maximum(z, 0.0) + jnp.log(1.0 + jnp.exp(-jnp.abs(z)))
    sig = jnp.exp(z - sp)
    if causal is not None:
        sp = jnp.where(causal, sp, 0.0)
    big_l = _sum_right(sp, suffix)
    a = jnp.exp(z - big_l - c)
    if causal is not None:
        a = jnp.where(causal, a, 0.0)
    return sig, a, big_l


def _sb_tile_copies(to_hbm, bufs, stores, sems, slot, head, i, j, n_chain):
    copies = []
    for ch in range(n_chain):
        for w in range(2):
            vmem, hbm = bufs[w].at[slot, ch], stores[w].at[head + ch, i, j]
            src, dst = (vmem, hbm) if to_hbm else (hbm, vmem)
            copies.append(pltpu.make_async_copy(src, dst, sems.at[slot, ch, w]))
    return copies


def _half_masked(pair, lane, scale=1.0):
    pair = pair.astype(F32) * scale
    return jnp.where(lane < SB_DH, pair, 0.0).astype(BF16), jnp.where(lane < SB_DH, 0.0, pair).astype(BF16)


def _sb_fwd(qkv):
    T = qkv.shape[0]
    width = SB_PP * SB_BLOCK
    n_chain = 2 * SB_PP
    n_sub = SB_TQ // SB_TK

    def body(q_ref, k_ref, v_ref, o_ref, a_st, s_st, acc_ref, c_ref, qm, a_buf, s_buf, sems):
        g = pl.program_id(0)
        i = pl.program_id(1)
        suffix, _, causal, lane = _sb_masks()
        for pp in range(SB_PP):
            qm[2 * pp], qm[2 * pp + 1] = _half_masked(q_ref[:, pp * SB_BLOCK:(pp + 1) * SB_BLOCK], lane, SB_DH ** -0.5)
        acc_ref[...] = jnp.zeros_like(acc_ref)
        c_ref[...] = jnp.zeros_like(c_ref)
        a_buf[...] = jnp.zeros_like(a_buf)
        s_buf[...] = jnp.zeros_like(s_buf)

        def copies(slot, j):
            return _sb_tile_copies(True, (a_buf, s_buf), (a_st, s_st), sems, slot, g * n_chain, i, j, n_chain)

        def step(j, slot, r=None, reuse=True):
            rows = pl.ds(pl.multiple_of(j * SB_TK, SB_TK), SB_TK)
            qs = slice(0 if r is None else r * SB_TK, SB_TQ)
            mask = None if r is None else causal[r][qs]
            if reuse:
                for cp in copies(slot, j):
                    cp.wait()
            for ch in range(n_chain):
                pair = slice((ch // 2) * SB_BLOCK, (ch // 2 + 1) * SB_BLOCK)
                c = c_ref[ch, qs]
                sig, a, big_l = _sb_block(qm[ch, qs], k_ref[rows, pair], suffix, mask, c)
                ab = a.astype(BF16)
                acc_ref[ch, qs] += _dot(ab, v_ref[rows, pair])
                a_buf[slot, ch, qs] = ab
                s_buf[slot, ch, qs] = sig.astype(BF16)
                c_ref[ch, qs] = c + jnp.broadcast_to(big_l[:, 0:1], c.shape)
            for cp in copies(slot, j):
                cp.start()

        first = i * n_sub
        for s, r in enumerate(reversed(range(n_sub))):
            step(first + r, s % 2, r, reuse=s >= 2)

        @pl.loop(0, first)
        def _(t):
            step(first - 1 - t, (n_sub + t) % 2)

        for slot in range(2):
            for cp in copies(slot, 0):
                cp.wait()
        for pp in range(SB_PP):
            o_ref[:, pp * SB_BLOCK:(pp + 1) * SB_BLOCK] = jnp.where(lane < SB_DH, acc_ref[2 * pp], acc_ref[2 * pp + 1])

    assert n_sub >= 2 and n_sub % 2 == 0
    n_g = SB_WIDTH // width
    blk = lambda part: pl.BlockSpec((SB_TQ, width), lambda g, i, part=part: (i, part * n_g + g))
    whole = lambda part: pl.BlockSpec((T, width), lambda g, i, part=part: (0, part * n_g + g))
    tiles = jax.ShapeDtypeStruct((SB_HEADS, T // SB_TQ, T // SB_TK, SB_TQ, SB_TK), BF16)
    return pl.pallas_call(
        body, grid=(n_g, T // SB_TQ), name="sb_fwd",
        in_specs=[blk(0), whole(1), whole(2)],
        out_specs=[blk(0), pl.BlockSpec(memory_space=pl.ANY), pl.BlockSpec(memory_space=pl.ANY)],
        out_shape=[jax.ShapeDtypeStruct((T, SB_WIDTH), F32), tiles, tiles],
        scratch_shapes=[pltpu.VMEM((n_chain, SB_TQ, SB_BLOCK), F32), pltpu.VMEM((n_chain, SB_TQ, SB_TK), F32),
                        pltpu.VMEM((n_chain, SB_TQ, SB_BLOCK), BF16),
                        pltpu.VMEM((2, n_chain, SB_TQ, SB_TK), BF16), pltpu.VMEM((2, n_chain, SB_TQ, SB_TK), BF16),
                        pltpu.SemaphoreType.DMA((2, n_chain, 2))],
        compiler_params=_params(("parallel", "arbitrary")),
    )(qkv, qkv, qkv)


def _sb_bwd(qkv, a_st, s_st, d_o, after):
    T = qkv.shape[0]
    width = SB_PP * SB_BLOCK
    n_chain = 2 * SB_PP
    n_sub = SB_TQ // SB_TK
    scale = SB_DH ** -0.5

    def body(q_ref, k_ref, v_ref, a_st_ref, s_st_ref, do_ref, after_ref, dq_ref, dk_ref, dv_ref, acc_ref, gc_ref, qm, dom,
             a_buf, s_buf, sems):
        g_idx = pl.program_id(0)
        i = pl.program_id(1)

        @pl.when(i == 0)
        def _():
            dk_ref[...] = jnp.zeros_like(dk_ref)
            dv_ref[...] = jnp.zeros_like(dv_ref)

        _, prefix, causal, lane = _sb_masks()
        for pp in range(SB_PP):
            pair = slice(pp * SB_BLOCK, (pp + 1) * SB_BLOCK)
            qm[2 * pp], qm[2 * pp + 1] = _half_masked(q_ref[:, pair], lane, scale)
            dom[2 * pp], dom[2 * pp + 1] = _half_masked(do_ref[:, pair], lane)
        acc_ref[...] = jnp.zeros_like(acc_ref)
        gc_ref[...] = jnp.zeros_like(gc_ref)

        def copies(slot, j):
            return _sb_tile_copies(False, (a_buf, s_buf), (a_st_ref, s_st_ref), sems, slot, g_idx * n_chain, i, j, n_chain)

        def step(j, slot, r=None, last=False):
            rows = pl.ds(pl.multiple_of(j * SB_TK, SB_TK), SB_TK)
            qs = slice(0 if r is None else r * SB_TK, SB_TQ)
            mask = None if r is None else causal[r][qs]
            for cp in copies(slot, j):
                cp.wait()
            if not last:
                for cp in copies(1 - slot, j + 1):
                    cp.start()
            for pp in range(SB_PP):
                pair = slice(pp * SB_BLOCK, (pp + 1) * SB_BLOCK)
                kj = k_ref[rows, pair]
                vj = v_ref[rows, pair]
                dk = jnp.zeros((SB_TK, SB_BLOCK), F32)
                dv = jnp.zeros((SB_TK, SB_BLOCK), F32)
                for ch in (2 * pp, 2 * pp + 1):
                    ab = a_buf[slot, ch, qs]
                    g = ab.astype(F32) * _dot_nt(dom[ch, qs], vj)
                    g_left = _sum_right(g, prefix)
                    gc = gc_ref[ch, qs]
                    dz = g - s_buf[slot, ch, qs].astype(F32) * (gc + g_left)
                    if mask is not None:
                        dz = jnp.where(mask, dz, 0.0)
                    dzb = dz.astype(BF16)
                    dk = dk + _dot_tn(dzb, qm[ch, qs])
                    dv = dv + _dot_tn(ab, dom[ch, qs])
                    acc_ref[ch, qs] += _dot(dzb, kj)
                    gc_ref[ch, qs] = gc + jnp.broadcast_to(g_left[:, SB_TK - 1:SB_TK], gc.shape)
                dk_ref[rows, pair] += dk
                dv_ref[rows, pair] += dv

        first = i * n_sub
        for cp in copies(0, 0):
            cp.start()

        @pl.loop(0, first)
        def _(j):
            step(j, j % 2)

        for r in range(n_sub):
            step(first + r, r % 2, r, last=r == n_sub - 1)
        for pp in range(SB_PP):
            dq_ref[:, pp * SB_BLOCK:(pp + 1) * SB_BLOCK] = scale * jnp.where(lane < SB_DH, acc_ref[2 * pp],
                                                                             acc_ref[2 * pp + 1])
    n_g = SB_WIDTH // width
    blk = lambda part: pl.BlockSpec((SB_TQ, width), lambda g, i, part=part: (i, part * n_g + g))
    whole = lambda part: pl.BlockSpec((T, width), lambda g, i, part=part: (0, part * n_g + g))
    return pl.pallas_call(
        body, grid=(n_g, T // SB_TQ), name="sb_bwd",
        in_specs=[blk(0), whole(1), whole(2), pl.BlockSpec(memory_space=pl.ANY), pl.BlockSpec(memory_space=pl.ANY), blk(0),
                  pl.BlockSpec(memory_space=pl.ANY)],
        out_specs=[blk(0), whole(0), whole(0)],
        out_shape=[jax.ShapeDtypeStruct((T, SB_WIDTH), F32)] * 3,
        scratch_shapes=[pltpu.VMEM((n_chain, SB_TQ, SB_BLOCK), F32), pltpu.VMEM((n_chain, SB_TQ, SB_TK), F32),
                        pltpu.VMEM((n_chain, SB_TQ, SB_BLOCK), BF16), pltpu.VMEM((n_chain, SB_TQ, SB_BLOCK), BF16),
                        pltpu.VMEM((2, n_chain, SB_TQ, SB_TK), BF16), pltpu.VMEM((2, n_chain, SB_TQ, SB_TK), BF16),
                        pltpu.SemaphoreType.DMA((2, n_chain, 2))],
        compiler_params=_params(("parallel", "arbitrary")),
    )(qkv, qkv, qkv, a_st, s_st, d_o, after)


HBM = pl.BlockSpec(memory_space=pltpu.HBM)
MESH = pl.DeviceIdType.MESH


def _place():
    return lax.axis_index("x"), lax.axis_index("y"), lax.axis_index("c")


def _all_gather(block):
    rows, cols = block.shape

    def body(x_ref, out_ref, send_sems, recv_sems, local_sem):
        x, y, c = _place()
        me, sibling = (x, y, c), (x, y, 1 - c)
        chips = [(1 - x, y), (x, 1 - y), (1 - x, 1 - y)]

        def slot(px, py, pc):
            return out_ref.at[4 * px + 2 * py + pc]

        def copy(k, blk, to, src=None):
            return pltpu.make_async_remote_copy(
                src_ref=slot(*blk) if src is None else src, dst_ref=slot(*blk),
                send_sem=send_sems.at[k], recv_sem=recv_sems.at[k], device_id=to, device_id_type=MESH)

        mine = pltpu.make_async_copy(x_ref, slot(*me), local_sem)
        mine.start()
        first = [copy(0, me, sibling, src=x_ref)]
        first += [copy(1 + j, me, (*chip, c), src=x_ref) for j, chip in enumerate(chips)]
        for cp in first:
            cp.start()
        passed = [copy(4 + j, (*chip, c), sibling) for j, chip in enumerate(chips)]
        for j, chip in enumerate(chips):
            copy(1 + j, (*chip, c), me).wait_recv()
            passed[j].start()
        copy(0, sibling, me).wait_recv()
        for j, chip in enumerate(chips):
            copy(4 + j, (*chip, 1 - c), me).wait_recv()
        for cp in first + passed:
            cp.wait_send()
        mine.wait()

    return pl.pallas_call(
        body, name="all_gather",
        out_shape=jax.ShapeDtypeStruct((N_DEV, rows, cols), block.dtype),
        in_specs=[HBM], out_specs=HBM,
        scratch_shapes=[pltpu.SemaphoreType.DMA((7,)), pltpu.SemaphoreType.DMA((7,)), pltpu.SemaphoreType.DMA],
    )(block)


SEM = pl.BlockSpec(memory_space=pltpu.SEMAPHORE)
ANY = pl.BlockSpec(memory_space=pl.ANY)
SPLIT_EFFECT = pltpu.SideEffectType.DATAFLOW_SIDE_EFFECTING
N_PEER = N_DEV - 1


def _flow_copies(kind, src_ref, land_ref, send_sems, recv_sems):
    x, y, c = _place()
    copies = []
    for r in range(1, N_DEV):
        px = 1 - x if r & 4 else x
        py = 1 - y if r & 2 else y
        pc = 1 - c if r & 1 else c
        if kind == "gather":
            src, dst = src_ref, land_ref.at[4 * x + 2 * y + c]
        else:
            src, dst = src_ref.at[4 * px + 2 * py + pc], land_ref.at[r - 1]
        copies.append(pltpu.make_async_remote_copy(
            src_ref=src, dst_ref=dst, send_sem=send_sems.at[r - 1], recv_sem=recv_sems.at[r - 1],
            device_id=(px, py, pc), device_id_type=MESH))
    return copies


def _landing(kind, src):
    if kind == "gather":
        return lax.empty((N_DEV,) + src.shape, src.dtype)
    return lax.empty((N_PEER,) + src.shape[1:], src.dtype)


def _push_start(kinds, arrays, after, name):
    n = len(arrays)

    def body(*refs):
        ins, sems, token = refs[:n], refs[n + 1:2 * n + 1], refs[3 * n + 1]
        for f, kind in enumerate(kinds):
            for cp in _flow_copies(kind, ins[2 * f], ins[2 * f + 1], sems[2 * f], sems[2 * f + 1]):
                cp.start()
        token[...] = jnp.zeros_like(token)

    outs = pl.pallas_call(
        body, name=name,
        out_shape=[pltpu.SemaphoreType.DMA((N_PEER,))] * n + [pltpu.HBM(a.shape, a.dtype) for a in arrays]
        + [jax.ShapeDtypeStruct((8, 128), F32)],
        in_specs=[HBM] * n + [ANY], out_specs=[SEM] * n + [HBM] * n + [pl.BlockSpec(memory_space=pltpu.VMEM)],
        input_output_aliases={i: n + i for i in range(n)},
        compiler_params=pltpu.CompilerParams(has_side_effects=SPLIT_EFFECT),
    )(*[pltpu.with_memory_space_constraint(a, pltpu.HBM) for a in arrays], after)
    return outs[:n], outs[n:2 * n], outs[2 * n]


def _push_wait(kinds, sems, arrays, after, name):
    n = len(arrays)

    def body(*refs):
        ins, sem_refs = refs[:n], refs[n:2 * n]
        for f, kind in enumerate(kinds):
            for cp in _flow_copies(kind, ins[2 * f], ins[2 * f + 1], sem_refs[2 * f], sem_refs[2 * f + 1]):
                cp.wait_send()
                cp.wait_recv()

    outs = pl.pallas_call(
        body, name=name,
        out_shape=[pltpu.HBM(a.shape, a.dtype) for a in arrays],
        in_specs=[HBM] * n + [SEM] * n + [ANY], out_specs=[HBM] * n,
        input_output_aliases={i: i for i in range(n)},
        compiler_params=pltpu.CompilerParams(has_side_effects=SPLIT_EFFECT),
    )(*arrays, *sems, after)
    return outs


def _sum_blocks(own, got, name):
    n, rows, cols = got.shape
    tr = next(t for t in (128, 64, 32) if rows % t == 0)

    def body(own_ref, got_ref, out_ref):
        acc = own_ref[...].astype(F32)
        for r in range(n):
            acc = acc + got_ref[r].astype(F32)
        out_ref[...] = acc

    return pl.pallas_call(
        body, grid=(rows // tr,), name=name,
        in_specs=[pl.BlockSpec((tr, cols), lambda i: (i, 0)), pl.BlockSpec((n, tr, cols), lambda i: (0, i, 0))],
        out_specs=pl.BlockSpec((tr, cols), lambda i: (i, 0)),
        out_shape=jax.ShapeDtypeStruct((rows, cols), F32),
        compiler_params=_params(("parallel",)),
    )(own, got)


def _adamw_math(w, g, m, v):
    m = ADAM_B1 * m + (1.0 - ADAM_B1) * g
    v = ADAM_B2 * v + (1.0 - ADAM_B2) * (g * g)
    m_hat = m / (1.0 - ADAM_B1 ** ADAM_STEP)
    v_hat = v / (1.0 - ADAM_B2 ** ADAM_STEP)
    delta = -ADAM_LR * (m_hat / (jnp.sqrt(v_hat) + ADAM_EPS) + ADAM_WD * w)
    return delta, m, v


def _adamw(w, g, m, v, name):
    rows, cols = w.shape
    tr = rows if rows <= 352 else 256

    def body(w_ref, g_ref, m_ref, v_ref, d_out, m_out, v_out):
        d_out[...], m_out[...], v_out[...] = _adamw_math(w_ref[...], g_ref[...], m_ref[...], v_ref[...])

    spec = pl.BlockSpec((tr, cols), lambda i: (i, 0))
    return pl.pallas_call(
        body, grid=(rows // tr,), name=name,
        in_specs=[spec] * 4, out_specs=[spec] * 3,
        out_shape=[jax.ShapeDtypeStruct((rows, cols), F32)] * 3,
        compiler_params=_params(("parallel",)),
    )(w, g, m, v)


ROW_ATTN_PRE, ROW_ATTN_POST, ROW_FFN_PRE, ROW_FFN_POST, ROW_OUT_NORMS, ROW_GAMMA, ROW_LOSS = range(7)


def _small_update(smalls, w, m, v):
    def body(s_ref, w_ref, m_ref, v_ref, g_out, d_out, m_out, v_out):
        g = s_ref[0]
        for j in range(1, N_DEV):
            g = g + s_ref[j]
        wv = w_ref[...]
        g0 = wv[ROW_GAMMA:ROW_GAMMA + 1, 0:HG_WIDTH]
        g1 = wv[ROW_GAMMA:ROW_GAMMA + 1, HG_WIDTH:]
        mx = jnp.maximum(g0, g1)
        e0 = jnp.exp(g0 - mx)
        e1 = jnp.exp(g1 - mx)
        lb = e0 / (e0 + e1)
        dg0 = g[ROW_GAMMA:ROW_GAMMA + 1, 0:HG_WIDTH] * lb * (1.0 - lb)
        row = lax.broadcasted_iota(jnp.int32, g.shape, 0)
        g = jnp.where(row == ROW_GAMMA, jnp.concatenate([dg0, -dg0], axis=1), g)
        g_out[...] = g
        d_out[...], m_out[...], v_out[...] = _adamw_math(wv, g, m_ref[...], v_ref[...])

    return pl.pallas_call(
        body, name="small_update",
        out_shape=[jax.ShapeDtypeStruct((8, D_MODEL), F32)] * 4,
    )(smalls, w, m, v)


def _local_step(x, p, target, g_attn_pre, w_in_t, gamma, g_hg, g_sb, g_attn_post, g_ffn_pre, g_ffn_post, later_weights,
                grads_ready):
    proj, u, qkv = _in_proj(x, g_attn_pre, w_in_t)
    o_hg, hist = _hg_fwd(proj, gamma)
    o_sb, a_st, s_st = _sb_fwd(qkv)
    w_out, w_gu_t, w_down, w_pp_t, w_pg = later_weights(o_sb)
    cat, mix, h1 = _out_proj(o_hg, proj, o_sb, x, g_hg, g_sb, g_attn_post, w_out)
    u2, gu, y, h2 = _ffn_fwd(h1, g_ffn_pre, w_gu_t, w_down, g_ffn_post)
    dh2, de, dg, loss = _ple_loss(h2, p, target, w_pp_t, w_pg)

    d_wpp_t = _wgrad(de, p, "wgrad_ple_proj")
    d_wpg = _wgrad(h2, dg, "wgrad_ple_gate")
    dy, a, dgu, dh1, dg_ffn_post, dg_ffn_pre = _ffn_bwd(dh2, y, h1, gu, g_ffn_post, g_ffn_pre, w_gu_t, w_down)
    d_wdown = _wgrad(a, dy, "wgrad_down")
    sent = grads_ready("ffn", (_wgrad(dgu, u2, "wgrad_gate_up"), d_wdown, d_wpp_t, d_wpg))
    dmix, d_ohg, d_hg, d_osb, dg_attn_post, dg_hg, dg_sb = _out_bwd(dh1, mix, o_hg, proj, o_sb, g_hg, g_sb, g_attn_post,
                                                                    w_out, sent)
    sent = grads_ready("out", (_wgrad(cat, dmix, "wgrad_out"),))
    dq, dk, dv = _sb_bwd(qkv, a_st, s_st, d_osb, sent)
    d_hgrn, d_lb = _hg_bwd(proj, gamma, d_ohg, hist)
    dproj, dx, dg_attn_pre = _in_bwd(d_hgrn, d_hg, dq, dk, dv, x, dh1, g_attn_pre, w_in_t)
    grads_ready("in", (_wgrad(dproj, u, "wgrad_in"),))
    return loss, dx, (dg_attn_pre, dg_attn_post, dg_ffn_pre, dg_ffn_post, dg_hg, dg_sb, d_lb)


def _pack_row(*parts):
    return jnp.concatenate([q.reshape(1, -1) for q in parts], axis=1)


def kernel(x, p, attn_pre_norm, w_in, hg_lower_gamma, hg_out_norm, sb_out_norm, w_out, attn_post_norm, ffn_pre_norm, w_gate_up, w_down, ffn_post_norm, ple_proj, ple_gate, loss_target, m_attn_pre_norm, m_w_in, m_hg_lower_gamma, m_hg_out_norm, m_sb_out_norm, m_w_out, m_attn_post_norm, m_ffn_pre_norm, m_w_gate_up, m_w_down, m_ffn_post_norm, m_ple_proj, m_ple_gate, v_attn_pre_norm, v_w_in, v_hg_lower_gamma, v_hg_out_norm, v_sb_out_norm, v_w_out, v_attn_post_norm, v_ffn_pre_norm, v_w_gate_up, v_w_down, v_ffn_post_norm, v_ple_proj, v_ple_gate):
    big = (w_in[0], w_out[0], w_gate_up[0], w_down[0], ple_proj[0], ple_gate[0])
    big_m = (m_w_in[0], m_w_out[0], m_w_gate_up[0], m_w_down[0], m_ple_proj[0], m_ple_gate[0])
    big_v = (v_w_in[0], v_w_out[0], v_w_gate_up[0], v_w_down[0], v_ple_proj[0], v_ple_gate[0])
    names = ("w_in", "w_out", "w_gate_up", "w_down", "ple_proj", "ple_gate")

    by_columns = (True, False, True, False, True, False)
    shards = [(w.T if t else w).astype(BF16) for w, t in zip(big, by_columns)]
    w_in_t = _all_gather(shards[0]).reshape(IN_COLS, D_MODEL)
    me = 4 * lax.axis_index("x") + 2 * lax.axis_index("y") + lax.axis_index("c")
    later = shards[1:]
    w_sems, w_thru, token = _push_start(
        ("gather",) * len(later), [a for s in later for a in (s, _landing("gather", s))], w_in_t, "weights_start")

    def later_weights(after):
        done = _push_wait(("gather",) * len(later), w_sems, w_thru, after, "weights_wait")
        whole = [lax.dynamic_update_index_in_dim(got, mine, me, 0) for mine, got in zip(done[0::2], done[1::2])]
        return (whole[0].reshape(D_MODEL, D_MODEL), whole[1].reshape(2, D_FF, D_MODEL), whole[2].reshape(D_FF, D_MODEL),
                whole[3].reshape(D_MODEL, PLE_DIM), whole[4].reshape(D_MODEL, D_MODEL))

    flights = {}

    def by_owner(g):
        return g.reshape(N_DEV, -1, g.shape[-1])

    def grads_ready(group, grads):
        blocks = [by_owner(g) for g in grads]
        if group == "in":
            flights[group] = blocks[0]
            return None
        sems, thru, sent = _push_start(
            ("scatter",) * len(blocks), [a for b in blocks for a in (b, _landing("scatter", b))], token,
            "grads_" + group + "_start")
        flights[group] = (sems, thru)
        return sent

    loss, dx, smalls = _local_step(
        x[0], p[0, 0], loss_target[0], attn_pre_norm + token[0:1, 0:1], w_in_t,
        hg_lower_gamma, hg_out_norm, sb_out_norm, attn_post_norm, ffn_pre_norm, ffn_post_norm, later_weights, grads_ready)
    dg_attn_pre, dg_attn_post, dg_ffn_pre, dg_ffn_post, dg_hg, dg_sb, d_lb = smalls

    zeros_half = jnp.zeros((1, HG_WIDTH), F32)
    small_pack = jnp.concatenate([
        dg_attn_pre, dg_attn_post, dg_ffn_pre, dg_ffn_post, _pack_row(dg_hg, dg_sb), _pack_row(d_lb, zeros_half),
        jnp.broadcast_to(loss[0:1, 0:1], (1, D_MODEL)), jnp.zeros((1, D_MODEL), F32)], axis=0)
    in_blocks = flights["in"]
    in_sems, in_thru, in_token = _push_start(
        ("scatter", "gather"),
        (in_blocks, _landing("scatter", in_blocks), small_pack, _landing("gather", small_pack)), token,
        "grads_in_start")
    ffn = _push_wait(("scatter",) * 4, *flights["ffn"], in_token, "grads_ffn_wait")
    sent_out, got_out = _push_wait(("scatter",), *flights["out"], ffn[1], "grads_out_wait")
    sent_in, got_in, my_small, all_smalls = _push_wait(("scatter", "gather"), in_sems, in_thru, got_out, "grads_in_wait")
    all_smalls = lax.dynamic_update_index_in_dim(all_smalls, my_small, me, 0)

    def reduced(sent, got, name):
        return _sum_blocks(lax.dynamic_index_in_dim(sent, me, 0, keepdims=False), got, "grad_sum_" + name)

    sums = (reduced(sent_in, got_in, "w_in"), reduced(sent_out, got_out, "w_out"), reduced(ffn[0], ffn[1], "w_gate_up"),
            reduced(ffn[2], ffn[3], "w_down"), reduced(ffn[4], ffn[5], "ple_proj"), reduced(ffn[6], ffn[7], "ple_gate"))

    out_g, out_d, out_m, out_v = {}, {}, {}, {}
    for i, name in enumerate(names):
        g = sums[i].T if by_columns[i] else sums[i]
        out_g[name] = g[None]
        d, m, v = _adamw(big[i], g, big_m[i], big_v[i], "adamw_" + name)
        out_d[name], out_m[name], out_v[name] = d[None], m[None], v[None]

    def small_rows(pre, gam, hg, sb, post, fpre, fpost):
        return jnp.concatenate([pre, post, fpre, fpost, _pack_row(hg, sb), _pack_row(gam[0], gam[1]),
                                jnp.zeros((2, D_MODEL), F32)], axis=0)

    packs = _small_update(
        all_smalls,
        small_rows(attn_pre_norm, hg_lower_gamma, hg_out_norm, sb_out_norm, attn_post_norm, ffn_pre_norm, ffn_post_norm),
        small_rows(m_attn_pre_norm, m_hg_lower_gamma, m_hg_out_norm, m_sb_out_norm, m_attn_post_norm, m_ffn_pre_norm,
                   m_ffn_post_norm),
        small_rows(v_attn_pre_norm, v_hg_lower_gamma, v_hg_out_norm, v_sb_out_norm, v_attn_post_norm, v_ffn_pre_norm,
                   v_ffn_post_norm))

    def unpack(pk):
        return {
            "attn_pre_norm": pk[ROW_ATTN_PRE:ROW_ATTN_PRE + 1],
            "hg_lower_gamma": pk[ROW_GAMMA].reshape(2, HG_WIDTH),
            "hg_out_norm": pk[ROW_OUT_NORMS:ROW_OUT_NORMS + 1, :HG_WIDTH],
            "sb_out_norm": pk[ROW_OUT_NORMS:ROW_OUT_NORMS + 1, HG_WIDTH:],
            "attn_post_norm": pk[ROW_ATTN_POST:ROW_ATTN_POST + 1],
            "ffn_pre_norm": pk[ROW_FFN_PRE:ROW_FFN_PRE + 1],
            "ffn_post_norm": pk[ROW_FFN_POST:ROW_FFN_POST + 1],
        }

    sg, sd, sm, sv = (unpack(pk) for pk in packs)
    out_g.update(sg), out_d.update(sd), out_m.update(sm), out_v.update(sv)
    order = ("attn_pre_norm", "w_in", "hg_lower_gamma", "hg_out_norm", "sb_out_norm", "w_out", "attn_post_norm",
             "ffn_pre_norm", "w_gate_up", "w_down", "ffn_post_norm", "ple_proj", "ple_gate")
    total_loss = packs[0][ROW_LOSS, 0]
    return (total_loss, dx[None], *[out_g[n] for n in order], *[out_d[n] for n in order],
            *[out_m[n] for n in order], *[out_v[n] for n in order])
```

```python
import functools

import jax
import jax.numpy as jnp
from jax import lax
from jax.experimental import pallas as pl
from jax.experimental.pallas import tpu as pltpu

F32 = jnp.float32
BF16 = jnp.bfloat16

D_MODEL = 1024
HG_WIDTH = 512
HG_HEADS = 4
HG_DK = 128
SB_WIDTH = 512
SB_HEADS = 8
SB_DH = 64
SB_BLOCK = 128
SB_PP = 2
SB_TQ = 512
SB_TK = 256
D_FF = 2816
PLE_DIM = 256
IN_COLS = 4 * HG_WIDTH + 3 * SB_WIDTH
EPS = 1e-6
N_DEV = 8

HG_SUB = 16
HG_TILE = 128
FFN_TF = 1408
ROW_TILE = 256
VMEM_LIMIT = 56 * 1024 * 1024
WGRAD_ACC_BYTES = 8 * 1024 * 1024

ADAM_LR = 0.001
ADAM_B1 = 0.9
ADAM_B2 = 0.999
ADAM_EPS = 1e-08
ADAM_WD = 0.01
ADAM_STEP = 10

NT_DIMS = (((1,), (1,)), ((), ()))
TN_DIMS = (((0,), (0,)), ((), ()))


def _params(semantics):
    return pltpu.CompilerParams(dimension_semantics=semantics, vmem_limit_bytes=VMEM_LIMIT)


def _dot(a, b):
    return jnp.dot(a, b, preferred_element_type=F32)


def _dot_nt(a, b):
    return lax.dot_general(a, b, NT_DIMS, preferred_element_type=F32)


def _dot_tn(a, b):
    return lax.dot_general(a, b, TN_DIMS, preferred_element_type=F32)


def _sigmoid(z):
    return 1.0 / (1.0 + jnp.exp(-z))


def _rstd(xv):
    return lax.rsqrt(jnp.mean(xv * xv, axis=-1, keepdims=True) + EPS)


def _rms_bwd(xv, r, g, dn):
    xn = xv * r
    gh = dn * g
    dx = r * (gh - xn * jnp.mean(gh * xn, axis=-1, keepdims=True))
    return dx, dn * xn


def _colsum(a):
    return jnp.sum(a, axis=0, keepdims=True)


def _row_spec(tm, width, col=0):
    return pl.BlockSpec((tm, width), lambda i, col=col: (i, col))


def _full_spec(shape):
    return pl.BlockSpec(shape, lambda i: (0,) * len(shape))


def _in_proj(x, g, w_t):
    T = x.shape[0]
    tm = ROW_TILE

    def body(x_ref, g_ref, w_ref, proj_ref, u_ref, qkv_ref):
        xv = x_ref[...]
        u = (xv * _rstd(xv) * g_ref[...]).astype(BF16)
        u_ref[...] = u
        proj = _dot_nt(u, w_ref[...])
        proj_ref[...] = proj
        qkv_ref[...] = proj[:, 4 * HG_WIDTH:].astype(BF16)

    return pl.pallas_call(
        body, grid=(T // tm,), name="in_proj",
        in_specs=[_row_spec(tm, D_MODEL), _full_spec((1, D_MODEL)), _full_spec((IN_COLS, D_MODEL))],
        out_specs=[_row_spec(tm, IN_COLS), _row_spec(tm, D_MODEL), _row_spec(tm, 3 * SB_WIDTH)],
        out_shape=[jax.ShapeDtypeStruct((T, IN_COLS), F32), jax.ShapeDtypeStruct((T, D_MODEL), BF16),
                   jax.ShapeDtypeStruct((T, 3 * SB_WIDTH), BF16)],
        compiler_params=_params(("parallel",)),
    )(x, g, w_t)


def _out_proj(o_hg, proj, o_sb, x, g_hg, g_sb, g_post, w):
    T = x.shape[0]
    tm = ROW_TILE

    def body(ohg_ref, hg_ref, osb_ref, x_ref, ghg_ref, gsb_ref, gpost_ref, w_ref, cat_ref, mix_ref, h1_ref):
        ohg = ohg_ref[...]
        hg = hg_ref[...]
        osb = osb_ref[...]
        a = ohg * _rstd(ohg) * ghg_ref[...] * (hg * _sigmoid(hg))
        n2 = osb * _rstd(osb) * gsb_ref[...]
        cat = jnp.concatenate([a, n2], axis=1).astype(BF16)
        cat_ref[...] = cat
        mix = _dot(cat, w_ref[...])
        mix_ref[...] = mix
        h1_ref[...] = x_ref[...] + mix * _rstd(mix) * gpost_ref[...]

    return pl.pallas_call(
        body, grid=(T // tm,), name="out_proj",
        in_specs=[_row_spec(tm, HG_WIDTH), _row_spec(tm, HG_WIDTH, 3), _row_spec(tm, SB_WIDTH), _row_spec(tm, D_MODEL),
                  _full_spec((1, HG_WIDTH)), _full_spec((1, SB_WIDTH)), _full_spec((1, D_MODEL)),
                  _full_spec((D_MODEL, D_MODEL))],
        out_specs=[_row_spec(tm, D_MODEL)] * 3,
        out_shape=[jax.ShapeDtypeStruct((T, D_MODEL), BF16), jax.ShapeDtypeStruct((T, D_MODEL), F32),
                   jax.ShapeDtypeStruct((T, D_MODEL), F32)],
        compiler_params=_params(("parallel",)),
    )(o_hg, proj, o_sb, x, g_hg, g_sb, g_post, w)


def _ffn_fwd(h1, g_pre, w_gu_t, w_down, g_post):
    T = h1.shape[0]
    tm = 2 * ROW_TILE
    n_f = D_FF // FFN_TF

    def body(h1_ref, gpre_ref, wgu_ref, wd_ref, gpost_ref, u2_ref, gu_ref, y_ref, h2_ref, acc_ref):
        j = pl.program_id(1)

        @pl.when(j == 0)
        def _():
            hv = h1_ref[...]
            u2_ref[...] = (hv * _rstd(hv) * gpre_ref[...]).astype(BF16)
            acc_ref[...] = jnp.zeros_like(acc_ref)

        u2 = u2_ref[...]
        gate = _dot_nt(u2, wgu_ref[0])
        up = _dot_nt(u2, wgu_ref[1])
        gu_ref[0] = gate.astype(BF16)
        gu_ref[1] = up.astype(BF16)
        a = (gate * _sigmoid(gate) * up).astype(BF16)
        acc_ref[...] += _dot(a, wd_ref[...])

        @pl.when(j == n_f - 1)
        def _():
            y = acc_ref[...]
            y_ref[...] = y
            h2_ref[...] = h1_ref[...] + y * _rstd(y) * gpost_ref[...]

    row = lambda width: pl.BlockSpec((tm, width), lambda i, j: (i, 0))
    vec = pl.BlockSpec((1, D_MODEL), lambda i, j: (0, 0))
    return pl.pallas_call(
        body, grid=(T // tm, n_f), name="ffn_fwd",
        in_specs=[row(D_MODEL), vec,
                  pl.BlockSpec((2, FFN_TF, D_MODEL), lambda i, j: (0, j, 0)),
                  pl.BlockSpec((FFN_TF, D_MODEL), lambda i, j: (j, 0)), vec],
        out_specs=[row(D_MODEL),
                   pl.BlockSpec((2, tm, FFN_TF), lambda i, j: (0, i, j)),
                   row(D_MODEL), row(D_MODEL)],
        out_shape=[jax.ShapeDtypeStruct((T, D_MODEL), BF16), jax.ShapeDtypeStruct((2, T, D_FF), BF16),
                   jax.ShapeDtypeStruct((T, D_MODEL), F32), jax.ShapeDtypeStruct((T, D_MODEL), F32)],
        scratch_shapes=[pltpu.VMEM((tm, D_MODEL), F32)],
        compiler_params=_params(("parallel", "arbitrary")),
    )(h1, g_pre, w_gu_t, w_down, g_post)


def _ple_loss(h2, p, target, w_pp_t, w_pg):
    T = h2.shape[0]
    tm = ROW_TILE

    def body(h2_ref, p_ref, t_ref, wpp_ref, wpg_ref, dh2_ref, de_ref, dg_ref, loss_ref):
        i = pl.program_id(0)
        h2v = h2_ref[...]
        e = _dot_nt(p_ref[...].astype(BF16), wpp_ref[...])
        sg = _sigmoid(_dot(h2v.astype(BF16), wpg_ref[...]))
        diff = h2v + e * sg - t_ref[...]
        part = jnp.sum(jnp.sum(diff * diff, axis=1, keepdims=True), axis=0, keepdims=True) * (0.5 / D_MODEL)

        @pl.when(i == 0)
        def _():
            loss_ref[...] = jnp.zeros_like(loss_ref)

        loss_ref[...] += jnp.broadcast_to(part, loss_ref.shape)
        dh3 = diff * (1.0 / D_MODEL)
        de_ref[...] = (dh3 * sg).astype(BF16)
        dg = (dh3 * e * sg * (1.0 - sg)).astype(BF16)
        dg_ref[...] = dg
        dh2_ref[...] = dh3 + _dot_nt(dg, wpg_ref[...])

    return pl.pallas_call(
        body, grid=(T // tm,), name="ple_loss",
        in_specs=[_row_spec(tm, D_MODEL), _row_spec(tm, PLE_DIM), _row_spec(tm, D_MODEL),
                  _full_spec((D_MODEL, PLE_DIM)), _full_spec((D_MODEL, D_MODEL))],
        out_specs=[_row_spec(tm, D_MODEL)] * 3 + [_full_spec((8, 128))],
        out_shape=[jax.ShapeDtypeStruct((T, D_MODEL), F32), jax.ShapeDtypeStruct((T, D_MODEL), BF16),
                   jax.ShapeDtypeStruct((T, D_MODEL), BF16), jax.ShapeDtypeStruct((8, 128), F32)],
        compiler_params=_params(("arbitrary",)),
    )(h2, p, target, w_pp_t, w_pg)


def _ffn_bwd(dh2, y, h1, gu, g_post, g_pre, w_gu_t, w_down):
    T = h1.shape[0]
    tm = ROW_TILE
    n_f = D_FF // FFN_TF

    def body(dh2_ref, y_ref, h1_ref, gu_ref, gpost_ref, gpre_ref, wgu_ref, wd_ref,
             dy_ref, a_ref, dgu_ref, dh1_ref, dgpost_ref, dgpre_ref, du2_ref):
        i = pl.program_id(0)
        j = pl.program_id(1)

        @pl.when((i == 0) & (j == 0))
        def _():
            dgpost_ref[...] = jnp.zeros_like(dgpost_ref)
            dgpre_ref[...] = jnp.zeros_like(dgpre_ref)

        @pl.when(j == 0)
        def _():
            yv = y_ref[...]
            dy, gterm = _rms_bwd(yv, _rstd(yv), gpost_ref[...], dh2_ref[...])
            dy_ref[...] = dy.astype(BF16)
            dgpost_ref[...] += _colsum(gterm)
            du2_ref[...] = jnp.zeros_like(du2_ref)

        da = _dot_nt(dy_ref[...], wd_ref[...])
        gate = gu_ref[0].astype(F32)
        up = gu_ref[1].astype(F32)
        s = _sigmoid(gate)
        sil = gate * s
        a_ref[...] = (sil * up).astype(BF16)
        dgate = (da * up * (s * (1.0 + gate * (1.0 - s)))).astype(BF16)
        dup = (da * sil).astype(BF16)
        dgu_ref[0] = dgate
        dgu_ref[1] = dup
        du2_ref[...] += _dot(dgate, wgu_ref[0]) + _dot(dup, wgu_ref[1])

        @pl.when(j == n_f - 1)
        def _():
            hv = h1_ref[...]
            dx, gterm = _rms_bwd(hv, _rstd(hv), gpre_ref[...], du2_ref[...])
            dh1_ref[...] = dh2_ref[...] + dx
            dgpre_ref[...] += _colsum(gterm)

    row = lambda width: pl.BlockSpec((tm, width), lambda i, j: (i, 0))
    col = pl.BlockSpec((tm, FFN_TF), lambda i, j: (i, j))
    both = pl.BlockSpec((2, tm, FFN_TF), lambda i, j: (0, i, j))
    vec = pl.BlockSpec((1, D_MODEL), lambda i, j: (0, 0))
    return pl.pallas_call(
        body, grid=(T // tm, n_f), name="ffn_bwd",
        in_specs=[row(D_MODEL), row(D_MODEL), row(D_MODEL), both, vec, vec,
                  pl.BlockSpec((2, FFN_TF, D_MODEL), lambda i, j: (0, j, 0)),
                  pl.BlockSpec((FFN_TF, D_MODEL), lambda i, j: (j, 0))],
        out_specs=[row(D_MODEL), col, both, row(D_MODEL), vec, vec],
        out_shape=[jax.ShapeDtypeStruct((T, D_MODEL), BF16), jax.ShapeDtypeStruct((T, D_FF), BF16),
                   jax.ShapeDtypeStruct((2, T, D_FF), BF16),
                   jax.ShapeDtypeStruct((T, D_MODEL), F32), jax.ShapeDtypeStruct((1, D_MODEL), F32),
                   jax.ShapeDtypeStruct((1, D_MODEL), F32)],
        scratch_shapes=[pltpu.VMEM((tm, D_MODEL), F32)],
        compiler_params=_params(("arbitrary", "arbitrary")),
    )(dh2, y, h1, gu, g_post, g_pre, w_gu_t, w_down)


def _out_bwd(dh1, mix, o_hg, proj, o_sb, g_hg, g_sb, g_post, w, after):
    T = dh1.shape[0]
    tm = ROW_TILE

    def body(dh1_ref, mix_ref, ohg_ref, hg_ref, osb_ref, ghg_ref, gsb_ref, gpost_ref, w_ref, after_ref,
             dmix_ref, dohg_ref, dhg_ref, dosb_ref, dgpost_ref, dghg_ref, dgsb_ref):
        i = pl.program_id(0)

        @pl.when(i == 0)
        def _():
            dgpost_ref[...] = jnp.zeros_like(dgpost_ref)
            dghg_ref[...] = jnp.zeros_like(dghg_ref)
            dgsb_ref[...] = jnp.zeros_like(dgsb_ref)

        mix = mix_ref[...]
        dmix, gterm = _rms_bwd(mix, _rstd(mix), gpost_ref[...], dh1_ref[...])
        dgpost_ref[...] += _colsum(gterm)
        dmix = dmix.astype(BF16)
        dmix_ref[...] = dmix
        dcat = _dot_nt(dmix, w_ref[...])
        da = dcat[:, :HG_WIDTH]
        dn2 = dcat[:, HG_WIDTH:]
        ohg = ohg_ref[...]
        r1 = _rstd(ohg)
        hg = hg_ref[...]
        s = _sigmoid(hg)
        dhg_ref[...] = da * (ohg * r1 * ghg_ref[...]) * (s * (1.0 + hg * (1.0 - s)))
        dohg, gterm = _rms_bwd(ohg, r1, ghg_ref[...], da * (hg * s))
        dohg_ref[...] = dohg
        dghg_ref[...] += _colsum(gterm)
        osb = osb_ref[...]
        dosb, gterm = _rms_bwd(osb, _rstd(osb), gsb_ref[...], dn2)
        dosb_ref[...] = dosb
        dgsb_ref[...] += _colsum(gterm)

    return pl.pallas_call(
        body, grid=(T // tm,), name="out_bwd",
        in_specs=[_row_spec(tm, D_MODEL), _row_spec(tm, D_MODEL), _row_spec(tm, HG_WIDTH), _row_spec(tm, HG_WIDTH, 3),
                  _row_spec(tm, SB_WIDTH), _full_spec((1, HG_WIDTH)), _full_spec((1, SB_WIDTH)),
                  _full_spec((1, D_MODEL)), _full_spec((D_MODEL, D_MODEL)), pl.BlockSpec(memory_space=pl.ANY)],
        out_specs=[_row_spec(tm, D_MODEL), _row_spec(tm, HG_WIDTH), _row_spec(tm, HG_WIDTH), _row_spec(tm, SB_WIDTH),
                   _full_spec((1, D_MODEL)), _full_spec((1, HG_WIDTH)), _full_spec((1, SB_WIDTH))],
        out_shape=[jax.ShapeDtypeStruct((T, D_MODEL), BF16), jax.ShapeDtypeStruct((T, HG_WIDTH), F32),
                   jax.ShapeDtypeStruct((T, HG_WIDTH), F32), jax.ShapeDtypeStruct((T, SB_WIDTH), F32),
                   jax.ShapeDtypeStruct((1, D_MODEL), F32), jax.ShapeDtypeStruct((1, HG_WIDTH), F32),
                   jax.ShapeDtypeStruct((1, SB_WIDTH), F32)],
        compiler_params=_params(("arbitrary",)),
    )(dh1, mix, o_hg, proj, o_sb, g_hg, g_sb, g_post, w, after)


def _in_bwd(d_hgrn, d_hg, d_sq, d_sk, d_sv, x, dh1, g_pre, w_t):
    T = x.shape[0]
    tm = ROW_TILE

    def body(dh_ref, dhg_ref, dsq_ref, dsk_ref, dsv_ref, x_ref, dh1_ref, gpre_ref, w_ref, dproj_ref, dx_ref, dgpre_ref):
        i = pl.program_id(0)

        @pl.when(i == 0)
        def _():
            dgpre_ref[...] = jnp.zeros_like(dgpre_ref)

        dp = jnp.concatenate([dh_ref[...], dhg_ref[...], dsq_ref[...], dsk_ref[...], dsv_ref[...]], axis=1).astype(BF16)
        dproj_ref[...] = dp
        du = _dot(dp, w_ref[...])
        xv = x_ref[...]
        dx, gterm = _rms_bwd(xv, _rstd(xv), gpre_ref[...], du)
        dx_ref[...] = dh1_ref[...] + dx
        dgpre_ref[...] += _colsum(gterm)

    return pl.pallas_call(
        body, grid=(T // tm,), name="in_bwd",
        in_specs=[_row_spec(tm, 3 * HG_WIDTH), _row_spec(tm, HG_WIDTH), _row_spec(tm, SB_WIDTH), _row_spec(tm, SB_WIDTH),
                  _row_spec(tm, SB_WIDTH), _row_spec(tm, D_MODEL), _row_spec(tm, D_MODEL), _full_spec((1, D_MODEL)),
                  _full_spec((IN_COLS, D_MODEL))],
        out_specs=[_row_spec(tm, IN_COLS), _row_spec(tm, D_MODEL), _full_spec((1, D_MODEL))],
        out_shape=[jax.ShapeDtypeStruct((T, IN_COLS), BF16), jax.ShapeDtypeStruct((T, D_MODEL), F32),
                   jax.ShapeDtypeStruct((1, D_MODEL), F32)],
        compiler_params=_params(("arbitrary",)),
    )(d_hgrn, d_hg, d_sq, d_sk, d_sv, x, dh1, g_pre, w_t)


def _wgrad(a, b, name):
    stacked = a.ndim == 3
    S, T, K = a.shape if stacked else (1,) + a.shape
    N = b.shape[1]
    tk = K
    while tk * N * 4 > WGRAD_ACC_BYTES:
        tk //= 2
    assert K % tk == 0 and tk % 128 == 0
    tt = 512
    n_t = T // tt

    def body(a_ref, b_ref, o_ref, acc_ref):
        t = pl.program_id(2)

        @pl.when(t == 0)
        def _():
            acc_ref[...] = jnp.zeros_like(acc_ref)

        acc_ref[...] += _dot_tn(a_ref[...].astype(BF16), b_ref[...].astype(BF16))

        @pl.when(t == n_t - 1)
        def _():
            o_ref[...] = acc_ref[...].astype(BF16)

    if stacked:
        a_spec = pl.BlockSpec((None, tt, tk), lambda s, k, t: (s, t, k))
        o_spec = pl.BlockSpec((None, tk, N), lambda s, k, t: (s, k, 0))
        o_shape = (S, K, N)
    else:
        a_spec = pl.BlockSpec((tt, tk), lambda s, k, t: (t, k))
        o_spec = pl.BlockSpec((tk, N), lambda s, k, t: (k, 0))
        o_shape = (K, N)
    return pl.pallas_call(
        body, grid=(S, K // tk, n_t), name=name,
        in_specs=[a_spec, pl.BlockSpec((tt, N), lambda s, k, t: (t, 0))],
        out_specs=o_spec,
        out_shape=jax.ShapeDtypeStruct(o_shape, BF16),
        scratch_shapes=[pltpu.VMEM((tk, N), F32)],
        compiler_params=_params(("parallel", "parallel", "arbitrary")),
    )(a, b)


def _hg_gates(hq, hf, gam):
    g0 = gam[0:1, :]
    g1 = gam[1:2, :]
    mx = jnp.maximum(g0, g1)
    e0 = jnp.exp(g0 - mx)
    e1 = jnp.exp(g1 - mx)
    lb = e0 / (e0 + e1)
    s = _sigmoid(hf)
    f = lb + (1.0 - lb) * s
    sq = _sigmoid(hq)
    return hq * sq, sq, s, f, (1.0 - lb) * (1.0 - s), jnp.log(f), lb


def _prefix_in_sub(a, rowmod):
    n = a.shape[0]
    sh = 1
    while sh < HG_SUB:
        a = a + jnp.where(rowmod >= sh, pltpu.roll(a, sh, axis=0), 0.0)
        sh *= 2
    return a


def _suffix_in_sub(a, rowmod):
    n = a.shape[0]
    sh = 1
    while sh < HG_SUB:
        a = a + jnp.where(rowmod < HG_SUB - sh, pltpu.roll(a, n - sh, axis=0), 0.0)
        sh *= 2
    return a


def _hg_fwd(proj, gam):
    T = proj.shape[0]
    tb = HG_TILE
    n_sub = tb // HG_SUB

    def body(hq_ref, hf_ref, hi_ref, gam_ref, o_ref, hist_ref, st_ref, qt_ref, kk_ref, b_ref, od_ref):
        i = pl.program_id(0)

        @pl.when(i == 0)
        def _():
            st_ref[...] = jnp.zeros_like(st_ref)

        rowmod_w = lax.broadcasted_iota(jnp.int32, (tb, HG_WIDTH), 0) % HG_SUB
        rowmod = lax.broadcasted_iota(jnp.int32, (tb, 1), 0) % HG_SUB
        q, _, _, f, kk, lf, _ = _hg_gates(hq_ref[...], hf_ref[...], gam_ref[...])
        b = _prefix_in_sub(lf, rowmod_w)
        qt_ref[...] = (q * jnp.exp(b)).astype(BF16)
        kk_ref[...] = kk
        b_ref[...] = b

        for h in range(HG_HEADS):
            cs = slice(h * HG_DK, (h + 1) * HG_DK)
            qh = q[:, cs]
            kh = kk[:, cs]
            fh = f[:, cs]
            vh = hi_ref[:, cs]
            acc = jnp.sum(qh * kh, axis=1, keepdims=True) * vh
            decay = fh
            for d in range(1, HG_SUB):
                if d > 1:
                    decay = decay * pltpu.roll(fh, d - 1, axis=0)
                e = qh * pltpu.roll(kh, d, axis=0) * decay
                sc = jnp.where(rowmod >= d, jnp.sum(e, axis=1, keepdims=True), 0.0)
                acc = acc + sc * pltpu.roll(vh, d, axis=0)
            od_ref[:, cs] = acc

        for m in range(n_sub):
            rs = slice(m * HG_SUB, (m + 1) * HG_SUB)
            for h in range(HG_HEADS):
                cs = slice(h * HG_DK, (h + 1) * HG_DK)
                st = st_ref[h]
                hist_ref[m, h] = st
                o_int = _dot_nt(qt_ref[rs, cs], st.astype(BF16))
                bs = b_ref[rs, cs]
                bl = bs[HG_SUB - 1:HG_SUB, :]
                kt = (kk_ref[rs, cs] * jnp.exp(bl - bs)).astype(BF16)
                st_ref[h] = st * jnp.exp(bl) + _dot_tn(hi_ref[rs, cs].astype(BF16), kt)
                o_ref[rs, cs] = o_int + od_ref[rs, cs]

    col = lambda c: pl.BlockSpec((tb, HG_WIDTH), lambda i, c=c: (i, c))
    return pl.pallas_call(
        body, grid=(T // tb,), name="hg_fwd",
        in_specs=[col(0), col(1), col(2), _full_spec((2, HG_WIDTH))],
        out_specs=[pl.BlockSpec((tb, HG_WIDTH), lambda i: (i, 0)),
                   pl.BlockSpec((n_sub, HG_HEADS, HG_DK, HG_DK), lambda i: (i, 0, 0, 0))],
        out_shape=[jax.ShapeDtypeStruct((T, HG_WIDTH), F32),
                   jax.ShapeDtypeStruct((T // HG_SUB, HG_HEADS, HG_DK, HG_DK), F32)],
        scratch_shapes=[pltpu.VMEM((HG_HEADS, HG_DK, HG_DK), F32), pltpu.VMEM((tb, HG_WIDTH), BF16),
                        pltpu.VMEM((tb, HG_WIDTH), F32), pltpu.VMEM((tb, HG_WIDTH), F32),
                        pltpu.VMEM((tb, HG_WIDTH), F32)],
        compiler_params=_params(("arbitrary",)),
    )(proj, proj, proj, gam)


def _hg_bwd(proj, gam, d_o, hist):
    T = proj.shape[0]
    tb = HG_TILE
    n_sub = tb // HG_SUB
    n_t = T // tb

    def body(hq_ref, hf_ref, hi_ref, gam_ref, do_ref, hist_ref, dout_ref, dlb_ref,
             dst_ref, q_ref, kk_ref, b_ref, dq_ref, dkk_ref, db_ref, dv_ref):
        i = pl.program_id(0)

        @pl.when(i == 0)
        def _():
            dst_ref[...] = jnp.zeros_like(dst_ref)
            dlb_ref[...] = jnp.zeros_like(dlb_ref)

        rowmod_w = lax.broadcasted_iota(jnp.int32, (tb, HG_WIDTH), 0) % HG_SUB
        rowmod = lax.broadcasted_iota(jnp.int32, (tb, 1), 0) % HG_SUB
        last_row = lax.broadcasted_iota(jnp.int32, (HG_SUB, 1), 0) == HG_SUB - 1
        hq = hq_ref[...]
        q, sq, s, f, kk, lf, lb = _hg_gates(hq, hf_ref[...], gam_ref[...])
        b = _prefix_in_sub(lf, rowmod_w)
        q_ref[...] = q
        kk_ref[...] = kk
        b_ref[...] = b

        for m in reversed(range(n_sub)):
            rs = slice(m * HG_SUB, (m + 1) * HG_SUB)
            for h in range(HG_HEADS):
                cs = slice(h * HG_DK, (h + 1) * HG_DK)
                st = hist_ref[m, h]
                dst = dst_ref[h]
                dstb = dst.astype(BF16)
                dos = do_ref[rs, cs].astype(BF16)
                vs = hi_ref[rs, cs].astype(BF16)
                bs = b_ref[rs, cs]
                bl = bs[HG_SUB - 1:HG_SUB, :]
                ebl = jnp.exp(bl)
                qt = q_ref[rs, cs] * jnp.exp(bs)
                decay = jnp.exp(bl - bs)
                kt = kk_ref[rs, cs] * decay
                dqt = _dot(dos, st.astype(BF16))
                dkt = _dot(vs, dstb)
                dv_ref[rs, cs] = _dot_nt(kt.astype(BF16), dstb)
                dst_ref[h] = dst * ebl + _dot_tn(dos, qt.astype(BF16))
                dq_ref[rs, cs] = dqt * jnp.exp(bs)
                dkk_ref[rs, cs] = dkt * decay
                dktk = dkt * kt
                dbl = _colsum(dktk) + _colsum(dst * st) * ebl
                db_ref[rs, cs] = dqt * qt - dktk + jnp.where(last_row, dbl, 0.0)

        for h in range(HG_HEADS):
            cs = slice(h * HG_DK, (h + 1) * HG_DK)
            qh = q[:, cs]
            kh = kk[:, cs]
            fh = f[:, cs]
            vh = hi_ref[:, cs]
            doh = do_ref[:, cs]
            dsc = jnp.sum(doh * vh, axis=1, keepdims=True)
            sc = jnp.sum(qh * kh, axis=1, keepdims=True)
            dqh = dsc * kh
            dkh = dsc * qh
            dvh = sc * doh
            dbh = jnp.zeros_like(qh)
            decay = fh
            for d in range(1, HG_SUB):
                back = tb - d
                valid = rowmod >= d
                if d > 1:
                    decay = decay * pltpu.roll(fh, d - 1, axis=0)
                kd = pltpu.roll(kh, d, axis=0)
                ex = jnp.where(valid, decay, 0.0)
                pd = kd * ex
                sc = jnp.sum(qh * pd, axis=1, keepdims=True)
                dsc = jnp.where(valid, jnp.sum(doh * pltpu.roll(vh, d, axis=0), axis=1, keepdims=True), 0.0)
                dqh = dqh + dsc * pd
                dsq = dsc * qh
                w = dsq * pd
                dkh = dkh + pltpu.roll(dsq * ex, back, axis=0)
                dvh = dvh + pltpu.roll(sc * doh, back, axis=0)
                dbh = dbh + w - pltpu.roll(w, back, axis=0)
            dq_ref[:, cs] += dqh
            dkk_ref[:, cs] += dkh
            dv_ref[:, cs] += dvh
            db_ref[:, cs] += dbh

        dlf = _suffix_in_sub(db_ref[...], rowmod_w)
        df = dlf / f - dkk_ref[...]
        dout_ref[:, 0:HG_WIDTH] = dq_ref[...] * (sq * (1.0 + hq * (1.0 - sq)))
        dout_ref[:, HG_WIDTH:2 * HG_WIDTH] = df * (1.0 - lb) * (s * (1.0 - s))
        dout_ref[:, 2 * HG_WIDTH:3 * HG_WIDTH] = dv_ref[...]
        dlb_ref[...] += _colsum(df * (1.0 - s))

    col = lambda c: pl.BlockSpec((tb, HG_WIDTH), lambda i, c=c: (n_t - 1 - i, c))
    return pl.pallas_call(
        body, grid=(n_t,), name="hg_bwd",
        in_specs=[col(0), col(1), col(2), _full_spec((2, HG_WIDTH)),
                  pl.BlockSpec((tb, HG_WIDTH), lambda i: (n_t - 1 - i, 0)),
                  pl.BlockSpec((n_sub, HG_HEADS, HG_DK, HG_DK), lambda i: (n_t - 1 - i, 0, 0, 0))],
        out_specs=[pl.BlockSpec((tb, 3 * HG_WIDTH), lambda i: (n_t - 1 - i, 0)), _full_spec((1, HG_WIDTH))],
        out_shape=[jax.ShapeDtypeStruct((T, 3 * HG_WIDTH), F32), jax.ShapeDtypeStruct((1, HG_WIDTH), F32)],
        scratch_shapes=[pltpu.VMEM((HG_HEADS, HG_DK, HG_DK), F32)] + [pltpu.VMEM((tb, HG_WIDTH), F32)] * 7,
        compiler_params=_params(("arbitrary",)),
    )(proj, proj, proj, gam, d_o, hist)


def _sb_masks():
    row = lax.broadcasted_iota(jnp.int32, (SB_TK, SB_TK), 0)
    col = lax.broadcasted_iota(jnp.int32, (SB_TK, SB_TK), 1)
    suffix = (row >= col).astype(BF16)
    prefix = (row <= col).astype(BF16)
    query = lax.broadcasted_iota(jnp.int32, (SB_TQ, SB_TK), 0)
    key = lax.broadcasted_iota(jnp.int32, (SB_TQ, SB_TK), 1)
    causal = [key + r * SB_TK < query for r in range(SB_TQ // SB_TK)]
    lane = lax.broadcasted_iota(jnp.int32, (SB_TQ, SB_BLOCK), 1)
    return suffix, prefix, causal, lane


def _sum_right(a, suffix):
    return _dot(a.astype(BF16), suffix)


def _per_lane_tile(t, c, op):
    return jnp.concatenate([op(t[:, n:n + SB_BLOCK], c) for n in range(0, t.shape[1], SB_BLOCK)], axis=1)


def _sb_block(qb, kj, suffix, causal, c):
    z = _dot_nt(qb, kj)
    sp = jnp.maximum(z, 0.0) + jnp.log(1.0 + jnp.exp(-jnp.abs(z)))
    if causal is not None:
        sp = jnp.where(causal, sp, 0.0)
    big_l = _sum_right(sp, suffix)
    a = jnp.exp(_per_lane_tile(z - big_l, c, jnp.subtract))
    if causal is not None:
        a = jnp.where(causal, a, 0.0)
    return z, a, big_l


def _sb_tile_copies(to_hbm, bufs, stores, sems, slot, head, i, j, n_chain):
    copies = []
    for ch in range(n_chain):
        for w in range(2):
            vmem, hbm = bufs[w].at[slot, ch], stores[w].at[head + ch, i, j]
            src, dst = (vmem, hbm) if to_hbm else (hbm, vmem)
            copies.append(pltpu.make_async_copy(src, dst, sems.at[slot, ch, w]))
    return copies


def _half_masked(pair, lane, scale=1.0):
    pair = pair.astype(F32) * scale
    return jnp.where(lane < SB_DH, pair, 0.0).astype(BF16), jnp.where(lane < SB_DH, 0.0, pair).astype(BF16)


def _sb_fwd(qkv):
    T = qkv.shape[0]
    width = SB_PP * SB_BLOCK
    n_chain = 2 * SB_PP
    n_sub = SB_TQ // SB_TK

    def body(q_ref, k_ref, v_ref, o_ref, a_st, s_st, acc_ref, c_ref, qm, a_buf, s_buf, sems):
        g = pl.program_id(0)
        i = pl.program_id(1)
        suffix, _, causal, lane = _sb_masks()
        for pp in range(SB_PP):
            qm[2 * pp], qm[2 * pp + 1] = _half_masked(q_ref[:, pp * SB_BLOCK:(pp + 1) * SB_BLOCK], lane, SB_DH ** -0.5)
        acc_ref[...] = jnp.zeros_like(acc_ref)
        c_ref[...] = jnp.zeros_like(c_ref)
        a_buf[...] = jnp.zeros_like(a_buf)
        s_buf[...] = jnp.zeros_like(s_buf)

        def copies(slot, j):
            return _sb_tile_copies(True, (a_buf, s_buf), (a_st, s_st), sems, slot, g * n_chain, i, j, n_chain)

        def step(j, slot, r=None, reuse=True):
            rows = pl.ds(pl.multiple_of(j * SB_TK, SB_TK), SB_TK)
            qs = slice(0 if r is None else r * SB_TK, SB_TQ)
            mask = None if r is None else causal[r][qs]
            if reuse:
                for cp in copies(slot, j):
                    cp.wait()
            for ch in range(n_chain):
                pair = slice((ch // 2) * SB_BLOCK, (ch // 2 + 1) * SB_BLOCK)
                c = c_ref[ch, qs]
                z, a, big_l = _sb_block(qm[ch, qs], k_ref[rows, pair], suffix, mask, c)
                ab = a.astype(BF16)
                acc_ref[ch, qs] += _dot(ab, v_ref[rows, pair])
                a_buf[slot, ch, qs] = ab
                s_buf[slot, ch, qs] = z.astype(BF16)
                c_ref[ch, qs] = c + jnp.broadcast_to(big_l[:, 0:1], c.shape)
            for cp in copies(slot, j):
                cp.start()

        first = i * n_sub
        for s, r in enumerate(reversed(range(n_sub))):
            step(first + r, s % 2, r, reuse=s >= 2)

        @pl.loop(0, first)
        def _(t):
            step(first - 1 - t, (n_sub + t) % 2)

        for slot in range(2):
            for cp in copies(slot, 0):
                cp.wait()
        for pp in range(SB_PP):
            o_ref[:, pp * SB_BLOCK:(pp + 1) * SB_BLOCK] = jnp.where(lane < SB_DH, acc_ref[2 * pp], acc_ref[2 * pp + 1])

    assert n_sub >= 2 and n_sub % 2 == 0
    n_g = SB_WIDTH // width
    blk = lambda part: pl.BlockSpec((SB_TQ, width), lambda g, i, part=part: (i, part * n_g + g))
    whole = lambda part: pl.BlockSpec((T, width), lambda g, i, part=part: (0, part * n_g + g))
    tiles = jax.ShapeDtypeStruct((SB_HEADS, T // SB_TQ, T // SB_TK, SB_TQ, SB_TK), BF16)
    return pl.pallas_call(
        body, grid=(n_g, T // SB_TQ), name="sb_fwd",
        in_specs=[blk(0), whole(1), whole(2)],
        out_specs=[blk(0), pl.BlockSpec(memory_space=pl.ANY), pl.BlockSpec(memory_space=pl.ANY)],
        out_shape=[jax.ShapeDtypeStruct((T, SB_WIDTH), F32), tiles, tiles],
        scratch_shapes=[pltpu.VMEM((n_chain, SB_TQ, SB_BLOCK), F32), pltpu.VMEM((n_chain, SB_TQ, SB_BLOCK), F32),
                        pltpu.VMEM((n_chain, SB_TQ, SB_BLOCK), BF16),
                        pltpu.VMEM((2, n_chain, SB_TQ, SB_TK), BF16), pltpu.VMEM((2, n_chain, SB_TQ, SB_TK), BF16),
                        pltpu.SemaphoreType.DMA((2, n_chain, 2))],
        compiler_params=_params(("parallel", "arbitrary")),
    )(qkv, qkv, qkv)


def _sb_bwd(qkv, a_st, s_st, d_o, after):
    T = qkv.shape[0]
    width = SB_PP * SB_BLOCK
    n_chain = 2 * SB_PP
    n_sub = SB_TQ // SB_TK
    scale = SB_DH ** -0.5

    def body(q_ref, k_ref, v_ref, a_st_ref, s_st_ref, do_ref, after_ref, dq_ref, dk_ref, dv_ref, acc_ref, gc_ref, qm, dom,
             a_buf, s_buf, sems):
        g_idx = pl.program_id(0)
        i = pl.program_id(1)

        @pl.when(i == 0)
        def _():
            dk_ref[...] = jnp.zeros_like(dk_ref)
            dv_ref[...] = jnp.zeros_like(dv_ref)

        _, prefix, causal, lane = _sb_masks()
        for pp in range(SB_PP):
            pair = slice(pp * SB_BLOCK, (pp + 1) * SB_BLOCK)
            qm[2 * pp], qm[2 * pp + 1] = _half_masked(q_ref[:, pair], lane, scale)
            dom[2 * pp], dom[2 * pp + 1] = _half_masked(do_ref[:, pair], lane)
        acc_ref[...] = jnp.zeros_like(acc_ref)
        gc_ref[...] = jnp.zeros_like(gc_ref)

        def copies(slot, j):
            return _sb_tile_copies(False, (a_buf, s_buf), (a_st_ref, s_st_ref), sems, slot, g_idx * n_chain, i, j, n_chain)

        def step(j, slot, r=None, last=False):
            rows = pl.ds(pl.multiple_of(j * SB_TK, SB_TK), SB_TK)
            qs = slice(0 if r is None else r * SB_TK, SB_TQ)
            mask = None if r is None else causal[r][qs]
            for cp in copies(slot, j):
                cp.wait()
            if not last:
                for cp in copies(1 - slot, j + 1):
                    cp.start()
            for pp in range(SB_PP):
                pair = slice(pp * SB_BLOCK, (pp + 1) * SB_BLOCK)
                kj = k_ref[rows, pair]
                vj = v_ref[rows, pair]
                dk = jnp.zeros((SB_TK, SB_BLOCK), F32)
                dv = jnp.zeros((SB_TK, SB_BLOCK), F32)
                for ch in (2 * pp, 2 * pp + 1):
                    ab = a_buf[slot, ch, qs]
                    g = ab.astype(F32) * _dot_nt(dom[ch, qs], vj)
                    g_left = _sum_right(g, prefix)
                    gc = gc_ref[ch, qs]
                    dz = g - _sigmoid(s_buf[slot, ch, qs].astype(F32)) * _per_lane_tile(g_left, gc, jnp.add)
                    if mask is not None:
                        dz = jnp.where(mask, dz, 0.0)
                    dzb = dz.astype(BF16)
                    dk = dk + _dot_tn(dzb, qm[ch, qs])
                    dv = dv + _dot_tn(ab, dom[ch, qs])
                    acc_ref[ch, qs] += _dot(dzb, kj)
                    gc_ref[ch, qs] = gc + jnp.broadcast_to(g_left[:, SB_TK - 1:SB_TK], gc.shape)
                dk_ref[rows, pair] += dk
                dv_ref[rows, pair] += dv

        first = i * n_sub
        for cp in copies(0, 0):
            cp.start()

        @pl.loop(0, first)
        def _(j):
            step(j, j % 2)

        for r in range(n_sub):
            step(first + r, r % 2, r, last=r == n_sub - 1)
        for pp in range(SB_PP):
            dq_ref[:, pp * SB_BLOCK:(pp + 1) * SB_BLOCK] = scale * jnp.where(lane < SB_DH, acc_ref[2 * pp],
                                                                             acc_ref[2 * pp + 1])
    n_g = SB_WIDTH // width
    blk = lambda part: pl.BlockSpec((SB_TQ, width), lambda g, i, part=part: (i, part * n_g + g))
    whole = lambda part: pl.BlockSpec((T, width), lambda g, i, part=part: (0, part * n_g + g))
    return pl.pallas_call(
        body, grid=(n_g, T // SB_TQ), name="sb_bwd",
        in_specs=[blk(0), whole(1), whole(2), pl.BlockSpec(memory_space=pl.ANY), pl.BlockSpec(memory_space=pl.ANY), blk(0),
                  pl.BlockSpec(memory_space=pl.ANY)],
        out_specs=[blk(0), whole(0), whole(0)],
        out_shape=[jax.ShapeDtypeStruct((T, SB_WIDTH), F32)] * 3,
        scratch_shapes=[pltpu.VMEM((n_chain, SB_TQ, SB_BLOCK), F32), pltpu.VMEM((n_chain, SB_TQ, SB_BLOCK), F32),
                        pltpu.VMEM((n_chain, SB_TQ, SB_BLOCK), BF16), pltpu.VMEM((n_chain, SB_TQ, SB_BLOCK), BF16),
                        pltpu.VMEM((2, n_chain, SB_TQ, SB_TK), BF16), pltpu.VMEM((2, n_chain, SB_TQ, SB_TK), BF16),
                        pltpu.SemaphoreType.DMA((2, n_chain, 2))],
        compiler_params=_params(("parallel", "arbitrary")),
    )(qkv, qkv, qkv, a_st, s_st, d_o, after)


HBM = pl.BlockSpec(memory_space=pltpu.HBM)
MESH = pl.DeviceIdType.MESH


def _place():
    return lax.axis_index("x"), lax.axis_index("y"), lax.axis_index("c")


def _all_gather(block):
    rows, cols = block.shape

    def body(x_ref, out_ref, send_sems, recv_sems, local_sem):
        x, y, c = _place()
        me, sibling = (x, y, c), (x, y, 1 - c)
        chips = [(1 - x, y), (x, 1 - y), (1 - x, 1 - y)]

        def slot(px, py, pc):
            return out_ref.at[4 * px + 2 * py + pc]

        def copy(k, blk, to, src=None):
            return pltpu.make_async_remote_copy(
                src_ref=slot(*blk) if src is None else src, dst_ref=slot(*blk),
                send_sem=send_sems.at[k], recv_sem=recv_sems.at[k], device_id=to, device_id_type=MESH)

        mine = pltpu.make_async_copy(x_ref, slot(*me), local_sem)
        mine.start()
        first = [copy(0, me, sibling, src=x_ref)]
        first += [copy(1 + j, me, (*chip, c), src=x_ref) for j, chip in enumerate(chips)]
        for cp in first:
            cp.start()
        passed = [copy(4 + j, (*chip, c), sibling) for j, chip in enumerate(chips)]
        for j, chip in enumerate(chips):
            copy(1 + j, (*chip, c), me).wait_recv()
            passed[j].start()
        copy(0, sibling, me).wait_recv()
        for j, chip in enumerate(chips):
            copy(4 + j, (*chip, 1 - c), me).wait_recv()
        for cp in first + passed:
            cp.wait_send()
        mine.wait()

    return pl.pallas_call(
        body, name="all_gather",
        out_shape=jax.ShapeDtypeStruct((N_DEV, rows, cols), block.dtype),
        in_specs=[HBM], out_specs=HBM,
        scratch_shapes=[pltpu.SemaphoreType.DMA((7,)), pltpu.SemaphoreType.DMA((7,)), pltpu.SemaphoreType.DMA],
    )(block)


SEM = pl.BlockSpec(memory_space=pltpu.SEMAPHORE)
ANY = pl.BlockSpec(memory_space=pl.ANY)
SPLIT_EFFECT = pltpu.SideEffectType.DATAFLOW_SIDE_EFFECTING
N_PEER = N_DEV - 1


def _flow_copies(kind, src_ref, land_ref, send_sems, recv_sems):
    x, y, c = _place()
    copies = []
    for r in range(1, N_DEV):
        px = 1 - x if r & 4 else x
        py = 1 - y if r & 2 else y
        pc = 1 - c if r & 1 else c
        if kind == "gather":
            src, dst = src_ref, land_ref.at[4 * x + 2 * y + c]
        else:
            src, dst = src_ref.at[4 * px + 2 * py + pc], land_ref.at[r - 1]
        copies.append(pltpu.make_async_remote_copy(
            src_ref=src, dst_ref=dst, send_sem=send_sems.at[r - 1], recv_sem=recv_sems.at[r - 1],
            device_id=(px, py, pc), device_id_type=MESH))
    return copies


def _landing(kind, src):
    if kind == "gather":
        return lax.empty((N_DEV,) + src.shape, src.dtype)
    return lax.empty((N_PEER,) + src.shape[1:], src.dtype)


def _push_start(kinds, arrays, after, name):
    n = len(arrays)

    def body(*refs):
        ins, sems, token = refs[:n], refs[n + 1:2 * n + 1], refs[3 * n + 1]
        for f, kind in enumerate(kinds):
            for cp in _flow_copies(kind, ins[2 * f], ins[2 * f + 1], sems[2 * f], sems[2 * f + 1]):
                cp.start()
        token[...] = jnp.zeros_like(token)

    outs = pl.pallas_call(
        body, name=name,
        out_shape=[pltpu.SemaphoreType.DMA((N_PEER,))] * n + [pltpu.HBM(a.shape, a.dtype) for a in arrays]
        + [jax.ShapeDtypeStruct((8, 128), F32)],
        in_specs=[HBM] * n + [ANY], out_specs=[SEM] * n + [HBM] * n + [pl.BlockSpec(memory_space=pltpu.VMEM)],
        input_output_aliases={i: n + i for i in range(n)},
        compiler_params=pltpu.CompilerParams(has_side_effects=SPLIT_EFFECT),
    )(*[pltpu.with_memory_space_constraint(a, pltpu.HBM) for a in arrays], after)
    return outs[:n], outs[n:2 * n], outs[2 * n]


def _push_wait(kinds, sems, arrays, after, name):
    n = len(arrays)

    def body(*refs):
        ins, sem_refs = refs[:n], refs[n:2 * n]
        for f, kind in enumerate(kinds):
            for cp in _flow_copies(kind, ins[2 * f], ins[2 * f + 1], sem_refs[2 * f], sem_refs[2 * f + 1]):
                cp.wait_send()
                cp.wait_recv()

    outs = pl.pallas_call(
        body, name=name,
        out_shape=[pltpu.HBM(a.shape, a.dtype) for a in arrays],
        in_specs=[HBM] * n + [SEM] * n + [ANY], out_specs=[HBM] * n,
        input_output_aliases={i: i for i in range(n)},
        compiler_params=pltpu.CompilerParams(has_side_effects=SPLIT_EFFECT),
    )(*arrays, *sems, after)
    return outs


def _sum_blocks(own, got, name):
    n, rows, cols = got.shape
    tr = next(t for t in (128, 64, 32) if rows % t == 0)

    def body(own_ref, got_ref, out_ref):
        acc = own_ref[...].astype(F32)
        for r in range(n):
            acc = acc + got_ref[r].astype(F32)
        out_ref[...] = acc

    return pl.pallas_call(
        body, grid=(rows // tr,), name=name,
        in_specs=[pl.BlockSpec((tr, cols), lambda i: (i, 0)), pl.BlockSpec((n, tr, cols), lambda i: (0, i, 0))],
        out_specs=pl.BlockSpec((tr, cols), lambda i: (i, 0)),
        out_shape=jax.ShapeDtypeStruct((rows, cols), F32),
        compiler_params=_params(("parallel",)),
    )(own, got)


def _adamw_math(w, g, m, v):
    m = ADAM_B1 * m + (1.0 - ADAM_B1) * g
    v = ADAM_B2 * v + (1.0 - ADAM_B2) * (g * g)
    m_hat = m / (1.0 - ADAM_B1 ** ADAM_STEP)
    v_hat = v / (1.0 - ADAM_B2 ** ADAM_STEP)
    delta = -ADAM_LR * (m_hat / (jnp.sqrt(v_hat) + ADAM_EPS) + ADAM_WD * w)
    return delta, m, v


def _adamw(w, g, m, v, name):
    rows, cols = w.shape
    tr = rows if rows <= 352 else 256

    def body(w_ref, g_ref, m_ref, v_ref, d_out, m_out, v_out):
        d_out[...], m_out[...], v_out[...] = _adamw_math(w_ref[...], g_ref[...], m_ref[...], v_ref[...])

    spec = pl.BlockSpec((tr, cols), lambda i: (i, 0))
    return pl.pallas_call(
        body, grid=(rows // tr,), name=name,
        in_specs=[spec] * 4, out_specs=[spec] * 3,
        out_shape=[jax.ShapeDtypeStruct((rows, cols), F32)] * 3,
        compiler_params=_params(("parallel",)),
    )(w, g, m, v)


ROW_ATTN_PRE, ROW_ATTN_POST, ROW_FFN_PRE, ROW_FFN_POST, ROW_OUT_NORMS, ROW_GAMMA, ROW_LOSS = range(7)


def _small_update(smalls, w, m, v):
    def body(s_ref, w_ref, m_ref, v_ref, g_out, d_out, m_out, v_out):
        g = s_ref[0]
        for j in range(1, N_DEV):
            g = g + s_ref[j]
        wv = w_ref[...]
        g0 = wv[ROW_GAMMA:ROW_GAMMA + 1, 0:HG_WIDTH]
        g1 = wv[ROW_GAMMA:ROW_GAMMA + 1, HG_WIDTH:]
        mx = jnp.maximum(g0, g1)
        e0 = jnp.exp(g0 - mx)
        e1 = jnp.exp(g1 - mx)
        lb = e0 / (e0 + e1)
        dg0 = g[ROW_GAMMA:ROW_GAMMA + 1, 0:HG_WIDTH] * lb * (1.0 - lb)
        row = lax.broadcasted_iota(jnp.int32, g.shape, 0)
        g = jnp.where(row == ROW_GAMMA, jnp.concatenate([dg0, -dg0], axis=1), g)
        g_out[...] = g
        d_out[...], m_out[...], v_out[...] = _adamw_math(wv, g, m_ref[...], v_ref[...])

    return pl.pallas_call(
        body, name="small_update",
        out_shape=[jax.ShapeDtypeStruct((8, D_MODEL), F32)] * 4,
    )(smalls, w, m, v)


def _local_step(x, p, target, g_attn_pre, w_in_t, gamma, g_hg, g_sb, g_attn_post, g_ffn_pre, g_ffn_post, later_weights,
                grads_ready):
    proj, u, qkv = _in_proj(x, g_attn_pre, w_in_t)
    o_hg, hist = _hg_fwd(proj, gamma)
    o_sb, a_st, s_st = _sb_fwd(qkv)
    w_out, w_gu_t, w_down, w_pp_t, w_pg = later_weights(o_sb)
    cat, mix, h1 = _out_proj(o_hg, proj, o_sb, x, g_hg, g_sb, g_attn_post, w_out)
    u2, gu, y, h2 = _ffn_fwd(h1, g_ffn_pre, w_gu_t, w_down, g_ffn_post)
    dh2, de, dg, loss = _ple_loss(h2, p, target, w_pp_t, w_pg)

    d_wpp_t = _wgrad(de, p, "wgrad_ple_proj")
    d_wpg = _wgrad(h2, dg, "wgrad_ple_gate")
    dy, a, dgu, dh1, dg_ffn_post, dg_ffn_pre = _ffn_bwd(dh2, y, h1, gu, g_ffn_post, g_ffn_pre, w_gu_t, w_down)
    d_wdown = _wgrad(a, dy, "wgrad_down")
    sent = grads_ready("ffn", (_wgrad(dgu, u2, "wgrad_gate_up"), d_wdown, d_wpp_t, d_wpg))
    dmix, d_ohg, d_hg, d_osb, dg_attn_post, dg_hg, dg_sb = _out_bwd(dh1, mix, o_hg, proj, o_sb, g_hg, g_sb, g_attn_post,
                                                                    w_out, sent)
    sent = grads_ready("out", (_wgrad(cat, dmix, "wgrad_out"),))
    dq, dk, dv = _sb_bwd(qkv, a_st, s_st, d_osb, sent)
    d_hgrn, d_lb = _hg_bwd(proj, gamma, d_ohg, hist)
    dproj, dx, dg_attn_pre = _in_bwd(d_hgrn, d_hg, dq, dk, dv, x, dh1, g_attn_pre, w_in_t)
    grads_ready("in", (_wgrad(dproj, u, "wgrad_in"),))
    return loss, dx, (dg_attn_pre, dg_attn_post, dg_ffn_pre, dg_ffn_post, dg_hg, dg_sb, d_lb)


def _pack_row(*parts):
    return jnp.concatenate([q.reshape(1, -1) for q in parts], axis=1)


def kernel(x, p, attn_pre_norm, w_in, hg_lower_gamma, hg_out_norm, sb_out_norm, w_out, attn_post_norm, ffn_pre_norm, w_gate_up, w_down, ffn_post_norm, ple_proj, ple_gate, loss_target, m_attn_pre_norm, m_w_in, m_hg_lower_gamma, m_hg_out_norm, m_sb_out_norm, m_w_out, m_attn_post_norm, m_ffn_pre_norm, m_w_gate_up, m_w_down, m_ffn_post_norm, m_ple_proj, m_ple_gate, v_attn_pre_norm, v_w_in, v_hg_lower_gamma, v_hg_out_norm, v_sb_out_norm, v_w_out, v_attn_post_norm, v_ffn_pre_norm, v_w_gate_up, v_w_down, v_ffn_post_norm, v_ple_proj, v_ple_gate):
    big = (w_in[0], w_out[0], w_gate_up[0], w_down[0], ple_proj[0], ple_gate[0])
    big_m = (m_w_in[0], m_w_out[0], m_w_gate_up[0], m_w_down[0], m_ple_proj[0], m_ple_gate[0])
    big_v = (v_w_in[0], v_w_out[0], v_w_gate_up[0], v_w_down[0], v_ple_proj[0], v_ple_gate[0])
    names = ("w_in", "w_out", "w_gate_up", "w_down", "ple_proj", "ple_gate")

    by_columns = (True, False, True, False, True, False)
    shards = [(w.T if t else w).astype(BF16) for w, t in zip(big, by_columns)]
    w_in_t = _all_gather(shards[0]).reshape(IN_COLS, D_MODEL)
    me = 4 * lax.axis_index("x") + 2 * lax.axis_index("y") + lax.axis_index("c")
    later = shards[1:]
    w_sems, w_thru, token = _push_start(
        ("gather",) * len(later), [a for s in later for a in (s, _landing("gather", s))], w_in_t, "weights_start")

    def later_weights(after):
        done = _push_wait(("gather",) * len(later), w_sems, w_thru, after, "weights_wait")
        whole = [lax.dynamic_update_index_in_dim(got, mine, me, 0) for mine, got in zip(done[0::2], done[1::2])]
        return (whole[0].reshape(D_MODEL, D_MODEL), whole[1].reshape(2, D_FF, D_MODEL), whole[2].reshape(D_FF, D_MODEL),
                whole[3].reshape(D_MODEL, PLE_DIM), whole[4].reshape(D_MODEL, D_MODEL))

    flights = {}

    def by_owner(g):
        return g.reshape(N_DEV, -1, g.shape[-1])

    def grads_ready(group, grads):
        blocks = [by_owner(g) for g in grads]
        if group == "in":
            flights[group] = blocks[0]
            return None
        sems, thru, sent = _push_start(
            ("scatter",) * len(blocks), [a for b in blocks for a in (b, _landing("scatter", b))], token,
            "grads_" + group + "_start")
        flights[group] = (sems, thru)
        return sent

    loss, dx, smalls = _local_step(
        x[0], p[0, 0], loss_target[0], attn_pre_norm + token[0:1, 0:1], w_in_t,
        hg_lower_gamma, hg_out_norm, sb_out_norm, attn_post_norm, ffn_pre_norm, ffn_post_norm, later_weights, grads_ready)
    dg_attn_pre, dg_attn_post, dg_ffn_pre, dg_ffn_post, dg_hg, dg_sb, d_lb = smalls

    zeros_half = jnp.zeros((1, HG_WIDTH), F32)
    small_pack = jnp.concatenate([
        dg_attn_pre, dg_attn_post, dg_ffn_pre, dg_ffn_post, _pack_row(dg_hg, dg_sb), _pack_row(d_lb, zeros_half),
        jnp.broadcast_to(loss[0:1, 0:1], (1, D_MODEL)), jnp.zeros((1, D_MODEL), F32)], axis=0)
    in_blocks = flights["in"]
    in_sems, in_thru, in_token = _push_start(
        ("scatter", "gather"),
        (in_blocks, _landing("scatter", in_blocks), small_pack, _landing("gather", small_pack)), token,
        "grads_in_start")
    ffn = _push_wait(("scatter",) * 4, *flights["ffn"], in_token, "grads_ffn_wait")
    sent_out, got_out = _push_wait(("scatter",), *flights["out"], ffn[1], "grads_out_wait")
    sent_in, got_in, my_small, all_smalls = _push_wait(("scatter", "gather"), in_sems, in_thru, got_out, "grads_in_wait")
    all_smalls = lax.dynamic_update_index_in_dim(all_smalls, my_small, me, 0)

    def reduced(sent, got, name):
        return _sum_blocks(lax.dynamic_index_in_dim(sent, me, 0, keepdims=False), got, "grad_sum_" + name)

    sums = (reduced(sent_in, got_in, "w_in"), reduced(sent_out, got_out, "w_out"), reduced(ffn[0], ffn[1], "w_gate_up"),
            reduced(ffn[2], ffn[3], "w_down"), reduced(ffn[4], ffn[5], "ple_proj"), reduced(ffn[6], ffn[7], "ple_gate"))

    out_g, out_d, out_m, out_v = {}, {}, {}, {}
    for i, name in enumerate(names):
        g = sums[i].T if by_columns[i] else sums[i]
        out_g[name] = g[None]
        d, m, v = _adamw(big[i], g, big_m[i], big_v[i], "adamw_" + name)
        out_d[name], out_m[name], out_v[name] = d[None], m[None], v[None]

    def small_rows(pre, gam, hg, sb, post, fpre, fpost):
        return jnp.concatenate([pre, post, fpre, fpost, _pack_row(hg, sb), _pack_row(gam[0], gam[1]),
                                jnp.zeros((2, D_MODEL), F32)], axis=0)

    packs = _small_update(
        all_smalls,
        small_rows(attn_pre_norm, hg_lower_gamma, hg_out_norm, sb_out_norm, attn_post_norm, ffn_pre_norm, ffn_post_norm),
        small_rows(m_attn_pre_norm, m_hg_lower_gamma, m_hg_out_norm, m_sb_out_norm, m_attn_post_norm, m_ffn_pre_norm,
                   m_ffn_post_norm),
        small_rows(v_attn_pre_norm, v_hg_lower_gamma, v_hg_out_norm, v_sb_out_norm, v_attn_post_norm, v_ffn_pre_norm,
                   v_ffn_post_norm))

    def unpack(pk):
        return {
            "attn_pre_norm": pk[ROW_ATTN_PRE:ROW_ATTN_PRE + 1],
            "hg_lower_gamma": pk[ROW_GAMMA].reshape(2, HG_WIDTH),
            "hg_out_norm": pk[ROW_OUT_NORMS:ROW_OUT_NORMS + 1, :HG_WIDTH],
            "sb_out_norm": pk[ROW_OUT_NORMS:ROW_OUT_NORMS + 1, HG_WIDTH:],
            "attn_post_norm": pk[ROW_ATTN_POST:ROW_ATTN_POST + 1],
            "ffn_pre_norm": pk[ROW_FFN_PRE:ROW_FFN_PRE + 1],
            "ffn_post_norm": pk[ROW_FFN_POST:ROW_FFN_POST + 1],
        }

    sg, sd, sm, sv = (unpack(pk) for pk in packs)
    out_g.update(sg), out_d.update(sd), out_m.update(sm), out_v.update(sv)
    order = ("attn_pre_norm", "w_in", "hg_lower_gamma", "hg_out_norm", "sb_out_norm", "w_out", "attn_post_norm",
             "ffn_pre_norm", "w_gate_up", "w_down", "ffn_post_norm", "ple_proj", "ple_gate")
    total_loss = packs[0][ROW_LOSS, 0]
    return (total_loss, dx[None], *[out_g[n] for n in order], *[out_d[n] for n in order],
            *[out_m[n] for n in order], *[out_v[n] for n in order])
```

```python
import functools

import jax
import jax.numpy as jnp
from jax import lax
from jax.experimental import pallas as pl
from jax.experimental.pallas import tpu as pltpu

F32 = jnp.float32
BF16 = jnp.bfloat16

D_MODEL = 1024
HG_WIDTH = 512
HG_HEADS = 4
HG_DK = 128
SB_WIDTH = 512
SB_HEADS = 8
SB_DH = 64
SB_BLOCK = 128
SB_PP = 2
SB_TQ = 1024
SB_TK = 256
D_FF = 2816
PLE_DIM = 256
IN_COLS = 4 * HG_WIDTH + 3 * SB_WIDTH
EPS = 1e-6
N_DEV = 8

HG_SUB = 16
HG_TILE = 128
FFN_TF = 1408
ROW_TILE = 256
VMEM_LIMIT = 56 * 1024 * 1024
WGRAD_ACC_BYTES = 8 * 1024 * 1024

ADAM_LR = 0.001
ADAM_B1 = 0.9
ADAM_B2 = 0.999
ADAM_EPS = 1e-08
ADAM_WD = 0.01
ADAM_STEP = 10

NT_DIMS = (((1,), (1,)), ((), ()))
TN_DIMS = (((0,), (0,)), ((), ()))


def _params(semantics):
    return pltpu.CompilerParams(dimension_semantics=semantics, vmem_limit_bytes=VMEM_LIMIT)


def _dot(a, b):
    return jnp.dot(a, b, preferred_element_type=F32)


def _dot_nt(a, b):
    return lax.dot_general(a, b, NT_DIMS, preferred_element_type=F32)


def _dot_tn(a, b):
    return lax.dot_general(a, b, TN_DIMS, preferred_element_type=F32)


def _sigmoid(z):
    return 1.0 / (1.0 + jnp.exp(-z))


def _rstd(xv):
    return lax.rsqrt(jnp.mean(xv * xv, axis=-1, keepdims=True) + EPS)


def _rms_bwd(xv, r, g, dn):
    xn = xv * r
    gh = dn * g
    dx = r * (gh - xn * jnp.mean(gh * xn, axis=-1, keepdims=True))
    return dx, dn * xn


def _colsum(a):
    return jnp.sum(a, axis=0, keepdims=True)


def _row_spec(tm, width, col=0):
    return pl.BlockSpec((tm, width), lambda i, col=col: (i, col))


def _full_spec(shape):
    return pl.BlockSpec(shape, lambda i: (0,) * len(shape))


def _in_proj(x, g, w_t):
    T = x.shape[0]
    tm = ROW_TILE

    def body(x_ref, g_ref, w_ref, proj_ref, u_ref, qkv_ref):
        xv = x_ref[...]
        u = (xv * _rstd(xv) * g_ref[...]).astype(BF16)
        u_ref[...] = u
        proj = _dot_nt(u, w_ref[...])
        proj_ref[...] = proj
        qkv_ref[...] = proj[:, 4 * HG_WIDTH:].astype(BF16)

    return pl.pallas_call(
        body, grid=(T // tm,), name="in_proj",
        in_specs=[_row_spec(tm, D_MODEL), _full_spec((1, D_MODEL)), _full_spec((IN_COLS, D_MODEL))],
        out_specs=[_row_spec(tm, IN_COLS), _row_spec(tm, D_MODEL), _row_spec(tm, 3 * SB_WIDTH)],
        out_shape=[jax.ShapeDtypeStruct((T, IN_COLS), F32), jax.ShapeDtypeStruct((T, D_MODEL), BF16),
                   jax.ShapeDtypeStruct((T, 3 * SB_WIDTH), BF16)],
        compiler_params=_params(("parallel",)),
    )(x, g, w_t)


def _out_proj(o_hg, proj, o_sb, x, g_hg, g_sb, g_post, w):
    T = x.shape[0]
    tm = ROW_TILE

    def body(ohg_ref, hg_ref, osb_ref, x_ref, ghg_ref, gsb_ref, gpost_ref, w_ref, cat_ref, mix_ref, h1_ref):
        ohg = ohg_ref[...]
        hg = hg_ref[...]
        osb = osb_ref[...]
        a = ohg * _rstd(ohg) * ghg_ref[...] * (hg * _sigmoid(hg))
        n2 = osb * _rstd(osb) * gsb_ref[...]
        cat = jnp.concatenate([a, n2], axis=1).astype(BF16)
        cat_ref[...] = cat
        mix = _dot(cat, w_ref[...])
        mix_ref[...] = mix
        h1_ref[...] = x_ref[...] + mix * _rstd(mix) * gpost_ref[...]

    return pl.pallas_call(
        body, grid=(T // tm,), name="out_proj",
        in_specs=[_row_spec(tm, HG_WIDTH), _row_spec(tm, HG_WIDTH, 3), _row_spec(tm, SB_WIDTH), _row_spec(tm, D_MODEL),
                  _full_spec((1, HG_WIDTH)), _full_spec((1, SB_WIDTH)), _full_spec((1, D_MODEL)),
                  _full_spec((D_MODEL, D_MODEL))],
        out_specs=[_row_spec(tm, D_MODEL)] * 3,
        out_shape=[jax.ShapeDtypeStruct((T, D_MODEL), BF16), jax.ShapeDtypeStruct((T, D_MODEL), F32),
                   jax.ShapeDtypeStruct((T, D_MODEL), F32)],
        compiler_params=_params(("parallel",)),
    )(o_hg, proj, o_sb, x, g_hg, g_sb, g_post, w)


def _ffn_fwd(h1, g_pre, w_gu_t, w_down, g_post):
    T = h1.shape[0]
    tm = 2 * ROW_TILE
    n_f = D_FF // FFN_TF

    def body(h1_ref, gpre_ref, wgu_ref, wd_ref, gpost_ref, u2_ref, gu_ref, y_ref, h2_ref, acc_ref):
        j = pl.program_id(1)

        @pl.when(j == 0)
        def _():
            hv = h1_ref[...]
            u2_ref[...] = (hv * _rstd(hv) * gpre_ref[...]).astype(BF16)
            acc_ref[...] = jnp.zeros_like(acc_ref)

        u2 = u2_ref[...]
        gate = _dot_nt(u2, wgu_ref[0])
        up = _dot_nt(u2, wgu_ref[1])
        gu_ref[0] = gate.astype(BF16)
        gu_ref[1] = up.astype(BF16)
        a = (gate * _sigmoid(gate) * up).astype(BF16)
        acc_ref[...] += _dot(a, wd_ref[...])

        @pl.when(j == n_f - 1)
        def _():
            y = acc_ref[...]
            y_ref[...] = y
            h2_ref[...] = h1_ref[...] + y * _rstd(y) * gpost_ref[...]

    row = lambda width: pl.BlockSpec((tm, width), lambda i, j: (i, 0))
    vec = pl.BlockSpec((1, D_MODEL), lambda i, j: (0, 0))
    return pl.pallas_call(
        body, grid=(T // tm, n_f), name="ffn_fwd",
        in_specs=[row(D_MODEL), vec,
                  pl.BlockSpec((2, FFN_TF, D_MODEL), lambda i, j: (0, j, 0)),
                  pl.BlockSpec((FFN_TF, D_MODEL), lambda i, j: (j, 0)), vec],
        out_specs=[row(D_MODEL),
                   pl.BlockSpec((2, tm, FFN_TF), lambda i, j: (0, i, j)),
                   row(D_MODEL), row(D_MODEL)],
        out_shape=[jax.ShapeDtypeStruct((T, D_MODEL), BF16), jax.ShapeDtypeStruct((2, T, D_FF), BF16),
                   jax.ShapeDtypeStruct((T, D_MODEL), F32), jax.ShapeDtypeStruct((T, D_MODEL), F32)],
        scratch_shapes=[pltpu.VMEM((tm, D_MODEL), F32)],
        compiler_params=_params(("parallel", "arbitrary")),
    )(h1, g_pre, w_gu_t, w_down, g_post)


def _ple_loss(h2, p, target, w_pp_t, w_pg):
    T = h2.shape[0]
    tm = ROW_TILE

    def body(h2_ref, p_ref, t_ref, wpp_ref, wpg_ref, dh2_ref, de_ref, dg_ref, loss_ref):
        i = pl.program_id(0)
        h2v = h2_ref[...]
        e = _dot_nt(p_ref[...].astype(BF16), wpp_ref[...])
        sg = _sigmoid(_dot(h2v.astype(BF16), wpg_ref[...]))
        diff = h2v + e * sg - t_ref[...]
        part = jnp.sum(jnp.sum(diff * diff, axis=1, keepdims=True), axis=0, keepdims=True) * (0.5 / D_MODEL)

        @pl.when(i == 0)
        def _():
            loss_ref[...] = jnp.zeros_like(loss_ref)

        loss_ref[...] += jnp.broadcast_to(part, loss_ref.shape)
        dh3 = diff * (1.0 / D_MODEL)
        de_ref[...] = (dh3 * sg).astype(BF16)
        dg = (dh3 * e * sg * (1.0 - sg)).astype(BF16)
        dg_ref[...] = dg
        dh2_ref[...] = dh3 + _dot_nt(dg, wpg_ref[...])

    return pl.pallas_call(
        body, grid=(T // tm,), name="ple_loss",
        in_specs=[_row_spec(tm, D_MODEL), _row_spec(tm, PLE_DIM), _row_spec(tm, D_MODEL),
                  _full_spec((D_MODEL, PLE_DIM)), _full_spec((D_MODEL, D_MODEL))],
        out_specs=[_row_spec(tm, D_MODEL)] * 3 + [_full_spec((8, 128))],
        out_shape=[jax.ShapeDtypeStruct((T, D_MODEL), F32), jax.ShapeDtypeStruct((T, D_MODEL), BF16),
                   jax.ShapeDtypeStruct((T, D_MODEL), BF16), jax.ShapeDtypeStruct((8, 128), F32)],
        compiler_params=_params(("arbitrary",)),
    )(h2, p, target, w_pp_t, w_pg)


def _ffn_bwd(dh2, y, h1, gu, g_post, g_pre, w_gu_t, w_down):
    T = h1.shape[0]
    tm = ROW_TILE
    n_f = D_FF // FFN_TF

    def body(dh2_ref, y_ref, h1_ref, gu_ref, gpost_ref, gpre_ref, wgu_ref, wd_ref,
             dy_ref, a_ref, dgu_ref, dh1_ref, dgpost_ref, dgpre_ref, du2_ref):
        i = pl.program_id(0)
        j = pl.program_id(1)

        @pl.when((i == 0) & (j == 0))
        def _():
            dgpost_ref[...] = jnp.zeros_like(dgpost_ref)
            dgpre_ref[...] = jnp.zeros_like(dgpre_ref)

        @pl.when(j == 0)
        def _():
            yv = y_ref[...]
            dy, gterm = _rms_bwd(yv, _rstd(yv), gpost_ref[...], dh2_ref[...])
            dy_ref[...] = dy.astype(BF16)
            dgpost_ref[...] += _colsum(gterm)
            du2_ref[...] = jnp.zeros_like(du2_ref)

        da = _dot_nt(dy_ref[...], wd_ref[...])
        gate = gu_ref[0].astype(F32)
        up = gu_ref[1].astype(F32)
        s = _sigmoid(gate)
        sil = gate * s
        a_ref[...] = (sil * up).astype(BF16)
        dgate = (da * up * (s * (1.0 + gate * (1.0 - s)))).astype(BF16)
        dup = (da * sil).astype(BF16)
        dgu_ref[0] = dgate
        dgu_ref[1] = dup
        du2_ref[...] += _dot(dgate, wgu_ref[0]) + _dot(dup, wgu_ref[1])

        @pl.when(j == n_f - 1)
        def _():
            hv = h1_ref[...]
            dx, gterm = _rms_bwd(hv, _rstd(hv), gpre_ref[...], du2_ref[...])
            dh1_ref[...] = dh2_ref[...] + dx
            dgpre_ref[...] += _colsum(gterm)

    row = lambda width: pl.BlockSpec((tm, width), lambda i, j: (i, 0))
    col = pl.BlockSpec((tm, FFN_TF), lambda i, j: (i, j))
    both = pl.BlockSpec((2, tm, FFN_TF), lambda i, j: (0, i, j))
    vec = pl.BlockSpec((1, D_MODEL), lambda i, j: (0, 0))
    return pl.pallas_call(
        body, grid=(T // tm, n_f), name="ffn_bwd",
        in_specs=[row(D_MODEL), row(D_MODEL), row(D_MODEL), both, vec, vec,
                  pl.BlockSpec((2, FFN_TF, D_MODEL), lambda i, j: (0, j, 0)),
                  pl.BlockSpec((FFN_TF, D_MODEL), lambda i, j: (j, 0))],
        out_specs=[row(D_MODEL), col, both, row(D_MODEL), vec, vec],
        out_shape=[jax.ShapeDtypeStruct((T, D_MODEL), BF16), jax.ShapeDtypeStruct((T, D_FF), BF16),
                   jax.ShapeDtypeStruct((2, T, D_FF), BF16),
                   jax.ShapeDtypeStruct((T, D_MODEL), F32), jax.ShapeDtypeStruct((1, D_MODEL), F32),
                   jax.ShapeDtypeStruct((1, D_MODEL), F32)],
        scratch_shapes=[pltpu.VMEM((tm, D_MODEL), F32)],
        compiler_params=_params(("arbitrary", "arbitrary")),
    )(dh2, y, h1, gu, g_post, g_pre, w_gu_t, w_down)


def _out_bwd(dh1, mix, o_hg, proj, o_sb, g_hg, g_sb, g_post, w, after):
    T = dh1.shape[0]
    tm = ROW_TILE

    def body(dh1_ref, mix_ref, ohg_ref, hg_ref, osb_ref, ghg_ref, gsb_ref, gpost_ref, w_ref, after_ref,
             dmix_ref, dohg_ref, dhg_ref, dosb_ref, dgpost_ref, dghg_ref, dgsb_ref):
        i = pl.program_id(0)

        @pl.when(i == 0)
        def _():
            dgpost_ref[...] = jnp.zeros_like(dgpost_ref)
            dghg_ref[...] = jnp.zeros_like(dghg_ref)
            dgsb_ref[...] = jnp.zeros_like(dgsb_ref)

        mix = mix_ref[...]
        dmix, gterm = _rms_bwd(mix, _rstd(mix), gpost_ref[...], dh1_ref[...])
        dgpost_ref[...] += _colsum(gterm)
        dmix = dmix.astype(BF16)
        dmix_ref[...] = dmix
        dcat = _dot_nt(dmix, w_ref[...])
        da = dcat[:, :HG_WIDTH]
        dn2 = dcat[:, HG_WIDTH:]
        ohg = ohg_ref[...]
        r1 = _rstd(ohg)
        hg = hg_ref[...]
        s = _sigmoid(hg)
        dhg_ref[...] = da * (ohg * r1 * ghg_ref[...]) * (s * (1.0 + hg * (1.0 - s)))
        dohg, gterm = _rms_bwd(ohg, r1, ghg_ref[...], da * (hg * s))
        dohg_ref[...] = dohg
        dghg_ref[...] += _colsum(gterm)
        osb = osb_ref[...]
        dosb, gterm = _rms_bwd(osb, _rstd(osb), gsb_ref[...], dn2)
        dosb_ref[...] = dosb
        dgsb_ref[...] += _colsum(gterm)

    return pl.pallas_call(
        body, grid=(T // tm,), name="out_bwd",
        in_specs=[_row_spec(tm, D_MODEL), _row_spec(tm, D_MODEL), _row_spec(tm, HG_WIDTH), _row_spec(tm, HG_WIDTH, 3),
                  _row_spec(tm, SB_WIDTH), _full_spec((1, HG_WIDTH)), _full_spec((1, SB_WIDTH)),
                  _full_spec((1, D_MODEL)), _full_spec((D_MODEL, D_MODEL)), pl.BlockSpec(memory_space=pl.ANY)],
        out_specs=[_row_spec(tm, D_MODEL), _row_spec(tm, HG_WIDTH), _row_spec(tm, HG_WIDTH), _row_spec(tm, SB_WIDTH),
                   _full_spec((1, D_MODEL)), _full_spec((1, HG_WIDTH)), _full_spec((1, SB_WIDTH))],
        out_shape=[jax.ShapeDtypeStruct((T, D_MODEL), BF16), jax.ShapeDtypeStruct((T, HG_WIDTH), F32),
                   jax.ShapeDtypeStruct((T, HG_WIDTH), F32), jax.ShapeDtypeStruct((T, SB_WIDTH), F32),
                   jax.ShapeDtypeStruct((1, D_MODEL), F32), jax.ShapeDtypeStruct((1, HG_WIDTH), F32),
                   jax.ShapeDtypeStruct((1, SB_WIDTH), F32)],
        compiler_params=_params(("arbitrary",)),
    )(dh1, mix, o_hg, proj, o_sb, g_hg, g_sb, g_post, w, after)


def _in_bwd(d_hgrn, d_hg, d_sq, d_sk, d_sv, x, dh1, g_pre, w_t):
    T = x.shape[0]
    tm = ROW_TILE

    def body(dh_ref, dhg_ref, dsq_ref, dsk_ref, dsv_ref, x_ref, dh1_ref, gpre_ref, w_ref, dproj_ref, dx_ref, dgpre_ref):
        i = pl.program_id(0)

        @pl.when(i == 0)
        def _():
            dgpre_ref[...] = jnp.zeros_like(dgpre_ref)

        dp = jnp.concatenate([dh_ref[...], dhg_ref[...], dsq_ref[...], dsk_ref[...], dsv_ref[...]], axis=1).astype(BF16)
        dproj_ref[...] = dp
        du = _dot(dp, w_ref[...])
        xv = x_ref[...]
        dx, gterm = _rms_bwd(xv, _rstd(xv), gpre_ref[...], du)
        dx_ref[...] = dh1_ref[...] + dx
        dgpre_ref[...] += _colsum(gterm)

    return pl.pallas_call(
        body, grid=(T // tm,), name="in_bwd",
        in_specs=[_row_spec(tm, 3 * HG_WIDTH), _row_spec(tm, HG_WIDTH), _row_spec(tm, SB_WIDTH), _row_spec(tm, SB_WIDTH),
                  _row_spec(tm, SB_WIDTH), _row_spec(tm, D_MODEL), _row_spec(tm, D_MODEL), _full_spec((1, D_MODEL)),
                  _full_spec((IN_COLS, D_MODEL))],
        out_specs=[_row_spec(tm, IN_COLS), _row_spec(tm, D_MODEL), _full_spec((1, D_MODEL))],
        out_shape=[jax.ShapeDtypeStruct((T, IN_COLS), BF16), jax.ShapeDtypeStruct((T, D_MODEL), F32),
                   jax.ShapeDtypeStruct((1, D_MODEL), F32)],
        compiler_params=_params(("arbitrary",)),
    )(d_hgrn, d_hg, d_sq, d_sk, d_sv, x, dh1, g_pre, w_t)


def _wgrad(a, b, name):
    stacked = a.ndim == 3
    S, T, K = a.shape if stacked else (1,) + a.shape
    N = b.shape[1]
    tk = K
    while tk * N * 4 > WGRAD_ACC_BYTES:
        tk //= 2
    assert K % tk == 0 and tk % 128 == 0
    tt = 512
    n_t = T // tt

    def body(a_ref, b_ref, o_ref, acc_ref):
        t = pl.program_id(2)

        @pl.when(t == 0)
        def _():
            acc_ref[...] = jnp.zeros_like(acc_ref)

        acc_ref[...] += _dot_tn(a_ref[...].astype(BF16), b_ref[...].astype(BF16))

        @pl.when(t == n_t - 1)
        def _():
            o_ref[...] = acc_ref[...].astype(BF16)

    if stacked:
        a_spec = pl.BlockSpec((None, tt, tk), lambda s, k, t: (s, t, k))
        o_spec = pl.BlockSpec((None, tk, N), lambda s, k, t: (s, k, 0))
        o_shape = (S, K, N)
    else:
        a_spec = pl.BlockSpec((tt, tk), lambda s, k, t: (t, k))
        o_spec = pl.BlockSpec((tk, N), lambda s, k, t: (k, 0))
        o_shape = (K, N)
    return pl.pallas_call(
        body, grid=(S, K // tk, n_t), name=name,
        in_specs=[a_spec, pl.BlockSpec((tt, N), lambda s, k, t: (t, 0))],
        out_specs=o_spec,
        out_shape=jax.ShapeDtypeStruct(o_shape, BF16),
        scratch_shapes=[pltpu.VMEM((tk, N), F32)],
        compiler_params=_params(("parallel", "parallel", "arbitrary")),
    )(a, b)


def _hg_gates(hq, hf, gam):
    g0 = gam[0:1, :]
    g1 = gam[1:2, :]
    mx = jnp.maximum(g0, g1)
    e0 = jnp.exp(g0 - mx)
    e1 = jnp.exp(g1 - mx)
    lb = e0 / (e0 + e1)
    s = _sigmoid(hf)
    f = lb + (1.0 - lb) * s
    sq = _sigmoid(hq)
    return hq * sq, sq, s, f, (1.0 - lb) * (1.0 - s), jnp.log(f), lb


def _prefix_in_sub(a, rowmod):
    n = a.shape[0]
    sh = 1
    while sh < HG_SUB:
        a = a + jnp.where(rowmod >= sh, pltpu.roll(a, sh, axis=0), 0.0)
        sh *= 2
    return a


def _suffix_in_sub(a, rowmod):
    n = a.shape[0]
    sh = 1
    while sh < HG_SUB:
        a = a + jnp.where(rowmod < HG_SUB - sh, pltpu.roll(a, n - sh, axis=0), 0.0)
        sh *= 2
    return a


def _hg_fwd(proj, gam):
    T = proj.shape[0]
    tb = HG_TILE
    n_sub = tb // HG_SUB

    def body(hq_ref, hf_ref, hi_ref, gam_ref, o_ref, hist_ref, st_ref, qt_ref, kk_ref, b_ref, od_ref):
        i = pl.program_id(0)

        @pl.when(i == 0)
        def _():
            st_ref[...] = jnp.zeros_like(st_ref)

        rowmod_w = lax.broadcasted_iota(jnp.int32, (tb, HG_WIDTH), 0) % HG_SUB
        rowmod = lax.broadcasted_iota(jnp.int32, (tb, 1), 0) % HG_SUB
        q, _, _, f, kk, lf, _ = _hg_gates(hq_ref[...], hf_ref[...], gam_ref[...])
        b = _prefix_in_sub(lf, rowmod_w)
        qt_ref[...] = (q * jnp.exp(b)).astype(BF16)
        kk_ref[...] = kk
        b_ref[...] = b

        for h in range(HG_HEADS):
            cs = slice(h * HG_DK, (h + 1) * HG_DK)
            qh = q[:, cs]
            kh = kk[:, cs]
            fh = f[:, cs]
            vh = hi_ref[:, cs]
            acc = jnp.sum(qh * kh, axis=1, keepdims=True) * vh
            decay = fh
            for d in range(1, HG_SUB):
                if d > 1:
                    decay = decay * pltpu.roll(fh, d - 1, axis=0)
                e = qh * pltpu.roll(kh, d, axis=0) * decay
                sc = jnp.where(rowmod >= d, jnp.sum(e, axis=1, keepdims=True), 0.0)
                acc = acc + sc * pltpu.roll(vh, d, axis=0)
            od_ref[:, cs] = acc

        for m in range(n_sub):
            rs = slice(m * HG_SUB, (m + 1) * HG_SUB)
            for h in range(HG_HEADS):
                cs = slice(h * HG_DK, (h + 1) * HG_DK)
                st = st_ref[h]
                hist_ref[m, h] = st
                o_int = _dot_nt(qt_ref[rs, cs], st.astype(BF16))
                bs = b_ref[rs, cs]
                bl = bs[HG_SUB - 1:HG_SUB, :]
                kt = (kk_ref[rs, cs] * jnp.exp(bl - bs)).astype(BF16)
                st_ref[h] = st * jnp.exp(bl) + _dot_tn(hi_ref[rs, cs].astype(BF16), kt)
                o_ref[rs, cs] = o_int + od_ref[rs, cs]

    col = lambda c: pl.BlockSpec((tb, HG_WIDTH), lambda i, c=c: (i, c))
    return pl.pallas_call(
        body, grid=(T // tb,), name="hg_fwd",
        in_specs=[col(0), col(1), col(2), _full_spec((2, HG_WIDTH))],
        out_specs=[pl.BlockSpec((tb, HG_WIDTH), lambda i: (i, 0)),
                   pl.BlockSpec((n_sub, HG_HEADS, HG_DK, HG_DK), lambda i: (i, 0, 0, 0))],
        out_shape=[jax.ShapeDtypeStruct((T, HG_WIDTH), F32),
                   jax.ShapeDtypeStruct((T // HG_SUB, HG_HEADS, HG_DK, HG_DK), F32)],
        scratch_shapes=[pltpu.VMEM((HG_HEADS, HG_DK, HG_DK), F32), pltpu.VMEM((tb, HG_WIDTH), BF16),
                        pltpu.VMEM((tb, HG_WIDTH), F32), pltpu.VMEM((tb, HG_WIDTH), F32),
                        pltpu.VMEM((tb, HG_WIDTH), F32)],
        compiler_params=_params(("arbitrary",)),
    )(proj, proj, proj, gam)


def _hg_bwd(proj, gam, d_o, hist):
    T = proj.shape[0]
    tb = HG_TILE
    n_sub = tb // HG_SUB
    n_t = T // tb

    def body(hq_ref, hf_ref, hi_ref, gam_ref, do_ref, hist_ref, dout_ref, dlb_ref,
             dst_ref, q_ref, kk_ref, b_ref, dq_ref, dkk_ref, db_ref, dv_ref):
        i = pl.program_id(0)

        @pl.when(i == 0)
        def _():
            dst_ref[...] = jnp.zeros_like(dst_ref)
            dlb_ref[...] = jnp.zeros_like(dlb_ref)

        rowmod_w = lax.broadcasted_iota(jnp.int32, (tb, HG_WIDTH), 0) % HG_SUB
        rowmod = lax.broadcasted_iota(jnp.int32, (tb, 1), 0) % HG_SUB
        last_row = lax.broadcasted_iota(jnp.int32, (HG_SUB, 1), 0) == HG_SUB - 1
        hq = hq_ref[...]
        q, sq, s, f, kk, lf, lb = _hg_gates(hq, hf_ref[...], gam_ref[...])
        b = _prefix_in_sub(lf, rowmod_w)
        q_ref[...] = q
        kk_ref[...] = kk
        b_ref[...] = b

        for m in reversed(range(n_sub)):
            rs = slice(m * HG_SUB, (m + 1) * HG_SUB)
            for h in range(HG_HEADS):
                cs = slice(h * HG_DK, (h + 1) * HG_DK)
                st = hist_ref[m, h]
                dst = dst_ref[h]
                dstb = dst.astype(BF16)
                dos = do_ref[rs, cs].astype(BF16)
                vs = hi_ref[rs, cs].astype(BF16)
                bs = b_ref[rs, cs]
                bl = bs[HG_SUB - 1:HG_SUB, :]
                ebl = jnp.exp(bl)
                qt = q_ref[rs, cs] * jnp.exp(bs)
                decay = jnp.exp(bl - bs)
                kt = kk_ref[rs, cs] * decay
                dqt = _dot(dos, st.astype(BF16))
                dkt = _dot(vs, dstb)
                dv_ref[rs, cs] = _dot_nt(kt.astype(BF16), dstb)
                dst_ref[h] = dst * ebl + _dot_tn(dos, qt.astype(BF16))
                dq_ref[rs, cs] = dqt * jnp.exp(bs)
                dkk_ref[rs, cs] = dkt * decay
                dktk = dkt * kt
                dbl = _colsum(dktk) + _colsum(dst * st) * ebl
                db_ref[rs, cs] = dqt * qt - dktk + jnp.where(last_row, dbl, 0.0)

        for h in range(HG_HEADS):
            cs = slice(h * HG_DK, (h + 1) * HG_DK)
            qh = q[:, cs]
            kh = kk[:, cs]
            fh = f[:, cs]
            vh = hi_ref[:, cs]
            doh = do_ref[:, cs]
            dsc = jnp.sum(doh * vh, axis=1, keepdims=True)
            sc = jnp.sum(qh * kh, axis=1, keepdims=True)
            dqh = dsc * kh
            dkh = dsc * qh
            dvh = sc * doh
            dbh = jnp.zeros_like(qh)
            decay = fh
            for d in range(1, HG_SUB):
                back = tb - d
                valid = rowmod >= d
                if d > 1:
                    decay = decay * pltpu.roll(fh, d - 1, axis=0)
                kd = pltpu.roll(kh, d, axis=0)
                ex = jnp.where(valid, decay, 0.0)
                pd = kd * ex
                sc = jnp.sum(qh * pd, axis=1, keepdims=True)
                dsc = jnp.where(valid, jnp.sum(doh * pltpu.roll(vh, d, axis=0), axis=1, keepdims=True), 0.0)
                dqh = dqh + dsc * pd
                dsq = dsc * qh
                w = dsq * pd
                dkh = dkh + pltpu.roll(dsq * ex, back, axis=0)
                dvh = dvh + pltpu.roll(sc * doh, back, axis=0)
                dbh = dbh + w - pltpu.roll(w, back, axis=0)
            dq_ref[:, cs] += dqh
            dkk_ref[:, cs] += dkh
            dv_ref[:, cs] += dvh
            db_ref[:, cs] += dbh

        dlf = _suffix_in_sub(db_ref[...], rowmod_w)
        df = dlf / f - dkk_ref[...]
        dout_ref[:, 0:HG_WIDTH] = dq_ref[...] * (sq * (1.0 + hq * (1.0 - sq)))
        dout_ref[:, HG_WIDTH:2 * HG_WIDTH] = df * (1.0 - lb) * (s * (1.0 - s))
        dout_ref[:, 2 * HG_WIDTH:3 * HG_WIDTH] = dv_ref[...]
        dlb_ref[...] += _colsum(df * (1.0 - s))

    col = lambda c: pl.BlockSpec((tb, HG_WIDTH), lambda i, c=c: (n_t - 1 - i, c))
    return pl.pallas_call(
        body, grid=(n_t,), name="hg_bwd",
        in_specs=[col(0), col(1), col(2), _full_spec((2, HG_WIDTH)),
                  pl.BlockSpec((tb, HG_WIDTH), lambda i: (n_t - 1 - i, 0)),
                  pl.BlockSpec((n_sub, HG_HEADS, HG_DK, HG_DK), lambda i: (n_t - 1 - i, 0, 0, 0))],
        out_specs=[pl.BlockSpec((tb, 3 * HG_WIDTH), lambda i: (n_t - 1 - i, 0)), _full_spec((1, HG_WIDTH))],
        out_shape=[jax.ShapeDtypeStruct((T, 3 * HG_WIDTH), F32), jax.ShapeDtypeStruct((1, HG_WIDTH), F32)],
        scratch_shapes=[pltpu.VMEM((HG_HEADS, HG_DK, HG_DK), F32)] + [pltpu.VMEM((tb, HG_WIDTH), F32)] * 7,
        compiler_params=_params(("arbitrary",)),
    )(proj, proj, proj, gam, d_o, hist)


def _sb_masks():
    row = lax.broadcasted_iota(jnp.int32, (SB_TK, SB_TK), 0)
    col = lax.broadcasted_iota(jnp.int32, (SB_TK, SB_TK), 1)
    suffix = (row >= col).astype(BF16)
    prefix = (row <= col).astype(BF16)
    query = lax.broadcasted_iota(jnp.int32, (SB_TQ, SB_TK), 0)
    key = lax.broadcasted_iota(jnp.int32, (SB_TQ, SB_TK), 1)
    causal = [key + r * SB_TK < query for r in range(SB_TQ // SB_TK)]
    lane = lax.broadcasted_iota(jnp.int32, (SB_TQ, SB_BLOCK), 1)
    return suffix, prefix, causal, lane


def _sum_right(a, suffix):
    return _dot(a.astype(BF16), suffix)


def _per_lane_tile(t, c, op):
    return jnp.concatenate([op(t[:, n:n + SB_BLOCK], c) for n in range(0, t.shape[1], SB_BLOCK)], axis=1)


def _sb_block(qb, kj, suffix, causal, c):
    z = _dot_nt(qb, kj)
    sp = jnp.maximum(z, 0.0) + jnp.log(1.0 + jnp.exp(-jnp.abs(z)))
    if causal is not None:
        sp = jnp.where(causal, sp, 0.0)
    big_l = _sum_right(sp, suffix)
    a = jnp.exp(_per_lane_tile(z - big_l, c, jnp.subtract))
    if causal is not None:
        a = jnp.where(causal, a, 0.0)
    return z, a, big_l


def _sb_tile_copies(to_hbm, bufs, stores, sems, slot, head, i, j, n_chain):
    copies = []
    for ch in range(n_chain):
        for w in range(2):
            vmem, hbm = bufs[w].at[slot, ch], stores[w].at[head + ch, i, j]
            src, dst = (vmem, hbm) if to_hbm else (hbm, vmem)
            copies.append(pltpu.make_async_copy(src, dst, sems.at[slot, ch, w]))
    return copies


def _half_masked(pair, lane, scale=1.0):
    pair = pair.astype(F32) * scale
    return jnp.where(lane < SB_DH, pair, 0.0).astype(BF16), jnp.where(lane < SB_DH, 0.0, pair).astype(BF16)


def _sb_fwd(qkv):
    T = qkv.shape[0]
    width = SB_PP * SB_BLOCK
    n_chain = 2 * SB_PP
    n_sub = SB_TQ // SB_TK

    def body(q_ref, k_ref, v_ref, o_ref, a_st, s_st, acc_ref, c_ref, qm, a_buf, s_buf, sems):
        g = pl.program_id(0)
        i = pl.program_id(1)
        suffix, _, causal, lane = _sb_masks()
        for pp in range(SB_PP):
            qm[2 * pp], qm[2 * pp + 1] = _half_masked(q_ref[:, pp * SB_BLOCK:(pp + 1) * SB_BLOCK], lane, SB_DH ** -0.5)
        acc_ref[...] = jnp.zeros_like(acc_ref)
        c_ref[...] = jnp.zeros_like(c_ref)
        a_buf[...] = jnp.zeros_like(a_buf)
        s_buf[...] = jnp.zeros_like(s_buf)

        def copies(slot, j):
            return _sb_tile_copies(True, (a_buf, s_buf), (a_st, s_st), sems, slot, g * n_chain, i, j, n_chain)

        def step(j, slot, r=None, reuse=True):
            rows = pl.ds(pl.multiple_of(j * SB_TK, SB_TK), SB_TK)
            qs = slice(0 if r is None else r * SB_TK, SB_TQ)
            mask = None if r is None else causal[r][qs]
            if reuse:
                for cp in copies(slot, j):
                    cp.wait()
            for ch in range(n_chain):
                pair = slice((ch // 2) * SB_BLOCK, (ch // 2 + 1) * SB_BLOCK)
                c = c_ref[ch, qs]
                z, a, big_l = _sb_block(qm[ch, qs], k_ref[rows, pair], suffix, mask, c)
                ab = a.astype(BF16)
                acc_ref[ch, qs] += _dot(ab, v_ref[rows, pair])
                a_buf[slot, ch, qs] = ab
                s_buf[slot, ch, qs] = z.astype(BF16)
                c_ref[ch, qs] = c + jnp.broadcast_to(big_l[:, 0:1], c.shape)
            for cp in copies(slot, j):
                cp.start()

        first = i * n_sub
        for s, r in enumerate(reversed(range(n_sub))):
            step(first + r, s % 2, r, reuse=s >= 2)

        @pl.loop(0, first)
        def _(t):
            step(first - 1 - t, (n_sub + t) % 2)

        for slot in range(2):
            for cp in copies(slot, 0):
                cp.wait()
        for pp in range(SB_PP):
            o_ref[:, pp * SB_BLOCK:(pp + 1) * SB_BLOCK] = jnp.where(lane < SB_DH, acc_ref[2 * pp], acc_ref[2 * pp + 1])

    assert n_sub >= 2 and n_sub % 2 == 0
    n_g = SB_WIDTH // width
    blk = lambda part: pl.BlockSpec((SB_TQ, width), lambda g, i, part=part: (i, part * n_g + g))
    whole = lambda part: pl.BlockSpec((T, width), lambda g, i, part=part: (0, part * n_g + g))
    tiles = jax.ShapeDtypeStruct((SB_HEADS, T // SB_TQ, T // SB_TK, SB_TQ, SB_TK), BF16)
    return pl.pallas_call(
        body, grid=(n_g, T // SB_TQ), name="sb_fwd",
        in_specs=[blk(0), whole(1), whole(2)],
        out_specs=[blk(0), pl.BlockSpec(memory_space=pl.ANY), pl.BlockSpec(memory_space=pl.ANY)],
        out_shape=[jax.ShapeDtypeStruct((T, SB_WIDTH), F32), tiles, tiles],
        scratch_shapes=[pltpu.VMEM((n_chain, SB_TQ, SB_BLOCK), F32), pltpu.VMEM((n_chain, SB_TQ, SB_BLOCK), F32),
                        pltpu.VMEM((n_chain, SB_TQ, SB_BLOCK), BF16),
                        pltpu.VMEM((2, n_chain, SB_TQ, SB_TK), BF16), pltpu.VMEM((2, n_chain, SB_TQ, SB_TK), BF16),
                        pltpu.SemaphoreType.DMA((2, n_chain, 2))],
        compiler_params=_params(("parallel", "arbitrary")),
    )(qkv, qkv, qkv)


def _sb_bwd(qkv, a_st, s_st, d_o, after):
    T = qkv.shape[0]
    width = SB_PP * SB_BLOCK
    n_chain = 2 * SB_PP
    n_sub = SB_TQ // SB_TK
    scale = SB_DH ** -0.5

    def body(q_ref, k_ref, v_ref, a_st_ref, s_st_ref, do_ref, after_ref, dq_ref, dk_ref, dv_ref, acc_ref, gc_ref, qm, dom,
             a_buf, s_buf, sems):
        g_idx = pl.program_id(0)
        i = pl.program_id(1)

        @pl.when(i == 0)
        def _():
            dk_ref[...] = jnp.zeros_like(dk_ref)
            dv_ref[...] = jnp.zeros_like(dv_ref)

        _, prefix, causal, lane = _sb_masks()
        for pp in range(SB_PP):
            pair = slice(pp * SB_BLOCK, (pp + 1) * SB_BLOCK)
            qm[2 * pp], qm[2 * pp + 1] = _half_masked(q_ref[:, pair], lane, scale)
            dom[2 * pp], dom[2 * pp + 1] = _half_masked(do_ref[:, pair], lane)
        acc_ref[...] = jnp.zeros_like(acc_ref)
        gc_ref[...] = jnp.zeros_like(gc_ref)

        def copies(slot, j):
            return _sb_tile_copies(False, (a_buf, s_buf), (a_st_ref, s_st_ref), sems, slot, g_idx * n_chain, i, j, n_chain)

        def step(j, slot, r=None, last=False):
            rows = pl.ds(pl.multiple_of(j * SB_TK, SB_TK), SB_TK)
            qs = slice(0 if r is None else r * SB_TK, SB_TQ)
            mask = None if r is None else causal[r][qs]
            for cp in copies(slot, j):
                cp.wait()
            if not last:
                for cp in copies(1 - slot, j + 1):
                    cp.start()
            for pp in range(SB_PP):
                pair = slice(pp * SB_BLOCK, (pp + 1) * SB_BLOCK)
                kj = k_ref[rows, pair]
                vj = v_ref[rows, pair]
                dk = jnp.zeros((SB_TK, SB_BLOCK), F32)
                dv = jnp.zeros((SB_TK, SB_BLOCK), F32)
                for ch in (2 * pp, 2 * pp + 1):
                    ab = a_buf[slot, ch, qs]
                    g = ab.astype(F32) * _dot_nt(dom[ch, qs], vj)
                    g_left = _sum_right(g, prefix)
                    gc = gc_ref[ch, qs]
                    dz = g - _sigmoid(s_buf[slot, ch, qs].astype(F32)) * _per_lane_tile(g_left, gc, jnp.add)
                    if mask is not None:
                        dz = jnp.where(mask, dz, 0.0)
                    dzb = dz.astype(BF16)
                    dk = dk + _dot_tn(dzb, qm[ch, qs])
                    dv = dv + _dot_tn(ab, dom[ch, qs])
                    acc_ref[ch, qs] += _dot(dzb, kj)
                    gc_ref[ch, qs] = gc + jnp.broadcast_to(g_left[:, SB_TK - 1:SB_TK], gc.shape)
                dk_ref[rows, pair] += dk
                dv_ref[rows, pair] += dv

        first = i * n_sub
        for cp in copies(0, 0):
            cp.start()

        @pl.loop(0, first)
        def _(j):
            step(j, j % 2)

        for r in range(n_sub):
            step(first + r, r % 2, r, last=r == n_sub - 1)
        for pp in range(SB_PP):
            dq_ref[:, pp * SB_BLOCK:(pp + 1) * SB_BLOCK] = scale * jnp.where(lane < SB_DH, acc_ref[2 * pp],
                                                                             acc_ref[2 * pp + 1])
    n_g = SB_WIDTH // width
    blk = lambda part: pl.BlockSpec((SB_TQ, width), lambda g, i, part=part: (i, part * n_g + g))
    whole = lambda part: pl.BlockSpec((T, width), lambda g, i, part=part: (0, part * n_g + g))
    return pl.pallas_call(
        body, grid=(n_g, T // SB_TQ), name="sb_bwd",
        in_specs=[blk(0), whole(1), whole(2), pl.BlockSpec(memory_space=pl.ANY), pl.BlockSpec(memory_space=pl.ANY), blk(0),
                  pl.BlockSpec(memory_space=pl.ANY)],
        out_specs=[blk(0), whole(0), whole(0)],
        out_shape=[jax.ShapeDtypeStruct((T, SB_WIDTH), F32)] * 3,
        scratch_shapes=[pltpu.VMEM((n_chain, SB_TQ, SB_BLOCK), F32), pltpu.VMEM((n_chain, SB_TQ, SB_BLOCK), F32),
                        pltpu.VMEM((n_chain, SB_TQ, SB_BLOCK), BF16), pltpu.VMEM((n_chain, SB_TQ, SB_BLOCK), BF16),
                        pltpu.VMEM((2, n_chain, SB_TQ, SB_TK), BF16), pltpu.VMEM((2, n_chain, SB_TQ, SB_TK), BF16),
                        pltpu.SemaphoreType.DMA((2, n_chain, 2))],
        compiler_params=_params(("parallel", "arbitrary")),
    )(qkv, qkv, qkv, a_st, s_st, d_o, after)


HBM = pl.BlockSpec(memory_space=pltpu.HBM)
MESH = pl.DeviceIdType.MESH


def _place():
    return lax.axis_index("x"), lax.axis_index("y"), lax.axis_index("c")


def _all_gather(block):
    rows, cols = block.shape

    def body(x_ref, out_ref, send_sems, recv_sems, local_sem):
        x, y, c = _place()
        me, sibling = (x, y, c), (x, y, 1 - c)
        chips = [(1 - x, y), (x, 1 - y), (1 - x, 1 - y)]

        def slot(px, py, pc):
            return out_ref.at[4 * px + 2 * py + pc]

        def copy(k, blk, to, src=None):
            return pltpu.make_async_remote_copy(
                src_ref=slot(*blk) if src is None else src, dst_ref=slot(*blk),
                send_sem=send_sems.at[k], recv_sem=recv_sems.at[k], device_id=to, device_id_type=MESH)

        mine = pltpu.make_async_copy(x_ref, slot(*me), local_sem)
        mine.start()
        first = [copy(0, me, sibling, src=x_ref)]
        first += [copy(1 + j, me, (*chip, c), src=x_ref) for j, chip in enumerate(chips)]
        for cp in first:
            cp.start()
        passed = [copy(4 + j, (*chip, c), sibling) for j, chip in enumerate(chips)]
        for j, chip in enumerate(chips):
            copy(1 + j, (*chip, c), me).wait_recv()
            passed[j].start()
        copy(0, sibling, me).wait_recv()
        for j, chip in enumerate(chips):
            copy(4 + j, (*chip, 1 - c), me).wait_recv()
        for cp in first + passed:
            cp.wait_send()
        mine.wait()

    return pl.pallas_call(
        body, name="all_gather",
        out_shape=jax.ShapeDtypeStruct((N_DEV, rows, cols), block.dtype),
        in_specs=[HBM], out_specs=HBM,
        scratch_shapes=[pltpu.SemaphoreType.DMA((7,)), pltpu.SemaphoreType.DMA((7,)), pltpu.SemaphoreType.DMA],
    )(block)


SEM = pl.BlockSpec(memory_space=pltpu.SEMAPHORE)
ANY = pl.BlockSpec(memory_space=pl.ANY)
SPLIT_EFFECT = pltpu.SideEffectType.DATAFLOW_SIDE_EFFECTING
N_PEER = N_DEV - 1


def _flow_copies(kind, src_ref, land_ref, send_sems, recv_sems):
    x, y, c = _place()
    copies = []
    for r in range(1, N_DEV):
        px = 1 - x if r & 4 else x
        py = 1 - y if r & 2 else y
        pc = 1 - c if r & 1 else c
        if kind == "gather":
            src, dst = src_ref, land_ref.at[4 * x + 2 * y + c]
        else:
            src, dst = src_ref.at[4 * px + 2 * py + pc], land_ref.at[r - 1]
        copies.append(pltpu.make_async_remote_copy(
            src_ref=src, dst_ref=dst, send_sem=send_sems.at[r - 1], recv_sem=recv_sems.at[r - 1],
            device_id=(px, py, pc), device_id_type=MESH))
    return copies


def _landing(kind, src):
    if kind == "gather":
        return lax.empty((N_DEV,) + src.shape, src.dtype)
    return lax.empty((N_PEER,) + src.shape[1:], src.dtype)


def _push_start(kinds, arrays, after, name):
    n = len(arrays)

    def body(*refs):
        ins, sems, token = refs[:n], refs[n + 1:2 * n + 1], refs[3 * n + 1]
        for f, kind in enumerate(kinds):
            for cp in _flow_copies(kind, ins[2 * f], ins[2 * f + 1], sems[2 * f], sems[2 * f + 1]):
                cp.start()
        token[...] = jnp.zeros_like(token)

    outs = pl.pallas_call(
        body, name=name,
        out_shape=[pltpu.SemaphoreType.DMA((N_PEER,))] * n + [pltpu.HBM(a.shape, a.dtype) for a in arrays]
        + [jax.ShapeDtypeStruct((8, 128), F32)],
        in_specs=[HBM] * n + [ANY], out_specs=[SEM] * n + [HBM] * n + [pl.BlockSpec(memory_space=pltpu.VMEM)],
        input_output_aliases={i: n + i for i in range(n)},
        compiler_params=pltpu.CompilerParams(has_side_effects=SPLIT_EFFECT),
    )(*[pltpu.with_memory_space_constraint(a, pltpu.HBM) for a in arrays], after)
    return outs[:n], outs[n:2 * n], outs[2 * n]


def _push_wait(kinds, sems, arrays, after, name):
    n = len(arrays)

    def body(*refs):
        ins, sem_refs = refs[:n], refs[n:2 * n]
        for f, kind in enumerate(kinds):
            for cp in _flow_copies(kind, ins[2 * f], ins[2 * f + 1], sem_refs[2 * f], sem_refs[2 * f + 1]):
                cp.wait_send()
                cp.wait_recv()

    outs = pl.pallas_call(
        body, name=name,
        out_shape=[pltpu.HBM(a.shape, a.dtype) for a in arrays],
        in_specs=[HBM] * n + [SEM] * n + [ANY], out_specs=[HBM] * n,
        input_output_aliases={i: i for i in range(n)},
        compiler_params=pltpu.CompilerParams(has_side_effects=SPLIT_EFFECT),
    )(*arrays, *sems, after)
    return outs


def _sum_blocks(own, got, name):
    n, rows, cols = got.shape
    tr = next(t for t in (128, 64, 32) if rows % t == 0)

    def body(own_ref, got_ref, out_ref):
        acc = own_ref[...].astype(F32)
        for r in range(n):
            acc = acc + got_ref[r].astype(F32)
        out_ref[...] = acc

    return pl.pallas_call(
        body, grid=(rows // tr,), name=name,
        in_specs=[pl.BlockSpec((tr, cols), lambda i: (i, 0)), pl.BlockSpec((n, tr, cols), lambda i: (0, i, 0))],
        out_specs=pl.BlockSpec((tr, cols), lambda i: (i, 0)),
        out_shape=jax.ShapeDtypeStruct((rows, cols), F32),
        compiler_params=_params(("parallel",)),
    )(own, got)


def _adamw_math(w, g, m, v):
    m = ADAM_B1 * m + (1.0 - ADAM_B1) * g
    v = ADAM_B2 * v + (1.0 - ADAM_B2) * (g * g)
    m_hat = m / (1.0 - ADAM_B1 ** ADAM_STEP)
    v_hat = v / (1.0 - ADAM_B2 ** ADAM_STEP)
    delta = -ADAM_LR * (m_hat / (jnp.sqrt(v_hat) + ADAM_EPS) + ADAM_WD * w)
    return delta, m, v


def _adamw(w, g, m, v, name):
    rows, cols = w.shape
    tr = rows if rows <= 352 else 256

    def body(w_ref, g_ref, m_ref, v_ref, d_out, m_out, v_out):
        d_out[...], m_out[...], v_out[...] = _adamw_math(w_ref[...], g_ref[...], m_ref[...], v_ref[...])

    spec = pl.BlockSpec((tr, cols), lambda i: (i, 0))
    return pl.pallas_call(
        body, grid=(rows // tr,), name=name,
        in_specs=[spec] * 4, out_specs=[spec] * 3,
        out_shape=[jax.ShapeDtypeStruct((rows, cols), F32)] * 3,
        compiler_params=_params(("parallel",)),
    )(w, g, m, v)


ROW_ATTN_PRE, ROW_ATTN_POST, ROW_FFN_PRE, ROW_FFN_POST, ROW_OUT_NORMS, ROW_GAMMA, ROW_LOSS = range(7)


def _small_update(smalls, w, m, v):
    def body(s_ref, w_ref, m_ref, v_ref, g_out, d_out, m_out, v_out):
        g = s_ref[0]
        for j in range(1, N_DEV):
            g = g + s_ref[j]
        wv = w_ref[...]
        g0 = wv[ROW_GAMMA:ROW_GAMMA + 1, 0:HG_WIDTH]
        g1 = wv[ROW_GAMMA:ROW_GAMMA + 1, HG_WIDTH:]
        mx = jnp.maximum(g0, g1)
        e0 = jnp.exp(g0 - mx)
        e1 = jnp.exp(g1 - mx)
        lb = e0 / (e0 + e1)
        dg0 = g[ROW_GAMMA:ROW_GAMMA + 1, 0:HG_WIDTH] * lb * (1.0 - lb)
        row = lax.broadcasted_iota(jnp.int32, g.shape, 0)
        g = jnp.where(row == ROW_GAMMA, jnp.concatenate([dg0, -dg0], axis=1), g)
        g_out[...] = g
        d_out[...], m_out[...], v_out[...] = _adamw_math(wv, g, m_ref[...], v_ref[...])

    return pl.pallas_call(
        body, name="small_update",
        out_shape=[jax.ShapeDtypeStruct((8, D_MODEL), F32)] * 4,
    )(smalls, w, m, v)


def _local_step(x, p, target, g_attn_pre, w_in_t, gamma, g_hg, g_sb, g_attn_post, g_ffn_pre, g_ffn_post, later_weights,
                grads_ready):
    proj, u, qkv = _in_proj(x, g_attn_pre, w_in_t)
    o_hg, hist = _hg_fwd(proj, gamma)
    o_sb, a_st, s_st = _sb_fwd(qkv)
    w_out, w_gu_t, w_down, w_pp_t, w_pg = later_weights(o_sb)
    cat, mix, h1 = _out_proj(o_hg, proj, o_sb, x, g_hg, g_sb, g_attn_post, w_out)
    u2, gu, y, h2 = _ffn_fwd(h1, g_ffn_pre, w_gu_t, w_down, g_ffn_post)
    dh2, de, dg, loss = _ple_loss(h2, p, target, w_pp_t, w_pg)

    d_wpp_t = _wgrad(de, p, "wgrad_ple_proj")
    d_wpg = _wgrad(h2, dg, "wgrad_ple_gate")
    dy, a, dgu, dh1, dg_ffn_post, dg_ffn_pre = _ffn_bwd(dh2, y, h1, gu, g_ffn_post, g_ffn_pre, w_gu_t, w_down)
    d_wdown = _wgrad(a, dy, "wgrad_down")
    sent = grads_ready("ffn", (_wgrad(dgu, u2, "wgrad_gate_up"), d_wdown, d_wpp_t, d_wpg))
    dmix, d_ohg, d_hg, d_osb, dg_attn_post, dg_hg, dg_sb = _out_bwd(dh1, mix, o_hg, proj, o_sb, g_hg, g_sb, g_attn_post,
                                                                    w_out, sent)
    sent = grads_ready("out", (_wgrad(cat, dmix, "wgrad_out"),))
    dq, dk, dv = _sb_bwd(qkv, a_st, s_st, d_osb, sent)
    d_hgrn, d_lb = _hg_bwd(proj, gamma, d_ohg, hist)
    dproj, dx, dg_attn_pre = _in_bwd(d_hgrn, d_hg, dq, dk, dv, x, dh1, g_attn_pre, w_in_t)
    grads_ready("in", (_wgrad(dproj, u, "wgrad_in"),))
    return loss, dx, (dg_attn_pre, dg_attn_post, dg_ffn_pre, dg_ffn_post, dg_hg, dg_sb, d_lb)


def _pack_row(*parts):
    return jnp.concatenate([q.reshape(1, -1) for q in parts], axis=1)


def kernel(x, p, attn_pre_norm, w_in, hg_lower_gamma, hg_out_norm, sb_out_norm, w_out, attn_post_norm, ffn_pre_norm, w_gate_up, w_down, ffn_post_norm, ple_proj, ple_gate, loss_target, m_attn_pre_norm, m_w_in, m_hg_lower_gamma, m_hg_out_norm, m_sb_out_norm, m_w_out, m_attn_post_norm, m_ffn_pre_norm, m_w_gate_up, m_w_down, m_ffn_post_norm, m_ple_proj, m_ple_gate, v_attn_pre_norm, v_w_in, v_hg_lower_gamma, v_hg_out_norm, v_sb_out_norm, v_w_out, v_attn_post_norm, v_ffn_pre_norm, v_w_gate_up, v_w_down, v_ffn_post_norm, v_ple_proj, v_ple_gate):
    big = (w_in[0], w_out[0], w_gate_up[0], w_down[0], ple_proj[0], ple_gate[0])
    big_m = (m_w_in[0], m_w_out[0], m_w_gate_up[0], m_w_down[0], m_ple_proj[0], m_ple_gate[0])
    big_v = (v_w_in[0], v_w_out[0], v_w_gate_up[0], v_w_down[0], v_ple_proj[0], v_ple_gate[0])
    names = ("w_in", "w_out", "w_gate_up", "w_down", "ple_proj", "ple_gate")

    by_columns = (True, False, True, False, True, False)
    shards = [(w.T if t else w).astype(BF16) for w, t in zip(big, by_columns)]
    w_in_t = _all_gather(shards[0]).reshape(IN_COLS, D_MODEL)
    me = 4 * lax.axis_index("x") + 2 * lax.axis_index("y") + lax.axis_index("c")
    later = shards[1:]
    w_sems, w_thru, token = _push_start(
        ("gather",) * len(later), [a for s in later for a in (s, _landing("gather", s))], w_in_t, "weights_start")

    def later_weights(after):
        done = _push_wait(("gather",) * len(later), w_sems, w_thru, after, "weights_wait")
        whole = [lax.dynamic_update_index_in_dim(got, mine, me, 0) for mine, got in zip(done[0::2], done[1::2])]
        return (whole[0].reshape(D_MODEL, D_MODEL), whole[1].reshape(2, D_FF, D_MODEL), whole[2].reshape(D_FF, D_MODEL),
                whole[3].reshape(D_MODEL, PLE_DIM), whole[4].reshape(D_MODEL, D_MODEL))

    flights = {}

    def by_owner(g):
        return g.reshape(N_DEV, -1, g.shape[-1])

    def grads_ready(group, grads):
        blocks = [by_owner(g) for g in grads]
        if group == "in":
            flights[group] = blocks[0]
            return None
        sems, thru, sent = _push_start(
            ("scatter",) * len(blocks), [a for b in blocks for a in (b, _landing("scatter", b))], token,
            "grads_" + group + "_start")
        flights[group] = (sems, thru)
        return sent

    loss, dx, smalls = _local_step(
        x[0], p[0, 0], loss_target[0], attn_pre_norm + token[0:1, 0:1], w_in_t,
        hg_lower_gamma, hg_out_norm, sb_out_norm, attn_post_norm, ffn_pre_norm, ffn_post_norm, later_weights, grads_ready)
    dg_attn_pre, dg_attn_post, dg_ffn_pre, dg_ffn_post, dg_hg, dg_sb, d_lb = smalls

    zeros_half = jnp.zeros((1, HG_WIDTH), F32)
    small_pack = jnp.concatenate([
        dg_attn_pre, dg_attn_post, dg_ffn_pre, dg_ffn_post, _pack_row(dg_hg, dg_sb), _pack_row(d_lb, zeros_half),
        jnp.broadcast_to(loss[0:1, 0:1], (1, D_MODEL)), jnp.zeros((1, D_MODEL), F32)], axis=0)
    in_blocks = flights["in"]
    in_sems, in_thru, in_token = _push_start(
        ("scatter", "gather"),
        (in_blocks, _landing("scatter", in_blocks), small_pack, _landing("gather", small_pack)), token,
        "grads_in_start")
    ffn = _push_wait(("scatter",) * 4, *flights["ffn"], in_token, "grads_ffn_wait")
    sent_out, got_out = _push_wait(("scatter",), *flights["out"], ffn[1], "grads_out_wait")
    sent_in, got_in, my_small, all_smalls = _push_wait(("scatter", "gather"), in_sems, in_thru, got_out, "grads_in_wait")
    all_smalls = lax.dynamic_update_index_in_dim(all_smalls, my_small, me, 0)

    def reduced(sent, got, name):
        return _sum_blocks(lax.dynamic_index_in_dim(sent, me, 0, keepdims=False), got, "grad_sum_" + name)

    sums = (reduced(sent_in, got_in, "w_in"), reduced(sent_out, got_out, "w_out"), reduced(ffn[0], ffn[1], "w_gate_up"),
            reduced(ffn[2], ffn[3], "w_down"), reduced(ffn[4], ffn[5], "ple_proj"), reduced(ffn[6], ffn[7], "ple_gate"))

    out_g, out_d, out_m, out_v = {}, {}, {}, {}
    for i, name in enumerate(names):
        g = sums[i].T if by_columns[i] else sums[i]
        out_g[name] = g[None]
        d, m, v = _adamw(big[i], g, big_m[i], big_v[i], "adamw_" + name)
        out_d[name], out_m[name], out_v[name] = d[None], m[None], v[None]

    def small_rows(pre, gam, hg, sb, post, fpre, fpost):
        return jnp.concatenate([pre, post, fpre, fpost, _pack_row(hg, sb), _pack_row(gam[0], gam[1]),
                                jnp.zeros((2, D_MODEL), F32)], axis=0)

    packs = _small_update(
        all_smalls,
        small_rows(attn_pre_norm, hg_lower_gamma, hg_out_norm, sb_out_norm, attn_post_norm, ffn_pre_norm, ffn_post_norm),
        small_rows(m_attn_pre_norm, m_hg_lower_gamma, m_hg_out_norm, m_sb_out_norm, m_attn_post_norm, m_ffn_pre_norm,
                   m_ffn_post_norm),
        small_rows(v_attn_pre_norm, v_hg_lower_gamma, v_hg_out_norm, v_sb_out_norm, v_attn_post_norm, v_ffn_pre_norm,
                   v_ffn_post_norm))

    def unpack(pk):
        return {
            "attn_pre_norm": pk[ROW_ATTN_PRE:ROW_ATTN_PRE + 1],
            "hg_lower_gamma": pk[ROW_GAMMA].reshape(2, HG_WIDTH),
            "hg_out_norm": pk[ROW_OUT_NORMS:ROW_OUT_NORMS + 1, :HG_WIDTH],
            "sb_out_norm": pk[ROW_OUT_NORMS:ROW_OUT_NORMS + 1, HG_WIDTH:],
            "attn_post_norm": pk[ROW_ATTN_POST:ROW_ATTN_POST + 1],
            "ffn_pre_norm": pk[ROW_FFN_PRE:ROW_FFN_PRE + 1],
            "ffn_post_norm": pk[ROW_FFN_POST:ROW_FFN_POST + 1],
        }

    sg, sd, sm, sv = (unpack(pk) for pk in packs)
    out_g.update(sg), out_d.update(sd), out_m.update(sm), out_v.update(sv)
    order = ("attn_pre_norm", "w_in", "hg_lower_gamma", "hg_out_norm", "sb_out_norm", "w_out", "attn_post_norm",
             "ffn_pre_norm", "w_gate_up", "w_down", "ffn_post_norm", "ple_proj", "ple_gate")
    total_loss = packs[0][ROW_LOSS, 0]
    return (total_loss, dx[None], *[out_g[n] for n in order], *[out_d[n] for n in order],
            *[out_m[n] for n in order], *[out_v[n] for n in order])
```

```python
import functools

import jax
import jax.numpy as jnp
from jax import lax
from jax.experimental import pallas as pl
from jax.experimental.pallas import tpu as pltpu

F32 = jnp.float32
BF16 = jnp.bfloat16

D_MODEL = 1024
HG_WIDTH = 512
HG_HEADS = 4
HG_DK = 128
SB_WIDTH = 512
SB_HEADS = 8
SB_DH = 64
SB_BLOCK = 128
SB_PP = 2
SB_TQ = 1024
SB_TK = 256
D_FF = 2816
PLE_DIM = 256
IN_COLS = 4 * HG_WIDTH + 3 * SB_WIDTH
EPS = 1e-6
N_DEV = 8

HG_SUB = 16
HG_TILE = 128
FFN_TF = 1408
ROW_TILE = 256
VMEM_LIMIT = 56 * 1024 * 1024
WGRAD_ACC_BYTES = 8 * 1024 * 1024

ADAM_LR = 0.001
ADAM_B1 = 0.9
ADAM_B2 = 0.999
ADAM_EPS = 1e-08
ADAM_WD = 0.01
ADAM_STEP = 10

NT_DIMS = (((1,), (1,)), ((), ()))
TN_DIMS = (((0,), (0,)), ((), ()))


def _params(semantics):
    return pltpu.CompilerParams(dimension_semantics=semantics, vmem_limit_bytes=VMEM_LIMIT)


def _dot(a, b):
    return jnp.dot(a, b, preferred_element_type=F32)


def _dot_nt(a, b):
    return lax.dot_general(a, b, NT_DIMS, preferred_element_type=F32)


def _dot_tn(a, b):
    return lax.dot_general(a, b, TN_DIMS, preferred_element_type=F32)


def _sigmoid(z):
    return 1.0 / (1.0 + jnp.exp(-z))


def _rstd(xv):
    return lax.rsqrt(jnp.mean(xv * xv, axis=-1, keepdims=True) + EPS)


def _rms_bwd(xv, r, g, dn):
    xn = xv * r
    gh = dn * g
    dx = r * (gh - xn * jnp.mean(gh * xn, axis=-1, keepdims=True))
    return dx, dn * xn


def _colsum(a):
    return jnp.sum(a, axis=0, keepdims=True)


def _row_spec(tm, width, col=0):
    return pl.BlockSpec((tm, width), lambda i, col=col: (i, col))


def _full_spec(shape):
    return pl.BlockSpec(shape, lambda i: (0,) * len(shape))


def _in_proj(x, g, w_t):
    T = x.shape[0]
    tm = ROW_TILE

    def body(x_ref, g_ref, w_ref, proj_ref, u_ref, qkv_ref):
        xv = x_ref[...]
        u = (xv * _rstd(xv) * g_ref[...]).astype(BF16)
        u_ref[...] = u
        proj = _dot_nt(u, w_ref[...])
        proj_ref[...] = proj
        qkv_ref[...] = proj[:, 4 * HG_WIDTH:].astype(BF16)

    return pl.pallas_call(
        body, grid=(T // tm,), name="in_proj",
        in_specs=[_row_spec(tm, D_MODEL), _full_spec((1, D_MODEL)), _full_spec((IN_COLS, D_MODEL))],
        out_specs=[_row_spec(tm, IN_COLS), _row_spec(tm, D_MODEL), _row_spec(tm, 3 * SB_WIDTH)],
        out_shape=[jax.ShapeDtypeStruct((T, IN_COLS), F32), jax.ShapeDtypeStruct((T, D_MODEL), BF16),
                   jax.ShapeDtypeStruct((T, 3 * SB_WIDTH), BF16)],
        compiler_params=_params(("parallel",)),
    )(x, g, w_t)


def _out_proj(o_hg, proj, o_sb, x, g_hg, g_sb, g_post, w):
    T = x.shape[0]
    tm = ROW_TILE

    def body(ohg_ref, hg_ref, osb_ref, x_ref, ghg_ref, gsb_ref, gpost_ref, w_ref, cat_ref, mix_ref, h1_ref):
        ohg = ohg_ref[...]
        hg = hg_ref[...]
        osb = osb_ref[...]
        a = ohg * _rstd(ohg) * ghg_ref[...] * (hg * _sigmoid(hg))
        n2 = osb * _rstd(osb) * gsb_ref[...]
        cat = jnp.concatenate([a, n2], axis=1).astype(BF16)
        cat_ref[...] = cat
        mix = _dot(cat, w_ref[...])
        mix_ref[...] = mix
        h1_ref[...] = x_ref[...] + mix * _rstd(mix) * gpost_ref[...]

    return pl.pallas_call(
        body, grid=(T // tm,), name="out_proj",
        in_specs=[_row_spec(tm, HG_WIDTH), _row_spec(tm, HG_WIDTH, 3), _row_spec(tm, SB_WIDTH), _row_spec(tm, D_MODEL),
                  _full_spec((1, HG_WIDTH)), _full_spec((1, SB_WIDTH)), _full_spec((1, D_MODEL)),
                  _full_spec((D_MODEL, D_MODEL))],
        out_specs=[_row_spec(tm, D_MODEL)] * 3,
        out_shape=[jax.ShapeDtypeStruct((T, D_MODEL), BF16), jax.ShapeDtypeStruct((T, D_MODEL), F32),
                   jax.ShapeDtypeStruct((T, D_MODEL), F32)],
        compiler_params=_params(("parallel",)),
    )(o_hg, proj, o_sb, x, g_hg, g_sb, g_post, w)


def _ffn_fwd(h1, g_pre, w_gu_t, w_down, g_post):
    T = h1.shape[0]
    tm = 2 * ROW_TILE
    n_f = D_FF // FFN_TF

    def body(h1_ref, gpre_ref, wgu_ref, wd_ref, gpost_ref, u2_ref, gu_ref, y_ref, h2_ref, acc_ref):
        j = pl.program_id(1)

        @pl.when(j == 0)
        def _():
            hv = h1_ref[...]
            u2_ref[...] = (hv * _rstd(hv) * gpre_ref[...]).astype(BF16)
            acc_ref[...] = jnp.zeros_like(acc_ref)

        u2 = u2_ref[...]
        gate = _dot_nt(u2, wgu_ref[0])
        up = _dot_nt(u2, wgu_ref[1])
        gu_ref[0] = gate.astype(BF16)
        gu_ref[1] = up.astype(BF16)
        a = (gate * _sigmoid(gate) * up).astype(BF16)
        acc_ref[...] += _dot(a, wd_ref[...])

        @pl.when(j == n_f - 1)
        def _():
            y = acc_ref[...]
            y_ref[...] = y
            h2_ref[...] = h1_ref[...] + y * _rstd(y) * gpost_ref[...]

    row = lambda width: pl.BlockSpec((tm, width), lambda i, j: (i, 0))
    vec = pl.BlockSpec((1, D_MODEL), lambda i, j: (0, 0))
    return pl.pallas_call(
        body, grid=(T // tm, n_f), name="ffn_fwd",
        in_specs=[row(D_MODEL), vec,
                  pl.BlockSpec((2, FFN_TF, D_MODEL), lambda i, j: (0, j, 0)),
                  pl.BlockSpec((FFN_TF, D_MODEL), lambda i, j: (j, 0)), vec],
        out_specs=[row(D_MODEL),
                   pl.BlockSpec((2, tm, FFN_TF), lambda i, j: (0, i, j)),
                   row(D_MODEL), row(D_MODEL)],
        out_shape=[jax.ShapeDtypeStruct((T, D_MODEL), BF16), jax.ShapeDtypeStruct((2, T, D_FF), BF16),
                   jax.ShapeDtypeStruct((T, D_MODEL), F32), jax.ShapeDtypeStruct((T, D_MODEL), F32)],
        scratch_shapes=[pltpu.VMEM((tm, D_MODEL), F32)],
        compiler_params=_params(("parallel", "arbitrary")),
    )(h1, g_pre, w_gu_t, w_down, g_post)


def _ple_loss(h2, p, target, w_pp_t, w_pg):
    T = h2.shape[0]
    tm = ROW_TILE

    def body(h2_ref, p_ref, t_ref, wpp_ref, wpg_ref, dh2_ref, de_ref, dg_ref, loss_ref):
        i = pl.program_id(0)
        h2v = h2_ref[...]
        e = _dot_nt(p_ref[...].astype(BF16), wpp_ref[...])
        sg = _sigmoid(_dot(h2v.astype(BF16), wpg_ref[...]))
        diff = h2v + e * sg - t_ref[...]
        part = jnp.sum(jnp.sum(diff * diff, axis=1, keepdims=True), axis=0, keepdims=True) * (0.5 / D_MODEL)

        @pl.when(i == 0)
        def _():
            loss_ref[...] = jnp.zeros_like(loss_ref)

        loss_ref[...] += jnp.broadcast_to(part, loss_ref.shape)
        dh3 = diff * (1.0 / D_MODEL)
        de_ref[...] = (dh3 * sg).astype(BF16)
        dg = (dh3 * e * sg * (1.0 - sg)).astype(BF16)
        dg_ref[...] = dg
        dh2_ref[...] = dh3 + _dot_nt(dg, wpg_ref[...])

    return pl.pallas_call(
        body, grid=(T // tm,), name="ple_loss",
        in_specs=[_row_spec(tm, D_MODEL), _row_spec(tm, PLE_DIM), _row_spec(tm, D_MODEL),
                  _full_spec((D_MODEL, PLE_DIM)), _full_spec((D_MODEL, D_MODEL))],
        out_specs=[_row_spec(tm, D_MODEL)] * 3 + [_full_spec((8, 128))],
        out_shape=[jax.ShapeDtypeStruct((T, D_MODEL), F32), jax.ShapeDtypeStruct((T, D_MODEL), BF16),
                   jax.ShapeDtypeStruct((T, D_MODEL), BF16), jax.ShapeDtypeStruct((8, 128), F32)],
        compiler_params=_params(("arbitrary",)),
    )(h2, p, target, w_pp_t, w_pg)


def _ffn_bwd(dh2, y, h1, gu, g_post, g_pre, w_gu_t, w_down):
    T = h1.shape[0]
    tm = ROW_TILE
    n_f = D_FF // FFN_TF

    def body(dh2_ref, y_ref, h1_ref, gu_ref, gpost_ref, gpre_ref, wgu_hbm, wd_hbm,
             dy_ref, a_ref, dgu_ref, dh1_ref, dgpost_ref, dgpre_ref, wgu_ref, wd_ref, sems):
        i = pl.program_id(0)

        @pl.when(i == 0)
        def _():
            fetch = [pltpu.make_async_copy(wgu_hbm, wgu_ref, sems.at[0]), pltpu.make_async_copy(wd_hbm, wd_ref, sems.at[1])]
            for cp in fetch:
                cp.start()
            dgpost_ref[...] = jnp.zeros_like(dgpost_ref)
            dgpre_ref[...] = jnp.zeros_like(dgpre_ref)
            for cp in fetch:
                cp.wait()

        yv = y_ref[...]
        dh2 = dh2_ref[...]
        dy, gterm = _rms_bwd(yv, _rstd(yv), gpost_ref[...], dh2)
        dgpost_ref[...] += _colsum(gterm)
        dy = dy.astype(BF16)
        dy_ref[...] = dy
        du2 = jnp.zeros((tm, D_MODEL), F32)
        for j in range(n_f):
            cols = slice(j * FFN_TF, (j + 1) * FFN_TF)
            da = _dot_nt(dy, wd_ref[cols, :])
            gate = gu_ref[0, :, cols].astype(F32)
            up = gu_ref[1, :, cols].astype(F32)
            s = _sigmoid(gate)
            sil = gate * s
            a_ref[:, cols] = (sil * up).astype(BF16)
            dgate = (da * up * (s * (1.0 + gate * (1.0 - s)))).astype(BF16)
            dup = (da * sil).astype(BF16)
            dgu_ref[0, :, cols] = dgate
            dgu_ref[1, :, cols] = dup
            du2 = du2 + _dot(dgate, wgu_ref[0, cols, :]) + _dot(dup, wgu_ref[1, cols, :])
        hv = h1_ref[...]
        dx, gterm = _rms_bwd(hv, _rstd(hv), gpre_ref[...], du2)
        dh1_ref[...] = dh2 + dx
        dgpre_ref[...] += _colsum(gterm)

    row = lambda width: pl.BlockSpec((tm, width), lambda i: (i, 0))
    both = pl.BlockSpec((2, tm, D_FF), lambda i: (0, i, 0))
    vec = pl.BlockSpec((1, D_MODEL), lambda i: (0, 0))
    hbm = pl.BlockSpec(memory_space=pl.ANY)
    return pl.pallas_call(
        body, grid=(T // tm,), name="ffn_bwd",
        in_specs=[row(D_MODEL), row(D_MODEL), row(D_MODEL), both, vec, vec, hbm, hbm],
        out_specs=[row(D_MODEL), row(D_FF), both, row(D_MODEL), vec, vec],
        out_shape=[jax.ShapeDtypeStruct((T, D_MODEL), BF16), jax.ShapeDtypeStruct((T, D_FF), BF16),
                   jax.ShapeDtypeStruct((2, T, D_FF), BF16),
                   jax.ShapeDtypeStruct((T, D_MODEL), F32), jax.ShapeDtypeStruct((1, D_MODEL), F32),
                   jax.ShapeDtypeStruct((1, D_MODEL), F32)],
        scratch_shapes=[pltpu.VMEM((2, D_FF, D_MODEL), BF16), pltpu.VMEM((D_FF, D_MODEL), BF16),
                        pltpu.SemaphoreType.DMA((2,))],
        compiler_params=_params(("arbitrary",)),
    )(dh2, y, h1, gu, g_post, g_pre, w_gu_t, w_down)


def _out_bwd(dh1, mix, o_hg, proj, o_sb, g_hg, g_sb, g_post, w, after):
    T = dh1.shape[0]
    tm = ROW_TILE

    def body(dh1_ref, mix_ref, ohg_ref, hg_ref, osb_ref, ghg_ref, gsb_ref, gpost_ref, w_ref, after_ref,
             dmix_ref, dohg_ref, dhg_ref, dosb_ref, dgpost_ref, dghg_ref, dgsb_ref):
        i = pl.program_id(0)

        @pl.when(i == 0)
        def _():
            dgpost_ref[...] = jnp.zeros_like(dgpost_ref)
            dghg_ref[...] = jnp.zeros_like(dghg_ref)
            dgsb_ref[...] = jnp.zeros_like(dgsb_ref)

        mix = mix_ref[...]
        dmix, gterm = _rms_bwd(mix, _rstd(mix), gpost_ref[...], dh1_ref[...])
        dgpost_ref[...] += _colsum(gterm)
        dmix = dmix.astype(BF16)
        dmix_ref[...] = dmix
        dcat = _dot_nt(dmix, w_ref[...])
        da = dcat[:, :HG_WIDTH]
        dn2 = dcat[:, HG_WIDTH:]
        ohg = ohg_ref[...]
        r1 = _rstd(ohg)
        hg = hg_ref[...]
        s = _sigmoid(hg)
        dhg_ref[...] = da * (ohg * r1 * ghg_ref[...]) * (s * (1.0 + hg * (1.0 - s)))
        dohg, gterm = _rms_bwd(ohg, r1, ghg_ref[...], da * (hg * s))
        dohg_ref[...] = dohg
        dghg_ref[...] += _colsum(gterm)
        osb = osb_ref[...]
        dosb, gterm = _rms_bwd(osb, _rstd(osb), gsb_ref[...], dn2)
        dosb_ref[...] = dosb
        dgsb_ref[...] += _colsum(gterm)

    return pl.pallas_call(
        body, grid=(T // tm,), name="out_bwd",
        in_specs=[_row_spec(tm, D_MODEL), _row_spec(tm, D_MODEL), _row_spec(tm, HG_WIDTH), _row_spec(tm, HG_WIDTH, 3),
                  _row_spec(tm, SB_WIDTH), _full_spec((1, HG_WIDTH)), _full_spec((1, SB_WIDTH)),
                  _full_spec((1, D_MODEL)), _full_spec((D_MODEL, D_MODEL)), pl.BlockSpec(memory_space=pl.ANY)],
        out_specs=[_row_spec(tm, D_MODEL), _row_spec(tm, HG_WIDTH), _row_spec(tm, HG_WIDTH), _row_spec(tm, SB_WIDTH),
                   _full_spec((1, D_MODEL)), _full_spec((1, HG_WIDTH)), _full_spec((1, SB_WIDTH))],
        out_shape=[jax.ShapeDtypeStruct((T, D_MODEL), BF16), jax.ShapeDtypeStruct((T, HG_WIDTH), F32),
                   jax.ShapeDtypeStruct((T, HG_WIDTH), F32), jax.ShapeDtypeStruct((T, SB_WIDTH), F32),
                   jax.ShapeDtypeStruct((1, D_MODEL), F32), jax.ShapeDtypeStruct((1, HG_WIDTH), F32),
                   jax.ShapeDtypeStruct((1, SB_WIDTH), F32)],
        compiler_params=_params(("arbitrary",)),
    )(dh1, mix, o_hg, proj, o_sb, g_hg, g_sb, g_post, w, after)


def _in_bwd(d_hgrn, d_hg, d_sq, d_sk, d_sv, x, dh1, g_pre, w_t):
    T = x.shape[0]
    tm = ROW_TILE

    def body(dh_ref, dhg_ref, dsq_ref, dsk_ref, dsv_ref, x_ref, dh1_ref, gpre_ref, w_ref, dproj_ref, dx_ref, dgpre_ref):
        i = pl.program_id(0)

        @pl.when(i == 0)
        def _():
            dgpre_ref[...] = jnp.zeros_like(dgpre_ref)

        dp = jnp.concatenate([dh_ref[...], dhg_ref[...], dsq_ref[...], dsk_ref[...], dsv_ref[...]], axis=1).astype(BF16)
        dproj_ref[...] = dp
        du = _dot(dp, w_ref[...])
        xv = x_ref[...]
        dx, gterm = _rms_bwd(xv, _rstd(xv), gpre_ref[...], du)
        dx_ref[...] = dh1_ref[...] + dx
        dgpre_ref[...] += _colsum(gterm)

    return pl.pallas_call(
        body, grid=(T // tm,), name="in_bwd",
        in_specs=[_row_spec(tm, 3 * HG_WIDTH), _row_spec(tm, HG_WIDTH), _row_spec(tm, SB_WIDTH), _row_spec(tm, SB_WIDTH),
                  _row_spec(tm, SB_WIDTH), _row_spec(tm, D_MODEL), _row_spec(tm, D_MODEL), _full_spec((1, D_MODEL)),
                  _full_spec((IN_COLS, D_MODEL))],
        out_specs=[_row_spec(tm, IN_COLS), _row_spec(tm, D_MODEL), _full_spec((1, D_MODEL))],
        out_shape=[jax.ShapeDtypeStruct((T, IN_COLS), BF16), jax.ShapeDtypeStruct((T, D_MODEL), F32),
                   jax.ShapeDtypeStruct((1, D_MODEL), F32)],
        compiler_params=_params(("arbitrary",)),
    )(d_hgrn, d_hg, d_sq, d_sk, d_sv, x, dh1, g_pre, w_t)


def _wgrad(a, b, name):
    stacked = a.ndim == 3
    S, T, K = a.shape if stacked else (1,) + a.shape
    N = b.shape[1]
    tk = K
    while tk * N * 4 > WGRAD_ACC_BYTES:
        tk //= 2
    assert K % tk == 0 and tk % 128 == 0
    tt = 512
    n_t = T // tt

    def body(a_ref, b_ref, o_ref, acc_ref):
        t = pl.program_id(2)

        @pl.when(t == 0)
        def _():
            acc_ref[...] = jnp.zeros_like(acc_ref)

        acc_ref[...] += _dot_tn(a_ref[...].astype(BF16), b_ref[...].astype(BF16))

        @pl.when(t == n_t - 1)
        def _():
            o_ref[...] = acc_ref[...].astype(BF16)

    if stacked:
        a_spec = pl.BlockSpec((None, tt, tk), lambda s, k, t: (s, t, k))
        o_spec = pl.BlockSpec((None, tk, N), lambda s, k, t: (s, k, 0))
        o_shape = (S, K, N)
    else:
        a_spec = pl.BlockSpec((tt, tk), lambda s, k, t: (t, k))
        o_spec = pl.BlockSpec((tk, N), lambda s, k, t: (k, 0))
        o_shape = (K, N)
    return pl.pallas_call(
        body, grid=(S, K // tk, n_t), name=name,
        in_specs=[a_spec, pl.BlockSpec((tt, N), lambda s, k, t: (t, 0))],
        out_specs=o_spec,
        out_shape=jax.ShapeDtypeStruct(o_shape, BF16),
        scratch_shapes=[pltpu.VMEM((tk, N), F32)],
        compiler_params=_params(("parallel", "parallel", "arbitrary")),
    )(a, b)


def _hg_gates(hq, hf, gam):
    g0 = gam[0:1, :]
    g1 = gam[1:2, :]
    mx = jnp.maximum(g0, g1)
    e0 = jnp.exp(g0 - mx)
    e1 = jnp.exp(g1 - mx)
    lb = e0 / (e0 + e1)
    s = _sigmoid(hf)
    f = lb + (1.0 - lb) * s
    sq = _sigmoid(hq)
    return hq * sq, sq, s, f, (1.0 - lb) * (1.0 - s), jnp.log(f), lb


def _prefix_in_sub(a, rowmod):
    n = a.shape[0]
    sh = 1
    while sh < HG_SUB:
        a = a + jnp.where(rowmod >= sh, pltpu.roll(a, sh, axis=0), 0.0)
        sh *= 2
    return a


def _suffix_in_sub(a, rowmod):
    n = a.shape[0]
    sh = 1
    while sh < HG_SUB:
        a = a + jnp.where(rowmod < HG_SUB - sh, pltpu.roll(a, n - sh, axis=0), 0.0)
        sh *= 2
    return a


def _hg_fwd(proj, gam):
    T = proj.shape[0]
    tb = HG_TILE
    n_sub = tb // HG_SUB

    def body(hq_ref, hf_ref, hi_ref, gam_ref, o_ref, hist_ref, st_ref, qt_ref, kk_ref, b_ref, od_ref):
        i = pl.program_id(0)

        @pl.when(i == 0)
        def _():
            st_ref[...] = jnp.zeros_like(st_ref)

        rowmod_w = lax.broadcasted_iota(jnp.int32, (tb, HG_WIDTH), 0) % HG_SUB
        rowmod = lax.broadcasted_iota(jnp.int32, (tb, 1), 0) % HG_SUB
        q, _, _, f, kk, lf, _ = _hg_gates(hq_ref[...], hf_ref[...], gam_ref[...])
        b = _prefix_in_sub(lf, rowmod_w)
        qt_ref[...] = (q * jnp.exp(b)).astype(BF16)
        kk_ref[...] = kk
        b_ref[...] = b

        for h in range(HG_HEADS):
            cs = slice(h * HG_DK, (h + 1) * HG_DK)
            qh = q[:, cs]
            kh = kk[:, cs]
            fh = f[:, cs]
            vh = hi_ref[:, cs]
            acc = jnp.sum(qh * kh, axis=1, keepdims=True) * vh
            decay = fh
            for d in range(1, HG_SUB):
                if d > 1:
                    decay = decay * pltpu.roll(fh, d - 1, axis=0)
                e = qh * pltpu.roll(kh, d, axis=0) * decay
                sc = jnp.where(rowmod >= d, jnp.sum(e, axis=1, keepdims=True), 0.0)
                acc = acc + sc * pltpu.roll(vh, d, axis=0)
            od_ref[:, cs] = acc

        for m in range(n_sub):
            rs = slice(m * HG_SUB, (m + 1) * HG_SUB)
            for h in range(HG_HEADS):
                cs = slice(h * HG_DK, (h + 1) * HG_DK)
                st = st_ref[h]
                hist_ref[m, h] = st
                o_int = _dot_nt(qt_ref[rs, cs], st.astype(BF16))
                bs = b_ref[rs, cs]
                bl = bs[HG_SUB - 1:HG_SUB, :]
                kt = (kk_ref[rs, cs] * jnp.exp(bl - bs)).astype(BF16)
                st_ref[h] = st * jnp.exp(bl) + _dot_tn(hi_ref[rs, cs].astype(BF16), kt)
                o_ref[rs, cs] = o_int + od_ref[rs, cs]

    col = lambda c: pl.BlockSpec((tb, HG_WIDTH), lambda i, c=c: (i, c))
    return pl.pallas_call(
        body, grid=(T // tb,), name="hg_fwd",
        in_specs=[col(0), col(1), col(2), _full_spec((2, HG_WIDTH))],
        out_specs=[pl.BlockSpec((tb, HG_WIDTH), lambda i: (i, 0)),
                   pl.BlockSpec((n_sub, HG_HEADS, HG_DK, HG_DK), lambda i: (i, 0, 0, 0))],
        out_shape=[jax.ShapeDtypeStruct((T, HG_WIDTH), F32),
                   jax.ShapeDtypeStruct((T // HG_SUB, HG_HEADS, HG_DK, HG_DK), F32)],
        scratch_shapes=[pltpu.VMEM((HG_HEADS, HG_DK, HG_DK), F32), pltpu.VMEM((tb, HG_WIDTH), BF16),
                        pltpu.VMEM((tb, HG_WIDTH), F32), pltpu.VMEM((tb, HG_WIDTH), F32),
                        pltpu.VMEM((tb, HG_WIDTH), F32)],
        compiler_params=_params(("arbitrary",)),
    )(proj, proj, proj, gam)


def _hg_bwd(proj, gam, d_o, hist):
    T = proj.shape[0]
    tb = HG_TILE
    n_sub = tb // HG_SUB
    n_t = T // tb

    def body(hq_ref, hf_ref, hi_ref, gam_ref, do_ref, hist_ref, dout_ref, dlb_ref,
             dst_ref, q_ref, kk_ref, b_ref, dq_ref, dkk_ref, db_ref, dv_ref):
        i = pl.program_id(0)

        @pl.when(i == 0)
        def _():
            dst_ref[...] = jnp.zeros_like(dst_ref)
            dlb_ref[...] = jnp.zeros_like(dlb_ref)

        rowmod_w = lax.broadcasted_iota(jnp.int32, (tb, HG_WIDTH), 0) % HG_SUB
        rowmod = lax.broadcasted_iota(jnp.int32, (tb, 1), 0) % HG_SUB
        last_row = lax.broadcasted_iota(jnp.int32, (HG_SUB, 1), 0) == HG_SUB - 1
        hq = hq_ref[...]
        q, sq, s, f, kk, lf, lb = _hg_gates(hq, hf_ref[...], gam_ref[...])
        b = _prefix_in_sub(lf, rowmod_w)
        q_ref[...] = q
        kk_ref[...] = kk
        b_ref[...] = b

        for m in reversed(range(n_sub)):
            rs = slice(m * HG_SUB, (m + 1) * HG_SUB)
            for h in range(HG_HEADS):
                cs = slice(h * HG_DK, (h + 1) * HG_DK)
                st = hist_ref[m, h]
                dst = dst_ref[h]
                dstb = dst.astype(BF16)
                dos = do_ref[rs, cs].astype(BF16)
                vs = hi_ref[rs, cs].astype(BF16)
                bs = b_ref[rs, cs]
                bl = bs[HG_SUB - 1:HG_SUB, :]
                ebl = jnp.exp(bl)
                qt = q_ref[rs, cs] * jnp.exp(bs)
                decay = jnp.exp(bl - bs)
                kt = kk_ref[rs, cs] * decay
                dqt = _dot(dos, st.astype(BF16))
                dkt = _dot(vs, dstb)
                dv_ref[rs, cs] = _dot_nt(kt.astype(BF16), dstb)
                dst_ref[h] = dst * ebl + _dot_tn(dos, qt.astype(BF16))
                dq_ref[rs, cs] = dqt * jnp.exp(bs)
                dkk_ref[rs, cs] = dkt * decay
                dktk = dkt * kt
                dbl = _colsum(dktk) + _colsum(dst * st) * ebl
                db_ref[rs, cs] = dqt * qt - dktk + jnp.where(last_row, dbl, 0.0)

        for h in range(HG_HEADS):
            cs = slice(h * HG_DK, (h + 1) * HG_DK)
            qh = q[:, cs]
            kh = kk[:, cs]
            fh = f[:, cs]
            vh = hi_ref[:, cs]
            doh = do_ref[:, cs]
            dsc = jnp.sum(doh * vh, axis=1, keepdims=True)
            sc = jnp.sum(qh * kh, axis=1, keepdims=True)
            dqh = dsc * kh
            dkh = dsc * qh
            dvh = sc * doh
            dbh = jnp.zeros_like(qh)
            decay = fh
            for d in range(1, HG_SUB):
                back = tb - d
                valid = rowmod >= d
                if d > 1:
                    decay = decay * pltpu.roll(fh, d - 1, axis=0)
                kd = pltpu.roll(kh, d, axis=0)
                ex = jnp.where(valid, decay, 0.0)
                pd = kd * ex
                sc = jnp.sum(qh * pd, axis=1, keepdims=True)
                dsc = jnp.where(valid, jnp.sum(doh * pltpu.roll(vh, d, axis=0), axis=1, keepdims=True), 0.0)
                dqh = dqh + dsc * pd
                dsq = dsc * qh
                w = dsq * pd
                dkh = dkh + pltpu.roll(dsq * ex, back, axis=0)
                dvh = dvh + pltpu.roll(sc * doh, back, axis=0)
                dbh = dbh + w - pltpu.roll(w, back, axis=0)
            dq_ref[:, cs] += dqh
            dkk_ref[:, cs] += dkh
            dv_ref[:, cs] += dvh
            db_ref[:, cs] += dbh

        dlf = _suffix_in_sub(db_ref[...], rowmod_w)
        df = dlf / f - dkk_ref[...]
        dout_ref[:, 0:HG_WIDTH] = dq_ref[...] * (sq * (1.0 + hq * (1.0 - sq)))
        dout_ref[:, HG_WIDTH:2 * HG_WIDTH] = df * (1.0 - lb) * (s * (1.0 - s))
        dout_ref[:, 2 * HG_WIDTH:3 * HG_WIDTH] = dv_ref[...]
        dlb_ref[...] += _colsum(df * (1.0 - s))

    col = lambda c: pl.BlockSpec((tb, HG_WIDTH), lambda i, c=c: (n_t - 1 - i, c))
    return pl.pallas_call(
        body, grid=(n_t,), name="hg_bwd",
        in_specs=[col(0), col(1), col(2), _full_spec((2, HG_WIDTH)),
                  pl.BlockSpec((tb, HG_WIDTH), lambda i: (n_t - 1 - i, 0)),
                  pl.BlockSpec((n_sub, HG_HEADS, HG_DK, HG_DK), lambda i: (n_t - 1 - i, 0, 0, 0))],
        out_specs=[pl.BlockSpec((tb, 3 * HG_WIDTH), lambda i: (n_t - 1 - i, 0)), _full_spec((1, HG_WIDTH))],
        out_shape=[jax.ShapeDtypeStruct((T, 3 * HG_WIDTH), F32), jax.ShapeDtypeStruct((1, HG_WIDTH), F32)],
        scratch_shapes=[pltpu.VMEM((HG_HEADS, HG_DK, HG_DK), F32)] + [pltpu.VMEM((tb, HG_WIDTH), F32)] * 7,
        compiler_params=_params(("arbitrary",)),
    )(proj, proj, proj, gam, d_o, hist)


def _sb_masks():
    row = lax.broadcasted_iota(jnp.int32, (SB_TK, SB_TK), 0)
    col = lax.broadcasted_iota(jnp.int32, (SB_TK, SB_TK), 1)
    suffix = (row >= col).astype(BF16)
    prefix = (row <= col).astype(BF16)
    query = lax.broadcasted_iota(jnp.int32, (SB_TQ, SB_TK), 0)
    key = lax.broadcasted_iota(jnp.int32, (SB_TQ, SB_TK), 1)
    causal = [key + r * SB_TK < query for r in range(SB_TQ // SB_TK)]
    lane = lax.broadcasted_iota(jnp.int32, (SB_TQ, SB_BLOCK), 1)
    return suffix, prefix, causal, lane


def _sum_right(a, suffix):
    return _dot(a.astype(BF16), suffix)


def _per_lane_tile(t, c, op):
    return jnp.concatenate([op(t[:, n:n + SB_BLOCK], c) for n in range(0, t.shape[1], SB_BLOCK)], axis=1)


def _sb_block(qb, kj, suffix, causal, c):
    z = _dot_nt(qb, kj)
    sp = jnp.maximum(z, 0.0) + jnp.log(1.0 + jnp.exp(-jnp.abs(z)))
    if causal is not None:
        sp = jnp.where(causal, sp, 0.0)
    big_l = _sum_right(sp, suffix)
    a = jnp.exp(_per_lane_tile(z - big_l, c, jnp.subtract))
    if causal is not None:
        a = jnp.where(causal, a, 0.0)
    return z, a, big_l


def _sb_tile_copies(to_hbm, bufs, stores, sems, slot, head, i, j, n_chain):
    copies = []
    for ch in range(n_chain):
        for w in range(2):
            vmem, hbm = bufs[w].at[slot, ch], stores[w].at[head + ch, i, j]
            src, dst = (vmem, hbm) if to_hbm else (hbm, vmem)
            copies.append(pltpu.make_async_copy(src, dst, sems.at[slot, ch, w]))
    return copies


def _half_masked(pair, lane, scale=1.0):
    pair = pair.astype(F32) * scale
    return jnp.where(lane < SB_DH, pair, 0.0).astype(BF16), jnp.where(lane < SB_DH, 0.0, pair).astype(BF16)


def _sb_fwd(qkv):
    T = qkv.shape[0]
    width = SB_PP * SB_BLOCK
    n_chain = 2 * SB_PP
    n_sub = SB_TQ // SB_TK

    def body(q_ref, k_ref, v_ref, o_ref, a_st, s_st, acc_ref, c_ref, qm, a_buf, s_buf, sems):
        g = pl.program_id(0)
        i = pl.program_id(1)
        suffix, _, causal, lane = _sb_masks()
        for pp in range(SB_PP):
            qm[2 * pp], qm[2 * pp + 1] = _half_masked(q_ref[:, pp * SB_BLOCK:(pp + 1) * SB_BLOCK], lane, SB_DH ** -0.5)
        acc_ref[...] = jnp.zeros_like(acc_ref)
        c_ref[...] = jnp.zeros_like(c_ref)
        a_buf[...] = jnp.zeros_like(a_buf)
        s_buf[...] = jnp.zeros_like(s_buf)

        def copies(slot, j):
            return _sb_tile_copies(True, (a_buf, s_buf), (a_st, s_st), sems, slot, g * n_chain, i, j, n_chain)

        def step(j, slot, r=None, reuse=True):
            rows = pl.ds(pl.multiple_of(j * SB_TK, SB_TK), SB_TK)
            qs = slice(0 if r is None else r * SB_TK, SB_TQ)
            mask = None if r is None else causal[r][qs]
            if reuse:
                for cp in copies(slot, j):
                    cp.wait()
            for ch in range(n_chain):
                pair = slice((ch // 2) * SB_BLOCK, (ch // 2 + 1) * SB_BLOCK)
                c = c_ref[ch, qs]
                z, a, big_l = _sb_block(qm[ch, qs], k_ref[rows, pair], suffix, mask, c)
                ab = a.astype(BF16)
                acc_ref[ch, qs] += _dot(ab, v_ref[rows, pair])
                a_buf[slot, ch, qs] = ab
                s_buf[slot, ch, qs] = z.astype(BF16)
                c_ref[ch, qs] = c + jnp.broadcast_to(big_l[:, 0:1], c.shape)
            for cp in copies(slot, j):
                cp.start()

        first = i * n_sub
        for s, r in enumerate(reversed(range(n_sub))):
            step(first + r, s % 2, r, reuse=s >= 2)

        @pl.loop(0, first)
        def _(t):
            step(first - 1 - t, (n_sub + t) % 2)

        for slot in range(2):
            for cp in copies(slot, 0):
                cp.wait()
        for pp in range(SB_PP):
            o_ref[:, pp * SB_BLOCK:(pp + 1) * SB_BLOCK] = jnp.where(lane < SB_DH, acc_ref[2 * pp], acc_ref[2 * pp + 1])

    assert n_sub >= 2 and n_sub % 2 == 0
    n_g = SB_WIDTH // width
    blk = lambda part: pl.BlockSpec((SB_TQ, width), lambda g, i, part=part: (i, part * n_g + g))
    whole = lambda part: pl.BlockSpec((T, width), lambda g, i, part=part: (0, part * n_g + g))
    tiles = jax.ShapeDtypeStruct((SB_HEADS, T // SB_TQ, T // SB_TK, SB_TQ, SB_TK), BF16)
    return pl.pallas_call(
        body, grid=(n_g, T // SB_TQ), name="sb_fwd",
        in_specs=[blk(0), whole(1), whole(2)],
        out_specs=[blk(0), pl.BlockSpec(memory_space=pl.ANY), pl.BlockSpec(memory_space=pl.ANY)],
        out_shape=[jax.ShapeDtypeStruct((T, SB_WIDTH), F32), tiles, tiles],
        scratch_shapes=[pltpu.VMEM((n_chain, SB_TQ, SB_BLOCK), F32), pltpu.VMEM((n_chain, SB_TQ, SB_BLOCK), F32),
                        pltpu.VMEM((n_chain, SB_TQ, SB_BLOCK), BF16),
                        pltpu.VMEM((2, n_chain, SB_TQ, SB_TK), BF16), pltpu.VMEM((2, n_chain, SB_TQ, SB_TK), BF16),
                        pltpu.SemaphoreType.DMA((2, n_chain, 2))],
        compiler_params=_params(("parallel", "arbitrary")),
    )(qkv, qkv, qkv)


def _sb_bwd(qkv, a_st, s_st, d_o, after):
    T = qkv.shape[0]
    width = SB_PP * SB_BLOCK
    n_chain = 2 * SB_PP
    n_sub = SB_TQ // SB_TK
    scale = SB_DH ** -0.5

    def body(q_ref, k_ref, v_ref, a_st_ref, s_st_ref, do_ref, after_ref, dq_ref, dk_ref, dv_ref, acc_ref, gc_ref, qm, dom,
             a_buf, s_buf, sems):
        g_idx = pl.program_id(0)
        i = pl.program_id(1)

        @pl.when(i == 0)
        def _():
            dk_ref[...] = jnp.zeros_like(dk_ref)
            dv_ref[...] = jnp.zeros_like(dv_ref)

        _, prefix, causal, lane = _sb_masks()
        for pp in range(SB_PP):
            pair = slice(pp * SB_BLOCK, (pp + 1) * SB_BLOCK)
            qm[2 * pp], qm[2 * pp + 1] = _half_masked(q_ref[:, pair], lane, scale)
            dom[2 * pp], dom[2 * pp + 1] = _half_masked(do_ref[:, pair], lane)
        acc_ref[...] = jnp.zeros_like(acc_ref)
        gc_ref[...] = jnp.zeros_like(gc_ref)

        def copies(slot, j):
            return _sb_tile_copies(False, (a_buf, s_buf), (a_st_ref, s_st_ref), sems, slot, g_idx * n_chain, i, j, n_chain)

        def step(j, slot, r=None, last=False):
            rows = pl.ds(pl.multiple_of(j * SB_TK, SB_TK), SB_TK)
            qs = slice(0 if r is None else r * SB_TK, SB_TQ)
            mask = None if r is None else causal[r][qs]
            for cp in copies(slot, j):
                cp.wait()
            if not last:
                for cp in copies(1 - slot, j + 1):
                    cp.start()
            for pp in range(SB_PP):
                pair = slice(pp * SB_BLOCK, (pp + 1) * SB_BLOCK)
                kj = k_ref[rows, pair]
                vj = v_ref[rows, pair]
                dk = jnp.zeros((SB_TK, SB_BLOCK), F32)
                dv = jnp.zeros((SB_TK, SB_BLOCK), F32)
                for ch in (2 * pp, 2 * pp + 1):
                    ab = a_buf[slot, ch, qs]
                    g = ab.astype(F32) * _dot_nt(dom[ch, qs], vj)
                    g_left = _sum_right(g, prefix)
                    gc = gc_ref[ch, qs]
                    dz = g - _sigmoid(s_buf[slot, ch, qs].astype(F32)) * _per_lane_tile(g_left, gc, jnp.add)
                    if mask is not None:
                        dz = jnp.where(mask, dz, 0.0)
                    dzb = dz.astype(BF16)
                    dk = dk + _dot_tn(dzb, qm[ch, qs])
                    dv = dv + _dot_tn(ab, dom[ch, qs])
                    acc_ref[ch, qs] += _dot(dzb, kj)
                    gc_ref[ch, qs] = gc + jnp.broadcast_to(g_left[:, SB_TK - 1:SB_TK], gc.shape)
                dk_ref[rows, pair] += dk
                dv_ref[rows, pair] += dv

        first = i * n_sub
        for cp in copies(0, 0):
            cp.start()

        @pl.loop(0, first)
        def _(j):
            step(j, j % 2)

        for r in range(n_sub):
            step(first + r, r % 2, r, last=r == n_sub - 1)
        for pp in range(SB_PP):
            dq_ref[:, pp * SB_BLOCK:(pp + 1) * SB_BLOCK] = scale * jnp.where(lane < SB_DH, acc_ref[2 * pp],
                                                                             acc_ref[2 * pp + 1])
    n_g = SB_WIDTH // width
    blk = lambda part: pl.BlockSpec((SB_TQ, width), lambda g, i, part=part: (i, part * n_g + g))
    whole = lambda part: pl.BlockSpec((T, width), lambda g, i, part=part: (0, part * n_g + g))
    return pl.pallas_call(
        body, grid=(n_g, T // SB_TQ), name="sb_bwd",
        in_specs=[blk(0), whole(1), whole(2), pl.BlockSpec(memory_space=pl.ANY), pl.BlockSpec(memory_space=pl.ANY), blk(0),
                  pl.BlockSpec(memory_space=pl.ANY)],
        out_specs=[blk(0), whole(0), whole(0)],
        out_shape=[jax.ShapeDtypeStruct((T, SB_WIDTH), F32)] * 3,
        scratch_shapes=[pltpu.VMEM((n_chain, SB_TQ, SB_BLOCK), F32), pltpu.VMEM((n_chain, SB_TQ, SB_BLOCK), F32),
                        pltpu.VMEM((n_chain, SB_TQ, SB_BLOCK), BF16), pltpu.VMEM((n_chain, SB_TQ, SB_BLOCK), BF16),
                        pltpu.VMEM((2, n_chain, SB_TQ, SB_TK), BF16), pltpu.VMEM((2, n_chain, SB_TQ, SB_TK), BF16),
                        pltpu.SemaphoreType.DMA((2, n_chain, 2))],
        compiler_params=_params(("parallel", "arbitrary")),
    )(qkv, qkv, qkv, a_st, s_st, d_o, after)


HBM = pl.BlockSpec(memory_space=pltpu.HBM)
MESH = pl.DeviceIdType.MESH


def _place():
    return lax.axis_index("x"), lax.axis_index("y"), lax.axis_index("c")


def _all_gather(block):
    rows, cols = block.shape

    def body(x_ref, out_ref, send_sems, recv_sems, local_sem):
        x, y, c = _place()
        me, sibling = (x, y, c), (x, y, 1 - c)
        chips = [(1 - x, y), (x, 1 - y), (1 - x, 1 - y)]

        def slot(px, py, pc):
            return out_ref.at[4 * px + 2 * py + pc]

        def copy(k, blk, to, src=None):
            return pltpu.make_async_remote_copy(
                src_ref=slot(*blk) if src is None else src, dst_ref=slot(*blk),
                send_sem=send_sems.at[k], recv_sem=recv_sems.at[k], device_id=to, device_id_type=MESH)

        mine = pltpu.make_async_copy(x_ref, slot(*me), local_sem)
        mine.start()
        first = [copy(0, me, sibling, src=x_ref)]
        first += [copy(1 + j, me, (*chip, c), src=x_ref) for j, chip in enumerate(chips)]
        for cp in first:
            cp.start()
        passed = [copy(4 + j, (*chip, c), sibling) for j, chip in enumerate(chips)]
        for j, chip in enumerate(chips):
            copy(1 + j, (*chip, c), me).wait_recv()
            passed[j].start()
        copy(0, sibling, me).wait_recv()
        for j, chip in enumerate(chips):
            copy(4 + j, (*chip, 1 - c), me).wait_recv()
        for cp in first + passed:
            cp.wait_send()
        mine.wait()

    return pl.pallas_call(
        body, name="all_gather",
        out_shape=jax.ShapeDtypeStruct((N_DEV, rows, cols), block.dtype),
        in_specs=[HBM], out_specs=HBM,
        scratch_shapes=[pltpu.SemaphoreType.DMA((7,)), pltpu.SemaphoreType.DMA((7,)), pltpu.SemaphoreType.DMA],
    )(block)


SEM = pl.BlockSpec(memory_space=pltpu.SEMAPHORE)
ANY = pl.BlockSpec(memory_space=pl.ANY)
SPLIT_EFFECT = pltpu.SideEffectType.DATAFLOW_SIDE_EFFECTING
N_PEER = N_DEV - 1


def _flow_copies(kind, src_ref, land_ref, send_sems, recv_sems):
    x, y, c = _place()
    copies = []
    for r in range(1, N_DEV):
        px = 1 - x if r & 4 else x
        py = 1 - y if r & 2 else y
        pc = 1 - c if r & 1 else c
        if kind == "gather":
            src, dst = src_ref, land_ref.at[4 * x + 2 * y + c]
        else:
            src, dst = src_ref.at[4 * px + 2 * py + pc], land_ref.at[r - 1]
        copies.append(pltpu.make_async_remote_copy(
            src_ref=src, dst_ref=dst, send_sem=send_sems.at[r - 1], recv_sem=recv_sems.at[r - 1],
            device_id=(px, py, pc), device_id_type=MESH))
    return copies


def _landing(kind, src):
    if kind == "gather":
        return lax.empty((N_DEV,) + src.shape, src.dtype)
    return lax.empty((N_PEER,) + src.shape[1:], src.dtype)


def _push_start(kinds, arrays, after, name):
    n = len(arrays)

    def body(*refs):
        ins, sems, token = refs[:n], refs[n + 1:2 * n + 1], refs[3 * n + 1]
        for f, kind in enumerate(kinds):
            for cp in _flow_copies(kind, ins[2 * f], ins[2 * f + 1], sems[2 * f], sems[2 * f + 1]):
                cp.start()
        token[...] = jnp.zeros_like(token)

    outs = pl.pallas_call(
        body, name=name,
        out_shape=[pltpu.SemaphoreType.DMA((N_PEER,))] * n + [pltpu.HBM(a.shape, a.dtype) for a in arrays]
        + [jax.ShapeDtypeStruct((8, 128), F32)],
        in_specs=[HBM] * n + [ANY], out_specs=[SEM] * n + [HBM] * n + [pl.BlockSpec(memory_space=pltpu.VMEM)],
        input_output_aliases={i: n + i for i in range(n)},
        compiler_params=pltpu.CompilerParams(has_side_effects=SPLIT_EFFECT),
    )(*[pltpu.with_memory_space_constraint(a, pltpu.HBM) for a in arrays], after)
    return outs[:n], outs[n:2 * n], outs[2 * n]


def _push_wait(kinds, sems, arrays, after, name):
    n = len(arrays)

    def body(*refs):
        ins, sem_refs = refs[:n], refs[n:2 * n]
        for f, kind in enumerate(kinds):
            for cp in _flow_copies(kind, ins[2 * f], ins[2 * f + 1], sem_refs[2 * f], sem_refs[2 * f + 1]):
                cp.wait_send()
                cp.wait_recv()

    outs = pl.pallas_call(
        body, name=name,
        out_shape=[pltpu.HBM(a.shape, a.dtype) for a in arrays],
        in_specs=[HBM] * n + [SEM] * n + [ANY], out_specs=[HBM] * n,
        input_output_aliases={i: i for i in range(n)},
        compiler_params=pltpu.CompilerParams(has_side_effects=SPLIT_EFFECT),
    )(*arrays, *sems, after)
    return outs


def _sum_blocks(own, got, name):
    n, rows, cols = got.shape
    tr = next(t for t in (128, 64, 32) if rows % t == 0)

    def body(own_ref, got_ref, out_ref):
        acc = own_ref[...].astype(F32)
        for r in range(n):
            acc = acc + got_ref[r].astype(F32)
        out_ref[...] = acc

    return pl.pallas_call(
        body, grid=(rows // tr,), name=name,
        in_specs=[pl.BlockSpec((tr, cols), lambda i: (i, 0)), pl.BlockSpec((n, tr, cols), lambda i: (0, i, 0))],
        out_specs=pl.BlockSpec((tr, cols), lambda i: (i, 0)),
        out_shape=jax.ShapeDtypeStruct((rows, cols), F32),
        compiler_params=_params(("parallel",)),
    )(own, got)


def _adamw_math(w, g, m, v):
    m = ADAM_B1 * m + (1.0 - ADAM_B1) * g
    v = ADAM_B2 * v + (1.0 - ADAM_B2) * (g * g)
    m_hat = m / (1.0 - ADAM_B1 ** ADAM_STEP)
    v_hat = v / (1.0 - ADAM_B2 ** ADAM_STEP)
    delta = -ADAM_LR * (m_hat / (jnp.sqrt(v_hat) + ADAM_EPS) + ADAM_WD * w)
    return delta, m, v


def _adamw(w, g, m, v, name):
    rows, cols = w.shape
    tr = rows if rows <= 352 else 256

    def body(w_ref, g_ref, m_ref, v_ref, d_out, m_out, v_out):
        d_out[...], m_out[...], v_out[...] = _adamw_math(w_ref[...], g_ref[...], m_ref[...], v_ref[...])

    spec = pl.BlockSpec((tr, cols), lambda i: (i, 0))
    return pl.pallas_call(
        body, grid=(rows // tr,), name=name,
        in_specs=[spec] * 4, out_specs=[spec] * 3,
        out_shape=[jax.ShapeDtypeStruct((rows, cols), F32)] * 3,
        compiler_params=_params(("parallel",)),
    )(w, g, m, v)


ROW_ATTN_PRE, ROW_ATTN_POST, ROW_FFN_PRE, ROW_FFN_POST, ROW_OUT_NORMS, ROW_GAMMA, ROW_LOSS = range(7)


def _small_update(smalls, w, m, v):
    def body(s_ref, w_ref, m_ref, v_ref, g_out, d_out, m_out, v_out):
        g = s_ref[0]
        for j in range(1, N_DEV):
            g = g + s_ref[j]
        wv = w_ref[...]
        g0 = wv[ROW_GAMMA:ROW_GAMMA + 1, 0:HG_WIDTH]
        g1 = wv[ROW_GAMMA:ROW_GAMMA + 1, HG_WIDTH:]
        mx = jnp.maximum(g0, g1)
        e0 = jnp.exp(g0 - mx)
        e1 = jnp.exp(g1 - mx)
        lb = e0 / (e0 + e1)
        dg0 = g[ROW_GAMMA:ROW_GAMMA + 1, 0:HG_WIDTH] * lb * (1.0 - lb)
        row = lax.broadcasted_iota(jnp.int32, g.shape, 0)
        g = jnp.where(row == ROW_GAMMA, jnp.concatenate([dg0, -dg0], axis=1), g)
        g_out[...] = g
        d_out[...], m_out[...], v_out[...] = _adamw_math(wv, g, m_ref[...], v_ref[...])

    return pl.pallas_call(
        body, name="small_update",
        out_shape=[jax.ShapeDtypeStruct((8, D_MODEL), F32)] * 4,
    )(smalls, w, m, v)


def _local_step(x, p, target, g_attn_pre, w_in_t, gamma, g_hg, g_sb, g_attn_post, g_ffn_pre, g_ffn_post, later_weights,
                grads_ready):
    proj, u, qkv = _in_proj(x, g_attn_pre, w_in_t)
    o_hg, hist = _hg_fwd(proj, gamma)
    o_sb, a_st, s_st = _sb_fwd(qkv)
    w_out, w_gu_t, w_down, w_pp_t, w_pg = later_weights(o_sb)
    cat, mix, h1 = _out_proj(o_hg, proj, o_sb, x, g_hg, g_sb, g_attn_post, w_out)
    u2, gu, y, h2 = _ffn_fwd(h1, g_ffn_pre, w_gu_t, w_down, g_ffn_post)
    dh2, de, dg, loss = _ple_loss(h2, p, target, w_pp_t, w_pg)

    d_wpp_t = _wgrad(de, p, "wgrad_ple_proj")
    d_wpg = _wgrad(h2, dg, "wgrad_ple_gate")
    dy, a, dgu, dh1, dg_ffn_post, dg_ffn_pre = _ffn_bwd(dh2, y, h1, gu, g_ffn_post, g_ffn_pre, w_gu_t, w_down)
    d_wdown = _wgrad(a, dy, "wgrad_down")
    sent = grads_ready("ffn", (_wgrad(dgu, u2, "wgrad_gate_up"), d_wdown, d_wpp_t, d_wpg))
    dmix, d_ohg, d_hg, d_osb, dg_attn_post, dg_hg, dg_sb = _out_bwd(dh1, mix, o_hg, proj, o_sb, g_hg, g_sb, g_attn_post,
                                                                    w_out, sent)
    sent = grads_ready("out", (_wgrad(cat, dmix, "wgrad_out"),))
    dq, dk, dv = _sb_bwd(qkv, a_st, s_st, d_osb, sent)
    d_hgrn, d_lb = _hg_bwd(proj, gamma, d_ohg, hist)
    dproj, dx, dg_attn_pre = _in_bwd(d_hgrn, d_hg, dq, dk, dv, x, dh1, g_attn_pre, w_in_t)
    grads_ready("in", (_wgrad(dproj, u, "wgrad_in"),))
    return loss, dx, (dg_attn_pre, dg_attn_post, dg_ffn_pre, dg_ffn_post, dg_hg, dg_sb, d_lb)


def _pack_row(*parts):
    return jnp.concatenate([q.reshape(1, -1) for q in parts], axis=1)


def kernel(x, p, attn_pre_norm, w_in, hg_lower_gamma, hg_out_norm, sb_out_norm, w_out, attn_post_norm, ffn_pre_norm, w_gate_up, w_down, ffn_post_norm, ple_proj, ple_gate, loss_target, m_attn_pre_norm, m_w_in, m_hg_lower_gamma, m_hg_out_norm, m_sb_out_norm, m_w_out, m_attn_post_norm, m_ffn_pre_norm, m_w_gate_up, m_w_down, m_ffn_post_norm, m_ple_proj, m_ple_gate, v_attn_pre_norm, v_w_in, v_hg_lower_gamma, v_hg_out_norm, v_sb_out_norm, v_w_out, v_attn_post_norm, v_ffn_pre_norm, v_w_gate_up, v_w_down, v_ffn_post_norm, v_ple_proj, v_ple_gate):
    big = (w_in[0], w_out[0], w_gate_up[0], w_down[0], ple_proj[0], ple_gate[0])
    big_m = (m_w_in[0], m_w_out[0], m_w_gate_up[0], m_w_down[0], m_ple_proj[0], m_ple_gate[0])
    big_v = (v_w_in[0], v_w_out[0], v_w_gate_up[0], v_w_down[0], v_ple_proj[0], v_ple_gate[0])
    names = ("w_in", "w_out", "w_gate_up", "w_down", "ple_proj", "ple_gate")

    by_columns = (True, False, True, False, True, False)
    shards = [(w.T if t else w).astype(BF16) for w, t in zip(big, by_columns)]
    w_in_t = _all_gather(shards[0]).reshape(IN_COLS, D_MODEL)
    me = 4 * lax.axis_index("x") + 2 * lax.axis_index("y") + lax.axis_index("c")
    later = shards[1:]
    w_sems, w_thru, token = _push_start(
        ("gather",) * len(later), [a for s in later for a in (s, _landing("gather", s))], w_in_t, "weights_start")

    def later_weights(after):
        done = _push_wait(("gather",) * len(later), w_sems, w_thru, after, "weights_wait")
        whole = [lax.dynamic_update_index_in_dim(got, mine, me, 0) for mine, got in zip(done[0::2], done[1::2])]
        return (whole[0].reshape(D_MODEL, D_MODEL), whole[1].reshape(2, D_FF, D_MODEL), whole[2].reshape(D_FF, D_MODEL),
                whole[3].reshape(D_MODEL, PLE_DIM), whole[4].reshape(D_MODEL, D_MODEL))

    flights = {}

    def by_owner(g):
        return g.reshape(N_DEV, -1, g.shape[-1])

    def grads_ready(group, grads):
        blocks = [by_owner(g) for g in grads]
        if group == "in":
            flights[group] = blocks[0]
            return None
        sems, thru, sent = _push_start(
            ("scatter",) * len(blocks), [a for b in blocks for a in (b, _landing("scatter", b))], token,
            "grads_" + group + "_start")
        flights[group] = (sems, thru)
        return sent

    loss, dx, smalls = _local_step(
        x[0], p[0, 0], loss_target[0], attn_pre_norm + token[0:1, 0:1], w_in_t,
        hg_lower_gamma, hg_out_norm, sb_out_norm, attn_post_norm, ffn_pre_norm, ffn_post_norm, later_weights, grads_ready)
    dg_attn_pre, dg_attn_post, dg_ffn_pre, dg_ffn_post, dg_hg, dg_sb, d_lb = smalls

    zeros_half = jnp.zeros((1, HG_WIDTH), F32)
    small_pack = jnp.concatenate([
        dg_attn_pre, dg_attn_post, dg_ffn_pre, dg_ffn_post, _pack_row(dg_hg, dg_sb), _pack_row(d_lb, zeros_half),
        jnp.broadcast_to(loss[0:1, 0:1], (1, D_MODEL)), jnp.zeros((1, D_MODEL), F32)], axis=0)
    in_blocks = flights["in"]
    in_sems, in_thru, in_token = _push_start(
        ("scatter", "gather"),
        (in_blocks, _landing("scatter", in_blocks), small_pack, _landing("gather", small_pack)), token,
        "grads_in_start")
    ffn = _push_wait(("scatter",) * 4, *flights["ffn"], in_token, "grads_ffn_wait")
    sent_out, got_out = _push_wait(("scatter",), *flights["out"], ffn[1], "grads_out_wait")
    sent_in, got_in, my_small, all_smalls = _push_wait(("scatter", "gather"), in_sems, in_thru, got_out, "grads_in_wait")
    all_smalls = lax.dynamic_update_index_in_dim(all_smalls, my_small, me, 0)

    def reduced(sent, got, name):
        return _sum_blocks(lax.dynamic_index_in_dim(sent, me, 0, keepdims=False), got, "grad_sum_" + name)

    sums = (reduced(sent_in, got_in, "w_in"), reduced(sent_out, got_out, "w_out"), reduced(ffn[0], ffn[1], "w_gate_up"),
            reduced(ffn[2], ffn[3], "w_down"), reduced(ffn[4], ffn[5], "ple_proj"), reduced(ffn[6], ffn[7], "ple_gate"))

    out_g, out_d, out_m, out_v = {}, {}, {}, {}
    for i, name in enumerate(names):
        g = sums[i].T if by_columns[i] else sums[i]
        out_g[name] = g[None]
        d, m, v = _adamw(big[i], g, big_m[i], big_v[i], "adamw_" + name)
        out_d[name], out_m[name], out_v[name] = d[None], m[None], v[None]

    def small_rows(pre, gam, hg, sb, post, fpre, fpost):
        return jnp.concatenate([pre, post, fpre, fpost, _pack_row(hg, sb), _pack_row(gam[0], gam[1]),
                                jnp.zeros((2, D_MODEL), F32)], axis=0)

    packs = _small_update(
        all_smalls,
        small_rows(attn_pre_norm, hg_lower_gamma, hg_out_norm, sb_out_norm, attn_post_norm, ffn_pre_norm, ffn_post_norm),
        small_rows(m_attn_pre_norm, m_hg_lower_gamma, m_hg_out_norm, m_sb_out_norm, m_attn_post_norm, m_ffn_pre_norm,
                   m_ffn_post_norm),
        small_rows(v_attn_pre_norm, v_hg_lower_gamma, v_hg_out_norm, v_sb_out_norm, v_attn_post_norm, v_ffn_pre_norm,
                   v_ffn_post_norm))

    def unpack(pk):
        return {
            "attn_pre_norm": pk[ROW_ATTN_PRE:ROW_ATTN_PRE + 1],
            "hg_lower_gamma": pk[ROW_GAMMA].reshape(2, HG_WIDTH),
            "hg_out_norm": pk[ROW_OUT_NORMS:ROW_OUT_NORMS + 1, :HG_WIDTH],
            "sb_out_norm": pk[ROW_OUT_NORMS:ROW_OUT_NORMS + 1, HG_WIDTH:],
            "attn_post_norm": pk[ROW_ATTN_POST:ROW_ATTN_POST + 1],
            "ffn_pre_norm": pk[ROW_FFN_PRE:ROW_FFN_PRE + 1],
            "ffn_post_norm": pk[ROW_FFN_POST:ROW_FFN_POST + 1],
        }

    sg, sd, sm, sv = (unpack(pk) for pk in packs)
    out_g.update(sg), out_d.update(sd), out_m.update(sm), out_v.update(sv)
    order = ("attn_pre_norm", "w_in", "hg_lower_gamma", "hg_out_norm", "sb_out_norm", "w_out", "attn_post_norm",
             "ffn_pre_norm", "w_gate_up", "w_down", "ffn_post_norm", "ple_proj", "ple_gate")
    total_loss = packs[0][ROW_LOSS, 0]
    return (total_loss, dx[None], *[out_g[n] for n in order], *[out_d[n] for n in order],
            *[out_m[n] for n in order], *[out_v[n] for n in order])
```

```python
import jax
import jax.numpy as jnp
from jax import lax
from jax.experimental import pallas as pl
from jax.experimental.pallas import tpu as pltpu

F32 = jnp.float32
BF16 = jnp.bfloat16

D_MODEL = 1024
HG_WIDTH = 512
HG_HEADS = 4
HG_DK = 128
SB_WIDTH = 512
SB_HEADS = 8
SB_DH = 64
SB_BLOCK = 128
SB_PP = 2
SB_TQ = 1024
SB_TK = 256
D_FF = 2816
PLE_DIM = 256
IN_COLS = 4 * HG_WIDTH + 3 * SB_WIDTH
EPS = 1e-6
N_DEV = 8

HG_SUB = 16
HG_TILE = 128
FFN_TF = 1408
ROW_TILE = 256
VMEM_LIMIT = 56 * 1024 * 1024
WGRAD_ACC_BYTES = 8 * 1024 * 1024

ADAM_LR = 0.001
ADAM_B1 = 0.9
ADAM_B2 = 0.999
ADAM_EPS = 1e-08
ADAM_WD = 0.01
ADAM_STEP = 10

NT_DIMS = (((1,), (1,)), ((), ()))
TN_DIMS = (((0,), (0,)), ((), ()))


def _params(semantics):
    return pltpu.CompilerParams(dimension_semantics=semantics, vmem_limit_bytes=VMEM_LIMIT)


def _dot(a, b):
    return jnp.dot(a, b, preferred_element_type=F32)


def _dot_nt(a, b):
    return lax.dot_general(a, b, NT_DIMS, preferred_element_type=F32)


def _dot_tn(a, b):
    return lax.dot_general(a, b, TN_DIMS, preferred_element_type=F32)


def _sigmoid(z):
    return 1.0 / (1.0 + jnp.exp(-z))


def _rstd(xv):
    return lax.rsqrt(jnp.mean(xv * xv, axis=-1, keepdims=True) + EPS)


def _rms_bwd(xv, r, g, dn):
    xn = xv * r
    gh = dn * g
    dx = r * (gh - xn * jnp.mean(gh * xn, axis=-1, keepdims=True))
    return dx, dn * xn


def _colsum(a):
    return jnp.sum(a, axis=0, keepdims=True)


def _row_spec(tm, width, col=0):
    return pl.BlockSpec((tm, width), lambda i, col=col: (i, col))


def _full_spec(shape):
    return pl.BlockSpec(shape, lambda i: (0,) * len(shape))


def _in_proj(x, g, w_t):
    T = x.shape[0]
    tm = ROW_TILE

    def body(x_ref, g_ref, w_ref, proj_ref, u_ref, qkv_ref):
        xv = x_ref[...]
        u = (xv * _rstd(xv) * g_ref[...]).astype(BF16)
        u_ref[...] = u
        proj = _dot_nt(u, w_ref[...])
        proj_ref[...] = proj
        qkv_ref[...] = proj[:, 4 * HG_WIDTH:].astype(BF16)

    return pl.pallas_call(
        body, grid=(T // tm,), name="in_proj",
        in_specs=[_row_spec(tm, D_MODEL), _full_spec((1, D_MODEL)), _full_spec((IN_COLS, D_MODEL))],
        out_specs=[_row_spec(tm, IN_COLS), _row_spec(tm, D_MODEL), _row_spec(tm, 3 * SB_WIDTH)],
        out_shape=[jax.ShapeDtypeStruct((T, IN_COLS), F32), jax.ShapeDtypeStruct((T, D_MODEL), BF16),
                   jax.ShapeDtypeStruct((T, 3 * SB_WIDTH), BF16)],
        compiler_params=_params(("parallel",)),
    )(x, g, w_t)


def _out_proj(o_hg, proj, o_sb, x, g_hg, g_sb, g_post, w):
    T = x.shape[0]
    tm = ROW_TILE

    def body(ohg_ref, hg_ref, osb_ref, x_ref, ghg_ref, gsb_ref, gpost_ref, w_ref, cat_ref, mix_ref, h1_ref):
        ohg = ohg_ref[...]
        hg = hg_ref[...]
        osb = osb_ref[...]
        a = ohg * _rstd(ohg) * ghg_ref[...] * (hg * _sigmoid(hg))
        n2 = osb * _rstd(osb) * gsb_ref[...]
        cat = jnp.concatenate([a, n2], axis=1).astype(BF16)
        cat_ref[...] = cat
        mix = _dot(cat, w_ref[...])
        mix_ref[...] = mix
        h1_ref[...] = x_ref[...] + mix * _rstd(mix) * gpost_ref[...]

    return pl.pallas_call(
        body, grid=(T // tm,), name="out_proj",
        in_specs=[_row_spec(tm, HG_WIDTH), _row_spec(tm, HG_WIDTH, 3), _row_spec(tm, SB_WIDTH), _row_spec(tm, D_MODEL),
                  _full_spec((1, HG_WIDTH)), _full_spec((1, SB_WIDTH)), _full_spec((1, D_MODEL)),
                  _full_spec((D_MODEL, D_MODEL))],
        out_specs=[_row_spec(tm, D_MODEL)] * 3,
        out_shape=[jax.ShapeDtypeStruct((T, D_MODEL), BF16), jax.ShapeDtypeStruct((T, D_MODEL), F32),
                   jax.ShapeDtypeStruct((T, D_MODEL), F32)],
        compiler_params=_params(("parallel",)),
    )(o_hg, proj, o_sb, x, g_hg, g_sb, g_post, w)


def _ffn_fwd(h1, g_pre, w_gu_t, w_down, g_post):
    T = h1.shape[0]
    tm = 2 * ROW_TILE
    n_f = D_FF // FFN_TF

    def body(h1_ref, gpre_ref, wgu_ref, wd_ref, gpost_ref, u2_ref, gu_ref, y_ref, h2_ref, acc_ref):
        j = pl.program_id(1)

        @pl.when(j == 0)
        def _():
            hv = h1_ref[...]
            u2_ref[...] = (hv * _rstd(hv) * gpre_ref[...]).astype(BF16)
            acc_ref[...] = jnp.zeros_like(acc_ref)

        u2 = u2_ref[...]
        gate = _dot_nt(u2, wgu_ref[0])
        up = _dot_nt(u2, wgu_ref[1])
        gu_ref[0] = gate.astype(BF16)
        gu_ref[1] = up.astype(BF16)
        a = (gate * _sigmoid(gate) * up).astype(BF16)
        acc_ref[...] += _dot(a, wd_ref[...])

        @pl.when(j == n_f - 1)
        def _():
            y = acc_ref[...]
            y_ref[...] = y
            h2_ref[...] = h1_ref[...] + y * _rstd(y) * gpost_ref[...]

    row = lambda width: pl.BlockSpec((tm, width), lambda i, j: (i, 0))
    vec = pl.BlockSpec((1, D_MODEL), lambda i, j: (0, 0))
    return pl.pallas_call(
        body, grid=(T // tm, n_f), name="ffn_fwd",
        in_specs=[row(D_MODEL), vec,
                  pl.BlockSpec((2, FFN_TF, D_MODEL), lambda i, j: (0, j, 0)),
                  pl.BlockSpec((FFN_TF, D_MODEL), lambda i, j: (j, 0)), vec],
        out_specs=[row(D_MODEL),
                   pl.BlockSpec((2, tm, FFN_TF), lambda i, j: (0, i, j)),
                   row(D_MODEL), row(D_MODEL)],
        out_shape=[jax.ShapeDtypeStruct((T, D_MODEL), BF16), jax.ShapeDtypeStruct((2, T, D_FF), BF16),
                   jax.ShapeDtypeStruct((T, D_MODEL), F32), jax.ShapeDtypeStruct((T, D_MODEL), F32)],
        scratch_shapes=[pltpu.VMEM((tm, D_MODEL), F32)],
        compiler_params=_params(("parallel", "arbitrary")),
    )(h1, g_pre, w_gu_t, w_down, g_post)


def _ple_loss(h2, p, target, w_pp_t, w_pg):
    T = h2.shape[0]
    tm = ROW_TILE

    def body(h2_ref, p_ref, t_ref, wpp_ref, wpg_ref, dh2_ref, de_ref, dg_ref, loss_ref):
        i = pl.program_id(0)
        h2v = h2_ref[...]
        e = _dot_nt(p_ref[...].astype(BF16), wpp_ref[...])
        sg = _sigmoid(_dot(h2v.astype(BF16), wpg_ref[...]))
        diff = h2v + e * sg - t_ref[...]
        part = jnp.sum(jnp.sum(diff * diff, axis=1, keepdims=True), axis=0, keepdims=True) * (0.5 / D_MODEL)

        @pl.when(i == 0)
        def _():
            loss_ref[...] = jnp.zeros_like(loss_ref)

        loss_ref[...] += jnp.broadcast_to(part, loss_ref.shape)
        dh3 = diff * (1.0 / D_MODEL)
        de_ref[...] = (dh3 * sg).astype(BF16)
        dg = (dh3 * e * sg * (1.0 - sg)).astype(BF16)
        dg_ref[...] = dg
        dh2_ref[...] = dh3 + _dot_nt(dg, wpg_ref[...])

    return pl.pallas_call(
        body, grid=(T // tm,), name="ple_loss",
        in_specs=[_row_spec(tm, D_MODEL), _row_spec(tm, PLE_DIM), _row_spec(tm, D_MODEL),
                  _full_spec((D_MODEL, PLE_DIM)), _full_spec((D_MODEL, D_MODEL))],
        out_specs=[_row_spec(tm, D_MODEL)] * 3 + [_full_spec((8, 128))],
        out_shape=[jax.ShapeDtypeStruct((T, D_MODEL), F32), jax.ShapeDtypeStruct((T, D_MODEL), BF16),
                   jax.ShapeDtypeStruct((T, D_MODEL), BF16), jax.ShapeDtypeStruct((8, 128), F32)],
        compiler_params=_params(("arbitrary",)),
    )(h2, p, target, w_pp_t, w_pg)


def _ffn_bwd(dh2, y, h1, gu, g_post, g_pre, w_gu_t, w_down):
    T = h1.shape[0]
    tm = ROW_TILE
    n_f = D_FF // FFN_TF

    def body(dh2_ref, y_ref, h1_ref, gu_ref, gpost_ref, gpre_ref, wgu_hbm, wd_hbm,
             dy_ref, a_ref, dgu_ref, dh1_ref, dgpost_ref, dgpre_ref, wgu_ref, wd_ref, sems):
        i = pl.program_id(0)

        @pl.when(i == 0)
        def _():
            fetch = [pltpu.make_async_copy(wgu_hbm, wgu_ref, sems.at[0]), pltpu.make_async_copy(wd_hbm, wd_ref, sems.at[1])]
            for cp in fetch:
                cp.start()
            dgpost_ref[...] = jnp.zeros_like(dgpost_ref)
            dgpre_ref[...] = jnp.zeros_like(dgpre_ref)
            for cp in fetch:
                cp.wait()

        yv = y_ref[...]
        dh2 = dh2_ref[...]
        dy, gterm = _rms_bwd(yv, _rstd(yv), gpost_ref[...], dh2)
        dgpost_ref[...] += _colsum(gterm)
        dy = dy.astype(BF16)
        dy_ref[...] = dy
        du2 = jnp.zeros((tm, D_MODEL), F32)
        for j in range(n_f):
            cols = slice(j * FFN_TF, (j + 1) * FFN_TF)
            da = _dot_nt(dy, wd_ref[cols, :])
            gate = gu_ref[0, :, cols].astype(F32)
            up = gu_ref[1, :, cols].astype(F32)
            s = _sigmoid(gate)
            sil = gate * s
            a_ref[:, cols] = (sil * up).astype(BF16)
            dgate = (da * up * (s * (1.0 + gate * (1.0 - s)))).astype(BF16)
            dup = (da * sil).astype(BF16)
            dgu_ref[0, :, cols] = dgate
            dgu_ref[1, :, cols] = dup
            du2 = du2 + _dot(dgate, wgu_ref[0, cols, :]) + _dot(dup, wgu_ref[1, cols, :])
        hv = h1_ref[...]
        dx, gterm = _rms_bwd(hv, _rstd(hv), gpre_ref[...], du2)
        dh1_ref[...] = dh2 + dx
        dgpre_ref[...] += _colsum(gterm)

    row = lambda width: pl.BlockSpec((tm, width), lambda i: (i, 0))
    both = pl.BlockSpec((2, tm, D_FF), lambda i: (0, i, 0))
    vec = pl.BlockSpec((1, D_MODEL), lambda i: (0, 0))
    hbm = pl.BlockSpec(memory_space=pl.ANY)
    return pl.pallas_call(
        body, grid=(T // tm,), name="ffn_bwd",
        in_specs=[row(D_MODEL), row(D_MODEL), row(D_MODEL), both, vec, vec, hbm, hbm],
        out_specs=[row(D_MODEL), row(D_FF), both, row(D_MODEL), vec, vec],
        out_shape=[jax.ShapeDtypeStruct((T, D_MODEL), BF16), jax.ShapeDtypeStruct((T, D_FF), BF16),
                   jax.ShapeDtypeStruct((2, T, D_FF), BF16),
                   jax.ShapeDtypeStruct((T, D_MODEL), F32), jax.ShapeDtypeStruct((1, D_MODEL), F32),
                   jax.ShapeDtypeStruct((1, D_MODEL), F32)],
        scratch_shapes=[pltpu.VMEM((2, D_FF, D_MODEL), BF16), pltpu.VMEM((D_FF, D_MODEL), BF16),
                        pltpu.SemaphoreType.DMA((2,))],
        compiler_params=_params(("arbitrary",)),
    )(dh2, y, h1, gu, g_post, g_pre, w_gu_t, w_down)


def _out_bwd(dh1, mix, o_hg, proj, o_sb, g_hg, g_sb, g_post, w, after):
    T = dh1.shape[0]
    tm = ROW_TILE

    def body(dh1_ref, mix_ref, ohg_ref, hg_ref, osb_ref, ghg_ref, gsb_ref, gpost_ref, w_ref, after_ref,
             dmix_ref, dohg_ref, dhg_ref, dosb_ref, dgpost_ref, dghg_ref, dgsb_ref):
        i = pl.program_id(0)

        @pl.when(i == 0)
        def _():
            dgpost_ref[...] = jnp.zeros_like(dgpost_ref)
            dghg_ref[...] = jnp.zeros_like(dghg_ref)
            dgsb_ref[...] = jnp.zeros_like(dgsb_ref)

        mix = mix_ref[...]
        dmix, gterm = _rms_bwd(mix, _rstd(mix), gpost_ref[...], dh1_ref[...])
        dgpost_ref[...] += _colsum(gterm)
        dmix = dmix.astype(BF16)
        dmix_ref[...] = dmix
        dcat = _dot_nt(dmix, w_ref[...])
        da = dcat[:, :HG_WIDTH]
        dn2 = dcat[:, HG_WIDTH:]
        ohg = ohg_ref[...]
        r1 = _rstd(ohg)
        hg = hg_ref[...]
        s = _sigmoid(hg)
        dhg_ref[...] = da * (ohg * r1 * ghg_ref[...]) * (s * (1.0 + hg * (1.0 - s)))
        dohg, gterm = _rms_bwd(ohg, r1, ghg_ref[...], da * (hg * s))
        dohg_ref[...] = dohg
        dghg_ref[...] += _colsum(gterm)
        osb = osb_ref[...]
        dosb, gterm = _rms_bwd(osb, _rstd(osb), gsb_ref[...], dn2)
        dosb_ref[...] = dosb
        dgsb_ref[...] += _colsum(gterm)

    return pl.pallas_call(
        body, grid=(T // tm,), name="out_bwd",
        in_specs=[_row_spec(tm, D_MODEL), _row_spec(tm, D_MODEL), _row_spec(tm, HG_WIDTH), _row_spec(tm, HG_WIDTH, 3),
                  _row_spec(tm, SB_WIDTH), _full_spec((1, HG_WIDTH)), _full_spec((1, SB_WIDTH)),
                  _full_spec((1, D_MODEL)), _full_spec((D_MODEL, D_MODEL)), pl.BlockSpec(memory_space=pl.ANY)],
        out_specs=[_row_spec(tm, D_MODEL), _row_spec(tm, HG_WIDTH), _row_spec(tm, HG_WIDTH), _row_spec(tm, SB_WIDTH),
                   _full_spec((1, D_MODEL)), _full_spec((1, HG_WIDTH)), _full_spec((1, SB_WIDTH))],
        out_shape=[jax.ShapeDtypeStruct((T, D_MODEL), BF16), jax.ShapeDtypeStruct((T, HG_WIDTH), F32),
                   jax.ShapeDtypeStruct((T, HG_WIDTH), F32), jax.ShapeDtypeStruct((T, SB_WIDTH), F32),
                   jax.ShapeDtypeStruct((1, D_MODEL), F32), jax.ShapeDtypeStruct((1, HG_WIDTH), F32),
                   jax.ShapeDtypeStruct((1, SB_WIDTH), F32)],
        compiler_params=_params(("arbitrary",)),
    )(dh1, mix, o_hg, proj, o_sb, g_hg, g_sb, g_post, w, after)


def _in_bwd(d_hgrn, d_hg, d_sq, d_sk, d_sv, x, dh1, g_pre, w_t):
    T = x.shape[0]
    tm = ROW_TILE

    def body(dh_ref, dhg_ref, dsq_ref, dsk_ref, dsv_ref, x_ref, dh1_ref, gpre_ref, w_ref, dproj_ref, dx_ref, dgpre_ref):
        i = pl.program_id(0)

        @pl.when(i == 0)
        def _():
            dgpre_ref[...] = jnp.zeros_like(dgpre_ref)

        dp = jnp.concatenate([dh_ref[...], dhg_ref[...], dsq_ref[...], dsk_ref[...], dsv_ref[...]], axis=1).astype(BF16)
        dproj_ref[...] = dp
        du = _dot(dp, w_ref[...])
        xv = x_ref[...]
        dx, gterm = _rms_bwd(xv, _rstd(xv), gpre_ref[...], du)
        dx_ref[...] = dh1_ref[...] + dx
        dgpre_ref[...] += _colsum(gterm)

    return pl.pallas_call(
        body, grid=(T // tm,), name="in_bwd",
        in_specs=[_row_spec(tm, 3 * HG_WIDTH), _row_spec(tm, HG_WIDTH), _row_spec(tm, SB_WIDTH), _row_spec(tm, SB_WIDTH),
                  _row_spec(tm, SB_WIDTH), _row_spec(tm, D_MODEL), _row_spec(tm, D_MODEL), _full_spec((1, D_MODEL)),
                  _full_spec((IN_COLS, D_MODEL))],
        out_specs=[_row_spec(tm, IN_COLS), _row_spec(tm, D_MODEL), _full_spec((1, D_MODEL))],
        out_shape=[jax.ShapeDtypeStruct((T, IN_COLS), BF16), jax.ShapeDtypeStruct((T, D_MODEL), F32),
                   jax.ShapeDtypeStruct((1, D_MODEL), F32)],
        compiler_params=_params(("arbitrary",)),
    )(d_hgrn, d_hg, d_sq, d_sk, d_sv, x, dh1, g_pre, w_t)


def _wgrad(a, b, name):
    stacked = a.ndim == 3
    S, T, K = a.shape if stacked else (1,) + a.shape
    N = b.shape[1]
    tk = K
    while tk * N * 4 > WGRAD_ACC_BYTES:
        tk //= 2
    assert K % tk == 0 and tk % 128 == 0
    tt = 1024 if T % 1024 == 0 else 512
    n_t = T // tt

    def body(a_ref, b_ref, o_ref, acc_ref):
        t = pl.program_id(2)

        @pl.when(t == 0)
        def _():
            acc_ref[...] = jnp.zeros_like(acc_ref)

        acc_ref[...] += _dot_tn(a_ref[...].astype(BF16), b_ref[...].astype(BF16))

        @pl.when(t == n_t - 1)
        def _():
            o_ref[...] = acc_ref[...].astype(BF16)

    if stacked:
        a_spec = pl.BlockSpec((None, tt, tk), lambda s, k, t: (s, t, k))
        o_spec = pl.BlockSpec((None, tk, N), lambda s, k, t: (s, k, 0))
        o_shape = (S, K, N)
    else:
        a_spec = pl.BlockSpec((tt, tk), lambda s, k, t: (t, k))
        o_spec = pl.BlockSpec((tk, N), lambda s, k, t: (k, 0))
        o_shape = (K, N)
    return pl.pallas_call(
        body, grid=(S, K // tk, n_t), name=name,
        in_specs=[a_spec, pl.BlockSpec((tt, N), lambda s, k, t: (t, 0))],
        out_specs=o_spec,
        out_shape=jax.ShapeDtypeStruct(o_shape, BF16),
        scratch_shapes=[pltpu.VMEM((tk, N), F32)],
        compiler_params=_params(("parallel", "parallel", "arbitrary")),
    )(a, b)


def _hg_gates(hq, hf, gam):
    g0 = gam[0:1, :]
    g1 = gam[1:2, :]
    mx = jnp.maximum(g0, g1)
    e0 = jnp.exp(g0 - mx)
    e1 = jnp.exp(g1 - mx)
    lb = e0 / (e0 + e1)
    s = _sigmoid(hf)
    f = lb + (1.0 - lb) * s
    sq = _sigmoid(hq)
    return hq * sq, sq, s, f, (1.0 - lb) * (1.0 - s), jnp.log(f), lb


def _prefix_in_sub(a, rowmod):
    n = a.shape[0]
    sh = 1
    while sh < HG_SUB:
        a = a + jnp.where(rowmod >= sh, pltpu.roll(a, sh, axis=0), 0.0)
        sh *= 2
    return a


def _suffix_in_sub(a, rowmod):
    n = a.shape[0]
    sh = 1
    while sh < HG_SUB:
        a = a + jnp.where(rowmod < HG_SUB - sh, pltpu.roll(a, n - sh, axis=0), 0.0)
        sh *= 2
    return a


def _hg_fwd(proj, gam):
    T = proj.shape[0]
    tb = HG_TILE
    n_sub = tb // HG_SUB

    def body(hq_ref, hf_ref, hi_ref, gam_ref, o_ref, hist_ref, st_ref, qt_ref, kk_ref, b_ref, od_ref):
        i = pl.program_id(0)

        @pl.when(i == 0)
        def _():
            st_ref[...] = jnp.zeros_like(st_ref)

        rowmod_w = lax.broadcasted_iota(jnp.int32, (tb, HG_WIDTH), 0) % HG_SUB
        rowmod = lax.broadcasted_iota(jnp.int32, (tb, 1), 0) % HG_SUB
        q, _, _, f, kk, lf, _ = _hg_gates(hq_ref[...], hf_ref[...], gam_ref[...])
        b = _prefix_in_sub(lf, rowmod_w)
        qt_ref[...] = (q * jnp.exp(b)).astype(BF16)
        kk_ref[...] = kk
        b_ref[...] = b

        for h in range(HG_HEADS):
            cs = slice(h * HG_DK, (h + 1) * HG_DK)
            qh = q[:, cs]
            kh = kk[:, cs]
            fh = f[:, cs]
            vh = hi_ref[:, cs]
            acc = jnp.sum(qh * kh, axis=1, keepdims=True) * vh
            decay = fh
            for d in range(1, HG_SUB):
                if d > 1:
                    decay = decay * pltpu.roll(fh, d - 1, axis=0)
                e = qh * pltpu.roll(kh, d, axis=0) * decay
                sc = jnp.where(rowmod >= d, jnp.sum(e, axis=1, keepdims=True), 0.0)
                acc = acc + sc * pltpu.roll(vh, d, axis=0)
            od_ref[:, cs] = acc

        for m in range(n_sub):
            rs = slice(m * HG_SUB, (m + 1) * HG_SUB)
            for h in range(HG_HEADS):
                cs = slice(h * HG_DK, (h + 1) * HG_DK)
                st = st_ref[h]
                hist_ref[m, h] = st
                o_int = _dot_nt(qt_ref[rs, cs], st.astype(BF16))
                bs = b_ref[rs, cs]
                bl = bs[HG_SUB - 1:HG_SUB, :]
                kt = (kk_ref[rs, cs] * jnp.exp(bl - bs)).astype(BF16)
                st_ref[h] = st * jnp.exp(bl) + _dot_tn(hi_ref[rs, cs].astype(BF16), kt)
                o_ref[rs, cs] = o_int + od_ref[rs, cs]

    col = lambda c: pl.BlockSpec((tb, HG_WIDTH), lambda i, c=c: (i, c))
    return pl.pallas_call(
        body, grid=(T // tb,), name="hg_fwd",
        in_specs=[col(0), col(1), col(2), _full_spec((2, HG_WIDTH))],
        out_specs=[pl.BlockSpec((tb, HG_WIDTH), lambda i: (i, 0)),
                   pl.BlockSpec((n_sub, HG_HEADS, HG_DK, HG_DK), lambda i: (i, 0, 0, 0))],
        out_shape=[jax.ShapeDtypeStruct((T, HG_WIDTH), F32),
                   jax.ShapeDtypeStruct((T // HG_SUB, HG_HEADS, HG_DK, HG_DK), F32)],
        scratch_shapes=[pltpu.VMEM((HG_HEADS, HG_DK, HG_DK), F32), pltpu.VMEM((tb, HG_WIDTH), BF16),
                        pltpu.VMEM((tb, HG_WIDTH), F32), pltpu.VMEM((tb, HG_WIDTH), F32),
                        pltpu.VMEM((tb, HG_WIDTH), F32)],
        compiler_params=_params(("arbitrary",)),
    )(proj, proj, proj, gam)


def _hg_bwd(proj, gam, d_o, hist):
    T = proj.shape[0]
    tb = HG_TILE
    n_sub = tb // HG_SUB
    n_t = T // tb

    def body(hq_ref, hf_ref, hi_ref, gam_ref, do_ref, hist_ref, dout_ref, dlb_ref,
             dst_ref, q_ref, kk_ref, b_ref, dq_ref, dkk_ref, db_ref, dv_ref):
        i = pl.program_id(0)

        @pl.when(i == 0)
        def _():
            dst_ref[...] = jnp.zeros_like(dst_ref)
            dlb_ref[...] = jnp.zeros_like(dlb_ref)

        rowmod_w = lax.broadcasted_iota(jnp.int32, (tb, HG_WIDTH), 0) % HG_SUB
        rowmod = lax.broadcasted_iota(jnp.int32, (tb, 1), 0) % HG_SUB
        last_row = lax.broadcasted_iota(jnp.int32, (HG_SUB, 1), 0) == HG_SUB - 1
        hq = hq_ref[...]
        q, sq, s, f, kk, lf, lb = _hg_gates(hq, hf_ref[...], gam_ref[...])
        b = _prefix_in_sub(lf, rowmod_w)
        q_ref[...] = q
        kk_ref[...] = kk
        b_ref[...] = b

        for m in reversed(range(n_sub)):
            rs = slice(m * HG_SUB, (m + 1) * HG_SUB)
            for h in range(HG_HEADS):
                cs = slice(h * HG_DK, (h + 1) * HG_DK)
                st = hist_ref[m, h]
                dst = dst_ref[h]
                dstb = dst.astype(BF16)
                dos = do_ref[rs, cs].astype(BF16)
                vs = hi_ref[rs, cs].astype(BF16)
                bs = b_ref[rs, cs]
                bl = bs[HG_SUB - 1:HG_SUB, :]
                ebl = jnp.exp(bl)
                qt = q_ref[rs, cs] * jnp.exp(bs)
                decay = jnp.exp(bl - bs)
                kt = kk_ref[rs, cs] * decay
                dqt = _dot(dos, st.astype(BF16))
                dkt = _dot(vs, dstb)
                dv_ref[rs, cs] = _dot_nt(kt.astype(BF16), dstb)
                dst_ref[h] = dst * ebl + _dot_tn(dos, qt.astype(BF16))
                dq_ref[rs, cs] = dqt * jnp.exp(bs)
                dkk_ref[rs, cs] = dkt * decay
                dktk = dkt * kt
                dbl = _colsum(dktk) + _colsum(dst * st) * ebl
                db_ref[rs, cs] = dqt * qt - dktk + jnp.where(last_row, dbl, 0.0)

        for h in range(HG_HEADS):
            cs = slice(h * HG_DK, (h + 1) * HG_DK)
            qh = q[:, cs]
            kh = kk[:, cs]
            fh = f[:, cs]
            vh = hi_ref[:, cs]
            doh = do_ref[:, cs]
            dsc = jnp.sum(doh * vh, axis=1, keepdims=True)
            sc = jnp.sum(qh * kh, axis=1, keepdims=True)
            dqh = dsc * kh
            dkh = dsc * qh
            dvh = sc * doh
            dbh = jnp.zeros_like(qh)
            decay = fh
            for d in range(1, HG_SUB):
                back = tb - d
                valid = rowmod >= d
                if d > 1:
                    decay = decay * pltpu.roll(fh, d - 1, axis=0)
                kd = pltpu.roll(kh, d, axis=0)
                ex = jnp.where(valid, decay, 0.0)
                pd = kd * ex
                sc = jnp.sum(qh * pd, axis=1, keepdims=True)
                dsc = jnp.where(valid, jnp.sum(doh * pltpu.roll(vh, d, axis=0), axis=1, keepdims=True), 0.0)
                dqh = dqh + dsc * pd
                dsq = dsc * qh
                w = dsq * pd
                dkh = dkh + pltpu.roll(dsq * ex, back, axis=0)
                dvh = dvh + pltpu.roll(sc * doh, back, axis=0)
                dbh = dbh + w - pltpu.roll(w, back, axis=0)
            dq_ref[:, cs] += dqh
            dkk_ref[:, cs] += dkh
            dv_ref[:, cs] += dvh
            db_ref[:, cs] += dbh

        dlf = _suffix_in_sub(db_ref[...], rowmod_w)
        df = dlf / f - dkk_ref[...]
        dout_ref[:, 0:HG_WIDTH] = dq_ref[...] * (sq * (1.0 + hq * (1.0 - sq)))
        dout_ref[:, HG_WIDTH:2 * HG_WIDTH] = df * (1.0 - lb) * (s * (1.0 - s))
        dout_ref[:, 2 * HG_WIDTH:3 * HG_WIDTH] = dv_ref[...]
        dlb_ref[...] += _colsum(df * (1.0 - s))

    col = lambda c: pl.BlockSpec((tb, HG_WIDTH), lambda i, c=c: (n_t - 1 - i, c))
    return pl.pallas_call(
        body, grid=(n_t,), name="hg_bwd",
        in_specs=[col(0), col(1), col(2), _full_spec((2, HG_WIDTH)),
                  pl.BlockSpec((tb, HG_WIDTH), lambda i: (n_t - 1 - i, 0)),
                  pl.BlockSpec((n_sub, HG_HEADS, HG_DK, HG_DK), lambda i: (n_t - 1 - i, 0, 0, 0))],
        out_specs=[pl.BlockSpec((tb, 3 * HG_WIDTH), lambda i: (n_t - 1 - i, 0)), _full_spec((1, HG_WIDTH))],
        out_shape=[jax.ShapeDtypeStruct((T, 3 * HG_WIDTH), F32), jax.ShapeDtypeStruct((1, HG_WIDTH), F32)],
        scratch_shapes=[pltpu.VMEM((HG_HEADS, HG_DK, HG_DK), F32)] + [pltpu.VMEM((tb, HG_WIDTH), F32)] * 7,
        compiler_params=_params(("arbitrary",)),
    )(proj, proj, proj, gam, d_o, hist)


def _sb_masks():
    row = lax.broadcasted_iota(jnp.int32, (SB_TK, SB_TK), 0)
    col = lax.broadcasted_iota(jnp.int32, (SB_TK, SB_TK), 1)
    suffix = (row >= col).astype(BF16)
    prefix = (row <= col).astype(BF16)
    query = lax.broadcasted_iota(jnp.int32, (SB_TQ, SB_TK), 0)
    key = lax.broadcasted_iota(jnp.int32, (SB_TQ, SB_TK), 1)
    causal = [key + r * SB_TK < query for r in range(SB_TQ // SB_TK)]
    lane = lax.broadcasted_iota(jnp.int32, (SB_TQ, SB_BLOCK), 1)
    return suffix, prefix, causal, lane


def _sum_right(a, suffix):
    return _dot(a.astype(BF16), suffix)


def _per_lane_tile(t, c, op):
    return jnp.concatenate([op(t[:, n:n + SB_BLOCK], c) for n in range(0, t.shape[1], SB_BLOCK)], axis=1)


def _sb_block(qb, kj, suffix, causal, c):
    z = _dot_nt(qb, kj)
    sp = jnp.maximum(z, 0.0) + jnp.log(1.0 + jnp.exp(-jnp.abs(z)))
    if causal is not None:
        sp = jnp.where(causal, sp, 0.0)
    big_l = _sum_right(sp, suffix)
    a = jnp.exp(_per_lane_tile(z - big_l, c, jnp.subtract))
    if causal is not None:
        a = jnp.where(causal, a, 0.0)
    return z, a, big_l


def _sb_tile_copies(to_hbm, bufs, stores, sems, slot, head, i, j, n_chain):
    copies = []
    for ch in range(n_chain):
        for w in range(2):
            vmem, hbm = bufs[w].at[slot, ch], stores[w].at[head + ch, i, j]
            src, dst = (vmem, hbm) if to_hbm else (hbm, vmem)
            copies.append(pltpu.make_async_copy(src, dst, sems.at[slot, ch, w]))
    return copies


def _half_masked(pair, lane, scale=1.0):
    pair = pair.astype(F32) * scale
    return jnp.where(lane < SB_DH, pair, 0.0).astype(BF16), jnp.where(lane < SB_DH, 0.0, pair).astype(BF16)


def _sb_fwd(qkv):
    T = qkv.shape[0]
    width = SB_PP * SB_BLOCK
    n_chain = 2 * SB_PP
    n_sub = SB_TQ // SB_TK

    def body(q_ref, k_ref, v_ref, o_ref, a_st, s_st, acc_ref, c_ref, qm, a_buf, s_buf, sems):
        g = pl.program_id(0)
        i = pl.program_id(1)
        suffix, _, causal, lane = _sb_masks()
        for pp in range(SB_PP):
            qm[2 * pp], qm[2 * pp + 1] = _half_masked(q_ref[:, pp * SB_BLOCK:(pp + 1) * SB_BLOCK], lane, SB_DH ** -0.5)
        acc_ref[...] = jnp.zeros_like(acc_ref)
        c_ref[...] = jnp.zeros_like(c_ref)
        a_buf[...] = jnp.zeros_like(a_buf)
        s_buf[...] = jnp.zeros_like(s_buf)

        def copies(slot, j):
            return _sb_tile_copies(True, (a_buf, s_buf), (a_st, s_st), sems, slot, g * n_chain, i, j, n_chain)

        def step(j, slot, r=None, reuse=True):
            rows = pl.ds(pl.multiple_of(j * SB_TK, SB_TK), SB_TK)
            qs = slice(0 if r is None else r * SB_TK, SB_TQ)
            mask = None if r is None else causal[r][qs]
            if reuse:
                for cp in copies(slot, j):
                    cp.wait()
            for ch in range(n_chain):
                pair = slice((ch // 2) * SB_BLOCK, (ch // 2 + 1) * SB_BLOCK)
                c = c_ref[ch, qs]
                z, a, big_l = _sb_block(qm[ch, qs], k_ref[rows, pair], suffix, mask, c)
                ab = a.astype(BF16)
                acc_ref[ch, qs] += _dot(ab, v_ref[rows, pair])
                a_buf[slot, ch, qs] = ab
                s_buf[slot, ch, qs] = z.astype(BF16)
                c_ref[ch, qs] = c + jnp.broadcast_to(big_l[:, 0:1], c.shape)
            for cp in copies(slot, j):
                cp.start()

        first = i * n_sub
        for s, r in enumerate(reversed(range(n_sub))):
            step(first + r, s % 2, r, reuse=s >= 2)

        @pl.loop(0, first)
        def _(t):
            step(first - 1 - t, (n_sub + t) % 2)

        for slot in range(2):
            for cp in copies(slot, 0):
                cp.wait()
        for pp in range(SB_PP):
            o_ref[:, pp * SB_BLOCK:(pp + 1) * SB_BLOCK] = jnp.where(lane < SB_DH, acc_ref[2 * pp], acc_ref[2 * pp + 1])

    assert n_sub >= 2 and n_sub % 2 == 0
    n_g = SB_WIDTH // width
    blk = lambda part: pl.BlockSpec((SB_TQ, width), lambda g, i, part=part: (i, part * n_g + g))
    whole = lambda part: pl.BlockSpec((T, width), lambda g, i, part=part: (0, part * n_g + g))
    tiles = jax.ShapeDtypeStruct((SB_HEADS, T // SB_TQ, T // SB_TK, SB_TQ, SB_TK), BF16)
    return pl.pallas_call(
        body, grid=(n_g, T // SB_TQ), name="sb_fwd",
        in_specs=[blk(0), whole(1), whole(2)],
        out_specs=[blk(0), pl.BlockSpec(memory_space=pl.ANY), pl.BlockSpec(memory_space=pl.ANY)],
        out_shape=[jax.ShapeDtypeStruct((T, SB_WIDTH), F32), tiles, tiles],
        scratch_shapes=[pltpu.VMEM((n_chain, SB_TQ, SB_BLOCK), F32), pltpu.VMEM((n_chain, SB_TQ, SB_BLOCK), F32),
                        pltpu.VMEM((n_chain, SB_TQ, SB_BLOCK), BF16),
                        pltpu.VMEM((2, n_chain, SB_TQ, SB_TK), BF16), pltpu.VMEM((2, n_chain, SB_TQ, SB_TK), BF16),
                        pltpu.SemaphoreType.DMA((2, n_chain, 2))],
        compiler_params=_params(("parallel", "arbitrary")),
    )(qkv, qkv, qkv)


def _sb_bwd(qkv, a_st, s_st, d_o, after):
    T = qkv.shape[0]
    width = SB_PP * SB_BLOCK
    n_chain = 2 * SB_PP
    n_sub = SB_TQ // SB_TK
    scale = SB_DH ** -0.5

    def body(q_ref, k_ref, v_ref, a_st_ref, s_st_ref, do_ref, after_ref, dq_ref, dk_ref, dv_ref, acc_ref, gc_ref, qm, dom,
             a_buf, s_buf, sems):
        g_idx = pl.program_id(0)
        i = pl.program_id(1)

        @pl.when(i == 0)
        def _():
            dk_ref[...] = jnp.zeros_like(dk_ref)
            dv_ref[...] = jnp.zeros_like(dv_ref)

        _, prefix, causal, lane = _sb_masks()
        for pp in range(SB_PP):
            pair = slice(pp * SB_BLOCK, (pp + 1) * SB_BLOCK)
            qm[2 * pp], qm[2 * pp + 1] = _half_masked(q_ref[:, pair], lane, scale)
            dom[2 * pp], dom[2 * pp + 1] = _half_masked(do_ref[:, pair], lane)
        acc_ref[...] = jnp.zeros_like(acc_ref)
        gc_ref[...] = jnp.zeros_like(gc_ref)

        def copies(slot, j):
            return _sb_tile_copies(False, (a_buf, s_buf), (a_st_ref, s_st_ref), sems, slot, g_idx * n_chain, i, j, n_chain)

        def step(j, slot, r=None, last=False):
            rows = pl.ds(pl.multiple_of(j * SB_TK, SB_TK), SB_TK)
            qs = slice(0 if r is None else r * SB_TK, SB_TQ)
            mask = None if r is None else causal[r][qs]
            for cp in copies(slot, j):
                cp.wait()
            if not last:
                for cp in copies(1 - slot, j + 1):
                    cp.start()
            for pp in range(SB_PP):
                pair = slice(pp * SB_BLOCK, (pp + 1) * SB_BLOCK)
                kj = k_ref[rows, pair]
                vj = v_ref[rows, pair]
                dk = jnp.zeros((SB_TK, SB_BLOCK), F32)
                dv = jnp.zeros((SB_TK, SB_BLOCK), F32)
                for ch in (2 * pp, 2 * pp + 1):
                    ab = a_buf[slot, ch, qs]
                    g = ab.astype(F32) * _dot_nt(dom[ch, qs], vj)
                    g_left = _sum_right(g, prefix)
                    gc = gc_ref[ch, qs]
                    dz = g - _sigmoid(s_buf[slot, ch, qs].astype(F32)) * _per_lane_tile(g_left, gc, jnp.add)
                    if mask is not None:
                        dz = jnp.where(mask, dz, 0.0)
                    dzb = dz.astype(BF16)
                    dk = dk + _dot_tn(dzb, qm[ch, qs])
                    dv = dv + _dot_tn(ab, dom[ch, qs])
                    acc_ref[ch, qs] += _dot(dzb, kj)
                    gc_ref[ch, qs] = gc + jnp.broadcast_to(g_left[:, SB_TK - 1:SB_TK], gc.shape)
                dk_ref[rows, pair] += dk
                dv_ref[rows, pair] += dv

        first = i * n_sub
        for cp in copies(0, 0):
            cp.start()

        @pl.loop(0, first)
        def _(j):
            step(j, j % 2)

        for r in range(n_sub):
            step(first + r, r % 2, r, last=r == n_sub - 1)
        for pp in range(SB_PP):
            dq_ref[:, pp * SB_BLOCK:(pp + 1) * SB_BLOCK] = scale * jnp.where(lane < SB_DH, acc_ref[2 * pp],
                                                                             acc_ref[2 * pp + 1])
    n_g = SB_WIDTH // width
    blk = lambda part: pl.BlockSpec((SB_TQ, width), lambda g, i, part=part: (i, part * n_g + g))
    whole = lambda part: pl.BlockSpec((T, width), lambda g, i, part=part: (0, part * n_g + g))
    return pl.pallas_call(
        body, grid=(n_g, T // SB_TQ), name="sb_bwd",
        in_specs=[blk(0), whole(1), whole(2), pl.BlockSpec(memory_space=pl.ANY), pl.BlockSpec(memory_space=pl.ANY), blk(0),
                  pl.BlockSpec(memory_space=pl.ANY)],
        out_specs=[blk(0), whole(0), whole(0)],
        out_shape=[jax.ShapeDtypeStruct((T, SB_WIDTH), F32)] * 3,
        scratch_shapes=[pltpu.VMEM((n_chain, SB_TQ, SB_BLOCK), F32), pltpu.VMEM((n_chain, SB_TQ, SB_BLOCK), F32),
                        pltpu.VMEM((n_chain, SB_TQ, SB_BLOCK), BF16), pltpu.VMEM((n_chain, SB_TQ, SB_BLOCK), BF16),
                        pltpu.VMEM((2, n_chain, SB_TQ, SB_TK), BF16), pltpu.VMEM((2, n_chain, SB_TQ, SB_TK), BF16),
                        pltpu.SemaphoreType.DMA((2, n_chain, 2))],
        compiler_params=_params(("parallel", "arbitrary")),
    )(qkv, qkv, qkv, a_st, s_st, d_o, after)


HBM = pl.BlockSpec(memory_space=pltpu.HBM)
MESH = pl.DeviceIdType.MESH


def _place():
    return lax.axis_index("x"), lax.axis_index("y"), lax.axis_index("c")


def _all_gather(block):
    rows, cols = block.shape

    def body(x_ref, out_ref, send_sems, recv_sems, local_sem):
        x, y, c = _place()
        me, sibling = (x, y, c), (x, y, 1 - c)
        chips = [(1 - x, y), (x, 1 - y), (1 - x, 1 - y)]

        def slot(px, py, pc):
            return out_ref.at[4 * px + 2 * py + pc]

        def copy(k, blk, to, src=None):
            return pltpu.make_async_remote_copy(
                src_ref=slot(*blk) if src is None else src, dst_ref=slot(*blk),
                send_sem=send_sems.at[k], recv_sem=recv_sems.at[k], device_id=to, device_id_type=MESH)

        mine = pltpu.make_async_copy(x_ref, slot(*me), local_sem)
        mine.start()
        first = [copy(0, me, sibling, src=x_ref)]
        first += [copy(1 + j, me, (*chip, c), src=x_ref) for j, chip in enumerate(chips)]
        for cp in first:
            cp.start()
        passed = [copy(4 + j, (*chip, c), sibling) for j, chip in enumerate(chips)]
        for j, chip in enumerate(chips):
            copy(1 + j, (*chip, c), me).wait_recv()
            passed[j].start()
        copy(0, sibling, me).wait_recv()
        for j, chip in enumerate(chips):
            copy(4 + j, (*chip, 1 - c), me).wait_recv()
        for cp in first + passed:
            cp.wait_send()
        mine.wait()

    return pl.pallas_call(
        body, name="all_gather",
        out_shape=jax.ShapeDtypeStruct((N_DEV, rows, cols), block.dtype),
        in_specs=[HBM], out_specs=HBM,
        scratch_shapes=[pltpu.SemaphoreType.DMA((7,)), pltpu.SemaphoreType.DMA((7,)), pltpu.SemaphoreType.DMA],
    )(block)


SEM = pl.BlockSpec(memory_space=pltpu.SEMAPHORE)
ANY = pl.BlockSpec(memory_space=pl.ANY)
SPLIT_EFFECT = pltpu.SideEffectType.DATAFLOW_SIDE_EFFECTING
N_PEER = N_DEV - 1


def _flow_copies(kind, src_ref, land_ref, send_sems, recv_sems):
    x, y, c = _place()
    copies = []
    for r in range(1, N_DEV):
        px = 1 - x if r & 4 else x
        py = 1 - y if r & 2 else y
        pc = 1 - c if r & 1 else c
        if kind == "gather":
            src, dst = src_ref, land_ref.at[4 * x + 2 * y + c]
        else:
            src, dst = src_ref.at[4 * px + 2 * py + pc], land_ref.at[r - 1]
        copies.append(pltpu.make_async_remote_copy(
            src_ref=src, dst_ref=dst, send_sem=send_sems.at[r - 1], recv_sem=recv_sems.at[r - 1],
            device_id=(px, py, pc), device_id_type=MESH))
    return copies


def _landing(kind, src):
    if kind == "gather":
        return lax.empty((N_DEV,) + src.shape, src.dtype)
    return lax.empty((N_PEER,) + src.shape[1:], src.dtype)


def _push_start(kinds, arrays, after, name):
    n = len(arrays)

    def body(*refs):
        ins, sems, token = refs[:n], refs[n + 1:2 * n + 1], refs[3 * n + 1]
        for f, kind in enumerate(kinds):
            for cp in _flow_copies(kind, ins[2 * f], ins[2 * f + 1], sems[2 * f], sems[2 * f + 1]):
                cp.start()
        token[...] = jnp.zeros_like(token)

    outs = pl.pallas_call(
        body, name=name,
        out_shape=[pltpu.SemaphoreType.DMA((N_PEER,))] * n + [pltpu.HBM(a.shape, a.dtype) for a in arrays]
        + [jax.ShapeDtypeStruct((8, 128), F32)],
        in_specs=[HBM] * n + [ANY], out_specs=[SEM] * n + [HBM] * n + [pl.BlockSpec(memory_space=pltpu.VMEM)],
        input_output_aliases={i: n + i for i in range(n)},
        compiler_params=pltpu.CompilerParams(has_side_effects=SPLIT_EFFECT),
    )(*[pltpu.with_memory_space_constraint(a, pltpu.HBM) for a in arrays], after)
    return outs[:n], outs[n:2 * n], outs[2 * n]


def _push_wait(kinds, sems, arrays, after, name):
    n = len(arrays)

    def body(*refs):
        ins, sem_refs = refs[:n], refs[n:2 * n]
        for f, kind in enumerate(kinds):
            for cp in _flow_copies(kind, ins[2 * f], ins[2 * f + 1], sem_refs[2 * f], sem_refs[2 * f + 1]):
                cp.wait_send()
                cp.wait_recv()

    outs = pl.pallas_call(
        body, name=name,
        out_shape=[pltpu.HBM(a.shape, a.dtype) for a in arrays],
        in_specs=[HBM] * n + [SEM] * n + [ANY], out_specs=[HBM] * n,
        input_output_aliases={i: i for i in range(n)},
        compiler_params=pltpu.CompilerParams(has_side_effects=SPLIT_EFFECT),
    )(*arrays, *sems, after)
    return outs


def _sum_blocks(own, got, name):
    n, rows, cols = got.shape
    tr = next(t for t in (128, 64, 32) if rows % t == 0)

    def body(own_ref, got_ref, out_ref):
        acc = own_ref[...].astype(F32)
        for r in range(n):
            acc = acc + got_ref[r].astype(F32)
        out_ref[...] = acc

    return pl.pallas_call(
        body, grid=(rows // tr,), name=name,
        in_specs=[pl.BlockSpec((tr, cols), lambda i: (i, 0)), pl.BlockSpec((n, tr, cols), lambda i: (0, i, 0))],
        out_specs=pl.BlockSpec((tr, cols), lambda i: (i, 0)),
        out_shape=jax.ShapeDtypeStruct((rows, cols), F32),
        compiler_params=_params(("parallel",)),
    )(own, got)


def _adamw_math(w, g, m, v):
    m = ADAM_B1 * m + (1.0 - ADAM_B1) * g
    v = ADAM_B2 * v + (1.0 - ADAM_B2) * (g * g)
    m_hat = m / (1.0 - ADAM_B1 ** ADAM_STEP)
    v_hat = v / (1.0 - ADAM_B2 ** ADAM_STEP)
    delta = -ADAM_LR * (m_hat / (jnp.sqrt(v_hat) + ADAM_EPS) + ADAM_WD * w)
    return delta, m, v


def _adamw(w, g, m, v, name):
    rows, cols = w.shape
    tr = rows if rows <= 352 else 256

    def body(w_ref, g_ref, m_ref, v_ref, d_out, m_out, v_out):
        d_out[...], m_out[...], v_out[...] = _adamw_math(w_ref[...], g_ref[...], m_ref[...], v_ref[...])

    spec = pl.BlockSpec((tr, cols), lambda i: (i, 0))
    return pl.pallas_call(
        body, grid=(rows // tr,), name=name,
        in_specs=[spec] * 4, out_specs=[spec] * 3,
        out_shape=[jax.ShapeDtypeStruct((rows, cols), F32)] * 3,
        compiler_params=_params(("parallel",)),
    )(w, g, m, v)


ROW_ATTN_PRE, ROW_ATTN_POST, ROW_FFN_PRE, ROW_FFN_POST, ROW_OUT_NORMS, ROW_GAMMA, ROW_LOSS = range(7)


def _small_update(smalls, w, m, v):
    def body(s_ref, w_ref, m_ref, v_ref, g_out, d_out, m_out, v_out):
        g = s_ref[0]
        for j in range(1, N_DEV):
            g = g + s_ref[j]
        wv = w_ref[...]
        g0 = wv[ROW_GAMMA:ROW_GAMMA + 1, 0:HG_WIDTH]
        g1 = wv[ROW_GAMMA:ROW_GAMMA + 1, HG_WIDTH:]
        mx = jnp.maximum(g0, g1)
        e0 = jnp.exp(g0 - mx)
        e1 = jnp.exp(g1 - mx)
        lb = e0 / (e0 + e1)
        dg0 = g[ROW_GAMMA:ROW_GAMMA + 1, 0:HG_WIDTH] * lb * (1.0 - lb)
        row = lax.broadcasted_iota(jnp.int32, g.shape, 0)
        g = jnp.where(row == ROW_GAMMA, jnp.concatenate([dg0, -dg0], axis=1), g)
        g_out[...] = g
        d_out[...], m_out[...], v_out[...] = _adamw_math(wv, g, m_ref[...], v_ref[...])

    return pl.pallas_call(
        body, name="small_update",
        out_shape=[jax.ShapeDtypeStruct((8, D_MODEL), F32)] * 4,
    )(smalls, w, m, v)


def _local_step(x, p, target, g_attn_pre, w_in_t, gamma, g_hg, g_sb, g_attn_post, g_ffn_pre, g_ffn_post, later_weights,
                grads_ready):
    proj, u, qkv = _in_proj(x, g_attn_pre, w_in_t)
    o_hg, hist = _hg_fwd(proj, gamma)
    o_sb, a_st, s_st = _sb_fwd(qkv)
    w_out, w_gu_t, w_down, w_pp_t, w_pg = later_weights(o_sb)
    cat, mix, h1 = _out_proj(o_hg, proj, o_sb, x, g_hg, g_sb, g_attn_post, w_out)
    u2, gu, y, h2 = _ffn_fwd(h1, g_ffn_pre, w_gu_t, w_down, g_ffn_post)
    dh2, de, dg, loss = _ple_loss(h2, p, target, w_pp_t, w_pg)

    d_wpp_t = _wgrad(de, p, "wgrad_ple_proj")
    d_wpg = _wgrad(h2, dg, "wgrad_ple_gate")
    dy, a, dgu, dh1, dg_ffn_post, dg_ffn_pre = _ffn_bwd(dh2, y, h1, gu, g_ffn_post, g_ffn_pre, w_gu_t, w_down)
    d_wdown = _wgrad(a, dy, "wgrad_down")
    sent = grads_ready("ffn", (_wgrad(dgu, u2, "wgrad_gate_up"), d_wdown, d_wpp_t, d_wpg))
    dmix, d_ohg, d_hg, d_osb, dg_attn_post, dg_hg, dg_sb = _out_bwd(dh1, mix, o_hg, proj, o_sb, g_hg, g_sb, g_attn_post,
                                                                    w_out, sent)
    sent = grads_ready("out", (_wgrad(cat, dmix, "wgrad_out"),))
    dq, dk, dv = _sb_bwd(qkv, a_st, s_st, d_osb, sent)
    d_hgrn, d_lb = _hg_bwd(proj, gamma, d_ohg, hist)
    dproj, dx, dg_attn_pre = _in_bwd(d_hgrn, d_hg, dq, dk, dv, x, dh1, g_attn_pre, w_in_t)
    grads_ready("in", (_wgrad(dproj, u, "wgrad_in"),))
    return loss, dx, (dg_attn_pre, dg_attn_post, dg_ffn_pre, dg_ffn_post, dg_hg, dg_sb, d_lb)


def _pack_row(*parts):
    return jnp.concatenate([q.reshape(1, -1) for q in parts], axis=1)


def kernel(x, p, attn_pre_norm, w_in, hg_lower_gamma, hg_out_norm, sb_out_norm, w_out, attn_post_norm, ffn_pre_norm, w_gate_up, w_down, ffn_post_norm, ple_proj, ple_gate, loss_target, m_attn_pre_norm, m_w_in, m_hg_lower_gamma, m_hg_out_norm, m_sb_out_norm, m_w_out, m_attn_post_norm, m_ffn_pre_norm, m_w_gate_up, m_w_down, m_ffn_post_norm, m_ple_proj, m_ple_gate, v_attn_pre_norm, v_w_in, v_hg_lower_gamma, v_hg_out_norm, v_sb_out_norm, v_w_out, v_attn_post_norm, v_ffn_pre_norm, v_w_gate_up, v_w_down, v_ffn_post_norm, v_ple_proj, v_ple_gate):
    big = (w_in[0], w_out[0], w_gate_up[0], w_down[0], ple_proj[0], ple_gate[0])
    big_m = (m_w_in[0], m_w_out[0], m_w_gate_up[0], m_w_down[0], m_ple_proj[0], m_ple_gate[0])
    big_v = (v_w_in[0], v_w_out[0], v_w_gate_up[0], v_w_down[0], v_ple_proj[0], v_ple_gate[0])
    names = ("w_in", "w_out", "w_gate_up", "w_down", "ple_proj", "ple_gate")

    by_columns = (True, False, True, False, True, False)
    shards = [(w.T if t else w).astype(BF16) for w, t in zip(big, by_columns)]
    w_in_t = _all_gather(shards[0]).reshape(IN_COLS, D_MODEL)
    me = 4 * lax.axis_index("x") + 2 * lax.axis_index("y") + lax.axis_index("c")
    later = shards[1:]
    w_sems, w_thru, token = _push_start(
        ("gather",) * len(later), [a for s in later for a in (s, _landing("gather", s))], w_in_t, "weights_start")

    def later_weights(after):
        done = _push_wait(("gather",) * len(later), w_sems, w_thru, after, "weights_wait")
        whole = [lax.dynamic_update_index_in_dim(got, mine, me, 0) for mine, got in zip(done[0::2], done[1::2])]
        return (whole[0].reshape(D_MODEL, D_MODEL), whole[1].reshape(2, D_FF, D_MODEL), whole[2].reshape(D_FF, D_MODEL),
                whole[3].reshape(D_MODEL, PLE_DIM), whole[4].reshape(D_MODEL, D_MODEL))

    flights = {}

    def by_owner(g):
        return g.reshape(N_DEV, -1, g.shape[-1])

    def grads_ready(group, grads):
        blocks = [by_owner(g) for g in grads]
        if group == "in":
            flights[group] = blocks[0]
            return None
        sems, thru, sent = _push_start(
            ("scatter",) * len(blocks), [a for b in blocks for a in (b, _landing("scatter", b))], token,
            "grads_" + group + "_start")
        flights[group] = (sems, thru)
        return sent

    loss, dx, smalls = _local_step(
        x[0], p[0, 0], loss_target[0], attn_pre_norm + token[0:1, 0:1], w_in_t,
        hg_lower_gamma, hg_out_norm, sb_out_norm, attn_post_norm, ffn_pre_norm, ffn_post_norm, later_weights, grads_ready)
    dg_attn_pre, dg_attn_post, dg_ffn_pre, dg_ffn_post, dg_hg, dg_sb, d_lb = smalls

    zeros_half = jnp.zeros((1, HG_WIDTH), F32)
    small_pack = jnp.concatenate([
        dg_attn_pre, dg_attn_post, dg_ffn_pre, dg_ffn_post, _pack_row(dg_hg, dg_sb), _pack_row(d_lb, zeros_half),
        jnp.broadcast_to(loss[0:1, 0:1], (1, D_MODEL)), jnp.zeros((1, D_MODEL), F32)], axis=0)
    in_blocks = flights["in"]
    in_sems, in_thru, in_token = _push_start(
        ("scatter", "gather"),
        (in_blocks, _landing("scatter", in_blocks), small_pack, _landing("gather", small_pack)), token,
        "grads_in_start")
    ffn = _push_wait(("scatter",) * 4, *flights["ffn"], in_token, "grads_ffn_wait")
    sent_out, got_out = _push_wait(("scatter",), *flights["out"], ffn[1], "grads_out_wait")
    sent_in, got_in, my_small, all_smalls = _push_wait(("scatter", "gather"), in_sems, in_thru, got_out, "grads_in_wait")
    all_smalls = lax.dynamic_update_index_in_dim(all_smalls, my_small, me, 0)

    def reduced(sent, got, name):
        return _sum_blocks(lax.dynamic_index_in_dim(sent, me, 0, keepdims=False), got, "grad_sum_" + name)

    sums = (reduced(sent_in, got_in, "w_in"), reduced(sent_out, got_out, "w_out"), reduced(ffn[0], ffn[1], "w_gate_up"),
            reduced(ffn[2], ffn[3], "w_down"), reduced(ffn[4], ffn[5], "ple_proj"), reduced(ffn[6], ffn[7], "ple_gate"))

    out_g, out_d, out_m, out_v = {}, {}, {}, {}
    for i, name in enumerate(names):
        g = sums[i].T if by_columns[i] else sums[i]
        out_g[name] = g[None]
        d, m, v = _adamw(big[i], g, big_m[i], big_v[i], "adamw_" + name)
        out_d[name], out_m[name], out_v[name] = d[None], m[None], v[None]

    def small_rows(pre, gam, hg, sb, post, fpre, fpost):
        return jnp.concatenate([pre, post, fpre, fpost, _pack_row(hg, sb), _pack_row(gam[0], gam[1]),
                                jnp.zeros((2, D_MODEL), F32)], axis=0)

    packs = _small_update(
        all_smalls,
        small_rows(attn_pre_norm, hg_lower_gamma, hg_out_norm, sb_out_norm, attn_post_norm, ffn_pre_norm, ffn_post_norm),
        small_rows(m_attn_pre_norm, m_hg_lower_gamma, m_hg_out_norm, m_sb_out_norm, m_attn_post_norm, m_ffn_pre_norm,
                   m_ffn_post_norm),
        small_rows(v_attn_pre_norm, v_hg_lower_gamma, v_hg_out_norm, v_sb_out_norm, v_attn_post_norm, v_ffn_pre_norm,
                   v_ffn_post_norm))

    def unpack(pk):
        return {
            "attn_pre_norm": pk[ROW_ATTN_PRE:ROW_ATTN_PRE + 1],
            "hg_lower_gamma": pk[ROW_GAMMA].reshape(2, HG_WIDTH),
            "hg_out_norm": pk[ROW_OUT_NORMS:ROW_OUT_NORMS + 1, :HG_WIDTH],
            "sb_out_norm": pk[ROW_OUT_NORMS:ROW_OUT_NORMS + 1, HG_WIDTH:],
            "attn_post_norm": pk[ROW_ATTN_POST:ROW_ATTN_POST + 1],
            "ffn_pre_norm": pk[ROW_FFN_PRE:ROW_FFN_PRE + 1],
            "ffn_post_norm": pk[ROW_FFN_POST:ROW_FFN_POST + 1],
        }

    sg, sd, sm, sv = (unpack(pk) for pk in packs)
    out_g.update(sg), out_d.update(sd), out_m.update(sm), out_v.update(sv)
    order = ("attn_pre_norm", "w_in", "hg_lower_gamma", "hg_out_norm", "sb_out_norm", "w_out", "attn_post_norm",
             "ffn_pre_norm", "w_gate_up", "w_down", "ffn_post_norm", "ple_proj", "ple_gate")
    total_loss = packs[0][ROW_LOSS, 0]
    return (total_loss, dx[None], *[out_g[n] for n in order], *[out_d[n] for n in order],
            *[out_m[n] for n in order], *[out_v[n] for n in order])
```

```python
import jax
import jax.numpy as jnp
from jax import lax
from jax.experimental import pallas as pl
from jax.experimental.pallas import tpu as pltpu

F32 = jnp.float32
BF16 = jnp.bfloat16

D_MODEL = 1024
HG_WIDTH = 512
HG_HEADS = 4
HG_DK = 128
SB_WIDTH = 512
SB_HEADS = 8
SB_DH = 64
SB_BLOCK = 128
SB_PP = 2
SB_TQ = 1024
SB_TK = 256
D_FF = 2816
PLE_DIM = 256
IN_COLS = 4 * HG_WIDTH + 3 * SB_WIDTH
EPS = 1e-6
N_DEV = 8

HG_SUB = 16
HG_TILE = 128
FFN_TF = 1408
ROW_TILE = 256
VMEM_LIMIT = 56 * 1024 * 1024
WGRAD_ACC_BYTES = 8 * 1024 * 1024

ADAM_LR = 0.001
ADAM_B1 = 0.9
ADAM_B2 = 0.999
ADAM_EPS = 1e-08
ADAM_WD = 0.01
ADAM_STEP = 10

NT_DIMS = (((1,), (1,)), ((), ()))
TN_DIMS = (((0,), (0,)), ((), ()))


def _params(semantics):
    return pltpu.CompilerParams(dimension_semantics=semantics, vmem_limit_bytes=VMEM_LIMIT)


def _dot(a, b):
    return jnp.dot(a, b, preferred_element_type=F32)


def _dot_nt(a, b):
    return lax.dot_general(a, b, NT_DIMS, preferred_element_type=F32)


def _dot_tn(a, b):
    return lax.dot_general(a, b, TN_DIMS, preferred_element_type=F32)


def _sigmoid(z):
    return 1.0 / (1.0 + jnp.exp(-z))


def _rstd(xv):
    return lax.rsqrt(jnp.mean(xv * xv, axis=-1, keepdims=True) + EPS)


def _rms_bwd(xv, r, g, dn):
    xn = xv * r
    gh = dn * g
    dx = r * (gh - xn * jnp.mean(gh * xn, axis=-1, keepdims=True))
    return dx, dn * xn


def _colsum(a):
    return jnp.sum(a, axis=0, keepdims=True)


def _row_spec(tm, width, col=0):
    return pl.BlockSpec((tm, width), lambda i, col=col: (i, col))


def _full_spec(shape):
    return pl.BlockSpec(shape, lambda i: (0,) * len(shape))


def _in_proj(x, g, w_t):
    T = x.shape[0]
    tm = ROW_TILE

    def body(x_ref, g_ref, w_ref, proj_ref, u_ref, qkv_ref):
        xv = x_ref[...]
        u = (xv * _rstd(xv) * g_ref[...]).astype(BF16)
        u_ref[...] = u
        proj = _dot_nt(u, w_ref[...])
        proj_ref[...] = proj
        qkv_ref[...] = proj[:, 4 * HG_WIDTH:].astype(BF16)

    return pl.pallas_call(
        body, grid=(T // tm,), name="in_proj",
        in_specs=[_row_spec(tm, D_MODEL), _full_spec((1, D_MODEL)), _full_spec((IN_COLS, D_MODEL))],
        out_specs=[_row_spec(tm, IN_COLS), _row_spec(tm, D_MODEL), _row_spec(tm, 3 * SB_WIDTH)],
        out_shape=[jax.ShapeDtypeStruct((T, IN_COLS), F32), jax.ShapeDtypeStruct((T, D_MODEL), BF16),
                   jax.ShapeDtypeStruct((T, 3 * SB_WIDTH), BF16)],
        compiler_params=_params(("parallel",)),
    )(x, g, w_t)


def _out_proj(o_hg, proj, o_sb, x, g_hg, g_sb, g_post, w):
    T = x.shape[0]
    tm = ROW_TILE

    def body(ohg_ref, hg_ref, osb_ref, x_ref, ghg_ref, gsb_ref, gpost_ref, w_ref, cat_ref, mix_ref, h1_ref):
        ohg = ohg_ref[...]
        hg = hg_ref[...]
        osb = osb_ref[...]
        a = ohg * _rstd(ohg) * ghg_ref[...] * (hg * _sigmoid(hg))
        n2 = osb * _rstd(osb) * gsb_ref[...]
        cat = jnp.concatenate([a, n2], axis=1).astype(BF16)
        cat_ref[...] = cat
        mix = _dot(cat, w_ref[...])
        mix_ref[...] = mix
        h1_ref[...] = x_ref[...] + mix * _rstd(mix) * gpost_ref[...]

    return pl.pallas_call(
        body, grid=(T // tm,), name="out_proj",
        in_specs=[_row_spec(tm, HG_WIDTH), _row_spec(tm, HG_WIDTH, 3), _row_spec(tm, SB_WIDTH), _row_spec(tm, D_MODEL),
                  _full_spec((1, HG_WIDTH)), _full_spec((1, SB_WIDTH)), _full_spec((1, D_MODEL)),
                  _full_spec((D_MODEL, D_MODEL))],
        out_specs=[_row_spec(tm, D_MODEL)] * 3,
        out_shape=[jax.ShapeDtypeStruct((T, D_MODEL), BF16), jax.ShapeDtypeStruct((T, D_MODEL), F32),
                   jax.ShapeDtypeStruct((T, D_MODEL), F32)],
        compiler_params=_params(("parallel",)),
    )(o_hg, proj, o_sb, x, g_hg, g_sb, g_post, w)


def _ffn_fwd(h1, g_pre, w_gu_t, w_down, g_post):
    T = h1.shape[0]
    tm = ROW_TILE
    n_f = D_FF // FFN_TF

    def body(h1_ref, gpre_ref, wgu_hbm, wd_hbm, gpost_ref, u2_ref, gu_ref, y_ref, h2_ref, wgu_ref, wd_ref, sems):
        @pl.when(pl.program_id(0) == 0)
        def _():
            fetch = [pltpu.make_async_copy(wgu_hbm, wgu_ref, sems.at[0]), pltpu.make_async_copy(wd_hbm, wd_ref, sems.at[1])]
            for cp in fetch:
                cp.start()
            for cp in fetch:
                cp.wait()

        hv = h1_ref[...]
        u2 = (hv * _rstd(hv) * gpre_ref[...]).astype(BF16)
        u2_ref[...] = u2
        y = jnp.zeros((tm, D_MODEL), F32)
        for j in range(n_f):
            cols = slice(j * FFN_TF, (j + 1) * FFN_TF)
            gate = _dot_nt(u2, wgu_ref[0, cols, :])
            up = _dot_nt(u2, wgu_ref[1, cols, :])
            gu_ref[0, :, cols] = gate.astype(BF16)
            gu_ref[1, :, cols] = up.astype(BF16)
            y = y + _dot((gate * _sigmoid(gate) * up).astype(BF16), wd_ref[cols, :])
        y_ref[...] = y
        h2_ref[...] = hv + y * _rstd(y) * gpost_ref[...]

    row = lambda width: pl.BlockSpec((tm, width), lambda i: (i, 0))
    vec = pl.BlockSpec((1, D_MODEL), lambda i: (0, 0))
    hbm = pl.BlockSpec(memory_space=pl.ANY)
    return pl.pallas_call(
        body, grid=(T // tm,), name="ffn_fwd",
        in_specs=[row(D_MODEL), vec, hbm, hbm, vec],
        out_specs=[row(D_MODEL), pl.BlockSpec((2, tm, D_FF), lambda i: (0, i, 0)), row(D_MODEL), row(D_MODEL)],
        out_shape=[jax.ShapeDtypeStruct((T, D_MODEL), BF16), jax.ShapeDtypeStruct((2, T, D_FF), BF16),
                   jax.ShapeDtypeStruct((T, D_MODEL), F32), jax.ShapeDtypeStruct((T, D_MODEL), F32)],
        scratch_shapes=[pltpu.VMEM((2, D_FF, D_MODEL), BF16), pltpu.VMEM((D_FF, D_MODEL), BF16),
                        pltpu.SemaphoreType.DMA((2,))],
        compiler_params=_params(("arbitrary",)),
    )(h1, g_pre, w_gu_t, w_down, g_post)


def _ple_loss(h2, p, target, w_pp_t, w_pg):
    T = h2.shape[0]
    tm = ROW_TILE

    def body(h2_ref, p_ref, t_ref, wpp_ref, wpg_ref, dh2_ref, de_ref, dg_ref, loss_ref):
        i = pl.program_id(0)
        h2v = h2_ref[...]
        e = _dot_nt(p_ref[...].astype(BF16), wpp_ref[...])
        sg = _sigmoid(_dot(h2v.astype(BF16), wpg_ref[...]))
        diff = h2v + e * sg - t_ref[...]
        part = jnp.sum(jnp.sum(diff * diff, axis=1, keepdims=True), axis=0, keepdims=True) * (0.5 / D_MODEL)

        @pl.when(i == 0)
        def _():
            loss_ref[...] = jnp.zeros_like(loss_ref)

        loss_ref[...] += jnp.broadcast_to(part, loss_ref.shape)
        dh3 = diff * (1.0 / D_MODEL)
        de_ref[...] = (dh3 * sg).astype(BF16)
        dg = (dh3 * e * sg * (1.0 - sg)).astype(BF16)
        dg_ref[...] = dg
        dh2_ref[...] = dh3 + _dot_nt(dg, wpg_ref[...])

    return pl.pallas_call(
        body, grid=(T // tm,), name="ple_loss",
        in_specs=[_row_spec(tm, D_MODEL), _row_spec(tm, PLE_DIM), _row_spec(tm, D_MODEL),
                  _full_spec((D_MODEL, PLE_DIM)), _full_spec((D_MODEL, D_MODEL))],
        out_specs=[_row_spec(tm, D_MODEL)] * 3 + [_full_spec((8, 128))],
        out_shape=[jax.ShapeDtypeStruct((T, D_MODEL), F32), jax.ShapeDtypeStruct((T, D_MODEL), BF16),
                   jax.ShapeDtypeStruct((T, D_MODEL), BF16), jax.ShapeDtypeStruct((8, 128), F32)],
        compiler_params=_params(("arbitrary",)),
    )(h2, p, target, w_pp_t, w_pg)


def _ffn_bwd(dh2, y, h1, gu, g_post, g_pre, w_gu_t, w_down):
    T = h1.shape[0]
    tm = ROW_TILE
    n_f = D_FF // FFN_TF

    def body(dh2_ref, y_ref, h1_ref, gu_ref, gpost_ref, gpre_ref, wgu_hbm, wd_hbm,
             dy_ref, a_ref, dgu_ref, dh1_ref, dgpost_ref, dgpre_ref, wgu_ref, wd_ref, sems):
        i = pl.program_id(0)

        @pl.when(i == 0)
        def _():
            fetch = [pltpu.make_async_copy(wgu_hbm, wgu_ref, sems.at[0]), pltpu.make_async_copy(wd_hbm, wd_ref, sems.at[1])]
            for cp in fetch:
                cp.start()
            dgpost_ref[...] = jnp.zeros_like(dgpost_ref)
            dgpre_ref[...] = jnp.zeros_like(dgpre_ref)
            for cp in fetch:
                cp.wait()

        yv = y_ref[...]
        dh2 = dh2_ref[...]
        dy, gterm = _rms_bwd(yv, _rstd(yv), gpost_ref[...], dh2)
        dgpost_ref[...] += _colsum(gterm)
        dy = dy.astype(BF16)
        dy_ref[...] = dy
        du2 = jnp.zeros((tm, D_MODEL), F32)
        for j in range(n_f):
            cols = slice(j * FFN_TF, (j + 1) * FFN_TF)
            da = _dot_nt(dy, wd_ref[cols, :])
            gate = gu_ref[0, :, cols].astype(F32)
            up = gu_ref[1, :, cols].astype(F32)
            s = _sigmoid(gate)
            sil = gate * s
            a_ref[:, cols] = (sil * up).astype(BF16)
            dgate = (da * up * (s * (1.0 + gate * (1.0 - s)))).astype(BF16)
            dup = (da * sil).astype(BF16)
            dgu_ref[0, :, cols] = dgate
            dgu_ref[1, :, cols] = dup
            du2 = du2 + _dot(dgate, wgu_ref[0, cols, :]) + _dot(dup, wgu_ref[1, cols, :])
        hv = h1_ref[...]
        dx, gterm = _rms_bwd(hv, _rstd(hv), gpre_ref[...], du2)
        dh1_ref[...] = dh2 + dx
        dgpre_ref[...] += _colsum(gterm)

    row = lambda width: pl.BlockSpec((tm, width), lambda i: (i, 0))
    both = pl.BlockSpec((2, tm, D_FF), lambda i: (0, i, 0))
    vec = pl.BlockSpec((1, D_MODEL), lambda i: (0, 0))
    hbm = pl.BlockSpec(memory_space=pl.ANY)
    return pl.pallas_call(
        body, grid=(T // tm,), name="ffn_bwd",
        in_specs=[row(D_MODEL), row(D_MODEL), row(D_MODEL), both, vec, vec, hbm, hbm],
        out_specs=[row(D_MODEL), row(D_FF), both, row(D_MODEL), vec, vec],
        out_shape=[jax.ShapeDtypeStruct((T, D_MODEL), BF16), jax.ShapeDtypeStruct((T, D_FF), BF16),
                   jax.ShapeDtypeStruct((2, T, D_FF), BF16),
                   jax.ShapeDtypeStruct((T, D_MODEL), F32), jax.ShapeDtypeStruct((1, D_MODEL), F32),
                   jax.ShapeDtypeStruct((1, D_MODEL), F32)],
        scratch_shapes=[pltpu.VMEM((2, D_FF, D_MODEL), BF16), pltpu.VMEM((D_FF, D_MODEL), BF16),
                        pltpu.SemaphoreType.DMA((2,))],
        compiler_params=_params(("arbitrary",)),
    )(dh2, y, h1, gu, g_post, g_pre, w_gu_t, w_down)


def _out_bwd(dh1, mix, o_hg, proj, o_sb, g_hg, g_sb, g_post, w, after):
    T = dh1.shape[0]
    tm = ROW_TILE

    def body(dh1_ref, mix_ref, ohg_ref, hg_ref, osb_ref, ghg_ref, gsb_ref, gpost_ref, w_ref, after_ref,
             dmix_ref, dohg_ref, dhg_ref, dosb_ref, dgpost_ref, dghg_ref, dgsb_ref):
        i = pl.program_id(0)

        @pl.when(i == 0)
        def _():
            dgpost_ref[...] = jnp.zeros_like(dgpost_ref)
            dghg_ref[...] = jnp.zeros_like(dghg_ref)
            dgsb_ref[...] = jnp.zeros_like(dgsb_ref)

        mix = mix_ref[...]
        dmix, gterm = _rms_bwd(mix, _rstd(mix), gpost_ref[...], dh1_ref[...])
        dgpost_ref[...] += _colsum(gterm)
        dmix = dmix.astype(BF16)
        dmix_ref[...] = dmix
        dcat = _dot_nt(dmix, w_ref[...])
        da = dcat[:, :HG_WIDTH]
        dn2 = dcat[:, HG_WIDTH:]
        ohg = ohg_ref[...]
        r1 = _rstd(ohg)
        hg = hg_ref[...]
        s = _sigmoid(hg)
        dhg_ref[...] = da * (ohg * r1 * ghg_ref[...]) * (s * (1.0 + hg * (1.0 - s)))
        dohg, gterm = _rms_bwd(ohg, r1, ghg_ref[...], da * (hg * s))
        dohg_ref[...] = dohg
        dghg_ref[...] += _colsum(gterm)
        osb = osb_ref[...]
        dosb, gterm = _rms_bwd(osb, _rstd(osb), gsb_ref[...], dn2)
        dosb_ref[...] = dosb
        dgsb_ref[...] += _colsum(gterm)

    return pl.pallas_call(
        body, grid=(T // tm,), name="out_bwd",
        in_specs=[_row_spec(tm, D_MODEL), _row_spec(tm, D_MODEL), _row_spec(tm, HG_WIDTH), _row_spec(tm, HG_WIDTH, 3),
                  _row_spec(tm, SB_WIDTH), _full_spec((1, HG_WIDTH)), _full_spec((1, SB_WIDTH)),
                  _full_spec((1, D_MODEL)), _full_spec((D_MODEL, D_MODEL)), pl.BlockSpec(memory_space=pl.ANY)],
        out_specs=[_row_spec(tm, D_MODEL), _row_spec(tm, HG_WIDTH), _row_spec(tm, HG_WIDTH), _row_spec(tm, SB_WIDTH),
                   _full_spec((1, D_MODEL)), _full_spec((1, HG_WIDTH)), _full_spec((1, SB_WIDTH))],
        out_shape=[jax.ShapeDtypeStruct((T, D_MODEL), BF16), jax.ShapeDtypeStruct((T, HG_WIDTH), F32),
                   jax.ShapeDtypeStruct((T, HG_WIDTH), F32), jax.ShapeDtypeStruct((T, SB_WIDTH), F32),
                   jax.ShapeDtypeStruct((1, D_MODEL), F32), jax.ShapeDtypeStruct((1, HG_WIDTH), F32),
                   jax.ShapeDtypeStruct((1, SB_WIDTH), F32)],
        compiler_params=_params(("arbitrary",)),
    )(dh1, mix, o_hg, proj, o_sb, g_hg, g_sb, g_post, w, after)


def _in_bwd(d_hgrn, d_hg, d_sq, d_sk, d_sv, x, dh1, g_pre, w_t):
    T = x.shape[0]
    tm = ROW_TILE

    def body(dh_ref, dhg_ref, dsq_ref, dsk_ref, dsv_ref, x_ref, dh1_ref, gpre_ref, w_ref, dproj_ref, dx_ref, dgpre_ref):
        i = pl.program_id(0)

        @pl.when(i == 0)
        def _():
            dgpre_ref[...] = jnp.zeros_like(dgpre_ref)

        dp = jnp.concatenate([dh_ref[...], dhg_ref[...], dsq_ref[...], dsk_ref[...], dsv_ref[...]], axis=1).astype(BF16)
        dproj_ref[...] = dp
        du = _dot(dp, w_ref[...])
        xv = x_ref[...]
        dx, gterm = _rms_bwd(xv, _rstd(xv), gpre_ref[...], du)
        dx_ref[...] = dh1_ref[...] + dx
        dgpre_ref[...] += _colsum(gterm)

    return pl.pallas_call(
        body, grid=(T // tm,), name="in_bwd",
        in_specs=[_row_spec(tm, 3 * HG_WIDTH), _row_spec(tm, HG_WIDTH), _row_spec(tm, SB_WIDTH), _row_spec(tm, SB_WIDTH),
                  _row_spec(tm, SB_WIDTH), _row_spec(tm, D_MODEL), _row_spec(tm, D_MODEL), _full_spec((1, D_MODEL)),
                  _full_spec((IN_COLS, D_MODEL))],
        out_specs=[_row_spec(tm, IN_COLS), _row_spec(tm, D_MODEL), _full_spec((1, D_MODEL))],
        out_shape=[jax.ShapeDtypeStruct((T, IN_COLS), BF16), jax.ShapeDtypeStruct((T, D_MODEL), F32),
                   jax.ShapeDtypeStruct((1, D_MODEL), F32)],
        compiler_params=_params(("arbitrary",)),
    )(d_hgrn, d_hg, d_sq, d_sk, d_sv, x, dh1, g_pre, w_t)


def _wgrad(a, b, name):
    stacked = a.ndim == 3
    S, T, K = a.shape if stacked else (1,) + a.shape
    N = b.shape[1]
    tk = K
    while tk * N * 4 > WGRAD_ACC_BYTES:
        tk //= 2
    assert K % tk == 0 and tk % 128 == 0
    tt = 1024 if T % 1024 == 0 else 512
    n_t = T // tt

    def body(a_ref, b_ref, o_ref, acc_ref):
        t = pl.program_id(2)

        @pl.when(t == 0)
        def _():
            acc_ref[...] = jnp.zeros_like(acc_ref)

        acc_ref[...] += _dot_tn(a_ref[...].astype(BF16), b_ref[...].astype(BF16))

        @pl.when(t == n_t - 1)
        def _():
            o_ref[...] = acc_ref[...].astype(BF16)

    if stacked:
        a_spec = pl.BlockSpec((None, tt, tk), lambda s, k, t: (s, t, k))
        o_spec = pl.BlockSpec((None, tk, N), lambda s, k, t: (s, k, 0))
        o_shape = (S, K, N)
    else:
        a_spec = pl.BlockSpec((tt, tk), lambda s, k, t: (t, k))
        o_spec = pl.BlockSpec((tk, N), lambda s, k, t: (k, 0))
        o_shape = (K, N)
    return pl.pallas_call(
        body, grid=(S, K // tk, n_t), name=name,
        in_specs=[a_spec, pl.BlockSpec((tt, N), lambda s, k, t: (t, 0))],
        out_specs=o_spec,
        out_shape=jax.ShapeDtypeStruct(o_shape, BF16),
        scratch_shapes=[pltpu.VMEM((tk, N), F32)],
        compiler_params=_params(("parallel", "parallel", "arbitrary")),
    )(a, b)


def _hg_gates(hq, hf, gam):
    g0 = gam[0:1, :]
    g1 = gam[1:2, :]
    mx = jnp.maximum(g0, g1)
    e0 = jnp.exp(g0 - mx)
    e1 = jnp.exp(g1 - mx)
    lb = e0 / (e0 + e1)
    s = _sigmoid(hf)
    f = lb + (1.0 - lb) * s
    sq = _sigmoid(hq)
    return hq * sq, sq, s, f, (1.0 - lb) * (1.0 - s), jnp.log(f), lb


def _prefix_in_sub(a, rowmod):
    n = a.shape[0]
    sh = 1
    while sh < HG_SUB:
        a = a + jnp.where(rowmod >= sh, pltpu.roll(a, sh, axis=0), 0.0)
        sh *= 2
    return a


def _suffix_in_sub(a, rowmod):
    n = a.shape[0]
    sh = 1
    while sh < HG_SUB:
        a = a + jnp.where(rowmod < HG_SUB - sh, pltpu.roll(a, n - sh, axis=0), 0.0)
        sh *= 2
    return a


def _hg_fwd(proj, gam):
    T = proj.shape[0]
    tb = HG_TILE
    n_sub = tb // HG_SUB

    def body(hq_ref, hf_ref, hi_ref, gam_ref, o_ref, hist_ref, st_ref, qt_ref, kk_ref, b_ref, od_ref):
        i = pl.program_id(0)

        @pl.when(i == 0)
        def _():
            st_ref[...] = jnp.zeros_like(st_ref)

        rowmod_w = lax.broadcasted_iota(jnp.int32, (tb, HG_WIDTH), 0) % HG_SUB
        rowmod = lax.broadcasted_iota(jnp.int32, (tb, 1), 0) % HG_SUB
        q, _, _, f, kk, lf, _ = _hg_gates(hq_ref[...], hf_ref[...], gam_ref[...])
        b = _prefix_in_sub(lf, rowmod_w)
        qt_ref[...] = (q * jnp.exp(b)).astype(BF16)
        kk_ref[...] = kk
        b_ref[...] = b

        for h in range(HG_HEADS):
            cs = slice(h * HG_DK, (h + 1) * HG_DK)
            qh = q[:, cs]
            kh = kk[:, cs]
            fh = f[:, cs]
            vh = hi_ref[:, cs]
            acc = jnp.sum(qh * kh, axis=1, keepdims=True) * vh
            decay = fh
            for d in range(1, HG_SUB):
                if d > 1:
                    decay = decay * pltpu.roll(fh, d - 1, axis=0)
                e = qh * pltpu.roll(kh, d, axis=0) * decay
                sc = jnp.where(rowmod >= d, jnp.sum(e, axis=1, keepdims=True), 0.0)
                acc = acc + sc * pltpu.roll(vh, d, axis=0)
            od_ref[:, cs] = acc

        for m in range(n_sub):
            rs = slice(m * HG_SUB, (m + 1) * HG_SUB)
            for h in range(HG_HEADS):
                cs = slice(h * HG_DK, (h + 1) * HG_DK)
                st = st_ref[h]
                hist_ref[m, h] = st
                o_int = _dot_nt(qt_ref[rs, cs], st.astype(BF16))
                bs = b_ref[rs, cs]
                bl = bs[HG_SUB - 1:HG_SUB, :]
                kt = (kk_ref[rs, cs] * jnp.exp(bl - bs)).astype(BF16)
                st_ref[h] = st * jnp.exp(bl) + _dot_tn(hi_ref[rs, cs].astype(BF16), kt)
                o_ref[rs, cs] = o_int + od_ref[rs, cs]

    col = lambda c: pl.BlockSpec((tb, HG_WIDTH), lambda i, c=c: (i, c))
    return pl.pallas_call(
        body, grid=(T // tb,), name="hg_fwd",
        in_specs=[col(0), col(1), col(2), _full_spec((2, HG_WIDTH))],
        out_specs=[pl.BlockSpec((tb, HG_WIDTH), lambda i: (i, 0)),
                   pl.BlockSpec((n_sub, HG_HEADS, HG_DK, HG_DK), lambda i: (i, 0, 0, 0))],
        out_shape=[jax.ShapeDtypeStruct((T, HG_WIDTH), F32),
                   jax.ShapeDtypeStruct((T // HG_SUB, HG_HEADS, HG_DK, HG_DK), F32)],
        scratch_shapes=[pltpu.VMEM((HG_HEADS, HG_DK, HG_DK), F32), pltpu.VMEM((tb, HG_WIDTH), BF16),
                        pltpu.VMEM((tb, HG_WIDTH), F32), pltpu.VMEM((tb, HG_WIDTH), F32),
                        pltpu.VMEM((tb, HG_WIDTH), F32)],
        compiler_params=_params(("arbitrary",)),
    )(proj, proj, proj, gam)


def _hg_bwd(proj, gam, d_o, hist):
    T = proj.shape[0]
    tb = HG_TILE
    n_sub = tb // HG_SUB
    n_t = T // tb

    def body(hq_ref, hf_ref, hi_ref, gam_ref, do_ref, hist_ref, dout_ref, dlb_ref,
             dst_ref, q_ref, kk_ref, b_ref, dq_ref, dkk_ref, db_ref, dv_ref):
        i = pl.program_id(0)

        @pl.when(i == 0)
        def _():
            dst_ref[...] = jnp.zeros_like(dst_ref)
            dlb_ref[...] = jnp.zeros_like(dlb_ref)

        rowmod_w = lax.broadcasted_iota(jnp.int32, (tb, HG_WIDTH), 0) % HG_SUB
        rowmod = lax.broadcasted_iota(jnp.int32, (tb, 1), 0) % HG_SUB
        last_row = lax.broadcasted_iota(jnp.int32, (HG_SUB, 1), 0) == HG_SUB - 1
        hq = hq_ref[...]
        q, sq, s, f, kk, lf, lb = _hg_gates(hq, hf_ref[...], gam_ref[...])
        b = _prefix_in_sub(lf, rowmod_w)
        q_ref[...] = q
        kk_ref[...] = kk
        b_ref[...] = b

        for m in reversed(range(n_sub)):
            rs = slice(m * HG_SUB, (m + 1) * HG_SUB)
            for h in range(HG_HEADS):
                cs = slice(h * HG_DK, (h + 1) * HG_DK)
                st = hist_ref[m, h]
                dst = dst_ref[h]
                dstb = dst.astype(BF16)
                dos = do_ref[rs, cs].astype(BF16)
                vs = hi_ref[rs, cs].astype(BF16)
                bs = b_ref[rs, cs]
                bl = bs[HG_SUB - 1:HG_SUB, :]
                ebl = jnp.exp(bl)
                qt = q_ref[rs, cs] * jnp.exp(bs)
                decay = jnp.exp(bl - bs)
                kt = kk_ref[rs, cs] * decay
                dqt = _dot(dos, st.astype(BF16))
                dkt = _dot(vs, dstb)
                dv_ref[rs, cs] = _dot_nt(kt.astype(BF16), dstb)
                dst_ref[h] = dst * ebl + _dot_tn(dos, qt.astype(BF16))
                dq_ref[rs, cs] = dqt * jnp.exp(bs)
                dkk_ref[rs, cs] = dkt * decay
                dktk = dkt * kt
                dbl = _colsum(dktk) + _colsum(dst * st) * ebl
                db_ref[rs, cs] = dqt * qt - dktk + jnp.where(last_row, dbl, 0.0)

        for h in range(HG_HEADS):
            cs = slice(h * HG_DK, (h + 1) * HG_DK)
            qh = q[:, cs]
            kh = kk[:, cs]
            fh = f[:, cs]
            vh = hi_ref[:, cs]
            doh = do_ref[:, cs]
            dsc = jnp.sum(doh * vh, axis=1, keepdims=True)
            sc = jnp.sum(qh * kh, axis=1, keepdims=True)
            dqh = dsc * kh
            dkh = dsc * qh
            dvh = sc * doh
            dbh = jnp.zeros_like(qh)
            decay = fh
            for d in range(1, HG_SUB):
                back = tb - d
                valid = rowmod >= d
                if d > 1:
                    decay = decay * pltpu.roll(fh, d - 1, axis=0)
                kd = pltpu.roll(kh, d, axis=0)
                ex = jnp.where(valid, decay, 0.0)
                pd = kd * ex
                sc = jnp.sum(qh * pd, axis=1, keepdims=True)
                dsc = jnp.where(valid, jnp.sum(doh * pltpu.roll(vh, d, axis=0), axis=1, keepdims=True), 0.0)
                dqh = dqh + dsc * pd
                dsq = dsc * qh
                w = dsq * pd
                dkh = dkh + pltpu.roll(dsq * ex, back, axis=0)
                dvh = dvh + pltpu.roll(sc * doh, back, axis=0)
                dbh = dbh + w - pltpu.roll(w, back, axis=0)
            dq_ref[:, cs] += dqh
            dkk_ref[:, cs] += dkh
            dv_ref[:, cs] += dvh
            db_ref[:, cs] += dbh

        dlf = _suffix_in_sub(db_ref[...], rowmod_w)
        df = dlf / f - dkk_ref[...]
        dout_ref[:, 0:HG_WIDTH] = dq_ref[...] * (sq * (1.0 + hq * (1.0 - sq)))
        dout_ref[:, HG_WIDTH:2 * HG_WIDTH] = df * (1.0 - lb) * (s * (1.0 - s))
        dout_ref[:, 2 * HG_WIDTH:3 * HG_WIDTH] = dv_ref[...]
        dlb_ref[...] += _colsum(df * (1.0 - s))

    col = lambda c: pl.BlockSpec((tb, HG_WIDTH), lambda i, c=c: (n_t - 1 - i, c))
    return pl.pallas_call(
        body, grid=(n_t,), name="hg_bwd",
        in_specs=[col(0), col(1), col(2), _full_spec((2, HG_WIDTH)),
                  pl.BlockSpec((tb, HG_WIDTH), lambda i: (n_t - 1 - i, 0)),
                  pl.BlockSpec((n_sub, HG_HEADS, HG_DK, HG_DK), lambda i: (n_t - 1 - i, 0, 0, 0))],
        out_specs=[pl.BlockSpec((tb, 3 * HG_WIDTH), lambda i: (n_t - 1 - i, 0)), _full_spec((1, HG_WIDTH))],
        out_shape=[jax.ShapeDtypeStruct((T, 3 * HG_WIDTH), F32), jax.ShapeDtypeStruct((1, HG_WIDTH), F32)],
        scratch_shapes=[pltpu.VMEM((HG_HEADS, HG_DK, HG_DK), F32)] + [pltpu.VMEM((tb, HG_WIDTH), F32)] * 7,
        compiler_params=_params(("arbitrary",)),
    )(proj, proj, proj, gam, d_o, hist)


def _sb_masks():
    row = lax.broadcasted_iota(jnp.int32, (SB_TK, SB_TK), 0)
    col = lax.broadcasted_iota(jnp.int32, (SB_TK, SB_TK), 1)
    suffix = (row >= col).astype(BF16)
    prefix = (row <= col).astype(BF16)
    query = lax.broadcasted_iota(jnp.int32, (SB_TQ, SB_TK), 0)
    key = lax.broadcasted_iota(jnp.int32, (SB_TQ, SB_TK), 1)
    causal = [key + r * SB_TK < query for r in range(SB_TQ // SB_TK)]
    lane = lax.broadcasted_iota(jnp.int32, (SB_TQ, SB_BLOCK), 1)
    return suffix, prefix, causal, lane


def _sum_right(a, suffix):
    return _dot(a.astype(BF16), suffix)


def _per_lane_tile(t, c, op):
    return jnp.concatenate([op(t[:, n:n + SB_BLOCK], c) for n in range(0, t.shape[1], SB_BLOCK)], axis=1)


def _sb_block(qb, kj, suffix, causal, c):
    z = _dot_nt(qb, kj)
    sp = jnp.maximum(z, 0.0) + jnp.log(1.0 + jnp.exp(-jnp.abs(z)))
    if causal is not None:
        sp = jnp.where(causal, sp, 0.0)
    big_l = _sum_right(sp, suffix)
    a = jnp.exp(_per_lane_tile(z - big_l, c, jnp.subtract))
    if causal is not None:
        a = jnp.where(causal, a, 0.0)
    return z, a, big_l


def _sb_tile_copies(to_hbm, bufs, stores, sems, slot, head, i, j, n_chain):
    copies = []
    for ch in range(n_chain):
        for w in range(2):
            vmem, hbm = bufs[w].at[slot, ch], stores[w].at[head + ch, i, j]
            src, dst = (vmem, hbm) if to_hbm else (hbm, vmem)
            copies.append(pltpu.make_async_copy(src, dst, sems.at[slot, ch, w]))
    return copies


def _half_masked(pair, lane, scale=1.0):
    pair = pair.astype(F32) * scale
    return jnp.where(lane < SB_DH, pair, 0.0).astype(BF16), jnp.where(lane < SB_DH, 0.0, pair).astype(BF16)


def _sb_fwd(qkv):
    T = qkv.shape[0]
    width = SB_PP * SB_BLOCK
    n_chain = 2 * SB_PP
    n_sub = SB_TQ // SB_TK

    def body(q_ref, k_ref, v_ref, o_ref, a_st, s_st, acc_ref, c_ref, qm, a_buf, s_buf, sems):
        g = pl.program_id(0)
        i = pl.program_id(1)
        suffix, _, causal, lane = _sb_masks()
        for pp in range(SB_PP):
            qm[2 * pp], qm[2 * pp + 1] = _half_masked(q_ref[:, pp * SB_BLOCK:(pp + 1) * SB_BLOCK], lane, SB_DH ** -0.5)
        acc_ref[...] = jnp.zeros_like(acc_ref)
        c_ref[...] = jnp.zeros_like(c_ref)
        a_buf[...] = jnp.zeros_like(a_buf)
        s_buf[...] = jnp.zeros_like(s_buf)

        def copies(slot, j):
            return _sb_tile_copies(True, (a_buf, s_buf), (a_st, s_st), sems, slot, g * n_chain, i, j, n_chain)

        def step(j, slot, r=None, reuse=True):
            rows = pl.ds(pl.multiple_of(j * SB_TK, SB_TK), SB_TK)
            qs = slice(0 if r is None else r * SB_TK, SB_TQ)
            mask = None if r is None else causal[r][qs]
            if reuse:
                for cp in copies(slot, j):
                    cp.wait()
            for ch in range(n_chain):
                pair = slice((ch // 2) * SB_BLOCK, (ch // 2 + 1) * SB_BLOCK)
                c = c_ref[ch, qs]
                z, a, big_l = _sb_block(qm[ch, qs], k_ref[rows, pair], suffix, mask, c)
                ab = a.astype(BF16)
                acc_ref[ch, qs] += _dot(ab, v_ref[rows, pair])
                a_buf[slot, ch, qs] = ab
                s_buf[slot, ch, qs] = z.astype(BF16)
                c_ref[ch, qs] = c + jnp.broadcast_to(big_l[:, 0:1], c.shape)
            for cp in copies(slot, j):
                cp.start()

        first = i * n_sub
        for s, r in enumerate(reversed(range(n_sub))):
            step(first + r, s % 2, r, reuse=s >= 2)

        @pl.loop(0, first)
        def _(t):
            step(first - 1 - t, (n_sub + t) % 2)

        for slot in range(2):
            for cp in copies(slot, 0):
                cp.wait()
        for pp in range(SB_PP):
            o_ref[:, pp * SB_BLOCK:(pp + 1) * SB_BLOCK] = jnp.where(lane < SB_DH, acc_ref[2 * pp], acc_ref[2 * pp + 1])

    assert n_sub >= 2 and n_sub % 2 == 0
    n_g = SB_WIDTH // width
    blk = lambda part: pl.BlockSpec((SB_TQ, width), lambda g, i, part=part: (i, part * n_g + g))
    whole = lambda part: pl.BlockSpec((T, width), lambda g, i, part=part: (0, part * n_g + g))
    tiles = jax.ShapeDtypeStruct((SB_HEADS, T // SB_TQ, T // SB_TK, SB_TQ, SB_TK), BF16)
    return pl.pallas_call(
        body, grid=(n_g, T // SB_TQ), name="sb_fwd",
        in_specs=[blk(0), whole(1), whole(2)],
        out_specs=[blk(0), pl.BlockSpec(memory_space=pl.ANY), pl.BlockSpec(memory_space=pl.ANY)],
        out_shape=[jax.ShapeDtypeStruct((T, SB_WIDTH), F32), tiles, tiles],
        scratch_shapes=[pltpu.VMEM((n_chain, SB_TQ, SB_BLOCK), F32), pltpu.VMEM((n_chain, SB_TQ, SB_BLOCK), F32),
                        pltpu.VMEM((n_chain, SB_TQ, SB_BLOCK), BF16),
                        pltpu.VMEM((2, n_chain, SB_TQ, SB_TK), BF16), pltpu.VMEM((2, n_chain, SB_TQ, SB_TK), BF16),
                        pltpu.SemaphoreType.DMA((2, n_chain, 2))],
        compiler_params=_params(("parallel", "arbitrary")),
    )(qkv, qkv, qkv)


def _sb_bwd(qkv, a_st, s_st, d_o, after):
    T = qkv.shape[0]
    width = SB_PP * SB_BLOCK
    n_chain = 2 * SB_PP
    n_sub = SB_TQ // SB_TK
    scale = SB_DH ** -0.5

    def body(q_ref, k_ref, v_ref, a_st_ref, s_st_ref, do_ref, after_ref, dq_ref, dk_ref, dv_ref, acc_ref, gc_ref, qm, dom,
             a_buf, s_buf, sems):
        g_idx = pl.program_id(0)
        i = pl.program_id(1)

        @pl.when(i == 0)
        def _():
            dk_ref[...] = jnp.zeros_like(dk_ref)
            dv_ref[...] = jnp.zeros_like(dv_ref)

        _, prefix, causal, lane = _sb_masks()
        for pp in range(SB_PP):
            pair = slice(pp * SB_BLOCK, (pp + 1) * SB_BLOCK)
            qm[2 * pp], qm[2 * pp + 1] = _half_masked(q_ref[:, pair], lane, scale)
            dom[2 * pp], dom[2 * pp + 1] = _half_masked(do_ref[:, pair], lane)
        acc_ref[...] = jnp.zeros_like(acc_ref)
        gc_ref[...] = jnp.zeros_like(gc_ref)

        def copies(slot, j):
            return _sb_tile_copies(False, (a_buf, s_buf), (a_st_ref, s_st_ref), sems, slot, g_idx * n_chain, i, j, n_chain)

        def step(j, slot, r=None, last=False):
            rows = pl.ds(pl.multiple_of(j * SB_TK, SB_TK), SB_TK)
            qs = slice(0 if r is None else r * SB_TK, SB_TQ)
            mask = None if r is None else causal[r][qs]
            for cp in copies(slot, j):
                cp.wait()
            if not last:
                for cp in copies(1 - slot, j + 1):
                    cp.start()
            for pp in range(SB_PP):
                pair = slice(pp * SB_BLOCK, (pp + 1) * SB_BLOCK)
                kj = k_ref[rows, pair]
                vj = v_ref[rows, pair]
                dk = jnp.zeros((SB_TK, SB_BLOCK), F32)
                dv = jnp.zeros((SB_TK, SB_BLOCK), F32)
                for ch in (2 * pp, 2 * pp + 1):
                    ab = a_buf[slot, ch, qs]
                    g = ab.astype(F32) * _dot_nt(dom[ch, qs], vj)
                    g_left = _sum_right(g, prefix)
                    gc = gc_ref[ch, qs]
                    dz = g - _sigmoid(s_buf[slot, ch, qs].astype(F32)) * _per_lane_tile(g_left, gc, jnp.add)
                    if mask is not None:
                        dz = jnp.where(mask, dz, 0.0)
                    dzb = dz.astype(BF16)
                    dk = dk + _dot_tn(dzb, qm[ch, qs])
                    dv = dv + _dot_tn(ab, dom[ch, qs])
                    acc_ref[ch, qs] += _dot(dzb, kj)
                    gc_ref[ch, qs] = gc + jnp.broadcast_to(g_left[:, SB_TK - 1:SB_TK], gc.shape)
                dk_ref[rows, pair] += dk
                dv_ref[rows, pair] += dv

        first = i * n_sub
        for cp in copies(0, 0):
            cp.start()

        @pl.loop(0, first)
        def _(j):
            step(j, j % 2)

        for r in range(n_sub):
            step(first + r, r % 2, r, last=r == n_sub - 1)
        for pp in range(SB_PP):
            dq_ref[:, pp * SB_BLOCK:(pp + 1) * SB_BLOCK] = scale * jnp.where(lane < SB_DH, acc_ref[2 * pp],
                                                                             acc_ref[2 * pp + 1])
    n_g = SB_WIDTH // width
    blk = lambda part: pl.BlockSpec((SB_TQ, width), lambda g, i, part=part: (i, part * n_g + g))
    whole = lambda part: pl.BlockSpec((T, width), lambda g, i, part=part: (0, part * n_g + g))
    return pl.pallas_call(
        body, grid=(n_g, T // SB_TQ), name="sb_bwd",
        in_specs=[blk(0), whole(1), whole(2), pl.BlockSpec(memory_space=pl.ANY), pl.BlockSpec(memory_space=pl.ANY), blk(0),
                  pl.BlockSpec(memory_space=pl.ANY)],
        out_specs=[blk(0), whole(0), whole(0)],
        out_shape=[jax.ShapeDtypeStruct((T, SB_WIDTH), F32)] * 3,
        scratch_shapes=[pltpu.VMEM((n_chain, SB_TQ, SB_BLOCK), F32), pltpu.VMEM((n_chain, SB_TQ, SB_BLOCK), F32),
                        pltpu.VMEM((n_chain, SB_TQ, SB_BLOCK), BF16), pltpu.VMEM((n_chain, SB_TQ, SB_BLOCK), BF16),
                        pltpu.VMEM((2, n_chain, SB_TQ, SB_TK), BF16), pltpu.VMEM((2, n_chain, SB_TQ, SB_TK), BF16),
                        pltpu.SemaphoreType.DMA((2, n_chain, 2))],
        compiler_params=_params(("parallel", "arbitrary")),
    )(qkv, qkv, qkv, a_st, s_st, d_o, after)


HBM = pl.BlockSpec(memory_space=pltpu.HBM)
MESH = pl.DeviceIdType.MESH


def _place():
    return lax.axis_index("x"), lax.axis_index("y"), lax.axis_index("c")


def _all_gather(block):
    rows, cols = block.shape

    def body(x_ref, out_ref, send_sems, recv_sems, local_sem):
        x, y, c = _place()
        me, sibling = (x, y, c), (x, y, 1 - c)
        chips = [(1 - x, y), (x, 1 - y), (1 - x, 1 - y)]

        def slot(px, py, pc):
            return out_ref.at[4 * px + 2 * py + pc]

        def copy(k, blk, to, src=None):
            return pltpu.make_async_remote_copy(
                src_ref=slot(*blk) if src is None else src, dst_ref=slot(*blk),
                send_sem=send_sems.at[k], recv_sem=recv_sems.at[k], device_id=to, device_id_type=MESH)

        mine = pltpu.make_async_copy(x_ref, slot(*me), local_sem)
        mine.start()
        first = [copy(0, me, sibling, src=x_ref)]
        first += [copy(1 + j, me, (*chip, c), src=x_ref) for j, chip in enumerate(chips)]
        for cp in first:
            cp.start()
        passed = [copy(4 + j, (*chip, c), sibling) for j, chip in enumerate(chips)]
        for j, chip in enumerate(chips):
            copy(1 + j, (*chip, c), me).wait_recv()
            passed[j].start()
        copy(0, sibling, me).wait_recv()
        for j, chip in enumerate(chips):
            copy(4 + j, (*chip, 1 - c), me).wait_recv()
        for cp in first + passed:
            cp.wait_send()
        mine.wait()

    return pl.pallas_call(
        body, name="all_gather",
        out_shape=jax.ShapeDtypeStruct((N_DEV, rows, cols), block.dtype),
        in_specs=[HBM], out_specs=HBM,
        scratch_shapes=[pltpu.SemaphoreType.DMA((7,)), pltpu.SemaphoreType.DMA((7,)), pltpu.SemaphoreType.DMA],
    )(block)


SEM = pl.BlockSpec(memory_space=pltpu.SEMAPHORE)
ANY = pl.BlockSpec(memory_space=pl.ANY)
SPLIT_EFFECT = pltpu.SideEffectType.DATAFLOW_SIDE_EFFECTING
N_PEER = N_DEV - 1


def _flow_copies(kind, src_ref, land_ref, send_sems, recv_sems):
    x, y, c = _place()
    copies = []
    for r in range(1, N_DEV):
        px = 1 - x if r & 4 else x
        py = 1 - y if r & 2 else y
        pc = 1 - c if r & 1 else c
        if kind == "gather":
            src, dst = src_ref, land_ref.at[4 * x + 2 * y + c]
        else:
            src, dst = src_ref.at[4 * px + 2 * py + pc], land_ref.at[r - 1]
        copies.append(pltpu.make_async_remote_copy(
            src_ref=src, dst_ref=dst, send_sem=send_sems.at[r - 1], recv_sem=recv_sems.at[r - 1],
            device_id=(px, py, pc), device_id_type=MESH))
    return copies


def _landing(kind, src):
    if kind == "gather":
        return lax.empty((N_DEV,) + src.shape, src.dtype)
    return lax.empty((N_PEER,) + src.shape[1:], src.dtype)


def _push_start(kinds, arrays, after, name):
    n = len(arrays)

    def body(*refs):
        ins, sems, token = refs[:n], refs[n + 1:2 * n + 1], refs[3 * n + 1]
        for f, kind in enumerate(kinds):
            for cp in _flow_copies(kind, ins[2 * f], ins[2 * f + 1], sems[2 * f], sems[2 * f + 1]):
                cp.start()
        token[...] = jnp.zeros_like(token)

    outs = pl.pallas_call(
        body, name=name,
        out_shape=[pltpu.SemaphoreType.DMA((N_PEER,))] * n + [pltpu.HBM(a.shape, a.dtype) for a in arrays]
        + [jax.ShapeDtypeStruct((8, 128), F32)],
        in_specs=[HBM] * n + [ANY], out_specs=[SEM] * n + [HBM] * n + [pl.BlockSpec(memory_space=pltpu.VMEM)],
        input_output_aliases={i: n + i for i in range(n)},
        compiler_params=pltpu.CompilerParams(has_side_effects=SPLIT_EFFECT),
    )(*[pltpu.with_memory_space_constraint(a, pltpu.HBM) for a in arrays], after)
    return outs[:n], outs[n:2 * n], outs[2 * n]


def _push_wait(kinds, sems, arrays, after, name):
    n = len(arrays)

    def body(*refs):
        ins, sem_refs = refs[:n], refs[n:2 * n]
        for f, kind in enumerate(kinds):
            for cp in _flow_copies(kind, ins[2 * f], ins[2 * f + 1], sem_refs[2 * f], sem_refs[2 * f + 1]):
                cp.wait_send()
                cp.wait_recv()

    outs = pl.pallas_call(
        body, name=name,
        out_shape=[pltpu.HBM(a.shape, a.dtype) for a in arrays],
        in_specs=[HBM] * n + [SEM] * n + [ANY], out_specs=[HBM] * n,
        input_output_aliases={i: i for i in range(n)},
        compiler_params=pltpu.CompilerParams(has_side_effects=SPLIT_EFFECT),
    )(*arrays, *sems, after)
    return outs


def _sum_blocks(own, got, name):
    n, rows, cols = got.shape
    tr = next(t for t in (128, 64, 32) if rows % t == 0)

    def body(own_ref, got_ref, out_ref):
        acc = own_ref[...].astype(F32)
        for r in range(n):
            acc = acc + got_ref[r].astype(F32)
        out_ref[...] = acc

    return pl.pallas_call(
        body, grid=(rows // tr,), name=name,
        in_specs=[pl.BlockSpec((tr, cols), lambda i: (i, 0)), pl.BlockSpec((n, tr, cols), lambda i: (0, i, 0))],
        out_specs=pl.BlockSpec((tr, cols), lambda i: (i, 0)),
        out_shape=jax.ShapeDtypeStruct((rows, cols), F32),
        compiler_params=_params(("parallel",)),
    )(own, got)


def _adamw_math(w, g, m, v):
    m = ADAM_B1 * m + (1.0 - ADAM_B1) * g
    v = ADAM_B2 * v + (1.0 - ADAM_B2) * (g * g)
    m_hat = m / (1.0 - ADAM_B1 ** ADAM_STEP)
    v_hat = v / (1.0 - ADAM_B2 ** ADAM_STEP)
    delta = -ADAM_LR * (m_hat / (jnp.sqrt(v_hat) + ADAM_EPS) + ADAM_WD * w)
    return delta, m, v


def _adamw(w, g, m, v, name):
    rows, cols = w.shape
    tr = rows if rows <= 352 else 256

    def body(w_ref, g_ref, m_ref, v_ref, d_out, m_out, v_out):
        d_out[...], m_out[...], v_out[...] = _adamw_math(w_ref[...], g_ref[...], m_ref[...], v_ref[...])

    spec = pl.BlockSpec((tr, cols), lambda i: (i, 0))
    return pl.pallas_call(
        body, grid=(rows // tr,), name=name,
        in_specs=[spec] * 4, out_specs=[spec] * 3,
        out_shape=[jax.ShapeDtypeStruct((rows, cols), F32)] * 3,
        compiler_params=_params(("parallel",)),
    )(w, g, m, v)


ROW_ATTN_PRE, ROW_ATTN_POST, ROW_FFN_PRE, ROW_FFN_POST, ROW_OUT_NORMS, ROW_GAMMA, ROW_LOSS = range(7)


def _small_update(smalls, w, m, v):
    def body(s_ref, w_ref, m_ref, v_ref, g_out, d_out, m_out, v_out):
        g = s_ref[0]
        for j in range(1, N_DEV):
            g = g + s_ref[j]
        wv = w_ref[...]
        g0 = wv[ROW_GAMMA:ROW_GAMMA + 1, 0:HG_WIDTH]
        g1 = wv[ROW_GAMMA:ROW_GAMMA + 1, HG_WIDTH:]
        mx = jnp.maximum(g0, g1)
        e0 = jnp.exp(g0 - mx)
        e1 = jnp.exp(g1 - mx)
        lb = e0 / (e0 + e1)
        dg0 = g[ROW_GAMMA:ROW_GAMMA + 1, 0:HG_WIDTH] * lb * (1.0 - lb)
        row = lax.broadcasted_iota(jnp.int32, g.shape, 0)
        g = jnp.where(row == ROW_GAMMA, jnp.concatenate([dg0, -dg0], axis=1), g)
        g_out[...] = g
        d_out[...], m_out[...], v_out[...] = _adamw_math(wv, g, m_ref[...], v_ref[...])

    return pl.pallas_call(
        body, name="small_update",
        out_shape=[jax.ShapeDtypeStruct((8, D_MODEL), F32)] * 4,
    )(smalls, w, m, v)


def _local_step(x, p, target, g_attn_pre, w_in_t, gamma, g_hg, g_sb, g_attn_post, g_ffn_pre, g_ffn_post, later_weights,
                grads_ready):
    proj, u, qkv = _in_proj(x, g_attn_pre, w_in_t)
    o_hg, hist = _hg_fwd(proj, gamma)
    o_sb, a_st, s_st = _sb_fwd(qkv)
    w_out, w_gu_t, w_down, w_pp_t, w_pg = later_weights(o_sb)
    cat, mix, h1 = _out_proj(o_hg, proj, o_sb, x, g_hg, g_sb, g_attn_post, w_out)
    u2, gu, y, h2 = _ffn_fwd(h1, g_ffn_pre, w_gu_t, w_down, g_ffn_post)
    dh2, de, dg, loss = _ple_loss(h2, p, target, w_pp_t, w_pg)

    d_wpp_t = _wgrad(de, p, "wgrad_ple_proj")
    d_wpg = _wgrad(h2, dg, "wgrad_ple_gate")
    dy, a, dgu, dh1, dg_ffn_post, dg_ffn_pre = _ffn_bwd(dh2, y, h1, gu, g_ffn_post, g_ffn_pre, w_gu_t, w_down)
    d_wdown = _wgrad(a, dy, "wgrad_down")
    sent = grads_ready("ffn", (_wgrad(dgu, u2, "wgrad_gate_up"), d_wdown, d_wpp_t, d_wpg))
    dmix, d_ohg, d_hg, d_osb, dg_attn_post, dg_hg, dg_sb = _out_bwd(dh1, mix, o_hg, proj, o_sb, g_hg, g_sb, g_attn_post,
                                                                    w_out, sent)
    sent = grads_ready("out", (_wgrad(cat, dmix, "wgrad_out"),))
    dq, dk, dv = _sb_bwd(qkv, a_st, s_st, d_osb, sent)
    d_hgrn, d_lb = _hg_bwd(proj, gamma, d_ohg, hist)
    dproj, dx, dg_attn_pre = _in_bwd(d_hgrn, d_hg, dq, dk, dv, x, dh1, g_attn_pre, w_in_t)
    grads_ready("in", (_wgrad(dproj, u, "wgrad_in"),))
    return loss, dx, (dg_attn_pre, dg_attn_post, dg_ffn_pre, dg_ffn_post, dg_hg, dg_sb, d_lb)


def _pack_row(*parts):
    return jnp.concatenate([q.reshape(1, -1) for q in parts], axis=1)


def kernel(x, p, attn_pre_norm, w_in, hg_lower_gamma, hg_out_norm, sb_out_norm, w_out, attn_post_norm, ffn_pre_norm, w_gate_up, w_down, ffn_post_norm, ple_proj, ple_gate, loss_target, m_attn_pre_norm, m_w_in, m_hg_lower_gamma, m_hg_out_norm, m_sb_out_norm, m_w_out, m_attn_post_norm, m_ffn_pre_norm, m_w_gate_up, m_w_down, m_ffn_post_norm, m_ple_proj, m_ple_gate, v_attn_pre_norm, v_w_in, v_hg_lower_gamma, v_hg_out_norm, v_sb_out_norm, v_w_out, v_attn_post_norm, v_ffn_pre_norm, v_w_gate_up, v_w_down, v_ffn_post_norm, v_ple_proj, v_ple_gate):
    big = (w_in[0], w_out[0], w_gate_up[0], w_down[0], ple_proj[0], ple_gate[0])
    big_m = (m_w_in[0], m_w_out[0], m_w_gate_up[0], m_w_down[0], m_ple_proj[0], m_ple_gate[0])
    big_v = (v_w_in[0], v_w_out[0], v_w_gate_up[0], v_w_down[0], v_ple_proj[0], v_ple_gate[0])
    names = ("w_in", "w_out", "w_gate_up", "w_down", "ple_proj", "ple_gate")

    by_columns = (True, False, True, False, True, False)
    shards = [(w.T if t else w).astype(BF16) for w, t in zip(big, by_columns)]
    w_in_t = _all_gather(shards[0]).reshape(IN_COLS, D_MODEL)
    me = 4 * lax.axis_index("x") + 2 * lax.axis_index("y") + lax.axis_index("c")
    later = shards[1:]
    w_sems, w_thru, token = _push_start(
        ("gather",) * len(later), [a for s in later for a in (s, _landing("gather", s))], w_in_t, "weights_start")

    def later_weights(after):
        done = _push_wait(("gather",) * len(later), w_sems, w_thru, after, "weights_wait")
        whole = [lax.dynamic_update_index_in_dim(got, mine, me, 0) for mine, got in zip(done[0::2], done[1::2])]
        return (whole[0].reshape(D_MODEL, D_MODEL), whole[1].reshape(2, D_FF, D_MODEL), whole[2].reshape(D_FF, D_MODEL),
                whole[3].reshape(D_MODEL, PLE_DIM), whole[4].reshape(D_MODEL, D_MODEL))

    flights = {}

    def by_owner(g):
        return g.reshape(N_DEV, -1, g.shape[-1])

    def grads_ready(group, grads):
        blocks = [by_owner(g) for g in grads]
        if group == "in":
            flights[group] = blocks[0]
            return None
        sems, thru, sent = _push_start(
            ("scatter",) * len(blocks), [a for b in blocks for a in (b, _landing("scatter", b))], token,
            "grads_" + group + "_start")
        flights[group] = (sems, thru)
        return sent

    loss, dx, smalls = _local_step(
        x[0], p[0, 0], loss_target[0], attn_pre_norm + token[0:1, 0:1], w_in_t,
        hg_lower_gamma, hg_out_norm, sb_out_norm, attn_post_norm, ffn_pre_norm, ffn_post_norm, later_weights, grads_ready)
    dg_attn_pre, dg_attn_post, dg_ffn_pre, dg_ffn_post, dg_hg, dg_sb, d_lb = smalls

    zeros_half = jnp.zeros((1, HG_WIDTH), F32)
    small_pack = jnp.concatenate([
        dg_attn_pre, dg_attn_post, dg_ffn_pre, dg_ffn_post, _pack_row(dg_hg, dg_sb), _pack_row(d_lb, zeros_half),
        jnp.broadcast_to(loss[0:1, 0:1], (1, D_MODEL)), jnp.zeros((1, D_MODEL), F32)], axis=0)
    in_blocks = flights["in"]
    in_sems, in_thru, in_token = _push_start(
        ("scatter", "gather"),
        (in_blocks, _landing("scatter", in_blocks), small_pack, _landing("gather", small_pack)), token,
        "grads_in_start")
    ffn = _push_wait(("scatter",) * 4, *flights["ffn"], in_token, "grads_ffn_wait")
    sent_out, got_out = _push_wait(("scatter",), *flights["out"], ffn[1], "grads_out_wait")
    sent_in, got_in, my_small, all_smalls = _push_wait(("scatter", "gather"), in_sems, in_thru, got_out, "grads_in_wait")
    all_smalls = lax.dynamic_update_index_in_dim(all_smalls, my_small, me, 0)

    def reduced(sent, got, name):
        return _sum_blocks(lax.dynamic_index_in_dim(sent, me, 0, keepdims=False), got, "grad_sum_" + name)

    sums = (reduced(sent_in, got_in, "w_in"), reduced(sent_out, got_out, "w_out"), reduced(ffn[0], ffn[1], "w_gate_up"),
            reduced(ffn[2], ffn[3], "w_down"), reduced(ffn[4], ffn[5], "ple_proj"), reduced(ffn[6], ffn[7], "ple_gate"))

    out_g, out_d, out_m, out_v = {}, {}, {}, {}
    for i, name in enumerate(names):
        g = sums[i].T if by_columns[i] else sums[i]
        out_g[name] = g[None]
        d, m, v = _adamw(big[i], g, big_m[i], big_v[i], "adamw_" + name)
        out_d[name], out_m[name], out_v[name] = d[None], m[None], v[None]

    def small_rows(pre, gam, hg, sb, post, fpre, fpost):
        return jnp.concatenate([pre, post, fpre, fpost, _pack_row(hg, sb), _pack_row(gam[0], gam[1]),
                                jnp.zeros((2, D_MODEL), F32)], axis=0)

    packs = _small_update(
        all_smalls,
        small_rows(attn_pre_norm, hg_lower_gamma, hg_out_norm, sb_out_norm, attn_post_norm, ffn_pre_norm, ffn_post_norm),
        small_rows(m_attn_pre_norm, m_hg_lower_gamma, m_hg_out_norm, m_sb_out_norm, m_attn_post_norm, m_ffn_pre_norm,
                   m_ffn_post_norm),
        small_rows(v_attn_pre_norm, v_hg_lower_gamma, v_hg_out_norm, v_sb_out_norm, v_attn_post_norm, v_ffn_pre_norm,
                   v_ffn_post_norm))

    def unpack(pk):
        return {
            "attn_pre_norm": pk[ROW_ATTN_PRE:ROW_ATTN_PRE + 1],
            "hg_lower_gamma": pk[ROW_GAMMA].reshape(2, HG_WIDTH),
            "hg_out_norm": pk[ROW_OUT_NORMS:ROW_OUT_NORMS + 1, :HG_WIDTH],
            "sb_out_norm": pk[ROW_OUT_NORMS:ROW_OUT_NORMS + 1, HG_WIDTH:],
            "attn_post_norm": pk[ROW_ATTN_POST:ROW_ATTN_POST + 1],
            "ffn_pre_norm": pk[ROW_FFN_PRE:ROW_FFN_PRE + 1],
            "ffn_post_norm": pk[ROW_FFN_POST:ROW_FFN_POST + 1],
        }

    sg, sd, sm, sv = (unpack(pk) for pk in packs)
    out_g.update(sg), out_d.update(sd), out_m.update(sm), out_v.update(sv)
    order = ("attn_pre_norm", "w_in", "hg_lower_gamma", "hg_out_norm", "sb_out_norm", "w_out", "attn_post_norm",
             "ffn_pre_norm", "w_gate_up", "w_down", "ffn_post_norm", "ple_proj", "ple_gate")
    total_loss = packs[0][ROW_LOSS, 0]
    return (total_loss, dx[None], *[out_g[n] for n in order], *[out_d[n] for n in order],
            *[out_m[n] for n in order], *[out_v[n] for n in order])
```

```python
import jax
import jax.numpy as jnp
from jax import lax
from jax.experimental import pallas as pl
from jax.experimental.pallas import tpu as pltpu

F32 = jnp.float32
BF16 = jnp.bfloat16

D_MODEL = 1024
HG_WIDTH = 512
HG_HEADS = 4
HG_DK = 128
SB_WIDTH = 512
SB_HEADS = 8
SB_DH = 64
SB_BLOCK = 128
SB_PP = 2
SB_TQ = 1024
SB_TK = 256
D_FF = 2816
PLE_DIM = 256
IN_COLS = 4 * HG_WIDTH + 3 * SB_WIDTH
EPS = 1e-6
N_DEV = 8

HG_SUB = 16
HG_TILE = 128
FFN_TF = 1408
ROW_TILE = 256
VMEM_LIMIT = 56 * 1024 * 1024
WGRAD_ACC_BYTES = 8 * 1024 * 1024

ADAM_LR = 0.001
ADAM_B1 = 0.9
ADAM_B2 = 0.999
ADAM_EPS = 1e-08
ADAM_WD = 0.01
ADAM_STEP = 10

NT_DIMS = (((1,), (1,)), ((), ()))
TN_DIMS = (((0,), (0,)), ((), ()))


def _params(semantics):
    return pltpu.CompilerParams(dimension_semantics=semantics, vmem_limit_bytes=VMEM_LIMIT)


def _dot(a, b):
    return jnp.dot(a, b, preferred_element_type=F32)


def _dot_nt(a, b):
    return lax.dot_general(a, b, NT_DIMS, preferred_element_type=F32)


def _dot_tn(a, b):
    return lax.dot_general(a, b, TN_DIMS, preferred_element_type=F32)


def _sigmoid(z):
    return 1.0 / (1.0 + jnp.exp(-z))


def _rstd(xv):
    return lax.rsqrt(jnp.mean(xv * xv, axis=-1, keepdims=True) + EPS)


def _rms_bwd(xv, r, g, dn):
    xn = xv * r
    gh = dn * g
    dx = r * (gh - xn * jnp.mean(gh * xn, axis=-1, keepdims=True))
    return dx, dn * xn


def _colsum(a):
    return jnp.sum(a, axis=0, keepdims=True)


def _row_spec(tm, width, col=0):
    return pl.BlockSpec((tm, width), lambda i, col=col: (i, col))


def _full_spec(shape):
    return pl.BlockSpec(shape, lambda i: (0,) * len(shape))


def _in_proj(x, g, w_t):
    T = x.shape[0]
    tm = 2 * ROW_TILE

    def body(x_ref, g_ref, w_ref, proj_ref, u_ref, qkv_ref):
        xv = x_ref[...]
        u = (xv * _rstd(xv) * g_ref[...]).astype(BF16)
        u_ref[...] = u
        proj = _dot_nt(u, w_ref[...])
        proj_ref[...] = proj
        qkv_ref[...] = proj[:, 4 * HG_WIDTH:].astype(BF16)

    return pl.pallas_call(
        body, grid=(T // tm,), name="in_proj",
        in_specs=[_row_spec(tm, D_MODEL), _full_spec((1, D_MODEL)), _full_spec((IN_COLS, D_MODEL))],
        out_specs=[_row_spec(tm, IN_COLS), _row_spec(tm, D_MODEL), _row_spec(tm, 3 * SB_WIDTH)],
        out_shape=[jax.ShapeDtypeStruct((T, IN_COLS), F32), jax.ShapeDtypeStruct((T, D_MODEL), BF16),
                   jax.ShapeDtypeStruct((T, 3 * SB_WIDTH), BF16)],
        compiler_params=_params(("parallel",)),
    )(x, g, w_t)


def _out_proj(o_hg, proj, o_sb, x, g_hg, g_sb, g_post, w):
    T = x.shape[0]
    tm = 2 * ROW_TILE

    def body(ohg_ref, hg_ref, osb_ref, x_ref, ghg_ref, gsb_ref, gpost_ref, w_ref, cat_ref, mix_ref, h1_ref):
        ohg = ohg_ref[...]
        hg = hg_ref[...]
        osb = osb_ref[...]
        a = ohg * _rstd(ohg) * ghg_ref[...] * (hg * _sigmoid(hg))
        n2 = osb * _rstd(osb) * gsb_ref[...]
        cat = jnp.concatenate([a, n2], axis=1).astype(BF16)
        cat_ref[...] = cat
        mix = _dot(cat, w_ref[...])
        mix_ref[...] = mix
        h1_ref[...] = x_ref[...] + mix * _rstd(mix) * gpost_ref[...]

    return pl.pallas_call(
        body, grid=(T // tm,), name="out_proj",
        in_specs=[_row_spec(tm, HG_WIDTH), _row_spec(tm, HG_WIDTH, 3), _row_spec(tm, SB_WIDTH), _row_spec(tm, D_MODEL),
                  _full_spec((1, HG_WIDTH)), _full_spec((1, SB_WIDTH)), _full_spec((1, D_MODEL)),
                  _full_spec((D_MODEL, D_MODEL))],
        out_specs=[_row_spec(tm, D_MODEL)] * 3,
        out_shape=[jax.ShapeDtypeStruct((T, D_MODEL), BF16), jax.ShapeDtypeStruct((T, D_MODEL), F32),
                   jax.ShapeDtypeStruct((T, D_MODEL), F32)],
        compiler_params=_params(("parallel",)),
    )(o_hg, proj, o_sb, x, g_hg, g_sb, g_post, w)


def _ffn_fwd(h1, g_pre, w_gu_t, w_down, g_post):
    T = h1.shape[0]
    tm = ROW_TILE
    n_f = D_FF // FFN_TF

    def body(h1_ref, gpre_ref, wgu_hbm, wd_hbm, gpost_ref, u2_ref, gu_ref, y_ref, h2_ref, wgu_ref, wd_ref, sems):
        @pl.when(pl.program_id(0) == 0)
        def _():
            fetch = [pltpu.make_async_copy(wgu_hbm, wgu_ref, sems.at[0]), pltpu.make_async_copy(wd_hbm, wd_ref, sems.at[1])]
            for cp in fetch:
                cp.start()
            for cp in fetch:
                cp.wait()

        hv = h1_ref[...]
        u2 = (hv * _rstd(hv) * gpre_ref[...]).astype(BF16)
        u2_ref[...] = u2
        y = jnp.zeros((tm, D_MODEL), F32)
        for j in range(n_f):
            cols = slice(j * FFN_TF, (j + 1) * FFN_TF)
            gate = _dot_nt(u2, wgu_ref[0, cols, :])
            up = _dot_nt(u2, wgu_ref[1, cols, :])
            gu_ref[0, :, cols] = gate.astype(BF16)
            gu_ref[1, :, cols] = up.astype(BF16)
            y = y + _dot((gate * _sigmoid(gate) * up).astype(BF16), wd_ref[cols, :])
        y_ref[...] = y
        h2_ref[...] = hv + y * _rstd(y) * gpost_ref[...]

    row = lambda width: pl.BlockSpec((tm, width), lambda i: (i, 0))
    vec = pl.BlockSpec((1, D_MODEL), lambda i: (0, 0))
    hbm = pl.BlockSpec(memory_space=pl.ANY)
    return pl.pallas_call(
        body, grid=(T // tm,), name="ffn_fwd",
        in_specs=[row(D_MODEL), vec, hbm, hbm, vec],
        out_specs=[row(D_MODEL), pl.BlockSpec((2, tm, D_FF), lambda i: (0, i, 0)), row(D_MODEL), row(D_MODEL)],
        out_shape=[jax.ShapeDtypeStruct((T, D_MODEL), BF16), jax.ShapeDtypeStruct((2, T, D_FF), BF16),
                   jax.ShapeDtypeStruct((T, D_MODEL), F32), jax.ShapeDtypeStruct((T, D_MODEL), F32)],
        scratch_shapes=[pltpu.VMEM((2, D_FF, D_MODEL), BF16), pltpu.VMEM((D_FF, D_MODEL), BF16),
                        pltpu.SemaphoreType.DMA((2,))],
        compiler_params=_params(("arbitrary",)),
    )(h1, g_pre, w_gu_t, w_down, g_post)


def _ple_loss(h2, p, target, w_pp_t, w_pg):
    T = h2.shape[0]
    tm = 2 * ROW_TILE

    def body(h2_ref, p_ref, t_ref, wpp_ref, wpg_ref, dh2_ref, de_ref, dg_ref, loss_ref):
        i = pl.program_id(0)
        h2v = h2_ref[...]
        e = _dot_nt(p_ref[...].astype(BF16), wpp_ref[...])
        sg = _sigmoid(_dot(h2v.astype(BF16), wpg_ref[...]))
        diff = h2v + e * sg - t_ref[...]
        part = jnp.sum(jnp.sum(diff * diff, axis=1, keepdims=True), axis=0, keepdims=True) * (0.5 / D_MODEL)

        @pl.when(i == 0)
        def _():
            loss_ref[...] = jnp.zeros_like(loss_ref)

        loss_ref[...] += jnp.broadcast_to(part, loss_ref.shape)
        dh3 = diff * (1.0 / D_MODEL)
        de_ref[...] = (dh3 * sg).astype(BF16)
        dg = (dh3 * e * sg * (1.0 - sg)).astype(BF16)
        dg_ref[...] = dg
        dh2_ref[...] = dh3 + _dot_nt(dg, wpg_ref[...])

    return pl.pallas_call(
        body, grid=(T // tm,), name="ple_loss",
        in_specs=[_row_spec(tm, D_MODEL), _row_spec(tm, PLE_DIM), _row_spec(tm, D_MODEL),
                  _full_spec((D_MODEL, PLE_DIM)), _full_spec((D_MODEL, D_MODEL))],
        out_specs=[_row_spec(tm, D_MODEL)] * 3 + [_full_spec((8, 128))],
        out_shape=[jax.ShapeDtypeStruct((T, D_MODEL), F32), jax.ShapeDtypeStruct((T, D_MODEL), BF16),
                   jax.ShapeDtypeStruct((T, D_MODEL), BF16), jax.ShapeDtypeStruct((8, 128), F32)],
        compiler_params=_params(("arbitrary",)),
    )(h2, p, target, w_pp_t, w_pg)


def _ffn_bwd(dh2, y, h1, gu, g_post, g_pre, w_gu_t, w_down):
    T = h1.shape[0]
    tm = ROW_TILE
    n_f = D_FF // FFN_TF

    def body(dh2_ref, y_ref, h1_ref, gu_ref, gpost_ref, gpre_ref, wgu_hbm, wd_hbm,
             dy_ref, a_ref, dgu_ref, dh1_ref, dgpost_ref, dgpre_ref, wgu_ref, wd_ref, sems):
        i = pl.program_id(0)

        @pl.when(i == 0)
        def _():
            fetch = [pltpu.make_async_copy(wgu_hbm, wgu_ref, sems.at[0]), pltpu.make_async_copy(wd_hbm, wd_ref, sems.at[1])]
            for cp in fetch:
                cp.start()
            dgpost_ref[...] = jnp.zeros_like(dgpost_ref)
            dgpre_ref[...] = jnp.zeros_like(dgpre_ref)
            for cp in fetch:
                cp.wait()

        yv = y_ref[...]
        dh2 = dh2_ref[...]
        dy, gterm = _rms_bwd(yv, _rstd(yv), gpost_ref[...], dh2)
        dgpost_ref[...] += _colsum(gterm)
        dy = dy.astype(BF16)
        dy_ref[...] = dy
        du2 = jnp.zeros((tm, D_MODEL), F32)
        for j in range(n_f):
            cols = slice(j * FFN_TF, (j + 1) * FFN_TF)
            da = _dot_nt(dy, wd_ref[cols, :])
            gate = gu_ref[0, :, cols].astype(F32)
            up = gu_ref[1, :, cols].astype(F32)
            s = _sigmoid(gate)
            sil = gate * s
            a_ref[:, cols] = (sil * up).astype(BF16)
            dgate = (da * up * (s * (1.0 + gate * (1.0 - s)))).astype(BF16)
            dup = (da * sil).astype(BF16)
            dgu_ref[0, :, cols] = dgate
            dgu_ref[1, :, cols] = dup
            du2 = du2 + _dot(dgate, wgu_ref[0, cols, :]) + _dot(dup, wgu_ref[1, cols, :])
        hv = h1_ref[...]
        dx, gterm = _rms_bwd(hv, _rstd(hv), gpre_ref[...], du2)
        dh1_ref[...] = dh2 + dx
        dgpre_ref[...] += _colsum(gterm)

    row = lambda width: pl.BlockSpec((tm, width), lambda i: (i, 0))
    both = pl.BlockSpec((2, tm, D_FF), lambda i: (0, i, 0))
    vec = pl.BlockSpec((1, D_MODEL), lambda i: (0, 0))
    hbm = pl.BlockSpec(memory_space=pl.ANY)
    return pl.pallas_call(
        body, grid=(T // tm,), name="ffn_bwd",
        in_specs=[row(D_MODEL), row(D_MODEL), row(D_MODEL), both, vec, vec, hbm, hbm],
        out_specs=[row(D_MODEL), row(D_FF), both, row(D_MODEL), vec, vec],
        out_shape=[jax.ShapeDtypeStruct((T, D_MODEL), BF16), jax.ShapeDtypeStruct((T, D_FF), BF16),
                   jax.ShapeDtypeStruct((2, T, D_FF), BF16),
                   jax.ShapeDtypeStruct((T, D_MODEL), F32), jax.ShapeDtypeStruct((1, D_MODEL), F32),
                   jax.ShapeDtypeStruct((1, D_MODEL), F32)],
        scratch_shapes=[pltpu.VMEM((2, D_FF, D_MODEL), BF16), pltpu.VMEM((D_FF, D_MODEL), BF16),
                        pltpu.SemaphoreType.DMA((2,))],
        compiler_params=_params(("arbitrary",)),
    )(dh2, y, h1, gu, g_post, g_pre, w_gu_t, w_down)


def _out_bwd(dh1, mix, o_hg, proj, o_sb, g_hg, g_sb, g_post, w, after):
    T = dh1.shape[0]
    tm = ROW_TILE

    def body(dh1_ref, mix_ref, ohg_ref, hg_ref, osb_ref, ghg_ref, gsb_ref, gpost_ref, w_ref, after_ref,
             dmix_ref, dohg_ref, dhg_ref, dosb_ref, dgpost_ref, dghg_ref, dgsb_ref):
        i = pl.program_id(0)

        @pl.when(i == 0)
        def _():
            dgpost_ref[...] = jnp.zeros_like(dgpost_ref)
            dghg_ref[...] = jnp.zeros_like(dghg_ref)
            dgsb_ref[...] = jnp.zeros_like(dgsb_ref)

        mix = mix_ref[...]
        dmix, gterm = _rms_bwd(mix, _rstd(mix), gpost_ref[...], dh1_ref[...])
        dgpost_ref[...] += _colsum(gterm)
        dmix = dmix.astype(BF16)
        dmix_ref[...] = dmix
        dcat = _dot_nt(dmix, w_ref[...])
        da = dcat[:, :HG_WIDTH]
        dn2 = dcat[:, HG_WIDTH:]
        ohg = ohg_ref[...]
        r1 = _rstd(ohg)
        hg = hg_ref[...]
        s = _sigmoid(hg)
        dhg_ref[...] = da * (ohg * r1 * ghg_ref[...]) * (s * (1.0 + hg * (1.0 - s)))
        dohg, gterm = _rms_bwd(ohg, r1, ghg_ref[...], da * (hg * s))
        dohg_ref[...] = dohg
        dghg_ref[...] += _colsum(gterm)
        osb = osb_ref[...]
        dosb, gterm = _rms_bwd(osb, _rstd(osb), gsb_ref[...], dn2)
        dosb_ref[...] = dosb
        dgsb_ref[...] += _colsum(gterm)

    return pl.pallas_call(
        body, grid=(T // tm,), name="out_bwd",
        in_specs=[_row_spec(tm, D_MODEL), _row_spec(tm, D_MODEL), _row_spec(tm, HG_WIDTH), _row_spec(tm, HG_WIDTH, 3),
                  _row_spec(tm, SB_WIDTH), _full_spec((1, HG_WIDTH)), _full_spec((1, SB_WIDTH)),
                  _full_spec((1, D_MODEL)), _full_spec((D_MODEL, D_MODEL)), pl.BlockSpec(memory_space=pl.ANY)],
        out_specs=[_row_spec(tm, D_MODEL), _row_spec(tm, HG_WIDTH), _row_spec(tm, HG_WIDTH), _row_spec(tm, SB_WIDTH),
                   _full_spec((1, D_MODEL)), _full_spec((1, HG_WIDTH)), _full_spec((1, SB_WIDTH))],
        out_shape=[jax.ShapeDtypeStruct((T, D_MODEL), BF16), jax.ShapeDtypeStruct((T, HG_WIDTH), F32),
                   jax.ShapeDtypeStruct((T, HG_WIDTH), F32), jax.ShapeDtypeStruct((T, SB_WIDTH), F32),
                   jax.ShapeDtypeStruct((1, D_MODEL), F32), jax.ShapeDtypeStruct((1, HG_WIDTH), F32),
                   jax.ShapeDtypeStruct((1, SB_WIDTH), F32)],
        compiler_params=_params(("arbitrary",)),
    )(dh1, mix, o_hg, proj, o_sb, g_hg, g_sb, g_post, w, after)


def _in_bwd(d_hgrn, d_hg, d_sq, d_sk, d_sv, x, dh1, g_pre, w_t):
    T = x.shape[0]
    tm = ROW_TILE

    def body(dh_ref, dhg_ref, dsq_ref, dsk_ref, dsv_ref, x_ref, dh1_ref, gpre_ref, w_ref, dproj_ref, dx_ref, dgpre_ref):
        i = pl.program_id(0)

        @pl.when(i == 0)
        def _():
            dgpre_ref[...] = jnp.zeros_like(dgpre_ref)

        dp = jnp.concatenate([dh_ref[...], dhg_ref[...], dsq_ref[...], dsk_ref[...], dsv_ref[...]], axis=1).astype(BF16)
        dproj_ref[...] = dp
        du = _dot(dp, w_ref[...])
        xv = x_ref[...]
        dx, gterm = _rms_bwd(xv, _rstd(xv), gpre_ref[...], du)
        dx_ref[...] = dh1_ref[...] + dx
        dgpre_ref[...] += _colsum(gterm)

    return pl.pallas_call(
        body, grid=(T // tm,), name="in_bwd",
        in_specs=[_row_spec(tm, 3 * HG_WIDTH), _row_spec(tm, HG_WIDTH), _row_spec(tm, SB_WIDTH), _row_spec(tm, SB_WIDTH),
                  _row_spec(tm, SB_WIDTH), _row_spec(tm, D_MODEL), _row_spec(tm, D_MODEL), _full_spec((1, D_MODEL)),
                  _full_spec((IN_COLS, D_MODEL))],
        out_specs=[_row_spec(tm, IN_COLS), _row_spec(tm, D_MODEL), _full_spec((1, D_MODEL))],
        out_shape=[jax.ShapeDtypeStruct((T, IN_COLS), BF16), jax.ShapeDtypeStruct((T, D_MODEL), F32),
                   jax.ShapeDtypeStruct((1, D_MODEL), F32)],
        compiler_params=_params(("arbitrary",)),
    )(d_hgrn, d_hg, d_sq, d_sk, d_sv, x, dh1, g_pre, w_t)


def _wgrad(a, b, name):
    stacked = a.ndim == 3
    S, T, K = a.shape if stacked else (1,) + a.shape
    N = b.shape[1]
    tk = K
    while tk * N * 4 > WGRAD_ACC_BYTES:
        tk //= 2
    assert K % tk == 0 and tk % 128 == 0
    tt = 1024 if T % 1024 == 0 else 512
    n_t = T // tt

    def body(a_ref, b_ref, o_ref, acc_ref):
        t = pl.program_id(2)

        @pl.when(t == 0)
        def _():
            acc_ref[...] = jnp.zeros_like(acc_ref)

        acc_ref[...] += _dot_tn(a_ref[...].astype(BF16), b_ref[...].astype(BF16))

        @pl.when(t == n_t - 1)
        def _():
            o_ref[...] = acc_ref[...].astype(BF16)

    if stacked:
        a_spec = pl.BlockSpec((None, tt, tk), lambda s, k, t: (s, t, k))
        o_spec = pl.BlockSpec((None, tk, N), lambda s, k, t: (s, k, 0))
        o_shape = (S, K, N)
    else:
        a_spec = pl.BlockSpec((tt, tk), lambda s, k, t: (t, k))
        o_spec = pl.BlockSpec((tk, N), lambda s, k, t: (k, 0))
        o_shape = (K, N)
    return pl.pallas_call(
        body, grid=(S, K // tk, n_t), name=name,
        in_specs=[a_spec, pl.BlockSpec((tt, N), lambda s, k, t: (t, 0))],
        out_specs=o_spec,
        out_shape=jax.ShapeDtypeStruct(o_shape, BF16),
        scratch_shapes=[pltpu.VMEM((tk, N), F32)],
        compiler_params=_params(("parallel", "parallel", "arbitrary")),
    )(a, b)


def _hg_gates(hq, hf, gam):
    g0 = gam[0:1, :]
    g1 = gam[1:2, :]
    mx = jnp.maximum(g0, g1)
    e0 = jnp.exp(g0 - mx)
    e1 = jnp.exp(g1 - mx)
    lb = e0 / (e0 + e1)
    s = _sigmoid(hf)
    f = lb + (1.0 - lb) * s
    sq = _sigmoid(hq)
    return hq * sq, sq, s, f, (1.0 - lb) * (1.0 - s), jnp.log(f), lb


def _prefix_in_sub(a, rowmod):
    n = a.shape[0]
    sh = 1
    while sh < HG_SUB:
        a = a + jnp.where(rowmod >= sh, pltpu.roll(a, sh, axis=0), 0.0)
        sh *= 2
    return a


def _suffix_in_sub(a, rowmod):
    n = a.shape[0]
    sh = 1
    while sh < HG_SUB:
        a = a + jnp.where(rowmod < HG_SUB - sh, pltpu.roll(a, n - sh, axis=0), 0.0)
        sh *= 2
    return a


def _hg_fwd(proj, gam):
    T = proj.shape[0]
    tb = HG_TILE
    n_sub = tb // HG_SUB

    def body(hq_ref, hf_ref, hi_ref, gam_ref, o_ref, hist_ref, st_ref, qt_ref, kk_ref, b_ref, od_ref):
        i = pl.program_id(0)

        @pl.when(i == 0)
        def _():
            st_ref[...] = jnp.zeros_like(st_ref)

        rowmod_w = lax.broadcasted_iota(jnp.int32, (tb, HG_WIDTH), 0) % HG_SUB
        rowmod = lax.broadcasted_iota(jnp.int32, (tb, 1), 0) % HG_SUB
        q, _, _, f, kk, lf, _ = _hg_gates(hq_ref[...], hf_ref[...], gam_ref[...])
        b = _prefix_in_sub(lf, rowmod_w)
        qt_ref[...] = (q * jnp.exp(b)).astype(BF16)
        kk_ref[...] = kk
        b_ref[...] = b

        for h in range(HG_HEADS):
            cs = slice(h * HG_DK, (h + 1) * HG_DK)
            qh = q[:, cs]
            kh = kk[:, cs]
            fh = f[:, cs]
            vh = hi_ref[:, cs]
            acc = jnp.sum(qh * kh, axis=1, keepdims=True) * vh
            decay = fh
            for d in range(1, HG_SUB):
                if d > 1:
                    decay = decay * pltpu.roll(fh, d - 1, axis=0)
                e = qh * pltpu.roll(kh, d, axis=0) * decay
                sc = jnp.where(rowmod >= d, jnp.sum(e, axis=1, keepdims=True), 0.0)
                acc = acc + sc * pltpu.roll(vh, d, axis=0)
            od_ref[:, cs] = acc

        for m in range(n_sub):
            rs = slice(m * HG_SUB, (m + 1) * HG_SUB)
            for h in range(HG_HEADS):
                cs = slice(h * HG_DK, (h + 1) * HG_DK)
                st = st_ref[h]
                hist_ref[m, h] = st
                o_int = _dot_nt(qt_ref[rs, cs], st.astype(BF16))
                bs = b_ref[rs, cs]
                bl = bs[HG_SUB - 1:HG_SUB, :]
                kt = (kk_ref[rs, cs] * jnp.exp(bl - bs)).astype(BF16)
                st_ref[h] = st * jnp.exp(bl) + _dot_tn(hi_ref[rs, cs].astype(BF16), kt)
                o_ref[rs, cs] = o_int + od_ref[rs, cs]

    col = lambda c: pl.BlockSpec((tb, HG_WIDTH), lambda i, c=c: (i, c))
    return pl.pallas_call(
        body, grid=(T // tb,), name="hg_fwd",
        in_specs=[col(0), col(1), col(2), _full_spec((2, HG_WIDTH))],
        out_specs=[pl.BlockSpec((tb, HG_WIDTH), lambda i: (i, 0)),
                   pl.BlockSpec((n_sub, HG_HEADS, HG_DK, HG_DK), lambda i: (i, 0, 0, 0))],
        out_shape=[jax.ShapeDtypeStruct((T, HG_WIDTH), F32),
                   jax.ShapeDtypeStruct((T // HG_SUB, HG_HEADS, HG_DK, HG_DK), F32)],
        scratch_shapes=[pltpu.VMEM((HG_HEADS, HG_DK, HG_DK), F32), pltpu.VMEM((tb, HG_WIDTH), BF16),
                        pltpu.VMEM((tb, HG_WIDTH), F32), pltpu.VMEM((tb, HG_WIDTH), F32),
                        pltpu.VMEM((tb, HG_WIDTH), F32)],
        compiler_params=_params(("arbitrary",)),
    )(proj, proj, proj, gam)


def _hg_bwd(proj, gam, d_o, hist):
    T = proj.shape[0]
    tb = HG_TILE
    n_sub = tb // HG_SUB
    n_t = T // tb

    def body(hq_ref, hf_ref, hi_ref, gam_ref, do_ref, hist_ref, dout_ref, dlb_ref,
             dst_ref, q_ref, kk_ref, b_ref, dq_ref, dkk_ref, db_ref, dv_ref):
        i = pl.program_id(0)

        @pl.when(i == 0)
        def _():
            dst_ref[...] = jnp.zeros_like(dst_ref)
            dlb_ref[...] = jnp.zeros_like(dlb_ref)

        rowmod_w = lax.broadcasted_iota(jnp.int32, (tb, HG_WIDTH), 0) % HG_SUB
        rowmod = lax.broadcasted_iota(jnp.int32, (tb, 1), 0) % HG_SUB
        last_row = lax.broadcasted_iota(jnp.int32, (HG_SUB, 1), 0) == HG_SUB - 1
        hq = hq_ref[...]
        q, sq, s, f, kk, lf, lb = _hg_gates(hq, hf_ref[...], gam_ref[...])
        b = _prefix_in_sub(lf, rowmod_w)
        q_ref[...] = q
        kk_ref[...] = kk
        b_ref[...] = b

        for m in reversed(range(n_sub)):
            rs = slice(m * HG_SUB, (m + 1) * HG_SUB)
            for h in range(HG_HEADS):
                cs = slice(h * HG_DK, (h + 1) * HG_DK)
                st = hist_ref[m, h]
                dst = dst_ref[h]
                dstb = dst.astype(BF16)
                dos = do_ref[rs, cs].astype(BF16)
                vs = hi_ref[rs, cs].astype(BF16)
                bs = b_ref[rs, cs]
                bl = bs[HG_SUB - 1:HG_SUB, :]
                ebl = jnp.exp(bl)
                qt = q_ref[rs, cs] * jnp.exp(bs)
                decay = jnp.exp(bl - bs)
                kt = kk_ref[rs, cs] * decay
                dqt = _dot(dos, st.astype(BF16))
                dkt = _dot(vs, dstb)
                dv_ref[rs, cs] = _dot_nt(kt.astype(BF16), dstb)
                dst_ref[h] = dst * ebl + _dot_tn(dos, qt.astype(BF16))
                dq_ref[rs, cs] = dqt * jnp.exp(bs)
                dkk_ref[rs, cs] = dkt * decay
                dktk = dkt * kt
                dbl = _colsum(dktk) + _colsum(dst * st) * ebl
                db_ref[rs, cs] = dqt * qt - dktk + jnp.where(last_row, dbl, 0.0)

        for h in range(HG_HEADS):
            cs = slice(h * HG_DK, (h + 1) * HG_DK)
            qh = q[:, cs]
            kh = kk[:, cs]
            fh = f[:, cs]
            vh = hi_ref[:, cs]
            doh = do_ref[:, cs]
            dsc = jnp.sum(doh * vh, axis=1, keepdims=True)
            sc = jnp.sum(qh * kh, axis=1, keepdims=True)
            dqh = dsc * kh
            dkh = dsc * qh
            dvh = sc * doh
            dbh = jnp.zeros_like(qh)
            decay = fh
            for d in range(1, HG_SUB):
                back = tb - d
                valid = rowmod >= d
                if d > 1:
                    decay = decay * pltpu.roll(fh, d - 1, axis=0)
                kd = pltpu.roll(kh, d, axis=0)
                ex = jnp.where(valid, decay, 0.0)
                pd = kd * ex
                sc = jnp.sum(qh * pd, axis=1, keepdims=True)
                dsc = jnp.where(valid, jnp.sum(doh * pltpu.roll(vh, d, axis=0), axis=1, keepdims=True), 0.0)
                dqh = dqh + dsc * pd
                dsq = dsc * qh
                w = dsq * pd
                dkh = dkh + pltpu.roll(dsq * ex, back, axis=0)
                dvh = dvh + pltpu.roll(sc * doh, back, axis=0)
                dbh = dbh + w - pltpu.roll(w, back, axis=0)
            dq_ref[:, cs] += dqh
            dkk_ref[:, cs] += dkh
            dv_ref[:, cs] += dvh
            db_ref[:, cs] += dbh

        dlf = _suffix_in_sub(db_ref[...], rowmod_w)
        df = dlf / f - dkk_ref[...]
        dout_ref[:, 0:HG_WIDTH] = dq_ref[...] * (sq * (1.0 + hq * (1.0 - sq)))
        dout_ref[:, HG_WIDTH:2 * HG_WIDTH] = df * (1.0 - lb) * (s * (1.0 - s))
        dout_ref[:, 2 * HG_WIDTH:3 * HG_WIDTH] = dv_ref[...]
        dlb_ref[...] += _colsum(df * (1.0 - s))

    col = lambda c: pl.BlockSpec((tb, HG_WIDTH), lambda i, c=c: (n_t - 1 - i, c))
    return pl.pallas_call(
        body, grid=(n_t,), name="hg_bwd",
        in_specs=[col(0), col(1), col(2), _full_spec((2, HG_WIDTH)),
                  pl.BlockSpec((tb, HG_WIDTH), lambda i: (n_t - 1 - i, 0)),
                  pl.BlockSpec((n_sub, HG_HEADS, HG_DK, HG_DK), lambda i: (n_t - 1 - i, 0, 0, 0))],
        out_specs=[pl.BlockSpec((tb, 3 * HG_WIDTH), lambda i: (n_t - 1 - i, 0)), _full_spec((1, HG_WIDTH))],
        out_shape=[jax.ShapeDtypeStruct((T, 3 * HG_WIDTH), F32), jax.ShapeDtypeStruct((1, HG_WIDTH), F32)],
        scratch_shapes=[pltpu.VMEM((HG_HEADS, HG_DK, HG_DK), F32)] + [pltpu.VMEM((tb, HG_WIDTH), F32)] * 7,
        compiler_params=_params(("arbitrary",)),
    )(proj, proj, proj, gam, d_o, hist)


def _sb_masks():
    row = lax.broadcasted_iota(jnp.int32, (SB_TK, SB_TK), 0)
    col = lax.broadcasted_iota(jnp.int32, (SB_TK, SB_TK), 1)
    suffix = (row >= col).astype(BF16)
    prefix = (row <= col).astype(BF16)
    query = lax.broadcasted_iota(jnp.int32, (SB_TQ, SB_TK), 0)
    key = lax.broadcasted_iota(jnp.int32, (SB_TQ, SB_TK), 1)
    causal = [key + r * SB_TK < query for r in range(SB_TQ // SB_TK)]
    lane = lax.broadcasted_iota(jnp.int32, (SB_TQ, SB_BLOCK), 1)
    return suffix, prefix, causal, lane


def _sum_right(a, suffix):
    return _dot(a.astype(BF16), suffix)


def _per_lane_tile(t, c, op):
    return jnp.concatenate([op(t[:, n:n + SB_BLOCK], c) for n in range(0, t.shape[1], SB_BLOCK)], axis=1)


def _sb_block(qb, kj, suffix, causal, c):
    z = _dot_nt(qb, kj)
    sp = jnp.maximum(z, 0.0) + jnp.log(1.0 + jnp.exp(-jnp.abs(z)))
    if causal is not None:
        sp = jnp.where(causal, sp, 0.0)
    big_l = _sum_right(sp, suffix)
    a = jnp.exp(_per_lane_tile(z - big_l, c, jnp.subtract))
    if causal is not None:
        a = jnp.where(causal, a, 0.0)
    return z, a, big_l


def _sb_tile_copies(to_hbm, bufs, stores, sems, slot, head, i, j, n_chain):
    copies = []
    for ch in range(n_chain):
        for w in range(2):
            vmem, hbm = bufs[w].at[slot, ch], stores[w].at[head + ch, i, j]
            src, dst = (vmem, hbm) if to_hbm else (hbm, vmem)
            copies.append(pltpu.make_async_copy(src, dst, sems.at[slot, ch, w]))
    return copies


def _half_masked(pair, lane, scale=1.0):
    pair = pair.astype(F32) * scale
    return jnp.where(lane < SB_DH, pair, 0.0).astype(BF16), jnp.where(lane < SB_DH, 0.0, pair).astype(BF16)


def _sb_fwd(qkv):
    T = qkv.shape[0]
    width = SB_PP * SB_BLOCK
    n_chain = 2 * SB_PP
    n_sub = SB_TQ // SB_TK

    def body(q_ref, k_ref, v_ref, o_ref, a_st, s_st, acc_ref, c_ref, qm, a_buf, s_buf, sems):
        g = pl.program_id(0)
        i = pl.program_id(1)
        suffix, _, causal, lane = _sb_masks()
        for pp in range(SB_PP):
            qm[2 * pp], qm[2 * pp + 1] = _half_masked(q_ref[:, pp * SB_BLOCK:(pp + 1) * SB_BLOCK], lane, SB_DH ** -0.5)
        acc_ref[...] = jnp.zeros_like(acc_ref)
        c_ref[...] = jnp.zeros_like(c_ref)
        a_buf[...] = jnp.zeros_like(a_buf)
        s_buf[...] = jnp.zeros_like(s_buf)

        def copies(slot, j):
            return _sb_tile_copies(True, (a_buf, s_buf), (a_st, s_st), sems, slot, g * n_chain, i, j, n_chain)

        def step(j, slot, r=None, reuse=True):
            rows = pl.ds(pl.multiple_of(j * SB_TK, SB_TK), SB_TK)
            qs = slice(0 if r is None else r * SB_TK, SB_TQ)
            mask = None if r is None else causal[r][qs]
            if reuse:
                for cp in copies(slot, j):
                    cp.wait()
            for ch in range(n_chain):
                pair = slice((ch // 2) * SB_BLOCK, (ch // 2 + 1) * SB_BLOCK)
                c = c_ref[ch, qs]
                z, a, big_l = _sb_block(qm[ch, qs], k_ref[rows, pair], suffix, mask, c)
                ab = a.astype(BF16)
                acc_ref[ch, qs] += _dot(ab, v_ref[rows, pair])
                a_buf[slot, ch, qs] = ab
                s_buf[slot, ch, qs] = z.astype(BF16)
                c_ref[ch, qs] = c + jnp.broadcast_to(big_l[:, 0:1], c.shape)
            for cp in copies(slot, j):
                cp.start()

        first = i * n_sub
        for s, r in enumerate(reversed(range(n_sub))):
            step(first + r, s % 2, r, reuse=s >= 2)

        @pl.loop(0, first)
        def _(t):
            step(first - 1 - t, (n_sub + t) % 2)

        for slot in range(2):
            for cp in copies(slot, 0):
                cp.wait()
        for pp in range(SB_PP):
            o_ref[:, pp * SB_BLOCK:(pp + 1) * SB_BLOCK] = jnp.where(lane < SB_DH, acc_ref[2 * pp], acc_ref[2 * pp + 1])

    assert n_sub >= 2 and n_sub % 2 == 0
    n_g = SB_WIDTH // width
    blk = lambda part: pl.BlockSpec((SB_TQ, width), lambda g, i, part=part: (i, part * n_g + g))
    whole = lambda part: pl.BlockSpec((T, width), lambda g, i, part=part: (0, part * n_g + g))
    tiles = jax.ShapeDtypeStruct((SB_HEADS, T // SB_TQ, T // SB_TK, SB_TQ, SB_TK), BF16)
    return pl.pallas_call(
        body, grid=(n_g, T // SB_TQ), name="sb_fwd",
        in_specs=[blk(0), whole(1), whole(2)],
        out_specs=[blk(0), pl.BlockSpec(memory_space=pl.ANY), pl.BlockSpec(memory_space=pl.ANY)],
        out_shape=[jax.ShapeDtypeStruct((T, SB_WIDTH), F32), tiles, tiles],
        scratch_shapes=[pltpu.VMEM((n_chain, SB_TQ, SB_BLOCK), F32), pltpu.VMEM((n_chain, SB_TQ, SB_BLOCK), F32),
                        pltpu.VMEM((n_chain, SB_TQ, SB_BLOCK), BF16),
                        pltpu.VMEM((2, n_chain, SB_TQ, SB_TK), BF16), pltpu.VMEM((2, n_chain, SB_TQ, SB_TK), BF16),
                        pltpu.SemaphoreType.DMA((2, n_chain, 2))],
        compiler_params=_params(("parallel", "arbitrary")),
    )(qkv, qkv, qkv)


def _sb_bwd(qkv, a_st, s_st, d_o, after):
    T = qkv.shape[0]
    width = SB_PP * SB_BLOCK
    n_chain = 2 * SB_PP
    n_sub = SB_TQ // SB_TK
    scale = SB_DH ** -0.5

    def body(q_ref, k_ref, v_ref, a_st_ref, s_st_ref, do_ref, after_ref, dq_ref, dk_ref, dv_ref, acc_ref, gc_ref, qm, dom,
             a_buf, s_buf, sems):
        g_idx = pl.program_id(0)
        i = pl.program_id(1)

        @pl.when(i == 0)
        def _():
            dk_ref[...] = jnp.zeros_like(dk_ref)
            dv_ref[...] = jnp.zeros_like(dv_ref)

        _, prefix, causal, lane = _sb_masks()
        for pp in range(SB_PP):
            pair = slice(pp * SB_BLOCK, (pp + 1) * SB_BLOCK)
            qm[2 * pp], qm[2 * pp + 1] = _half_masked(q_ref[:, pair], lane, scale)
            dom[2 * pp], dom[2 * pp + 1] = _half_masked(do_ref[:, pair], lane)
        acc_ref[...] = jnp.zeros_like(acc_ref)
        gc_ref[...] = jnp.zeros_like(gc_ref)

        def copies(slot, j):
            return _sb_tile_copies(False, (a_buf, s_buf), (a_st_ref, s_st_ref), sems, slot, g_idx * n_chain, i, j, n_chain)

        def step(j, slot, r=None, last=False):
            rows = pl.ds(pl.multiple_of(j * SB_TK, SB_TK), SB_TK)
            qs = slice(0 if r is None else r * SB_TK, SB_TQ)
            mask = None if r is None else causal[r][qs]
            for cp in copies(slot, j):
                cp.wait()
            if not last:
                for cp in copies(1 - slot, j + 1):
                    cp.start()
            for pp in range(SB_PP):
                pair = slice(pp * SB_BLOCK, (pp + 1) * SB_BLOCK)
                kj = k_ref[rows, pair]
                vj = v_ref[rows, pair]
                dk = jnp.zeros((SB_TK, SB_BLOCK), F32)
                dv = jnp.zeros((SB_TK, SB_BLOCK), F32)
                for ch in (2 * pp, 2 * pp + 1):
                    ab = a_buf[slot, ch, qs]
                    g = ab.astype(F32) * _dot_nt(dom[ch, qs], vj)
                    g_left = _sum_right(g, prefix)
                    gc = gc_ref[ch, qs]
                    dz = g - _sigmoid(s_buf[slot, ch, qs].astype(F32)) * _per_lane_tile(g_left, gc, jnp.add)
                    if mask is not None:
                        dz = jnp.where(mask, dz, 0.0)
                    dzb = dz.astype(BF16)
                    dk = dk + _dot_tn(dzb, qm[ch, qs])
                    dv = dv + _dot_tn(ab, dom[ch, qs])
                    acc_ref[ch, qs] += _dot(dzb, kj)
                    gc_ref[ch, qs] = gc + jnp.broadcast_to(g_left[:, SB_TK - 1:SB_TK], gc.shape)
                dk_ref[rows, pair] += dk
                dv_ref[rows, pair] += dv

        first = i * n_sub
        for cp in copies(0, 0):
            cp.start()

        @pl.loop(0, first)
        def _(j):
            step(j, j % 2)

        for r in range(n_sub):
            step(first + r, r % 2, r, last=r == n_sub - 1)
        for pp in range(SB_PP):
            dq_ref[:, pp * SB_BLOCK:(pp + 1) * SB_BLOCK] = scale * jnp.where(lane < SB_DH, acc_ref[2 * pp],
                                                                             acc_ref[2 * pp + 1])
    n_g = SB_WIDTH // width
    blk = lambda part: pl.BlockSpec((SB_TQ, width), lambda g, i, part=part: (i, part * n_g + g))
    whole = lambda part: pl.BlockSpec((T, width), lambda g, i, part=part: (0, part * n_g + g))
    return pl.pallas_call(
        body, grid=(n_g, T // SB_TQ), name="sb_bwd",
        in_specs=[blk(0), whole(1), whole(2), pl.BlockSpec(memory_space=pl.ANY), pl.BlockSpec(memory_space=pl.ANY), blk(0),
                  pl.BlockSpec(memory_space=pl.ANY)],
        out_specs=[blk(0), whole(0), whole(0)],
        out_shape=[jax.ShapeDtypeStruct((T, SB_WIDTH), F32)] * 3,
        scratch_shapes=[pltpu.VMEM((n_chain, SB_TQ, SB_BLOCK), F32), pltpu.VMEM((n_chain, SB_TQ, SB_BLOCK), F32),
                        pltpu.VMEM((n_chain, SB_TQ, SB_BLOCK), BF16), pltpu.VMEM((n_chain, SB_TQ, SB_BLOCK), BF16),
                        pltpu.VMEM((2, n_chain, SB_TQ, SB_TK), BF16), pltpu.VMEM((2, n_chain, SB_TQ, SB_TK), BF16),
                        pltpu.SemaphoreType.DMA((2, n_chain, 2))],
        compiler_params=_params(("parallel", "arbitrary")),
    )(qkv, qkv, qkv, a_st, s_st, d_o, after)


HBM = pl.BlockSpec(memory_space=pltpu.HBM)
MESH = pl.DeviceIdType.MESH


def _place():
    return lax.axis_index("x"), lax.axis_index("y"), lax.axis_index("c")


def _all_gather(block):
    rows, cols = block.shape

    def body(x_ref, out_ref, send_sems, recv_sems, local_sem):
        x, y, c = _place()
        me, sibling = (x, y, c), (x, y, 1 - c)
        chips = [(1 - x, y), (x, 1 - y), (1 - x, 1 - y)]

        def slot(px, py, pc):
            return out_ref.at[4 * px + 2 * py + pc]

        def copy(k, blk, to, src=None):
            return pltpu.make_async_remote_copy(
                src_ref=slot(*blk) if src is None else src, dst_ref=slot(*blk),
                send_sem=send_sems.at[k], recv_sem=recv_sems.at[k], device_id=to, device_id_type=MESH)

        mine = pltpu.make_async_copy(x_ref, slot(*me), local_sem)
        mine.start()
        first = [copy(0, me, sibling, src=x_ref)]
        first += [copy(1 + j, me, (*chip, c), src=x_ref) for j, chip in enumerate(chips)]
        for cp in first:
            cp.start()
        passed = [copy(4 + j, (*chip, c), sibling) for j, chip in enumerate(chips)]
        for j, chip in enumerate(chips):
            copy(1 + j, (*chip, c), me).wait_recv()
            passed[j].start()
        copy(0, sibling, me).wait_recv()
        for j, chip in enumerate(chips):
            copy(4 + j, (*chip, 1 - c), me).wait_recv()
        for cp in first + passed:
            cp.wait_send()
        mine.wait()

    return pl.pallas_call(
        body, name="all_gather",
        out_shape=jax.ShapeDtypeStruct((N_DEV, rows, cols), block.dtype),
        in_specs=[HBM], out_specs=HBM,
        scratch_shapes=[pltpu.SemaphoreType.DMA((7,)), pltpu.SemaphoreType.DMA((7,)), pltpu.SemaphoreType.DMA],
    )(block)


SEM = pl.BlockSpec(memory_space=pltpu.SEMAPHORE)
ANY = pl.BlockSpec(memory_space=pl.ANY)
SPLIT_EFFECT = pltpu.SideEffectType.DATAFLOW_SIDE_EFFECTING
N_PEER = N_DEV - 1


def _flow_copies(kind, src_ref, land_ref, send_sems, recv_sems):
    x, y, c = _place()
    copies = []
    for r in range(1, N_DEV):
        px = 1 - x if r & 4 else x
        py = 1 - y if r & 2 else y
        pc = 1 - c if r & 1 else c
        if kind == "gather":
            src, dst = src_ref, land_ref.at[4 * x + 2 * y + c]
        else:
            src, dst = src_ref.at[4 * px + 2 * py + pc], land_ref.at[r - 1]
        copies.append(pltpu.make_async_remote_copy(
            src_ref=src, dst_ref=dst, send_sem=send_sems.at[r - 1], recv_sem=recv_sems.at[r - 1],
            device_id=(px, py, pc), device_id_type=MESH))
    return copies


def _landing(kind, src):
    if kind == "gather":
        return lax.empty((N_DEV,) + src.shape, src.dtype)
    return lax.empty((N_PEER,) + src.shape[1:], src.dtype)


def _push_start(kinds, arrays, after, name):
    n = len(arrays)

    def body(*refs):
        ins, sems, token = refs[:n], refs[n + 1:2 * n + 1], refs[3 * n + 1]
        for f, kind in enumerate(kinds):
            for cp in _flow_copies(kind, ins[2 * f], ins[2 * f + 1], sems[2 * f], sems[2 * f + 1]):
                cp.start()
        token[...] = jnp.zeros_like(token)

    outs = pl.pallas_call(
        body, name=name,
        out_shape=[pltpu.SemaphoreType.DMA((N_PEER,))] * n + [pltpu.HBM(a.shape, a.dtype) for a in arrays]
        + [jax.ShapeDtypeStruct((8, 128), F32)],
        in_specs=[HBM] * n + [ANY], out_specs=[SEM] * n + [HBM] * n + [pl.BlockSpec(memory_space=pltpu.VMEM)],
        input_output_aliases={i: n + i for i in range(n)},
        compiler_params=pltpu.CompilerParams(has_side_effects=SPLIT_EFFECT),
    )(*[pltpu.with_memory_space_constraint(a, pltpu.HBM) for a in arrays], after)
    return outs[:n], outs[n:2 * n], outs[2 * n]


def _push_wait(kinds, sems, arrays, after, name):
    n = len(arrays)

    def body(*refs):
        ins, sem_refs = refs[:n], refs[n:2 * n]
        for f, kind in enumerate(kinds):
            for cp in _flow_copies(kind, ins[2 * f], ins[2 * f + 1], sem_refs[2 * f], sem_refs[2 * f + 1]):
                cp.wait_send()
                cp.wait_recv()

    outs = pl.pallas_call(
        body, name=name,
        out_shape=[pltpu.HBM(a.shape, a.dtype) for a in arrays],
        in_specs=[HBM] * n + [SEM] * n + [ANY], out_specs=[HBM] * n,
        input_output_aliases={i: i for i in range(n)},
        compiler_params=pltpu.CompilerParams(has_side_effects=SPLIT_EFFECT),
    )(*arrays, *sems, after)
    return outs


def _sum_blocks(own, got, name):
    n, rows, cols = got.shape
    tr = next(t for t in (128, 64, 32) if rows % t == 0)

    def body(own_ref, got_ref, out_ref):
        acc = own_ref[...].astype(F32)
        for r in range(n):
            acc = acc + got_ref[r].astype(F32)
        out_ref[...] = acc

    return pl.pallas_call(
        body, grid=(rows // tr,), name=name,
        in_specs=[pl.BlockSpec((tr, cols), lambda i: (i, 0)), pl.BlockSpec((n, tr, cols), lambda i: (0, i, 0))],
        out_specs=pl.BlockSpec((tr, cols), lambda i: (i, 0)),
        out_shape=jax.ShapeDtypeStruct((rows, cols), F32),
        compiler_params=_params(("parallel",)),
    )(own, got)


def _adamw_math(w, g, m, v):
    m = ADAM_B1 * m + (1.0 - ADAM_B1) * g
    v = ADAM_B2 * v + (1.0 - ADAM_B2) * (g * g)
    m_hat = m / (1.0 - ADAM_B1 ** ADAM_STEP)
    v_hat = v / (1.0 - ADAM_B2 ** ADAM_STEP)
    delta = -ADAM_LR * (m_hat / (jnp.sqrt(v_hat) + ADAM_EPS) + ADAM_WD * w)
    return delta, m, v


def _adamw(w, g, m, v, name):
    rows, cols = w.shape
    tr = rows if rows <= 352 else 256

    def body(w_ref, g_ref, m_ref, v_ref, d_out, m_out, v_out):
        d_out[...], m_out[...], v_out[...] = _adamw_math(w_ref[...], g_ref[...], m_ref[...], v_ref[...])

    spec = pl.BlockSpec((tr, cols), lambda i: (i, 0))
    return pl.pallas_call(
        body, grid=(rows // tr,), name=name,
        in_specs=[spec] * 4, out_specs=[spec] * 3,
        out_shape=[jax.ShapeDtypeStruct((rows, cols), F32)] * 3,
        compiler_params=_params(("parallel",)),
    )(w, g, m, v)


ROW_ATTN_PRE, ROW_ATTN_POST, ROW_FFN_PRE, ROW_FFN_POST, ROW_OUT_NORMS, ROW_GAMMA, ROW_LOSS = range(7)


def _small_update(smalls, w, m, v):
    def body(s_ref, w_ref, m_ref, v_ref, g_out, d_out, m_out, v_out):
        g = s_ref[0]
        for j in range(1, N_DEV):
            g = g + s_ref[j]
        wv = w_ref[...]
        g0 = wv[ROW_GAMMA:ROW_GAMMA + 1, 0:HG_WIDTH]
        g1 = wv[ROW_GAMMA:ROW_GAMMA + 1, HG_WIDTH:]
        mx = jnp.maximum(g0, g1)
        e0 = jnp.exp(g0 - mx)
        e1 = jnp.exp(g1 - mx)
        lb = e0 / (e0 + e1)
        dg0 = g[ROW_GAMMA:ROW_GAMMA + 1, 0:HG_WIDTH] * lb * (1.0 - lb)
        row = lax.broadcasted_iota(jnp.int32, g.shape, 0)
        g = jnp.where(row == ROW_GAMMA, jnp.concatenate([dg0, -dg0], axis=1), g)
        g_out[...] = g
        d_out[...], m_out[...], v_out[...] = _adamw_math(wv, g, m_ref[...], v_ref[...])

    return pl.pallas_call(
        body, name="small_update",
        out_shape=[jax.ShapeDtypeStruct((8, D_MODEL), F32)] * 4,
    )(smalls, w, m, v)


def _local_step(x, p, target, g_attn_pre, w_in_t, gamma, g_hg, g_sb, g_attn_post, g_ffn_pre, g_ffn_post, later_weights,
                grads_ready):
    proj, u, qkv = _in_proj(x, g_attn_pre, w_in_t)
    o_hg, hist = _hg_fwd(proj, gamma)
    o_sb, a_st, s_st = _sb_fwd(qkv)
    w_out, w_gu_t, w_down, w_pp_t, w_pg = later_weights(o_sb)
    cat, mix, h1 = _out_proj(o_hg, proj, o_sb, x, g_hg, g_sb, g_attn_post, w_out)
    u2, gu, y, h2 = _ffn_fwd(h1, g_ffn_pre, w_gu_t, w_down, g_ffn_post)
    dh2, de, dg, loss = _ple_loss(h2, p, target, w_pp_t, w_pg)

    d_wpp_t = _wgrad(de, p, "wgrad_ple_proj")
    d_wpg = _wgrad(h2, dg, "wgrad_ple_gate")
    dy, a, dgu, dh1, dg_ffn_post, dg_ffn_pre = _ffn_bwd(dh2, y, h1, gu, g_ffn_post, g_ffn_pre, w_gu_t, w_down)
    d_wdown = _wgrad(a, dy, "wgrad_down")
    sent = grads_ready("ffn", (_wgrad(dgu, u2, "wgrad_gate_up"), d_wdown, d_wpp_t, d_wpg))
    dmix, d_ohg, d_hg, d_osb, dg_attn_post, dg_hg, dg_sb = _out_bwd(dh1, mix, o_hg, proj, o_sb, g_hg, g_sb, g_attn_post,
                                                                    w_out, sent)
    sent = grads_ready("out", (_wgrad(cat, dmix, "wgrad_out"),))
    dq, dk, dv = _sb_bwd(qkv, a_st, s_st, d_osb, sent)
    d_hgrn, d_lb = _hg_bwd(proj, gamma, d_ohg, hist)
    dproj, dx, dg_attn_pre = _in_bwd(d_hgrn, d_hg, dq, dk, dv, x, dh1, g_attn_pre, w_in_t)
    grads_ready("in", (_wgrad(dproj, u, "wgrad_in"),))
    return loss, dx, (dg_attn_pre, dg_attn_post, dg_ffn_pre, dg_ffn_post, dg_hg, dg_sb, d_lb)


def _pack_row(*parts):
    return jnp.concatenate([q.reshape(1, -1) for q in parts], axis=1)


def kernel(x, p, attn_pre_norm, w_in, hg_lower_gamma, hg_out_norm, sb_out_norm, w_out, attn_post_norm, ffn_pre_norm, w_gate_up, w_down, ffn_post_norm, ple_proj, ple_gate, loss_target, m_attn_pre_norm, m_w_in, m_hg_lower_gamma, m_hg_out_norm, m_sb_out_norm, m_w_out, m_attn_post_norm, m_ffn_pre_norm, m_w_gate_up, m_w_down, m_ffn_post_norm, m_ple_proj, m_ple_gate, v_attn_pre_norm, v_w_in, v_hg_lower_gamma, v_hg_out_norm, v_sb_out_norm, v_w_out, v_attn_post_norm, v_ffn_pre_norm, v_w_gate_up, v_w_down, v_ffn_post_norm, v_ple_proj, v_ple_gate):
    big = (w_in[0], w_out[0], w_gate_up[0], w_down[0], ple_proj[0], ple_gate[0])
    big_m = (m_w_in[0], m_w_out[0], m_w_gate_up[0], m_w_down[0], m_ple_proj[0], m_ple_gate[0])
    big_v = (v_w_in[0], v_w_out[0], v_w_gate_up[0], v_w_down[0], v_ple_proj[0], v_ple_gate[0])
    names = ("w_in", "w_out", "w_gate_up", "w_down", "ple_proj", "ple_gate")

    by_columns = (True, False, True, False, True, False)
    shards = [(w.T if t else w).astype(BF16) for w, t in zip(big, by_columns)]
    w_in_t = _all_gather(shards[0]).reshape(IN_COLS, D_MODEL)
    me = 4 * lax.axis_index("x") + 2 * lax.axis_index("y") + lax.axis_index("c")
    later = shards[1:]
    w_sems, w_thru, token = _push_start(
        ("gather",) * len(later), [a for s in later for a in (s, _landing("gather", s))], w_in_t, "weights_start")

    def later_weights(after):
        done = _push_wait(("gather",) * len(later), w_sems, w_thru, after, "weights_wait")
        whole = [lax.dynamic_update_index_in_dim(got, mine, me, 0) for mine, got in zip(done[0::2], done[1::2])]
        return (whole[0].reshape(D_MODEL, D_MODEL), whole[1].reshape(2, D_FF, D_MODEL), whole[2].reshape(D_FF, D_MODEL),
                whole[3].reshape(D_MODEL, PLE_DIM), whole[4].reshape(D_MODEL, D_MODEL))

    flights = {}

    def by_owner(g):
        return g.reshape(N_DEV, -1, g.shape[-1])

    def grads_ready(group, grads):
        blocks = [by_owner(g) for g in grads]
        if group == "in":
            flights[group] = blocks[0]
            return None
        sems, thru, sent = _push_start(
            ("scatter",) * len(blocks), [a for b in blocks for a in (b, _landing("scatter", b))], token,
            "grads_" + group + "_start")
        flights[group] = (sems, thru)
        return sent

    loss, dx, smalls = _local_step(
        x[0], p[0, 0], loss_target[0], attn_pre_norm + token[0:1, 0:1], w_in_t,
        hg_lower_gamma, hg_out_norm, sb_out_norm, attn_post_norm, ffn_pre_norm, ffn_post_norm, later_weights, grads_ready)
    dg_attn_pre, dg_attn_post, dg_ffn_pre, dg_ffn_post, dg_hg, dg_sb, d_lb = smalls

    zeros_half = jnp.zeros((1, HG_WIDTH), F32)
    small_pack = jnp.concatenate([
        dg_attn_pre, dg_attn_post, dg_ffn_pre, dg_ffn_post, _pack_row(dg_hg, dg_sb), _pack_row(d_lb, zeros_half),
        jnp.broadcast_to(loss[0:1, 0:1], (1, D_MODEL)), jnp.zeros((1, D_MODEL), F32)], axis=0)
    in_blocks = flights["in"]
    in_sems, in_thru, in_token = _push_start(
        ("scatter", "gather"),
        (in_blocks, _landing("scatter", in_blocks), small_pack, _landing("gather", small_pack)), token,
        "grads_in_start")
    ffn = _push_wait(("scatter",) * 4, *flights["ffn"], in_token, "grads_ffn_wait")
    sent_out, got_out = _push_wait(("scatter",), *flights["out"], ffn[1], "grads_out_wait")
    sent_in, got_in, my_small, all_smalls = _push_wait(("scatter", "gather"), in_sems, in_thru, got_out, "grads_in_wait")
    all_smalls = lax.dynamic_update_index_in_dim(all_smalls, my_small, me, 0)

    def reduced(sent, got, name):
        return _sum_blocks(lax.dynamic_index_in_dim(sent, me, 0, keepdims=False), got, "grad_sum_" + name)

    sums = (reduced(sent_in, got_in, "w_in"), reduced(sent_out, got_out, "w_out"), reduced(ffn[0], ffn[1], "w_gate_up"),
            reduced(ffn[2], ffn[3], "w_down"), reduced(ffn[4], ffn[5], "ple_proj"), reduced(ffn[6], ffn[7], "ple_gate"))

    out_g, out_d, out_m, out_v = {}, {}, {}, {}
    for i, name in enumerate(names):
        g = sums[i].T if by_columns[i] else sums[i]
        out_g[name] = g[None]
        d, m, v = _adamw(big[i], g, big_m[i], big_v[i], "adamw_" + name)
        out_d[name], out_m[name], out_v[name] = d[None], m[None], v[None]

    def small_rows(pre, gam, hg, sb, post, fpre, fpost):
        return jnp.concatenate([pre, post, fpre, fpost, _pack_row(hg, sb), _pack_row(gam[0], gam[1]),
                                jnp.zeros((2, D_MODEL), F32)], axis=0)

    packs = _small_update(
        all_smalls,
        small_rows(attn_pre_norm, hg_lower_gamma, hg_out_norm, sb_out_norm, attn_post_norm, ffn_pre_norm, ffn_post_norm),
        small_rows(m_attn_pre_norm, m_hg_lower_gamma, m_hg_out_norm, m_sb_out_norm, m_attn_post_norm, m_ffn_pre_norm,
                   m_ffn_post_norm),
        small_rows(v_attn_pre_norm, v_hg_lower_gamma, v_hg_out_norm, v_sb_out_norm, v_attn_post_norm, v_ffn_pre_norm,
                   v_ffn_post_norm))

    def unpack(pk):
        return {
            "attn_pre_norm": pk[ROW_ATTN_PRE:ROW_ATTN_PRE + 1],
            "hg_lower_gamma": pk[ROW_GAMMA].reshape(2, HG_WIDTH),
            "hg_out_norm": pk[ROW_OUT_NORMS:ROW_OUT_NORMS + 1, :HG_WIDTH],
            "sb_out_norm": pk[ROW_OUT_NORMS:ROW_OUT_NORMS + 1, HG_WIDTH:],
            "attn_post_norm": pk[ROW_ATTN_POST:ROW_ATTN_POST + 1],
            "ffn_pre_norm": pk[ROW_FFN_PRE:ROW_FFN_PRE + 1],
            "ffn_post_norm": pk[ROW_FFN_POST:ROW_FFN_POST + 1],
        }

    sg, sd, sm, sv = (unpack(pk) for pk in packs)
    out_g.update(sg), out_d.update(sd), out_m.update(sm), out_v.update(sv)
    order = ("attn_pre_norm", "w_in", "hg_lower_gamma", "hg_out_norm", "sb_out_norm", "w_out", "attn_post_norm",
             "ffn_pre_norm", "w_gate_up", "w_down", "ffn_post_norm", "ple_proj", "ple_gate")
    total_loss = packs[0][ROW_LOSS, 0]
    return (total_loss, dx[None], *[out_g[n] for n in order], *[out_d[n] for n in order],
            *[out_m[n] for n in order], *[out_v[n] for n in order])
```
